```python
import math
import jax
import jax.numpy as jnp
from jax import lax
import numpy as np

D_MODEL = 1024
BATCH = 4
SEQ = 4096
DEPTH = 2
DEC_BATCH = 32
DEC_SEQ = 8
PAST_LEN = 8192
PAGE_SIZE = 128

HEAD_DIM = 64
GLA_WIDTH = D_MODEL // 4
NSA_WIDTH = D_MODEL // 2
ML_WIDTH = D_MODEL - GLA_WIDTH - NSA_WIDTH
D_MIX = GLA_WIDTH + NSA_WIDTH + ML_WIDTH

GLA_HEADS = GLA_WIDTH // HEAD_DIM
GLA_DK = HEAD_DIM // 2
GLA_DV = HEAD_DIM
GLA_RANK = 16
GLA_TAU = 16.0
GLA_CHUNK = 64

NSA_HEADS = NSA_WIDTH // HEAD_DIM
NSA_KV_HEADS = 2
NSA_HPG = NSA_HEADS // NSA_KV_HEADS
CMP_BLOCK = 32
CMP_STRIDE = 16
SEL_BLOCK = 64
N_SELECT = 16
WINDOW = 512
Q_BLOCK = 128
N_KV_SLOTS = 4
ROT_DIM = HEAD_DIM // 4
ROPE_THETA = 500000.0
ATTN_SCALE = HEAD_DIM ** -0.5

ML_HEADS = ML_WIDTH // HEAD_DIM
ML_DH = HEAD_DIM
ML_CHUNK = 64
CONV_W = 4

SPLIT_SIZES = (GLA_HEADS * GLA_DK, GLA_HEADS * GLA_DK, GLA_WIDTH, GLA_RANK, GLA_WIDTH,
               NSA_WIDTH, 6 * NSA_KV_HEADS * HEAD_DIM, 3 * NSA_HEADS, NSA_WIDTH,
               2 * ML_WIDTH, ML_WIDTH, 2 * ML_HEADS, ML_WIDTH, ML_WIDTH)
D_IN = sum(SPLIT_SIZES)

kernel_name = 'hybrid_gla_nsa_mlstm_step'


def _rms_norm(x, g, eps=1e-6):
    xf = x.astype(jnp.float32)
    y = xf * lax.rsqrt(jnp.mean(xf * xf, axis=-1, keepdims=True) + eps)
    return (y * g.astype(jnp.float32)).astype(x.dtype)


def _rope(x, pos):
    half = ROT_DIM // 2
    inv = jnp.exp(-math.log(ROPE_THETA) * jnp.arange(half, dtype=jnp.float32) * 2.0 / ROT_DIM)
    ang = pos.astype(jnp.float32)[:, None] * inv[None, :]
    cos = jnp.cos(ang)[:, None, :]
    sin = jnp.sin(ang)[:, None, :]
    xf = x.astype(jnp.float32)
    x1, x2, rest = xf[..., :half], xf[..., half:ROT_DIM], xf[..., ROT_DIM:]
    return jnp.concatenate([x1 * cos - x2 * sin, x2 * cos + x1 * sin, rest], axis=-1).astype(x.dtype)


def _masked_softmax(s, mask):
    s = jnp.where(mask, s, -jnp.inf)
    m = jnp.max(s, axis=-1, keepdims=True)
    m = jnp.where(jnp.isfinite(m), m, 0.0)
    e = jnp.where(mask, jnp.exp(s - m), 0.0)
    return e / jnp.maximum(jnp.sum(e, axis=-1, keepdims=True), 1e-30)


def _to_chunks(x, c):
    b, l = x.shape[:2]
    x = x.reshape((b, l // c, c) + x.shape[2:])
    x = jnp.moveaxis(x, 1, 0)
    return jnp.moveaxis(x, 2, 3)


def _from_chunks(x):
    n, b, h, c, d = x.shape
    return jnp.transpose(x, (1, 0, 3, 2, 4)).reshape(b, n * c, h, d)


def _gla_chunked(q, k, v, log_a, s0):
    l = q.shape[1]
    c = math.gcd(l, GLA_CHUNK)
    qc, kc, vc, gc = (_to_chunks(t.astype(jnp.float32), c) for t in (q, k, v, log_a))
    tri = jnp.tril(jnp.ones((c, c), dtype=bool))

    def step(s, inp):
        qi, ki, vi, gi = inp
        bcum = jnp.cumsum(gi, axis=2)
        diff = bcum[:, :, :, None, :] - bcum[:, :, None, :, :]
        dec = jnp.exp(jnp.where(tri[:, :, None], diff, -jnp.inf))
        a = jnp.einsum('bhid,bhjd,bhijd->bhij', qi, ki, dec)
        o = (jnp.einsum('bhid,bhde->bhie', qi * jnp.exp(bcum), s)
             + jnp.einsum('bhij,bhje->bhie', a, vi))
        blast = bcum[:, :, -1]
        s = (jnp.exp(blast)[..., None] * s
             + jnp.einsum('bhjd,bhje->bhde', ki * jnp.exp(blast[:, :, None] - bcum), vi))
        return s, o

    s, o = lax.scan(step, s0.astype(jnp.float32), (qc, kc, vc, gc))
    return _from_chunks(o), s


def _mlstm_chunked(q, k, v, i_pre, logf, c0, n0, m0):
    l = q.shape[1]
    c = math.gcd(l, ML_CHUNK)
    qc, kc, vc = (_to_chunks(t.astype(jnp.float32), c) for t in (q, k, v))
    ic, fc = (_to_chunks(t.astype(jnp.float32), c) for t in (i_pre, logf))
    tri = jnp.tril(jnp.ones((c, c), dtype=bool))

    def step(carry, inp):
        cs, ns, ms = carry
        qi, ki, vi, ii, fi = inp
        fcum = jnp.cumsum(fi, axis=-1)
        dmat = jnp.where(tri, fcum[..., :, None] - fcum[..., None, :] + ii[..., None, :], -jnp.inf)
        inter = fcum + ms[..., None]
        m = jnp.maximum(inter, jnp.max(dmat, axis=-1))
        w_int = jnp.exp(inter - m)
        sij = jnp.einsum('bhid,bhjd->bhij', qi, ki) * jnp.exp(dmat - m[..., None])
        num = (w_int[..., None] * jnp.einsum('bhed,bhid->bhie', cs, qi)
               + jnp.einsum('bhij,bhje->bhie', sij, vi))
        den = w_int * jnp.einsum('bhd,bhid->bhi', ns, qi) + jnp.sum(sij, axis=-1)
        h = num / jnp.maximum(jnp.abs(den), jnp.exp(-m))[..., None]
        m_last = m[..., -1]
        f_last = fcum[..., -1]
        decay = jnp.exp(f_last + ms - m_last)
        wj = jnp.exp(f_last[..., None] - fcum + ii - m_last[..., None])
        cs = decay[..., None, None] * cs + jnp.einsum('bhj,bhje,bhjd->bhed', wj, vi, ki)
        ns = decay[..., None] * ns + jnp.einsum('bhj,bhjd->bhd', wj, ki)
        return (cs, ns, m_last), h

    init = (c0.astype(jnp.float32), n0.astype(jnp.float32), m0.astype(jnp.float32))
    (cs, ns, ms), h = lax.scan(step, init, (qc, kc, vc, ic, fc))
    return _from_chunks(h), cs, ns, ms


def _cmp_summaries(x, pe, w):
    b, t, g, d = x.shape
    n_half = t // CMP_STRIDE
    halves = x[:, :n_half * CMP_STRIDE].astype(jnp.float32).reshape(b, n_half, CMP_STRIDE, g, d)
    wf = w.astype(jnp.float32)
    first = jnp.einsum('bnsgd,sde->bnge', halves[:, :-1], wf[:CMP_STRIDE])
    second = jnp.einsum('bnsgd,sde->bnge', halves[:, 1:], wf[CMP_STRIDE:])
    bias = jnp.einsum('sd,sde->e', pe.astype(jnp.float32), wf)
    return first + second + bias


def _overlap_matrix(n_cmp, n_sel):
    a = SEL_BLOCK // CMP_STRIDE
    bb = CMP_BLOCK // CMP_STRIDE
    i = jnp.arange(n_cmp)[:, None]
    j = jnp.arange(n_sel)[None, :]
    s = i - a * j + (bb - 1)
    cnt = jnp.minimum(jnp.minimum(s + 1, a + bb - 1 - s), min(a, bb))
    return jnp.maximum(cnt, 0).astype(jnp.float32)


def _nsa_selected(q, k_hist, v_hist, idx, pos):
    b, l = q.shape[:2]
    t_len = k_hist.shape[1]
    n_sel = -(-t_len // SEL_BLOCK)
    n_k = idx.shape[-1]
    pad = n_sel * SEL_BLOCK - t_len

    def blocks(x):
        x = jnp.pad(x.astype(jnp.float32), ((0, 0), (0, pad), (0, 0), (0, 0)))
        return x.reshape(b, n_sel, SEL_BLOCK, NSA_KV_HEADS, HEAD_DIM).transpose(0, 3, 1, 2, 4)

    kb, vb = blocks(k_hist), blocks(v_hist)
    qb = math.gcd(l, Q_BLOCK)
    nq = l // qb
    qx = q.astype(jnp.float32).reshape(b * nq, qb, NSA_KV_HEADS, NSA_HPG, HEAD_DIM)
    ix = idx.reshape(b, NSA_KV_HEADS, nq, qb, n_k).transpose(0, 2, 1, 3, 4).reshape(b * nq, NSA_KV_HEADS, qb, n_k)
    px = jnp.tile(pos.reshape(nq, qb), (b, 1))
    bx = jnp.repeat(jnp.arange(b), nq)
    g_ix = jnp.arange(NSA_KV_HEADS)[:, None, None]
    offs = jnp.arange(SEL_BLOCK)
    n_keys = n_k * SEL_BLOCK

    def attend(args):
        qi, ii, pi, bi = args
        ks = kb[bi][g_ix, ii]
        vs = vb[bi][g_ix, ii]
        kpos = ii[..., None] * SEL_BLOCK + offs
        mask = (kpos <= pi[None, :, None, None]).reshape(NSA_KV_HEADS, 1, qb, n_keys)
        s = jnp.einsum('qghd,gqkjd->ghqkj', qi, ks).reshape(NSA_KV_HEADS, NSA_HPG, qb, n_keys) * ATTN_SCALE
        p = _masked_softmax(s, mask)
        return jnp.einsum('ghqn,gqnd->qghd', p, vs.reshape(NSA_KV_HEADS, qb, n_keys, HEAD_DIM))

    o = lax.map(attend, (qx, ix, px, bx))
    return o.reshape(b, l, NSA_HEADS, HEAD_DIM)


def _nsa_window(q, k_ctx, v_ctx, pos0):
    b, l = q.shape[:2]
    lc = k_ctx.shape[1]
    padw = ((0, 0), (WINDOW, 0), (0, 0), (0, 0))
    kp = jnp.pad(k_ctx.astype(jnp.float32), padw)
    vp = jnp.pad(v_ctx.astype(jnp.float32), padw)
    qb = math.gcd(l, Q_BLOCK)
    nq = l // qb
    span = WINDOW + qb
    qx = jnp.moveaxis(q.astype(jnp.float32).reshape(b, nq, qb, NSA_KV_HEADS, NSA_HPG, HEAD_DIM), 1, 0)
    starts = jnp.arange(nq) * qb
    first_pos = pos0 + l - lc

    def attend(args):
        qi, i0 = args
        kk = lax.dynamic_slice_in_dim(kp, i0 + lc - l, span, axis=1)
        vv = lax.dynamic_slice_in_dim(vp, i0 + lc - l, span, axis=1)
        kpos = pos0 + i0 - WINDOW + jnp.arange(span)
        qpos = pos0 + i0 + jnp.arange(qb)
        mask = ((kpos[None, :] >= first_pos) & (kpos[None, :] <= qpos[:, None])
                & (kpos[None, :] > qpos[:, None] - WINDOW))
        s = jnp.einsum('bqghd,bkgd->bghqk', qi, kk) * ATTN_SCALE
        p = _masked_softmax(s, mask)
        return jnp.einsum('bghqk,bkgd->bqghd', p, vv)

    o = lax.map(attend, (qx, starts))
    return jnp.moveaxis(o, 0, 1).reshape(b, l, NSA_HEADS, HEAD_DIM)


def _nsa_mixer(n_q, n_kv, n_g, past_kv, win_buf, win_keep, pos0, q_g, k_g, cmp_pos, cmp_w):
    b, l, _ = n_q.shape
    pos = pos0 + jnp.arange(l)
    q = _rms_norm(n_q.reshape(b, l, NSA_HEADS, HEAD_DIM), q_g)
    q_rot = _rope(q, pos)
    kv = n_kv.reshape(b, l, 6, NSA_KV_HEADS, HEAD_DIM)
    k_slc = _rope(_rms_norm(kv[:, :, 2], k_g[1]), pos)
    k_win = _rope(_rms_norm(kv[:, :, 4], k_g[2]), pos)
    new_rows = jnp.stack([kv[:, :, 0], kv[:, :, 1], k_slc, kv[:, :, 3]], axis=2)
    new_win = jnp.stack([k_win, kv[:, :, 5]], axis=2)
    hist = jnp.concatenate([past_kv.astype(new_rows.dtype), new_rows], axis=1)
    ctx = jnp.concatenate([win_buf.astype(new_win.dtype), new_win], axis=1)
    t_len = hist.shape[1]

    kc = _rms_norm(_cmp_summaries(hist[:, :, 0], cmp_pos[0], cmp_w[0]), k_g[0])
    vc = _cmp_summaries(hist[:, :, 1], cmp_pos[1], cmp_w[1])
    n_cmp = kc.shape[1]
    qg = q.astype(jnp.float32).reshape(b, l, NSA_KV_HEADS, NSA_HPG, HEAD_DIM)
    s = jnp.einsum('blghd,bngd->bghln', qg, kc) * ATTN_SCALE
    cmp_end = jnp.arange(n_cmp) * CMP_STRIDE + CMP_BLOCK - 1
    p_cmp = _masked_softmax(s, cmp_end[None, :] <= pos[:, None])
    o_cmp = jnp.einsum('bghln,bngd->blghd', p_cmp, vc).reshape(b, l, NSA_HEADS, HEAD_DIM)

    n_sel = -(-t_len // SEL_BLOCK)
    imp = jnp.einsum('bghln,nj->bglj', p_cmp, _overlap_matrix(n_cmp, n_sel))
    blk = jnp.arange(n_sel)[None, :]
    cur = (pos // SEL_BLOCK)[:, None]
    valid = blk * SEL_BLOCK <= pos[:, None]
    forced = (blk == 0) | (blk == cur) | (blk == cur - 1)
    score = jnp.where(forced, jnp.inf, jnp.where(valid, imp, -jnp.inf))
    _, idx = lax.top_k(score, min(N_SELECT, n_sel))
    o_slc = _nsa_selected(q_rot, hist[:, :, 2], hist[:, :, 3], idx, pos)

    o_win = _nsa_window(q_rot, ctx[:, :, 0], ctx[:, :, 1], pos0)

    gate = jax.nn.sigmoid(n_g.astype(jnp.float32)).reshape(b, l, 3, NSA_HEADS, 1)
    o = gate[:, :, 0] * o_cmp + gate[:, :, 1] * o_slc + gate[:, :, 2] * o_win
    return o.reshape(b, l, NSA_WIDTH), new_rows, ctx[:, ctx.shape[1] - win_keep:]


def _hybrid_layer(x, pos0, past_kv, win_buf, win_keep, s_gla, c_ml, n_ml, m_ml, conv_ml,
                  norm_g, w_in, w_out, gla_w_gate, gla_b_gate, gla_norm_g,
                  nsa_q_norm_g, nsa_k_norm_g, nsa_cmp_pos, nsa_cmp_w,
                  ml_conv_w, ml_conv_b, ml_gate_b, ml_norm_g):
    b, l, _ = x.shape
    u = _rms_norm(x, norm_g) @ w_in
    points = np.cumsum(SPLIT_SIZES)[:-1].tolist()
    (g_q, g_k, g_v, g_a, g_z, n_q, n_kv, n_g, n_z,
     m_qk, m_v, m_if, m_o, m_z) = jnp.split(u, points, axis=-1)

    gq = g_q.reshape(b, l, GLA_HEADS, GLA_DK) * GLA_DK ** -0.5
    gk = g_k.reshape(b, l, GLA_HEADS, GLA_DK)
    gv = g_v.reshape(b, l, GLA_HEADS, GLA_DV)
    log_a = jax.nn.log_sigmoid((g_a @ gla_w_gate + gla_b_gate).astype(jnp.float32)).reshape(b, l, GLA_HEADS, GLA_DK) / GLA_TAU
    o_a, s_new = _gla_chunked(gq, gk, gv, log_a, s_gla)
    o_a = _rms_norm(o_a, gla_norm_g).reshape(b, l, GLA_WIDTH) * jax.nn.silu(g_z.astype(jnp.float32))

    o_b, new_rows, new_win = _nsa_mixer(n_q, n_kv, n_g, past_kv, win_buf, win_keep, pos0,
                                        nsa_q_norm_g, nsa_k_norm_g, nsa_cmp_pos, nsa_cmp_w)
    o_b = o_b * jax.nn.silu(n_z.astype(jnp.float32))

    xp = jnp.concatenate([conv_ml.astype(m_qk.dtype), m_qk], axis=1)
    conv = sum(xp[:, w:w + l] * ml_conv_w[w] for w in range(CONV_W)) + ml_conv_b
    qk = jax.nn.silu(conv)
    mq = qk[..., :ML_WIDTH].reshape(b, l, ML_HEADS, ML_DH)
    mk = qk[..., ML_WIDTH:].reshape(b, l, ML_HEADS, ML_DH) * ML_DH ** -0.5
    mv = m_v.reshape(b, l, ML_HEADS, ML_DH)
    gates = m_if.astype(jnp.float32).reshape(b, l, 2, ML_HEADS) + ml_gate_b.astype(jnp.float32)
    h, c_new, n_new, m_new = _mlstm_chunked(mq, mk, mv, gates[:, :, 0], jax.nn.log_sigmoid(gates[:, :, 1]),
                                            c_ml, n_ml, m_ml)
    o_gate = jax.nn.sigmoid(m_o.astype(jnp.float32)).reshape(b, l, ML_HEADS, ML_DH)
    o_c = (_rms_norm(h, ml_norm_g) * o_gate).reshape(b, l, ML_WIDTH) * jax.nn.silu(m_z.astype(jnp.float32))
    conv_new = xp[:, l:]

    mixed = jnp.concatenate([o_a, o_b, o_c], axis=-1).astype(x.dtype)
    y = x + mixed @ w_out
    return y, new_rows, new_win, s_new, c_new, n_new, m_new, conv_new


def setup_inputs(seed: int = 0) -> dict:
    key = jax.random.key(seed)
    ks = jax.random.split(key, 24)
    f32 = jnp.float32
    nrm = jax.random.normal
    n_pages = PAST_LEN // PAGE_SIZE
    n_used = DEC_BATCH * n_pages
    n_pool = n_used + max(1, n_used // 4)
    win_buf = min(WINDOW, PAST_LEN)
    page_table = jax.random.permutation(ks[9], n_pool)[:n_used].reshape(DEC_BATCH, n_pages).astype(jnp.int32)
    f_bias = jnp.stack([jnp.zeros((ML_HEADS,), f32), jnp.linspace(3.0, 6.0, ML_HEADS, dtype=f32)])
    return {
        'x_prompt': nrm(ks[0], (BATCH, SEQ, D_MODEL), f32),
        'x_sample': nrm(ks[1], (DEC_BATCH, DEC_SEQ, D_MODEL), f32),
        'cache_nsa_kv': nrm(ks[2], (DEPTH, n_pool, PAGE_SIZE, N_KV_SLOTS, NSA_KV_HEADS, HEAD_DIM), f32),
        'state_nsa_win': nrm(ks[3], (DEPTH, DEC_BATCH, win_buf, 2, NSA_KV_HEADS, HEAD_DIM), f32),
        'state_gla': 0.3 * nrm(ks[4], (DEPTH, DEC_BATCH, GLA_HEADS, GLA_DK, GLA_DV), f32),
        'state_mlstm_C': 0.3 * nrm(ks[5], (DEPTH, DEC_BATCH, ML_HEADS, ML_DH, ML_DH), f32),
        'state_mlstm_n': 0.3 * nrm(ks[6], (DEPTH, DEC_BATCH, ML_HEADS, ML_DH), f32),
        'state_mlstm_m': nrm(ks[7], (DEPTH, DEC_BATCH, ML_HEADS), f32),
        'state_mlstm_conv': nrm(ks[8], (DEPTH, DEC_BATCH, CONV_W - 1, 2 * ML_WIDTH), f32),
        'page_table': page_table,
        'norm_g': 1.0 + 0.02 * nrm(ks[10], (DEPTH, D_MODEL), f32),
        'w_in': nrm(ks[11], (DEPTH, D_MODEL, D_IN), f32) * D_MODEL ** -0.5,
        'w_out': nrm(ks[12], (DEPTH, D_MIX, D_MODEL), f32) * D_MIX ** -0.5,
        'gla_w_gate': nrm(ks[13], (DEPTH, GLA_RANK, GLA_HEADS * GLA_DK), f32) * GLA_RANK ** -0.5,
        'gla_b_gate': 0.1 * nrm(ks[14], (DEPTH, GLA_HEADS * GLA_DK), f32),
        'gla_norm_g': 1.0 + 0.02 * nrm(ks[15], (DEPTH, GLA_DV), f32),
        'nsa_q_norm_g': 1.0 + 0.02 * nrm(ks[16], (DEPTH, HEAD_DIM), f32),
        'nsa_k_norm_g': 1.0 + 0.02 * nrm(ks[17], (DEPTH, 3, HEAD_DIM), f32),
        'nsa_cmp_pos': 0.1 * nrm(ks[18], (DEPTH, 2, CMP_BLOCK, HEAD_DIM), f32),
        'nsa_cmp_w': nrm(ks[19], (DEPTH, 2, CMP_BLOCK, HEAD_DIM, HEAD_DIM), f32) * (CMP_BLOCK * HEAD_DIM) ** -0.5,
        'ml_conv_w': nrm(ks[20], (DEPTH, CONV_W, 2 * ML_WIDTH), f32) * CONV_W ** -0.5,
        'ml_conv_b': 0.01 * nrm(ks[21], (DEPTH, 2 * ML_WIDTH), f32),
        'ml_gate_b': f_bias[None] + 0.1 * nrm(ks[22], (DEPTH, 2, ML_HEADS), f32),
        'ml_norm_g': 1.0 + 0.02 * nrm(ks[23], (DEPTH, ML_DH), f32),
    }


def reference(x_prompt, x_sample, cache_nsa_kv, state_nsa_win, state_gla, state_mlstm_C, state_mlstm_n,
              state_mlstm_m, state_mlstm_conv, page_table, norm_g, w_in, w_out, gla_w_gate, gla_b_gate,
              gla_norm_g, nsa_q_norm_g, nsa_k_norm_g, nsa_cmp_pos, nsa_cmp_w, ml_conv_w, ml_conv_b,
              ml_gate_b, ml_norm_g):
    f32 = jnp.float32
    bp, sp, _ = x_prompt.shape
    bs, _, _ = x_sample.shape
    n_pages = page_table.shape[1]
    past_len = n_pages * cache_nsa_kv.shape[2]
    dt = x_prompt.dtype
    empty_kv = jnp.zeros((bp, 0, N_KV_SLOTS, NSA_KV_HEADS, HEAD_DIM), dt)
    empty_win = jnp.zeros((bp, 0, 2, NSA_KV_HEADS, HEAD_DIM), dt)
    zero_gla = jnp.zeros((bp, GLA_HEADS, GLA_DK, GLA_DV), f32)
    zero_c = jnp.zeros((bp, ML_HEADS, ML_DH, ML_DH), f32)
    zero_n = jnp.zeros((bp, ML_HEADS, ML_DH), f32)
    zero_m = jnp.zeros((bp, ML_HEADS), f32)
    zero_conv = jnp.zeros((bp, CONV_W - 1, 2 * ML_WIDTH), dt)
    keep_p = min(WINDOW, sp)
    keep_s = state_nsa_win.shape[2]

    y_prompt, y_sample = x_prompt, x_sample
    p_layers, s_layers = [], []
    for layer in range(DEPTH):
        w = (norm_g[layer], w_in[layer], w_out[layer], gla_w_gate[layer], gla_b_gate[layer], gla_norm_g[layer],
             nsa_q_norm_g[layer], nsa_k_norm_g[layer], nsa_cmp_pos[layer], nsa_cmp_w[layer],
             ml_conv_w[layer], ml_conv_b[layer], ml_gate_b[layer], ml_norm_g[layer])
        y_prompt, *p_new = _hybrid_layer(y_prompt, 0, empty_kv, empty_win, keep_p, zero_gla, zero_c, zero_n,
                                         zero_m, zero_conv, *w)
        past = cache_nsa_kv[layer][page_table].reshape(bs, past_len, N_KV_SLOTS, NSA_KV_HEADS, HEAD_DIM)
        y_sample, *s_new = _hybrid_layer(y_sample, past_len, past, state_nsa_win[layer], keep_s,
                                         state_gla[layer], state_mlstm_C[layer], state_mlstm_n[layer],
                                         state_mlstm_m[layer], state_mlstm_conv[layer], *w)
        p_layers.append(p_new)
        s_layers.append(s_new)
    p_kv, p_win, p_gla, p_c, p_n, p_m, p_conv = [jnp.stack(z) for z in zip(*p_layers)]
    s_kv, s_win, s_gla, s_c, s_n, s_m, s_conv = [jnp.stack(z) for z in zip(*s_layers)]
    return (y_prompt, y_sample, p_kv, s_kv, p_win, s_win, p_gla, s_gla, p_c, s_c, p_n, s_n, p_m, s_m, p_conv, s_conv)
```

```python
import functools
import math

import jax
import jax.numpy as jnp
import numpy as np
from jax import lax
from jax.experimental import pallas as pl
from jax.experimental.pallas import tpu as pltpu

F32 = jnp.float32
BF16 = jnp.bfloat16

D_MODEL = 1024
HEAD_DIM = 64
GLA_WIDTH = D_MODEL // 4
NSA_WIDTH = D_MODEL // 2
ML_WIDTH = D_MODEL - GLA_WIDTH - NSA_WIDTH
D_MIX = GLA_WIDTH + NSA_WIDTH + ML_WIDTH

GLA_HEADS = GLA_WIDTH // HEAD_DIM
GLA_DK = HEAD_DIM // 2
GLA_DV = HEAD_DIM
GLA_RANK = 16
GLA_TAU = 16.0
GLA_CHUNK = 64

NSA_HEADS = NSA_WIDTH // HEAD_DIM
NSA_KV_HEADS = 2
NSA_HPG = NSA_HEADS // NSA_KV_HEADS
CMP_BLOCK = 32
CMP_STRIDE = 16
SEL_BLOCK = 64
N_SELECT = 16
WINDOW = 512
Q_BLOCK = 128
N_KV_SLOTS = 4
ROT_DIM = HEAD_DIM // 4
ROPE_THETA = 500000.0
ATTN_SCALE = HEAD_DIM ** -0.5

ML_HEADS = ML_WIDTH // HEAD_DIM
ML_DH = HEAD_DIM
ML_CHUNK = 64
CONV_W = 4

SPLIT_SIZES = (GLA_HEADS * GLA_DK, GLA_HEADS * GLA_DK, GLA_WIDTH, GLA_RANK, GLA_WIDTH,
               NSA_WIDTH, 6 * NSA_KV_HEADS * HEAD_DIM, 3 * NSA_HEADS, NSA_WIDTH,
               2 * ML_WIDTH, ML_WIDTH, 2 * ML_HEADS, ML_WIDTH, ML_WIDTH)

LANES = 128
VMEM_LIMIT = 56 * 1024 * 1024
NEG_BIG = -1e30


def _round_up(n, m):
    return -(-n // m) * m


PAD_SIZES = tuple(_round_up(s, LANES) for s in SPLIT_SIZES)
PAD_OFFS = tuple(int(v) for v in np.cumsum((0,) + PAD_SIZES)[:-1])
D_IN_PAD = sum(PAD_SIZES)


def _proj_in_kernel(x_ref, g_ref, w_ref, o_ref):
    x = x_ref[...]
    y = x * lax.rsqrt(jnp.mean(x * x, axis=-1, keepdims=True) + 1e-6) * g_ref[...]
    o_ref[...] = jnp.dot(y.astype(BF16), w_ref[...], preferred_element_type=F32)


def _proj_in(x2d, g, w_pad, tm):
    rows = x2d.shape[0]
    return pl.pallas_call(
        _proj_in_kernel,
        grid=(rows // tm,),
        in_specs=[pl.BlockSpec((tm, D_MODEL), lambda i: (i, 0)),
                  pl.BlockSpec((1, D_MODEL), lambda i: (0, 0)),
                  pl.BlockSpec((D_MODEL, D_IN_PAD), lambda i: (0, 0))],
        out_specs=pl.BlockSpec((tm, D_IN_PAD), lambda i: (i, 0)),
        out_shape=jax.ShapeDtypeStruct((rows, D_IN_PAD), F32),
        compiler_params=pltpu.CompilerParams(dimension_semantics=("parallel",),
                                             vmem_limit_bytes=VMEM_LIMIT),
        name="proj_in",
    )(x2d, g.reshape(1, D_MODEL), w_pad)


def _pad_w_in(w_in):
    parts = []
    off = 0
    for s, p in zip(SPLIT_SIZES, PAD_SIZES):
        seg = w_in[:, off:off + s]
        if p != s:
            seg = jnp.pad(seg, ((0, 0), (0, p - s)))
        parts.append(seg)
        off += s
    return jnp.concatenate(parts, axis=1).astype(BF16)


def _proj_out_kernel(m_ref, x_ref, w_ref, o_ref):
    o_ref[...] = x_ref[...] + jnp.dot(m_ref[...].astype(BF16), w_ref[...], preferred_element_type=F32)


def _proj_out(mixed2d, x2d, w_bf, tm):
    rows = x2d.shape[0]
    return pl.pallas_call(
        _proj_out_kernel,
        grid=(rows // tm,),
        in_specs=[pl.BlockSpec((tm, D_MIX), lambda i: (i, 0)),
                  pl.BlockSpec((tm, D_MODEL), lambda i: (i, 0)),
                  pl.BlockSpec((D_MIX, D_MODEL), lambda i: (0, 0))],
        out_specs=pl.BlockSpec((tm, D_MODEL), lambda i: (i, 0)),
        out_shape=jax.ShapeDtypeStruct((rows, D_MODEL), F32),
        compiler_params=pltpu.CompilerParams(dimension_semantics=("parallel",),
                                             vmem_limit_bytes=VMEM_LIMIT),
        name="proj_out",
    )(mixed2d, x2d, w_bf)


SEL_TQ = 128
SEL_TK = 512


def _sel_attn_kernel(q_ref, k_ref, v_ref, sel_ref, o_ref, m_sc, l_sc, acc_sc):
    qi = pl.program_id(1)
    ki = pl.program_id(2)
    kmax = (qi * SEL_TQ + SEL_TQ - 1) // SEL_TK
    n_blk = sel_ref.shape[-1] // NSA_KV_HEADS

    @pl.when(ki == 0)
    def _init():
        m_sc[...] = jnp.full(m_sc.shape, NEG_BIG, F32)
        l_sc[...] = jnp.zeros(l_sc.shape, F32)
        acc_sc[...] = jnp.zeros(acc_sc.shape, F32)

    @pl.when(ki <= kmax)
    def _step():
        e_row = lax.broadcasted_iota(jnp.int32, (n_blk, SEL_TK), 0)
        e_col = lax.broadcasted_iota(jnp.int32, (n_blk, SEL_TK), 1)
        expand = jnp.where(e_row == ki * (SEL_TK // SEL_BLOCK) + e_col // SEL_BLOCK, 1.0, 0.0).astype(BF16)
        qpos = qi * SEL_TQ + lax.broadcasted_iota(jnp.int32, (SEL_TQ, SEL_TK), 0)
        kpos = ki * SEL_TK + lax.broadcasted_iota(jnp.int32, (SEL_TQ, SEL_TK), 1)
        causal = kpos <= qpos
        for g in range(NSA_KV_HEADS):
            selg = sel_ref[0, :, g * n_blk:(g + 1) * n_blk]
            picked = jnp.dot(selg, expand, preferred_element_type=F32)
            mask = jnp.where(causal, picked, 0.0) > 0.5
            kg = k_ref[0, :, g * HEAD_DIM:(g + 1) * HEAD_DIM]
            vg = v_ref[0, :, g * HEAD_DIM:(g + 1) * HEAD_DIM]
            for h in range(NSA_HPG):
                hh = g * NSA_HPG + h
                qh = q_ref[0, :, hh * HEAD_DIM:(hh + 1) * HEAD_DIM]
                s = lax.dot_general(qh, kg, (((1,), (1,)), ((), ())), preferred_element_type=F32)
                s = jnp.where(mask, s, NEG_BIG)
                m_prev = m_sc[hh]
                m_new = jnp.maximum(m_prev, jnp.max(s, axis=-1, keepdims=True))
                alpha = jnp.exp(m_prev - m_new)
                p = jnp.where(mask, jnp.exp(s - m_new[:, :1]), 0.0)
                l_sc[hh] = alpha * l_sc[hh] + jnp.sum(p, axis=-1, keepdims=True)
                m_sc[hh] = m_new
                pv = jnp.dot(p.astype(BF16), vg, preferred_element_type=F32)
                acc_sc[hh] = alpha[:, :HEAD_DIM] * acc_sc[hh] + pv

    @pl.when(ki == kmax)
    def _fin():
        for hh in range(NSA_HEADS):
            o_ref[0, :, hh * HEAD_DIM:(hh + 1) * HEAD_DIM] = acc_sc[hh] / l_sc[hh][:, :HEAD_DIM]


def _sel_attn(q, k, v, sel):
    b, l, _ = q.shape
    nq, nk = l // SEL_TQ, l // SEL_TK

    def kv_map(bi, qi, ki):
        return (bi, jnp.minimum(ki, (qi * SEL_TQ + SEL_TQ - 1) // SEL_TK), 0)

    return pl.pallas_call(
        _sel_attn_kernel,
        grid=(b, nq, nk),
        in_specs=[pl.BlockSpec((1, SEL_TQ, NSA_WIDTH), lambda bi, qi, ki: (bi, qi, 0)),
                  pl.BlockSpec((1, SEL_TK, NSA_KV_HEADS * HEAD_DIM), kv_map),
                  pl.BlockSpec((1, SEL_TK, NSA_KV_HEADS * HEAD_DIM), kv_map),
                  pl.BlockSpec((1, SEL_TQ, sel.shape[-1]), lambda bi, qi, ki: (bi, qi, 0))],
        out_specs=pl.BlockSpec((1, SEL_TQ, NSA_WIDTH), lambda bi, qi, ki: (bi, qi, 0)),
        out_shape=jax.ShapeDtypeStruct((b, l, NSA_WIDTH), F32),
        scratch_shapes=[pltpu.VMEM((NSA_HEADS, SEL_TQ, LANES), F32),
                        pltpu.VMEM((NSA_HEADS, SEL_TQ, LANES), F32),
                        pltpu.VMEM((NSA_HEADS, SEL_TQ, HEAD_DIM), F32)],
        compiler_params=pltpu.CompilerParams(dimension_semantics=("parallel", "parallel", "arbitrary"),
                                             vmem_limit_bytes=VMEM_LIMIT),
        name="sel_attn",
    )(q, k, v, sel)


def _rms_norm(x, g, eps=1e-6):
    xf = x.astype(F32)
    y = xf * lax.rsqrt(jnp.mean(xf * xf, axis=-1, keepdims=True) + eps)
    return (y * g.astype(F32)).astype(x.dtype)


def _rope(x, pos):
    half = ROT_DIM // 2
    inv = jnp.exp(-math.log(ROPE_THETA) * jnp.arange(half, dtype=F32) * 2.0 / ROT_DIM)
    ang = pos.astype(F32)[:, None] * inv[None, :]
    cos = jnp.cos(ang)[:, None, :]
    sin = jnp.sin(ang)[:, None, :]
    xf = x.astype(F32)
    x1, x2, rest = xf[..., :half], xf[..., half:ROT_DIM], xf[..., ROT_DIM:]
    return jnp.concatenate([x1 * cos - x2 * sin, x2 * cos + x1 * sin, rest], axis=-1).astype(x.dtype)


def _masked_softmax(s, mask):
    s = jnp.where(mask, s, -jnp.inf)
    m = jnp.max(s, axis=-1, keepdims=True)
    m = jnp.where(jnp.isfinite(m), m, 0.0)
    e = jnp.where(mask, jnp.exp(s - m), 0.0)
    return e / jnp.maximum(jnp.sum(e, axis=-1, keepdims=True), 1e-30)


def _to_chunks(x, c):
    b, l = x.shape[:2]
    x = x.reshape((b, l // c, c) + x.shape[2:])
    x = jnp.moveaxis(x, 1, 0)
    return jnp.moveaxis(x, 2, 3)


def _from_chunks(x):
    n, b, h, c, d = x.shape
    return jnp.transpose(x, (1, 0, 3, 2, 4)).reshape(b, n * c, h, d)


def _gla_chunked(q, k, v, log_a, s0):
    l = q.shape[1]
    c = math.gcd(l, GLA_CHUNK)
    qc, kc, vc, gc = (_to_chunks(t.astype(F32), c) for t in (q, k, v, log_a))
    tri = jnp.tril(jnp.ones((c, c), dtype=bool))

    def step(s, inp):
        qi, ki, vi, gi = inp
        bcum = jnp.cumsum(gi, axis=2)
        diff = bcum[:, :, :, None, :] - bcum[:, :, None, :, :]
        dec = jnp.exp(jnp.where(tri[:, :, None], diff, -jnp.inf))
        a = jnp.einsum('bhid,bhjd,bhijd->bhij', qi, ki, dec)
        o = (jnp.einsum('bhid,bhde->bhie', qi * jnp.exp(bcum), s)
             + jnp.einsum('bhij,bhje->bhie', a, vi))
        blast = bcum[:, :, -1]
        s = (jnp.exp(blast)[..., None] * s
             + jnp.einsum('bhjd,bhje->bhde', ki * jnp.exp(blast[:, :, None] - bcum), vi))
        return s, o

    s, o = lax.scan(step, s0.astype(F32), (qc, kc, vc, gc))
    return _from_chunks(o), s


def _mlstm_chunked(q, k, v, i_pre, logf, c0, n0, m0):
    l = q.shape[1]
    c = math.gcd(l, ML_CHUNK)
    qc, kc, vc = (_to_chunks(t.astype(F32), c) for t in (q, k, v))
    ic, fc = (_to_chunks(t.astype(F32), c) for t in (i_pre, logf))
    tri = jnp.tril(jnp.ones((c, c), dtype=bool))

    def step(carry, inp):
        cs, ns, ms = carry
        qi, ki, vi, ii, fi = inp
        fcum = jnp.cumsum(fi, axis=-1)
        dmat = jnp.where(tri, fcum[..., :, None] - fcum[..., None, :] + ii[..., None, :], -jnp.inf)
        inter = fcum + ms[..., None]
        m = jnp.maximum(inter, jnp.max(dmat, axis=-1))
        w_int = jnp.exp(inter - m)
        sij = jnp.einsum('bhid,bhjd->bhij', qi, ki) * jnp.exp(dmat - m[..., None])
        num = (w_int[..., None] * jnp.einsum('bhed,bhid->bhie', cs, qi)
               + jnp.einsum('bhij,bhje->bhie', sij, vi))
        den = w_int * jnp.einsum('bhd,bhid->bhi', ns, qi) + jnp.sum(sij, axis=-1)
        h = num / jnp.maximum(jnp.abs(den), jnp.exp(-m))[..., None]
        m_last = m[..., -1]
        f_last = fcum[..., -1]
        decay = jnp.exp(f_last + ms - m_last)
        wj = jnp.exp(f_last[..., None] - fcum + ii - m_last[..., None])
        cs = decay[..., None, None] * cs + jnp.einsum('bhj,bhje,bhjd->bhed', wj, vi, ki)
        ns = decay[..., None] * ns + jnp.einsum('bhj,bhjd->bhd', wj, ki)
        return (cs, ns, m_last), h

    init = (c0.astype(F32), n0.astype(F32), m0.astype(F32))
    (cs, ns, ms), h = lax.scan(step, init, (qc, kc, vc, ic, fc))
    return _from_chunks(h), cs, ns, ms


def _cmp_summaries(x, pe, w):
    b, t, g, d = x.shape
    n_half = t // CMP_STRIDE
    halves = x[:, :n_half * CMP_STRIDE].astype(F32).reshape(b, n_half, CMP_STRIDE, g, d)
    wf = w.astype(F32)
    first = jnp.einsum('bnsgd,sde->bnge', halves[:, :-1], wf[:CMP_STRIDE])
    second = jnp.einsum('bnsgd,sde->bnge', halves[:, 1:], wf[CMP_STRIDE:])
    bias = jnp.einsum('sd,sde->e', pe.astype(F32), wf)
    return first + second + bias


def _overlap_matrix(n_cmp, n_sel):
    a = SEL_BLOCK // CMP_STRIDE
    bb = CMP_BLOCK // CMP_STRIDE
    i = jnp.arange(n_cmp)[:, None]
    j = jnp.arange(n_sel)[None, :]
    s = i - a * j + (bb - 1)
    cnt = jnp.minimum(jnp.minimum(s + 1, a + bb - 1 - s), min(a, bb))
    return jnp.maximum(cnt, 0).astype(F32)


def _nsa_selected_gather(q, k_hist, v_hist, idx, pos):
    b, l = q.shape[:2]
    t_len = k_hist.shape[1]
    n_sel = -(-t_len // SEL_BLOCK)
    n_k = idx.shape[-1]
    pad = n_sel * SEL_BLOCK - t_len

    def blocks(x):
        x = jnp.pad(x.astype(F32), ((0, 0), (0, pad), (0, 0), (0, 0)))
        return x.reshape(b, n_sel, SEL_BLOCK, NSA_KV_HEADS, HEAD_DIM).transpose(0, 3, 1, 2, 4)

    kb, vb = blocks(k_hist), blocks(v_hist)
    qb = math.gcd(l, Q_BLOCK)
    nq = l // qb
    qx = q.astype(F32).reshape(b * nq, qb, NSA_KV_HEADS, NSA_HPG, HEAD_DIM)
    ix = idx.reshape(b, NSA_KV_HEADS, nq, qb, n_k).transpose(0, 2, 1, 3, 4).reshape(b * nq, NSA_KV_HEADS, qb, n_k)
    px = jnp.tile(pos.reshape(nq, qb), (b, 1))
    bx = jnp.repeat(jnp.arange(b), nq)
    g_ix = jnp.arange(NSA_KV_HEADS)[:, None, None]
    offs = jnp.arange(SEL_BLOCK)
    n_keys = n_k * SEL_BLOCK

    def attend(args):
        qi, ii, pi, bi = args
        ks = kb[bi][g_ix, ii]
        vs = vb[bi][g_ix, ii]
        kpos = ii[..., None] * SEL_BLOCK + offs
        mask = (kpos <= pi[None, :, None, None]).reshape(NSA_KV_HEADS, 1, qb, n_keys)
        s = jnp.einsum('qghd,gqkjd->ghqkj', qi, ks).reshape(NSA_KV_HEADS, NSA_HPG, qb, n_keys) * ATTN_SCALE
        p = _masked_softmax(s, mask)
        return jnp.einsum('ghqn,gqnd->qghd', p, vs.reshape(NSA_KV_HEADS, qb, n_keys, HEAD_DIM))

    o = lax.map(attend, (qx, ix, px, bx))
    return o.reshape(b, l, NSA_HEADS, HEAD_DIM)


def _nsa_window(q, k_ctx, v_ctx, pos0):
    b, l = q.shape[:2]
    lc = k_ctx.shape[1]
    padw = ((0, 0), (WINDOW, 0), (0, 0), (0, 0))
    kp = jnp.pad(k_ctx.astype(F32), padw)
    vp = jnp.pad(v_ctx.astype(F32), padw)
    qb = math.gcd(l, Q_BLOCK)
    nq = l // qb
    span = WINDOW + qb
    qx = jnp.moveaxis(q.astype(F32).reshape(b, nq, qb, NSA_KV_HEADS, NSA_HPG, HEAD_DIM), 1, 0)
    starts = jnp.arange(nq) * qb
    first_pos = pos0 + l - lc

    def attend(args):
        qi, i0 = args
        kk = lax.dynamic_slice_in_dim(kp, i0 + lc - l, span, axis=1)
        vv = lax.dynamic_slice_in_dim(vp, i0 + lc - l, span, axis=1)
        kpos = pos0 + i0 - WINDOW + jnp.arange(span)
        qpos = pos0 + i0 + jnp.arange(qb)
        mask = ((kpos[None, :] >= first_pos) & (kpos[None, :] <= qpos[:, None])
                & (kpos[None, :] > qpos[:, None] - WINDOW))
        s = jnp.einsum('bqghd,bkgd->bghqk', qi, kk) * ATTN_SCALE
        p = _masked_softmax(s, mask)
        return jnp.einsum('bghqk,bkgd->bqghd', p, vv)

    o = lax.map(attend, (qx, starts))
    return jnp.moveaxis(o, 0, 1).reshape(b, l, NSA_HEADS, HEAD_DIM)


def _nsa_mixer(n_q, n_kv, n_g, past_kv, win_buf, win_keep, pos0, q_g, k_g, cmp_pos, cmp_w):
    b, l, _ = n_q.shape
    pos = pos0 + jnp.arange(l)
    q = _rms_norm(n_q.reshape(b, l, NSA_HEADS, HEAD_DIM), q_g)
    q_rot = _rope(q, pos)
    kv = n_kv.reshape(b, l, 6, NSA_KV_HEADS, HEAD_DIM)
    k_slc = _rope(_rms_norm(kv[:, :, 2], k_g[1]), pos)
    k_win = _rope(_rms_norm(kv[:, :, 4], k_g[2]), pos)
    new_rows = jnp.stack([kv[:, :, 0], kv[:, :, 1], k_slc, kv[:, :, 3]], axis=2)
    new_win = jnp.stack([k_win, kv[:, :, 5]], axis=2)
    if past_kv is None:
        hist = new_rows
        ctx = new_win
    else:
        hist = jnp.concatenate([past_kv.astype(new_rows.dtype), new_rows], axis=1)
        ctx = jnp.concatenate([win_buf.astype(new_win.dtype), new_win], axis=1)
    t_len = hist.shape[1]

    kc = _rms_norm(_cmp_summaries(hist[:, :, 0], cmp_pos[0], cmp_w[0]), k_g[0])
    vc = _cmp_summaries(hist[:, :, 1], cmp_pos[1], cmp_w[1])
    n_cmp = kc.shape[1]
    qg = q.astype(F32).reshape(b, l, NSA_KV_HEADS, NSA_HPG, HEAD_DIM)
    s = jnp.einsum('blghd,bngd->bghln', qg, kc) * ATTN_SCALE
    cmp_end = jnp.arange(n_cmp) * CMP_STRIDE + CMP_BLOCK - 1
    p_cmp = _masked_softmax(s, cmp_end[None, :] <= pos[:, None])
    o_cmp = jnp.einsum('bghln,bngd->blghd', p_cmp, vc).reshape(b, l, NSA_HEADS, HEAD_DIM)

    n_sel = -(-t_len // SEL_BLOCK)
    imp = jnp.einsum('bghln,nj->bglj', p_cmp, _overlap_matrix(n_cmp, n_sel))
    blk = jnp.arange(n_sel)[None, :]
    cur = (pos // SEL_BLOCK)[:, None]
    valid = blk * SEL_BLOCK <= pos[:, None]
    forced = (blk == 0) | (blk == cur) | (blk == cur - 1)
    score = jnp.where(forced, jnp.inf, jnp.where(valid, imp, -jnp.inf))
    _, idx = lax.top_k(score, min(N_SELECT, n_sel))
    if past_kv is None:
        sel = jnp.any(idx[..., None] == jnp.arange(n_sel), axis=-2)
        sel = jnp.transpose(sel, (0, 2, 1, 3)).reshape(b, l, NSA_KV_HEADS * n_sel).astype(BF16)
        qs = (q_rot * ATTN_SCALE).reshape(b, l, NSA_WIDTH).astype(BF16)
        ks = hist[:, :, 2].reshape(b, l, NSA_KV_HEADS * HEAD_DIM).astype(BF16)
        vs = hist[:, :, 3].reshape(b, l, NSA_KV_HEADS * HEAD_DIM).astype(BF16)
        o_slc = _sel_attn(qs, ks, vs, sel).reshape(b, l, NSA_HEADS, HEAD_DIM)
    else:
        o_slc = _nsa_selected_gather(q_rot, hist[:, :, 2], hist[:, :, 3], idx, pos)

    o_win = _nsa_window(q_rot, ctx[:, :, 0], ctx[:, :, 1], pos0)

    gate = jax.nn.sigmoid(n_g.astype(F32)).reshape(b, l, 3, NSA_HEADS, 1)
    o = gate[:, :, 0] * o_cmp + gate[:, :, 1] * o_slc + gate[:, :, 2] * o_win
    return o.reshape(b, l, NSA_WIDTH), new_rows, ctx[:, ctx.shape[1] - win_keep:]


def _hybrid_layer(x, pos0, past_kv, win_buf, win_keep, s_gla, c_ml, n_ml, m_ml, conv_ml,
                  norm_g, w_in_pad, w_out_bf, gla_w_gate, gla_b_gate, gla_norm_g,
                  nsa_q_norm_g, nsa_k_norm_g, nsa_cmp_pos, nsa_cmp_w,
                  ml_conv_w, ml_conv_b, ml_gate_b, ml_norm_g):
    b, l, _ = x.shape
    rows = b * l
    tm = 256
    x2d = x.reshape(rows, D_MODEL)
    u = _proj_in(x2d, norm_g, w_in_pad, tm).reshape(b, l, D_IN_PAD)
    (g_q, g_k, g_v, g_a, g_z, n_q, n_kv, n_g, n_z,
     m_qk, m_v, m_if, m_o, m_z) = [u[..., o:o + s] for o, s in zip(PAD_OFFS, SPLIT_SIZES)]

    gq = g_q.reshape(b, l, GLA_HEADS, GLA_DK) * GLA_DK ** -0.5
    gk = g_k.reshape(b, l, GLA_HEADS, GLA_DK)
    gv = g_v.reshape(b, l, GLA_HEADS, GLA_DV)
    log_a = jax.nn.log_sigmoid((g_a @ gla_w_gate + gla_b_gate).astype(F32)).reshape(b, l, GLA_HEADS, GLA_DK) / GLA_TAU
    o_a, s_new = _gla_chunked(gq, gk, gv, log_a, s_gla)
    o_a = _rms_norm(o_a, gla_norm_g).reshape(b, l, GLA_WIDTH) * jax.nn.silu(g_z.astype(F32))

    o_b, new_rows, new_win = _nsa_mixer(n_q, n_kv, n_g, past_kv, win_buf, win_keep, pos0,
                                        nsa_q_norm_g, nsa_k_norm_g, nsa_cmp_pos, nsa_cmp_w)
    o_b = o_b * jax.nn.silu(n_z.astype(F32))

    xp = jnp.concatenate([conv_ml.astype(m_qk.dtype), m_qk], axis=1)
    conv = sum(xp[:, w:w + l] * ml_conv_w[w] for w in range(CONV_W)) + ml_conv_b
    qk = jax.nn.silu(conv)
    mq = qk[..., :ML_WIDTH].reshape(b, l, ML_HEADS, ML_DH)
    mk = qk[..., ML_WIDTH:].reshape(b, l, ML_HEADS, ML_DH) * ML_DH ** -0.5
    mv = m_v.reshape(b, l, ML_HEADS, ML_DH)
    gates = m_if.astype(F32).reshape(b, l, 2, ML_HEADS) + ml_gate_b.astype(F32)
    h, c_new, n_new, m_new = _mlstm_chunked(mq, mk, mv, gates[:, :, 0], jax.nn.log_sigmoid(gates[:, :, 1]),
                                            c_ml, n_ml, m_ml)
    o_gate = jax.nn.sigmoid(m_o.astype(F32)).reshape(b, l, ML_HEADS, ML_DH)
    o_c = (_rms_norm(h, ml_norm_g) * o_gate).reshape(b, l, ML_WIDTH) * jax.nn.silu(m_z.astype(F32))
    conv_new = xp[:, l:]

    mixed = jnp.concatenate([o_a, o_b, o_c], axis=-1).astype(x.dtype)
    y = _proj_out(mixed.reshape(rows, D_MIX), x2d, w_out_bf, tm).reshape(b, l, D_MODEL)
    return y, new_rows, new_win, s_new, c_new, n_new, m_new, conv_new


def kernel(x_prompt, x_sample, cache_nsa_kv, state_nsa_win, state_gla, state_mlstm_C, state_mlstm_n,
           state_mlstm_m, state_mlstm_conv, page_table, norm_g, w_in, w_out, gla_w_gate, gla_b_gate,
           gla_norm_g, nsa_q_norm_g, nsa_k_norm_g, nsa_cmp_pos, nsa_cmp_w, ml_conv_w, ml_conv_b,
           ml_gate_b, ml_norm_g):
    bp, sp, _ = x_prompt.shape
    bs, _, _ = x_sample.shape
    depth = w_in.shape[0]
    n_pages = page_table.shape[1]
    past_len = n_pages * cache_nsa_kv.shape[2]
    dt = x_prompt.dtype
    zero_gla = jnp.zeros((bp, GLA_HEADS, GLA_DK, GLA_DV), F32)
    zero_c = jnp.zeros((bp, ML_HEADS, ML_DH, ML_DH), F32)
    zero_n = jnp.zeros((bp, ML_HEADS, ML_DH), F32)
    zero_m = jnp.zeros((bp, ML_HEADS), F32)
    zero_conv = jnp.zeros((bp, CONV_W - 1, 2 * ML_WIDTH), dt)
    keep_p = min(WINDOW, sp)
    keep_s = state_nsa_win.shape[2]

    y_prompt, y_sample = x_prompt, x_sample
    p_layers, s_layers = [], []
    for layer in range(depth):
        w = (norm_g[layer], _pad_w_in(w_in[layer]), w_out[layer].astype(BF16), gla_w_gate[layer],
             gla_b_gate[layer], gla_norm_g[layer],
             nsa_q_norm_g[layer], nsa_k_norm_g[layer], nsa_cmp_pos[layer], nsa_cmp_w[layer],
             ml_conv_w[layer], ml_conv_b[layer], ml_gate_b[layer], ml_norm_g[layer])
        y_prompt, *p_new = _hybrid_layer(y_prompt, 0, None, None, keep_p, zero_gla, zero_c, zero_n,
                                         zero_m, zero_conv, *w)
        past = cache_nsa_kv[layer][page_table].reshape(bs, past_len, N_KV_SLOTS, NSA_KV_HEADS, HEAD_DIM)
        y_sample, *s_new = _hybrid_layer(y_sample, past_len, past, state_nsa_win[layer], keep_s,
                                         state_gla[layer], state_mlstm_C[layer], state_mlstm_n[layer],
                                         state_mlstm_m[layer], state_mlstm_conv[layer], *w)
        p_layers.append(p_new)
        s_layers.append(s_new)
    p_kv, p_win, p_gla, p_c, p_n, p_m, p_conv = [jnp.stack(z) for z in zip(*p_layers)]
    s_kv, s_win, s_gla, s_c, s_n, s_m, s_conv = [jnp.stack(z) for z in zip(*s_layers)]
    return (y_prompt, y_sample, p_kv, s_kv, p_win, s_win, p_gla, s_gla, p_c, s_c, p_n, s_n, p_m, s_m, p_conv, s_conv)
```

```python
import functools
import math

import jax
import jax.numpy as jnp
import numpy as np
from jax import lax
from jax.experimental import pallas as pl
from jax.experimental.pallas import tpu as pltpu

F32 = jnp.float32
BF16 = jnp.bfloat16
HIGHEST = lax.Precision.HIGHEST

D_MODEL = 1024
HEAD_DIM = 64
GLA_WIDTH = D_MODEL // 4
NSA_WIDTH = D_MODEL // 2
ML_WIDTH = D_MODEL - GLA_WIDTH - NSA_WIDTH
D_MIX = GLA_WIDTH + NSA_WIDTH + ML_WIDTH

GLA_HEADS = GLA_WIDTH // HEAD_DIM
GLA_DK = HEAD_DIM // 2
GLA_DV = HEAD_DIM
GLA_RANK = 16
GLA_TAU = 16.0
GLA_CHUNK = 64

NSA_HEADS = NSA_WIDTH // HEAD_DIM
NSA_KV_HEADS = 2
NSA_HPG = NSA_HEADS // NSA_KV_HEADS
CMP_BLOCK = 32
CMP_STRIDE = 16
SEL_BLOCK = 64
N_SELECT = 16
WINDOW = 512
Q_BLOCK = 128
N_KV_SLOTS = 4
ROT_DIM = HEAD_DIM // 4
ROPE_THETA = 500000.0
ATTN_SCALE = HEAD_DIM ** -0.5

ML_HEADS = ML_WIDTH // HEAD_DIM
ML_DH = HEAD_DIM
ML_CHUNK = 64
CONV_W = 4

SPLIT_SIZES = (GLA_HEADS * GLA_DK, GLA_HEADS * GLA_DK, GLA_WIDTH, GLA_RANK, GLA_WIDTH,
               NSA_WIDTH, 6 * NSA_KV_HEADS * HEAD_DIM, 3 * NSA_HEADS, NSA_WIDTH,
               2 * ML_WIDTH, ML_WIDTH, 2 * ML_HEADS, ML_WIDTH, ML_WIDTH)

LANES = 128
SUBLANES = 8
VMEM_LIMIT = 56 * 1024 * 1024
NEG_BIG = -1e30
EPS = 1e-6


def _round_up(n, m):
    return -(-n // m) * m


PAD_SIZES = tuple(_round_up(s, LANES) for s in SPLIT_SIZES)
D_IN_PAD = sum(PAD_SIZES)
W_GLA = sum(PAD_SIZES[0:5])
W_NSA = sum(PAD_SIZES[5:9])
W_ML = sum(PAD_SIZES[9:14])
KV_W = NSA_KV_HEADS * HEAD_DIM


def _dot(a, b):
    return jnp.dot(a.astype(BF16), b.astype(BF16), preferred_element_type=F32)


def _dot_nt(a, b):
    return lax.dot_general(a.astype(BF16), b.astype(BF16), (((1,), (1,)), ((), ())), preferred_element_type=F32)


def _dot_tn(a, b):
    return lax.dot_general(a.astype(BF16), b.astype(BF16), (((0,), (0,)), ((), ())), preferred_element_type=F32)


def _dot_f32(a, b):
    return jnp.dot(a, b, precision=HIGHEST, preferred_element_type=F32)


def _dot_nt_f32(a, b):
    return lax.dot_general(a, b, (((1,), (1,)), ((), ())), precision=HIGHEST, preferred_element_type=F32)


def _log_sigmoid(x):
    return jnp.minimum(x, 0.0) - jnp.log1p(jnp.exp(-jnp.abs(x)))


def _sigmoid(x):
    return 1.0 / (1.0 + jnp.exp(-x))


def _silu(x):
    return x * _sigmoid(x)


def _group_mean_matrix(width):
    g = np.kron(np.eye(width // HEAD_DIM, dtype=np.float32), np.full((HEAD_DIM, HEAD_DIM), 1.0 / HEAD_DIM, np.float32))
    return jnp.asarray(g)


def _group_norm(x, gmat, gain):
    ms = _dot_f32(x * x, gmat)
    return x * lax.rsqrt(ms + EPS) * gain


def _cparams(*sem):
    return pltpu.CompilerParams(dimension_semantics=sem, vmem_limit_bytes=VMEM_LIMIT)


def _proj_in_kernel(x_ref, g_ref, w_ref, ug_ref, un_ref, um_ref):
    x = x_ref[...]
    y = x * lax.rsqrt(jnp.mean(x * x, axis=-1, keepdims=True) + EPS) * g_ref[...]
    r = jnp.dot(y.astype(BF16), w_ref[...], preferred_element_type=F32)
    ug_ref[...] = r[:, 0:W_GLA]
    un_ref[...] = r[:, W_GLA:W_GLA + W_NSA]
    um_ref[...] = r[:, W_GLA + W_NSA:D_IN_PAD]


def _proj_in(x2d, g, w_pad, tm):
    rows = x2d.shape[0]
    return pl.pallas_call(
        _proj_in_kernel,
        grid=(rows // tm,),
        in_specs=[pl.BlockSpec((tm, D_MODEL), lambda i: (i, 0)),
                  pl.BlockSpec((1, D_MODEL), lambda i: (0, 0)),
                  pl.BlockSpec((D_MODEL, D_IN_PAD), lambda i: (0, 0))],
        out_specs=[pl.BlockSpec((tm, W_GLA), lambda i: (i, 0)),
                   pl.BlockSpec((tm, W_NSA), lambda i: (i, 0)),
                   pl.BlockSpec((tm, W_ML), lambda i: (i, 0))],
        out_shape=[jax.ShapeDtypeStruct((rows, W_GLA), F32),
                   jax.ShapeDtypeStruct((rows, W_NSA), F32),
                   jax.ShapeDtypeStruct((rows, W_ML), F32)],
        compiler_params=_cparams("parallel"),
        name="proj_in",
    )(x2d, g.reshape(1, D_MODEL), w_pad)


def _pad_w_in(w_in):
    parts = []
    off = 0
    for s, p in zip(SPLIT_SIZES, PAD_SIZES):
        seg = w_in[:, off:off + s]
        if p != s:
            seg = jnp.pad(seg, ((0, 0), (0, p - s)))
        parts.append(seg)
        off += s
    return jnp.concatenate(parts, axis=1).astype(BF16)


def _proj_out_kernel(oa_ref, ob_ref, oc_ref, x_ref, w_ref, y_ref):
    y = x_ref[...]
    y = y + _dot(oa_ref[...], w_ref[0:GLA_WIDTH, :])
    y = y + _dot(ob_ref[...], w_ref[GLA_WIDTH:GLA_WIDTH + NSA_WIDTH, :])
    y = y + _dot(oc_ref[...], w_ref[GLA_WIDTH + NSA_WIDTH:D_MIX, :])
    y_ref[...] = y


def _proj_out(oa, ob, oc, x2d, w_bf, tm):
    rows = x2d.shape[0]
    return pl.pallas_call(
        _proj_out_kernel,
        grid=(rows // tm,),
        in_specs=[pl.BlockSpec((tm, GLA_WIDTH), lambda i: (i, 0)),
                  pl.BlockSpec((tm, NSA_WIDTH), lambda i: (i, 0)),
                  pl.BlockSpec((tm, ML_WIDTH), lambda i: (i, 0)),
                  pl.BlockSpec((tm, D_MODEL), lambda i: (i, 0)),
                  pl.BlockSpec((D_MIX, D_MODEL), lambda i: (0, 0))],
        out_specs=pl.BlockSpec((tm, D_MODEL), lambda i: (i, 0)),
        out_shape=jax.ShapeDtypeStruct((rows, D_MODEL), F32),
        compiler_params=_cparams("parallel"),
        name="proj_out",
    )(oa, ob, oc, x2d, w_bf)


def _gla_kernel(u_ref, st0_ref, wg_ref, bg_ref, ng_ref, gm_ref, o_ref, st_ref, *, tl, c):
    @pl.when(pl.program_id(1) == 0)
    def _init():
        st_ref[...] = st0_ref[...]

    nk = GLA_HEADS * GLA_DK
    u = u_ref[0]
    q = u[:, 0:nk] * (GLA_DK ** -0.5)
    k = u[:, nk:2 * nk]
    v = u[:, 2 * nk:2 * nk + GLA_WIDTH]
    a_off = 2 * nk + GLA_WIDTH
    pre = _dot(u[:, a_off:a_off + LANES], wg_ref[...]) + bg_ref[...]
    log_a = _log_sigmoid(pre) * (1.0 / GLA_TAU)
    z = u[:, a_off + LANES:a_off + LANES + GLA_WIDTH]

    tril = (lax.broadcasted_iota(jnp.int32, (c, c), 0) >= lax.broadcasted_iota(jnp.int32, (c, c), 1)).astype(F32)
    hc = GLA_HEADS * c
    tri_h = (lax.broadcasted_iota(jnp.int32, (hc, c), 0) % c) >= lax.broadcasted_iota(jnp.int32, (hc, c), 1)
    k_head = lax.broadcasted_iota(jnp.int32, (1, nk), 1) // GLA_DK
    v_head = lax.broadcasted_iota(jnp.int32, (1, GLA_WIDTH), 1) // GLA_DV
    st_diag = (lax.broadcasted_iota(jnp.int32, (GLA_WIDTH, nk), 0) // GLA_DV
               == lax.broadcasted_iota(jnp.int32, (GLA_WIDTH, nk), 1) // GLA_DK)

    st = st_ref[0]
    outs = []
    for j in range(tl // c):
        sl = slice(j * c, (j + 1) * c)
        b = _dot_f32(tril, log_a[sl])
        blast = b[c - 1:c]
        qe = q[sl] * jnp.exp(b)
        ke = k[sl] * jnp.exp(-b)
        kl = k[sl] * jnp.exp(blast - b)
        vc = v[sl]
        qx = jnp.concatenate([jnp.where(k_head == h, qe, 0.0) for h in range(GLA_HEADS)], axis=0)
        a = jnp.where(tri_h, _dot_nt(qx, ke), 0.0)
        r = _dot(a, vc)
        o = _dot_nt(qe, st)
        for h in range(GLA_HEADS):
            o = o + jnp.where(v_head == h, r[h * c:(h + 1) * c], 0.0)
        st = st * jnp.exp(blast) + jnp.where(st_diag, _dot_tn(vc, kl), 0.0)
        outs.append(o)
    st_ref[0] = st
    o = outs[0] if len(outs) == 1 else jnp.concatenate(outs, axis=0)
    o_ref[0] = _group_norm(o, gm_ref[...], ng_ref[...]) * _silu(z)


def _gla(u_gla, st0, wg, bg, ng, b, l):
    tl = min(l, 256)
    c = min(l, 16)
    nk = GLA_HEADS * GLA_DK
    u3 = u_gla.reshape(b, l, W_GLA)
    kern = functools.partial(_gla_kernel, tl=tl, c=c)
    return pl.pallas_call(
        kern,
        grid=(b, l // tl),
        in_specs=[pl.BlockSpec((1, tl, W_GLA), lambda bi, li: (bi, li, 0)),
                  pl.BlockSpec((1, GLA_WIDTH, nk), lambda bi, li: (bi, 0, 0)),
                  pl.BlockSpec((LANES, nk), lambda bi, li: (0, 0)),
                  pl.BlockSpec((1, nk), lambda bi, li: (0, 0)),
                  pl.BlockSpec((1, GLA_WIDTH), lambda bi, li: (0, 0)),
                  pl.BlockSpec((GLA_WIDTH, GLA_WIDTH), lambda bi, li: (0, 0))],
        out_specs=[pl.BlockSpec((1, tl, GLA_WIDTH), lambda bi, li: (bi, li, 0)),
                   pl.BlockSpec((1, GLA_WIDTH, nk), lambda bi, li: (bi, 0, 0))],
        out_shape=[jax.ShapeDtypeStruct((b, l, GLA_WIDTH), F32),
                   jax.ShapeDtypeStruct((b, GLA_WIDTH, nk), F32)],
        compiler_params=_cparams("parallel", "arbitrary"),
        name="gla",
    )(u3, st0, wg, bg, ng, _group_mean_matrix(GLA_WIDTH))


def _gla_state_in(s):
    b = s.shape[0]
    st = jnp.swapaxes(s.astype(F32), 2, 3)
    eye = jnp.eye(GLA_HEADS, dtype=F32)
    full = st[:, :, :, None, :] * eye[None, :, None, :, None]
    return full.reshape(b, GLA_WIDTH, GLA_HEADS * GLA_DK)


def _gla_state_out(st):
    b = st.shape[0]
    full = st.reshape(b, GLA_HEADS, GLA_DV, GLA_HEADS, GLA_DK)
    diag = jnp.stack([full[:, h, :, h, :] for h in range(GLA_HEADS)], axis=1)
    return jnp.swapaxes(diag, 2, 3)


def _mlstm_kernel(u_ref, c0_ref, n0_ref, m0_ref, cv0_ref, cw_ref, cb_ref, gb_ref, ng_ref,
                  o_ref, c_ref, n_ref, m_ref, cv_ref, xp_sc, *, tl, c):
    @pl.when(pl.program_id(1) == 0)
    def _init():
        c_ref[...] = c0_ref[...]
        n_ref[...] = n0_ref[...]
        m_ref[...] = m0_ref[...]
        xp_sc[0:SUBLANES, :] = cv0_ref[0]

    w2 = 2 * ML_WIDTH
    u = u_ref[0]
    xp_sc[SUBLANES:SUBLANES + tl, :] = u[:, 0:w2]
    conv = cb_ref[...]
    for w in range(CONV_W):
        off = SUBLANES - (CONV_W - 1) + w
        conv = conv + xp_sc[off:off + tl, :] * cw_ref[w:w + 1, :]
    tail = xp_sc[tl:tl + SUBLANES, :]
    xp_sc[0:SUBLANES, :] = tail
    cv_ref[0] = tail

    qk = _silu(conv)
    mq = qk[:, 0:ML_WIDTH]
    mk = qk[:, ML_WIDTH:w2] * (ML_DH ** -0.5)
    mv = u[:, w2:w2 + ML_WIDTH]
    ifg = u[:, w2 + ML_WIDTH:w2 + ML_WIDTH + LANES] + gb_ref[...]
    logf = _log_sigmoid(ifg)
    og_off = w2 + ML_WIDTH + LANES
    og = _sigmoid(u[:, og_off:og_off + ML_WIDTH])
    zz = _silu(u[:, og_off + ML_WIDTH:og_off + 2 * ML_WIDTH])

    row = lax.broadcasted_iota(jnp.int32, (c, c), 0)
    col = lax.broadcasted_iota(jnp.int32, (c, c), 1)
    tri = row >= col
    tril = tri.astype(F32)
    lane = lax.broadcasted_iota(jnp.int32, (c, LANES), 1)

    cs = [c_ref[0, h] for h in range(ML_HEADS)]
    ns = [n_ref[0, h:h + 1, :] for h in range(ML_HEADS)]
    ms = [m_ref[0, h:h + 1, 0:1] for h in range(ML_HEADS)]
    for j in range(tl // c):
        sl = slice(j * c, (j + 1) * c)
        fcum_all = _dot_f32(tril, logf[sl])
        for h in range(ML_HEADS):
            hs = slice(h * ML_DH, (h + 1) * ML_DH)
            fc = fcum_all[:, ML_HEADS + h:ML_HEADS + h + 1]
            ii = ifg[sl, h:h + 1]
            xm = jnp.where(lane == 0, fc, jnp.where(lane < 3, 1.0, 0.0))
            ym = jnp.where(lane == 0, 1.0, jnp.where(lane == 1, -fc, jnp.where(lane == 2, ii, 0.0)))
            dm = jnp.where(tri, _dot_nt_f32(xm, ym), NEG_BIG)
            inter = fc + ms[h]
            m = jnp.maximum(inter, jnp.max(dm, axis=-1, keepdims=True))
            w_int = jnp.exp(inter - m)
            qh, kh, vh = mq[sl, hs], mk[sl, hs], mv[sl, hs]
            sij = _dot_nt(qh, kh) * jnp.exp(dm - m)
            num = w_int * _dot_nt(qh, cs[h]) + _dot(sij, vh)
            den = w_int * jnp.sum(qh * ns[h], axis=-1, keepdims=True) + jnp.sum(sij, axis=-1, keepdims=True)
            hh = num / jnp.maximum(jnp.abs(den), jnp.exp(-m))
            m_last = m[c - 1:c]
            f_last = fc[c - 1:c]
            decay = jnp.exp(f_last + ms[h] - m_last)
            wj = jnp.exp(f_last - fc + ii - m_last)
            cs[h] = decay * cs[h] + _dot_tn(wj * vh, kh)
            ns[h] = decay * ns[h] + jnp.sum(wj * kh, axis=0, keepdims=True)
            ms[h] = m_last
            hn = hh * lax.rsqrt(jnp.mean(hh * hh, axis=-1, keepdims=True) + EPS) * ng_ref[...]
            o_ref[0, sl, hs] = hn * og[sl, hs] * zz[sl, hs]
    for h in range(ML_HEADS):
        c_ref[0, h] = cs[h]
        n_ref[0, h:h + 1, :] = ns[h]
        m_ref[0, h:h + 1, :] = jnp.broadcast_to(ms[h], (1, LANES))


def _mlstm(u_ml, c0, n0, m0, conv0, cw, cb, gb, ng, b, l):
    tl = min(l, 256)
    c = min(l, ML_CHUNK)
    w2 = 2 * ML_WIDTH
    u3 = u_ml.reshape(b, l, W_ML)
    m0b = jnp.broadcast_to(m0.astype(F32)[:, :, None], (b, ML_HEADS, LANES))
    cv0 = jnp.pad(conv0.astype(F32), ((0, 0), (SUBLANES - (CONV_W - 1), 0), (0, 0)))
    gbp = jnp.pad(gb.astype(F32).reshape(1, 2 * ML_HEADS), ((0, 0), (0, LANES - 2 * ML_HEADS)))
    kern = functools.partial(_mlstm_kernel, tl=tl, c=c)
    st = lambda bi, li: (bi, 0, 0)
    st4 = lambda bi, li: (bi, 0, 0, 0)
    cst = lambda bi, li: (0, 0)
    o, c_new, n_new, m_new, cv = pl.pallas_call(
        kern,
        grid=(b, l // tl),
        in_specs=[pl.BlockSpec((1, tl, W_ML), lambda bi, li: (bi, li, 0)),
                  pl.BlockSpec((1, ML_HEADS, ML_DH, ML_DH), st4),
                  pl.BlockSpec((1, ML_HEADS, ML_DH), st),
                  pl.BlockSpec((1, ML_HEADS, LANES), st),
                  pl.BlockSpec((1, SUBLANES, w2), st),
                  pl.BlockSpec((CONV_W, w2), cst),
                  pl.BlockSpec((1, w2), cst),
                  pl.BlockSpec((1, LANES), cst),
                  pl.BlockSpec((1, ML_DH), cst)],
        out_specs=[pl.BlockSpec((1, tl, ML_WIDTH), lambda bi, li: (bi, li, 0)),
                   pl.BlockSpec((1, ML_HEADS, ML_DH, ML_DH), st4),
                   pl.BlockSpec((1, ML_HEADS, ML_DH), st),
                   pl.BlockSpec((1, ML_HEADS, LANES), st),
                   pl.BlockSpec((1, SUBLANES, w2), st)],
        out_shape=[jax.ShapeDtypeStruct((b, l, ML_WIDTH), F32),
                   jax.ShapeDtypeStruct((b, ML_HEADS, ML_DH, ML_DH), F32),
                   jax.ShapeDtypeStruct((b, ML_HEADS, ML_DH), F32),
                   jax.ShapeDtypeStruct((b, ML_HEADS, LANES), F32),
                   jax.ShapeDtypeStruct((b, SUBLANES, w2), F32)],
        scratch_shapes=[pltpu.VMEM((tl + 2 * SUBLANES, w2), F32)],
        compiler_params=_cparams("parallel", "arbitrary"),
        name="mlstm",
    )(u3, c0.astype(F32), n0.astype(F32), m0b, cv0, cw.astype(F32), cb.astype(F32).reshape(1, w2), gbp,
      ng.astype(F32).reshape(1, ML_DH))
    return o, c_new, n_new, m_new[:, :, 0], cv[:, SUBLANES - (CONV_W - 1):, :]


def _rope_lanes(x, cos_t, sin_t):
    w = x.shape[1]
    half = ROT_DIM // 2
    reps = w // cos_t.shape[1]
    if reps > 1:
        cos_t = jnp.concatenate([cos_t] * reps, axis=1)
        sin_t = jnp.concatenate([sin_t] * reps, axis=1)
    lane = lax.broadcasted_iota(jnp.int32, x.shape, 1) % HEAD_DIM
    partner = jnp.where(lane < half, pltpu.roll(x, w - half, 1), pltpu.roll(x, half, 1))
    return x * cos_t + partner * sin_t


def _nsa_prep_kernel(u_ref, cos_ref, sin_ref, qg_ref, kg_ref, g4_ref, g1_ref, qn_ref, qr_ref, rows_ref, win_ref):
    u = u_ref[...]
    cos_t = cos_ref[...]
    sin_t = sin_ref[...]
    q = _group_norm(u[:, 0:NSA_WIDTH], g4_ref[...], qg_ref[...])
    qn_ref[...] = (q * ATTN_SCALE).astype(BF16)
    qr_ref[...] = (_rope_lanes(q, cos_t, sin_t) * ATTN_SCALE).astype(BF16)
    kv = NSA_WIDTH
    k_slc = _rope_lanes(_group_norm(u[:, kv + 2 * KV_W:kv + 3 * KV_W], g1_ref[...], kg_ref[1:2, :]), cos_t, sin_t)
    k_win = _rope_lanes(_group_norm(u[:, kv + 4 * KV_W:kv + 5 * KV_W], g1_ref[...], kg_ref[2:3, :]), cos_t, sin_t)
    rows_ref[:, 0:2 * KV_W] = u[:, kv:kv + 2 * KV_W]
    rows_ref[:, 2 * KV_W:3 * KV_W] = k_slc
    rows_ref[:, 3 * KV_W:4 * KV_W] = u[:, kv + 3 * KV_W:kv + 4 * KV_W]
    win_ref[:, 0:KV_W] = k_win
    win_ref[:, KV_W:2 * KV_W] = u[:, kv + 5 * KV_W:kv + 6 * KV_W]


def _rope_tables(pos):
    half = ROT_DIM // 2
    inv = jnp.exp(-math.log(ROPE_THETA) * jnp.arange(half, dtype=F32) * 2.0 / ROT_DIM)
    ang = pos.astype(F32)[:, None] * inv[None, :]
    cos, sin = jnp.cos(ang), jnp.sin(ang)
    n = pos.shape[0]
    ones = jnp.ones((n, HEAD_DIM - ROT_DIM), F32)
    cos_h = jnp.concatenate([cos, cos, ones], axis=1)
    sin_h = jnp.concatenate([-sin, sin, 0.0 * ones], axis=1)
    reps = LANES // HEAD_DIM
    return jnp.tile(cos_h, (1, reps)), jnp.tile(sin_h, (1, reps))


def _nsa_prep(u_nsa, pos, q_g, k_g, l):
    rows = u_nsa.shape[0]
    tl = min(l, 256)
    nb = l // tl
    cos_t, sin_t = _rope_tables(pos)
    qg = jnp.tile(q_g.astype(F32).reshape(1, HEAD_DIM), (1, NSA_HEADS))
    kg = jnp.tile(k_g.astype(F32), (1, NSA_KV_HEADS))
    cst = lambda i: (0, 0)
    return pl.pallas_call(
        _nsa_prep_kernel,
        grid=(rows // tl,),
        in_specs=[pl.BlockSpec((tl, W_NSA), lambda i: (i, 0)),
                  pl.BlockSpec((tl, LANES), lambda i: (i % nb, 0)),
                  pl.BlockSpec((tl, LANES), lambda i: (i % nb, 0)),
                  pl.BlockSpec((1, NSA_WIDTH), cst),
                  pl.BlockSpec((3, KV_W), cst),
                  pl.BlockSpec((NSA_WIDTH, NSA_WIDTH), cst),
                  pl.BlockSpec((KV_W, KV_W), cst)],
        out_specs=[pl.BlockSpec((tl, NSA_WIDTH), lambda i: (i, 0)),
                   pl.BlockSpec((tl, NSA_WIDTH), lambda i: (i, 0)),
                   pl.BlockSpec((tl, N_KV_SLOTS * KV_W), lambda i: (i, 0)),
                   pl.BlockSpec((tl, 2 * KV_W), lambda i: (i, 0))],
        out_shape=[jax.ShapeDtypeStruct((rows, NSA_WIDTH), BF16),
                   jax.ShapeDtypeStruct((rows, NSA_WIDTH), BF16),
                   jax.ShapeDtypeStruct((rows, N_KV_SLOTS * KV_W), F32),
                   jax.ShapeDtypeStruct((rows, 2 * KV_W), F32)],
        compiler_params=_cparams("parallel"),
        name="nsa_prep",
    )(u_nsa, cos_t, sin_t, qg, kg, _group_mean_matrix(NSA_WIDTH), _group_mean_matrix(KV_W))


def _cmp_kv_kernel(xk_ref, xv_ref, w_ref, bias_ref, kg_ref, g1_ref, kc_ref, vc_ref, *, n_half):
    acc_k = jnp.zeros((n_half, 2 * KV_W), F32)
    acc_v = jnp.zeros((n_half, 2 * KV_W), F32)
    for s in range(CMP_STRIDE):
        acc_k = acc_k + _dot(xk_ref[0, pl.ds(s, n_half, stride=CMP_STRIDE), :], w_ref[0, s])
        acc_v = acc_v + _dot(xv_ref[0, pl.ds(s, n_half, stride=CMP_STRIDE), :], w_ref[1, s])
    valid = lax.broadcasted_iota(jnp.int32, (n_half, KV_W), 0) < n_half - 1

    def summary(acc, bias):
        return acc[:, 0:KV_W] + pltpu.roll(acc[:, KV_W:2 * KV_W], n_half - 1, 0) + bias

    kc = _group_norm(summary(acc_k, bias_ref[0:1, :]), g1_ref[...], kg_ref[...])
    kc_ref[0] = jnp.where(valid, kc, 0.0).astype(BF16)
    vc_ref[0] = jnp.where(valid, summary(acc_v, bias_ref[1:2, :]), 0.0).astype(BF16)


def _cmp_weights(cmp_pos, cmp_w):
    wf = cmp_w.astype(F32)
    eye_g = jnp.eye(NSA_KV_HEADS, dtype=F32)

    def bd(w):
        return jnp.einsum('ksde,gh->ksgdhe', w, eye_g).reshape(2, CMP_STRIDE, KV_W, KV_W)

    wcat = jnp.concatenate([bd(wf[:, :CMP_STRIDE]), bd(wf[:, CMP_STRIDE:])], axis=3).astype(BF16)
    bias = jnp.einsum('ksd,ksde->ke', cmp_pos.astype(F32), wf)
    return wcat, jnp.tile(bias, (1, NSA_KV_HEADS))


def _cmp_kv(x3, k_blk, v_blk, t_use, wcat, bias, kg0):
    b = x3.shape[0]
    n_half = t_use // CMP_STRIDE
    kern = functools.partial(_cmp_kv_kernel, n_half=n_half)
    kg = jnp.tile(kg0.astype(F32).reshape(1, HEAD_DIM), (1, NSA_KV_HEADS))
    return pl.pallas_call(
        kern,
        grid=(b,),
        in_specs=[pl.BlockSpec((1, t_use, KV_W), lambda bi: (bi, 0, k_blk)),
                  pl.BlockSpec((1, t_use, KV_W), lambda bi: (bi, 0, v_blk)),
                  pl.BlockSpec((2, CMP_STRIDE, KV_W, 2 * KV_W), lambda bi: (0, 0, 0, 0)),
                  pl.BlockSpec((2, KV_W), lambda bi: (0, 0)),
                  pl.BlockSpec((1, KV_W), lambda bi: (0, 0)),
                  pl.BlockSpec((KV_W, KV_W), lambda bi: (0, 0))],
        out_specs=[pl.BlockSpec((1, n_half, KV_W), lambda bi: (bi, 0, 0)),
                   pl.BlockSpec((1, n_half, KV_W), lambda bi: (bi, 0, 0))],
        out_shape=[jax.ShapeDtypeStruct((b, n_half, KV_W), BF16),
                   jax.ShapeDtypeStruct((b, n_half, KV_W), BF16)],
        compiler_params=_cparams("parallel"),
        name="cmp_kv",
    )(x3, x3, wcat, bias, kg, _group_mean_matrix(KV_W))


def _cmp_attn_kernel(qn_ref, kc_ref, vc_ref, ov_ref, o_ref, sel_ref, *, tq, n_half, n_cmp, n_sel, nbp, pos0):
    qi = pl.program_id(1)
    pos_c = pos0 + qi * tq + lax.broadcasted_iota(jnp.int32, (tq, 1), 0)
    ncol = lax.broadcasted_iota(jnp.int32, (1, n_half), 1)
    cmask = (ncol * CMP_STRIDE + (CMP_BLOCK - 1) <= pos_c) & (ncol < n_cmp)
    pos_r = pos0 + qi * tq + lax.broadcasted_iota(jnp.int32, (1, tq), 1)
    blk = lax.broadcasted_iota(jnp.int32, (nbp, 1), 0)
    cur = pos_r // SEL_BLOCK
    forced = (blk == 0) | (blk == cur) | (blk == cur - 1)
    valid = blk * SEL_BLOCK <= pos_r
    real = blk < n_sel
    for g in range(NSA_KV_HEADS):
        gs = slice(g * HEAD_DIM, (g + 1) * HEAD_DIM)
        kc = kc_ref[0, :, gs]
        vc = vc_ref[0, :, gs]
        psum = jnp.zeros((tq, n_half), F32)
        for h in range(NSA_HPG):
            hs = slice((g * NSA_HPG + h) * HEAD_DIM, (g * NSA_HPG + h + 1) * HEAD_DIM)
            s = _dot_nt(qn_ref[0, :, hs], kc)
            m = jnp.max(jnp.where(cmask, s, NEG_BIG), axis=-1, keepdims=True)
            m = jnp.where(m > 0.5 * NEG_BIG, m, 0.0)
            e = jnp.where(cmask, jnp.exp(s - m), 0.0)
            p = e / jnp.maximum(jnp.sum(e, axis=-1, keepdims=True), 1e-30)
            o_ref[0, :, hs] = _dot(p, vc)
            psum = psum + p
        p_hi = psum.astype(BF16)
        p_lo = (psum - p_hi.astype(F32)).astype(BF16)
        ov = ov_ref[...]
        imp = _dot_nt(ov, p_hi) + _dot_nt(ov, p_lo)
        score = jnp.where(forced, 3e38, jnp.where(valid, imp, -1e38))
        score = jnp.where(real, score, -3e38)
        cnt = jnp.zeros((nbp, tq), F32)
        for jp in range(n_sel):
            rowv = score[jp:jp + 1, :]
            beats = (rowv > score) | ((rowv == score) & (blk > jp))
            cnt = cnt + jnp.where(beats, 1.0, 0.0)
        sel = (cnt < float(min(N_SELECT, n_sel))) & real
        sel_ref[0, g * nbp:(g + 1) * nbp, :] = jnp.where(sel, 1.0, 0.0).astype(BF16)


def _overlap_t(n_cmp, n_sel, n_half, nbp):
    a = SEL_BLOCK // CMP_STRIDE
    bb = CMP_BLOCK // CMP_STRIDE
    i = np.arange(n_half)[None, :]
    j = np.arange(nbp)[:, None]
    s = i - a * j + (bb - 1)
    cnt = np.maximum(np.minimum(np.minimum(s + 1, a + bb - 1 - s), min(a, bb)), 0)
    cnt = np.where((i < n_cmp) & (j < n_sel), cnt, 0)
    return jnp.asarray(cnt, dtype=BF16)


def _cmp_attn(qn3, kc, vc, pos0, t_len):
    b, l, _ = qn3.shape
    n_half = kc.shape[1]
    n_cmp = n_half - 1
    n_sel = -(-t_len // SEL_BLOCK)
    nbp = _round_up(n_sel, 16)
    tq = min(l, 128)
    kern = functools.partial(_cmp_attn_kernel, tq=tq, n_half=n_half, n_cmp=n_cmp, n_sel=n_sel, nbp=nbp, pos0=pos0)
    return pl.pallas_call(
        kern,
        grid=(b, l // tq),
        in_specs=[pl.BlockSpec((1, tq, NSA_WIDTH), lambda bi, qi: (bi, qi, 0)),
                  pl.BlockSpec((1, n_half, KV_W), lambda bi, qi: (bi, 0, 0)),
                  pl.BlockSpec((1, n_half, KV_W), lambda bi, qi: (bi, 0, 0)),
                  pl.BlockSpec((nbp, n_half), lambda bi, qi: (0, 0))],
        out_specs=[pl.BlockSpec((1, tq, NSA_WIDTH), lambda bi, qi: (bi, qi, 0)),
                   pl.BlockSpec((1, NSA_KV_HEADS * nbp, tq), lambda bi, qi: (bi, 0, qi))],
        out_shape=[jax.ShapeDtypeStruct((b, l, NSA_WIDTH), F32),
                   jax.ShapeDtypeStruct((b, NSA_KV_HEADS * nbp, l), BF16)],
        compiler_params=_cparams("parallel", "parallel"),
        name="cmp_attn",
    )(qn3, kc, vc, _overlap_t(n_cmp, n_sel, n_half, nbp))


ATT_TQ = 128
ATT_TK = 512


def _win_blocks(l, tq, tk):
    return max((q0 + tq - 1) // tk - max(q0 - WINDOW + 1, 0) // tk + 1 for q0 in range(0, l, tq))


def _flash_kernel(*refs, mode, tq, tk, nkv):
    if mode == "sel":
        q_ref, k_ref, v_ref, sel_ref, o_ref, m_sc, l_sc, acc_sc = refs
    else:
        q_ref, k_ref, v_ref, o_ref, m_sc, l_sc, acc_sc = refs
    qi = pl.program_id(1)
    ki = pl.program_id(2)
    kmax = (qi * tq + tq - 1) // tk
    kb = ki if mode == "sel" else kmax - (nkv - 1) + ki
    active = (ki <= kmax) if mode == "sel" else (kb >= 0)
    last = (ki == kmax) if mode == "sel" else (ki == nkv - 1)

    @pl.when(ki == 0)
    def _init():
        m_sc[...] = jnp.full(m_sc.shape, NEG_BIG, F32)
        l_sc[...] = jnp.zeros(l_sc.shape, F32)
        acc_sc[...] = jnp.zeros(acc_sc.shape, F32)

    @pl.when(active)
    def _step():
        qpos = qi * tq + lax.broadcasted_iota(jnp.int32, (tq, tk), 0)
        kpos = kb * tk + lax.broadcasted_iota(jnp.int32, (tq, tk), 1)
        causal = kpos <= qpos
        if mode == "sel":
            n_blk = sel_ref.shape[-1] // NSA_KV_HEADS
            e_row = lax.broadcasted_iota(jnp.int32, (n_blk, tk), 0)
            e_col = lax.broadcasted_iota(jnp.int32, (n_blk, tk), 1)
            expand = jnp.where(e_row == kb * (tk // SEL_BLOCK) + e_col // SEL_BLOCK, 1.0, 0.0).astype(BF16)
        else:
            mask = causal & (kpos > qpos - WINDOW)
        for g in range(NSA_KV_HEADS):
            gs = slice(g * HEAD_DIM, (g + 1) * HEAD_DIM)
            if mode == "sel":
                picked = jnp.dot(sel_ref[0, :, g * n_blk:(g + 1) * n_blk], expand, preferred_element_type=F32)
                mask = jnp.where(causal, picked, 0.0) > 0.5
            kg = k_ref[0, :, gs].astype(BF16)
            vg = v_ref[0, :, gs].astype(BF16)
            for h in range(NSA_HPG):
                hh = g * NSA_HPG + h
                qh = q_ref[0, :, hh * HEAD_DIM:(hh + 1) * HEAD_DIM]
                s = lax.dot_general(qh, kg, (((1,), (1,)), ((), ())), preferred_element_type=F32)
                s = jnp.where(mask, s, NEG_BIG)
                m_prev = m_sc[hh]
                m_new = jnp.maximum(m_prev, jnp.max(s, axis=-1, keepdims=True))
                alpha = jnp.exp(m_prev - m_new)
                p = jnp.exp(s - m_new[:, :1])
                l_sc[hh] = alpha * l_sc[hh] + jnp.sum(p, axis=-1, keepdims=True)
                m_sc[hh] = m_new
                pv = jnp.dot(p.astype(BF16), vg, preferred_element_type=F32)
                acc_sc[hh] = alpha[:, :HEAD_DIM] * acc_sc[hh] + pv

    @pl.when(last)
    def _fin():
        for hh in range(NSA_HEADS):
            o_ref[0, :, hh * HEAD_DIM:(hh + 1) * HEAD_DIM] = acc_sc[hh] / l_sc[hh][:, :HEAD_DIM]


def _flash(mode, q3, kv3, k_blk, v_blk, sel=None):
    b, l, _ = q3.shape
    tq, tk = min(ATT_TQ, l), min(ATT_TK, l)
    nq = l // tq
    nkv = l // tk if mode == "sel" else _win_blocks(l, tq, tk)

    def kb_of(qi, ki):
        kmax = (qi * tq + tq - 1) // tk
        if mode == "sel":
            return jnp.minimum(ki, kmax)
        return jnp.maximum(kmax - (nkv - 1) + ki, 0)

    in_specs = [pl.BlockSpec((1, tq, NSA_WIDTH), lambda bi, qi, ki: (bi, qi, 0)),
                pl.BlockSpec((1, tk, KV_W), lambda bi, qi, ki: (bi, kb_of(qi, ki), k_blk)),
                pl.BlockSpec((1, tk, KV_W), lambda bi, qi, ki: (bi, kb_of(qi, ki), v_blk))]
    args = [q3, kv3, kv3]
    if mode == "sel":
        in_specs.append(pl.BlockSpec((1, tq, sel.shape[-1]), lambda bi, qi, ki: (bi, qi, 0)))
        args.append(sel)
    kern = functools.partial(_flash_kernel, mode=mode, tq=tq, tk=tk, nkv=nkv)
    return pl.pallas_call(
        kern,
        grid=(b, nq, nkv),
        in_specs=in_specs,
        out_specs=pl.BlockSpec((1, tq, NSA_WIDTH), lambda bi, qi, ki: (bi, qi, 0)),
        out_shape=jax.ShapeDtypeStruct((b, l, NSA_WIDTH), F32),
        scratch_shapes=[pltpu.VMEM((NSA_HEADS, tq, LANES), F32),
                        pltpu.VMEM((NSA_HEADS, tq, LANES), F32),
                        pltpu.VMEM((NSA_HEADS, tq, HEAD_DIM), F32)],
        compiler_params=_cparams("parallel", "parallel", "arbitrary"),
        name="flash_" + mode,
    )(*args)


def _nsa_gate_kernel(oc_ref, os_ref, ow_ref, gz_ref, e_ref, o_ref):
    gate = _sigmoid(gz_ref[:, 0:LANES])
    o = (_dot_f32(gate, e_ref[0]) * oc_ref[...] + _dot_f32(gate, e_ref[1]) * os_ref[...]
         + _dot_f32(gate, e_ref[2]) * ow_ref[...])
    o_ref[...] = o * _silu(gz_ref[:, LANES:LANES + NSA_WIDTH])


def _gate_expand():
    e = np.zeros((3, LANES, NSA_WIDTH), np.float32)
    for j in range(3):
        for h in range(NSA_HEADS):
            e[j, j * NSA_HEADS + h, h * HEAD_DIM:(h + 1) * HEAD_DIM] = 1.0
    return jnp.asarray(e)


def _nsa_gate(o_cmp, o_slc, o_win, u_nsa, tm):
    rows = o_cmp.shape[0]
    gz_w = LANES + NSA_WIDTH
    gz_blk = (NSA_WIDTH + 6 * KV_W) // gz_w
    assert gz_blk * gz_w == NSA_WIDTH + 6 * KV_W
    row = lambda i: (i, 0)
    return pl.pallas_call(
        _nsa_gate_kernel,
        grid=(rows // tm,),
        in_specs=[pl.BlockSpec((tm, NSA_WIDTH), row),
                  pl.BlockSpec((tm, NSA_WIDTH), row),
                  pl.BlockSpec((tm, NSA_WIDTH), row),
                  pl.BlockSpec((tm, gz_w), lambda i: (i, gz_blk)),
                  pl.BlockSpec((3, LANES, NSA_WIDTH), lambda i: (0, 0, 0))],
        out_specs=pl.BlockSpec((tm, NSA_WIDTH), row),
        out_shape=jax.ShapeDtypeStruct((rows, NSA_WIDTH), F32),
        compiler_params=_cparams("parallel"),
        name="nsa_gate",
    )(o_cmp, o_slc, o_win, u_nsa, _gate_expand())


def _rms_norm(x, g, eps=EPS):
    xf = x.astype(F32)
    y = xf * lax.rsqrt(jnp.mean(xf * xf, axis=-1, keepdims=True) + eps)
    return (y * g.astype(F32)).astype(x.dtype)


def _rope(x, pos):
    half = ROT_DIM // 2
    inv = jnp.exp(-math.log(ROPE_THETA) * jnp.arange(half, dtype=F32) * 2.0 / ROT_DIM)
    ang = pos.astype(F32)[:, None] * inv[None, :]
    cos = jnp.cos(ang)[:, None, :]
    sin = jnp.sin(ang)[:, None, :]
    xf = x.astype(F32)
    x1, x2, rest = xf[..., :half], xf[..., half:ROT_DIM], xf[..., ROT_DIM:]
    return jnp.concatenate([x1 * cos - x2 * sin, x2 * cos + x1 * sin, rest], axis=-1).astype(x.dtype)


def _masked_softmax(s, mask):
    s = jnp.where(mask, s, -jnp.inf)
    m = jnp.max(s, axis=-1, keepdims=True)
    m = jnp.where(jnp.isfinite(m), m, 0.0)
    e = jnp.where(mask, jnp.exp(s - m), 0.0)
    return e / jnp.maximum(jnp.sum(e, axis=-1, keepdims=True), 1e-30)


def _cmp_summaries(x, pe, w):
    b, t, g, d = x.shape
    n_half = t // CMP_STRIDE
    halves = x[:, :n_half * CMP_STRIDE].astype(F32).reshape(b, n_half, CMP_STRIDE, g, d)
    wf = w.astype(F32)
    first = jnp.einsum('bnsgd,sde->bnge', halves[:, :-1], wf[:CMP_STRIDE])
    second = jnp.einsum('bnsgd,sde->bnge', halves[:, 1:], wf[CMP_STRIDE:])
    bias = jnp.einsum('sd,sde->e', pe.astype(F32), wf)
    return first + second + bias


def _overlap_matrix(n_cmp, n_sel):
    a = SEL_BLOCK // CMP_STRIDE
    bb = CMP_BLOCK // CMP_STRIDE
    i = jnp.arange(n_cmp)[:, None]
    j = jnp.arange(n_sel)[None, :]
    s = i - a * j + (bb - 1)
    cnt = jnp.minimum(jnp.minimum(s + 1, a + bb - 1 - s), min(a, bb))
    return jnp.maximum(cnt, 0).astype(F32)


def _nsa_selected_gather(q, k_hist, v_hist, idx, pos):
    b, l = q.shape[:2]
    t_len = k_hist.shape[1]
    n_sel = -(-t_len // SEL_BLOCK)
    n_k = idx.shape[-1]
    pad = n_sel * SEL_BLOCK - t_len

    def blocks(x):
        x = jnp.pad(x.astype(F32), ((0, 0), (0, pad), (0, 0), (0, 0)))
        return x.reshape(b, n_sel, SEL_BLOCK, NSA_KV_HEADS, HEAD_DIM).transpose(0, 3, 1, 2, 4)

    kb, vb = blocks(k_hist), blocks(v_hist)
    qb = math.gcd(l, Q_BLOCK)
    nq = l // qb
    qx = q.astype(F32).reshape(b * nq, qb, NSA_KV_HEADS, NSA_HPG, HEAD_DIM)
    ix = idx.reshape(b, NSA_KV_HEADS, nq, qb, n_k).transpose(0, 2, 1, 3, 4).reshape(b * nq, NSA_KV_HEADS, qb, n_k)
    px = jnp.tile(pos.reshape(nq, qb), (b, 1))
    bx = jnp.repeat(jnp.arange(b), nq)
    g_ix = jnp.arange(NSA_KV_HEADS)[:, None, None]
    offs = jnp.arange(SEL_BLOCK)
    n_keys = n_k * SEL_BLOCK

    def attend(args):
        qi, ii, pi, bi = args
        ks = kb[bi][g_ix, ii]
        vs = vb[bi][g_ix, ii]
        kpos = ii[..., None] * SEL_BLOCK + offs
        mask = (kpos <= pi[None, :, None, None]).reshape(NSA_KV_HEADS, 1, qb, n_keys)
        s = jnp.einsum('qghd,gqkjd->ghqkj', qi, ks).reshape(NSA_KV_HEADS, NSA_HPG, qb, n_keys) * ATTN_SCALE
        p = _masked_softmax(s, mask)
        return jnp.einsum('ghqn,gqnd->qghd', p, vs.reshape(NSA_KV_HEADS, qb, n_keys, HEAD_DIM))

    o = lax.map(attend, (qx, ix, px, bx))
    return o.reshape(b, l, NSA_HEADS, HEAD_DIM)


def _nsa_window(q, k_ctx, v_ctx, pos0):
    b, l = q.shape[:2]
    lc = k_ctx.shape[1]
    padw = ((0, 0), (WINDOW, 0), (0, 0), (0, 0))
    kp = jnp.pad(k_ctx.astype(F32), padw)
    vp = jnp.pad(v_ctx.astype(F32), padw)
    qb = math.gcd(l, Q_BLOCK)
    nq = l // qb
    span = WINDOW + qb
    qx = jnp.moveaxis(q.astype(F32).reshape(b, nq, qb, NSA_KV_HEADS, NSA_HPG, HEAD_DIM), 1, 0)
    starts = jnp.arange(nq) * qb
    first_pos = pos0 + l - lc

    def attend(args):
        qi, i0 = args
        kk = lax.dynamic_slice_in_dim(kp, i0 + lc - l, span, axis=1)
        vv = lax.dynamic_slice_in_dim(vp, i0 + lc - l, span, axis=1)
        kpos = pos0 + i0 - WINDOW + jnp.arange(span)
        qpos = pos0 + i0 + jnp.arange(qb)
        mask = ((kpos[None, :] >= first_pos) & (kpos[None, :] <= qpos[:, None])
                & (kpos[None, :] > qpos[:, None] - WINDOW))
        s = jnp.einsum('bqghd,bkgd->bghqk', qi, kk) * ATTN_SCALE
        p = _masked_softmax(s, mask)
        return jnp.einsum('bghqk,bkgd->bqghd', p, vv)

    o = lax.map(attend, (qx, starts))
    return jnp.moveaxis(o, 0, 1).reshape(b, l, NSA_HEADS, HEAD_DIM)


def _nsa_mixer_jax(n_q, n_kv, n_g, past_kv, win_buf, win_keep, pos0, q_g, k_g, cmp_pos, cmp_w):
    b, l, _ = n_q.shape
    pos = pos0 + jnp.arange(l)
    q = _rms_norm(n_q.reshape(b, l, NSA_HEADS, HEAD_DIM), q_g)
    q_rot = _rope(q, pos)
    kv = n_kv.reshape(b, l, 6, NSA_KV_HEADS, HEAD_DIM)
    k_slc = _rope(_rms_norm(kv[:, :, 2], k_g[1]), pos)
    k_win = _rope(_rms_norm(kv[:, :, 4], k_g[2]), pos)
    new_rows = jnp.stack([kv[:, :, 0], kv[:, :, 1], k_slc, kv[:, :, 3]], axis=2)
    new_win = jnp.stack([k_win, kv[:, :, 5]], axis=2)
    hist = jnp.concatenate([past_kv.astype(new_rows.dtype), new_rows], axis=1)
    ctx = jnp.concatenate([win_buf.astype(new_win.dtype), new_win], axis=1)
    t_len = hist.shape[1]

    kc = _rms_norm(_cmp_summaries(hist[:, :, 0], cmp_pos[0], cmp_w[0]), k_g[0])
    vc = _cmp_summaries(hist[:, :, 1], cmp_pos[1], cmp_w[1])
    n_cmp = kc.shape[1]
    qg = q.astype(F32).reshape(b, l, NSA_KV_HEADS, NSA_HPG, HEAD_DIM)
    s = jnp.einsum('blghd,bngd->bghln', qg, kc) * ATTN_SCALE
    cmp_end = jnp.arange(n_cmp) * CMP_STRIDE + CMP_BLOCK - 1
    p_cmp = _masked_softmax(s, cmp_end[None, :] <= pos[:, None])
    o_cmp = jnp.einsum('bghln,bngd->blghd', p_cmp, vc).reshape(b, l, NSA_HEADS, HEAD_DIM)

    n_sel = -(-t_len // SEL_BLOCK)
    imp = jnp.einsum('bghln,nj->bglj', p_cmp, _overlap_matrix(n_cmp, n_sel))
    blk = jnp.arange(n_sel)[None, :]
    cur = (pos // SEL_BLOCK)[:, None]
    valid = blk * SEL_BLOCK <= pos[:, None]
    forced = (blk == 0) | (blk == cur) | (blk == cur - 1)
    score = jnp.where(forced, jnp.inf, jnp.where(valid, imp, -jnp.inf))
    _, idx = lax.top_k(score, min(N_SELECT, n_sel))
    o_slc = _nsa_selected_gather(q_rot, hist[:, :, 2], hist[:, :, 3], idx, pos)
    o_win = _nsa_window(q_rot, ctx[:, :, 0], ctx[:, :, 1], pos0)

    gate = jax.nn.sigmoid(n_g.astype(F32)).reshape(b, l, 3, NSA_HEADS, 1)
    o = gate[:, :, 0] * o_cmp + gate[:, :, 1] * o_slc + gate[:, :, 2] * o_win
    return o.reshape(b, l, NSA_WIDTH), new_rows, ctx[:, ctx.shape[1] - win_keep:]


def _nsa_fresh(u_nsa, b, l, win_keep, q_g, k_g, cmp_pos, cmp_w, tm):
    qn, qr, rows, win = _nsa_prep(u_nsa, jnp.arange(l), q_g, k_g, l)
    rows3 = rows.reshape(b, l, N_KV_SLOTS * KV_W)
    win3 = win.reshape(b, l, 2 * KV_W)
    wcat, bias = _cmp_weights(cmp_pos, cmp_w)
    t_use = (l // CMP_STRIDE) * CMP_STRIDE
    kc, vc = _cmp_kv(rows3, 0, 1, t_use, wcat, bias, k_g[0])
    o_cmp, sel_t = _cmp_attn(qn.reshape(b, l, NSA_WIDTH), kc, vc, 0, l)
    sel = jnp.swapaxes(sel_t, 1, 2)
    qr3 = qr.reshape(b, l, NSA_WIDTH)
    o_slc = _flash("sel", qr3, rows3, 2, 3, sel)
    o_win = _flash("win", qr3, win3, 0, 1)
    o_b = _nsa_gate(o_cmp.reshape(b * l, NSA_WIDTH), o_slc.reshape(b * l, NSA_WIDTH),
                    o_win.reshape(b * l, NSA_WIDTH), u_nsa, tm)
    new_rows = rows3.reshape(b, l, N_KV_SLOTS, NSA_KV_HEADS, HEAD_DIM)
    new_win = win3[:, l - win_keep:].reshape(b, win_keep, 2, NSA_KV_HEADS, HEAD_DIM)
    return o_b, new_rows, new_win


def _hybrid_layer(x, pos0, past_kv, win_buf, win_keep, s_gla, c_ml, n_ml, m_ml, conv_ml,
                  norm_g, w_in_pad, w_out_bf, gla_w_gate, gla_b_gate, gla_norm_g,
                  nsa_q_norm_g, nsa_k_norm_g, nsa_cmp_pos, nsa_cmp_w,
                  ml_conv_w, ml_conv_b, ml_gate_b, ml_norm_g):
    b, l, _ = x.shape
    rows = b * l
    tm = 256
    x2d = x.reshape(rows, D_MODEL)
    u_gla, u_nsa, u_ml = _proj_in(x2d, norm_g, w_in_pad, tm)

    nk = GLA_HEADS * GLA_DK
    wg = jnp.pad(gla_w_gate.astype(F32), ((0, LANES - GLA_RANK), (0, 0))).astype(BF16)
    o_a, st_new = _gla(u_gla, _gla_state_in(s_gla), wg, gla_b_gate.astype(F32).reshape(1, nk),
                       jnp.tile(gla_norm_g.astype(F32).reshape(1, GLA_DV), (1, GLA_HEADS)), b, l)
    s_new = _gla_state_out(st_new)

    if past_kv is None:
        o_b, new_rows, new_win = _nsa_fresh(u_nsa, b, l, win_keep, nsa_q_norm_g, nsa_k_norm_g,
                                            nsa_cmp_pos, nsa_cmp_w, tm)
    else:
        un = u_nsa.reshape(b, l, W_NSA)
        n_q = un[..., 0:NSA_WIDTH]
        n_kv = un[..., NSA_WIDTH:NSA_WIDTH + 6 * KV_W]
        n_g = un[..., NSA_WIDTH + 6 * KV_W:NSA_WIDTH + 6 * KV_W + 3 * NSA_HEADS]
        n_z = un[..., NSA_WIDTH + 6 * KV_W + LANES:]
        o_b, new_rows, new_win = _nsa_mixer_jax(n_q, n_kv, n_g, past_kv, win_buf, win_keep, pos0,
                                                nsa_q_norm_g, nsa_k_norm_g, nsa_cmp_pos, nsa_cmp_w)
        o_b = (o_b * jax.nn.silu(n_z.astype(F32))).reshape(rows, NSA_WIDTH)

    o_c, c_new, n_new, m_new, conv_new = _mlstm(u_ml, c_ml, n_ml, m_ml, conv_ml, ml_conv_w, ml_conv_b,
                                                ml_gate_b, ml_norm_g, b, l)

    y = _proj_out(o_a.reshape(rows, GLA_WIDTH), o_b, o_c.reshape(rows, ML_WIDTH), x2d, w_out_bf, tm)
    return y.reshape(b, l, D_MODEL), new_rows, new_win, s_new, c_new, n_new, m_new, conv_new


def kernel(x_prompt, x_sample, cache_nsa_kv, state_nsa_win, state_gla, state_mlstm_C, state_mlstm_n,
           state_mlstm_m, state_mlstm_conv, page_table, norm_g, w_in, w_out, gla_w_gate, gla_b_gate,
           gla_norm_g, nsa_q_norm_g, nsa_k_norm_g, nsa_cmp_pos, nsa_cmp_w, ml_conv_w, ml_conv_b,
           ml_gate_b, ml_norm_g):
    bp, sp, _ = x_prompt.shape
    bs, _, _ = x_sample.shape
    depth = w_in.shape[0]
    n_pages = page_table.shape[1]
    past_len = n_pages * cache_nsa_kv.shape[2]
    dt = x_prompt.dtype
    zero_gla = jnp.zeros((bp, GLA_HEADS, GLA_DK, GLA_DV), F32)
    zero_c = jnp.zeros((bp, ML_HEADS, ML_DH, ML_DH), F32)
    zero_n = jnp.zeros((bp, ML_HEADS, ML_DH), F32)
    zero_m = jnp.zeros((bp, ML_HEADS), F32)
    zero_conv = jnp.zeros((bp, CONV_W - 1, 2 * ML_WIDTH), dt)
    keep_p = min(WINDOW, sp)
    keep_s = state_nsa_win.shape[2]

    y_prompt, y_sample = x_prompt, x_sample
    p_layers, s_layers = [], []
    for layer in range(depth):
        w = (norm_g[layer], _pad_w_in(w_in[layer]), w_out[layer].astype(BF16), gla_w_gate[layer],
             gla_b_gate[layer], gla_norm_g[layer],
             nsa_q_norm_g[layer], nsa_k_norm_g[layer], nsa_cmp_pos[layer], nsa_cmp_w[layer],
             ml_conv_w[layer], ml_conv_b[layer], ml_gate_b[layer], ml_norm_g[layer])
        y_prompt, *p_new = _hybrid_layer(y_prompt, 0, None, None, keep_p, zero_gla, zero_c, zero_n,
                                         zero_m, zero_conv, *w)
        past = cache_nsa_kv[layer][page_table].reshape(bs, past_len, N_KV_SLOTS, NSA_KV_HEADS, HEAD_DIM)
        y_sample, *s_new = _hybrid_layer(y_sample, past_len, past, state_nsa_win[layer], keep_s,
                                         state_gla[layer], state_mlstm_C[layer], state_mlstm_n[layer],
                                         state_mlstm_m[layer], state_mlstm_conv[layer], *w)
        p_layers.append(p_new)
        s_layers.append(s_new)
    p_kv, p_win, p_gla, p_c, p_n, p_m, p_conv = [jnp.stack(z) for z in zip(*p_layers)]
    s_kv, s_win, s_gla, s_c, s_n, s_m, s_conv = [jnp.stack(z) for z in zip(*s_layers)]
    return (y_prompt, y_sample, p_kv, s_kv, p_win, s_win, p_gla, s_gla, p_c, s_c, p_n, s_n, p_m, s_m, p_conv, s_conv)
```

```python
import functools
import math

import jax
import jax.numpy as jnp
import numpy as np
from jax import lax
from jax.experimental import pallas as pl
from jax.experimental.pallas import tpu as pltpu

F32 = jnp.float32
BF16 = jnp.bfloat16
HIGHEST = lax.Precision.HIGHEST

D_MODEL = 1024
HEAD_DIM = 64
GLA_WIDTH = D_MODEL // 4
NSA_WIDTH = D_MODEL // 2
ML_WIDTH = D_MODEL - GLA_WIDTH - NSA_WIDTH
D_MIX = GLA_WIDTH + NSA_WIDTH + ML_WIDTH

GLA_HEADS = GLA_WIDTH // HEAD_DIM
GLA_DK = HEAD_DIM // 2
GLA_DV = HEAD_DIM
GLA_RANK = 16
GLA_TAU = 16.0
GLA_CHUNK = 64

NSA_HEADS = NSA_WIDTH // HEAD_DIM
NSA_KV_HEADS = 2
NSA_HPG = NSA_HEADS // NSA_KV_HEADS
CMP_BLOCK = 32
CMP_STRIDE = 16
SEL_BLOCK = 64
N_SELECT = 16
WINDOW = 512
Q_BLOCK = 128
N_KV_SLOTS = 4
ROT_DIM = HEAD_DIM // 4
ROPE_THETA = 500000.0
ATTN_SCALE = HEAD_DIM ** -0.5

ML_HEADS = ML_WIDTH // HEAD_DIM
ML_DH = HEAD_DIM
ML_CHUNK = 64
CONV_W = 4

SPLIT_SIZES = (GLA_HEADS * GLA_DK, GLA_HEADS * GLA_DK, GLA_WIDTH, GLA_RANK, GLA_WIDTH,
               NSA_WIDTH, 6 * NSA_KV_HEADS * HEAD_DIM, 3 * NSA_HEADS, NSA_WIDTH,
               2 * ML_WIDTH, ML_WIDTH, 2 * ML_HEADS, ML_WIDTH, ML_WIDTH)

LANES = 128
SUBLANES = 8
VMEM_LIMIT = 56 * 1024 * 1024
NEG_BIG = -1e30
EPS = 1e-6


def _round_up(n, m):
    return -(-n // m) * m


PAD_SIZES = tuple(_round_up(s, LANES) for s in SPLIT_SIZES)
D_IN_PAD = sum(PAD_SIZES)
W_GLA = sum(PAD_SIZES[0:5])
W_NSA = sum(PAD_SIZES[5:9])
W_ML = sum(PAD_SIZES[9:14])
KV_W = NSA_KV_HEADS * HEAD_DIM


def _dot(a, b):
    return jnp.dot(a.astype(BF16), b.astype(BF16), preferred_element_type=F32)


def _dot_nt(a, b):
    return lax.dot_general(a.astype(BF16), b.astype(BF16), (((1,), (1,)), ((), ())), preferred_element_type=F32)


def _dot_tn(a, b):
    return lax.dot_general(a.astype(BF16), b.astype(BF16), (((0,), (0,)), ((), ())), preferred_element_type=F32)


def _dot_f32(a, b):
    return jnp.dot(a, b, precision=HIGHEST, preferred_element_type=F32)


def _dot_nt_f32(a, b):
    return lax.dot_general(a, b, (((1,), (1,)), ((), ())), precision=HIGHEST, preferred_element_type=F32)


def _log_sigmoid(x):
    return jnp.minimum(x, 0.0) - jnp.log1p(jnp.exp(-jnp.abs(x)))


def _sigmoid(x):
    return 1.0 / (1.0 + jnp.exp(-x))


def _silu(x):
    return x * _sigmoid(x)


def _group_mean_matrix(width):
    g = np.kron(np.eye(width // HEAD_DIM, dtype=np.float32), np.full((HEAD_DIM, HEAD_DIM), 1.0 / HEAD_DIM, np.float32))
    return jnp.asarray(g)


def _group_norm(x, gmat, gain):
    ms = _dot_f32(x * x, gmat)
    return x * lax.rsqrt(ms + EPS) * gain


def _cparams(*sem):
    return pltpu.CompilerParams(dimension_semantics=sem, vmem_limit_bytes=VMEM_LIMIT)


def _proj_in_kernel(x_ref, g_ref, w_ref, ug_ref, un_ref, um_ref):
    x = x_ref[...]
    y = x * lax.rsqrt(jnp.mean(x * x, axis=-1, keepdims=True) + EPS) * g_ref[...]
    r = jnp.dot(y.astype(BF16), w_ref[...], preferred_element_type=F32)
    ug_ref[...] = r[:, 0:W_GLA]
    un_ref[...] = r[:, W_GLA:W_GLA + W_NSA]
    um_ref[...] = r[:, W_GLA + W_NSA:D_IN_PAD]


def _proj_in(x2d, g, w_pad, tm):
    rows = x2d.shape[0]
    return pl.pallas_call(
        _proj_in_kernel,
        grid=(rows // tm,),
        in_specs=[pl.BlockSpec((tm, D_MODEL), lambda i: (i, 0)),
                  pl.BlockSpec((1, D_MODEL), lambda i: (0, 0)),
                  pl.BlockSpec((D_MODEL, D_IN_PAD), lambda i: (0, 0))],
        out_specs=[pl.BlockSpec((tm, W_GLA), lambda i: (i, 0)),
                   pl.BlockSpec((tm, W_NSA), lambda i: (i, 0)),
                   pl.BlockSpec((tm, W_ML), lambda i: (i, 0))],
        out_shape=[jax.ShapeDtypeStruct((rows, W_GLA), F32),
                   jax.ShapeDtypeStruct((rows, W_NSA), F32),
                   jax.ShapeDtypeStruct((rows, W_ML), F32)],
        compiler_params=_cparams("parallel"),
        name="proj_in",
    )(x2d, g.reshape(1, D_MODEL), w_pad)


def _pad_w_in(w_in):
    parts = []
    off = 0
    for s, p in zip(SPLIT_SIZES, PAD_SIZES):
        seg = w_in[:, off:off + s]
        if p != s:
            seg = jnp.pad(seg, ((0, 0), (0, p - s)))
        parts.append(seg)
        off += s
    return jnp.concatenate(parts, axis=1).astype(BF16)


def _proj_out_kernel(oa_ref, ob_ref, oc_ref, x_ref, w_ref, y_ref):
    y = x_ref[...]
    y = y + _dot(oa_ref[...], w_ref[0:GLA_WIDTH, :])
    y = y + _dot(ob_ref[...], w_ref[GLA_WIDTH:GLA_WIDTH + NSA_WIDTH, :])
    y = y + _dot(oc_ref[...], w_ref[GLA_WIDTH + NSA_WIDTH:D_MIX, :])
    y_ref[...] = y


def _proj_out(oa, ob, oc, x2d, w_bf, tm):
    rows = x2d.shape[0]
    return pl.pallas_call(
        _proj_out_kernel,
        grid=(rows // tm,),
        in_specs=[pl.BlockSpec((tm, GLA_WIDTH), lambda i: (i, 0)),
                  pl.BlockSpec((tm, NSA_WIDTH), lambda i: (i, 0)),
                  pl.BlockSpec((tm, ML_WIDTH), lambda i: (i, 0)),
                  pl.BlockSpec((tm, D_MODEL), lambda i: (i, 0)),
                  pl.BlockSpec((D_MIX, D_MODEL), lambda i: (0, 0))],
        out_specs=pl.BlockSpec((tm, D_MODEL), lambda i: (i, 0)),
        out_shape=jax.ShapeDtypeStruct((rows, D_MODEL), F32),
        compiler_params=_cparams("parallel"),
        name="proj_out",
    )(oa, ob, oc, x2d, w_bf)


def _gla_kernel(u_ref, st0_ref, wg_ref, bg_ref, ng_ref, gm_ref, o_ref, st_ref, *, tl, c):
    @pl.when(pl.program_id(1) == 0)
    def _init():
        st_ref[...] = st0_ref[...]

    nk = GLA_HEADS * GLA_DK
    u = u_ref[0]
    q = u[:, 0:nk] * (GLA_DK ** -0.5)
    k = u[:, nk:2 * nk]
    v = u[:, 2 * nk:2 * nk + GLA_WIDTH]
    a_off = 2 * nk + GLA_WIDTH
    pre = _dot(u[:, a_off:a_off + LANES], wg_ref[...]) + bg_ref[...]
    log_a = _log_sigmoid(pre) * (1.0 / GLA_TAU)
    z = u[:, a_off + LANES:a_off + LANES + GLA_WIDTH]

    tril = (lax.broadcasted_iota(jnp.int32, (c, c), 0) >= lax.broadcasted_iota(jnp.int32, (c, c), 1)).astype(F32)
    hc = GLA_HEADS * c
    tri_h = (lax.broadcasted_iota(jnp.int32, (hc, c), 0) % c) >= lax.broadcasted_iota(jnp.int32, (hc, c), 1)
    k_head = lax.broadcasted_iota(jnp.int32, (1, nk), 1) // GLA_DK
    v_head = lax.broadcasted_iota(jnp.int32, (1, GLA_WIDTH), 1) // GLA_DV
    st_diag = (lax.broadcasted_iota(jnp.int32, (GLA_WIDTH, nk), 0) // GLA_DV
               == lax.broadcasted_iota(jnp.int32, (GLA_WIDTH, nk), 1) // GLA_DK)

    st = st_ref[0]
    outs = []
    for j in range(tl // c):
        sl = slice(j * c, (j + 1) * c)
        b = _dot_f32(tril, log_a[sl])
        blast = b[c - 1:c]
        qe = q[sl] * jnp.exp(b)
        ke = k[sl] * jnp.exp(-b)
        kl = k[sl] * jnp.exp(blast - b)
        vc = v[sl]
        qx = jnp.concatenate([jnp.where(k_head == h, qe, 0.0) for h in range(GLA_HEADS)], axis=0)
        a = jnp.where(tri_h, _dot_nt(qx, ke), 0.0)
        r = _dot(a, vc)
        o = _dot_nt(qe, st)
        for h in range(GLA_HEADS):
            o = o + jnp.where(v_head == h, r[h * c:(h + 1) * c], 0.0)
        st = st * jnp.exp(blast) + jnp.where(st_diag, _dot_tn(vc, kl), 0.0)
        outs.append(o)
    st_ref[0] = st
    o = outs[0] if len(outs) == 1 else jnp.concatenate(outs, axis=0)
    o_ref[0] = _group_norm(o, gm_ref[...], ng_ref[...]) * _silu(z)


def _gla(u_gla, st0, wg, bg, ng, b, l):
    tl = min(l, 256)
    c = min(l, 16)
    nk = GLA_HEADS * GLA_DK
    u3 = u_gla.reshape(b, l, W_GLA)
    kern = functools.partial(_gla_kernel, tl=tl, c=c)
    return pl.pallas_call(
        kern,
        grid=(b, l // tl),
        in_specs=[pl.BlockSpec((1, tl, W_GLA), lambda bi, li: (bi, li, 0)),
                  pl.BlockSpec((1, GLA_WIDTH, nk), lambda bi, li: (bi, 0, 0)),
                  pl.BlockSpec((LANES, nk), lambda bi, li: (0, 0)),
                  pl.BlockSpec((1, nk), lambda bi, li: (0, 0)),
                  pl.BlockSpec((1, GLA_WIDTH), lambda bi, li: (0, 0)),
                  pl.BlockSpec((GLA_WIDTH, GLA_WIDTH), lambda bi, li: (0, 0))],
        out_specs=[pl.BlockSpec((1, tl, GLA_WIDTH), lambda bi, li: (bi, li, 0)),
                   pl.BlockSpec((1, GLA_WIDTH, nk), lambda bi, li: (bi, 0, 0))],
        out_shape=[jax.ShapeDtypeStruct((b, l, GLA_WIDTH), F32),
                   jax.ShapeDtypeStruct((b, GLA_WIDTH, nk), F32)],
        compiler_params=_cparams("parallel", "arbitrary"),
        name="gla",
    )(u3, st0, wg, bg, ng, _group_mean_matrix(GLA_WIDTH))


def _gla_state_in(s):
    b = s.shape[0]
    st = jnp.swapaxes(s.astype(F32), 2, 3)
    eye = jnp.eye(GLA_HEADS, dtype=F32)
    full = st[:, :, :, None, :] * eye[None, :, None, :, None]
    return full.reshape(b, GLA_WIDTH, GLA_HEADS * GLA_DK)


def _gla_state_out(st):
    b = st.shape[0]
    full = st.reshape(b, GLA_HEADS, GLA_DV, GLA_HEADS, GLA_DK)
    diag = jnp.stack([full[:, h, :, h, :] for h in range(GLA_HEADS)], axis=1)
    return jnp.swapaxes(diag, 2, 3)


def _mlstm_kernel(u_ref, c0_ref, n0_ref, m0_ref, cv0_ref, cw_ref, cb_ref, gb_ref, ng_ref,
                  o_ref, c_ref, n_ref, m_ref, cv_ref, xp_sc, *, tl, c):
    @pl.when(pl.program_id(1) == 0)
    def _init():
        c_ref[...] = c0_ref[...]
        n_ref[...] = n0_ref[...]
        m_ref[...] = m0_ref[...]
        xp_sc[0:SUBLANES, :] = cv0_ref[0]

    w2 = 2 * ML_WIDTH
    u = u_ref[0]
    xp_sc[SUBLANES:SUBLANES + tl, :] = u[:, 0:w2]
    conv = cb_ref[...]
    for w in range(CONV_W):
        off = SUBLANES - (CONV_W - 1) + w
        conv = conv + xp_sc[off:off + tl, :] * cw_ref[w:w + 1, :]
    tail = xp_sc[tl:tl + SUBLANES, :]
    xp_sc[0:SUBLANES, :] = tail
    cv_ref[0] = tail

    qk = _silu(conv)
    mq = qk[:, 0:ML_WIDTH]
    mk = qk[:, ML_WIDTH:w2] * (ML_DH ** -0.5)
    mv = u[:, w2:w2 + ML_WIDTH]
    ifg = u[:, w2 + ML_WIDTH:w2 + ML_WIDTH + LANES] + gb_ref[...]
    logf = _log_sigmoid(ifg)
    og_off = w2 + ML_WIDTH + LANES
    og = _sigmoid(u[:, og_off:og_off + ML_WIDTH])
    zz = _silu(u[:, og_off + ML_WIDTH:og_off + 2 * ML_WIDTH])

    row = lax.broadcasted_iota(jnp.int32, (c, c), 0)
    col = lax.broadcasted_iota(jnp.int32, (c, c), 1)
    tri = row >= col
    tril = tri.astype(F32)
    lane = lax.broadcasted_iota(jnp.int32, (c, LANES), 1)

    cs = [c_ref[0, h] for h in range(ML_HEADS)]
    ns = [n_ref[0, h:h + 1, :] for h in range(ML_HEADS)]
    ms = [m_ref[0, h:h + 1, 0:1] for h in range(ML_HEADS)]
    for j in range(tl // c):
        sl = slice(j * c, (j + 1) * c)
        fcum_all = _dot_f32(tril, logf[sl])
        for h in range(ML_HEADS):
            hs = slice(h * ML_DH, (h + 1) * ML_DH)
            fc = fcum_all[:, ML_HEADS + h:ML_HEADS + h + 1]
            ii = ifg[sl, h:h + 1]
            xm = jnp.where(lane == 0, fc, jnp.where(lane < 3, 1.0, 0.0))
            ym = jnp.where(lane == 0, 1.0, jnp.where(lane == 1, -fc, jnp.where(lane == 2, ii, 0.0)))
            dm = jnp.where(tri, _dot_nt_f32(xm, ym), NEG_BIG)
            inter = fc + ms[h]
            m = jnp.maximum(inter, jnp.max(dm, axis=-1, keepdims=True))
            w_int = jnp.exp(inter - m)
            qh, kh, vh = mq[sl, hs], mk[sl, hs], mv[sl, hs]
            sij = _dot_nt(qh, kh) * jnp.exp(dm - m)
            num = w_int * _dot_nt(qh, cs[h]) + _dot(sij, vh)
            den = w_int * jnp.sum(qh * ns[h], axis=-1, keepdims=True) + jnp.sum(sij, axis=-1, keepdims=True)
            hh = num / jnp.maximum(jnp.abs(den), jnp.exp(-m))
            m_last = m[c - 1:c]
            f_last = fc[c - 1:c]
            decay = jnp.exp(f_last + ms[h] - m_last)
            wj = jnp.exp(f_last - fc + ii - m_last)
            cs[h] = decay * cs[h] + _dot_tn(wj * vh, kh)
            ns[h] = decay * ns[h] + jnp.sum(wj * kh, axis=0, keepdims=True)
            ms[h] = m_last
            hn = hh * lax.rsqrt(jnp.mean(hh * hh, axis=-1, keepdims=True) + EPS) * ng_ref[...]
            o_ref[0, sl, hs] = hn * og[sl, hs] * zz[sl, hs]
    for h in range(ML_HEADS):
        c_ref[0, h] = cs[h]
        n_ref[0, h:h + 1, :] = ns[h]
        m_ref[0, h:h + 1, :] = jnp.broadcast_to(ms[h], (1, LANES))


def _mlstm(u_ml, c0, n0, m0, conv0, cw, cb, gb, ng, b, l):
    tl = min(l, 256)
    c = min(l, ML_CHUNK)
    w2 = 2 * ML_WIDTH
    u3 = u_ml.reshape(b, l, W_ML)
    m0b = jnp.broadcast_to(m0.astype(F32)[:, :, None], (b, ML_HEADS, LANES))
    cv0 = jnp.pad(conv0.astype(F32), ((0, 0), (SUBLANES - (CONV_W - 1), 0), (0, 0)))
    gbp = jnp.pad(gb.astype(F32).reshape(1, 2 * ML_HEADS), ((0, 0), (0, LANES - 2 * ML_HEADS)))
    kern = functools.partial(_mlstm_kernel, tl=tl, c=c)
    st = lambda bi, li: (bi, 0, 0)
    st4 = lambda bi, li: (bi, 0, 0, 0)
    cst = lambda bi, li: (0, 0)
    o, c_new, n_new, m_new, cv = pl.pallas_call(
        kern,
        grid=(b, l // tl),
        in_specs=[pl.BlockSpec((1, tl, W_ML), lambda bi, li: (bi, li, 0)),
                  pl.BlockSpec((1, ML_HEADS, ML_DH, ML_DH), st4),
                  pl.BlockSpec((1, ML_HEADS, ML_DH), st),
                  pl.BlockSpec((1, ML_HEADS, LANES), st),
                  pl.BlockSpec((1, SUBLANES, w2), st),
                  pl.BlockSpec((CONV_W, w2), cst),
                  pl.BlockSpec((1, w2), cst),
                  pl.BlockSpec((1, LANES), cst),
                  pl.BlockSpec((1, ML_DH), cst)],
        out_specs=[pl.BlockSpec((1, tl, ML_WIDTH), lambda bi, li: (bi, li, 0)),
                   pl.BlockSpec((1, ML_HEADS, ML_DH, ML_DH), st4),
                   pl.BlockSpec((1, ML_HEADS, ML_DH), st),
                   pl.BlockSpec((1, ML_HEADS, LANES), st),
                   pl.BlockSpec((1, SUBLANES, w2), st)],
        out_shape=[jax.ShapeDtypeStruct((b, l, ML_WIDTH), F32),
                   jax.ShapeDtypeStruct((b, ML_HEADS, ML_DH, ML_DH), F32),
                   jax.ShapeDtypeStruct((b, ML_HEADS, ML_DH), F32),
                   jax.ShapeDtypeStruct((b, ML_HEADS, LANES), F32),
                   jax.ShapeDtypeStruct((b, SUBLANES, w2), F32)],
        scratch_shapes=[pltpu.VMEM((tl + 2 * SUBLANES, w2), F32)],
        compiler_params=_cparams("parallel", "arbitrary"),
        name="mlstm",
    )(u3, c0.astype(F32), n0.astype(F32), m0b, cv0, cw.astype(F32), cb.astype(F32).reshape(1, w2), gbp,
      ng.astype(F32).reshape(1, ML_DH))
    return o, c_new, n_new, m_new[:, :, 0], cv[:, SUBLANES - (CONV_W - 1):, :]


def _rope_lanes(x, cos_t, sin_t):
    w = x.shape[1]
    half = ROT_DIM // 2
    reps = w // cos_t.shape[1]
    if reps > 1:
        cos_t = jnp.concatenate([cos_t] * reps, axis=1)
        sin_t = jnp.concatenate([sin_t] * reps, axis=1)
    lane = lax.broadcasted_iota(jnp.int32, x.shape, 1) % HEAD_DIM
    partner = jnp.where(lane < half, pltpu.roll(x, w - half, 1), pltpu.roll(x, half, 1))
    return x * cos_t + partner * sin_t


def _nsa_prep_kernel(u_ref, cos_ref, sin_ref, qg_ref, kg_ref, g4_ref, g1_ref, qn_ref, qr_ref, rows_ref, win_ref):
    u = u_ref[...]
    cos_t = cos_ref[...]
    sin_t = sin_ref[...]
    q = _group_norm(u[:, 0:NSA_WIDTH], g4_ref[...], qg_ref[...])
    qn_ref[...] = (q * ATTN_SCALE).astype(BF16)
    qr_ref[...] = (_rope_lanes(q, cos_t, sin_t) * ATTN_SCALE).astype(BF16)
    kv = NSA_WIDTH
    k_slc = _rope_lanes(_group_norm(u[:, kv + 2 * KV_W:kv + 3 * KV_W], g1_ref[...], kg_ref[1:2, :]), cos_t, sin_t)
    k_win = _rope_lanes(_group_norm(u[:, kv + 4 * KV_W:kv + 5 * KV_W], g1_ref[...], kg_ref[2:3, :]), cos_t, sin_t)
    rows_ref[:, 0:2 * KV_W] = u[:, kv:kv + 2 * KV_W]
    rows_ref[:, 2 * KV_W:3 * KV_W] = k_slc
    rows_ref[:, 3 * KV_W:4 * KV_W] = u[:, kv + 3 * KV_W:kv + 4 * KV_W]
    win_ref[:, 0:KV_W] = k_win
    win_ref[:, KV_W:2 * KV_W] = u[:, kv + 5 * KV_W:kv + 6 * KV_W]


def _rope_tables(pos):
    half = ROT_DIM // 2
    inv = jnp.exp(-math.log(ROPE_THETA) * jnp.arange(half, dtype=F32) * 2.0 / ROT_DIM)
    ang = pos.astype(F32)[:, None] * inv[None, :]
    cos, sin = jnp.cos(ang), jnp.sin(ang)
    n = pos.shape[0]
    ones = jnp.ones((n, HEAD_DIM - ROT_DIM), F32)
    cos_h = jnp.concatenate([cos, cos, ones], axis=1)
    sin_h = jnp.concatenate([-sin, sin, 0.0 * ones], axis=1)
    reps = LANES // HEAD_DIM
    return jnp.tile(cos_h, (1, reps)), jnp.tile(sin_h, (1, reps))


def _nsa_prep(u_nsa, pos, q_g, k_g, l):
    rows = u_nsa.shape[0]
    tl = min(l, 256)
    nb = l // tl
    cos_t, sin_t = _rope_tables(pos)
    qg = jnp.tile(q_g.astype(F32).reshape(1, HEAD_DIM), (1, NSA_HEADS))
    kg = jnp.tile(k_g.astype(F32), (1, NSA_KV_HEADS))
    cst = lambda i: (0, 0)
    return pl.pallas_call(
        _nsa_prep_kernel,
        grid=(rows // tl,),
        in_specs=[pl.BlockSpec((tl, W_NSA), lambda i: (i, 0)),
                  pl.BlockSpec((tl, LANES), lambda i: (i % nb, 0)),
                  pl.BlockSpec((tl, LANES), lambda i: (i % nb, 0)),
                  pl.BlockSpec((1, NSA_WIDTH), cst),
                  pl.BlockSpec((3, KV_W), cst),
                  pl.BlockSpec((NSA_WIDTH, NSA_WIDTH), cst),
                  pl.BlockSpec((KV_W, KV_W), cst)],
        out_specs=[pl.BlockSpec((tl, NSA_WIDTH), lambda i: (i, 0)),
                   pl.BlockSpec((tl, NSA_WIDTH), lambda i: (i, 0)),
                   pl.BlockSpec((tl, N_KV_SLOTS * KV_W), lambda i: (i, 0)),
                   pl.BlockSpec((tl, 2 * KV_W), lambda i: (i, 0))],
        out_shape=[jax.ShapeDtypeStruct((rows, NSA_WIDTH), BF16),
                   jax.ShapeDtypeStruct((rows, NSA_WIDTH), BF16),
                   jax.ShapeDtypeStruct((rows, N_KV_SLOTS * KV_W), F32),
                   jax.ShapeDtypeStruct((rows, 2 * KV_W), F32)],
        compiler_params=_cparams("parallel"),
        name="nsa_prep",
    )(u_nsa, cos_t, sin_t, qg, kg, _group_mean_matrix(NSA_WIDTH), _group_mean_matrix(KV_W))


def _cmp_halves(xk_ref, xv_ref, w_ref, n_half):
    acc_k = jnp.zeros((n_half, 2 * KV_W), F32)
    acc_v = jnp.zeros((n_half, 2 * KV_W), F32)
    for s in range(CMP_STRIDE):
        acc_k = acc_k + _dot(xk_ref[pl.ds(s, n_half, stride=CMP_STRIDE), :], w_ref[0, s])
        acc_v = acc_v + _dot(xv_ref[pl.ds(s, n_half, stride=CMP_STRIDE), :], w_ref[1, s])
    return acc_k, acc_v


def _cmp_finish(acc_k, acc_v, bias_ref, kg_ref, g1_ref, kc_ref, vc_ref, n_half):
    valid = lax.broadcasted_iota(jnp.int32, (n_half, KV_W), 0) < n_half - 1

    def summary(acc, bias):
        return acc[:, 0:KV_W] + pltpu.roll(acc[:, KV_W:2 * KV_W], n_half - 1, 0) + bias

    kc = _group_norm(summary(acc_k, bias_ref[0:1, :]), g1_ref[...], kg_ref[...])
    kc_ref[0] = jnp.where(valid, kc, 0.0).astype(BF16)
    vc_ref[0] = jnp.where(valid, summary(acc_v, bias_ref[1:2, :]), 0.0).astype(BF16)


def _cmp_kv_kernel(xk_ref, xv_ref, w_ref, bias_ref, kg_ref, g1_ref, kc_ref, vc_ref, *, n_half):
    acc_k, acc_v = _cmp_halves(xk_ref.at[0], xv_ref.at[0], w_ref, n_half)
    _cmp_finish(acc_k, acc_v, bias_ref, kg_ref, g1_ref, kc_ref, vc_ref, n_half)


PAGES_PER_STEP = 16


def _cmp_paged_kernel(pt_ref, *refs, npg, page):
    page_refs = refs[:npg]
    w_ref, acck_ref, accv_ref, xk_sc, xv_sc = refs[npg:]
    for i in range(npg):
        xk_sc[i * page:(i + 1) * page, :] = page_refs[i][0, 0:KV_W, :].T
        xv_sc[i * page:(i + 1) * page, :] = page_refs[i][0, KV_W:2 * KV_W, :].T
    acc_k, acc_v = _cmp_halves(xk_sc, xv_sc, w_ref, npg * page // CMP_STRIDE)
    acck_ref[0] = acc_k
    accv_ref[0] = acc_v


def _cmp_fin_kernel(acck_ref, accv_ref, bias_ref, kg_ref, g1_ref, kc_ref, vc_ref, *, n_half):
    _cmp_finish(acck_ref[0], accv_ref[0], bias_ref, kg_ref, g1_ref, kc_ref, vc_ref, n_half)


def _page_specs(npg, page, row_blk, pool_off):
    def spec(i):
        return pl.BlockSpec((1, 2 * KV_W, page), lambda bi, ji, pt: (pt[bi, ji * npg + i] + pool_off, row_blk, 0))
    return [spec(i) for i in range(npg)]


def _cmp_kv_paged(cache_t, page_table, pool_off, wcat, bias, kg0):
    b, n_pages = page_table.shape
    page = cache_t.shape[2]
    npg = PAGES_PER_STEP
    nh_step = npg * page // CMP_STRIDE
    n_half = n_pages * page // CMP_STRIDE
    kern = functools.partial(_cmp_paged_kernel, npg=npg, page=page)
    acc_k, acc_v = pl.pallas_call(
        kern,
        grid_spec=pltpu.PrefetchScalarGridSpec(
            num_scalar_prefetch=1, grid=(b, n_pages // npg),
            in_specs=_page_specs(npg, page, 0, pool_off)
            + [pl.BlockSpec((2, CMP_STRIDE, KV_W, 2 * KV_W), lambda bi, ji, pt: (0, 0, 0, 0))],
            out_specs=[pl.BlockSpec((1, nh_step, 2 * KV_W), lambda bi, ji, pt: (bi, ji, 0)),
                       pl.BlockSpec((1, nh_step, 2 * KV_W), lambda bi, ji, pt: (bi, ji, 0))],
            scratch_shapes=[pltpu.VMEM((npg * page, KV_W), F32), pltpu.VMEM((npg * page, KV_W), F32)]),
        out_shape=[jax.ShapeDtypeStruct((b, n_half, 2 * KV_W), F32),
                   jax.ShapeDtypeStruct((b, n_half, 2 * KV_W), F32)],
        compiler_params=_cparams("parallel", "arbitrary"),
        name="cmp_paged",
    )(page_table, *([cache_t] * npg), wcat)
    kg = jnp.tile(kg0.astype(F32).reshape(1, HEAD_DIM), (1, NSA_KV_HEADS))
    blk = lambda bi: (bi, 0, 0)
    return pl.pallas_call(
        functools.partial(_cmp_fin_kernel, n_half=n_half),
        grid=(b,),
        in_specs=[pl.BlockSpec((1, n_half, 2 * KV_W), blk),
                  pl.BlockSpec((1, n_half, 2 * KV_W), blk),
                  pl.BlockSpec((2, KV_W), lambda bi: (0, 0)),
                  pl.BlockSpec((1, KV_W), lambda bi: (0, 0)),
                  pl.BlockSpec((KV_W, KV_W), lambda bi: (0, 0))],
        out_specs=[pl.BlockSpec((1, n_half, KV_W), blk), pl.BlockSpec((1, n_half, KV_W), blk)],
        out_shape=[jax.ShapeDtypeStruct((b, n_half, KV_W), BF16),
                   jax.ShapeDtypeStruct((b, n_half, KV_W), BF16)],
        compiler_params=_cparams("parallel"),
        name="cmp_fin",
    )(acc_k, acc_v, bias, kg, _group_mean_matrix(KV_W))


def _cmp_weights(cmp_pos, cmp_w):
    wf = cmp_w.astype(F32)
    eye_g = jnp.eye(NSA_KV_HEADS, dtype=F32)

    def bd(w):
        return jnp.einsum('ksde,gh->ksgdhe', w, eye_g).reshape(2, CMP_STRIDE, KV_W, KV_W)

    wcat = jnp.concatenate([bd(wf[:, :CMP_STRIDE]), bd(wf[:, CMP_STRIDE:])], axis=3).astype(BF16)
    bias = jnp.einsum('ksd,ksde->ke', cmp_pos.astype(F32), wf)
    return wcat, jnp.tile(bias, (1, NSA_KV_HEADS))


def _cmp_kv(x3, k_blk, v_blk, t_use, wcat, bias, kg0):
    b = x3.shape[0]
    n_half = t_use // CMP_STRIDE
    kern = functools.partial(_cmp_kv_kernel, n_half=n_half)
    kg = jnp.tile(kg0.astype(F32).reshape(1, HEAD_DIM), (1, NSA_KV_HEADS))
    return pl.pallas_call(
        kern,
        grid=(b,),
        in_specs=[pl.BlockSpec((1, t_use, KV_W), lambda bi: (bi, 0, k_blk)),
                  pl.BlockSpec((1, t_use, KV_W), lambda bi: (bi, 0, v_blk)),
                  pl.BlockSpec((2, CMP_STRIDE, KV_W, 2 * KV_W), lambda bi: (0, 0, 0, 0)),
                  pl.BlockSpec((2, KV_W), lambda bi: (0, 0)),
                  pl.BlockSpec((1, KV_W), lambda bi: (0, 0)),
                  pl.BlockSpec((KV_W, KV_W), lambda bi: (0, 0))],
        out_specs=[pl.BlockSpec((1, n_half, KV_W), lambda bi: (bi, 0, 0)),
                   pl.BlockSpec((1, n_half, KV_W), lambda bi: (bi, 0, 0))],
        out_shape=[jax.ShapeDtypeStruct((b, n_half, KV_W), BF16),
                   jax.ShapeDtypeStruct((b, n_half, KV_W), BF16)],
        compiler_params=_cparams("parallel"),
        name="cmp_kv",
    )(x3, x3, wcat, bias, kg, _group_mean_matrix(KV_W))


def _cmp_attn_kernel(qn_ref, kc_ref, vc_ref, ov_ref, o_ref, sel_ref, *, tq, n_half, n_cmp, n_sel, nbp, pos0):
    qi = pl.program_id(1)
    pos_c = pos0 + qi * tq + lax.broadcasted_iota(jnp.int32, (tq, 1), 0)
    ncol = lax.broadcasted_iota(jnp.int32, (1, n_half), 1)
    cmask = (ncol * CMP_STRIDE + (CMP_BLOCK - 1) <= pos_c) & (ncol < n_cmp)
    pos_r = pos0 + qi * tq + lax.broadcasted_iota(jnp.int32, (1, tq), 1)
    blk = lax.broadcasted_iota(jnp.int32, (nbp, 1), 0)
    cur = pos_r // SEL_BLOCK
    forced = (blk == 0) | (blk == cur) | (blk == cur - 1)
    valid = blk * SEL_BLOCK <= pos_r
    real = blk < n_sel
    for g in range(NSA_KV_HEADS):
        gs = slice(g * HEAD_DIM, (g + 1) * HEAD_DIM)
        kc = kc_ref[0, :, gs]
        vc = vc_ref[0, :, gs]
        psum = jnp.zeros((tq, n_half), F32)
        for h in range(NSA_HPG):
            hs = slice((g * NSA_HPG + h) * HEAD_DIM, (g * NSA_HPG + h + 1) * HEAD_DIM)
            s = _dot_nt(qn_ref[0, :, hs], kc)
            m = jnp.max(jnp.where(cmask, s, NEG_BIG), axis=-1, keepdims=True)
            m = jnp.where(m > 0.5 * NEG_BIG, m, 0.0)
            e = jnp.where(cmask, jnp.exp(s - m), 0.0)
            p = e / jnp.maximum(jnp.sum(e, axis=-1, keepdims=True), 1e-30)
            o_ref[0, :, hs] = _dot(p, vc)
            psum = psum + p
        p_hi = psum.astype(BF16)
        p_lo = (psum - p_hi.astype(F32)).astype(BF16)
        ov = ov_ref[...]
        imp = _dot_nt(ov, p_hi) + _dot_nt(ov, p_lo)
        score = jnp.where(forced, 3e38, jnp.where(valid, imp, -1e38))
        score = jnp.where(real, score, -3e38)
        cnt = jnp.zeros((nbp, tq), F32)
        for jp in range(n_sel):
            rowv = score[jp:jp + 1, :]
            beats = (rowv > score) | ((rowv == score) & (blk > jp))
            cnt = cnt + jnp.where(beats, 1.0, 0.0)
        sel = (cnt < float(min(N_SELECT, n_sel))) & real
        sel_ref[0, g * nbp:(g + 1) * nbp, :] = jnp.where(sel, 1.0, 0.0).astype(BF16)


def _overlap_t(n_cmp, n_sel, n_half, nbp):
    a = SEL_BLOCK // CMP_STRIDE
    bb = CMP_BLOCK // CMP_STRIDE
    i = np.arange(n_half)[None, :]
    j = np.arange(nbp)[:, None]
    s = i - a * j + (bb - 1)
    cnt = np.maximum(np.minimum(np.minimum(s + 1, a + bb - 1 - s), min(a, bb)), 0)
    cnt = np.where((i < n_cmp) & (j < n_sel), cnt, 0)
    return jnp.asarray(cnt, dtype=BF16)


def _cmp_attn(qn3, kc, vc, pos0, t_len):
    b, l, _ = qn3.shape
    n_half = kc.shape[1]
    n_cmp = n_half - 1
    n_sel = -(-t_len // SEL_BLOCK)
    nbp = _round_up(n_sel, 16)
    tq = min(l, 128)
    kern = functools.partial(_cmp_attn_kernel, tq=tq, n_half=n_half, n_cmp=n_cmp, n_sel=n_sel, nbp=nbp, pos0=pos0)
    return pl.pallas_call(
        kern,
        grid=(b, l // tq),
        in_specs=[pl.BlockSpec((1, tq, NSA_WIDTH), lambda bi, qi: (bi, qi, 0)),
                  pl.BlockSpec((1, n_half, KV_W), lambda bi, qi: (bi, 0, 0)),
                  pl.BlockSpec((1, n_half, KV_W), lambda bi, qi: (bi, 0, 0)),
                  pl.BlockSpec((nbp, n_half), lambda bi, qi: (0, 0))],
        out_specs=[pl.BlockSpec((1, tq, NSA_WIDTH), lambda bi, qi: (bi, qi, 0)),
                   pl.BlockSpec((1, NSA_KV_HEADS * nbp, tq), lambda bi, qi: (bi, 0, qi))],
        out_shape=[jax.ShapeDtypeStruct((b, l, NSA_WIDTH), F32),
                   jax.ShapeDtypeStruct((b, NSA_KV_HEADS * nbp, l), BF16)],
        compiler_params=_cparams("parallel", "parallel"),
        name="cmp_attn",
    )(qn3, kc, vc, _overlap_t(n_cmp, n_sel, n_half, nbp))


ATT_TQ = 128
ATT_TK = 512


def _win_blocks(l, tq, tk):
    return max((q0 + tq - 1) // tk - max(q0 - WINDOW + 1, 0) // tk + 1 for q0 in range(0, l, tq))


def _flash_kernel(*refs, mode, tq, tk, nkv):
    if mode == "sel":
        q_ref, k_ref, v_ref, sel_ref, o_ref, m_sc, l_sc, acc_sc = refs
    else:
        q_ref, k_ref, v_ref, o_ref, m_sc, l_sc, acc_sc = refs
    qi = pl.program_id(1)
    ki = pl.program_id(2)
    kmax = (qi * tq + tq - 1) // tk
    kb = ki if mode == "sel" else kmax - (nkv - 1) + ki
    active = (ki <= kmax) if mode == "sel" else (kb >= 0)
    last = (ki == kmax) if mode == "sel" else (ki == nkv - 1)

    @pl.when(ki == 0)
    def _init():
        m_sc[...] = jnp.full(m_sc.shape, NEG_BIG, F32)
        l_sc[...] = jnp.zeros(l_sc.shape, F32)
        acc_sc[...] = jnp.zeros(acc_sc.shape, F32)

    @pl.when(active)
    def _step():
        qpos = qi * tq + lax.broadcasted_iota(jnp.int32, (tq, tk), 0)
        kpos = kb * tk + lax.broadcasted_iota(jnp.int32, (tq, tk), 1)
        causal = kpos <= qpos
        if mode == "sel":
            n_blk = sel_ref.shape[-1] // NSA_KV_HEADS
            e_row = lax.broadcasted_iota(jnp.int32, (n_blk, tk), 0)
            e_col = lax.broadcasted_iota(jnp.int32, (n_blk, tk), 1)
            expand = jnp.where(e_row == kb * (tk // SEL_BLOCK) + e_col // SEL_BLOCK, 1.0, 0.0).astype(BF16)
        else:
            mask = causal & (kpos > qpos - WINDOW)
        for g in range(NSA_KV_HEADS):
            gs = slice(g * HEAD_DIM, (g + 1) * HEAD_DIM)
            if mode == "sel":
                picked = jnp.dot(sel_ref[0, :, g * n_blk:(g + 1) * n_blk], expand, preferred_element_type=F32)
                mask = jnp.where(causal, picked, 0.0) > 0.5
            kg = k_ref[0, :, gs].astype(BF16)
            vg = v_ref[0, :, gs].astype(BF16)
            for h in range(NSA_HPG):
                hh = g * NSA_HPG + h
                qh = q_ref[0, :, hh * HEAD_DIM:(hh + 1) * HEAD_DIM]
                s = lax.dot_general(qh, kg, (((1,), (1,)), ((), ())), preferred_element_type=F32)
                s = jnp.where(mask, s, NEG_BIG)
                m_prev = m_sc[hh]
                m_new = jnp.maximum(m_prev, jnp.max(s, axis=-1, keepdims=True))
                alpha = jnp.exp(m_prev - m_new)
                p = jnp.exp(s - m_new[:, :1])
                l_sc[hh] = alpha * l_sc[hh] + jnp.sum(p, axis=-1, keepdims=True)
                m_sc[hh] = m_new
                pv = jnp.dot(p.astype(BF16), vg, preferred_element_type=F32)
                acc_sc[hh] = alpha[:, :HEAD_DIM] * acc_sc[hh] + pv

    @pl.when(last)
    def _fin():
        for hh in range(NSA_HEADS):
            o_ref[0, :, hh * HEAD_DIM:(hh + 1) * HEAD_DIM] = acc_sc[hh] / l_sc[hh][:, :HEAD_DIM]


def _flash(mode, q3, kv3, k_blk, v_blk, sel=None):
    b, l, _ = q3.shape
    tq, tk = min(ATT_TQ, l), min(ATT_TK, l)
    nq = l // tq
    nkv = l // tk if mode == "sel" else _win_blocks(l, tq, tk)

    def kb_of(qi, ki):
        kmax = (qi * tq + tq - 1) // tk
        if mode == "sel":
            return jnp.minimum(ki, kmax)
        return jnp.maximum(kmax - (nkv - 1) + ki, 0)

    in_specs = [pl.BlockSpec((1, tq, NSA_WIDTH), lambda bi, qi, ki: (bi, qi, 0)),
                pl.BlockSpec((1, tk, KV_W), lambda bi, qi, ki: (bi, kb_of(qi, ki), k_blk)),
                pl.BlockSpec((1, tk, KV_W), lambda bi, qi, ki: (bi, kb_of(qi, ki), v_blk))]
    args = [q3, kv3, kv3]
    if mode == "sel":
        in_specs.append(pl.BlockSpec((1, tq, sel.shape[-1]), lambda bi, qi, ki: (bi, qi, 0)))
        args.append(sel)
    kern = functools.partial(_flash_kernel, mode=mode, tq=tq, tk=tk, nkv=nkv)
    return pl.pallas_call(
        kern,
        grid=(b, nq, nkv),
        in_specs=in_specs,
        out_specs=pl.BlockSpec((1, tq, NSA_WIDTH), lambda bi, qi, ki: (bi, qi, 0)),
        out_shape=jax.ShapeDtypeStruct((b, l, NSA_WIDTH), F32),
        scratch_shapes=[pltpu.VMEM((NSA_HEADS, tq, LANES), F32),
                        pltpu.VMEM((NSA_HEADS, tq, LANES), F32),
                        pltpu.VMEM((NSA_HEADS, tq, HEAD_DIM), F32)],
        compiler_params=_cparams("parallel", "parallel", "arbitrary"),
        name="flash_" + mode,
    )(*args)


def _softmax_update(s, m_prev, l_prev):
    m_new = jnp.maximum(m_prev, jnp.max(s, axis=-1, keepdims=True))
    alpha = jnp.exp(m_prev - m_new)
    p = jnp.exp(s - m_new[:, :1])
    return m_new, alpha, alpha * l_prev + jnp.sum(p, axis=-1, keepdims=True), p


def _paged_attn_kernel(pt_ref, *refs, npg, page, l_new, nbp):
    page_refs = refs[:npg]
    (qz_ref, sel_ref, kn_ref, vn_ref, win_ref, kwn_ref, vwn_ref,
     oslc_ref, owin_ref, wout_ref, m_sc, l_sc, acc_sc) = refs[npg:]
    ji = pl.program_id(1)
    nr = qz_ref.shape[1]
    span = npg * page

    @pl.when(ji == 0)
    def _init():
        m_sc[...] = jnp.full(m_sc.shape, NEG_BIG, F32)
        l_sc[...] = jnp.zeros(l_sc.shape, F32)
        acc_sc[...] = jnp.zeros(acc_sc.shape, F32)

    qz = qz_ref[0]
    kt = jnp.concatenate([page_refs[i][0, 0:KV_W, :] for i in range(npg)], axis=1).astype(BF16)
    vt = jnp.concatenate([page_refs[i][0, KV_W:2 * KV_W, :] for i in range(npg)], axis=1).astype(BF16)
    s = jnp.dot(qz, kt, preferred_element_type=F32)
    e_row = lax.broadcasted_iota(jnp.int32, (nbp, span), 0)
    e_col = lax.broadcasted_iota(jnp.int32, (nbp, span), 1)
    expand = jnp.where(e_row == ji * (span // SEL_BLOCK) + e_col // SEL_BLOCK, 1.0, 0.0).astype(BF16)
    picked = jnp.dot(sel_ref[0], expand, preferred_element_type=F32) > 0.5
    m_new, alpha, l_new_v, p = _softmax_update(jnp.where(picked, s, NEG_BIG), m_sc[...], l_sc[...])
    m_sc[...] = m_new
    l_sc[...] = l_new_v
    acc_sc[...] = alpha * acc_sc[...] + _dot_nt(p, vt)

    @pl.when(ji == pl.num_programs(1) - 1)
    def _fin():
        q_of_row = lax.broadcasted_iota(jnp.int32, (nr, l_new), 0) % l_new
        j_new = lax.broadcasted_iota(jnp.int32, (nr, l_new), 1)
        new_ok = j_new <= q_of_row
        sn = jnp.where(new_ok, _dot_nt(qz, kn_ref[0]), NEG_BIG)
        m2, a2, l2, p2 = _softmax_update(sn, m_sc[...], l_sc[...])
        oslc_ref[0] = (a2 * acc_sc[...] + _dot(p2, vn_ref[0])) / l2
        wlen = win_ref.shape[2]
        kw = win_ref[0, 0:KV_W, :]
        vw = win_ref[0, KV_W:2 * KV_W, :]
        i_old = lax.broadcasted_iota(jnp.int32, (nr, wlen), 1)
        q_old = lax.broadcasted_iota(jnp.int32, (nr, wlen), 0) % l_new
        sw = jnp.where(i_old + (WINDOW - wlen) > q_old, _dot(qz, kw), NEG_BIG)
        swn = jnp.where(new_ok, _dot_nt(qz, kwn_ref[0]), NEG_BIG)
        mw = jnp.maximum(jnp.max(sw, axis=-1, keepdims=True), jnp.max(swn, axis=-1, keepdims=True))
        pw = jnp.exp(sw - mw)
        pwn = jnp.exp(swn - mw)
        lw = jnp.sum(pw, axis=-1, keepdims=True) + jnp.sum(pwn, axis=-1, keepdims=True)
        owin_ref[0] = (_dot_nt(pw, vw) + _dot(pwn, vwn_ref[0])) / lw
        new_t = jnp.concatenate([kwn_ref[0], vwn_ref[0]], axis=1).T
        place = (lax.broadcasted_iota(jnp.int32, (l_new, wlen), 1)
                 == lax.broadcasted_iota(jnp.int32, (l_new, wlen), 0) + (wlen - l_new)).astype(F32)
        lane = lax.broadcasted_iota(jnp.int32, (2 * KV_W, wlen), 1)
        wout_ref[0] = jnp.where(lane < wlen - l_new, pltpu.roll(win_ref[0], wlen - l_new, 1), _dot_f32(new_t, place))


def _paged_attn(cache_t, page_table, pool_off, qz, sel_rows, rows3, win_t, win_off, win3):
    b, n_pages = page_table.shape
    page = cache_t.shape[2]
    npg = PAGES_PER_STEP
    nr = qz.shape[1]
    l_new = rows3.shape[1]
    nbp = sel_rows.shape[2]
    wlen = win_t.shape[2]
    kern = functools.partial(_paged_attn_kernel, npg=npg, page=page, l_new=l_new, nbp=nbp)
    per_b = lambda bi, ji, pt: (bi, 0, 0)
    return pl.pallas_call(
        kern,
        grid_spec=pltpu.PrefetchScalarGridSpec(
            num_scalar_prefetch=1, grid=(b, n_pages // npg),
            in_specs=_page_specs(npg, page, 1, pool_off)
            + [pl.BlockSpec((1, nr, KV_W), per_b),
               pl.BlockSpec((1, nr, nbp), per_b),
               pl.BlockSpec((1, l_new, KV_W), lambda bi, ji, pt: (bi, 0, 2)),
               pl.BlockSpec((1, l_new, KV_W), lambda bi, ji, pt: (bi, 0, 3)),
               pl.BlockSpec((1, 2 * KV_W, wlen), lambda bi, ji, pt: (bi + win_off, 0, 0)),
               pl.BlockSpec((1, l_new, KV_W), lambda bi, ji, pt: (bi, 0, 0)),
               pl.BlockSpec((1, l_new, KV_W), lambda bi, ji, pt: (bi, 0, 1))],
            out_specs=[pl.BlockSpec((1, nr, KV_W), per_b),
                       pl.BlockSpec((1, nr, KV_W), per_b),
                       pl.BlockSpec((1, 2 * KV_W, wlen), per_b)],
            scratch_shapes=[pltpu.VMEM((nr, LANES), F32), pltpu.VMEM((nr, LANES), F32),
                            pltpu.VMEM((nr, KV_W), F32)]),
        out_shape=[jax.ShapeDtypeStruct((b, nr, KV_W), F32),
                   jax.ShapeDtypeStruct((b, nr, KV_W), F32),
                   jax.ShapeDtypeStruct((b, 2 * KV_W, wlen), F32)],
        compiler_params=_cparams("parallel", "arbitrary"),
        name="paged_attn",
    )(page_table, *([cache_t] * npg), qz, sel_rows, rows3, rows3, win_t, win3, win3)


def _nsa_gate_kernel(oc_ref, os_ref, ow_ref, gz_ref, e_ref, o_ref):
    gate = _sigmoid(gz_ref[:, 0:LANES])
    o = (_dot_f32(gate, e_ref[0]) * oc_ref[...] + _dot_f32(gate, e_ref[1]) * os_ref[...]
         + _dot_f32(gate, e_ref[2]) * ow_ref[...])
    o_ref[...] = o * _silu(gz_ref[:, LANES:LANES + NSA_WIDTH])


def _gate_expand():
    e = np.zeros((3, LANES, NSA_WIDTH), np.float32)
    for j in range(3):
        for h in range(NSA_HEADS):
            e[j, j * NSA_HEADS + h, h * HEAD_DIM:(h + 1) * HEAD_DIM] = 1.0
    return jnp.asarray(e)


def _nsa_gate(o_cmp, o_slc, o_win, u_nsa, tm):
    rows = o_cmp.shape[0]
    gz_w = LANES + NSA_WIDTH
    gz_blk = (NSA_WIDTH + 6 * KV_W) // gz_w
    assert gz_blk * gz_w == NSA_WIDTH + 6 * KV_W
    row = lambda i: (i, 0)
    return pl.pallas_call(
        _nsa_gate_kernel,
        grid=(rows // tm,),
        in_specs=[pl.BlockSpec((tm, NSA_WIDTH), row),
                  pl.BlockSpec((tm, NSA_WIDTH), row),
                  pl.BlockSpec((tm, NSA_WIDTH), row),
                  pl.BlockSpec((tm, gz_w), lambda i: (i, gz_blk)),
                  pl.BlockSpec((3, LANES, NSA_WIDTH), lambda i: (0, 0, 0))],
        out_specs=pl.BlockSpec((tm, NSA_WIDTH), row),
        out_shape=jax.ShapeDtypeStruct((rows, NSA_WIDTH), F32),
        compiler_params=_cparams("parallel"),
        name="nsa_gate",
    )(o_cmp, o_slc, o_win, u_nsa, _gate_expand())


def _rms_norm(x, g, eps=EPS):
    xf = x.astype(F32)
    y = xf * lax.rsqrt(jnp.mean(xf * xf, axis=-1, keepdims=True) + eps)
    return (y * g.astype(F32)).astype(x.dtype)


def _rope(x, pos):
    half = ROT_DIM // 2
    inv = jnp.exp(-math.log(ROPE_THETA) * jnp.arange(half, dtype=F32) * 2.0 / ROT_DIM)
    ang = pos.astype(F32)[:, None] * inv[None, :]
    cos = jnp.cos(ang)[:, None, :]
    sin = jnp.sin(ang)[:, None, :]
    xf = x.astype(F32)
    x1, x2, rest = xf[..., :half], xf[..., half:ROT_DIM], xf[..., ROT_DIM:]
    return jnp.concatenate([x1 * cos - x2 * sin, x2 * cos + x1 * sin, rest], axis=-1).astype(x.dtype)


def _masked_softmax(s, mask):
    s = jnp.where(mask, s, -jnp.inf)
    m = jnp.max(s, axis=-1, keepdims=True)
    m = jnp.where(jnp.isfinite(m), m, 0.0)
    e = jnp.where(mask, jnp.exp(s - m), 0.0)
    return e / jnp.maximum(jnp.sum(e, axis=-1, keepdims=True), 1e-30)


def _cmp_summaries(x, pe, w):
    b, t, g, d = x.shape
    n_half = t // CMP_STRIDE
    halves = x[:, :n_half * CMP_STRIDE].astype(F32).reshape(b, n_half, CMP_STRIDE, g, d)
    wf = w.astype(F32)
    first = jnp.einsum('bnsgd,sde->bnge', halves[:, :-1], wf[:CMP_STRIDE])
    second = jnp.einsum('bnsgd,sde->bnge', halves[:, 1:], wf[CMP_STRIDE:])
    bias = jnp.einsum('sd,sde->e', pe.astype(F32), wf)
    return first + second + bias


def _overlap_matrix(n_cmp, n_sel):
    a = SEL_BLOCK // CMP_STRIDE
    bb = CMP_BLOCK // CMP_STRIDE
    i = jnp.arange(n_cmp)[:, None]
    j = jnp.arange(n_sel)[None, :]
    s = i - a * j + (bb - 1)
    cnt = jnp.minimum(jnp.minimum(s + 1, a + bb - 1 - s), min(a, bb))
    return jnp.maximum(cnt, 0).astype(F32)


def _nsa_selected_gather(q, k_hist, v_hist, idx, pos):
    b, l = q.shape[:2]
    t_len = k_hist.shape[1]
    n_sel = -(-t_len // SEL_BLOCK)
    n_k = idx.shape[-1]
    pad = n_sel * SEL_BLOCK - t_len

    def blocks(x):
        x = jnp.pad(x.astype(F32), ((0, 0), (0, pad), (0, 0), (0, 0)))
        return x.reshape(b, n_sel, SEL_BLOCK, NSA_KV_HEADS, HEAD_DIM).transpose(0, 3, 1, 2, 4)

    kb, vb = blocks(k_hist), blocks(v_hist)
    qb = math.gcd(l, Q_BLOCK)
    nq = l // qb
    qx = q.astype(F32).reshape(b * nq, qb, NSA_KV_HEADS, NSA_HPG, HEAD_DIM)
    ix = idx.reshape(b, NSA_KV_HEADS, nq, qb, n_k).transpose(0, 2, 1, 3, 4).reshape(b * nq, NSA_KV_HEADS, qb, n_k)
    px = jnp.tile(pos.reshape(nq, qb), (b, 1))
    bx = jnp.repeat(jnp.arange(b), nq)
    g_ix = jnp.arange(NSA_KV_HEADS)[:, None, None]
    offs = jnp.arange(SEL_BLOCK)
    n_keys = n_k * SEL_BLOCK

    def attend(args):
        qi, ii, pi, bi = args
        ks = kb[bi][g_ix, ii]
        vs = vb[bi][g_ix, ii]
        kpos = ii[..., None] * SEL_BLOCK + offs
        mask = (kpos <= pi[None, :, None, None]).reshape(NSA_KV_HEADS, 1, qb, n_keys)
        s = jnp.einsum('qghd,gqkjd->ghqkj', qi, ks).reshape(NSA_KV_HEADS, NSA_HPG, qb, n_keys) * ATTN_SCALE
        p = _masked_softmax(s, mask)
        return jnp.einsum('ghqn,gqnd->qghd', p, vs.reshape(NSA_KV_HEADS, qb, n_keys, HEAD_DIM))

    o = lax.map(attend, (qx, ix, px, bx))
    return o.reshape(b, l, NSA_HEADS, HEAD_DIM)


def _nsa_window(q, k_ctx, v_ctx, pos0):
    b, l = q.shape[:2]
    lc = k_ctx.shape[1]
    padw = ((0, 0), (WINDOW, 0), (0, 0), (0, 0))
    kp = jnp.pad(k_ctx.astype(F32), padw)
    vp = jnp.pad(v_ctx.astype(F32), padw)
    qb = math.gcd(l, Q_BLOCK)
    nq = l // qb
    span = WINDOW + qb
    qx = jnp.moveaxis(q.astype(F32).reshape(b, nq, qb, NSA_KV_HEADS, NSA_HPG, HEAD_DIM), 1, 0)
    starts = jnp.arange(nq) * qb
    first_pos = pos0 + l - lc

    def attend(args):
        qi, i0 = args
        kk = lax.dynamic_slice_in_dim(kp, i0 + lc - l, span, axis=1)
        vv = lax.dynamic_slice_in_dim(vp, i0 + lc - l, span, axis=1)
        kpos = pos0 + i0 - WINDOW + jnp.arange(span)
        qpos = pos0 + i0 + jnp.arange(qb)
        mask = ((kpos[None, :] >= first_pos) & (kpos[None, :] <= qpos[:, None])
                & (kpos[None, :] > qpos[:, None] - WINDOW))
        s = jnp.einsum('bqghd,bkgd->bghqk', qi, kk) * ATTN_SCALE
        p = _masked_softmax(s, mask)
        return jnp.einsum('bghqk,bkgd->bqghd', p, vv)

    o = lax.map(attend, (qx, starts))
    return jnp.moveaxis(o, 0, 1).reshape(b, l, NSA_HEADS, HEAD_DIM)


def _nsa_mixer_jax(n_q, n_kv, n_g, past_kv, win_buf, win_keep, pos0, q_g, k_g, cmp_pos, cmp_w):
    b, l, _ = n_q.shape
    pos = pos0 + jnp.arange(l)
    q = _rms_norm(n_q.reshape(b, l, NSA_HEADS, HEAD_DIM), q_g)
    q_rot = _rope(q, pos)
    kv = n_kv.reshape(b, l, 6, NSA_KV_HEADS, HEAD_DIM)
    k_slc = _rope(_rms_norm(kv[:, :, 2], k_g[1]), pos)
    k_win = _rope(_rms_norm(kv[:, :, 4], k_g[2]), pos)
    new_rows = jnp.stack([kv[:, :, 0], kv[:, :, 1], k_slc, kv[:, :, 3]], axis=2)
    new_win = jnp.stack([k_win, kv[:, :, 5]], axis=2)
    hist = jnp.concatenate([past_kv.astype(new_rows.dtype), new_rows], axis=1)
    ctx = jnp.concatenate([win_buf.astype(new_win.dtype), new_win], axis=1)
    t_len = hist.shape[1]

    kc = _rms_norm(_cmp_summaries(hist[:, :, 0], cmp_pos[0], cmp_w[0]), k_g[0])
    vc = _cmp_summaries(hist[:, :, 1], cmp_pos[1], cmp_w[1])
    n_cmp = kc.shape[1]
    qg = q.astype(F32).reshape(b, l, NSA_KV_HEADS, NSA_HPG, HEAD_DIM)
    s = jnp.einsum('blghd,bngd->bghln', qg, kc) * ATTN_SCALE
    cmp_end = jnp.arange(n_cmp) * CMP_STRIDE + CMP_BLOCK - 1
    p_cmp = _masked_softmax(s, cmp_end[None, :] <= pos[:, None])
    o_cmp = jnp.einsum('bghln,bngd->blghd', p_cmp, vc).reshape(b, l, NSA_HEADS, HEAD_DIM)

    n_sel = -(-t_len // SEL_BLOCK)
    imp = jnp.einsum('bghln,nj->bglj', p_cmp, _overlap_matrix(n_cmp, n_sel))
    blk = jnp.arange(n_sel)[None, :]
    cur = (pos // SEL_BLOCK)[:, None]
    valid = blk * SEL_BLOCK <= pos[:, None]
    forced = (blk == 0) | (blk == cur) | (blk == cur - 1)
    score = jnp.where(forced, jnp.inf, jnp.where(valid, imp, -jnp.inf))
    _, idx = lax.top_k(score, min(N_SELECT, n_sel))
    o_slc = _nsa_selected_gather(q_rot, hist[:, :, 2], hist[:, :, 3], idx, pos)
    o_win = _nsa_window(q_rot, ctx[:, :, 0], ctx[:, :, 1], pos0)

    gate = jax.nn.sigmoid(n_g.astype(F32)).reshape(b, l, 3, NSA_HEADS, 1)
    o = gate[:, :, 0] * o_cmp + gate[:, :, 1] * o_slc + gate[:, :, 2] * o_win
    return o.reshape(b, l, NSA_WIDTH), new_rows, ctx[:, ctx.shape[1] - win_keep:]


def _nsa_fresh(u_nsa, b, l, win_keep, q_g, k_g, cmp_pos, cmp_w, tm):
    qn, qr, rows, win = _nsa_prep(u_nsa, jnp.arange(l), q_g, k_g, l)
    rows3 = rows.reshape(b, l, N_KV_SLOTS * KV_W)
    win3 = win.reshape(b, l, 2 * KV_W)
    wcat, bias = _cmp_weights(cmp_pos, cmp_w)
    t_use = (l // CMP_STRIDE) * CMP_STRIDE
    kc, vc = _cmp_kv(rows3, 0, 1, t_use, wcat, bias, k_g[0])
    o_cmp, sel_t = _cmp_attn(qn.reshape(b, l, NSA_WIDTH), kc, vc, 0, l)
    sel = jnp.swapaxes(sel_t, 1, 2)
    qr3 = qr.reshape(b, l, NSA_WIDTH)
    o_slc = _flash("sel", qr3, rows3, 2, 3, sel)
    o_win = _flash("win", qr3, win3, 0, 1)
    o_b = _nsa_gate(o_cmp.reshape(b * l, NSA_WIDTH), o_slc.reshape(b * l, NSA_WIDTH),
                    o_win.reshape(b * l, NSA_WIDTH), u_nsa, tm)
    new_rows = rows3.reshape(b, l, N_KV_SLOTS, NSA_KV_HEADS, HEAD_DIM)
    new_win = win3[:, l - win_keep:].reshape(b, win_keep, 2, NSA_KV_HEADS, HEAD_DIM)
    return o_b, new_rows, new_win


def _nsa_paged(u_nsa, b, l, paged, q_g, k_g, cmp_pos, cmp_w, tm):
    cache_t, page_table, pool_off, win_t, win_off = paged
    past_len = page_table.shape[1] * cache_t.shape[2]
    assert (past_len + l) // CMP_STRIDE == past_len // CMP_STRIDE and past_len % SEL_BLOCK == 0
    qn, qr, rows, win = _nsa_prep(u_nsa, past_len + jnp.arange(l), q_g, k_g, l)
    rows3 = rows.reshape(b, l, N_KV_SLOTS * KV_W)
    win3 = win.reshape(b, l, 2 * KV_W)
    wcat, bias = _cmp_weights(cmp_pos, cmp_w)
    kc, vc = _cmp_kv_paged(cache_t, page_table, pool_off, wcat, bias, k_g[0])
    o_cmp, sel_t = _cmp_attn(qn.reshape(b, l, NSA_WIDTH), kc, vc, past_len, past_len + l)
    nbp = sel_t.shape[1] // NSA_KV_HEADS
    sel_rows = jnp.swapaxes(sel_t.reshape(b, NSA_KV_HEADS, 1, nbp, l), 3, 4)
    sel_rows = jnp.broadcast_to(sel_rows, (b, NSA_KV_HEADS, NSA_HPG, l, nbp)).reshape(b, NSA_HEADS * l, nbp)
    q5 = jnp.transpose(qr.reshape(b, l, NSA_KV_HEADS, NSA_HPG, HEAD_DIM), (0, 2, 3, 1, 4))
    qz = jnp.einsum('bghqd,gk->bghqkd', q5, jnp.eye(NSA_KV_HEADS, dtype=q5.dtype)).reshape(b, NSA_HEADS * l, KV_W)
    o_slc_z, o_win_z, wout = _paged_attn(cache_t, page_table, pool_off, qz, sel_rows, rows3, win_t, win_off, win3)

    def own_group(o):
        o6 = o.reshape(b, NSA_KV_HEADS, NSA_HPG, l, NSA_KV_HEADS, HEAD_DIM)
        d = jnp.stack([o6[:, g, :, :, g, :] for g in range(NSA_KV_HEADS)], axis=1)
        return jnp.transpose(d, (0, 3, 1, 2, 4)).reshape(b * l, NSA_WIDTH)

    o_b = _nsa_gate(o_cmp.reshape(b * l, NSA_WIDTH), own_group(o_slc_z), own_group(o_win_z), u_nsa, min(tm, b * l))
    new_rows = rows3.reshape(b, l, N_KV_SLOTS, NSA_KV_HEADS, HEAD_DIM)
    wlen = wout.shape[2]
    new_win = jnp.transpose(wout.reshape(b, 2, NSA_KV_HEADS, HEAD_DIM, wlen), (0, 4, 1, 2, 3))
    return o_b, new_rows, new_win


def _hybrid_layer(x, paged, win_keep, s_gla, c_ml, n_ml, m_ml, conv_ml,
                  norm_g, w_in_pad, w_out_bf, gla_w_gate, gla_b_gate, gla_norm_g,
                  nsa_q_norm_g, nsa_k_norm_g, nsa_cmp_pos, nsa_cmp_w,
                  ml_conv_w, ml_conv_b, ml_gate_b, ml_norm_g):
    b, l, _ = x.shape
    rows = b * l
    tm = 256
    x2d = x.reshape(rows, D_MODEL)
    u_gla, u_nsa, u_ml = _proj_in(x2d, norm_g, w_in_pad, tm)

    nk = GLA_HEADS * GLA_DK
    wg = jnp.pad(gla_w_gate.astype(F32), ((0, LANES - GLA_RANK), (0, 0))).astype(BF16)
    o_a, st_new = _gla(u_gla, _gla_state_in(s_gla), wg, gla_b_gate.astype(F32).reshape(1, nk),
                       jnp.tile(gla_norm_g.astype(F32).reshape(1, GLA_DV), (1, GLA_HEADS)), b, l)
    s_new = _gla_state_out(st_new)

    if paged is None:
        o_b, new_rows, new_win = _nsa_fresh(u_nsa, b, l, win_keep, nsa_q_norm_g, nsa_k_norm_g,
                                            nsa_cmp_pos, nsa_cmp_w, tm)
    else:
        o_b, new_rows, new_win = _nsa_paged(u_nsa, b, l, paged, nsa_q_norm_g, nsa_k_norm_g,
                                            nsa_cmp_pos, nsa_cmp_w, tm)

    o_c, c_new, n_new, m_new, conv_new = _mlstm(u_ml, c_ml, n_ml, m_ml, conv_ml, ml_conv_w, ml_conv_b,
                                                ml_gate_b, ml_norm_g, b, l)

    y = _proj_out(o_a.reshape(rows, GLA_WIDTH), o_b, o_c.reshape(rows, ML_WIDTH), x2d, w_out_bf, tm)
    return y.reshape(b, l, D_MODEL), new_rows, new_win, s_new, c_new, n_new, m_new, conv_new


def kernel(x_prompt, x_sample, cache_nsa_kv, state_nsa_win, state_gla, state_mlstm_C, state_mlstm_n,
           state_mlstm_m, state_mlstm_conv, page_table, norm_g, w_in, w_out, gla_w_gate, gla_b_gate,
           gla_norm_g, nsa_q_norm_g, nsa_k_norm_g, nsa_cmp_pos, nsa_cmp_w, ml_conv_w, ml_conv_b,
           ml_gate_b, ml_norm_g):
    bp, sp, _ = x_prompt.shape
    bs, _, _ = x_sample.shape
    depth = w_in.shape[0]
    n_pages = page_table.shape[1]
    past_len = n_pages * cache_nsa_kv.shape[2]
    dt = x_prompt.dtype
    zero_gla = jnp.zeros((bp, GLA_HEADS, GLA_DK, GLA_DV), F32)
    zero_c = jnp.zeros((bp, ML_HEADS, ML_DH, ML_DH), F32)
    zero_n = jnp.zeros((bp, ML_HEADS, ML_DH), F32)
    zero_m = jnp.zeros((bp, ML_HEADS), F32)
    zero_conv = jnp.zeros((bp, CONV_W - 1, 2 * ML_WIDTH), dt)
    keep_p = min(WINDOW, sp)
    keep_s = state_nsa_win.shape[2]
    n_pool = cache_nsa_kv.shape[1]
    cache_t = jnp.transpose(cache_nsa_kv, (0, 1, 3, 4, 5, 2)).reshape(
        depth * n_pool, N_KV_SLOTS * KV_W, cache_nsa_kv.shape[2]).astype(F32)
    win_t = jnp.transpose(state_nsa_win, (0, 1, 3, 4, 5, 2)).reshape(depth * bs, 2 * KV_W, keep_s).astype(F32)

    y_prompt, y_sample = x_prompt, x_sample
    p_layers, s_layers = [], []
    for layer in range(depth):
        w = (norm_g[layer], _pad_w_in(w_in[layer]), w_out[layer].astype(BF16), gla_w_gate[layer],
             gla_b_gate[layer], gla_norm_g[layer],
             nsa_q_norm_g[layer], nsa_k_norm_g[layer], nsa_cmp_pos[layer], nsa_cmp_w[layer],
             ml_conv_w[layer], ml_conv_b[layer], ml_gate_b[layer], ml_norm_g[layer])
        y_prompt, *p_new = _hybrid_layer(y_prompt, None, keep_p, zero_gla, zero_c, zero_n,
                                         zero_m, zero_conv, *w)
        paged = (cache_t, page_table, layer * n_pool, win_t, layer * bs)
        y_sample, *s_new = _hybrid_layer(y_sample, paged, keep_s,
                                         state_gla[layer], state_mlstm_C[layer], state_mlstm_n[layer],
                                         state_mlstm_m[layer], state_mlstm_conv[layer], *w)
        p_layers.append(p_new)
        s_layers.append(s_new)
    p_kv, p_win, p_gla, p_c, p_n, p_m, p_conv = [jnp.stack(z) for z in zip(*p_layers)]
    s_kv, s_win, s_gla, s_c, s_n, s_m, s_conv = [jnp.stack(z) for z in zip(*s_layers)]
    return (y_prompt, y_sample, p_kv, s_kv, p_win, s_win, p_gla, s_gla, p_c, s_c, p_n, s_n, p_m, s_m, p_conv, s_conv)
```

```python
import functools
import math

import jax
import jax.numpy as jnp
import numpy as np
from jax import lax
from jax.experimental import pallas as pl
from jax.experimental.pallas import tpu as pltpu

F32 = jnp.float32
BF16 = jnp.bfloat16
HIGHEST = lax.Precision.HIGHEST

D_MODEL = 1024
HEAD_DIM = 64
GLA_WIDTH = D_MODEL // 4
NSA_WIDTH = D_MODEL // 2
ML_WIDTH = D_MODEL - GLA_WIDTH - NSA_WIDTH
D_MIX = GLA_WIDTH + NSA_WIDTH + ML_WIDTH

GLA_HEADS = GLA_WIDTH // HEAD_DIM
GLA_DK = HEAD_DIM // 2
GLA_DV = HEAD_DIM
GLA_RANK = 16
GLA_TAU = 16.0
GLA_CHUNK = 64

NSA_HEADS = NSA_WIDTH // HEAD_DIM
NSA_KV_HEADS = 2
NSA_HPG = NSA_HEADS // NSA_KV_HEADS
CMP_BLOCK = 32
CMP_STRIDE = 16
SEL_BLOCK = 64
N_SELECT = 16
WINDOW = 512
Q_BLOCK = 128
N_KV_SLOTS = 4
ROT_DIM = HEAD_DIM // 4
ROPE_THETA = 500000.0
ATTN_SCALE = HEAD_DIM ** -0.5

ML_HEADS = ML_WIDTH // HEAD_DIM
ML_DH = HEAD_DIM
ML_CHUNK = 64
CONV_W = 4

SPLIT_SIZES = (GLA_HEADS * GLA_DK, GLA_HEADS * GLA_DK, GLA_WIDTH, GLA_RANK, GLA_WIDTH,
               NSA_WIDTH, 6 * NSA_KV_HEADS * HEAD_DIM, 3 * NSA_HEADS, NSA_WIDTH,
               2 * ML_WIDTH, ML_WIDTH, 2 * ML_HEADS, ML_WIDTH, ML_WIDTH)

LANES = 128
SUBLANES = 8
VMEM_LIMIT = 56 * 1024 * 1024
NEG_BIG = -1e30
EPS = 1e-6
LOG2E = math.log2(math.e)


def _round_up(n, m):
    return -(-n // m) * m


PAD_SIZES = tuple(_round_up(s, LANES) for s in SPLIT_SIZES)
D_IN_PAD = sum(PAD_SIZES)
W_GLA = sum(PAD_SIZES[0:5])
W_NSA = sum(PAD_SIZES[5:9])
W_ML = sum(PAD_SIZES[9:14])
KV_W = NSA_KV_HEADS * HEAD_DIM


def _dot(a, b):
    return jnp.dot(a.astype(BF16), b.astype(BF16), preferred_element_type=F32)


def _dot_nt(a, b):
    return lax.dot_general(a.astype(BF16), b.astype(BF16), (((1,), (1,)), ((), ())), preferred_element_type=F32)


def _dot_tn(a, b):
    return lax.dot_general(a.astype(BF16), b.astype(BF16), (((0,), (0,)), ((), ())), preferred_element_type=F32)


def _dot_f32(a, b):
    return jnp.dot(a, b, precision=HIGHEST, preferred_element_type=F32)


def _dot_nt_f32(a, b):
    return lax.dot_general(a, b, (((1,), (1,)), ((), ())), precision=HIGHEST, preferred_element_type=F32)


def _dot_split(a, b):
    a_hi = a.astype(BF16)
    a_lo = (a - a_hi.astype(F32)).astype(BF16)
    bb = b.astype(BF16)
    return jnp.dot(a_hi, bb, preferred_element_type=F32) + jnp.dot(a_lo, bb, preferred_element_type=F32)


def _log_sigmoid(x):
    return jnp.minimum(x, 0.0) - jnp.log1p(jnp.exp(-jnp.abs(x)))


def _sigmoid(x):
    return 1.0 / (1.0 + jnp.exp(-x))


def _silu(x):
    return x * _sigmoid(x)


def _group_mean_matrix(width):
    g = np.kron(np.eye(width // HEAD_DIM, dtype=np.float32), np.full((HEAD_DIM, HEAD_DIM), 1.0 / HEAD_DIM, np.float32))
    return jnp.asarray(g, dtype=BF16)


def _group_norm(x, gmat, gain):
    ms = _dot_split(x * x, gmat)
    return x * lax.rsqrt(ms + EPS) * gain


def _cparams(*sem):
    return pltpu.CompilerParams(dimension_semantics=sem, vmem_limit_bytes=VMEM_LIMIT)


def _proj_in_kernel(x_ref, g_ref, w_ref, ug_ref, un_ref, um_ref):
    x = x_ref[...]
    y = x * lax.rsqrt(jnp.mean(x * x, axis=-1, keepdims=True) + EPS) * g_ref[...]
    r = jnp.dot(y.astype(BF16), w_ref[...], preferred_element_type=F32)
    ug_ref[...] = r[:, 0:W_GLA]
    un_ref[...] = r[:, W_GLA:W_GLA + W_NSA]
    um_ref[...] = r[:, W_GLA + W_NSA:D_IN_PAD]


def _proj_in(x2d, g, w_pad, tm):
    rows = x2d.shape[0]
    return pl.pallas_call(
        _proj_in_kernel,
        grid=(rows // tm,),
        in_specs=[pl.BlockSpec((tm, D_MODEL), lambda i: (i, 0)),
                  pl.BlockSpec((1, D_MODEL), lambda i: (0, 0)),
                  pl.BlockSpec((D_MODEL, D_IN_PAD), lambda i: (0, 0))],
        out_specs=[pl.BlockSpec((tm, W_GLA), lambda i: (i, 0)),
                   pl.BlockSpec((tm, W_NSA), lambda i: (i, 0)),
                   pl.BlockSpec((tm, W_ML), lambda i: (i, 0))],
        out_shape=[jax.ShapeDtypeStruct((rows, W_GLA), F32),
                   jax.ShapeDtypeStruct((rows, W_NSA), F32),
                   jax.ShapeDtypeStruct((rows, W_ML), F32)],
        compiler_params=_cparams("parallel"),
        name="proj_in",
    )(x2d, g.reshape(1, D_MODEL), w_pad)


def _pad_w_in(w_in):
    parts = []
    off = 0
    for s, p in zip(SPLIT_SIZES, PAD_SIZES):
        seg = w_in[:, off:off + s]
        if p != s:
            seg = jnp.pad(seg, ((0, 0), (0, p - s)))
        parts.append(seg)
        off += s
    return jnp.concatenate(parts, axis=1).astype(BF16)


def _proj_out_kernel(oa_ref, ob_ref, oc_ref, x_ref, w_ref, y_ref):
    y = x_ref[...]
    y = y + _dot(oa_ref[...], w_ref[0:GLA_WIDTH, :])
    y = y + _dot(ob_ref[...], w_ref[GLA_WIDTH:GLA_WIDTH + NSA_WIDTH, :])
    y = y + _dot(oc_ref[...], w_ref[GLA_WIDTH + NSA_WIDTH:D_MIX, :])
    y_ref[...] = y


def _proj_out(oa, ob, oc, x2d, w_bf, tm):
    rows = x2d.shape[0]
    return pl.pallas_call(
        _proj_out_kernel,
        grid=(rows // tm,),
        in_specs=[pl.BlockSpec((tm, GLA_WIDTH), lambda i: (i, 0)),
                  pl.BlockSpec((tm, NSA_WIDTH), lambda i: (i, 0)),
                  pl.BlockSpec((tm, ML_WIDTH), lambda i: (i, 0)),
                  pl.BlockSpec((tm, D_MODEL), lambda i: (i, 0)),
                  pl.BlockSpec((D_MIX, D_MODEL), lambda i: (0, 0))],
        out_specs=pl.BlockSpec((tm, D_MODEL), lambda i: (i, 0)),
        out_shape=jax.ShapeDtypeStruct((rows, D_MODEL), F32),
        compiler_params=_cparams("parallel"),
        name="proj_out",
    )(oa, ob, oc, x2d, w_bf)


def _gla_kernel(u_ref, st0_ref, wg_ref, bg_ref, ng_ref, gm_ref, o_ref, st_ref, *, tl, c):
    @pl.when(pl.program_id(1) == 0)
    def _init():
        st_ref[...] = st0_ref[...]

    nk = GLA_HEADS * GLA_DK
    u = u_ref[0]
    q = u[:, 0:nk] * (GLA_DK ** -0.5)
    k = u[:, nk:2 * nk]
    v = u[:, 2 * nk:2 * nk + GLA_WIDTH]
    a_off = 2 * nk + GLA_WIDTH
    pre = _dot(u[:, a_off:a_off + LANES], wg_ref[...]) + bg_ref[...]
    log_a = _log_sigmoid(pre) * (1.0 / GLA_TAU)
    z = u[:, a_off + LANES:a_off + LANES + GLA_WIDTH]

    tril = (lax.broadcasted_iota(jnp.int32, (c, c), 0) >= lax.broadcasted_iota(jnp.int32, (c, c), 1)).astype(F32)
    hc = GLA_HEADS * c
    tri_h = (lax.broadcasted_iota(jnp.int32, (hc, c), 0) % c) >= lax.broadcasted_iota(jnp.int32, (hc, c), 1)
    k_head = lax.broadcasted_iota(jnp.int32, (1, nk), 1) // GLA_DK
    v_head = lax.broadcasted_iota(jnp.int32, (1, GLA_WIDTH), 1) // GLA_DV
    st_diag = (lax.broadcasted_iota(jnp.int32, (GLA_WIDTH, nk), 0) // GLA_DV
               == lax.broadcasted_iota(jnp.int32, (GLA_WIDTH, nk), 1) // GLA_DK)

    st = st_ref[0]
    outs = []
    for j in range(tl // c):
        sl = slice(j * c, (j + 1) * c)
        b = _dot_f32(tril, log_a[sl])
        blast = b[c - 1:c]
        qe = q[sl] * jnp.exp(b)
        ke = k[sl] * jnp.exp(-b)
        kl = k[sl] * jnp.exp(blast - b)
        vc = v[sl]
        qx = jnp.concatenate([jnp.where(k_head == h, qe, 0.0) for h in range(GLA_HEADS)], axis=0)
        a = jnp.where(tri_h, _dot_nt(qx, ke), 0.0)
        r = _dot(a, vc)
        o = _dot_nt(qe, st)
        for h in range(GLA_HEADS):
            o = o + jnp.where(v_head == h, r[h * c:(h + 1) * c], 0.0)
        st = st * jnp.exp(blast) + jnp.where(st_diag, _dot_tn(vc, kl), 0.0)
        outs.append(o)
    st_ref[0] = st
    o = outs[0] if len(outs) == 1 else jnp.concatenate(outs, axis=0)
    o_ref[0] = _group_norm(o, gm_ref[...], ng_ref[...]) * _silu(z)


def _gla(u_gla, st0, wg, bg, ng, b, l):
    tl = min(l, 256)
    c = min(l, 16)
    nk = GLA_HEADS * GLA_DK
    u3 = u_gla.reshape(b, l, W_GLA)
    kern = functools.partial(_gla_kernel, tl=tl, c=c)
    return pl.pallas_call(
        kern,
        grid=(b, l // tl),
        in_specs=[pl.BlockSpec((1, tl, W_GLA), lambda bi, li: (bi, li, 0)),
                  pl.BlockSpec((1, GLA_WIDTH, nk), lambda bi, li: (bi, 0, 0)),
                  pl.BlockSpec((LANES, nk), lambda bi, li: (0, 0)),
                  pl.BlockSpec((1, nk), lambda bi, li: (0, 0)),
                  pl.BlockSpec((1, GLA_WIDTH), lambda bi, li: (0, 0)),
                  pl.BlockSpec((GLA_WIDTH, GLA_WIDTH), lambda bi, li: (0, 0))],
        out_specs=[pl.BlockSpec((1, tl, GLA_WIDTH), lambda bi, li: (bi, li, 0)),
                   pl.BlockSpec((1, GLA_WIDTH, nk), lambda bi, li: (bi, 0, 0))],
        out_shape=[jax.ShapeDtypeStruct((b, l, GLA_WIDTH), F32),
                   jax.ShapeDtypeStruct((b, GLA_WIDTH, nk), F32)],
        compiler_params=_cparams("parallel", "arbitrary"),
        name="gla",
    )(u3, st0, wg, bg, ng, _group_mean_matrix(GLA_WIDTH))


def _gla_state_in(s):
    b = s.shape[0]
    st = jnp.swapaxes(s.astype(F32), 2, 3)
    eye = jnp.eye(GLA_HEADS, dtype=F32)
    full = st[:, :, :, None, :] * eye[None, :, None, :, None]
    return full.reshape(b, GLA_WIDTH, GLA_HEADS * GLA_DK)


def _gla_state_out(st):
    b = st.shape[0]
    full = st.reshape(b, GLA_HEADS, GLA_DV, GLA_HEADS, GLA_DK)
    diag = jnp.stack([full[:, h, :, h, :] for h in range(GLA_HEADS)], axis=1)
    return jnp.swapaxes(diag, 2, 3)


def _mlstm_kernel(u_ref, c0_ref, n0_ref, m0_ref, cv0_ref, cw_ref, cb_ref, gb_ref, ng_ref,
                  o_ref, c_ref, n_ref, m_ref, cv_ref, xp_sc, *, tl, c):
    @pl.when(pl.program_id(1) == 0)
    def _init():
        c_ref[...] = c0_ref[...]
        n_ref[...] = n0_ref[...]
        m_ref[...] = m0_ref[...]
        xp_sc[0:SUBLANES, :] = cv0_ref[0]

    w2 = 2 * ML_WIDTH
    u = u_ref[0]
    xp_sc[SUBLANES:SUBLANES + tl, :] = u[:, 0:w2]
    conv = cb_ref[...]
    for w in range(CONV_W):
        off = SUBLANES - (CONV_W - 1) + w
        conv = conv + xp_sc[off:off + tl, :] * cw_ref[w:w + 1, :]
    tail = xp_sc[tl:tl + SUBLANES, :]
    xp_sc[0:SUBLANES, :] = tail
    cv_ref[0] = tail

    qk = _silu(conv)
    mq = qk[:, 0:ML_WIDTH]
    mk = qk[:, ML_WIDTH:w2] * (ML_DH ** -0.5)
    mv = u[:, w2:w2 + ML_WIDTH]
    ifg = u[:, w2 + ML_WIDTH:w2 + ML_WIDTH + LANES] + gb_ref[...]
    logf = _log_sigmoid(ifg)
    og_off = w2 + ML_WIDTH + LANES
    og = _sigmoid(u[:, og_off:og_off + ML_WIDTH])
    zz = _silu(u[:, og_off + ML_WIDTH:og_off + 2 * ML_WIDTH])

    row = lax.broadcasted_iota(jnp.int32, (c, c), 0)
    col = lax.broadcasted_iota(jnp.int32, (c, c), 1)
    tri = row >= col
    tril = tri.astype(F32)
    lane = lax.broadcasted_iota(jnp.int32, (c, LANES), 1)

    cs = [c_ref[0, h] for h in range(ML_HEADS)]
    ns = [n_ref[0, h:h + 1, :] for h in range(ML_HEADS)]
    ms = [m_ref[0, h:h + 1, 0:1] for h in range(ML_HEADS)]
    for j in range(tl // c):
        sl = slice(j * c, (j + 1) * c)
        fcum_all = _dot_f32(tril, logf[sl])
        gates_t = jnp.where(lane < ML_HEADS, ifg[sl], fcum_all).T
        for h in range(ML_HEADS):
            hs = slice(h * ML_DH, (h + 1) * ML_DH)
            fc = fcum_all[:, ML_HEADS + h:ML_HEADS + h + 1]
            ii = ifg[sl, h:h + 1]
            dm = jnp.where(tri, fc - gates_t[ML_HEADS + h:ML_HEADS + h + 1, :] + gates_t[h:h + 1, :], NEG_BIG)
            inter = fc + ms[h]
            m = jnp.maximum(inter, jnp.max(dm, axis=-1, keepdims=True))
            w_int = jnp.exp(inter - m)
            qh, kh, vh = mq[sl, hs], mk[sl, hs], mv[sl, hs]
            sij = _dot_nt(qh, kh) * jnp.exp(dm - m)
            num = w_int * _dot_nt(qh, cs[h]) + _dot(sij, vh)
            den = w_int * jnp.sum(qh * ns[h], axis=-1, keepdims=True) + jnp.sum(sij, axis=-1, keepdims=True)
            hh = num / jnp.maximum(jnp.abs(den), jnp.exp(-m))
            m_last = m[c - 1:c]
            f_last = fc[c - 1:c]
            decay = jnp.exp(f_last + ms[h] - m_last)
            wj = jnp.exp(f_last - fc + ii - m_last)
            cs[h] = decay * cs[h] + _dot_tn(wj * vh, kh)
            ns[h] = decay * ns[h] + jnp.sum(wj * kh, axis=0, keepdims=True)
            ms[h] = m_last
            hn = hh * lax.rsqrt(jnp.mean(hh * hh, axis=-1, keepdims=True) + EPS) * ng_ref[...]
            o_ref[0, sl, hs] = hn * og[sl, hs] * zz[sl, hs]
    for h in range(ML_HEADS):
        c_ref[0, h] = cs[h]
        n_ref[0, h:h + 1, :] = ns[h]
        m_ref[0, h:h + 1, :] = jnp.broadcast_to(ms[h], (1, LANES))


def _mlstm(u_ml, c0, n0, m0, conv0, cw, cb, gb, ng, b, l):
    tl = min(l, 256)
    c = tl
    w2 = 2 * ML_WIDTH
    u3 = u_ml.reshape(b, l, W_ML)
    m0b = jnp.broadcast_to(m0.astype(F32)[:, :, None], (b, ML_HEADS, LANES))
    cv0 = jnp.pad(conv0.astype(F32), ((0, 0), (SUBLANES - (CONV_W - 1), 0), (0, 0)))
    gbp = jnp.pad(gb.astype(F32).reshape(1, 2 * ML_HEADS), ((0, 0), (0, LANES - 2 * ML_HEADS)))
    kern = functools.partial(_mlstm_kernel, tl=tl, c=c)
    st = lambda bi, li: (bi, 0, 0)
    st4 = lambda bi, li: (bi, 0, 0, 0)
    cst = lambda bi, li: (0, 0)
    o, c_new, n_new, m_new, cv = pl.pallas_call(
        kern,
        grid=(b, l // tl),
        in_specs=[pl.BlockSpec((1, tl, W_ML), lambda bi, li: (bi, li, 0)),
                  pl.BlockSpec((1, ML_HEADS, ML_DH, ML_DH), st4),
                  pl.BlockSpec((1, ML_HEADS, ML_DH), st),
                  pl.BlockSpec((1, ML_HEADS, LANES), st),
                  pl.BlockSpec((1, SUBLANES, w2), st),
                  pl.BlockSpec((CONV_W, w2), cst),
                  pl.BlockSpec((1, w2), cst),
                  pl.BlockSpec((1, LANES), cst),
                  pl.BlockSpec((1, ML_DH), cst)],
        out_specs=[pl.BlockSpec((1, tl, ML_WIDTH), lambda bi, li: (bi, li, 0)),
                   pl.BlockSpec((1, ML_HEADS, ML_DH, ML_DH), st4),
                   pl.BlockSpec((1, ML_HEADS, ML_DH), st),
                   pl.BlockSpec((1, ML_HEADS, LANES), st),
                   pl.BlockSpec((1, SUBLANES, w2), st)],
        out_shape=[jax.ShapeDtypeStruct((b, l, ML_WIDTH), F32),
                   jax.ShapeDtypeStruct((b, ML_HEADS, ML_DH, ML_DH), F32),
                   jax.ShapeDtypeStruct((b, ML_HEADS, ML_DH), F32),
                   jax.ShapeDtypeStruct((b, ML_HEADS, LANES), F32),
                   jax.ShapeDtypeStruct((b, SUBLANES, w2), F32)],
        scratch_shapes=[pltpu.VMEM((tl + 2 * SUBLANES, w2), F32)],
        compiler_params=_cparams("parallel", "arbitrary"),
        name="mlstm",
    )(u3, c0.astype(F32), n0.astype(F32), m0b, cv0, cw.astype(F32), cb.astype(F32).reshape(1, w2), gbp,
      ng.astype(F32).reshape(1, ML_DH))
    return o, c_new, n_new, m_new[:, :, 0], cv[:, SUBLANES - (CONV_W - 1):, :]


def _rope_lanes(x, cos_t, sin_t):
    w = x.shape[1]
    half = ROT_DIM // 2
    reps = w // cos_t.shape[1]
    if reps > 1:
        cos_t = jnp.concatenate([cos_t] * reps, axis=1)
        sin_t = jnp.concatenate([sin_t] * reps, axis=1)
    lane = lax.broadcasted_iota(jnp.int32, x.shape, 1) % HEAD_DIM
    partner = jnp.where(lane < half, pltpu.roll(x, w - half, 1), pltpu.roll(x, half, 1))
    return x * cos_t + partner * sin_t


def _nsa_prep_kernel(u_ref, cos_ref, sin_ref, qg_ref, kg_ref, g4_ref, g1_ref, qn_ref, qr_ref, rows_ref, win_ref):
    u = u_ref[...]
    cos_t = cos_ref[...]
    sin_t = sin_ref[...]
    q = _group_norm(u[:, 0:NSA_WIDTH], g4_ref[...], qg_ref[...])
    qn_ref[...] = (q * ATTN_SCALE).astype(BF16)
    qr = _rope_lanes(q, cos_t, sin_t) * (ATTN_SCALE * LOG2E)
    low = lax.broadcasted_iota(jnp.int32, (qr.shape[0], LANES), 1) < HEAD_DIM
    for j in range(NSA_HEADS // 2):
        pair = qr[:, j * LANES:(j + 1) * LANES]
        swapped = pltpu.roll(pair, HEAD_DIM, 1)
        qr_ref[2 * j] = jnp.where(low, pair, swapped).astype(BF16)
        qr_ref[2 * j + 1] = jnp.where(low, swapped, pair).astype(BF16)
    kv = NSA_WIDTH
    k_slc = _rope_lanes(_group_norm(u[:, kv + 2 * KV_W:kv + 3 * KV_W], g1_ref[...], kg_ref[1:2, :]), cos_t, sin_t)
    k_win = _rope_lanes(_group_norm(u[:, kv + 4 * KV_W:kv + 5 * KV_W], g1_ref[...], kg_ref[2:3, :]), cos_t, sin_t)
    rows_ref[:, 0:2 * KV_W] = u[:, kv:kv + 2 * KV_W]
    rows_ref[:, 2 * KV_W:3 * KV_W] = k_slc
    rows_ref[:, 3 * KV_W:4 * KV_W] = u[:, kv + 3 * KV_W:kv + 4 * KV_W]
    win_ref[:, 0:KV_W] = k_win
    win_ref[:, KV_W:2 * KV_W] = u[:, kv + 5 * KV_W:kv + 6 * KV_W]


def _rope_tables(pos):
    half = ROT_DIM // 2
    inv = jnp.exp(-math.log(ROPE_THETA) * jnp.arange(half, dtype=F32) * 2.0 / ROT_DIM)
    ang = pos.astype(F32)[:, None] * inv[None, :]
    cos, sin = jnp.cos(ang), jnp.sin(ang)
    n = pos.shape[0]
    ones = jnp.ones((n, HEAD_DIM - ROT_DIM), F32)
    cos_h = jnp.concatenate([cos, cos, ones], axis=1)
    sin_h = jnp.concatenate([-sin, sin, 0.0 * ones], axis=1)
    reps = LANES // HEAD_DIM
    return jnp.tile(cos_h, (1, reps)), jnp.tile(sin_h, (1, reps))


def _nsa_prep(u_nsa, pos, q_g, k_g, l):
    rows = u_nsa.shape[0]
    tl = min(l, 256)
    nb = l // tl
    cos_t, sin_t = _rope_tables(pos)
    qg = jnp.tile(q_g.astype(F32).reshape(1, HEAD_DIM), (1, NSA_HEADS))
    kg = jnp.tile(k_g.astype(F32), (1, NSA_KV_HEADS))
    cst = lambda i: (0, 0)
    return pl.pallas_call(
        _nsa_prep_kernel,
        grid=(rows // tl,),
        in_specs=[pl.BlockSpec((tl, W_NSA), lambda i: (i, 0)),
                  pl.BlockSpec((tl, LANES), lambda i: (i % nb, 0)),
                  pl.BlockSpec((tl, LANES), lambda i: (i % nb, 0)),
                  pl.BlockSpec((1, NSA_WIDTH), cst),
                  pl.BlockSpec((3, KV_W), cst),
                  pl.BlockSpec((NSA_WIDTH, NSA_WIDTH), cst),
                  pl.BlockSpec((KV_W, KV_W), cst)],
        out_specs=[pl.BlockSpec((tl, NSA_WIDTH), lambda i: (i, 0)),
                   pl.BlockSpec((NSA_HEADS, tl, LANES), lambda i: (0, i, 0)),
                   pl.BlockSpec((tl, N_KV_SLOTS * KV_W), lambda i: (i, 0)),
                   pl.BlockSpec((tl, 2 * KV_W), lambda i: (i, 0))],
        out_shape=[jax.ShapeDtypeStruct((rows, NSA_WIDTH), BF16),
                   jax.ShapeDtypeStruct((NSA_HEADS, rows, LANES), BF16),
                   jax.ShapeDtypeStruct((rows, N_KV_SLOTS * KV_W), F32),
                   jax.ShapeDtypeStruct((rows, 2 * KV_W), F32)],
        compiler_params=_cparams("parallel"),
        name="nsa_prep",
    )(u_nsa, cos_t, sin_t, qg, kg, _group_mean_matrix(NSA_WIDTH), _group_mean_matrix(KV_W))


def _cmp_halves(xk_ref, xv_ref, w_ref, n_half):
    acc_k = jnp.zeros((n_half, 2 * KV_W), F32)
    acc_v = jnp.zeros((n_half, 2 * KV_W), F32)
    for s in range(CMP_STRIDE):
        acc_k = acc_k + _dot(xk_ref[pl.ds(s, n_half, stride=CMP_STRIDE), :], w_ref[0, s])
        acc_v = acc_v + _dot(xv_ref[pl.ds(s, n_half, stride=CMP_STRIDE), :], w_ref[1, s])
    return acc_k, acc_v


def _cmp_finish(acc_k, acc_v, bias_ref, kg_ref, g1_ref, kc_ref, vc_ref, n_half):
    valid = lax.broadcasted_iota(jnp.int32, (n_half, KV_W), 0) < n_half - 1

    def summary(acc, bias):
        return acc[:, 0:KV_W] + pltpu.roll(acc[:, KV_W:2 * KV_W], n_half - 1, 0) + bias

    kc = _group_norm(summary(acc_k, bias_ref[0:1, :]), g1_ref[...], kg_ref[...])
    kc_ref[0] = jnp.where(valid, kc, 0.0).astype(BF16)
    vc_ref[0] = jnp.where(valid, summary(acc_v, bias_ref[1:2, :]), 0.0).astype(BF16)


def _cmp_kv_kernel(xk_ref, xv_ref, w_ref, bias_ref, kg_ref, g1_ref, kc_ref, vc_ref, *, n_half):
    acc_k, acc_v = _cmp_halves(xk_ref.at[0], xv_ref.at[0], w_ref, n_half)
    _cmp_finish(acc_k, acc_v, bias_ref, kg_ref, g1_ref, kc_ref, vc_ref, n_half)


PAGES_PER_STEP = 16


def _cmp_paged_kernel(pt_ref, *refs, npg, page):
    page_refs = refs[:npg]
    w_ref, acck_ref, accv_ref, xk_sc, xv_sc = refs[npg:]
    for i in range(npg):
        xk_sc[i * page:(i + 1) * page, :] = page_refs[i][0, 0:KV_W, :].T
        xv_sc[i * page:(i + 1) * page, :] = page_refs[i][0, KV_W:2 * KV_W, :].T
    acc_k, acc_v = _cmp_halves(xk_sc, xv_sc, w_ref, npg * page // CMP_STRIDE)
    acck_ref[0] = acc_k
    accv_ref[0] = acc_v


def _cmp_fin_kernel(acck_ref, accv_ref, bias_ref, kg_ref, g1_ref, kc_ref, vc_ref, *, n_half):
    _cmp_finish(acck_ref[0], accv_ref[0], bias_ref, kg_ref, g1_ref, kc_ref, vc_ref, n_half)


def _page_specs(npg, page, row_blk, pool_off):
    def spec(i):
        return pl.BlockSpec((1, 2 * KV_W, page), lambda bi, ji, pt: (pt[bi, ji * npg + i] + pool_off, row_blk, 0))
    return [spec(i) for i in range(npg)]


def _cmp_kv_paged(cache_t, page_table, pool_off, wcat, bias, kg0):
    b, n_pages = page_table.shape
    page = cache_t.shape[2]
    npg = PAGES_PER_STEP
    nh_step = npg * page // CMP_STRIDE
    n_half = n_pages * page // CMP_STRIDE
    kern = functools.partial(_cmp_paged_kernel, npg=npg, page=page)
    acc_k, acc_v = pl.pallas_call(
        kern,
        grid_spec=pltpu.PrefetchScalarGridSpec(
            num_scalar_prefetch=1, grid=(b, n_pages // npg),
            in_specs=_page_specs(npg, page, 0, pool_off)
            + [pl.BlockSpec((2, CMP_STRIDE, KV_W, 2 * KV_W), lambda bi, ji, pt: (0, 0, 0, 0))],
            out_specs=[pl.BlockSpec((1, nh_step, 2 * KV_W), lambda bi, ji, pt: (bi, ji, 0)),
                       pl.BlockSpec((1, nh_step, 2 * KV_W), lambda bi, ji, pt: (bi, ji, 0))],
            scratch_shapes=[pltpu.VMEM((npg * page, KV_W), F32), pltpu.VMEM((npg * page, KV_W), F32)]),
        out_shape=[jax.ShapeDtypeStruct((b, n_half, 2 * KV_W), F32),
                   jax.ShapeDtypeStruct((b, n_half, 2 * KV_W), F32)],
        compiler_params=_cparams("parallel", "arbitrary"),
        name="cmp_paged",
    )(page_table, *([cache_t] * npg), wcat)
    kg = jnp.tile(kg0.astype(F32).reshape(1, HEAD_DIM), (1, NSA_KV_HEADS))
    blk = lambda bi: (bi, 0, 0)
    return pl.pallas_call(
        functools.partial(_cmp_fin_kernel, n_half=n_half),
        grid=(b,),
        in_specs=[pl.BlockSpec((1, n_half, 2 * KV_W), blk),
                  pl.BlockSpec((1, n_half, 2 * KV_W), blk),
                  pl.BlockSpec((2, KV_W), lambda bi: (0, 0)),
                  pl.BlockSpec((1, KV_W), lambda bi: (0, 0)),
                  pl.BlockSpec((KV_W, KV_W), lambda bi: (0, 0))],
        out_specs=[pl.BlockSpec((1, n_half, KV_W), blk), pl.BlockSpec((1, n_half, KV_W), blk)],
        out_shape=[jax.ShapeDtypeStruct((b, n_half, KV_W), BF16),
                   jax.ShapeDtypeStruct((b, n_half, KV_W), BF16)],
        compiler_params=_cparams("parallel"),
        name="cmp_fin",
    )(acc_k, acc_v, bias, kg, _group_mean_matrix(KV_W))


def _cmp_weights(cmp_pos, cmp_w):
    wf = cmp_w.astype(F32)
    eye_g = jnp.eye(NSA_KV_HEADS, dtype=F32)

    def bd(w):
        return jnp.einsum('ksde,gh->ksgdhe', w, eye_g).reshape(2, CMP_STRIDE, KV_W, KV_W)

    wcat = jnp.concatenate([bd(wf[:, :CMP_STRIDE]), bd(wf[:, CMP_STRIDE:])], axis=3).astype(BF16)
    bias = jnp.einsum('ksd,ksde->ke', cmp_pos.astype(F32), wf)
    return wcat, jnp.tile(bias, (1, NSA_KV_HEADS))


def _cmp_kv(x3, k_blk, v_blk, t_use, wcat, bias, kg0):
    b = x3.shape[0]
    n_half = t_use // CMP_STRIDE
    kern = functools.partial(_cmp_kv_kernel, n_half=n_half)
    kg = jnp.tile(kg0.astype(F32).reshape(1, HEAD_DIM), (1, NSA_KV_HEADS))
    return pl.pallas_call(
        kern,
        grid=(b,),
        in_specs=[pl.BlockSpec((1, t_use, KV_W), lambda bi: (bi, 0, k_blk)),
                  pl.BlockSpec((1, t_use, KV_W), lambda bi: (bi, 0, v_blk)),
                  pl.BlockSpec((2, CMP_STRIDE, KV_W, 2 * KV_W), lambda bi: (0, 0, 0, 0)),
                  pl.BlockSpec((2, KV_W), lambda bi: (0, 0)),
                  pl.BlockSpec((1, KV_W), lambda bi: (0, 0)),
                  pl.BlockSpec((KV_W, KV_W), lambda bi: (0, 0))],
        out_specs=[pl.BlockSpec((1, n_half, KV_W), lambda bi: (bi, 0, 0)),
                   pl.BlockSpec((1, n_half, KV_W), lambda bi: (bi, 0, 0))],
        out_shape=[jax.ShapeDtypeStruct((b, n_half, KV_W), BF16),
                   jax.ShapeDtypeStruct((b, n_half, KV_W), BF16)],
        compiler_params=_cparams("parallel"),
        name="cmp_kv",
    )(x3, x3, wcat, bias, kg, _group_mean_matrix(KV_W))


def _cmp_attn_kernel(qn_ref, kc_ref, vc_ref, ov_ref, o_ref, sel_ref, *, tq, n_half, n_cmp, n_sel, nbp, pos0):
    qi = pl.program_id(1)
    pos_c = pos0 + qi * tq + lax.broadcasted_iota(jnp.int32, (tq, 1), 0)
    ncol = lax.broadcasted_iota(jnp.int32, (1, n_half), 1)
    cmask = (ncol * CMP_STRIDE + (CMP_BLOCK - 1) <= pos_c) & (ncol < n_cmp)
    pos_r = pos0 + qi * tq + lax.broadcasted_iota(jnp.int32, (1, tq), 1)
    blk = lax.broadcasted_iota(jnp.int32, (nbp, 1), 0)
    cur = pos_r // SEL_BLOCK
    forced = (blk == 0) | (blk == cur) | (blk == cur - 1)
    valid = blk * SEL_BLOCK <= pos_r
    real = blk < n_sel
    for g in range(NSA_KV_HEADS):
        gs = slice(g * HEAD_DIM, (g + 1) * HEAD_DIM)
        kc = kc_ref[0, :, gs]
        vc = vc_ref[0, :, gs]
        psum = jnp.zeros((tq, n_half), F32)
        for h in range(NSA_HPG):
            hs = slice((g * NSA_HPG + h) * HEAD_DIM, (g * NSA_HPG + h + 1) * HEAD_DIM)
            s = _dot_nt(qn_ref[0, :, hs], kc)
            m = jnp.max(jnp.where(cmask, s, NEG_BIG), axis=-1, keepdims=True)
            m = jnp.where(m > 0.5 * NEG_BIG, m, 0.0)
            e = jnp.where(cmask, jnp.exp(s - m), 0.0)
            p = e / jnp.maximum(jnp.sum(e, axis=-1, keepdims=True), 1e-30)
            o_ref[0, :, hs] = _dot(p, vc)
            psum = psum + p
        p_hi = psum.astype(BF16)
        p_lo = (psum - p_hi.astype(F32)).astype(BF16)
        ov = ov_ref[...]
        imp = _dot_nt(ov, p_hi) + _dot_nt(ov, p_lo)
        score = jnp.where(forced, 3e38, jnp.where(valid, imp, -1e38))
        score = jnp.where(real, score, -3e38)
        cnt = jnp.zeros((nbp, tq), F32)
        for jp in range(n_sel):
            rowv = score[jp:jp + 1, :]
            beats = (rowv > score) | ((rowv == score) & (blk > jp))
            cnt = cnt + jnp.where(beats, 1.0, 0.0)
        sel = (cnt < float(min(N_SELECT, n_sel))) & real
        sel_ref[0, g * nbp:(g + 1) * nbp, :] = jnp.where(sel, 1.0, 0.0).astype(BF16)


def _overlap_t(n_cmp, n_sel, n_half, nbp):
    a = SEL_BLOCK // CMP_STRIDE
    bb = CMP_BLOCK // CMP_STRIDE
    i = np.arange(n_half)[None, :]
    j = np.arange(nbp)[:, None]
    s = i - a * j + (bb - 1)
    cnt = np.maximum(np.minimum(np.minimum(s + 1, a + bb - 1 - s), min(a, bb)), 0)
    cnt = np.where((i < n_cmp) & (j < n_sel), cnt, 0)
    return jnp.asarray(cnt, dtype=BF16)


def _cmp_attn(qn3, kc, vc, pos0, t_len):
    b, l, _ = qn3.shape
    n_half = kc.shape[1]
    n_cmp = n_half - 1
    n_sel = -(-t_len // SEL_BLOCK)
    nbp = _round_up(n_sel, 16)
    tq = min(l, 128)
    kern = functools.partial(_cmp_attn_kernel, tq=tq, n_half=n_half, n_cmp=n_cmp, n_sel=n_sel, nbp=nbp, pos0=pos0)
    return pl.pallas_call(
        kern,
        grid=(b, l // tq),
        in_specs=[pl.BlockSpec((1, tq, NSA_WIDTH), lambda bi, qi: (bi, qi, 0)),
                  pl.BlockSpec((1, n_half, KV_W), lambda bi, qi: (bi, 0, 0)),
                  pl.BlockSpec((1, n_half, KV_W), lambda bi, qi: (bi, 0, 0)),
                  pl.BlockSpec((nbp, n_half), lambda bi, qi: (0, 0))],
        out_specs=[pl.BlockSpec((1, tq, NSA_WIDTH), lambda bi, qi: (bi, qi, 0)),
                   pl.BlockSpec((1, NSA_KV_HEADS * nbp, tq), lambda bi, qi: (bi, 0, qi))],
        out_shape=[jax.ShapeDtypeStruct((b, l, NSA_WIDTH), F32),
                   jax.ShapeDtypeStruct((b, NSA_KV_HEADS * nbp, l), BF16)],
        compiler_params=_cparams("parallel", "parallel"),
        name="cmp_attn",
    )(qn3, kc, vc, _overlap_t(n_cmp, n_sel, n_half, nbp))


SEL_TQ = 128
SEL_TK = 512
WIN_TQ = 256


def _group_lanes(shape, g):
    lane = lax.broadcasted_iota(jnp.int32, shape, len(shape) - 1)
    return (lane < HEAD_DIM) if g == 0 else (lane >= HEAD_DIM)


def _store_heads(o_ref, r, g, tq):
    lo = g * HEAD_DIM
    for h in range(NSA_HPG):
        hh = g * NSA_HPG + h
        o_ref[0, :, hh * HEAD_DIM:(hh + 1) * HEAD_DIM] = r[h * tq:(h + 1) * tq, lo:lo + HEAD_DIM]


def _flash_sel_kernel(q_ref, k_ref, v_ref, sb_ref, o_ref, m_sc, acc_sc, *, tq, tk):
    qi = pl.program_id(1)
    ki = pl.program_id(2)
    kmax = (qi * tq + tq - 1) // tk
    hq = NSA_HPG * tq

    @pl.when(ki == 0)
    def _init():
        m_sc[...] = jnp.full(m_sc.shape, NEG_BIG, F32)
        acc_sc[...] = jnp.zeros(acc_sc.shape, F32)

    def step(diagonal):
        key_blk = ki * (tk // SEL_BLOCK) + lax.broadcasted_iota(jnp.int32, (tk, LANES), 0) // SEL_BLOCK
        one_hot = jnp.where(lax.broadcasted_iota(jnp.int32, (tk, LANES), 1) % HEAD_DIM == key_blk, 1.0, 0.0).astype(BF16)
        kblk = k_ref[0].astype(BF16)
        vblk = v_ref[0].astype(BF16)
        sb4 = jnp.concatenate([sb_ref[0]] * NSA_HPG, axis=0)
        if diagonal:
            qpos = qi * tq + lax.broadcasted_iota(jnp.int32, (tq, tk), 0)
            kpos = ki * tk + lax.broadcasted_iota(jnp.int32, (tq, tk), 1)
            causal4 = jnp.concatenate([jnp.where(kpos <= qpos, 0.0, NEG_BIG)] * NSA_HPG, axis=0)
        for g in range(NSA_KV_HEADS):
            own_k = _group_lanes((tk, LANES), g)
            q4 = q_ref[g * NSA_HPG:(g + 1) * NSA_HPG].reshape(hq, LANES)
            qx = jnp.where(_group_lanes((hq, LANES), g), q4, sb4)
            kx = jnp.where(own_k, kblk, one_hot)
            s = lax.dot_general(qx, kx, (((1,), (1,)), ((), ())), preferred_element_type=F32)
            if diagonal:
                s = s + causal4
            m_prev = m_sc[g]
            m_new = jnp.maximum(m_prev, jnp.max(s, axis=-1, keepdims=True))
            p = jnp.exp2(s - m_new[:, :1]).astype(BF16)
            vx = jnp.where(own_k, vblk, 1.0)
            acc_sc[g] = jnp.exp2(m_prev - m_new) * acc_sc[g] + jnp.dot(p, vx, preferred_element_type=F32)
            m_sc[g] = m_new

    @pl.when(ki < kmax)
    def _full():
        step(False)

    @pl.when(ki == kmax)
    def _last():
        step(True)
        for g in range(NSA_KV_HEADS):
            acc = acc_sc[g]
            _store_heads(o_ref, acc / pltpu.roll(acc, HEAD_DIM, 1), g, tq)


def _flash_sel(q8, rows3, selb):
    b, l, _ = rows3.shape
    tq, tk = min(SEL_TQ, l), min(SEL_TK, l)
    nq = l // tq
    kv_idx = lambda blk: (lambda bi, qi, ki: (bi, jnp.minimum(ki, (qi * tq + tq - 1) // tk), blk))
    return pl.pallas_call(
        functools.partial(_flash_sel_kernel, tq=tq, tk=tk),
        grid=(b, nq, l // tk),
        in_specs=[pl.BlockSpec((NSA_HEADS, tq, LANES), lambda bi, qi, ki: (0, bi * nq + qi, 0)),
                  pl.BlockSpec((1, tk, KV_W), kv_idx(2)),
                  pl.BlockSpec((1, tk, KV_W), kv_idx(3)),
                  pl.BlockSpec((1, tq, LANES), lambda bi, qi, ki: (bi, qi, 0))],
        out_specs=pl.BlockSpec((1, tq, NSA_WIDTH), lambda bi, qi, ki: (bi, qi, 0)),
        out_shape=jax.ShapeDtypeStruct((b, l, NSA_WIDTH), F32),
        scratch_shapes=[pltpu.VMEM((NSA_KV_HEADS, NSA_HPG * tq, LANES), F32),
                        pltpu.VMEM((NSA_KV_HEADS, NSA_HPG * tq, LANES), F32)],
        compiler_params=_cparams("parallel", "parallel", "arbitrary"),
        name="flash_sel",
    )(q8, rows3, rows3, selb)


def _flash_win_kernel(q_ref, *refs, tq, back):
    nspan = back + 1
    k_refs, v_refs, o_ref = refs[:nspan], refs[nspan:2 * nspan], refs[2 * nspan]
    qi = pl.program_id(1)
    hq = NSA_HPG * tq
    span = nspan * tq
    qpos = qi * tq + lax.broadcasted_iota(jnp.int32, (tq, span), 0)
    kpos = (qi - back) * tq + lax.broadcasted_iota(jnp.int32, (tq, span), 1)
    ok = (kpos <= qpos) & (kpos > qpos - WINDOW) & (kpos >= 0)
    bias4 = jnp.concatenate([jnp.where(ok, 0.0, NEG_BIG)] * NSA_HPG, axis=0)
    kcat = jnp.concatenate([r[0] for r in k_refs], axis=0).astype(BF16)
    vcat = jnp.concatenate([r[0] for r in v_refs], axis=0).astype(BF16)
    for g in range(NSA_KV_HEADS):
        q4 = q_ref[g * NSA_HPG:(g + 1) * NSA_HPG].reshape(hq, LANES)
        qx = jnp.where(_group_lanes((hq, LANES), g), q4, 0.0)
        s = lax.dot_general(qx, kcat, (((1,), (1,)), ((), ())), preferred_element_type=F32) + bias4
        p = jnp.exp2(s - jnp.max(s, axis=-1, keepdims=True)).astype(BF16)
        vx = jnp.where(_group_lanes((span, LANES), g), vcat, 1.0)
        r = jnp.dot(p, vx, preferred_element_type=F32)
        _store_heads(o_ref, r / pltpu.roll(r, HEAD_DIM, 1), g, tq)


def _flash_win(q8, win3):
    b, l, _ = win3.shape
    tq = min(WIN_TQ, l)
    nq = l // tq
    back = -(-(WINDOW - 1) // tq)

    def kv_specs(blk):
        return [pl.BlockSpec((1, tq, KV_W), (lambda bi, qi, j=j: (bi, jnp.maximum(qi - back + j, 0), blk)))
                for j in range(back + 1)]

    return pl.pallas_call(
        functools.partial(_flash_win_kernel, tq=tq, back=back),
        grid=(b, nq),
        in_specs=[pl.BlockSpec((NSA_HEADS, tq, LANES), lambda bi, qi: (0, bi * nq + qi, 0))]
        + kv_specs(0) + kv_specs(1),
        out_specs=pl.BlockSpec((1, tq, NSA_WIDTH), lambda bi, qi: (bi, qi, 0)),
        out_shape=jax.ShapeDtypeStruct((b, l, NSA_WIDTH), F32),
        compiler_params=_cparams("parallel", "parallel"),
        name="flash_win",
    )(q8, *([win3] * (2 * (back + 1))))


def _softmax_update(s, m_prev, l_prev):
    m_new = jnp.maximum(m_prev, jnp.max(s, axis=-1, keepdims=True))
    alpha = jnp.exp2(m_prev - m_new)
    p = jnp.exp2(s - m_new[:, :1])
    return m_new, alpha, alpha * l_prev + jnp.sum(p, axis=-1, keepdims=True), p


def _paged_attn_kernel(pt_ref, *refs, npg, page, l_new, nbp):
    page_refs = refs[:npg]
    (qz_ref, sel_ref, kn_ref, vn_ref, win_ref, kwn_ref, vwn_ref,
     oslc_ref, owin_ref, wout_ref, m_sc, l_sc, acc_sc) = refs[npg:]
    ji = pl.program_id(1)
    nr = qz_ref.shape[1]
    span = npg * page

    @pl.when(ji == 0)
    def _init():
        m_sc[...] = jnp.full(m_sc.shape, NEG_BIG, F32)
        l_sc[...] = jnp.zeros(l_sc.shape, F32)
        acc_sc[...] = jnp.zeros(acc_sc.shape, F32)

    qz = qz_ref[0]
    kt = jnp.concatenate([page_refs[i][0, 0:KV_W, :] for i in range(npg)], axis=1).astype(BF16)
    vt = jnp.concatenate([page_refs[i][0, KV_W:2 * KV_W, :] for i in range(npg)], axis=1).astype(BF16)
    s = jnp.dot(qz, kt, preferred_element_type=F32)
    e_row = lax.broadcasted_iota(jnp.int32, (nbp, span), 0)
    e_col = lax.broadcasted_iota(jnp.int32, (nbp, span), 1)
    expand = jnp.where(e_row == ji * (span // SEL_BLOCK) + e_col // SEL_BLOCK, 1.0, 0.0).astype(BF16)
    picked = jnp.dot(sel_ref[0], expand, preferred_element_type=F32) > 0.5
    m_new, alpha, l_new_v, p = _softmax_update(jnp.where(picked, s, NEG_BIG), m_sc[...], l_sc[...])
    m_sc[...] = m_new
    l_sc[...] = l_new_v
    acc_sc[...] = alpha * acc_sc[...] + _dot_nt(p, vt)

    @pl.when(ji == pl.num_programs(1) - 1)
    def _fin():
        q_of_row = lax.broadcasted_iota(jnp.int32, (nr, l_new), 0) % l_new
        j_new = lax.broadcasted_iota(jnp.int32, (nr, l_new), 1)
        new_ok = j_new <= q_of_row
        sn = jnp.where(new_ok, _dot_nt(qz, kn_ref[0]), NEG_BIG)
        m2, a2, l2, p2 = _softmax_update(sn, m_sc[...], l_sc[...])
        oslc_ref[0] = (a2 * acc_sc[...] + _dot(p2, vn_ref[0])) / l2
        wlen = win_ref.shape[2]
        kw = win_ref[0, 0:KV_W, :]
        vw = win_ref[0, KV_W:2 * KV_W, :]
        i_old = lax.broadcasted_iota(jnp.int32, (nr, wlen), 1)
        q_old = lax.broadcasted_iota(jnp.int32, (nr, wlen), 0) % l_new
        sw = jnp.where(i_old + (WINDOW - wlen) > q_old, _dot(qz, kw), NEG_BIG)
        swn = jnp.where(new_ok, _dot_nt(qz, kwn_ref[0]), NEG_BIG)
        mw = jnp.maximum(jnp.max(sw, axis=-1, keepdims=True), jnp.max(swn, axis=-1, keepdims=True))
        pw = jnp.exp2(sw - mw)
        pwn = jnp.exp2(swn - mw)
        lw = jnp.sum(pw, axis=-1, keepdims=True) + jnp.sum(pwn, axis=-1, keepdims=True)
        owin_ref[0] = (_dot_nt(pw, vw) + _dot(pwn, vwn_ref[0])) / lw
        new_t = jnp.concatenate([kwn_ref[0], vwn_ref[0]], axis=1).T
        place = (lax.broadcasted_iota(jnp.int32, (l_new, wlen), 1)
                 == lax.broadcasted_iota(jnp.int32, (l_new, wlen), 0) + (wlen - l_new)).astype(F32)
        lane = lax.broadcasted_iota(jnp.int32, (2 * KV_W, wlen), 1)
        wout_ref[0] = jnp.where(lane < wlen - l_new, pltpu.roll(win_ref[0], wlen - l_new, 1), _dot_f32(new_t, place))


def _paged_attn(cache_t, page_table, pool_off, qz, sel_rows, rows3, win_t, win_off, win3):
    b, n_pages = page_table.shape
    page = cache_t.shape[2]
    npg = PAGES_PER_STEP
    nr = qz.shape[1]
    l_new = rows3.shape[1]
    nbp = sel_rows.shape[2]
    wlen = win_t.shape[2]
    kern = functools.partial(_paged_attn_kernel, npg=npg, page=page, l_new=l_new, nbp=nbp)
    per_b = lambda bi, ji, pt: (bi, 0, 0)
    return pl.pallas_call(
        kern,
        grid_spec=pltpu.PrefetchScalarGridSpec(
            num_scalar_prefetch=1, grid=(b, n_pages // npg),
            in_specs=_page_specs(npg, page, 1, pool_off)
            + [pl.BlockSpec((1, nr, KV_W), per_b),
               pl.BlockSpec((1, nr, nbp), per_b),
               pl.BlockSpec((1, l_new, KV_W), lambda bi, ji, pt: (bi, 0, 2)),
               pl.BlockSpec((1, l_new, KV_W), lambda bi, ji, pt: (bi, 0, 3)),
               pl.BlockSpec((1, 2 * KV_W, wlen), lambda bi, ji, pt: (bi + win_off, 0, 0)),
               pl.BlockSpec((1, l_new, KV_W), lambda bi, ji, pt: (bi, 0, 0)),
               pl.BlockSpec((1, l_new, KV_W), lambda bi, ji, pt: (bi, 0, 1))],
            out_specs=[pl.BlockSpec((1, nr, KV_W), per_b),
                       pl.BlockSpec((1, nr, KV_W), per_b),
                       pl.BlockSpec((1, 2 * KV_W, wlen), per_b)],
            scratch_shapes=[pltpu.VMEM((nr, LANES), F32), pltpu.VMEM((nr, LANES), F32),
                            pltpu.VMEM((nr, KV_W), F32)]),
        out_shape=[jax.ShapeDtypeStruct((b, nr, KV_W), F32),
                   jax.ShapeDtypeStruct((b, nr, KV_W), F32),
                   jax.ShapeDtypeStruct((b, 2 * KV_W, wlen), F32)],
        compiler_params=_cparams("parallel", "arbitrary"),
        name="paged_attn",
    )(page_table, *([cache_t] * npg), qz, sel_rows, rows3, rows3, win_t, win3, win3)


def _nsa_gate_kernel(oc_ref, os_ref, ow_ref, gz_ref, e_ref, o_ref):
    gate = _sigmoid(gz_ref[:, 0:LANES])
    o = (_dot_split(gate, e_ref[0]) * oc_ref[...] + _dot_split(gate, e_ref[1]) * os_ref[...]
         + _dot_split(gate, e_ref[2]) * ow_ref[...])
    o_ref[...] = o * _silu(gz_ref[:, LANES:LANES + NSA_WIDTH])


def _gate_expand():
    e = np.zeros((3, LANES, NSA_WIDTH), np.float32)
    for j in range(3):
        for h in range(NSA_HEADS):
            e[j, j * NSA_HEADS + h, h * HEAD_DIM:(h + 1) * HEAD_DIM] = 1.0
    return jnp.asarray(e, dtype=BF16)


def _nsa_gate(o_cmp, o_slc, o_win, u_nsa, tm):
    rows = o_cmp.shape[0]
    gz_w = LANES + NSA_WIDTH
    gz_blk = (NSA_WIDTH + 6 * KV_W) // gz_w
    assert gz_blk * gz_w == NSA_WIDTH + 6 * KV_W
    row = lambda i: (i, 0)
    return pl.pallas_call(
        _nsa_gate_kernel,
        grid=(rows // tm,),
        in_specs=[pl.BlockSpec((tm, NSA_WIDTH), row),
                  pl.BlockSpec((tm, NSA_WIDTH), row),
                  pl.BlockSpec((tm, NSA_WIDTH), row),
                  pl.BlockSpec((tm, gz_w), lambda i: (i, gz_blk)),
                  pl.BlockSpec((3, LANES, NSA_WIDTH), lambda i: (0, 0, 0))],
        out_specs=pl.BlockSpec((tm, NSA_WIDTH), row),
        out_shape=jax.ShapeDtypeStruct((rows, NSA_WIDTH), F32),
        compiler_params=_cparams("parallel"),
        name="nsa_gate",
    )(o_cmp, o_slc, o_win, u_nsa, _gate_expand())


def _rms_norm(x, g, eps=EPS):
    xf = x.astype(F32)
    y = xf * lax.rsqrt(jnp.mean(xf * xf, axis=-1, keepdims=True) + eps)
    return (y * g.astype(F32)).astype(x.dtype)


def _rope(x, pos):
    half = ROT_DIM // 2
    inv = jnp.exp(-math.log(ROPE_THETA) * jnp.arange(half, dtype=F32) * 2.0 / ROT_DIM)
    ang = pos.astype(F32)[:, None] * inv[None, :]
    cos = jnp.cos(ang)[:, None, :]
    sin = jnp.sin(ang)[:, None, :]
    xf = x.astype(F32)
    x1, x2, rest = xf[..., :half], xf[..., half:ROT_DIM], xf[..., ROT_DIM:]
    return jnp.concatenate([x1 * cos - x2 * sin, x2 * cos + x1 * sin, rest], axis=-1).astype(x.dtype)


def _masked_softmax(s, mask):
    s = jnp.where(mask, s, -jnp.inf)
    m = jnp.max(s, axis=-1, keepdims=True)
    m = jnp.where(jnp.isfinite(m), m, 0.0)
    e = jnp.where(mask, jnp.exp(s - m), 0.0)
    return e / jnp.maximum(jnp.sum(e, axis=-1, keepdims=True), 1e-30)


def _cmp_summaries(x, pe, w):
    b, t, g, d = x.shape
    n_half = t // CMP_STRIDE
    halves = x[:, :n_half * CMP_STRIDE].astype(F32).reshape(b, n_half, CMP_STRIDE, g, d)
    wf = w.astype(F32)
    first = jnp.einsum('bnsgd,sde->bnge', halves[:, :-1], wf[:CMP_STRIDE])
    second = jnp.einsum('bnsgd,sde->bnge', halves[:, 1:], wf[CMP_STRIDE:])
    bias = jnp.einsum('sd,sde->e', pe.astype(F32), wf)
    return first + second + bias


def _overlap_matrix(n_cmp, n_sel):
    a = SEL_BLOCK // CMP_STRIDE
    bb = CMP_BLOCK // CMP_STRIDE
    i = jnp.arange(n_cmp)[:, None]
    j = jnp.arange(n_sel)[None, :]
    s = i - a * j + (bb - 1)
    cnt = jnp.minimum(jnp.minimum(s + 1, a + bb - 1 - s), min(a, bb))
    return jnp.maximum(cnt, 0).astype(F32)


def _nsa_selected_gather(q, k_hist, v_hist, idx, pos):
    b, l = q.shape[:2]
    t_len = k_hist.shape[1]
    n_sel = -(-t_len // SEL_BLOCK)
    n_k = idx.shape[-1]
    pad = n_sel * SEL_BLOCK - t_len

    def blocks(x):
        x = jnp.pad(x.astype(F32), ((0, 0), (0, pad), (0, 0), (0, 0)))
        return x.reshape(b, n_sel, SEL_BLOCK, NSA_KV_HEADS, HEAD_DIM).transpose(0, 3, 1, 2, 4)

    kb, vb = blocks(k_hist), blocks(v_hist)
    qb = math.gcd(l, Q_BLOCK)
    nq = l // qb
    qx = q.astype(F32).reshape(b * nq, qb, NSA_KV_HEADS, NSA_HPG, HEAD_DIM)
    ix = idx.reshape(b, NSA_KV_HEADS, nq, qb, n_k).transpose(0, 2, 1, 3, 4).reshape(b * nq, NSA_KV_HEADS, qb, n_k)
    px = jnp.tile(pos.reshape(nq, qb), (b, 1))
    bx = jnp.repeat(jnp.arange(b), nq)
    g_ix = jnp.arange(NSA_KV_HEADS)[:, None, None]
    offs = jnp.arange(SEL_BLOCK)
    n_keys = n_k * SEL_BLOCK

    def attend(args):
        qi, ii, pi, bi = args
        ks = kb[bi][g_ix, ii]
        vs = vb[bi][g_ix, ii]
        kpos = ii[..., None] * SEL_BLOCK + offs
        mask = (kpos <= pi[None, :, None, None]).reshape(NSA_KV_HEADS, 1, qb, n_keys)
        s = jnp.einsum('qghd,gqkjd->ghqkj', qi, ks).reshape(NSA_KV_HEADS, NSA_HPG, qb, n_keys) * ATTN_SCALE
        p = _masked_softmax(s, mask)
        return jnp.einsum('ghqn,gqnd->qghd', p, vs.reshape(NSA_KV_HEADS, qb, n_keys, HEAD_DIM))

    o = lax.map(attend, (qx, ix, px, bx))
    return o.reshape(b, l, NSA_HEADS, HEAD_DIM)


def _nsa_window(q, k_ctx, v_ctx, pos0):
    b, l = q.shape[:2]
    lc = k_ctx.shape[1]
    padw = ((0, 0), (WINDOW, 0), (0, 0), (0, 0))
    kp = jnp.pad(k_ctx.astype(F32), padw)
    vp = jnp.pad(v_ctx.astype(F32), padw)
    qb = math.gcd(l, Q_BLOCK)
    nq = l // qb
    span = WINDOW + qb
    qx = jnp.moveaxis(q.astype(F32).reshape(b, nq, qb, NSA_KV_HEADS, NSA_HPG, HEAD_DIM), 1, 0)
    starts = jnp.arange(nq) * qb
    first_pos = pos0 + l - lc

    def attend(args):
        qi, i0 = args
        kk = lax.dynamic_slice_in_dim(kp, i0 + lc - l, span, axis=1)
        vv = lax.dynamic_slice_in_dim(vp, i0 + lc - l, span, axis=1)
        kpos = pos0 + i0 - WINDOW + jnp.arange(span)
        qpos = pos0 + i0 + jnp.arange(qb)
        mask = ((kpos[None, :] >= first_pos) & (kpos[None, :] <= qpos[:, None])
                & (kpos[None, :] > qpos[:, None] - WINDOW))
        s = jnp.einsum('bqghd,bkgd->bghqk', qi, kk) * ATTN_SCALE
        p = _masked_softmax(s, mask)
        return jnp.einsum('bghqk,bkgd->bqghd', p, vv)

    o = lax.map(attend, (qx, starts))
    return jnp.moveaxis(o, 0, 1).reshape(b, l, NSA_HEADS, HEAD_DIM)


def _nsa_mixer_jax(n_q, n_kv, n_g, past_kv, win_buf, win_keep, pos0, q_g, k_g, cmp_pos, cmp_w):
    b, l, _ = n_q.shape
    pos = pos0 + jnp.arange(l)
    q = _rms_norm(n_q.reshape(b, l, NSA_HEADS, HEAD_DIM), q_g)
    q_rot = _rope(q, pos)
    kv = n_kv.reshape(b, l, 6, NSA_KV_HEADS, HEAD_DIM)
    k_slc = _rope(_rms_norm(kv[:, :, 2], k_g[1]), pos)
    k_win = _rope(_rms_norm(kv[:, :, 4], k_g[2]), pos)
    new_rows = jnp.stack([kv[:, :, 0], kv[:, :, 1], k_slc, kv[:, :, 3]], axis=2)
    new_win = jnp.stack([k_win, kv[:, :, 5]], axis=2)
    hist = jnp.concatenate([past_kv.astype(new_rows.dtype), new_rows], axis=1)
    ctx = jnp.concatenate([win_buf.astype(new_win.dtype), new_win], axis=1)
    t_len = hist.shape[1]

    kc = _rms_norm(_cmp_summaries(hist[:, :, 0], cmp_pos[0], cmp_w[0]), k_g[0])
    vc = _cmp_summaries(hist[:, :, 1], cmp_pos[1], cmp_w[1])
    n_cmp = kc.shape[1]
    qg = q.astype(F32).reshape(b, l, NSA_KV_HEADS, NSA_HPG, HEAD_DIM)
    s = jnp.einsum('blghd,bngd->bghln', qg, kc) * ATTN_SCALE
    cmp_end = jnp.arange(n_cmp) * CMP_STRIDE + CMP_BLOCK - 1
    p_cmp = _masked_softmax(s, cmp_end[None, :] <= pos[:, None])
    o_cmp = jnp.einsum('bghln,bngd->blghd', p_cmp, vc).reshape(b, l, NSA_HEADS, HEAD_DIM)

    n_sel = -(-t_len // SEL_BLOCK)
    imp = jnp.einsum('bghln,nj->bglj', p_cmp, _overlap_matrix(n_cmp, n_sel))
    blk = jnp.arange(n_sel)[None, :]
    cur = (pos // SEL_BLOCK)[:, None]
    valid = blk * SEL_BLOCK <= pos[:, None]
    forced = (blk == 0) | (blk == cur) | (blk == cur - 1)
    score = jnp.where(forced, jnp.inf, jnp.where(valid, imp, -jnp.inf))
    _, idx = lax.top_k(score, min(N_SELECT, n_sel))
    o_slc = _nsa_selected_gather(q_rot, hist[:, :, 2], hist[:, :, 3], idx, pos)
    o_win = _nsa_window(q_rot, ctx[:, :, 0], ctx[:, :, 1], pos0)

    gate = jax.nn.sigmoid(n_g.astype(F32)).reshape(b, l, 3, NSA_HEADS, 1)
    o = gate[:, :, 0] * o_cmp + gate[:, :, 1] * o_slc + gate[:, :, 2] * o_win
    return o.reshape(b, l, NSA_WIDTH), new_rows, ctx[:, ctx.shape[1] - win_keep:]


def _nsa_fresh(u_nsa, b, l, win_keep, q_g, k_g, cmp_pos, cmp_w, tm):
    qn, qr, rows, win = _nsa_prep(u_nsa, jnp.arange(l), q_g, k_g, l)
    rows3 = rows.reshape(b, l, N_KV_SLOTS * KV_W)
    win3 = win.reshape(b, l, 2 * KV_W)
    wcat, bias = _cmp_weights(cmp_pos, cmp_w)
    t_use = (l // CMP_STRIDE) * CMP_STRIDE
    kc, vc = _cmp_kv(rows3, 0, 1, t_use, wcat, bias, k_g[0])
    o_cmp, sel_t = _cmp_attn(qn.reshape(b, l, NSA_WIDTH), kc, vc, 0, l)
    nbp = sel_t.shape[1] // NSA_KV_HEADS
    assert NSA_KV_HEADS == 2 and nbp <= HEAD_DIM
    sb = jnp.where(sel_t.reshape(b, NSA_KV_HEADS, nbp, l) > 0.5, 0.0, NEG_BIG)
    sb = jnp.pad(sb, ((0, 0), (0, 0), (0, HEAD_DIM - nbp), (0, 0)), constant_values=NEG_BIG)[:, ::-1]
    selb = jnp.swapaxes(sb.reshape(b, LANES, l), 1, 2).astype(BF16)
    o_slc = _flash_sel(qr, rows3, selb)
    o_win = _flash_win(qr, win3)
    o_b = _nsa_gate(o_cmp.reshape(b * l, NSA_WIDTH), o_slc.reshape(b * l, NSA_WIDTH),
                    o_win.reshape(b * l, NSA_WIDTH), u_nsa, tm)
    new_rows = rows3.reshape(b, l, N_KV_SLOTS, NSA_KV_HEADS, HEAD_DIM)
    new_win = win3[:, l - win_keep:].reshape(b, win_keep, 2, NSA_KV_HEADS, HEAD_DIM)
    return o_b, new_rows, new_win


def _nsa_paged(u_nsa, b, l, paged, q_g, k_g, cmp_pos, cmp_w, tm):
    cache_t, page_table, pool_off, win_t, win_off = paged
    past_len = page_table.shape[1] * cache_t.shape[2]
    assert (past_len + l) // CMP_STRIDE == past_len // CMP_STRIDE and past_len % SEL_BLOCK == 0
    qn, qr, rows, win = _nsa_prep(u_nsa, past_len + jnp.arange(l), q_g, k_g, l)
    rows3 = rows.reshape(b, l, N_KV_SLOTS * KV_W)
    win3 = win.reshape(b, l, 2 * KV_W)
    wcat, bias = _cmp_weights(cmp_pos, cmp_w)
    kc, vc = _cmp_kv_paged(cache_t, page_table, pool_off, wcat, bias, k_g[0])
    o_cmp, sel_t = _cmp_attn(qn.reshape(b, l, NSA_WIDTH), kc, vc, past_len, past_len + l)
    nbp = sel_t.shape[1] // NSA_KV_HEADS
    sel_rows = jnp.swapaxes(sel_t.reshape(b, NSA_KV_HEADS, 1, nbp, l), 3, 4)
    sel_rows = jnp.broadcast_to(sel_rows, (b, NSA_KV_HEADS, NSA_HPG, l, nbp)).reshape(b, NSA_HEADS * l, nbp)
    q5 = jnp.transpose(qr[:, :, :HEAD_DIM].reshape(NSA_KV_HEADS, NSA_HPG, b, l, HEAD_DIM), (2, 0, 1, 3, 4))
    qz = jnp.einsum('bghqd,gk->bghqkd', q5, jnp.eye(NSA_KV_HEADS, dtype=q5.dtype)).reshape(b, NSA_HEADS * l, KV_W)
    o_slc_z, o_win_z, wout = _paged_attn(cache_t, page_table, pool_off, qz, sel_rows, rows3, win_t, win_off, win3)

    def own_group(o):
        o6 = o.reshape(b, NSA_KV_HEADS, NSA_HPG, l, NSA_KV_HEADS, HEAD_DIM)
        d = jnp.stack([o6[:, g, :, :, g, :] for g in range(NSA_KV_HEADS)], axis=1)
        return jnp.transpose(d, (0, 3, 1, 2, 4)).reshape(b * l, NSA_WIDTH)

    o_b = _nsa_gate(o_cmp.reshape(b * l, NSA_WIDTH), own_group(o_slc_z), own_group(o_win_z), u_nsa, min(tm, b * l))
    new_rows = rows3.reshape(b, l, N_KV_SLOTS, NSA_KV_HEADS, HEAD_DIM)
    wlen = wout.shape[2]
    new_win = jnp.transpose(wout.reshape(b, 2, NSA_KV_HEADS, HEAD_DIM, wlen), (0, 4, 1, 2, 3))
    return o_b, new_rows, new_win


def _hybrid_layer(x, paged, win_keep, s_gla, c_ml, n_ml, m_ml, conv_ml,
                  norm_g, w_in_pad, w_out_bf, gla_w_gate, gla_b_gate, gla_norm_g,
                  nsa_q_norm_g, nsa_k_norm_g, nsa_cmp_pos, nsa_cmp_w,
                  ml_conv_w, ml_conv_b, ml_gate_b, ml_norm_g):
    b, l, _ = x.shape
    rows = b * l
    tm = 256
    x2d = x.reshape(rows, D_MODEL)
    u_gla, u_nsa, u_ml = _proj_in(x2d, norm_g, w_in_pad, tm)

    nk = GLA_HEADS * GLA_DK
    wg = jnp.pad(gla_w_gate.astype(F32), ((0, LANES - GLA_RANK), (0, 0))).astype(BF16)
    o_a, st_new = _gla(u_gla, _gla_state_in(s_gla), wg, gla_b_gate.astype(F32).reshape(1, nk),
                       jnp.tile(gla_norm_g.astype(F32).reshape(1, GLA_DV), (1, GLA_HEADS)), b, l)
    s_new = _gla_state_out(st_new)

    if paged is None:
        o_b, new_rows, new_win = _nsa_fresh(u_nsa, b, l, win_keep, nsa_q_norm_g, nsa_k_norm_g,
                                            nsa_cmp_pos, nsa_cmp_w, tm)
    else:
        o_b, new_rows, new_win = _nsa_paged(u_nsa, b, l, paged, nsa_q_norm_g, nsa_k_norm_g,
                                            nsa_cmp_pos, nsa_cmp_w, tm)

    o_c, c_new, n_new, m_new, conv_new = _mlstm(u_ml, c_ml, n_ml, m_ml, conv_ml, ml_conv_w, ml_conv_b,
                                                ml_gate_b, ml_norm_g, b, l)

    y = _proj_out(o_a.reshape(rows, GLA_WIDTH), o_b, o_c.reshape(rows, ML_WIDTH), x2d, w_out_bf, tm)
    return y.reshape(b, l, D_MODEL), new_rows, new_win, s_new, c_new, n_new, m_new, conv_new


def kernel(x_prompt, x_sample, cache_nsa_kv, state_nsa_win, state_gla, state_mlstm_C, state_mlstm_n,
           state_mlstm_m, state_mlstm_conv, page_table, norm_g, w_in, w_out, gla_w_gate, gla_b_gate,
           gla_norm_g, nsa_q_norm_g, nsa_k_norm_g, nsa_cmp_pos, nsa_cmp_w, ml_conv_w, ml_conv_b,
           ml_gate_b, ml_norm_g):
    bp, sp, _ = x_prompt.shape
    bs, _, _ = x_sample.shape
    depth = w_in.shape[0]
    n_pages = page_table.shape[1]
    past_len = n_pages * cache_nsa_kv.shape[2]
    dt = x_prompt.dtype
    zero_gla = jnp.zeros((bp, GLA_HEADS, GLA_DK, GLA_DV), F32)
    zero_c = jnp.zeros((bp, ML_HEADS, ML_DH, ML_DH), F32)
    zero_n = jnp.zeros((bp, ML_HEADS, ML_DH), F32)
    zero_m = jnp.zeros((bp, ML_HEADS), F32)
    zero_conv = jnp.zeros((bp, CONV_W - 1, 2 * ML_WIDTH), dt)
    keep_p = min(WINDOW, sp)
    keep_s = state_nsa_win.shape[2]
    n_pool = cache_nsa_kv.shape[1]
    cache_t = jnp.transpose(cache_nsa_kv, (0, 1, 3, 4, 5, 2)).reshape(
        depth * n_pool, N_KV_SLOTS * KV_W, cache_nsa_kv.shape[2]).astype(F32)
    win_t = jnp.transpose(state_nsa_win, (0, 1, 3, 4, 5, 2)).reshape(depth * bs, 2 * KV_W, keep_s).astype(F32)

    y_prompt, y_sample = x_prompt, x_sample
    p_layers, s_layers = [], []
    for layer in range(depth):
        w = (norm_g[layer], _pad_w_in(w_in[layer]), w_out[layer].astype(BF16), gla_w_gate[layer],
             gla_b_gate[layer], gla_norm_g[layer],
             nsa_q_norm_g[layer], nsa_k_norm_g[layer], nsa_cmp_pos[layer], nsa_cmp_w[layer],
             ml_conv_w[layer], ml_conv_b[layer], ml_gate_b[layer], ml_norm_g[layer])
        y_prompt, *p_new = _hybrid_layer(y_prompt, None, keep_p, zero_gla, zero_c, zero_n,
                                         zero_m, zero_conv, *w)
        paged = (cache_t, page_table, layer * n_pool, win_t, layer * bs)
        y_sample, *s_new = _hybrid_layer(y_sample, paged, keep_s,
                                         state_gla[layer], state_mlstm_C[layer], state_mlstm_n[layer],
                                         state_mlstm_m[layer], state_mlstm_conv[layer], *w)
        p_layers.append(p_new)
        s_layers.append(s_new)
    p_kv, p_win, p_gla, p_c, p_n, p_m, p_conv = [jnp.stack(z) for z in zip(*p_layers)]
    s_kv, s_win, s_gla, s_c, s_n, s_m, s_conv = [jnp.stack(z) for z in zip(*s_layers)]
    return (y_prompt, y_sample, p_kv, s_kv, p_win, s_win, p_gla, s_gla, p_c, s_c, p_n, s_n, p_m, s_m, p_conv, s_conv)
```

```python
import functools
import math

import jax
import jax.numpy as jnp
import numpy as np
from jax import lax
from jax.experimental import pallas as pl
from jax.experimental.pallas import tpu as pltpu

F32 = jnp.float32
BF16 = jnp.bfloat16
HIGHEST = lax.Precision.HIGHEST

D_MODEL = 1024
HEAD_DIM = 64
GLA_WIDTH = D_MODEL // 4
NSA_WIDTH = D_MODEL // 2
ML_WIDTH = D_MODEL - GLA_WIDTH - NSA_WIDTH
D_MIX = GLA_WIDTH + NSA_WIDTH + ML_WIDTH

GLA_HEADS = GLA_WIDTH // HEAD_DIM
GLA_DK = HEAD_DIM // 2
GLA_DV = HEAD_DIM
GLA_RANK = 16
GLA_TAU = 16.0
GLA_CHUNK = 64

NSA_HEADS = NSA_WIDTH // HEAD_DIM
NSA_KV_HEADS = 2
NSA_HPG = NSA_HEADS // NSA_KV_HEADS
CMP_BLOCK = 32
CMP_STRIDE = 16
SEL_BLOCK = 64
N_SELECT = 16
WINDOW = 512
Q_BLOCK = 128
N_KV_SLOTS = 4
ROT_DIM = HEAD_DIM // 4
ROPE_THETA = 500000.0
ATTN_SCALE = HEAD_DIM ** -0.5

ML_HEADS = ML_WIDTH // HEAD_DIM
ML_DH = HEAD_DIM
ML_CHUNK = 64
CONV_W = 4

SPLIT_SIZES = (GLA_HEADS * GLA_DK, GLA_HEADS * GLA_DK, GLA_WIDTH, GLA_RANK, GLA_WIDTH,
               NSA_WIDTH, 6 * NSA_KV_HEADS * HEAD_DIM, 3 * NSA_HEADS, NSA_WIDTH,
               2 * ML_WIDTH, ML_WIDTH, 2 * ML_HEADS, ML_WIDTH, ML_WIDTH)

LANES = 128
SUBLANES = 8
VMEM_LIMIT = 56 * 1024 * 1024
NEG_BIG = -1e30
EPS = 1e-6
LOG2E = math.log2(math.e)


def _round_up(n, m):
    return -(-n // m) * m


PAD_SIZES = tuple(_round_up(s, LANES) for s in SPLIT_SIZES)
D_IN_PAD = sum(PAD_SIZES)
W_GLA = sum(PAD_SIZES[0:5])
W_NSA = sum(PAD_SIZES[5:9])
W_ML = sum(PAD_SIZES[9:14])
KV_W = NSA_KV_HEADS * HEAD_DIM


def _dot(a, b):
    return jnp.dot(a.astype(BF16), b.astype(BF16), preferred_element_type=F32)


def _dot_nt(a, b):
    return lax.dot_general(a.astype(BF16), b.astype(BF16), (((1,), (1,)), ((), ())), preferred_element_type=F32)


def _dot_tn(a, b):
    return lax.dot_general(a.astype(BF16), b.astype(BF16), (((0,), (0,)), ((), ())), preferred_element_type=F32)


def _dot_f32(a, b):
    return jnp.dot(a, b, precision=HIGHEST, preferred_element_type=F32)


def _dot_nt_f32(a, b):
    return lax.dot_general(a, b, (((1,), (1,)), ((), ())), precision=HIGHEST, preferred_element_type=F32)


def _dot_split(a, b):
    a_hi = a.astype(BF16)
    a_lo = (a - a_hi.astype(F32)).astype(BF16)
    bb = b.astype(BF16)
    return jnp.dot(a_hi, bb, preferred_element_type=F32) + jnp.dot(a_lo, bb, preferred_element_type=F32)


def _log_sigmoid(x):
    return jnp.minimum(x, 0.0) - jnp.log1p(jnp.exp(-jnp.abs(x)))


def _sigmoid(x):
    return 1.0 / (1.0 + jnp.exp(-x))


def _silu(x):
    return x * _sigmoid(x)


def _group_mean_matrix(width):
    g = np.kron(np.eye(width // HEAD_DIM, dtype=np.float32), np.full((HEAD_DIM, HEAD_DIM), 1.0 / HEAD_DIM, np.float32))
    return jnp.asarray(g, dtype=BF16)


def _group_norm(x, gmat, gain):
    ms = _dot_split(x * x, gmat)
    return x * lax.rsqrt(ms + EPS) * gain


def _cparams(*sem):
    return pltpu.CompilerParams(dimension_semantics=sem, vmem_limit_bytes=VMEM_LIMIT)


def _proj_in_kernel(x_ref, g_ref, w_ref, ug_ref, un_ref, um_ref):
    x = x_ref[...]
    y = x * lax.rsqrt(jnp.mean(x * x, axis=-1, keepdims=True) + EPS) * g_ref[...]
    r = jnp.dot(y.astype(BF16), w_ref[...], preferred_element_type=F32)
    ug_ref[...] = r[:, 0:W_GLA]
    un_ref[...] = r[:, W_GLA:W_GLA + W_NSA]
    um_ref[...] = r[:, W_GLA + W_NSA:D_IN_PAD]


def _proj_in(x2d, g, w_pad, tm):
    rows = x2d.shape[0]
    return pl.pallas_call(
        _proj_in_kernel,
        grid=(rows // tm,),
        in_specs=[pl.BlockSpec((tm, D_MODEL), lambda i: (i, 0)),
                  pl.BlockSpec((1, D_MODEL), lambda i: (0, 0)),
                  pl.BlockSpec((D_MODEL, D_IN_PAD), lambda i: (0, 0))],
        out_specs=[pl.BlockSpec((tm, W_GLA), lambda i: (i, 0)),
                   pl.BlockSpec((tm, W_NSA), lambda i: (i, 0)),
                   pl.BlockSpec((tm, W_ML), lambda i: (i, 0))],
        out_shape=[jax.ShapeDtypeStruct((rows, W_GLA), F32),
                   jax.ShapeDtypeStruct((rows, W_NSA), F32),
                   jax.ShapeDtypeStruct((rows, W_ML), F32)],
        compiler_params=_cparams("parallel"),
        name="proj_in",
    )(x2d, g.reshape(1, D_MODEL), w_pad)


def _pad_w_in(w_in):
    parts = []
    off = 0
    for s, p in zip(SPLIT_SIZES, PAD_SIZES):
        seg = w_in[:, off:off + s]
        if p != s:
            seg = jnp.pad(seg, ((0, 0), (0, p - s)))
        parts.append(seg)
        off += s
    return jnp.concatenate(parts, axis=1).astype(BF16)


def _proj_out_kernel(oa_ref, ob_ref, oc_ref, x_ref, w_ref, y_ref):
    y = x_ref[...]
    y = y + _dot(oa_ref[...], w_ref[0:GLA_WIDTH, :])
    y = y + _dot(ob_ref[...], w_ref[GLA_WIDTH:GLA_WIDTH + NSA_WIDTH, :])
    y = y + _dot(oc_ref[...], w_ref[GLA_WIDTH + NSA_WIDTH:D_MIX, :])
    y_ref[...] = y


def _proj_out(oa, ob, oc, x2d, w_bf, tm):
    rows = x2d.shape[0]
    return pl.pallas_call(
        _proj_out_kernel,
        grid=(rows // tm,),
        in_specs=[pl.BlockSpec((tm, GLA_WIDTH), lambda i: (i, 0)),
                  pl.BlockSpec((tm, NSA_WIDTH), lambda i: (i, 0)),
                  pl.BlockSpec((tm, ML_WIDTH), lambda i: (i, 0)),
                  pl.BlockSpec((tm, D_MODEL), lambda i: (i, 0)),
                  pl.BlockSpec((D_MIX, D_MODEL), lambda i: (0, 0))],
        out_specs=pl.BlockSpec((tm, D_MODEL), lambda i: (i, 0)),
        out_shape=jax.ShapeDtypeStruct((rows, D_MODEL), F32),
        compiler_params=_cparams("parallel"),
        name="proj_out",
    )(oa, ob, oc, x2d, w_bf)


def _gla_kernel(u_ref, st0_ref, wg_ref, bg_ref, ng_ref, gm_ref, o_ref, st_ref, *, bb, tl, c):
    @pl.when(pl.program_id(1) == 0)
    def _init():
        st_ref[...] = st0_ref[...]

    nk = GLA_HEADS * GLA_DK
    a_off = 2 * nk + GLA_WIDTH
    q, k, v, log_a = [], [], [], []
    for i in range(bb):
        q.append(u_ref[i, :, 0:nk] * (GLA_DK ** -0.5))
        k.append(u_ref[i, :, nk:2 * nk])
        v.append(u_ref[i, :, 2 * nk:2 * nk + GLA_WIDTH])
        pre = _dot(u_ref[i, :, a_off:a_off + LANES], wg_ref[...]) + bg_ref[...]
        log_a.append(_log_sigmoid(pre) * (1.0 / GLA_TAU))

    tril = (lax.broadcasted_iota(jnp.int32, (c, c), 0) >= lax.broadcasted_iota(jnp.int32, (c, c), 1)).astype(F32)
    hc = GLA_HEADS * c
    tri_h = (lax.broadcasted_iota(jnp.int32, (hc, c), 0) % c) >= lax.broadcasted_iota(jnp.int32, (hc, c), 1)
    k_head = lax.broadcasted_iota(jnp.int32, (1, nk), 1) // GLA_DK
    v_head = lax.broadcasted_iota(jnp.int32, (1, GLA_WIDTH), 1) // GLA_DV
    st_diag = (lax.broadcasted_iota(jnp.int32, (GLA_WIDTH, nk), 0) // GLA_DV
               == lax.broadcasted_iota(jnp.int32, (GLA_WIDTH, nk), 1) // GLA_DK)

    st = [st_ref[i] for i in range(bb)]
    outs = [[] for _ in range(bb)]
    for j in range(tl // c):
        sl = slice(j * c, (j + 1) * c)
        for i in range(bb):
            b = _dot_f32(tril, log_a[i][sl])
            blast = b[c - 1:c]
            qe = q[i][sl] * jnp.exp(b)
            ke = k[i][sl] * jnp.exp(-b)
            kl = k[i][sl] * jnp.exp(blast - b)
            vc = v[i][sl]
            qx = jnp.concatenate([jnp.where(k_head == h, qe, 0.0) for h in range(GLA_HEADS)], axis=0)
            a = jnp.where(tri_h, _dot_nt(qx, ke), 0.0)
            r = _dot(a, vc)
            o = _dot_nt(qe, st[i])
            for h in range(GLA_HEADS):
                o = o + jnp.where(v_head == h, r[h * c:(h + 1) * c], 0.0)
            st[i] = st[i] * jnp.exp(blast) + jnp.where(st_diag, _dot_tn(vc, kl), 0.0)
            outs[i].append(o)
    for i in range(bb):
        st_ref[i] = st[i]
        o = outs[i][0] if len(outs[i]) == 1 else jnp.concatenate(outs[i], axis=0)
        z = u_ref[i, :, a_off + LANES:a_off + LANES + GLA_WIDTH]
        o_ref[i] = _group_norm(o, gm_ref[...], ng_ref[...]) * _silu(z)


GLA_SEQS_PER_STEP = 4


def _gla(u_gla, st0, wg, bg, ng, b, l):
    tl = min(l, 128)
    c = min(l, 16)
    bb = math.gcd(b, GLA_SEQS_PER_STEP)
    nk = GLA_HEADS * GLA_DK
    u3 = u_gla.reshape(b, l, W_GLA)
    kern = functools.partial(_gla_kernel, bb=bb, tl=tl, c=c)
    return pl.pallas_call(
        kern,
        grid=(b // bb, l // tl),
        in_specs=[pl.BlockSpec((bb, tl, W_GLA), lambda bi, li: (bi, li, 0)),
                  pl.BlockSpec((bb, GLA_WIDTH, nk), lambda bi, li: (bi, 0, 0)),
                  pl.BlockSpec((LANES, nk), lambda bi, li: (0, 0)),
                  pl.BlockSpec((1, nk), lambda bi, li: (0, 0)),
                  pl.BlockSpec((1, GLA_WIDTH), lambda bi, li: (0, 0)),
                  pl.BlockSpec((GLA_WIDTH, GLA_WIDTH), lambda bi, li: (0, 0))],
        out_specs=[pl.BlockSpec((bb, tl, GLA_WIDTH), lambda bi, li: (bi, li, 0)),
                   pl.BlockSpec((bb, GLA_WIDTH, nk), lambda bi, li: (bi, 0, 0))],
        out_shape=[jax.ShapeDtypeStruct((b, l, GLA_WIDTH), F32),
                   jax.ShapeDtypeStruct((b, GLA_WIDTH, nk), F32)],
        compiler_params=_cparams("parallel", "arbitrary"),
        name="gla",
    )(u3, st0, wg, bg, ng, _group_mean_matrix(GLA_WIDTH))


def _gla_state_in(s):
    b = s.shape[0]
    st = jnp.swapaxes(s.astype(F32), 2, 3)
    eye = jnp.eye(GLA_HEADS, dtype=F32)
    full = st[:, :, :, None, :] * eye[None, :, None, :, None]
    return full.reshape(b, GLA_WIDTH, GLA_HEADS * GLA_DK)


def _gla_state_out(st):
    b = st.shape[0]
    full = st.reshape(b, GLA_HEADS, GLA_DV, GLA_HEADS, GLA_DK)
    diag = jnp.stack([full[:, h, :, h, :] for h in range(GLA_HEADS)], axis=1)
    return jnp.swapaxes(diag, 2, 3)


def _mlstm_kernel(u_ref, c0_ref, n0_ref, m0_ref, cv0_ref, cw_ref, cb_ref, gb_ref, ng_ref,
                  o_ref, c_ref, n_ref, m_ref, cv_ref, xp_sc, *, bb, tl, c):
    @pl.when(pl.program_id(1) == 0)
    def _init():
        c_ref[...] = c0_ref[...]
        n_ref[...] = n0_ref[...]
        m_ref[...] = m0_ref[...]
        xp_sc[:, 0:SUBLANES, :] = cv0_ref[...]

    for i in range(bb):
        _mlstm_seq(i, u_ref, cw_ref, cb_ref, gb_ref, ng_ref, o_ref, c_ref, n_ref, m_ref, cv_ref, xp_sc, tl, c)


def _mlstm_seq(i, u_ref, cw_ref, cb_ref, gb_ref, ng_ref, o_ref, c_ref, n_ref, m_ref, cv_ref, xp_sc, tl, c):
    w2 = 2 * ML_WIDTH
    u = u_ref[i]
    xp_sc[i, SUBLANES:SUBLANES + tl, :] = u[:, 0:w2]
    conv = cb_ref[...]
    for w in range(CONV_W):
        off = SUBLANES - (CONV_W - 1) + w
        conv = conv + xp_sc[i, off:off + tl, :] * cw_ref[w:w + 1, :]
    tail = xp_sc[i, tl:tl + SUBLANES, :]
    xp_sc[i, 0:SUBLANES, :] = tail
    cv_ref[i] = tail

    qk = _silu(conv)
    mq = qk[:, 0:ML_WIDTH]
    mk = qk[:, ML_WIDTH:w2] * (ML_DH ** -0.5)
    mv = u[:, w2:w2 + ML_WIDTH]
    ifg = u[:, w2 + ML_WIDTH:w2 + ML_WIDTH + LANES] + gb_ref[...]
    logf = _log_sigmoid(ifg)
    og_off = w2 + ML_WIDTH + LANES
    og = _sigmoid(u[:, og_off:og_off + ML_WIDTH])
    zz = _silu(u[:, og_off + ML_WIDTH:og_off + 2 * ML_WIDTH])

    row = lax.broadcasted_iota(jnp.int32, (c, c), 0)
    col = lax.broadcasted_iota(jnp.int32, (c, c), 1)
    tri = row >= col
    tril = tri.astype(F32)
    lane = lax.broadcasted_iota(jnp.int32, (c, LANES), 1)

    cs = [c_ref[i, h] for h in range(ML_HEADS)]
    ns = [n_ref[i, h:h + 1, :] for h in range(ML_HEADS)]
    ms = [m_ref[i, h:h + 1, 0:1] for h in range(ML_HEADS)]
    for j in range(tl // c):
        sl = slice(j * c, (j + 1) * c)
        fcum_all = _dot_f32(tril, logf[sl])
        gates_t = jnp.where(lane < ML_HEADS, ifg[sl], fcum_all).T
        for h in range(ML_HEADS):
            hs = slice(h * ML_DH, (h + 1) * ML_DH)
            fc = fcum_all[:, ML_HEADS + h:ML_HEADS + h + 1]
            ii = ifg[sl, h:h + 1]
            dm = jnp.where(tri, fc - gates_t[ML_HEADS + h:ML_HEADS + h + 1, :] + gates_t[h:h + 1, :], NEG_BIG)
            inter = fc + ms[h]
            m = jnp.maximum(inter, jnp.max(dm, axis=-1, keepdims=True))
            w_int = jnp.exp(inter - m)
            qh, kh, vh = mq[sl, hs], mk[sl, hs], mv[sl, hs]
            sij = _dot_nt(qh, kh) * jnp.exp(dm - m)
            num = w_int * _dot_nt(qh, cs[h]) + _dot(sij, vh)
            den = w_int * jnp.sum(qh * ns[h], axis=-1, keepdims=True) + jnp.sum(sij, axis=-1, keepdims=True)
            hh = num / jnp.maximum(jnp.abs(den), jnp.exp(-m))
            m_last = m[c - 1:c]
            f_last = fc[c - 1:c]
            decay = jnp.exp(f_last + ms[h] - m_last)
            wj = jnp.exp(f_last - fc + ii - m_last)
            cs[h] = decay * cs[h] + _dot_tn(wj * vh, kh)
            ns[h] = decay * ns[h] + jnp.sum(wj * kh, axis=0, keepdims=True)
            ms[h] = m_last
            hn = hh * lax.rsqrt(jnp.mean(hh * hh, axis=-1, keepdims=True) + EPS) * ng_ref[...]
            o_ref[i, sl, hs] = hn * og[sl, hs] * zz[sl, hs]
    for h in range(ML_HEADS):
        c_ref[i, h] = cs[h]
        n_ref[i, h:h + 1, :] = ns[h]
        m_ref[i, h:h + 1, :] = jnp.broadcast_to(ms[h], (1, LANES))


ML_SEQS_PER_STEP = 2


def _mlstm(u_ml, c0, n0, m0, conv0, cw, cb, gb, ng, b, l):
    tl = min(l, 256)
    c = tl
    bb = math.gcd(b, ML_SEQS_PER_STEP)
    w2 = 2 * ML_WIDTH
    u3 = u_ml.reshape(b, l, W_ML)
    m0b = jnp.broadcast_to(m0.astype(F32)[:, :, None], (b, ML_HEADS, LANES))
    cv0 = jnp.pad(conv0.astype(F32), ((0, 0), (SUBLANES - (CONV_W - 1), 0), (0, 0)))
    gbp = jnp.pad(gb.astype(F32).reshape(1, 2 * ML_HEADS), ((0, 0), (0, LANES - 2 * ML_HEADS)))
    kern = functools.partial(_mlstm_kernel, bb=bb, tl=tl, c=c)
    st = lambda bi, li: (bi, 0, 0)
    st4 = lambda bi, li: (bi, 0, 0, 0)
    cst = lambda bi, li: (0, 0)
    o, c_new, n_new, m_new, cv = pl.pallas_call(
        kern,
        grid=(b // bb, l // tl),
        in_specs=[pl.BlockSpec((bb, tl, W_ML), lambda bi, li: (bi, li, 0)),
                  pl.BlockSpec((bb, ML_HEADS, ML_DH, ML_DH), st4),
                  pl.BlockSpec((bb, ML_HEADS, ML_DH), st),
                  pl.BlockSpec((bb, ML_HEADS, LANES), st),
                  pl.BlockSpec((bb, SUBLANES, w2), st),
                  pl.BlockSpec((CONV_W, w2), cst),
                  pl.BlockSpec((1, w2), cst),
                  pl.BlockSpec((1, LANES), cst),
                  pl.BlockSpec((1, ML_DH), cst)],
        out_specs=[pl.BlockSpec((bb, tl, ML_WIDTH), lambda bi, li: (bi, li, 0)),
                   pl.BlockSpec((bb, ML_HEADS, ML_DH, ML_DH), st4),
                   pl.BlockSpec((bb, ML_HEADS, ML_DH), st),
                   pl.BlockSpec((bb, ML_HEADS, LANES), st),
                   pl.BlockSpec((bb, SUBLANES, w2), st)],
        out_shape=[jax.ShapeDtypeStruct((b, l, ML_WIDTH), F32),
                   jax.ShapeDtypeStruct((b, ML_HEADS, ML_DH, ML_DH), F32),
                   jax.ShapeDtypeStruct((b, ML_HEADS, ML_DH), F32),
                   jax.ShapeDtypeStruct((b, ML_HEADS, LANES), F32),
                   jax.ShapeDtypeStruct((b, SUBLANES, w2), F32)],
        scratch_shapes=[pltpu.VMEM((bb, tl + 2 * SUBLANES, w2), F32)],
        compiler_params=_cparams("parallel", "arbitrary"),
        name="mlstm",
    )(u3, c0.astype(F32), n0.astype(F32), m0b, cv0, cw.astype(F32), cb.astype(F32).reshape(1, w2), gbp,
      ng.astype(F32).reshape(1, ML_DH))
    return o, c_new, n_new, m_new[:, :, 0], cv[:, SUBLANES - (CONV_W - 1):, :]


def _rope_lanes(x, cos_t, sin_t):
    w = x.shape[1]
    half = ROT_DIM // 2
    reps = w // cos_t.shape[1]
    if reps > 1:
        cos_t = jnp.concatenate([cos_t] * reps, axis=1)
        sin_t = jnp.concatenate([sin_t] * reps, axis=1)
    lane = lax.broadcasted_iota(jnp.int32, x.shape, 1) % HEAD_DIM
    partner = jnp.where(lane < half, pltpu.roll(x, w - half, 1), pltpu.roll(x, half, 1))
    return x * cos_t + partner * sin_t


def _nsa_prep_kernel(u_ref, cos_ref, sin_ref, qg_ref, kg_ref, g4_ref, g1_ref, qn_ref, qr_ref, rows_ref, win_ref):
    u = u_ref[...]
    cos_t = cos_ref[...]
    sin_t = sin_ref[...]
    q = _group_norm(u[:, 0:NSA_WIDTH], g4_ref[...], qg_ref[...])
    qn_ref[...] = (q * ATTN_SCALE).astype(BF16)
    qr = _rope_lanes(q, cos_t, sin_t) * (ATTN_SCALE * LOG2E)
    low = lax.broadcasted_iota(jnp.int32, (qr.shape[0], LANES), 1) < HEAD_DIM
    for j in range(NSA_HEADS // 2):
        pair = qr[:, j * LANES:(j + 1) * LANES]
        swapped = pltpu.roll(pair, HEAD_DIM, 1)
        qr_ref[2 * j] = jnp.where(low, pair, swapped).astype(BF16)
        qr_ref[2 * j + 1] = jnp.where(low, swapped, pair).astype(BF16)
    kv = NSA_WIDTH
    k_slc = _rope_lanes(_group_norm(u[:, kv + 2 * KV_W:kv + 3 * KV_W], g1_ref[...], kg_ref[1:2, :]), cos_t, sin_t)
    k_win = _rope_lanes(_group_norm(u[:, kv + 4 * KV_W:kv + 5 * KV_W], g1_ref[...], kg_ref[2:3, :]), cos_t, sin_t)
    rows_ref[:, 0:2 * KV_W] = u[:, kv:kv + 2 * KV_W]
    rows_ref[:, 2 * KV_W:3 * KV_W] = k_slc
    rows_ref[:, 3 * KV_W:4 * KV_W] = u[:, kv + 3 * KV_W:kv + 4 * KV_W]
    win_ref[:, 0:KV_W] = k_win
    win_ref[:, KV_W:2 * KV_W] = u[:, kv + 5 * KV_W:kv + 6 * KV_W]


def _rope_tables(pos):
    half = ROT_DIM // 2
    inv = jnp.exp(-math.log(ROPE_THETA) * jnp.arange(half, dtype=F32) * 2.0 / ROT_DIM)
    ang = pos.astype(F32)[:, None] * inv[None, :]
    cos, sin = jnp.cos(ang), jnp.sin(ang)
    n = pos.shape[0]
    ones = jnp.ones((n, HEAD_DIM - ROT_DIM), F32)
    cos_h = jnp.concatenate([cos, cos, ones], axis=1)
    sin_h = jnp.concatenate([-sin, sin, 0.0 * ones], axis=1)
    reps = LANES // HEAD_DIM
    return jnp.tile(cos_h, (1, reps)), jnp.tile(sin_h, (1, reps))


def _nsa_prep(u_nsa, pos, q_g, k_g, l):
    rows = u_nsa.shape[0]
    tl = min(l, 256)
    nb = l // tl
    cos_t, sin_t = _rope_tables(pos)
    qg = jnp.tile(q_g.astype(F32).reshape(1, HEAD_DIM), (1, NSA_HEADS))
    kg = jnp.tile(k_g.astype(F32), (1, NSA_KV_HEADS))
    cst = lambda i: (0, 0)
    return pl.pallas_call(
        _nsa_prep_kernel,
        grid=(rows // tl,),
        in_specs=[pl.BlockSpec((tl, W_NSA), lambda i: (i, 0)),
                  pl.BlockSpec((tl, LANES), lambda i: (i % nb, 0)),
                  pl.BlockSpec((tl, LANES), lambda i: (i % nb, 0)),
                  pl.BlockSpec((1, NSA_WIDTH), cst),
                  pl.BlockSpec((3, KV_W), cst),
                  pl.BlockSpec((NSA_WIDTH, NSA_WIDTH), cst),
                  pl.BlockSpec((KV_W, KV_W), cst)],
        out_specs=[pl.BlockSpec((tl, NSA_WIDTH), lambda i: (i, 0)),
                   pl.BlockSpec((NSA_HEADS, tl, LANES), lambda i: (0, i, 0)),
                   pl.BlockSpec((tl, N_KV_SLOTS * KV_W), lambda i: (i, 0)),
                   pl.BlockSpec((tl, 2 * KV_W), lambda i: (i, 0))],
        out_shape=[jax.ShapeDtypeStruct((rows, NSA_WIDTH), BF16),
                   jax.ShapeDtypeStruct((NSA_HEADS, rows, LANES), BF16),
                   jax.ShapeDtypeStruct((rows, N_KV_SLOTS * KV_W), F32),
                   jax.ShapeDtypeStruct((rows, 2 * KV_W), F32)],
        compiler_params=_cparams("parallel"),
        name="nsa_prep",
    )(u_nsa, cos_t, sin_t, qg, kg, _group_mean_matrix(NSA_WIDTH), _group_mean_matrix(KV_W))


def _cmp_halves(xk_ref, xv_ref, w_ref, n_half):
    acc_k = jnp.zeros((n_half, 2 * KV_W), F32)
    acc_v = jnp.zeros((n_half, 2 * KV_W), F32)
    for s in range(CMP_STRIDE):
        acc_k = acc_k + _dot(xk_ref[pl.ds(s, n_half, stride=CMP_STRIDE), :], w_ref[0, s])
        acc_v = acc_v + _dot(xv_ref[pl.ds(s, n_half, stride=CMP_STRIDE), :], w_ref[1, s])
    return acc_k, acc_v


def _cmp_finish(acc_k, acc_v, bias_ref, kg_ref, g1_ref, kc_ref, vc_ref, n_half):
    valid = lax.broadcasted_iota(jnp.int32, (n_half, KV_W), 0) < n_half - 1

    def summary(acc, bias):
        return acc[:, 0:KV_W] + pltpu.roll(acc[:, KV_W:2 * KV_W], n_half - 1, 0) + bias

    kc = _group_norm(summary(acc_k, bias_ref[0:1, :]), g1_ref[...], kg_ref[...])
    kc_ref[0] = jnp.where(valid, kc, 0.0).astype(BF16)
    vc_ref[0] = jnp.where(valid, summary(acc_v, bias_ref[1:2, :]), 0.0).astype(BF16)


def _cmp_kv_kernel(xk_ref, xv_ref, w_ref, bias_ref, kg_ref, g1_ref, kc_ref, vc_ref, *, n_half):
    acc_k, acc_v = _cmp_halves(xk_ref.at[0], xv_ref.at[0], w_ref, n_half)
    _cmp_finish(acc_k, acc_v, bias_ref, kg_ref, g1_ref, kc_ref, vc_ref, n_half)


PAGES_PER_STEP = 16


def _cmp_paged_kernel(pt_ref, *refs, npg, page):
    page_refs = refs[:npg]
    w_ref, acck_ref, accv_ref, xk_sc, xv_sc = refs[npg:]
    for i in range(npg):
        xk_sc[i * page:(i + 1) * page, :] = page_refs[i][0, 0:KV_W, :].T
        xv_sc[i * page:(i + 1) * page, :] = page_refs[i][0, KV_W:2 * KV_W, :].T
    acc_k, acc_v = _cmp_halves(xk_sc, xv_sc, w_ref, npg * page // CMP_STRIDE)
    acck_ref[0] = acc_k
    accv_ref[0] = acc_v


def _cmp_fin_kernel(acck_ref, accv_ref, bias_ref, kg_ref, g1_ref, kc_ref, vc_ref, *, n_half):
    _cmp_finish(acck_ref[0], accv_ref[0], bias_ref, kg_ref, g1_ref, kc_ref, vc_ref, n_half)


def _page_specs(npg, page, row_blk, pool_off):
    def spec(i):
        return pl.BlockSpec((1, 2 * KV_W, page), lambda bi, ji, pt: (pt[bi, ji * npg + i] + pool_off, row_blk, 0))
    return [spec(i) for i in range(npg)]


def _cmp_kv_paged(cache_t, page_table, pool_off, wcat, bias, kg0):
    b, n_pages = page_table.shape
    page = cache_t.shape[2]
    npg = PAGES_PER_STEP
    nh_step = npg * page // CMP_STRIDE
    n_half = n_pages * page // CMP_STRIDE
    kern = functools.partial(_cmp_paged_kernel, npg=npg, page=page)
    acc_k, acc_v = pl.pallas_call(
        kern,
        grid_spec=pltpu.PrefetchScalarGridSpec(
            num_scalar_prefetch=1, grid=(b, n_pages // npg),
            in_specs=_page_specs(npg, page, 0, pool_off)
            + [pl.BlockSpec((2, CMP_STRIDE, KV_W, 2 * KV_W), lambda bi, ji, pt: (0, 0, 0, 0))],
            out_specs=[pl.BlockSpec((1, nh_step, 2 * KV_W), lambda bi, ji, pt: (bi, ji, 0)),
                       pl.BlockSpec((1, nh_step, 2 * KV_W), lambda bi, ji, pt: (bi, ji, 0))],
            scratch_shapes=[pltpu.VMEM((npg * page, KV_W), F32), pltpu.VMEM((npg * page, KV_W), F32)]),
        out_shape=[jax.ShapeDtypeStruct((b, n_half, 2 * KV_W), F32),
                   jax.ShapeDtypeStruct((b, n_half, 2 * KV_W), F32)],
        compiler_params=_cparams("parallel", "arbitrary"),
        name="cmp_paged",
    )(page_table, *([cache_t] * npg), wcat)
    kg = jnp.tile(kg0.astype(F32).reshape(1, HEAD_DIM), (1, NSA_KV_HEADS))
    blk = lambda bi: (bi, 0, 0)
    return pl.pallas_call(
        functools.partial(_cmp_fin_kernel, n_half=n_half),
        grid=(b,),
        in_specs=[pl.BlockSpec((1, n_half, 2 * KV_W), blk),
                  pl.BlockSpec((1, n_half, 2 * KV_W), blk),
                  pl.BlockSpec((2, KV_W), lambda bi: (0, 0)),
                  pl.BlockSpec((1, KV_W), lambda bi: (0, 0)),
                  pl.BlockSpec((KV_W, KV_W), lambda bi: (0, 0))],
        out_specs=[pl.BlockSpec((1, n_half, KV_W), blk), pl.BlockSpec((1, n_half, KV_W), blk)],
        out_shape=[jax.ShapeDtypeStruct((b, n_half, KV_W), BF16),
                   jax.ShapeDtypeStruct((b, n_half, KV_W), BF16)],
        compiler_params=_cparams("parallel"),
        name="cmp_fin",
    )(acc_k, acc_v, bias, kg, _group_mean_matrix(KV_W))


def _cmp_weights(cmp_pos, cmp_w):
    wf = cmp_w.astype(F32)
    eye_g = jnp.eye(NSA_KV_HEADS, dtype=F32)

    def bd(w):
        return jnp.einsum('ksde,gh->ksgdhe', w, eye_g).reshape(2, CMP_STRIDE, KV_W, KV_W)

    wcat = jnp.concatenate([bd(wf[:, :CMP_STRIDE]), bd(wf[:, CMP_STRIDE:])], axis=3).astype(BF16)
    bias = jnp.einsum('ksd,ksde->ke', cmp_pos.astype(F32), wf)
    return wcat, jnp.tile(bias, (1, NSA_KV_HEADS))


def _cmp_kv(x3, k_blk, v_blk, t_use, wcat, bias, kg0):
    b = x3.shape[0]
    n_half = t_use // CMP_STRIDE
    kern = functools.partial(_cmp_kv_kernel, n_half=n_half)
    kg = jnp.tile(kg0.astype(F32).reshape(1, HEAD_DIM), (1, NSA_KV_HEADS))
    return pl.pallas_call(
        kern,
        grid=(b,),
        in_specs=[pl.BlockSpec((1, t_use, KV_W), lambda bi: (bi, 0, k_blk)),
                  pl.BlockSpec((1, t_use, KV_W), lambda bi: (bi, 0, v_blk)),
                  pl.BlockSpec((2, CMP_STRIDE, KV_W, 2 * KV_W), lambda bi: (0, 0, 0, 0)),
                  pl.BlockSpec((2, KV_W), lambda bi: (0, 0)),
                  pl.BlockSpec((1, KV_W), lambda bi: (0, 0)),
                  pl.BlockSpec((KV_W, KV_W), lambda bi: (0, 0))],
        out_specs=[pl.BlockSpec((1, n_half, KV_W), lambda bi: (bi, 0, 0)),
                   pl.BlockSpec((1, n_half, KV_W), lambda bi: (bi, 0, 0))],
        out_shape=[jax.ShapeDtypeStruct((b, n_half, KV_W), BF16),
                   jax.ShapeDtypeStruct((b, n_half, KV_W), BF16)],
        compiler_params=_cparams("parallel"),
        name="cmp_kv",
    )(x3, x3, wcat, bias, kg, _group_mean_matrix(KV_W))


def _cmp_attn_kernel(qn_ref, kc_ref, vc_ref, ov_ref, o_ref, sel_ref, *, bb, tq, n_half, n_cmp, n_sel, nbp, pos0):
    qi = pl.program_id(1)
    nq = bb * tq
    pos_c = pos0 + qi * tq + lax.broadcasted_iota(jnp.int32, (tq, 1), 0)
    ncol = lax.broadcasted_iota(jnp.int32, (1, n_half), 1)
    cmask = (ncol * CMP_STRIDE + (CMP_BLOCK - 1) <= pos_c) & (ncol < n_cmp)
    pos_r = pos0 + qi * tq + lax.broadcasted_iota(jnp.int32, (1, nq), 1) % tq
    blk = lax.broadcasted_iota(jnp.int32, (nbp, 1), 0)
    cur = pos_r // SEL_BLOCK
    forced = (blk == 0) | (blk == cur) | (blk == cur - 1)
    valid = blk * SEL_BLOCK <= pos_r
    real = blk < n_sel
    for g in range(NSA_KV_HEADS):
        gs = slice(g * HEAD_DIM, (g + 1) * HEAD_DIM)
        psums = []
        for i in range(bb):
            kc = kc_ref[i, :, gs]
            vc = vc_ref[i, :, gs]
            psum = jnp.zeros((tq, n_half), F32)
            for h in range(NSA_HPG):
                hs = slice((g * NSA_HPG + h) * HEAD_DIM, (g * NSA_HPG + h + 1) * HEAD_DIM)
                s = _dot_nt(qn_ref[i, :, hs], kc)
                m = jnp.max(jnp.where(cmask, s, NEG_BIG), axis=-1, keepdims=True)
                m = jnp.where(m > 0.5 * NEG_BIG, m, 0.0)
                e = jnp.where(cmask, jnp.exp(s - m), 0.0)
                p = e / jnp.maximum(jnp.sum(e, axis=-1, keepdims=True), 1e-30)
                o_ref[i, :, hs] = _dot(p, vc)
                psum = psum + p
            psums.append(psum)
        psum = psums[0] if bb == 1 else jnp.concatenate(psums, axis=0)
        p_hi = psum.astype(BF16)
        p_lo = (psum - p_hi.astype(F32)).astype(BF16)
        ov = ov_ref[...]
        imp = _dot_nt(ov, p_hi) + _dot_nt(ov, p_lo)
        score = jnp.where(forced, 3e38, jnp.where(valid, imp, -1e38))
        score = jnp.where(real, score, -3e38)
        cnt = jnp.zeros((nbp, nq), F32)
        for jp in range(n_sel):
            rowv = score[jp:jp + 1, :]
            beats = (rowv > score) | ((rowv == score) & (blk > jp))
            cnt = cnt + jnp.where(beats, 1.0, 0.0)
        sel = (cnt < float(min(N_SELECT, n_sel))) & real
        sel_ref[0, g * nbp:(g + 1) * nbp, :] = jnp.where(sel, 1.0, 0.0).astype(BF16)


def _overlap_t(n_cmp, n_sel, n_half, nbp):
    a = SEL_BLOCK // CMP_STRIDE
    bb = CMP_BLOCK // CMP_STRIDE
    i = np.arange(n_half)[None, :]
    j = np.arange(nbp)[:, None]
    s = i - a * j + (bb - 1)
    cnt = np.maximum(np.minimum(np.minimum(s + 1, a + bb - 1 - s), min(a, bb)), 0)
    cnt = np.where((i < n_cmp) & (j < n_sel), cnt, 0)
    return jnp.asarray(cnt, dtype=BF16)


def _cmp_attn(qn3, kc, vc, pos0, t_len):
    b, l, _ = qn3.shape
    n_half = kc.shape[1]
    n_cmp = n_half - 1
    n_sel = -(-t_len // SEL_BLOCK)
    nbp = _round_up(n_sel, 16)
    tq = min(l, LANES)
    bb = math.gcd(b, LANES // tq)
    kern = functools.partial(_cmp_attn_kernel, bb=bb, tq=tq, n_half=n_half, n_cmp=n_cmp, n_sel=n_sel, nbp=nbp,
                             pos0=pos0)
    o_cmp, sel_t = pl.pallas_call(
        kern,
        grid=(b // bb, l // tq),
        in_specs=[pl.BlockSpec((bb, tq, NSA_WIDTH), lambda bi, qi: (bi, qi, 0)),
                  pl.BlockSpec((bb, n_half, KV_W), lambda bi, qi: (bi, 0, 0)),
                  pl.BlockSpec((bb, n_half, KV_W), lambda bi, qi: (bi, 0, 0)),
                  pl.BlockSpec((nbp, n_half), lambda bi, qi: (0, 0))],
        out_specs=[pl.BlockSpec((bb, tq, NSA_WIDTH), lambda bi, qi: (bi, qi, 0)),
                   pl.BlockSpec((1, NSA_KV_HEADS * nbp, bb * tq), lambda bi, qi: (bi, 0, qi))],
        out_shape=[jax.ShapeDtypeStruct((b, l, NSA_WIDTH), F32),
                   jax.ShapeDtypeStruct((b // bb, NSA_KV_HEADS * nbp, bb * l), BF16)],
        compiler_params=_cparams("parallel", "parallel"),
        name="cmp_attn",
    )(qn3, kc, vc, _overlap_t(n_cmp, n_sel, n_half, nbp))
    if bb > 1:
        sel_t = sel_t.reshape(b // bb, NSA_KV_HEADS * nbp, bb, l)
        sel_t = jnp.swapaxes(sel_t, 1, 2).reshape(b, NSA_KV_HEADS * nbp, l)
    return o_cmp, sel_t


SEL_TQ = 128
SEL_TK = 512
WIN_TQ = 256


def _group_lanes(shape, g):
    lane = lax.broadcasted_iota(jnp.int32, shape, len(shape) - 1)
    return (lane < HEAD_DIM) if g == 0 else (lane >= HEAD_DIM)


def _store_heads(o_ref, r, g, tq):
    lo = g * HEAD_DIM
    for h in range(NSA_HPG):
        hh = g * NSA_HPG + h
        o_ref[0, :, hh * HEAD_DIM:(hh + 1) * HEAD_DIM] = r[h * tq:(h + 1) * tq, lo:lo + HEAD_DIM]


def _flash_sel_kernel(qi_ref, ki_ref, q_ref, k_ref, v_ref, sb_ref, o_ref, m_sc, acc_sc, *, tq, tk):
    qi = qi_ref[pl.program_id(1)]
    ki = ki_ref[pl.program_id(1)]
    kmax = (qi * tq + tq - 1) // tk
    hq = NSA_HPG * tq

    @pl.when(ki == 0)
    def _init():
        m_sc[...] = jnp.full(m_sc.shape, NEG_BIG, F32)
        acc_sc[...] = jnp.zeros(acc_sc.shape, F32)

    def step(diagonal):
        key_blk = ki * (tk // SEL_BLOCK) + lax.broadcasted_iota(jnp.int32, (tk, LANES), 0) // SEL_BLOCK
        one_hot = jnp.where(lax.broadcasted_iota(jnp.int32, (tk, LANES), 1) % HEAD_DIM == key_blk, 1.0, 0.0).astype(BF16)
        kblk = k_ref[0].astype(BF16)
        vblk = v_ref[0].astype(BF16)
        sb4 = jnp.concatenate([sb_ref[0]] * NSA_HPG, axis=0)
        if diagonal:
            qpos = qi * tq + lax.broadcasted_iota(jnp.int32, (tq, tk), 0)
            kpos = ki * tk + lax.broadcasted_iota(jnp.int32, (tq, tk), 1)
            causal4 = jnp.concatenate([jnp.where(kpos <= qpos, 0.0, NEG_BIG)] * NSA_HPG, axis=0)
        for g in range(NSA_KV_HEADS):
            own_k = _group_lanes((tk, LANES), g)
            q4 = q_ref[g * NSA_HPG:(g + 1) * NSA_HPG].reshape(hq, LANES)
            qx = jnp.where(_group_lanes((hq, LANES), g), q4, sb4)
            kx = jnp.where(own_k, kblk, one_hot)
            s = lax.dot_general(qx, kx, (((1,), (1,)), ((), ())), preferred_element_type=F32)
            if diagonal:
                s = s + causal4
            m_prev = m_sc[g]
            m_new = jnp.maximum(m_prev, jnp.max(s, axis=-1, keepdims=True))
            p = jnp.exp2(s - m_new[:, :1]).astype(BF16)
            vx = jnp.where(own_k, vblk, 1.0)
            acc_sc[g] = jnp.exp2(m_prev - m_new) * acc_sc[g] + jnp.dot(p, vx, preferred_element_type=F32)
            m_sc[g] = m_new

    @pl.when(ki < kmax)
    def _full():
        step(False)

    @pl.when(ki == kmax)
    def _last():
        step(True)
        for g in range(NSA_KV_HEADS):
            acc = acc_sc[g]
            _store_heads(o_ref, acc / pltpu.roll(acc, HEAD_DIM, 1), g, tq)


def _flash_sel(q8, rows3, selb):
    b, l, _ = rows3.shape
    tq, tk = min(SEL_TQ, l), min(SEL_TK, l)
    nq = l // tq
    pairs = [(qi, ki) for qi in range(nq) for ki in range((qi * tq + tq - 1) // tk + 1)]
    qi_tab = jnp.asarray([p[0] for p in pairs], jnp.int32)
    ki_tab = jnp.asarray([p[1] for p in pairs], jnp.int32)
    kv_idx = lambda blk: (lambda bi, si, qt, kt: (bi, kt[si], blk))
    return pl.pallas_call(
        functools.partial(_flash_sel_kernel, tq=tq, tk=tk),
        grid_spec=pltpu.PrefetchScalarGridSpec(
            num_scalar_prefetch=2, grid=(b, len(pairs)),
            in_specs=[pl.BlockSpec((NSA_HEADS, tq, LANES), lambda bi, si, qt, kt: (0, bi * nq + qt[si], 0)),
                      pl.BlockSpec((1, tk, KV_W), kv_idx(2)),
                      pl.BlockSpec((1, tk, KV_W), kv_idx(3)),
                      pl.BlockSpec((1, tq, LANES), lambda bi, si, qt, kt: (bi, qt[si], 0))],
            out_specs=pl.BlockSpec((1, tq, NSA_WIDTH), lambda bi, si, qt, kt: (bi, qt[si], 0)),
            scratch_shapes=[pltpu.VMEM((NSA_KV_HEADS, NSA_HPG * tq, LANES), F32),
                            pltpu.VMEM((NSA_KV_HEADS, NSA_HPG * tq, LANES), F32)]),
        out_shape=jax.ShapeDtypeStruct((b, l, NSA_WIDTH), F32),
        compiler_params=_cparams("parallel", "arbitrary"),
        name="flash_sel",
    )(qi_tab, ki_tab, q8, rows3, rows3, selb)


def _flash_win_kernel(q_ref, *refs, tq, back):
    nspan = back + 1
    k_refs, v_refs, o_ref = refs[:nspan], refs[nspan:2 * nspan], refs[2 * nspan]
    qi = pl.program_id(1)
    hq = NSA_HPG * tq
    span = nspan * tq
    qpos = qi * tq + lax.broadcasted_iota(jnp.int32, (tq, span), 0)
    kpos = (qi - back) * tq + lax.broadcasted_iota(jnp.int32, (tq, span), 1)
    ok = (kpos <= qpos) & (kpos > qpos - WINDOW) & (kpos >= 0)
    bias4 = jnp.concatenate([jnp.where(ok, 0.0, NEG_BIG)] * NSA_HPG, axis=0)
    kcat = jnp.concatenate([r[0] for r in k_refs], axis=0).astype(BF16)
    vcat = jnp.concatenate([r[0] for r in v_refs], axis=0).astype(BF16)
    for g in range(NSA_KV_HEADS):
        q4 = q_ref[g * NSA_HPG:(g + 1) * NSA_HPG].reshape(hq, LANES)
        qx = jnp.where(_group_lanes((hq, LANES), g), q4, 0.0)
        s = lax.dot_general(qx, kcat, (((1,), (1,)), ((), ())), preferred_element_type=F32) + bias4
        p = jnp.exp2(s - jnp.max(s, axis=-1, keepdims=True)).astype(BF16)
        vx = jnp.where(_group_lanes((span, LANES), g), vcat, 1.0)
        r = jnp.dot(p, vx, preferred_element_type=F32)
        _store_heads(o_ref, r / pltpu.roll(r, HEAD_DIM, 1), g, tq)


def _flash_win(q8, win3):
    b, l, _ = win3.shape
    tq = min(WIN_TQ, l)
    nq = l // tq
    back = -(-(WINDOW - 1) // tq)

    def kv_specs(blk):
        return [pl.BlockSpec((1, tq, KV_W), (lambda bi, qi, j=j: (bi, jnp.maximum(qi - back + j, 0), blk)))
                for j in range(back + 1)]

    return pl.pallas_call(
        functools.partial(_flash_win_kernel, tq=tq, back=back),
        grid=(b, nq),
        in_specs=[pl.BlockSpec((NSA_HEADS, tq, LANES), lambda bi, qi: (0, bi * nq + qi, 0))]
        + kv_specs(0) + kv_specs(1),
        out_specs=pl.BlockSpec((1, tq, NSA_WIDTH), lambda bi, qi: (bi, qi, 0)),
        out_shape=jax.ShapeDtypeStruct((b, l, NSA_WIDTH), F32),
        compiler_params=_cparams("parallel", "parallel"),
        name="flash_win",
    )(q8, *([win3] * (2 * (back + 1))))


def _softmax_update(s, m_prev, l_prev):
    m_new = jnp.maximum(m_prev, jnp.max(s, axis=-1, keepdims=True))
    alpha = jnp.exp2(m_prev - m_new)
    p = jnp.exp2(s - m_new[:, :1])
    return m_new, alpha, alpha * l_prev + jnp.sum(p, axis=-1, keepdims=True), p


def _paged_attn_kernel(pt_ref, *refs, npg, page, l_new, nbp):
    page_refs = refs[:npg]
    (qz_ref, sel_ref, kn_ref, vn_ref, win_ref, kwn_ref, vwn_ref,
     oslc_ref, owin_ref, wout_ref, m_sc, l_sc, acc_sc) = refs[npg:]
    ji = pl.program_id(1)
    nr = qz_ref.shape[1]
    span = npg * page

    @pl.when(ji == 0)
    def _init():
        m_sc[...] = jnp.full(m_sc.shape, NEG_BIG, F32)
        l_sc[...] = jnp.zeros(l_sc.shape, F32)
        acc_sc[...] = jnp.zeros(acc_sc.shape, F32)

    qz = qz_ref[0]
    kt = jnp.concatenate([page_refs[i][0, 0:KV_W, :] for i in range(npg)], axis=1).astype(BF16)
    vt = jnp.concatenate([page_refs[i][0, KV_W:2 * KV_W, :] for i in range(npg)], axis=1).astype(BF16)
    s = jnp.dot(qz, kt, preferred_element_type=F32)
    e_row = lax.broadcasted_iota(jnp.int32, (nbp, span), 0)
    e_col = lax.broadcasted_iota(jnp.int32, (nbp, span), 1)
    expand = jnp.where(e_row == ji * (span // SEL_BLOCK) + e_col // SEL_BLOCK, 1.0, 0.0).astype(BF16)
    picked = jnp.dot(sel_ref[0], expand, preferred_element_type=F32) > 0.5
    m_new, alpha, l_new_v, p = _softmax_update(jnp.where(picked, s, NEG_BIG), m_sc[...], l_sc[...])
    m_sc[...] = m_new
    l_sc[...] = l_new_v
    acc_sc[...] = alpha * acc_sc[...] + _dot_nt(p, vt)

    @pl.when(ji == pl.num_programs(1) - 1)
    def _fin():
        q_of_row = lax.broadcasted_iota(jnp.int32, (nr, l_new), 0) % l_new
        j_new = lax.broadcasted_iota(jnp.int32, (nr, l_new), 1)
        new_ok = j_new <= q_of_row
        sn = jnp.where(new_ok, _dot_nt(qz, kn_ref[0]), NEG_BIG)
        m2, a2, l2, p2 = _softmax_update(sn, m_sc[...], l_sc[...])
        oslc_ref[0] = (a2 * acc_sc[...] + _dot(p2, vn_ref[0])) / l2
        wlen = win_ref.shape[2]
        kw = win_ref[0, 0:KV_W, :]
        vw = win_ref[0, KV_W:2 * KV_W, :]
        i_old = lax.broadcasted_iota(jnp.int32, (nr, wlen), 1)
        q_old = lax.broadcasted_iota(jnp.int32, (nr, wlen), 0) % l_new
        sw = jnp.where(i_old + (WINDOW - wlen) > q_old, _dot(qz, kw), NEG_BIG)
        swn = jnp.where(new_ok, _dot_nt(qz, kwn_ref[0]), NEG_BIG)
        mw = jnp.maximum(jnp.max(sw, axis=-1, keepdims=True), jnp.max(swn, axis=-1, keepdims=True))
        pw = jnp.exp2(sw - mw)
        pwn = jnp.exp2(swn - mw)
        lw = jnp.sum(pw, axis=-1, keepdims=True) + jnp.sum(pwn, axis=-1, keepdims=True)
        owin_ref[0] = (_dot_nt(pw, vw) + _dot(pwn, vwn_ref[0])) / lw
        new_t = jnp.concatenate([kwn_ref[0], vwn_ref[0]], axis=1).T
        place = (lax.broadcasted_iota(jnp.int32, (l_new, wlen), 1)
                 == lax.broadcasted_iota(jnp.int32, (l_new, wlen), 0) + (wlen - l_new)).astype(F32)
        lane = lax.broadcasted_iota(jnp.int32, (2 * KV_W, wlen), 1)
        wout_ref[0] = jnp.where(lane < wlen - l_new, pltpu.roll(win_ref[0], wlen - l_new, 1), _dot_f32(new_t, place))


def _paged_attn(cache_t, page_table, pool_off, qz, sel_rows, rows3, win_t, win_off, win3):
    b, n_pages = page_table.shape
    page = cache_t.shape[2]
    npg = PAGES_PER_STEP
    nr = qz.shape[1]
    l_new = rows3.shape[1]
    nbp = sel_rows.shape[2]
    wlen = win_t.shape[2]
    kern = functools.partial(_paged_attn_kernel, npg=npg, page=page, l_new=l_new, nbp=nbp)
    per_b = lambda bi, ji, pt: (bi, 0, 0)
    return pl.pallas_call(
        kern,
        grid_spec=pltpu.PrefetchScalarGridSpec(
            num_scalar_prefetch=1, grid=(b, n_pages // npg),
            in_specs=_page_specs(npg, page, 1, pool_off)
            + [pl.BlockSpec((1, nr, KV_W), per_b),
               pl.BlockSpec((1, nr, nbp), per_b),
               pl.BlockSpec((1, l_new, KV_W), lambda bi, ji, pt: (bi, 0, 2)),
               pl.BlockSpec((1, l_new, KV_W), lambda bi, ji, pt: (bi, 0, 3)),
               pl.BlockSpec((1, 2 * KV_W, wlen), lambda bi, ji, pt: (bi + win_off, 0, 0)),
               pl.BlockSpec((1, l_new, KV_W), lambda bi, ji, pt: (bi, 0, 0)),
               pl.BlockSpec((1, l_new, KV_W), lambda bi, ji, pt: (bi, 0, 1))],
            out_specs=[pl.BlockSpec((1, nr, KV_W), per_b),
                       pl.BlockSpec((1, nr, KV_W), per_b),
                       pl.BlockSpec((1, 2 * KV_W, wlen), per_b)],
            scratch_shapes=[pltpu.VMEM((nr, LANES), F32), pltpu.VMEM((nr, LANES), F32),
                            pltpu.VMEM((nr, KV_W), F32)]),
        out_shape=[jax.ShapeDtypeStruct((b, nr, KV_W), F32),
                   jax.ShapeDtypeStruct((b, nr, KV_W), F32),
                   jax.ShapeDtypeStruct((b, 2 * KV_W, wlen), F32)],
        compiler_params=_cparams("parallel", "arbitrary"),
        name="paged_attn",
    )(page_table, *([cache_t] * npg), qz, sel_rows, rows3, rows3, win_t, win3, win3)


def _nsa_gate_kernel(oc_ref, os_ref, ow_ref, gz_ref, e_ref, o_ref):
    gate = _sigmoid(gz_ref[:, 0:LANES])
    o = (_dot_split(gate, e_ref[0]) * oc_ref[...] + _dot_split(gate, e_ref[1]) * os_ref[...]
         + _dot_split(gate, e_ref[2]) * ow_ref[...])
    o_ref[...] = o * _silu(gz_ref[:, LANES:LANES + NSA_WIDTH])


def _gate_expand():
    e = np.zeros((3, LANES, NSA_WIDTH), np.float32)
    for j in range(3):
        for h in range(NSA_HEADS):
            e[j, j * NSA_HEADS + h, h * HEAD_DIM:(h + 1) * HEAD_DIM] = 1.0
    return jnp.asarray(e, dtype=BF16)


def _nsa_gate(o_cmp, o_slc, o_win, u_nsa, tm):
    rows = o_cmp.shape[0]
    gz_w = LANES + NSA_WIDTH
    gz_blk = (NSA_WIDTH + 6 * KV_W) // gz_w
    assert gz_blk * gz_w == NSA_WIDTH + 6 * KV_W
    row = lambda i: (i, 0)
    return pl.pallas_call(
        _nsa_gate_kernel,
        grid=(rows // tm,),
        in_specs=[pl.BlockSpec((tm, NSA_WIDTH), row),
                  pl.BlockSpec((tm, NSA_WIDTH), row),
                  pl.BlockSpec((tm, NSA_WIDTH), row),
                  pl.BlockSpec((tm, gz_w), lambda i: (i, gz_blk)),
                  pl.BlockSpec((3, LANES, NSA_WIDTH), lambda i: (0, 0, 0))],
        out_specs=pl.BlockSpec((tm, NSA_WIDTH), row),
        out_shape=jax.ShapeDtypeStruct((rows, NSA_WIDTH), F32),
        compiler_params=_cparams("parallel"),
        name="nsa_gate",
    )(o_cmp, o_slc, o_win, u_nsa, _gate_expand())


def _rms_norm(x, g, eps=EPS):
    xf = x.astype(F32)
    y = xf * lax.rsqrt(jnp.mean(xf * xf, axis=-1, keepdims=True) + eps)
    return (y * g.astype(F32)).astype(x.dtype)


def _rope(x, pos):
    half = ROT_DIM // 2
    inv = jnp.exp(-math.log(ROPE_THETA) * jnp.arange(half, dtype=F32) * 2.0 / ROT_DIM)
    ang = pos.astype(F32)[:, None] * inv[None, :]
    cos = jnp.cos(ang)[:, None, :]
    sin = jnp.sin(ang)[:, None, :]
    xf = x.astype(F32)
    x1, x2, rest = xf[..., :half], xf[..., half:ROT_DIM], xf[..., ROT_DIM:]
    return jnp.concatenate([x1 * cos - x2 * sin, x2 * cos + x1 * sin, rest], axis=-1).astype(x.dtype)


def _masked_softmax(s, mask):
    s = jnp.where(mask, s, -jnp.inf)
    m = jnp.max(s, axis=-1, keepdims=True)
    m = jnp.where(jnp.isfinite(m), m, 0.0)
    e = jnp.where(mask, jnp.exp(s - m), 0.0)
    return e / jnp.maximum(jnp.sum(e, axis=-1, keepdims=True), 1e-30)


def _cmp_summaries(x, pe, w):
    b, t, g, d = x.shape
    n_half = t // CMP_STRIDE
    halves = x[:, :n_half * CMP_STRIDE].astype(F32).reshape(b, n_half, CMP_STRIDE, g, d)
    wf = w.astype(F32)
    first = jnp.einsum('bnsgd,sde->bnge', halves[:, :-1], wf[:CMP_STRIDE])
    second = jnp.einsum('bnsgd,sde->bnge', halves[:, 1:], wf[CMP_STRIDE:])
    bias = jnp.einsum('sd,sde->e', pe.astype(F32), wf)
    return first + second + bias


def _overlap_matrix(n_cmp, n_sel):
    a = SEL_BLOCK // CMP_STRIDE
    bb = CMP_BLOCK // CMP_STRIDE
    i = jnp.arange(n_cmp)[:, None]
    j = jnp.arange(n_sel)[None, :]
    s = i - a * j + (bb - 1)
    cnt = jnp.minimum(jnp.minimum(s + 1, a + bb - 1 - s), min(a, bb))
    return jnp.maximum(cnt, 0).astype(F32)


def _nsa_selected_gather(q, k_hist, v_hist, idx, pos):
    b, l = q.shape[:2]
    t_len = k_hist.shape[1]
    n_sel = -(-t_len // SEL_BLOCK)
    n_k = idx.shape[-1]
    pad = n_sel * SEL_BLOCK - t_len

    def blocks(x):
        x = jnp.pad(x.astype(F32), ((0, 0), (0, pad), (0, 0), (0, 0)))
        return x.reshape(b, n_sel, SEL_BLOCK, NSA_KV_HEADS, HEAD_DIM).transpose(0, 3, 1, 2, 4)

    kb, vb = blocks(k_hist), blocks(v_hist)
    qb = math.gcd(l, Q_BLOCK)
    nq = l // qb
    qx = q.astype(F32).reshape(b * nq, qb, NSA_KV_HEADS, NSA_HPG, HEAD_DIM)
    ix = idx.reshape(b, NSA_KV_HEADS, nq, qb, n_k).transpose(0, 2, 1, 3, 4).reshape(b * nq, NSA_KV_HEADS, qb, n_k)
    px = jnp.tile(pos.reshape(nq, qb), (b, 1))
    bx = jnp.repeat(jnp.arange(b), nq)
    g_ix = jnp.arange(NSA_KV_HEADS)[:, None, None]
    offs = jnp.arange(SEL_BLOCK)
    n_keys = n_k * SEL_BLOCK

    def attend(args):
        qi, ii, pi, bi = args
        ks = kb[bi][g_ix, ii]
        vs = vb[bi][g_ix, ii]
        kpos = ii[..., None] * SEL_BLOCK + offs
        mask = (kpos <= pi[None, :, None, None]).reshape(NSA_KV_HEADS, 1, qb, n_keys)
        s = jnp.einsum('qghd,gqkjd->ghqkj', qi, ks).reshape(NSA_KV_HEADS, NSA_HPG, qb, n_keys) * ATTN_SCALE
        p = _masked_softmax(s, mask)
        return jnp.einsum('ghqn,gqnd->qghd', p, vs.reshape(NSA_KV_HEADS, qb, n_keys, HEAD_DIM))

    o = lax.map(attend, (qx, ix, px, bx))
    return o.reshape(b, l, NSA_HEADS, HEAD_DIM)


def _nsa_window(q, k_ctx, v_ctx, pos0):
    b, l = q.shape[:2]
    lc = k_ctx.shape[1]
    padw = ((0, 0), (WINDOW, 0), (0, 0), (0, 0))
    kp = jnp.pad(k_ctx.astype(F32), padw)
    vp = jnp.pad(v_ctx.astype(F32), padw)
    qb = math.gcd(l, Q_BLOCK)
    nq = l // qb
    span = WINDOW + qb
    qx = jnp.moveaxis(q.astype(F32).reshape(b, nq, qb, NSA_KV_HEADS, NSA_HPG, HEAD_DIM), 1, 0)
    starts = jnp.arange(nq) * qb
    first_pos = pos0 + l - lc

    def attend(args):
        qi, i0 = args
        kk = lax.dynamic_slice_in_dim(kp, i0 + lc - l, span, axis=1)
        vv = lax.dynamic_slice_in_dim(vp, i0 + lc - l, span, axis=1)
        kpos = pos0 + i0 - WINDOW + jnp.arange(span)
        qpos = pos0 + i0 + jnp.arange(qb)
        mask = ((kpos[None, :] >= first_pos) & (kpos[None, :] <= qpos[:, None])
                & (kpos[None, :] > qpos[:, None] - WINDOW))
        s = jnp.einsum('bqghd,bkgd->bghqk', qi, kk) * ATTN_SCALE
        p = _masked_softmax(s, mask)
        return jnp.einsum('bghqk,bkgd->bqghd', p, vv)

    o = lax.map(attend, (qx, starts))
    return jnp.moveaxis(o, 0, 1).reshape(b, l, NSA_HEADS, HEAD_DIM)


def _nsa_mixer_jax(n_q, n_kv, n_g, past_kv, win_buf, win_keep, pos0, q_g, k_g, cmp_pos, cmp_w):
    b, l, _ = n_q.shape
    pos = pos0 + jnp.arange(l)
    q = _rms_norm(n_q.reshape(b, l, NSA_HEADS, HEAD_DIM), q_g)
    q_rot = _rope(q, pos)
    kv = n_kv.reshape(b, l, 6, NSA_KV_HEADS, HEAD_DIM)
    k_slc = _rope(_rms_norm(kv[:, :, 2], k_g[1]), pos)
    k_win = _rope(_rms_norm(kv[:, :, 4], k_g[2]), pos)
    new_rows = jnp.stack([kv[:, :, 0], kv[:, :, 1], k_slc, kv[:, :, 3]], axis=2)
    new_win = jnp.stack([k_win, kv[:, :, 5]], axis=2)
    hist = jnp.concatenate([past_kv.astype(new_rows.dtype), new_rows], axis=1)
    ctx = jnp.concatenate([win_buf.astype(new_win.dtype), new_win], axis=1)
    t_len = hist.shape[1]

    kc = _rms_norm(_cmp_summaries(hist[:, :, 0], cmp_pos[0], cmp_w[0]), k_g[0])
    vc = _cmp_summaries(hist[:, :, 1], cmp_pos[1], cmp_w[1])
    n_cmp = kc.shape[1]
    qg = q.astype(F32).reshape(b, l, NSA_KV_HEADS, NSA_HPG, HEAD_DIM)
    s = jnp.einsum('blghd,bngd->bghln', qg, kc) * ATTN_SCALE
    cmp_end = jnp.arange(n_cmp) * CMP_STRIDE + CMP_BLOCK - 1
    p_cmp = _masked_softmax(s, cmp_end[None, :] <= pos[:, None])
    o_cmp = jnp.einsum('bghln,bngd->blghd', p_cmp, vc).reshape(b, l, NSA_HEADS, HEAD_DIM)

    n_sel = -(-t_len // SEL_BLOCK)
    imp = jnp.einsum('bghln,nj->bglj', p_cmp, _overlap_matrix(n_cmp, n_sel))
    blk = jnp.arange(n_sel)[None, :]
    cur = (pos // SEL_BLOCK)[:, None]
    valid = blk * SEL_BLOCK <= pos[:, None]
    forced = (blk == 0) | (blk == cur) | (blk == cur - 1)
    score = jnp.where(forced, jnp.inf, jnp.where(valid, imp, -jnp.inf))
    _, idx = lax.top_k(score, min(N_SELECT, n_sel))
    o_slc = _nsa_selected_gather(q_rot, hist[:, :, 2], hist[:, :, 3], idx, pos)
    o_win = _nsa_window(q_rot, ctx[:, :, 0], ctx[:, :, 1], pos0)

    gate = jax.nn.sigmoid(n_g.astype(F32)).reshape(b, l, 3, NSA_HEADS, 1)
    o = gate[:, :, 0] * o_cmp + gate[:, :, 1] * o_slc + gate[:, :, 2] * o_win
    return o.reshape(b, l, NSA_WIDTH), new_rows, ctx[:, ctx.shape[1] - win_keep:]


def _nsa_fresh(u_nsa, b, l, win_keep, q_g, k_g, cmp_pos, cmp_w, tm):
    qn, qr, rows, win = _nsa_prep(u_nsa, jnp.arange(l), q_g, k_g, l)
    rows3 = rows.reshape(b, l, N_KV_SLOTS * KV_W)
    win3 = win.reshape(b, l, 2 * KV_W)
    wcat, bias = _cmp_weights(cmp_pos, cmp_w)
    t_use = (l // CMP_STRIDE) * CMP_STRIDE
    kc, vc = _cmp_kv(rows3, 0, 1, t_use, wcat, bias, k_g[0])
    o_cmp, sel_t = _cmp_attn(qn.reshape(b, l, NSA_WIDTH), kc, vc, 0, l)
    nbp = sel_t.shape[1] // NSA_KV_HEADS
    assert NSA_KV_HEADS == 2 and nbp <= HEAD_DIM
    sb = jnp.where(sel_t.reshape(b, NSA_KV_HEADS, nbp, l) > 0.5, 0.0, NEG_BIG)
    sb = jnp.pad(sb, ((0, 0), (0, 0), (0, HEAD_DIM - nbp), (0, 0)), constant_values=NEG_BIG)[:, ::-1]
    selb = jnp.swapaxes(sb.reshape(b, LANES, l), 1, 2).astype(BF16)
    o_slc = _flash_sel(qr, rows3, selb)
    o_win = _flash_win(qr, win3)
    o_b = _nsa_gate(o_cmp.reshape(b * l, NSA_WIDTH), o_slc.reshape(b * l, NSA_WIDTH),
                    o_win.reshape(b * l, NSA_WIDTH), u_nsa, tm)
    new_rows = rows3.reshape(b, l, N_KV_SLOTS, NSA_KV_HEADS, HEAD_DIM)
    new_win = win3[:, l - win_keep:].reshape(b, win_keep, 2, NSA_KV_HEADS, HEAD_DIM)
    return o_b, new_rows, new_win


def _nsa_paged(u_nsa, b, l, paged, q_g, k_g, cmp_pos, cmp_w, tm):
    cache_t, page_table, pool_off, win_t, win_off = paged
    past_len = page_table.shape[1] * cache_t.shape[2]
    assert (past_len + l) // CMP_STRIDE == past_len // CMP_STRIDE and past_len % SEL_BLOCK == 0
    qn, qr, rows, win = _nsa_prep(u_nsa, past_len + jnp.arange(l), q_g, k_g, l)
    rows3 = rows.reshape(b, l, N_KV_SLOTS * KV_W)
    win3 = win.reshape(b, l, 2 * KV_W)
    wcat, bias = _cmp_weights(cmp_pos, cmp_w)
    kc, vc = _cmp_kv_paged(cache_t, page_table, pool_off, wcat, bias, k_g[0])
    o_cmp, sel_t = _cmp_attn(qn.reshape(b, l, NSA_WIDTH), kc, vc, past_len, past_len + l)
    nbp = sel_t.shape[1] // NSA_KV_HEADS
    sel_rows = jnp.swapaxes(sel_t.reshape(b, NSA_KV_HEADS, 1, nbp, l), 3, 4)
    sel_rows = jnp.broadcast_to(sel_rows, (b, NSA_KV_HEADS, NSA_HPG, l, nbp)).reshape(b, NSA_HEADS * l, nbp)
    q5 = jnp.transpose(qr[:, :, :HEAD_DIM].reshape(NSA_KV_HEADS, NSA_HPG, b, l, HEAD_DIM), (2, 0, 1, 3, 4))
    qz = jnp.einsum('bghqd,gk->bghqkd', q5, jnp.eye(NSA_KV_HEADS, dtype=q5.dtype)).reshape(b, NSA_HEADS * l, KV_W)
    o_slc_z, o_win_z, wout = _paged_attn(cache_t, page_table, pool_off, qz, sel_rows, rows3, win_t, win_off, win3)

    def own_group(o):
        o6 = o.reshape(b, NSA_KV_HEADS, NSA_HPG, l, NSA_KV_HEADS, HEAD_DIM)
        d = jnp.stack([o6[:, g, :, :, g, :] for g in range(NSA_KV_HEADS)], axis=1)
        return jnp.transpose(d, (0, 3, 1, 2, 4)).reshape(b * l, NSA_WIDTH)

    o_b = _nsa_gate(o_cmp.reshape(b * l, NSA_WIDTH), own_group(o_slc_z), own_group(o_win_z), u_nsa, min(tm, b * l))
    new_rows = rows3.reshape(b, l, N_KV_SLOTS, NSA_KV_HEADS, HEAD_DIM)
    wlen = wout.shape[2]
    new_win = jnp.transpose(wout.reshape(b, 2, NSA_KV_HEADS, HEAD_DIM, wlen), (0, 4, 1, 2, 3))
    return o_b, new_rows, new_win


def _hybrid_layer(x, paged, win_keep, s_gla, c_ml, n_ml, m_ml, conv_ml,
                  norm_g, w_in_pad, w_out_bf, gla_w_gate, gla_b_gate, gla_norm_g,
                  nsa_q_norm_g, nsa_k_norm_g, nsa_cmp_pos, nsa_cmp_w,
                  ml_conv_w, ml_conv_b, ml_gate_b, ml_norm_g):
    b, l, _ = x.shape
    rows = b * l
    tm = 256
    x2d = x.reshape(rows, D_MODEL)
    u_gla, u_nsa, u_ml = _proj_in(x2d, norm_g, w_in_pad, tm)

    nk = GLA_HEADS * GLA_DK
    wg = jnp.pad(gla_w_gate.astype(F32), ((0, LANES - GLA_RANK), (0, 0))).astype(BF16)
    o_a, st_new = _gla(u_gla, _gla_state_in(s_gla), wg, gla_b_gate.astype(F32).reshape(1, nk),
                       jnp.tile(gla_norm_g.astype(F32).reshape(1, GLA_DV), (1, GLA_HEADS)), b, l)
    s_new = _gla_state_out(st_new)

    if paged is None:
        o_b, new_rows, new_win = _nsa_fresh(u_nsa, b, l, win_keep, nsa_q_norm_g, nsa_k_norm_g,
                                            nsa_cmp_pos, nsa_cmp_w, tm)
    else:
        o_b, new_rows, new_win = _nsa_paged(u_nsa, b, l, paged, nsa_q_norm_g, nsa_k_norm_g,
                                            nsa_cmp_pos, nsa_cmp_w, tm)

    o_c, c_new, n_new, m_new, conv_new = _mlstm(u_ml, c_ml, n_ml, m_ml, conv_ml, ml_conv_w, ml_conv_b,
                                                ml_gate_b, ml_norm_g, b, l)

    y = _proj_out(o_a.reshape(rows, GLA_WIDTH), o_b, o_c.reshape(rows, ML_WIDTH), x2d, w_out_bf, tm)
    return y.reshape(b, l, D_MODEL), new_rows, new_win, s_new, c_new, n_new, m_new, conv_new


def kernel(x_prompt, x_sample, cache_nsa_kv, state_nsa_win, state_gla, state_mlstm_C, state_mlstm_n,
           state_mlstm_m, state_mlstm_conv, page_table, norm_g, w_in, w_out, gla_w_gate, gla_b_gate,
           gla_norm_g, nsa_q_norm_g, nsa_k_norm_g, nsa_cmp_pos, nsa_cmp_w, ml_conv_w, ml_conv_b,
           ml_gate_b, ml_norm_g):
    bp, sp, _ = x_prompt.shape
    bs, _, _ = x_sample.shape
    depth = w_in.shape[0]
    n_pages = page_table.shape[1]
    past_len = n_pages * cache_nsa_kv.shape[2]
    dt = x_prompt.dtype
    zero_gla = jnp.zeros((bp, GLA_HEADS, GLA_DK, GLA_DV), F32)
    zero_c = jnp.zeros((bp, ML_HEADS, ML_DH, ML_DH), F32)
    zero_n = jnp.zeros((bp, ML_HEADS, ML_DH), F32)
    zero_m = jnp.zeros((bp, ML_HEADS), F32)
    zero_conv = jnp.zeros((bp, CONV_W - 1, 2 * ML_WIDTH), dt)
    keep_p = min(WINDOW, sp)
    keep_s = state_nsa_win.shape[2]
    n_pool = cache_nsa_kv.shape[1]
    cache_t = jnp.transpose(cache_nsa_kv, (0, 1, 3, 4, 5, 2)).reshape(
        depth * n_pool, N_KV_SLOTS * KV_W, cache_nsa_kv.shape[2]).astype(F32)
    win_t = jnp.transpose(state_nsa_win, (0, 1, 3, 4, 5, 2)).reshape(depth * bs, 2 * KV_W, keep_s).astype(F32)

    y_prompt, y_sample = x_prompt, x_sample
    p_layers, s_layers = [], []
    for layer in range(depth):
        w = (norm_g[layer], _pad_w_in(w_in[layer]), w_out[layer].astype(BF16), gla_w_gate[layer],
             gla_b_gate[layer], gla_norm_g[layer],
             nsa_q_norm_g[layer], nsa_k_norm_g[layer], nsa_cmp_pos[layer], nsa_cmp_w[layer],
             ml_conv_w[layer], ml_conv_b[layer], ml_gate_b[layer], ml_norm_g[layer])
        y_prompt, *p_new = _hybrid_layer(y_prompt, None, keep_p, zero_gla, zero_c, zero_n,
                                         zero_m, zero_conv, *w)
        paged = (cache_t, page_table, layer * n_pool, win_t, layer * bs)
        y_sample, *s_new = _hybrid_layer(y_sample, paged, keep_s,
                                         state_gla[layer], state_mlstm_C[layer], state_mlstm_n[layer],
                                         state_mlstm_m[layer], state_mlstm_conv[layer], *w)
        p_layers.append(p_new)
        s_layers.append(s_new)
    p_kv, p_win, p_gla, p_c, p_n, p_m, p_conv = [jnp.stack(z) for z in zip(*p_layers)]
    s_kv, s_win, s_gla, s_c, s_n, s_m, s_conv = [jnp.stack(z) for z in zip(*s_layers)]
    return (y_prompt, y_sample, p_kv, s_kv, p_win, s_win, p_gla, s_gla, p_c, s_c, p_n, s_n, p_m, s_m, p_conv, s_conv)
```

```python
import functools
import math

import jax
import jax.numpy as jnp
import numpy as np
from jax import lax
from jax.experimental import pallas as pl
from jax.experimental.pallas import tpu as pltpu

F32 = jnp.float32
BF16 = jnp.bfloat16
HIGHEST = lax.Precision.HIGHEST

D_MODEL = 1024
HEAD_DIM = 64
GLA_WIDTH = D_MODEL // 4
NSA_WIDTH = D_MODEL // 2
ML_WIDTH = D_MODEL - GLA_WIDTH - NSA_WIDTH
D_MIX = GLA_WIDTH + NSA_WIDTH + ML_WIDTH

GLA_HEADS = GLA_WIDTH // HEAD_DIM
GLA_DK = HEAD_DIM // 2
GLA_DV = HEAD_DIM
GLA_RANK = 16
GLA_TAU = 16.0
GLA_CHUNK = 64

NSA_HEADS = NSA_WIDTH // HEAD_DIM
NSA_KV_HEADS = 2
NSA_HPG = NSA_HEADS // NSA_KV_HEADS
CMP_BLOCK = 32
CMP_STRIDE = 16
SEL_BLOCK = 64
N_SELECT = 16
WINDOW = 512
Q_BLOCK = 128
N_KV_SLOTS = 4
ROT_DIM = HEAD_DIM // 4
ROPE_THETA = 500000.0
ATTN_SCALE = HEAD_DIM ** -0.5

ML_HEADS = ML_WIDTH // HEAD_DIM
ML_DH = HEAD_DIM
ML_CHUNK = 64
CONV_W = 4

SPLIT_SIZES = (GLA_HEADS * GLA_DK, GLA_HEADS * GLA_DK, GLA_WIDTH, GLA_RANK, GLA_WIDTH,
               NSA_WIDTH, 6 * NSA_KV_HEADS * HEAD_DIM, 3 * NSA_HEADS, NSA_WIDTH,
               2 * ML_WIDTH, ML_WIDTH, 2 * ML_HEADS, ML_WIDTH, ML_WIDTH)

LANES = 128
SUBLANES = 8
VMEM_LIMIT = 56 * 1024 * 1024
NEG_BIG = -1e30
EPS = 1e-6
LOG2E = math.log2(math.e)


def _round_up(n, m):
    return -(-n // m) * m


PAD_SIZES = tuple(_round_up(s, LANES) for s in SPLIT_SIZES)
D_IN_PAD = sum(PAD_SIZES)
W_GLA = sum(PAD_SIZES[0:5])
W_NSA = sum(PAD_SIZES[5:9])
W_ML = sum(PAD_SIZES[9:14])
KV_W = NSA_KV_HEADS * HEAD_DIM


def _dot(a, b):
    return jnp.dot(a.astype(BF16), b.astype(BF16), preferred_element_type=F32)


def _dot_nt(a, b):
    return lax.dot_general(a.astype(BF16), b.astype(BF16), (((1,), (1,)), ((), ())), preferred_element_type=F32)


def _dot_tn(a, b):
    return lax.dot_general(a.astype(BF16), b.astype(BF16), (((0,), (0,)), ((), ())), preferred_element_type=F32)


def _dot_f32(a, b):
    return jnp.dot(a, b, precision=HIGHEST, preferred_element_type=F32)


def _dot_nt_f32(a, b):
    return lax.dot_general(a, b, (((1,), (1,)), ((), ())), precision=HIGHEST, preferred_element_type=F32)


def _dot_split(a, b):
    a_hi = a.astype(BF16)
    a_lo = (a - a_hi.astype(F32)).astype(BF16)
    bb = b.astype(BF16)
    return jnp.dot(a_hi, bb, preferred_element_type=F32) + jnp.dot(a_lo, bb, preferred_element_type=F32)


def _log_sigmoid(x):
    return jnp.minimum(x, 0.0) - jnp.log1p(jnp.exp(-jnp.abs(x)))


def _sigmoid(x):
    return 1.0 / (1.0 + jnp.exp(-x))


def _silu(x):
    return x * _sigmoid(x)


def _group_mean_matrix(width):
    g = np.kron(np.eye(width // HEAD_DIM, dtype=np.float32), np.full((HEAD_DIM, HEAD_DIM), 1.0 / HEAD_DIM, np.float32))
    return jnp.asarray(g, dtype=BF16)


def _group_norm(x, gmat, gain):
    ms = _dot_split(x * x, gmat)
    return x * lax.rsqrt(ms + EPS) * gain


def _cparams(*sem):
    return pltpu.CompilerParams(dimension_semantics=sem, vmem_limit_bytes=VMEM_LIMIT)


def _proj_in_kernel(x_ref, g_ref, w_ref, ug_ref, un_ref, um_ref):
    x = x_ref[...]
    y = x * lax.rsqrt(jnp.mean(x * x, axis=-1, keepdims=True) + EPS) * g_ref[...]
    r = jnp.dot(y.astype(BF16), w_ref[...], preferred_element_type=F32)
    ug_ref[...] = r[:, 0:W_GLA]
    un_ref[...] = r[:, W_GLA:W_GLA + W_NSA]
    um_ref[...] = r[:, W_GLA + W_NSA:D_IN_PAD]


def _proj_in(x2d, g, w_pad, tm):
    rows = x2d.shape[0]
    return pl.pallas_call(
        _proj_in_kernel,
        grid=(rows // tm,),
        in_specs=[pl.BlockSpec((tm, D_MODEL), lambda i: (i, 0)),
                  pl.BlockSpec((1, D_MODEL), lambda i: (0, 0)),
                  pl.BlockSpec((D_MODEL, D_IN_PAD), lambda i: (0, 0))],
        out_specs=[pl.BlockSpec((tm, W_GLA), lambda i: (i, 0)),
                   pl.BlockSpec((tm, W_NSA), lambda i: (i, 0)),
                   pl.BlockSpec((tm, W_ML), lambda i: (i, 0))],
        out_shape=[jax.ShapeDtypeStruct((rows, W_GLA), F32),
                   jax.ShapeDtypeStruct((rows, W_NSA), F32),
                   jax.ShapeDtypeStruct((rows, W_ML), F32)],
        compiler_params=_cparams("parallel"),
        name="proj_in",
    )(x2d, g.reshape(1, D_MODEL), w_pad)


def _pad_w_in(w_in):
    parts = []
    off = 0
    for s, p in zip(SPLIT_SIZES, PAD_SIZES):
        seg = w_in[:, off:off + s]
        if p != s:
            seg = jnp.pad(seg, ((0, 0), (0, p - s)))
        parts.append(seg)
        off += s
    return jnp.concatenate(parts, axis=1).astype(BF16)


def _proj_out_kernel(oa_ref, ocmp_ref, oslc_ref, owin_ref, gz_ref, oc_ref, x_ref, w_ref, e_ref, y_ref):
    gate = _sigmoid(gz_ref[:, 0:LANES])
    ob = (_dot_split(gate, e_ref[0]) * ocmp_ref[...] + _dot_split(gate, e_ref[1]) * oslc_ref[...]
          + _dot_split(gate, e_ref[2]) * owin_ref[...]) * _silu(gz_ref[:, LANES:LANES + NSA_WIDTH])
    y = x_ref[...]
    y = y + _dot(oa_ref[...], w_ref[0:GLA_WIDTH, :])
    y = y + _dot(ob, w_ref[GLA_WIDTH:GLA_WIDTH + NSA_WIDTH, :])
    y = y + _dot(oc_ref[...], w_ref[GLA_WIDTH + NSA_WIDTH:D_MIX, :])
    y_ref[...] = y


def _gate_expand():
    e = np.zeros((3, LANES, NSA_WIDTH), np.float32)
    for j in range(3):
        for h in range(NSA_HEADS):
            e[j, j * NSA_HEADS + h, h * HEAD_DIM:(h + 1) * HEAD_DIM] = 1.0
    return jnp.asarray(e, dtype=BF16)


def _proj_out(oa, o_cmp, o_slc, o_win, u_nsa, oc, x2d, w_bf, tm):
    rows = x2d.shape[0]
    gz_w = LANES + NSA_WIDTH
    gz_blk = (NSA_WIDTH + 6 * KV_W) // gz_w
    assert gz_blk * gz_w == NSA_WIDTH + 6 * KV_W
    row = lambda i: (i, 0)
    return pl.pallas_call(
        _proj_out_kernel,
        grid=(rows // tm,),
        in_specs=[pl.BlockSpec((tm, GLA_WIDTH), row),
                  pl.BlockSpec((tm, NSA_WIDTH), row),
                  pl.BlockSpec((tm, NSA_WIDTH), row),
                  pl.BlockSpec((tm, NSA_WIDTH), row),
                  pl.BlockSpec((tm, gz_w), lambda i: (i, gz_blk)),
                  pl.BlockSpec((tm, ML_WIDTH), row),
                  pl.BlockSpec((tm, D_MODEL), row),
                  pl.BlockSpec((D_MIX, D_MODEL), lambda i: (0, 0)),
                  pl.BlockSpec((3, LANES, NSA_WIDTH), lambda i: (0, 0, 0))],
        out_specs=pl.BlockSpec((tm, D_MODEL), row),
        out_shape=jax.ShapeDtypeStruct((rows, D_MODEL), F32),
        compiler_params=_cparams("parallel"),
        name="proj_out",
    )(oa, o_cmp, o_slc, o_win, u_nsa, oc, x2d, w_bf, _gate_expand())


def _gla_kernel(u_ref, st0_ref, wg_ref, bg_ref, ng_ref, gm_ref, o_ref, st_ref, *, bb, tl, c):
    @pl.when(pl.program_id(1) == 0)
    def _init():
        st_ref[...] = st0_ref[...]

    nk = GLA_HEADS * GLA_DK
    a_off = 2 * nk + GLA_WIDTH
    q, k, v, log_a = [], [], [], []
    for i in range(bb):
        q.append(u_ref[i, :, 0:nk] * (GLA_DK ** -0.5))
        k.append(u_ref[i, :, nk:2 * nk])
        v.append(u_ref[i, :, 2 * nk:2 * nk + GLA_WIDTH])
        pre = _dot(u_ref[i, :, a_off:a_off + LANES], wg_ref[...]) + bg_ref[...]
        log_a.append(_log_sigmoid(pre) * (1.0 / GLA_TAU))

    tril = (lax.broadcasted_iota(jnp.int32, (c, c), 0) >= lax.broadcasted_iota(jnp.int32, (c, c), 1)).astype(F32)
    hc = GLA_HEADS * c
    tri_h = (lax.broadcasted_iota(jnp.int32, (hc, c), 0) % c) >= lax.broadcasted_iota(jnp.int32, (hc, c), 1)
    k_head = lax.broadcasted_iota(jnp.int32, (1, nk), 1) // GLA_DK
    v_head = lax.broadcasted_iota(jnp.int32, (1, GLA_WIDTH), 1) // GLA_DV
    st_diag = (lax.broadcasted_iota(jnp.int32, (GLA_WIDTH, nk), 0) // GLA_DV
               == lax.broadcasted_iota(jnp.int32, (GLA_WIDTH, nk), 1) // GLA_DK)

    st = [st_ref[i] for i in range(bb)]
    outs = [[] for _ in range(bb)]
    for j in range(tl // c):
        sl = slice(j * c, (j + 1) * c)
        for i in range(bb):
            b = _dot_f32(tril, log_a[i][sl])
            blast = b[c - 1:c]
            qe = q[i][sl] * jnp.exp(b)
            ke = k[i][sl] * jnp.exp(-b)
            kl = k[i][sl] * jnp.exp(blast - b)
            vc = v[i][sl]
            qx = jnp.concatenate([jnp.where(k_head == h, qe, 0.0) for h in range(GLA_HEADS)], axis=0)
            a = jnp.where(tri_h, _dot_nt(qx, ke), 0.0)
            r = _dot(a, vc)
            o = _dot_nt(qe, st[i])
            for h in range(GLA_HEADS):
                o = o + jnp.where(v_head == h, r[h * c:(h + 1) * c], 0.0)
            st[i] = st[i] * jnp.exp(blast) + jnp.where(st_diag, _dot_tn(vc, kl), 0.0)
            outs[i].append(o)
    for i in range(bb):
        st_ref[i] = st[i]
        o = outs[i][0] if len(outs[i]) == 1 else jnp.concatenate(outs[i], axis=0)
        z = u_ref[i, :, a_off + LANES:a_off + LANES + GLA_WIDTH]
        o_ref[i] = _group_norm(o, gm_ref[...], ng_ref[...]) * _silu(z)


GLA_SEQS_PER_STEP = 4


def _gla(u_gla, st0, wg, bg, ng, b, l):
    tl = min(l, 128)
    c = min(l, 16)
    bb = math.gcd(b, GLA_SEQS_PER_STEP)
    nk = GLA_HEADS * GLA_DK
    u3 = u_gla.reshape(b, l, W_GLA)
    kern = functools.partial(_gla_kernel, bb=bb, tl=tl, c=c)
    return pl.pallas_call(
        kern,
        grid=(b // bb, l // tl),
        in_specs=[pl.BlockSpec((bb, tl, W_GLA), lambda bi, li: (bi, li, 0)),
                  pl.BlockSpec((bb, GLA_WIDTH, nk), lambda bi, li: (bi, 0, 0)),
                  pl.BlockSpec((LANES, nk), lambda bi, li: (0, 0)),
                  pl.BlockSpec((1, nk), lambda bi, li: (0, 0)),
                  pl.BlockSpec((1, GLA_WIDTH), lambda bi, li: (0, 0)),
                  pl.BlockSpec((GLA_WIDTH, GLA_WIDTH), lambda bi, li: (0, 0))],
        out_specs=[pl.BlockSpec((bb, tl, GLA_WIDTH), lambda bi, li: (bi, li, 0)),
                   pl.BlockSpec((bb, GLA_WIDTH, nk), lambda bi, li: (bi, 0, 0))],
        out_shape=[jax.ShapeDtypeStruct((b, l, GLA_WIDTH), F32),
                   jax.ShapeDtypeStruct((b, GLA_WIDTH, nk), F32)],
        compiler_params=_cparams("parallel", "arbitrary"),
        name="gla",
    )(u3, st0, wg, bg, ng, _group_mean_matrix(GLA_WIDTH))


def _gla_state_in(s):
    b = s.shape[0]
    st = jnp.swapaxes(s.astype(F32), 2, 3)
    eye = jnp.eye(GLA_HEADS, dtype=F32)
    full = st[:, :, :, None, :] * eye[None, :, None, :, None]
    return full.reshape(b, GLA_WIDTH, GLA_HEADS * GLA_DK)


def _gla_state_out(st):
    b = st.shape[0]
    full = st.reshape(b, GLA_HEADS, GLA_DV, GLA_HEADS, GLA_DK)
    diag = jnp.stack([full[:, h, :, h, :] for h in range(GLA_HEADS)], axis=1)
    return jnp.swapaxes(diag, 2, 3)


def _mlstm_kernel(u_ref, c0_ref, n0_ref, m0_ref, cv0_ref, cw_ref, cb_ref, gb_ref, ng_ref,
                  o_ref, c_ref, n_ref, m_ref, cv_ref, xp_sc, *, bb, tl, c):
    @pl.when(pl.program_id(1) == 0)
    def _init():
        c_ref[...] = c0_ref[...]
        n_ref[...] = n0_ref[...]
        m_ref[...] = m0_ref[...]
        xp_sc[:, 0:SUBLANES, :] = cv0_ref[...]

    state = [([c_ref[i, h] for h in range(ML_HEADS)],
              [n_ref[i, h:h + 1, :] for h in range(ML_HEADS)],
              [m_ref[i, h:h + 1, 0:1] for h in range(ML_HEADS)]) for i in range(bb)]
    new_state = [_mlstm_seq(i, state[i], u_ref, cw_ref, cb_ref, gb_ref, ng_ref, o_ref, cv_ref, xp_sc, tl, c)
                 for i in range(bb)]
    for i in range(bb):
        cs, ns, ms = new_state[i]
        for h in range(ML_HEADS):
            c_ref[i, h] = cs[h]
            n_ref[i, h:h + 1, :] = ns[h]
            m_ref[i, h:h + 1, :] = jnp.broadcast_to(ms[h], (1, LANES))


def _mlstm_seq(i, state, u_ref, cw_ref, cb_ref, gb_ref, ng_ref, o_ref, cv_ref, xp_sc, tl, c):
    cs, ns, ms = (list(s) for s in state)
    w2 = 2 * ML_WIDTH
    u = u_ref[i]
    xp_sc[i, SUBLANES:SUBLANES + tl, :] = u[:, 0:w2]
    conv = cb_ref[...]
    for w in range(CONV_W):
        off = SUBLANES - (CONV_W - 1) + w
        conv = conv + xp_sc[i, off:off + tl, :] * cw_ref[w:w + 1, :]
    tail = xp_sc[i, tl:tl + SUBLANES, :]
    xp_sc[i, 0:SUBLANES, :] = tail
    cv_ref[i] = tail

    qk = _silu(conv)
    mq = qk[:, 0:ML_WIDTH]
    mk = qk[:, ML_WIDTH:w2] * (ML_DH ** -0.5)
    mv = u[:, w2:w2 + ML_WIDTH]
    ifg = u[:, w2 + ML_WIDTH:w2 + ML_WIDTH + LANES] + gb_ref[...]
    logf = _log_sigmoid(ifg)
    og_off = w2 + ML_WIDTH + LANES
    og = _sigmoid(u[:, og_off:og_off + ML_WIDTH])
    zz = _silu(u[:, og_off + ML_WIDTH:og_off + 2 * ML_WIDTH])

    row = lax.broadcasted_iota(jnp.int32, (c, c), 0)
    col = lax.broadcasted_iota(jnp.int32, (c, c), 1)
    tri = row >= col
    tril = tri.astype(F32)
    lane = lax.broadcasted_iota(jnp.int32, (c, LANES), 1)

    for j in range(tl // c):
        sl = slice(j * c, (j + 1) * c)
        fcum_all = _dot_f32(tril, logf[sl])
        gates_t = jnp.where(lane < ML_HEADS, ifg[sl], fcum_all).T
        for h in range(ML_HEADS):
            hs = slice(h * ML_DH, (h + 1) * ML_DH)
            fc = fcum_all[:, ML_HEADS + h:ML_HEADS + h + 1]
            ii = ifg[sl, h:h + 1]
            dm = jnp.where(tri, fc - gates_t[ML_HEADS + h:ML_HEADS + h + 1, :] + gates_t[h:h + 1, :], NEG_BIG)
            inter = fc + ms[h]
            m = jnp.maximum(inter, jnp.max(dm, axis=-1, keepdims=True))
            w_int = jnp.exp(inter - m)
            qh, kh, vh = mq[sl, hs], mk[sl, hs], mv[sl, hs]
            sij = _dot_nt(qh, kh) * jnp.exp(dm - m)
            num = w_int * _dot_nt(qh, cs[h]) + _dot(sij, vh)
            den = w_int * jnp.sum(qh * ns[h], axis=-1, keepdims=True) + jnp.sum(sij, axis=-1, keepdims=True)
            hh = num / jnp.maximum(jnp.abs(den), jnp.exp(-m))
            m_last = m[c - 1:c]
            f_last = fc[c - 1:c]
            decay = jnp.exp(f_last + ms[h] - m_last)
            wj = jnp.exp(f_last - fc + ii - m_last)
            cs[h] = decay * cs[h] + _dot_tn(wj * vh, kh)
            ns[h] = decay * ns[h] + jnp.sum(wj * kh, axis=0, keepdims=True)
            ms[h] = m_last
            hn = hh * lax.rsqrt(jnp.mean(hh * hh, axis=-1, keepdims=True) + EPS) * ng_ref[...]
            o_ref[i, sl, hs] = hn * og[sl, hs] * zz[sl, hs]
    return cs, ns, ms


ML_SEQS_PER_STEP = 2


def _mlstm(u_ml, c0, n0, m0, conv0, cw, cb, gb, ng, b, l):
    tl = min(l, 256)
    c = tl
    bb = math.gcd(b, ML_SEQS_PER_STEP)
    w2 = 2 * ML_WIDTH
    u3 = u_ml.reshape(b, l, W_ML)
    m0b = jnp.broadcast_to(m0.astype(F32)[:, :, None], (b, ML_HEADS, LANES))
    cv0 = jnp.pad(conv0.astype(F32), ((0, 0), (SUBLANES - (CONV_W - 1), 0), (0, 0)))
    gbp = jnp.pad(gb.astype(F32).reshape(1, 2 * ML_HEADS), ((0, 0), (0, LANES - 2 * ML_HEADS)))
    kern = functools.partial(_mlstm_kernel, bb=bb, tl=tl, c=c)
    st = lambda bi, li: (bi, 0, 0)
    st4 = lambda bi, li: (bi, 0, 0, 0)
    cst = lambda bi, li: (0, 0)
    o, c_new, n_new, m_new, cv = pl.pallas_call(
        kern,
        grid=(b // bb, l // tl),
        in_specs=[pl.BlockSpec((bb, tl, W_ML), lambda bi, li: (bi, li, 0)),
                  pl.BlockSpec((bb, ML_HEADS, ML_DH, ML_DH), st4),
                  pl.BlockSpec((bb, ML_HEADS, ML_DH), st),
                  pl.BlockSpec((bb, ML_HEADS, LANES), st),
                  pl.BlockSpec((bb, SUBLANES, w2), st),
                  pl.BlockSpec((CONV_W, w2), cst),
                  pl.BlockSpec((1, w2), cst),
                  pl.BlockSpec((1, LANES), cst),
                  pl.BlockSpec((1, ML_DH), cst)],
        out_specs=[pl.BlockSpec((bb, tl, ML_WIDTH), lambda bi, li: (bi, li, 0)),
                   pl.BlockSpec((bb, ML_HEADS, ML_DH, ML_DH), st4),
                   pl.BlockSpec((bb, ML_HEADS, ML_DH), st),
                   pl.BlockSpec((bb, ML_HEADS, LANES), st),
                   pl.BlockSpec((bb, SUBLANES, w2), st)],
        out_shape=[jax.ShapeDtypeStruct((b, l, ML_WIDTH), F32),
                   jax.ShapeDtypeStruct((b, ML_HEADS, ML_DH, ML_DH), F32),
                   jax.ShapeDtypeStruct((b, ML_HEADS, ML_DH), F32),
                   jax.ShapeDtypeStruct((b, ML_HEADS, LANES), F32),
                   jax.ShapeDtypeStruct((b, SUBLANES, w2), F32)],
        scratch_shapes=[pltpu.VMEM((bb, tl + 2 * SUBLANES, w2), F32)],
        compiler_params=_cparams("parallel", "arbitrary"),
        name="mlstm",
    )(u3, c0.astype(F32), n0.astype(F32), m0b, cv0, cw.astype(F32), cb.astype(F32).reshape(1, w2), gbp,
      ng.astype(F32).reshape(1, ML_DH))
    return o, c_new, n_new, m_new[:, :, 0], cv[:, SUBLANES - (CONV_W - 1):, :]


def _rope_lanes(x, cos_t, sin_t):
    w = x.shape[1]
    half = ROT_DIM // 2
    reps = w // cos_t.shape[1]
    if reps > 1:
        cos_t = jnp.concatenate([cos_t] * reps, axis=1)
        sin_t = jnp.concatenate([sin_t] * reps, axis=1)
    lane = lax.broadcasted_iota(jnp.int32, x.shape, 1) % HEAD_DIM
    partner = jnp.where(lane < half, pltpu.roll(x, w - half, 1), pltpu.roll(x, half, 1))
    return x * cos_t + partner * sin_t


def _nsa_prep_kernel(u_ref, cos_ref, sin_ref, qg_ref, kg_ref, g4_ref, g1_ref, qn_ref, qr_ref, rows_ref, win_ref):
    u = u_ref[...]
    cos_t = cos_ref[...]
    sin_t = sin_ref[...]
    q = _group_norm(u[:, 0:NSA_WIDTH], g4_ref[...], qg_ref[...])
    qn_ref[...] = (q * ATTN_SCALE).astype(BF16)
    qr = _rope_lanes(q, cos_t, sin_t) * (ATTN_SCALE * LOG2E)
    low = lax.broadcasted_iota(jnp.int32, (qr.shape[0], LANES), 1) < HEAD_DIM
    for j in range(NSA_HEADS // 2):
        pair = qr[:, j * LANES:(j + 1) * LANES]
        swapped = pltpu.roll(pair, HEAD_DIM, 1)
        qr_ref[2 * j] = jnp.where(low, pair, swapped).astype(BF16)
        qr_ref[2 * j + 1] = jnp.where(low, swapped, pair).astype(BF16)
    kv = NSA_WIDTH
    k_slc = _rope_lanes(_group_norm(u[:, kv + 2 * KV_W:kv + 3 * KV_W], g1_ref[...], kg_ref[1:2, :]), cos_t, sin_t)
    k_win = _rope_lanes(_group_norm(u[:, kv + 4 * KV_W:kv + 5 * KV_W], g1_ref[...], kg_ref[2:3, :]), cos_t, sin_t)
    rows_ref[:, 0:2 * KV_W] = u[:, kv:kv + 2 * KV_W]
    rows_ref[:, 2 * KV_W:3 * KV_W] = k_slc
    rows_ref[:, 3 * KV_W:4 * KV_W] = u[:, kv + 3 * KV_W:kv + 4 * KV_W]
    win_ref[:, 0:KV_W] = k_win
    win_ref[:, KV_W:2 * KV_W] = u[:, kv + 5 * KV_W:kv + 6 * KV_W]


def _rope_tables(pos):
    half = ROT_DIM // 2
    inv = jnp.exp(-math.log(ROPE_THETA) * jnp.arange(half, dtype=F32) * 2.0 / ROT_DIM)
    ang = pos.astype(F32)[:, None] * inv[None, :]
    cos, sin = jnp.cos(ang), jnp.sin(ang)
    n = pos.shape[0]
    ones = jnp.ones((n, HEAD_DIM - ROT_DIM), F32)
    cos_h = jnp.concatenate([cos, cos, ones], axis=1)
    sin_h = jnp.concatenate([-sin, sin, 0.0 * ones], axis=1)
    reps = LANES // HEAD_DIM
    return jnp.tile(cos_h, (1, reps)), jnp.tile(sin_h, (1, reps))


def _nsa_prep(u_nsa, pos, q_g, k_g, l):
    rows = u_nsa.shape[0]
    tl = min(l, 256)
    nb = l // tl
    cos_t, sin_t = _rope_tables(pos)
    qg = jnp.tile(q_g.astype(F32).reshape(1, HEAD_DIM), (1, NSA_HEADS))
    kg = jnp.tile(k_g.astype(F32), (1, NSA_KV_HEADS))
    cst = lambda i: (0, 0)
    return pl.pallas_call(
        _nsa_prep_kernel,
        grid=(rows // tl,),
        in_specs=[pl.BlockSpec((tl, W_NSA), lambda i: (i, 0)),
                  pl.BlockSpec((tl, LANES), lambda i: (i % nb, 0)),
                  pl.BlockSpec((tl, LANES), lambda i: (i % nb, 0)),
                  pl.BlockSpec((1, NSA_WIDTH), cst),
                  pl.BlockSpec((3, KV_W), cst),
                  pl.BlockSpec((NSA_WIDTH, NSA_WIDTH), cst),
                  pl.BlockSpec((KV_W, KV_W), cst)],
        out_specs=[pl.BlockSpec((tl, NSA_WIDTH), lambda i: (i, 0)),
                   pl.BlockSpec((NSA_HEADS, tl, LANES), lambda i: (0, i, 0)),
                   pl.BlockSpec((tl, N_KV_SLOTS * KV_W), lambda i: (i, 0)),
                   pl.BlockSpec((tl, 2 * KV_W), lambda i: (i, 0))],
        out_shape=[jax.ShapeDtypeStruct((rows, NSA_WIDTH), BF16),
                   jax.ShapeDtypeStruct((NSA_HEADS, rows, LANES), BF16),
                   jax.ShapeDtypeStruct((rows, N_KV_SLOTS * KV_W), F32),
                   jax.ShapeDtypeStruct((rows, 2 * KV_W), F32)],
        compiler_params=_cparams("parallel"),
        name="nsa_prep",
    )(u_nsa, cos_t, sin_t, qg, kg, _group_mean_matrix(NSA_WIDTH), _group_mean_matrix(KV_W))


def _cmp_halves(xk_ref, xv_ref, w_ref, n_half):
    acc_k = jnp.zeros((n_half, 2 * KV_W), F32)
    acc_v = jnp.zeros((n_half, 2 * KV_W), F32)
    for s in range(CMP_STRIDE):
        acc_k = acc_k + _dot(xk_ref[pl.ds(s, n_half, stride=CMP_STRIDE), :], w_ref[0, s])
        acc_v = acc_v + _dot(xv_ref[pl.ds(s, n_half, stride=CMP_STRIDE), :], w_ref[1, s])
    return acc_k, acc_v


def _cmp_finish(acc_k, acc_v, bias_ref, kg_ref, g1_ref, kc_ref, vc_ref, n_half):
    valid = lax.broadcasted_iota(jnp.int32, (n_half, KV_W), 0) < n_half - 1

    def summary(acc, bias):
        return acc[:, 0:KV_W] + pltpu.roll(acc[:, KV_W:2 * KV_W], n_half - 1, 0) + bias

    kc = _group_norm(summary(acc_k, bias_ref[0:1, :]), g1_ref[...], kg_ref[...])
    kc_ref[0] = jnp.where(valid, kc, 0.0).astype(BF16)
    vc_ref[0] = jnp.where(valid, summary(acc_v, bias_ref[1:2, :]), 0.0).astype(BF16)


def _cmp_kv_kernel(xk_ref, xv_ref, w_ref, bias_ref, kg_ref, g1_ref, kc_ref, vc_ref, *, n_half):
    acc_k, acc_v = _cmp_halves(xk_ref.at[0], xv_ref.at[0], w_ref, n_half)
    _cmp_finish(acc_k, acc_v, bias_ref, kg_ref, g1_ref, kc_ref, vc_ref, n_half)


PAGES_PER_STEP = 16
CMP_PAGES_PER_STEP = 64


def _cmp_paged_kernel(pt_ref, *refs, npg, page, whole):
    page_refs = refs[:npg]
    if whole:
        w_ref, bias_ref, kg_ref, g1_ref, kc_ref, vc_ref, xk_sc, xv_sc = refs[npg:]
    else:
        w_ref, acck_ref, accv_ref, xk_sc, xv_sc = refs[npg:]
    for i in range(npg):
        xk_sc[i * page:(i + 1) * page, :] = page_refs[i][0, 0:KV_W, :].T
        xv_sc[i * page:(i + 1) * page, :] = page_refs[i][0, KV_W:2 * KV_W, :].T
    n_half = npg * page // CMP_STRIDE
    acc_k, acc_v = _cmp_halves(xk_sc, xv_sc, w_ref, n_half)
    if whole:
        _cmp_finish(acc_k, acc_v, bias_ref, kg_ref, g1_ref, kc_ref, vc_ref, n_half)
    else:
        acck_ref[0] = acc_k
        accv_ref[0] = acc_v


def _cmp_fin_kernel(acck_ref, accv_ref, bias_ref, kg_ref, g1_ref, kc_ref, vc_ref, *, n_half):
    _cmp_finish(acck_ref[0], accv_ref[0], bias_ref, kg_ref, g1_ref, kc_ref, vc_ref, n_half)


def _page_specs(npg, page, row_blk, pool_off):
    def spec(i):
        return pl.BlockSpec((1, 2 * KV_W, page), lambda bi, ji, pt: (pt[bi, ji * npg + i] + pool_off, row_blk, 0))
    return [spec(i) for i in range(npg)]


def _cmp_kv_paged(cache_t, page_table, pool_off, wcat, bias, kg0):
    b, n_pages = page_table.shape
    page = cache_t.shape[2]
    npg = math.gcd(n_pages, CMP_PAGES_PER_STEP)
    nh_step = npg * page // CMP_STRIDE
    n_half = n_pages * page // CMP_STRIDE
    kg = jnp.tile(kg0.astype(F32).reshape(1, HEAD_DIM), (1, NSA_KV_HEADS))
    if npg == n_pages:
        cst = lambda bi, ji, pt: (0, 0)
        return pl.pallas_call(
            functools.partial(_cmp_paged_kernel, npg=npg, page=page, whole=True),
            grid_spec=pltpu.PrefetchScalarGridSpec(
                num_scalar_prefetch=1, grid=(b, 1),
                in_specs=_page_specs(npg, page, 0, pool_off)
                + [pl.BlockSpec((2, CMP_STRIDE, KV_W, 2 * KV_W), lambda bi, ji, pt: (0, 0, 0, 0)),
                   pl.BlockSpec((2, KV_W), cst), pl.BlockSpec((1, KV_W), cst), pl.BlockSpec((KV_W, KV_W), cst)],
                out_specs=[pl.BlockSpec((1, n_half, KV_W), lambda bi, ji, pt: (bi, 0, 0)),
                           pl.BlockSpec((1, n_half, KV_W), lambda bi, ji, pt: (bi, 0, 0))],
                scratch_shapes=[pltpu.VMEM((npg * page, KV_W), F32), pltpu.VMEM((npg * page, KV_W), F32)]),
            out_shape=[jax.ShapeDtypeStruct((b, n_half, KV_W), BF16),
                       jax.ShapeDtypeStruct((b, n_half, KV_W), BF16)],
            compiler_params=_cparams("parallel", "arbitrary"),
            name="cmp_paged",
        )(page_table, *([cache_t] * npg), wcat, bias, kg, _group_mean_matrix(KV_W))
    kern = functools.partial(_cmp_paged_kernel, npg=npg, page=page, whole=False)
    acc_k, acc_v = pl.pallas_call(
        kern,
        grid_spec=pltpu.PrefetchScalarGridSpec(
            num_scalar_prefetch=1, grid=(b, n_pages // npg),
            in_specs=_page_specs(npg, page, 0, pool_off)
            + [pl.BlockSpec((2, CMP_STRIDE, KV_W, 2 * KV_W), lambda bi, ji, pt: (0, 0, 0, 0))],
            out_specs=[pl.BlockSpec((1, nh_step, 2 * KV_W), lambda bi, ji, pt: (bi, ji, 0)),
                       pl.BlockSpec((1, nh_step, 2 * KV_W), lambda bi, ji, pt: (bi, ji, 0))],
            scratch_shapes=[pltpu.VMEM((npg * page, KV_W), F32), pltpu.VMEM((npg * page, KV_W), F32)]),
        out_shape=[jax.ShapeDtypeStruct((b, n_half, 2 * KV_W), F32),
                   jax.ShapeDtypeStruct((b, n_half, 2 * KV_W), F32)],
        compiler_params=_cparams("parallel", "arbitrary"),
        name="cmp_paged",
    )(page_table, *([cache_t] * npg), wcat)
    blk = lambda bi: (bi, 0, 0)
    return pl.pallas_call(
        functools.partial(_cmp_fin_kernel, n_half=n_half),
        grid=(b,),
        in_specs=[pl.BlockSpec((1, n_half, 2 * KV_W), blk),
                  pl.BlockSpec((1, n_half, 2 * KV_W), blk),
                  pl.BlockSpec((2, KV_W), lambda bi: (0, 0)),
                  pl.BlockSpec((1, KV_W), lambda bi: (0, 0)),
                  pl.BlockSpec((KV_W, KV_W), lambda bi: (0, 0))],
        out_specs=[pl.BlockSpec((1, n_half, KV_W), blk), pl.BlockSpec((1, n_half, KV_W), blk)],
        out_shape=[jax.ShapeDtypeStruct((b, n_half, KV_W), BF16),
                   jax.ShapeDtypeStruct((b, n_half, KV_W), BF16)],
        compiler_params=_cparams("parallel"),
        name="cmp_fin",
    )(acc_k, acc_v, bias, kg, _group_mean_matrix(KV_W))


def _cmp_weights(cmp_pos, cmp_w):
    wf = cmp_w.astype(F32)
    eye_g = jnp.eye(NSA_KV_HEADS, dtype=F32)

    def bd(w):
        return jnp.einsum('ksde,gh->ksgdhe', w, eye_g).reshape(2, CMP_STRIDE, KV_W, KV_W)

    wcat = jnp.concatenate([bd(wf[:, :CMP_STRIDE]), bd(wf[:, CMP_STRIDE:])], axis=3).astype(BF16)
    bias = jnp.einsum('ksd,ksde->ke', cmp_pos.astype(F32), wf)
    return wcat, jnp.tile(bias, (1, NSA_KV_HEADS))


def _cmp_kv(x3, k_blk, v_blk, t_use, wcat, bias, kg0):
    b = x3.shape[0]
    n_half = t_use // CMP_STRIDE
    kern = functools.partial(_cmp_kv_kernel, n_half=n_half)
    kg = jnp.tile(kg0.astype(F32).reshape(1, HEAD_DIM), (1, NSA_KV_HEADS))
    return pl.pallas_call(
        kern,
        grid=(b,),
        in_specs=[pl.BlockSpec((1, t_use, KV_W), lambda bi: (bi, 0, k_blk)),
                  pl.BlockSpec((1, t_use, KV_W), lambda bi: (bi, 0, v_blk)),
                  pl.BlockSpec((2, CMP_STRIDE, KV_W, 2 * KV_W), lambda bi: (0, 0, 0, 0)),
                  pl.BlockSpec((2, KV_W), lambda bi: (0, 0)),
                  pl.BlockSpec((1, KV_W), lambda bi: (0, 0)),
                  pl.BlockSpec((KV_W, KV_W), lambda bi: (0, 0))],
        out_specs=[pl.BlockSpec((1, n_half, KV_W), lambda bi: (bi, 0, 0)),
                   pl.BlockSpec((1, n_half, KV_W), lambda bi: (bi, 0, 0))],
        out_shape=[jax.ShapeDtypeStruct((b, n_half, KV_W), BF16),
                   jax.ShapeDtypeStruct((b, n_half, KV_W), BF16)],
        compiler_params=_cparams("parallel"),
        name="cmp_kv",
    )(x3, x3, wcat, bias, kg, _group_mean_matrix(KV_W))


def _cmp_attn_kernel(qn_ref, kc_ref, vc_ref, ov_ref, o_ref, sel_ref, *, bb, tq, n_half, n_cmp, n_sel, nbp, pos0,
                     bias_out):
    biases = []
    qi = pl.program_id(1)
    nq = bb * tq
    pos_c = pos0 + qi * tq + lax.broadcasted_iota(jnp.int32, (tq, 1), 0)
    ncol = lax.broadcasted_iota(jnp.int32, (1, n_half), 1)
    cmask = (ncol * CMP_STRIDE + (CMP_BLOCK - 1) <= pos_c) & (ncol < n_cmp)
    pos_r = pos0 + qi * tq + lax.broadcasted_iota(jnp.int32, (1, nq), 1) % tq
    blk = lax.broadcasted_iota(jnp.int32, (nbp, 1), 0)
    cur = pos_r // SEL_BLOCK
    forced = (blk == 0) | (blk == cur) | (blk == cur - 1)
    valid = blk * SEL_BLOCK <= pos_r
    real = blk < n_sel
    for g in range(NSA_KV_HEADS):
        gs = slice(g * HEAD_DIM, (g + 1) * HEAD_DIM)
        psums = []
        for i in range(bb):
            kc = kc_ref[i, :, gs]
            vc = vc_ref[i, :, gs]
            psum = jnp.zeros((tq, n_half), F32)
            for h in range(NSA_HPG):
                hs = slice((g * NSA_HPG + h) * HEAD_DIM, (g * NSA_HPG + h + 1) * HEAD_DIM)
                s = _dot_nt(qn_ref[i, :, hs], kc)
                m = jnp.max(jnp.where(cmask, s, NEG_BIG), axis=-1, keepdims=True)
                m = jnp.where(m > 0.5 * NEG_BIG, m, 0.0)
                e = jnp.where(cmask, jnp.exp(s - m), 0.0)
                p = e / jnp.maximum(jnp.sum(e, axis=-1, keepdims=True), 1e-30)
                o_ref[i, :, hs] = _dot(p, vc)
                psum = psum + p
            psums.append(psum)
        psum = psums[0] if bb == 1 else jnp.concatenate(psums, axis=0)
        p_hi = psum.astype(BF16)
        p_lo = (psum - p_hi.astype(F32)).astype(BF16)
        ov = ov_ref[...]
        imp = _dot_nt(ov, p_hi) + _dot_nt(ov, p_lo)
        score = jnp.where(forced, 3e38, jnp.where(valid, imp, -1e38))
        score = jnp.where(real, score, -3e38)
        cnt = jnp.zeros((nbp, nq), F32)
        for jp in range(n_sel):
            rowv = score[jp:jp + 1, :]
            beats = (rowv > score) | ((rowv == score) & (blk > jp))
            cnt = cnt + jnp.where(beats, 1.0, 0.0)
        sel = (cnt < float(min(N_SELECT, n_sel))) & real
        if bias_out:
            sbg = jnp.where(sel, 0.0, NEG_BIG)
            if nbp < HEAD_DIM:
                sbg = jnp.concatenate([sbg, jnp.full((HEAD_DIM - nbp, nq), NEG_BIG, F32)], axis=0)
            biases.append(sbg)
        else:
            sel_ref[0, g * nbp:(g + 1) * nbp, :] = jnp.where(sel, 1.0, 0.0).astype(BF16)
    if bias_out:
        sel_ref[0] = jnp.concatenate(biases[::-1], axis=0).T.astype(BF16)


def _overlap_t(n_cmp, n_sel, n_half, nbp):
    a = SEL_BLOCK // CMP_STRIDE
    bb = CMP_BLOCK // CMP_STRIDE
    i = np.arange(n_half)[None, :]
    j = np.arange(nbp)[:, None]
    s = i - a * j + (bb - 1)
    cnt = np.maximum(np.minimum(np.minimum(s + 1, a + bb - 1 - s), min(a, bb)), 0)
    cnt = np.where((i < n_cmp) & (j < n_sel), cnt, 0)
    return jnp.asarray(cnt, dtype=BF16)


def _cmp_attn(qn3, kc, vc, pos0, t_len, bias_out=False):
    b, l, _ = qn3.shape
    n_half = kc.shape[1]
    n_cmp = n_half - 1
    n_sel = -(-t_len // SEL_BLOCK)
    nbp = _round_up(n_sel, 16)
    tq = min(l, LANES)
    bb = math.gcd(b, LANES // tq)
    kern = functools.partial(_cmp_attn_kernel, bb=bb, tq=tq, n_half=n_half, n_cmp=n_cmp, n_sel=n_sel, nbp=nbp,
                             pos0=pos0, bias_out=bias_out)
    if bias_out:
        assert bb == 1 and NSA_KV_HEADS == 2 and nbp <= HEAD_DIM
        sel_spec = pl.BlockSpec((1, tq, LANES), lambda bi, qi: (bi, qi, 0))
        sel_shape = jax.ShapeDtypeStruct((b, l, LANES), BF16)
    else:
        sel_spec = pl.BlockSpec((1, NSA_KV_HEADS * nbp, bb * tq), lambda bi, qi: (bi, 0, qi))
        sel_shape = jax.ShapeDtypeStruct((b // bb, NSA_KV_HEADS * nbp, bb * l), BF16)
    o_cmp, sel_t = pl.pallas_call(
        kern,
        grid=(b // bb, l // tq),
        in_specs=[pl.BlockSpec((bb, tq, NSA_WIDTH), lambda bi, qi: (bi, qi, 0)),
                  pl.BlockSpec((bb, n_half, KV_W), lambda bi, qi: (bi, 0, 0)),
                  pl.BlockSpec((bb, n_half, KV_W), lambda bi, qi: (bi, 0, 0)),
                  pl.BlockSpec((nbp, n_half), lambda bi, qi: (0, 0))],
        out_specs=[pl.BlockSpec((bb, tq, NSA_WIDTH), lambda bi, qi: (bi, qi, 0)), sel_spec],
        out_shape=[jax.ShapeDtypeStruct((b, l, NSA_WIDTH), F32), sel_shape],
        compiler_params=_cparams("parallel", "parallel"),
        name="cmp_attn",
    )(qn3, kc, vc, _overlap_t(n_cmp, n_sel, n_half, nbp))
    if bb > 1:
        sel_t = sel_t.reshape(b // bb, NSA_KV_HEADS * nbp, bb, l)
        sel_t = jnp.swapaxes(sel_t, 1, 2).reshape(b, NSA_KV_HEADS * nbp, l)
    return o_cmp, sel_t


SEL_TQ = 128
SEL_TK = 512
WIN_TQ = 256


def _group_lanes(shape, g):
    lane = lax.broadcasted_iota(jnp.int32, shape, len(shape) - 1)
    return (lane < HEAD_DIM) if g == 0 else (lane >= HEAD_DIM)


def _store_heads(o_ref, r, g, tq):
    lo = g * HEAD_DIM
    for h in range(NSA_HPG):
        hh = g * NSA_HPG + h
        o_ref[0, :, hh * HEAD_DIM:(hh + 1) * HEAD_DIM] = r[h * tq:(h + 1) * tq, lo:lo + HEAD_DIM]


def _flash_sel_kernel(qi_ref, ki_ref, q_ref, k_ref, v_ref, sb_ref, o_ref, m_sc, acc_sc, *, tq, tk):
    qi = qi_ref[pl.program_id(1)]
    ki = ki_ref[pl.program_id(1)]
    kmax = (qi * tq + tq - 1) // tk
    hq = NSA_HPG * tq

    @pl.when(ki == 0)
    def _init():
        m_sc[...] = jnp.full(m_sc.shape, NEG_BIG, F32)
        acc_sc[...] = jnp.zeros(acc_sc.shape, F32)

    def step(diagonal):
        key_blk = ki * (tk // SEL_BLOCK) + lax.broadcasted_iota(jnp.int32, (tk, LANES), 0) // SEL_BLOCK
        one_hot = jnp.where(lax.broadcasted_iota(jnp.int32, (tk, LANES), 1) % HEAD_DIM == key_blk, 1.0, 0.0).astype(BF16)
        kblk = k_ref[0].astype(BF16)
        vblk = v_ref[0].astype(BF16)
        sb4 = jnp.concatenate([sb_ref[0]] * NSA_HPG, axis=0)
        if diagonal:
            qpos = qi * tq + lax.broadcasted_iota(jnp.int32, (tq, tk), 0)
            kpos = ki * tk + lax.broadcasted_iota(jnp.int32, (tq, tk), 1)
            causal4 = jnp.concatenate([jnp.where(kpos <= qpos, 0.0, NEG_BIG)] * NSA_HPG, axis=0)
        m_prev = [m_sc[g] for g in range(NSA_KV_HEADS)]
        acc_prev = [acc_sc[g] for g in range(NSA_KV_HEADS)]
        m_out, acc_out = [], []
        for g in range(NSA_KV_HEADS):
            own_k = _group_lanes((tk, LANES), g)
            q4 = q_ref[g * NSA_HPG:(g + 1) * NSA_HPG].reshape(hq, LANES)
            qx = jnp.where(_group_lanes((hq, LANES), g), q4, sb4)
            kx = jnp.where(own_k, kblk, one_hot)
            s = lax.dot_general(qx, kx, (((1,), (1,)), ((), ())), preferred_element_type=F32)
            if diagonal:
                s = s + causal4
            m_new = jnp.maximum(m_prev[g], jnp.max(s, axis=-1, keepdims=True))
            p = jnp.exp2(s - m_new[:, :1]).astype(BF16)
            vx = jnp.where(own_k, vblk, 1.0)
            acc_out.append(jnp.exp2(m_prev[g] - m_new) * acc_prev[g] + jnp.dot(p, vx, preferred_element_type=F32))
            m_out.append(m_new)
        for g in range(NSA_KV_HEADS):
            m_sc[g] = m_out[g]
            acc_sc[g] = acc_out[g]

    @pl.when(ki < kmax)
    def _full():
        step(False)

    @pl.when(ki == kmax)
    def _last():
        step(True)
        for g in range(NSA_KV_HEADS):
            acc = acc_sc[g]
            _store_heads(o_ref, acc / pltpu.roll(acc, HEAD_DIM, 1), g, tq)


def _flash_sel(q8, rows3, selb):
    b, l, _ = rows3.shape
    tq, tk = min(SEL_TQ, l), min(SEL_TK, l)
    nq = l // tq
    pairs = [(qi, ki) for qi in range(nq) for ki in range((qi * tq + tq - 1) // tk + 1)]
    qi_tab = jnp.asarray([p[0] for p in pairs], jnp.int32)
    ki_tab = jnp.asarray([p[1] for p in pairs], jnp.int32)
    kv_idx = lambda blk: (lambda bi, si, qt, kt: (bi, kt[si], blk))
    return pl.pallas_call(
        functools.partial(_flash_sel_kernel, tq=tq, tk=tk),
        grid_spec=pltpu.PrefetchScalarGridSpec(
            num_scalar_prefetch=2, grid=(b, len(pairs)),
            in_specs=[pl.BlockSpec((NSA_HEADS, tq, LANES), lambda bi, si, qt, kt: (0, bi * nq + qt[si], 0)),
                      pl.BlockSpec((1, tk, KV_W), kv_idx(2)),
                      pl.BlockSpec((1, tk, KV_W), kv_idx(3)),
                      pl.BlockSpec((1, tq, LANES), lambda bi, si, qt, kt: (bi, qt[si], 0))],
            out_specs=pl.BlockSpec((1, tq, NSA_WIDTH), lambda bi, si, qt, kt: (bi, qt[si], 0)),
            scratch_shapes=[pltpu.VMEM((NSA_KV_HEADS, NSA_HPG * tq, LANES), F32),
                            pltpu.VMEM((NSA_KV_HEADS, NSA_HPG * tq, LANES), F32)]),
        out_shape=jax.ShapeDtypeStruct((b, l, NSA_WIDTH), F32),
        compiler_params=_cparams("parallel", "arbitrary"),
        name="flash_sel",
    )(qi_tab, ki_tab, q8, rows3, rows3, selb)


def _flash_win_kernel(q_ref, *refs, tq, back):
    nspan = back + 1
    k_refs, v_refs, o_ref = refs[:nspan], refs[nspan:2 * nspan], refs[2 * nspan]
    qi = pl.program_id(1)
    hq = NSA_HPG * tq
    span = nspan * tq
    qpos = qi * tq + lax.broadcasted_iota(jnp.int32, (tq, span), 0)
    kpos = (qi - back) * tq + lax.broadcasted_iota(jnp.int32, (tq, span), 1)
    ok = (kpos <= qpos) & (kpos > qpos - WINDOW) & (kpos >= 0)
    bias4 = jnp.concatenate([jnp.where(ok, 0.0, NEG_BIG)] * NSA_HPG, axis=0)
    kcat = jnp.concatenate([r[0] for r in k_refs], axis=0).astype(BF16)
    vcat = jnp.concatenate([r[0] for r in v_refs], axis=0).astype(BF16)
    for g in range(NSA_KV_HEADS):
        q4 = q_ref[g * NSA_HPG:(g + 1) * NSA_HPG].reshape(hq, LANES)
        qx = jnp.where(_group_lanes((hq, LANES), g), q4, 0.0)
        s = lax.dot_general(qx, kcat, (((1,), (1,)), ((), ())), preferred_element_type=F32) + bias4
        p = jnp.exp2(s - jnp.max(s, axis=-1, keepdims=True)).astype(BF16)
        vx = jnp.where(_group_lanes((span, LANES), g), vcat, 1.0)
        r = jnp.dot(p, vx, preferred_element_type=F32)
        _store_heads(o_ref, r / pltpu.roll(r, HEAD_DIM, 1), g, tq)


def _flash_win(q8, win3):
    b, l, _ = win3.shape
    tq = min(WIN_TQ, l)
    nq = l // tq
    back = -(-(WINDOW - 1) // tq)

    def kv_specs(blk):
        return [pl.BlockSpec((1, tq, KV_W), (lambda bi, qi, j=j: (bi, jnp.maximum(qi - back + j, 0), blk)))
                for j in range(back + 1)]

    return pl.pallas_call(
        functools.partial(_flash_win_kernel, tq=tq, back=back),
        grid=(b, nq),
        in_specs=[pl.BlockSpec((NSA_HEADS, tq, LANES), lambda bi, qi: (0, bi * nq + qi, 0))]
        + kv_specs(0) + kv_specs(1),
        out_specs=pl.BlockSpec((1, tq, NSA_WIDTH), lambda bi, qi: (bi, qi, 0)),
        out_shape=jax.ShapeDtypeStruct((b, l, NSA_WIDTH), F32),
        compiler_params=_cparams("parallel", "parallel"),
        name="flash_win",
    )(q8, *([win3] * (2 * (back + 1))))


def _softmax_update(s, m_prev, l_prev):
    m_new = jnp.maximum(m_prev, jnp.max(s, axis=-1, keepdims=True))
    alpha = jnp.exp2(m_prev - m_new)
    p = jnp.exp2(s - m_new[:, :1])
    return m_new, alpha, alpha * l_prev + jnp.sum(p, axis=-1, keepdims=True), p


def _paged_attn_kernel(pt_ref, *refs, npg, page, l_new, nbp):
    page_refs = refs[:npg]
    (qz_ref, sel_ref, kn_ref, vn_ref, win_ref, kwn_ref, vwn_ref,
     oslc_ref, owin_ref, wout_ref, m_sc, l_sc, acc_sc) = refs[npg:]
    ji = pl.program_id(1)
    nr = qz_ref.shape[1]
    span = npg * page

    @pl.when(ji == 0)
    def _init():
        m_sc[...] = jnp.full(m_sc.shape, NEG_BIG, F32)
        l_sc[...] = jnp.zeros(l_sc.shape, F32)
        acc_sc[...] = jnp.zeros(acc_sc.shape, F32)

    qz = qz_ref[0]
    kt = jnp.concatenate([page_refs[i][0, 0:KV_W, :] for i in range(npg)], axis=1).astype(BF16)
    vt = jnp.concatenate([page_refs[i][0, KV_W:2 * KV_W, :] for i in range(npg)], axis=1).astype(BF16)
    s = jnp.dot(qz, kt, preferred_element_type=F32)
    e_row = lax.broadcasted_iota(jnp.int32, (nbp, span), 0)
    e_col = lax.broadcasted_iota(jnp.int32, (nbp, span), 1)
    expand = jnp.where(e_row == ji * (span // SEL_BLOCK) + e_col // SEL_BLOCK, 1.0, 0.0).astype(BF16)
    picked = jnp.dot(sel_ref[0], expand, preferred_element_type=F32) > 0.5
    m_new, alpha, l_new_v, p = _softmax_update(jnp.where(picked, s, NEG_BIG), m_sc[...], l_sc[...])
    m_sc[...] = m_new
    l_sc[...] = l_new_v
    acc_sc[...] = alpha * acc_sc[...] + _dot_nt(p, vt)

    @pl.when(ji == pl.num_programs(1) - 1)
    def _fin():
        q_of_row = lax.broadcasted_iota(jnp.int32, (nr, l_new), 0) % l_new
        j_new = lax.broadcasted_iota(jnp.int32, (nr, l_new), 1)
        new_ok = j_new <= q_of_row
        sn = jnp.where(new_ok, _dot_nt(qz, kn_ref[0]), NEG_BIG)
        m2, a2, l2, p2 = _softmax_update(sn, m_sc[...], l_sc[...])
        oslc_ref[0] = (a2 * acc_sc[...] + _dot(p2, vn_ref[0])) / l2
        wlen = win_ref.shape[2]
        kw = win_ref[0, 0:KV_W, :]
        vw = win_ref[0, KV_W:2 * KV_W, :]
        i_old = lax.broadcasted_iota(jnp.int32, (nr, wlen), 1)
        q_old = lax.broadcasted_iota(jnp.int32, (nr, wlen), 0) % l_new
        sw = jnp.where(i_old + (WINDOW - wlen) > q_old, _dot(qz, kw), NEG_BIG)
        swn = jnp.where(new_ok, _dot_nt(qz, kwn_ref[0]), NEG_BIG)
        mw = jnp.maximum(jnp.max(sw, axis=-1, keepdims=True), jnp.max(swn, axis=-1, keepdims=True))
        pw = jnp.exp2(sw - mw)
        pwn = jnp.exp2(swn - mw)
        lw = jnp.sum(pw, axis=-1, keepdims=True) + jnp.sum(pwn, axis=-1, keepdims=True)
        owin_ref[0] = (_dot_nt(pw, vw) + _dot(pwn, vwn_ref[0])) / lw
        new_t = jnp.concatenate([kwn_ref[0], vwn_ref[0]], axis=1).T
        place = (lax.broadcasted_iota(jnp.int32, (l_new, wlen), 1)
                 == lax.broadcasted_iota(jnp.int32, (l_new, wlen), 0) + (wlen - l_new)).astype(F32)
        lane = lax.broadcasted_iota(jnp.int32, (2 * KV_W, wlen), 1)
        wout_ref[0] = jnp.where(lane < wlen - l_new, pltpu.roll(win_ref[0], wlen - l_new, 1), _dot_f32(new_t, place))


def _paged_attn(cache_t, page_table, pool_off, qz, sel_rows, rows3, win_t, win_off, win3):
    b, n_pages = page_table.shape
    page = cache_t.shape[2]
    npg = PAGES_PER_STEP
    nr = qz.shape[1]
    l_new = rows3.shape[1]
    nbp = sel_rows.shape[2]
    wlen = win_t.shape[2]
    kern = functools.partial(_paged_attn_kernel, npg=npg, page=page, l_new=l_new, nbp=nbp)
    per_b = lambda bi, ji, pt: (bi, 0, 0)
    return pl.pallas_call(
        kern,
        grid_spec=pltpu.PrefetchScalarGridSpec(
            num_scalar_prefetch=1, grid=(b, n_pages // npg),
            in_specs=_page_specs(npg, page, 1, pool_off)
            + [pl.BlockSpec((1, nr, KV_W), per_b),
               pl.BlockSpec((1, nr, nbp), per_b),
               pl.BlockSpec((1, l_new, KV_W), lambda bi, ji, pt: (bi, 0, 2)),
               pl.BlockSpec((1, l_new, KV_W), lambda bi, ji, pt: (bi, 0, 3)),
               pl.BlockSpec((1, 2 * KV_W, wlen), lambda bi, ji, pt: (bi + win_off, 0, 0)),
               pl.BlockSpec((1, l_new, KV_W), lambda bi, ji, pt: (bi, 0, 0)),
               pl.BlockSpec((1, l_new, KV_W), lambda bi, ji, pt: (bi, 0, 1))],
            out_specs=[pl.BlockSpec((1, nr, KV_W), per_b),
                       pl.BlockSpec((1, nr, KV_W), per_b),
                       pl.BlockSpec((1, 2 * KV_W, wlen), per_b)],
            scratch_shapes=[pltpu.VMEM((nr, LANES), F32), pltpu.VMEM((nr, LANES), F32),
                            pltpu.VMEM((nr, KV_W), F32)]),
        out_shape=[jax.ShapeDtypeStruct((b, nr, KV_W), F32),
                   jax.ShapeDtypeStruct((b, nr, KV_W), F32),
                   jax.ShapeDtypeStruct((b, 2 * KV_W, wlen), F32)],
        compiler_params=_cparams("parallel", "arbitrary"),
        name="paged_attn",
    )(page_table, *([cache_t] * npg), qz, sel_rows, rows3, rows3, win_t, win3, win3)


def _rms_norm(x, g, eps=EPS):
    xf = x.astype(F32)
    y = xf * lax.rsqrt(jnp.mean(xf * xf, axis=-1, keepdims=True) + eps)
    return (y * g.astype(F32)).astype(x.dtype)


def _rope(x, pos):
    half = ROT_DIM // 2
    inv = jnp.exp(-math.log(ROPE_THETA) * jnp.arange(half, dtype=F32) * 2.0 / ROT_DIM)
    ang = pos.astype(F32)[:, None] * inv[None, :]
    cos = jnp.cos(ang)[:, None, :]
    sin = jnp.sin(ang)[:, None, :]
    xf = x.astype(F32)
    x1, x2, rest = xf[..., :half], xf[..., half:ROT_DIM], xf[..., ROT_DIM:]
    return jnp.concatenate([x1 * cos - x2 * sin, x2 * cos + x1 * sin, rest], axis=-1).astype(x.dtype)


def _masked_softmax(s, mask):
    s = jnp.where(mask, s, -jnp.inf)
    m = jnp.max(s, axis=-1, keepdims=True)
    m = jnp.where(jnp.isfinite(m), m, 0.0)
    e = jnp.where(mask, jnp.exp(s - m), 0.0)
    return e / jnp.maximum(jnp.sum(e, axis=-1, keepdims=True), 1e-30)


def _cmp_summaries(x, pe, w):
    b, t, g, d = x.shape
    n_half = t // CMP_STRIDE
    halves = x[:, :n_half * CMP_STRIDE].astype(F32).reshape(b, n_half, CMP_STRIDE, g, d)
    wf = w.astype(F32)
    first = jnp.einsum('bnsgd,sde->bnge', halves[:, :-1], wf[:CMP_STRIDE])
    second = jnp.einsum('bnsgd,sde->bnge', halves[:, 1:], wf[CMP_STRIDE:])
    bias = jnp.einsum('sd,sde->e', pe.astype(F32), wf)
    return first + second + bias


def _overlap_matrix(n_cmp, n_sel):
    a = SEL_BLOCK // CMP_STRIDE
    bb = CMP_BLOCK // CMP_STRIDE
    i = jnp.arange(n_cmp)[:, None]
    j = jnp.arange(n_sel)[None, :]
    s = i - a * j + (bb - 1)
    cnt = jnp.minimum(jnp.minimum(s + 1, a + bb - 1 - s), min(a, bb))
    return jnp.maximum(cnt, 0).astype(F32)


def _nsa_selected_gather(q, k_hist, v_hist, idx, pos):
    b, l = q.shape[:2]
    t_len = k_hist.shape[1]
    n_sel = -(-t_len // SEL_BLOCK)
    n_k = idx.shape[-1]
    pad = n_sel * SEL_BLOCK - t_len

    def blocks(x):
        x = jnp.pad(x.astype(F32), ((0, 0), (0, pad), (0, 0), (0, 0)))
        return x.reshape(b, n_sel, SEL_BLOCK, NSA_KV_HEADS, HEAD_DIM).transpose(0, 3, 1, 2, 4)

    kb, vb = blocks(k_hist), blocks(v_hist)
    qb = math.gcd(l, Q_BLOCK)
    nq = l // qb
    qx = q.astype(F32).reshape(b * nq, qb, NSA_KV_HEADS, NSA_HPG, HEAD_DIM)
    ix = idx.reshape(b, NSA_KV_HEADS, nq, qb, n_k).transpose(0, 2, 1, 3, 4).reshape(b * nq, NSA_KV_HEADS, qb, n_k)
    px = jnp.tile(pos.reshape(nq, qb), (b, 1))
    bx = jnp.repeat(jnp.arange(b), nq)
    g_ix = jnp.arange(NSA_KV_HEADS)[:, None, None]
    offs = jnp.arange(SEL_BLOCK)
    n_keys = n_k * SEL_BLOCK

    def attend(args):
        qi, ii, pi, bi = args
        ks = kb[bi][g_ix, ii]
        vs = vb[bi][g_ix, ii]
        kpos = ii[..., None] * SEL_BLOCK + offs
        mask = (kpos <= pi[None, :, None, None]).reshape(NSA_KV_HEADS, 1, qb, n_keys)
        s = jnp.einsum('qghd,gqkjd->ghqkj', qi, ks).reshape(NSA_KV_HEADS, NSA_HPG, qb, n_keys) * ATTN_SCALE
        p = _masked_softmax(s, mask)
        return jnp.einsum('ghqn,gqnd->qghd', p, vs.reshape(NSA_KV_HEADS, qb, n_keys, HEAD_DIM))

    o = lax.map(attend, (qx, ix, px, bx))
    return o.reshape(b, l, NSA_HEADS, HEAD_DIM)


def _nsa_window(q, k_ctx, v_ctx, pos0):
    b, l = q.shape[:2]
    lc = k_ctx.shape[1]
    padw = ((0, 0), (WINDOW, 0), (0, 0), (0, 0))
    kp = jnp.pad(k_ctx.astype(F32), padw)
    vp = jnp.pad(v_ctx.astype(F32), padw)
    qb = math.gcd(l, Q_BLOCK)
    nq = l // qb
    span = WINDOW + qb
    qx = jnp.moveaxis(q.astype(F32).reshape(b, nq, qb, NSA_KV_HEADS, NSA_HPG, HEAD_DIM), 1, 0)
    starts = jnp.arange(nq) * qb
    first_pos = pos0 + l - lc

    def attend(args):
        qi, i0 = args
        kk = lax.dynamic_slice_in_dim(kp, i0 + lc - l, span, axis=1)
        vv = lax.dynamic_slice_in_dim(vp, i0 + lc - l, span, axis=1)
        kpos = pos0 + i0 - WINDOW + jnp.arange(span)
        qpos = pos0 + i0 + jnp.arange(qb)
        mask = ((kpos[None, :] >= first_pos) & (kpos[None, :] <= qpos[:, None])
                & (kpos[None, :] > qpos[:, None] - WINDOW))
        s = jnp.einsum('bqghd,bkgd->bghqk', qi, kk) * ATTN_SCALE
        p = _masked_softmax(s, mask)
        return jnp.einsum('bghqk,bkgd->bqghd', p, vv)

    o = lax.map(attend, (qx, starts))
    return jnp.moveaxis(o, 0, 1).reshape(b, l, NSA_HEADS, HEAD_DIM)


def _nsa_mixer_jax(n_q, n_kv, n_g, past_kv, win_buf, win_keep, pos0, q_g, k_g, cmp_pos, cmp_w):
    b, l, _ = n_q.shape
    pos = pos0 + jnp.arange(l)
    q = _rms_norm(n_q.reshape(b, l, NSA_HEADS, HEAD_DIM), q_g)
    q_rot = _rope(q, pos)
    kv = n_kv.reshape(b, l, 6, NSA_KV_HEADS, HEAD_DIM)
    k_slc = _rope(_rms_norm(kv[:, :, 2], k_g[1]), pos)
    k_win = _rope(_rms_norm(kv[:, :, 4], k_g[2]), pos)
    new_rows = jnp.stack([kv[:, :, 0], kv[:, :, 1], k_slc, kv[:, :, 3]], axis=2)
    new_win = jnp.stack([k_win, kv[:, :, 5]], axis=2)
    hist = jnp.concatenate([past_kv.astype(new_rows.dtype), new_rows], axis=1)
    ctx = jnp.concatenate([win_buf.astype(new_win.dtype), new_win], axis=1)
    t_len = hist.shape[1]

    kc = _rms_norm(_cmp_summaries(hist[:, :, 0], cmp_pos[0], cmp_w[0]), k_g[0])
    vc = _cmp_summaries(hist[:, :, 1], cmp_pos[1], cmp_w[1])
    n_cmp = kc.shape[1]
    qg = q.astype(F32).reshape(b, l, NSA_KV_HEADS, NSA_HPG, HEAD_DIM)
    s = jnp.einsum('blghd,bngd->bghln', qg, kc) * ATTN_SCALE
    cmp_end = jnp.arange(n_cmp) * CMP_STRIDE + CMP_BLOCK - 1
    p_cmp = _masked_softmax(s, cmp_end[None, :] <= pos[:, None])
    o_cmp = jnp.einsum('bghln,bngd->blghd', p_cmp, vc).reshape(b, l, NSA_HEADS, HEAD_DIM)

    n_sel = -(-t_len // SEL_BLOCK)
    imp = jnp.einsum('bghln,nj->bglj', p_cmp, _overlap_matrix(n_cmp, n_sel))
    blk = jnp.arange(n_sel)[None, :]
    cur = (pos // SEL_BLOCK)[:, None]
    valid = blk * SEL_BLOCK <= pos[:, None]
    forced = (blk == 0) | (blk == cur) | (blk == cur - 1)
    score = jnp.where(forced, jnp.inf, jnp.where(valid, imp, -jnp.inf))
    _, idx = lax.top_k(score, min(N_SELECT, n_sel))
    o_slc = _nsa_selected_gather(q_rot, hist[:, :, 2], hist[:, :, 3], idx, pos)
    o_win = _nsa_window(q_rot, ctx[:, :, 0], ctx[:, :, 1], pos0)

    gate = jax.nn.sigmoid(n_g.astype(F32)).reshape(b, l, 3, NSA_HEADS, 1)
    o = gate[:, :, 0] * o_cmp + gate[:, :, 1] * o_slc + gate[:, :, 2] * o_win
    return o.reshape(b, l, NSA_WIDTH), new_rows, ctx[:, ctx.shape[1] - win_keep:]


def _nsa_fresh(u_nsa, b, l, win_keep, q_g, k_g, cmp_pos, cmp_w):
    qn, qr, rows, win = _nsa_prep(u_nsa, jnp.arange(l), q_g, k_g, l)
    rows3 = rows.reshape(b, l, N_KV_SLOTS * KV_W)
    win3 = win.reshape(b, l, 2 * KV_W)
    wcat, bias = _cmp_weights(cmp_pos, cmp_w)
    t_use = (l // CMP_STRIDE) * CMP_STRIDE
    kc, vc = _cmp_kv(rows3, 0, 1, t_use, wcat, bias, k_g[0])
    o_cmp, selb = _cmp_attn(qn.reshape(b, l, NSA_WIDTH), kc, vc, 0, l, bias_out=True)
    o_slc = _flash_sel(qr, rows3, selb)
    o_win = _flash_win(qr, win3)
    branches = tuple(o.reshape(b * l, NSA_WIDTH) for o in (o_cmp, o_slc, o_win))
    new_rows = rows3.reshape(b, l, N_KV_SLOTS, NSA_KV_HEADS, HEAD_DIM)
    new_win = win3[:, l - win_keep:].reshape(b, win_keep, 2, NSA_KV_HEADS, HEAD_DIM)
    return branches, new_rows, new_win


def _nsa_paged(u_nsa, b, l, paged, q_g, k_g, cmp_pos, cmp_w):
    cache_t, page_table, pool_off, win_t, win_off = paged
    past_len = page_table.shape[1] * cache_t.shape[2]
    assert (past_len + l) // CMP_STRIDE == past_len // CMP_STRIDE and past_len % SEL_BLOCK == 0
    qn, qr, rows, win = _nsa_prep(u_nsa, past_len + jnp.arange(l), q_g, k_g, l)
    rows3 = rows.reshape(b, l, N_KV_SLOTS * KV_W)
    win3 = win.reshape(b, l, 2 * KV_W)
    wcat, bias = _cmp_weights(cmp_pos, cmp_w)
    kc, vc = _cmp_kv_paged(cache_t, page_table, pool_off, wcat, bias, k_g[0])
    o_cmp, sel_t = _cmp_attn(qn.reshape(b, l, NSA_WIDTH), kc, vc, past_len, past_len + l)
    nbp = sel_t.shape[1] // NSA_KV_HEADS
    sel_rows = jnp.swapaxes(sel_t.reshape(b, NSA_KV_HEADS, 1, nbp, l), 3, 4)
    sel_rows = jnp.broadcast_to(sel_rows, (b, NSA_KV_HEADS, NSA_HPG, l, nbp)).reshape(b, NSA_HEADS * l, nbp)
    q5 = jnp.transpose(qr[:, :, :HEAD_DIM].reshape(NSA_KV_HEADS, NSA_HPG, b, l, HEAD_DIM), (2, 0, 1, 3, 4))
    qz = jnp.einsum('bghqd,gk->bghqkd', q5, jnp.eye(NSA_KV_HEADS, dtype=q5.dtype)).reshape(b, NSA_HEADS * l, KV_W)
    o_slc_z, o_win_z, wout = _paged_attn(cache_t, page_table, pool_off, qz, sel_rows, rows3, win_t, win_off, win3)

    def own_group(o):
        o6 = o.reshape(b, NSA_KV_HEADS, NSA_HPG, l, NSA_KV_HEADS, HEAD_DIM)
        d = jnp.stack([o6[:, g, :, :, g, :] for g in range(NSA_KV_HEADS)], axis=1)
        return jnp.transpose(d, (0, 3, 1, 2, 4)).reshape(b * l, NSA_WIDTH)

    branches = (o_cmp.reshape(b * l, NSA_WIDTH), own_group(o_slc_z), own_group(o_win_z))
    new_rows = rows3.reshape(b, l, N_KV_SLOTS, NSA_KV_HEADS, HEAD_DIM)
    wlen = wout.shape[2]
    new_win = jnp.transpose(wout.reshape(b, 2, NSA_KV_HEADS, HEAD_DIM, wlen), (0, 4, 1, 2, 3))
    return branches, new_rows, new_win


def _hybrid_layer(x, paged, win_keep, s_gla, c_ml, n_ml, m_ml, conv_ml,
                  norm_g, w_in_pad, w_out_bf, gla_w_gate, gla_b_gate, gla_norm_g,
                  nsa_q_norm_g, nsa_k_norm_g, nsa_cmp_pos, nsa_cmp_w,
                  ml_conv_w, ml_conv_b, ml_gate_b, ml_norm_g):
    b, l, _ = x.shape
    rows = b * l
    tm = 256
    x2d = x.reshape(rows, D_MODEL)
    u_gla, u_nsa, u_ml = _proj_in(x2d, norm_g, w_in_pad, tm)

    nk = GLA_HEADS * GLA_DK
    wg = jnp.pad(gla_w_gate.astype(F32), ((0, LANES - GLA_RANK), (0, 0))).astype(BF16)
    o_a, st_new = _gla(u_gla, _gla_state_in(s_gla), wg, gla_b_gate.astype(F32).reshape(1, nk),
                       jnp.tile(gla_norm_g.astype(F32).reshape(1, GLA_DV), (1, GLA_HEADS)), b, l)
    s_new = _gla_state_out(st_new)

    if paged is None:
        o_nsa, new_rows, new_win = _nsa_fresh(u_nsa, b, l, win_keep, nsa_q_norm_g, nsa_k_norm_g,
                                              nsa_cmp_pos, nsa_cmp_w)
    else:
        o_nsa, new_rows, new_win = _nsa_paged(u_nsa, b, l, paged, nsa_q_norm_g, nsa_k_norm_g,
                                              nsa_cmp_pos, nsa_cmp_w)

    o_c, c_new, n_new, m_new, conv_new = _mlstm(u_ml, c_ml, n_ml, m_ml, conv_ml, ml_conv_w, ml_conv_b,
                                                ml_gate_b, ml_norm_g, b, l)

    y = _proj_out(o_a.reshape(rows, GLA_WIDTH), *o_nsa, u_nsa, o_c.reshape(rows, ML_WIDTH), x2d, w_out_bf, tm)
    return y.reshape(b, l, D_MODEL), new_rows, new_win, s_new, c_new, n_new, m_new, conv_new


def kernel(x_prompt, x_sample, cache_nsa_kv, state_nsa_win, state_gla, state_mlstm_C, state_mlstm_n,
           state_mlstm_m, state_mlstm_conv, page_table, norm_g, w_in, w_out, gla_w_gate, gla_b_gate,
           gla_norm_g, nsa_q_norm_g, nsa_k_norm_g, nsa_cmp_pos, nsa_cmp_w, ml_conv_w, ml_conv_b,
           ml_gate_b, ml_norm_g):
    bp, sp, _ = x_prompt.shape
    bs, _, _ = x_sample.shape
    depth = w_in.shape[0]
    n_pages = page_table.shape[1]
    past_len = n_pages * cache_nsa_kv.shape[2]
    dt = x_prompt.dtype
    zero_gla = jnp.zeros((bp, GLA_HEADS, GLA_DK, GLA_DV), F32)
    zero_c = jnp.zeros((bp, ML_HEADS, ML_DH, ML_DH), F32)
    zero_n = jnp.zeros((bp, ML_HEADS, ML_DH), F32)
    zero_m = jnp.zeros((bp, ML_HEADS), F32)
    zero_conv = jnp.zeros((bp, CONV_W - 1, 2 * ML_WIDTH), dt)
    keep_p = min(WINDOW, sp)
    keep_s = state_nsa_win.shape[2]
    n_pool = cache_nsa_kv.shape[1]
    cache_t = jnp.transpose(cache_nsa_kv, (0, 1, 3, 4, 5, 2)).reshape(
        depth * n_pool, N_KV_SLOTS * KV_W, cache_nsa_kv.shape[2]).astype(F32)
    win_t = jnp.transpose(state_nsa_win, (0, 1, 3, 4, 5, 2)).reshape(depth * bs, 2 * KV_W, keep_s).astype(F32)

    y_prompt, y_sample = x_prompt, x_sample
    p_layers, s_layers = [], []
    for layer in range(depth):
        w = (norm_g[layer], _pad_w_in(w_in[layer]), w_out[layer].astype(BF16), gla_w_gate[layer],
             gla_b_gate[layer], gla_norm_g[layer],
             nsa_q_norm_g[layer], nsa_k_norm_g[layer], nsa_cmp_pos[layer], nsa_cmp_w[layer],
             ml_conv_w[layer], ml_conv_b[layer], ml_gate_b[layer], ml_norm_g[layer])
        y_prompt, *p_new = _hybrid_layer(y_prompt, None, keep_p, zero_gla, zero_c, zero_n,
                                         zero_m, zero_conv, *w)
        paged = (cache_t, page_table, layer * n_pool, win_t, layer * bs)
        y_sample, *s_new = _hybrid_layer(y_sample, paged, keep_s,
                                         state_gla[layer], state_mlstm_C[layer], state_mlstm_n[layer],
                                         state_mlstm_m[layer], state_mlstm_conv[layer], *w)
        p_layers.append(p_new)
        s_layers.append(s_new)
    p_kv, p_win, p_gla, p_c, p_n, p_m, p_conv = [jnp.stack(z) for z in zip(*p_layers)]
    s_kv, s_win, s_gla, s_c, s_n, s_m, s_conv = [jnp.stack(z) for z in zip(*s_layers)]
    return (y_prompt, y_sample, p_kv, s_kv, p_win, s_win, p_gla, s_gla, p_c, s_c, p_n, s_n, p_m, s_m, p_conv, s_conv)
```

```python
import functools
import math

import jax
import jax.numpy as jnp
import numpy as np
from jax import lax
from jax.experimental import pallas as pl
from jax.experimental.pallas import tpu as pltpu

F32 = jnp.float32
BF16 = jnp.bfloat16
HIGHEST = lax.Precision.HIGHEST

D_MODEL = 1024
HEAD_DIM = 64
GLA_WIDTH = D_MODEL // 4
NSA_WIDTH = D_MODEL // 2
ML_WIDTH = D_MODEL - GLA_WIDTH - NSA_WIDTH
D_MIX = GLA_WIDTH + NSA_WIDTH + ML_WIDTH

GLA_HEADS = GLA_WIDTH // HEAD_DIM
GLA_DK = HEAD_DIM // 2
GLA_DV = HEAD_DIM
GLA_RANK = 16
GLA_TAU = 16.0
GLA_CHUNK = 64

NSA_HEADS = NSA_WIDTH // HEAD_DIM
NSA_KV_HEADS = 2
NSA_HPG = NSA_HEADS // NSA_KV_HEADS
CMP_BLOCK = 32
CMP_STRIDE = 16
SEL_BLOCK = 64
N_SELECT = 16
WINDOW = 512
Q_BLOCK = 128
N_KV_SLOTS = 4
ROT_DIM = HEAD_DIM // 4
ROPE_THETA = 500000.0
ATTN_SCALE = HEAD_DIM ** -0.5

ML_HEADS = ML_WIDTH // HEAD_DIM
ML_DH = HEAD_DIM
ML_CHUNK = 64
CONV_W = 4

SPLIT_SIZES = (GLA_HEADS * GLA_DK, GLA_HEADS * GLA_DK, GLA_WIDTH, GLA_RANK, GLA_WIDTH,
               NSA_WIDTH, 6 * NSA_KV_HEADS * HEAD_DIM, 3 * NSA_HEADS, NSA_WIDTH,
               2 * ML_WIDTH, ML_WIDTH, 2 * ML_HEADS, ML_WIDTH, ML_WIDTH)

LANES = 128
SUBLANES = 8
VMEM_LIMIT = 56 * 1024 * 1024
NEG_BIG = -1e30
EPS = 1e-6
LOG2E = math.log2(math.e)


def _round_up(n, m):
    return -(-n // m) * m


PAD_SIZES = tuple(_round_up(s, LANES) for s in SPLIT_SIZES)
D_IN_PAD = sum(PAD_SIZES)
W_GLA = sum(PAD_SIZES[0:5])
W_NSA = sum(PAD_SIZES[5:9])
W_ML = sum(PAD_SIZES[9:14])
KV_W = NSA_KV_HEADS * HEAD_DIM


def _dot(a, b):
    return jnp.dot(a.astype(BF16), b.astype(BF16), preferred_element_type=F32)


def _dot_nt(a, b):
    return lax.dot_general(a.astype(BF16), b.astype(BF16), (((1,), (1,)), ((), ())), preferred_element_type=F32)


def _dot_tn(a, b):
    return lax.dot_general(a.astype(BF16), b.astype(BF16), (((0,), (0,)), ((), ())), preferred_element_type=F32)


def _dot_f32(a, b):
    return jnp.dot(a, b, precision=HIGHEST, preferred_element_type=F32)


def _dot_nt_f32(a, b):
    return lax.dot_general(a, b, (((1,), (1,)), ((), ())), precision=HIGHEST, preferred_element_type=F32)


def _dot_split(a, b):
    a_hi = a.astype(BF16)
    a_lo = (a - a_hi.astype(F32)).astype(BF16)
    bb = b.astype(BF16)
    return jnp.dot(a_hi, bb, preferred_element_type=F32) + jnp.dot(a_lo, bb, preferred_element_type=F32)


def _log_sigmoid(x):
    return jnp.minimum(x, 0.0) - jnp.log1p(jnp.exp(-jnp.abs(x)))


def _sigmoid(x):
    return 1.0 / (1.0 + jnp.exp(-x))


def _silu(x):
    return x * _sigmoid(x)


def _group_mean_matrix(width):
    g = np.kron(np.eye(width // HEAD_DIM, dtype=np.float32), np.full((HEAD_DIM, HEAD_DIM), 1.0 / HEAD_DIM, np.float32))
    return jnp.asarray(g, dtype=BF16)


def _group_norm(x, gmat, gain):
    ms = _dot_split(x * x, gmat)
    return x * lax.rsqrt(ms + EPS) * gain


def _cparams(*sem):
    return pltpu.CompilerParams(dimension_semantics=sem, vmem_limit_bytes=VMEM_LIMIT)


def _proj_in_kernel(x_ref, g_ref, w_ref, ug_ref, un_ref, um_ref):
    x = x_ref[...]
    y = x * lax.rsqrt(jnp.mean(x * x, axis=-1, keepdims=True) + EPS) * g_ref[...]
    r = jnp.dot(y.astype(BF16), w_ref[...], preferred_element_type=F32)
    ug_ref[...] = r[:, 0:W_GLA]
    un_ref[...] = r[:, W_GLA:W_GLA + W_NSA]
    um_ref[...] = r[:, W_GLA + W_NSA:D_IN_PAD]


def _proj_in(x2d, g, w_pad, tm):
    rows = x2d.shape[0]
    return pl.pallas_call(
        _proj_in_kernel,
        grid=(rows // tm,),
        in_specs=[pl.BlockSpec((tm, D_MODEL), lambda i: (i, 0)),
                  pl.BlockSpec((1, D_MODEL), lambda i: (0, 0)),
                  pl.BlockSpec((D_MODEL, D_IN_PAD), lambda i: (0, 0))],
        out_specs=[pl.BlockSpec((tm, W_GLA), lambda i: (i, 0)),
                   pl.BlockSpec((tm, W_NSA), lambda i: (i, 0)),
                   pl.BlockSpec((tm, W_ML), lambda i: (i, 0))],
        out_shape=[jax.ShapeDtypeStruct((rows, W_GLA), F32),
                   jax.ShapeDtypeStruct((rows, W_NSA), F32),
                   jax.ShapeDtypeStruct((rows, W_ML), F32)],
        compiler_params=_cparams("parallel"),
        name="proj_in",
    )(x2d, g.reshape(1, D_MODEL), w_pad)


def _pad_w_in(w_in):
    parts = []
    off = 0
    for s, p in zip(SPLIT_SIZES, PAD_SIZES):
        seg = w_in[:, off:off + s]
        if p != s:
            seg = jnp.pad(seg, ((0, 0), (0, p - s)))
        parts.append(seg)
        off += s
    return jnp.concatenate(parts, axis=1).astype(BF16)


def _proj_out_kernel(oa_ref, ocmp_ref, oslc_ref, owin_ref, gz_ref, oc_ref, x_ref, w_ref, e_ref, y_ref):
    gate = _sigmoid(gz_ref[:, 0:LANES])
    ob = (_dot_split(gate, e_ref[0]) * ocmp_ref[...] + _dot_split(gate, e_ref[1]) * oslc_ref[...]
          + _dot_split(gate, e_ref[2]) * owin_ref[...]) * _silu(gz_ref[:, LANES:LANES + NSA_WIDTH])
    y = x_ref[...]
    y = y + _dot(oa_ref[...], w_ref[0:GLA_WIDTH, :])
    y = y + _dot(ob, w_ref[GLA_WIDTH:GLA_WIDTH + NSA_WIDTH, :])
    y = y + _dot(oc_ref[...], w_ref[GLA_WIDTH + NSA_WIDTH:D_MIX, :])
    y_ref[...] = y


def _gate_expand():
    e = np.zeros((3, LANES, NSA_WIDTH), np.float32)
    for j in range(3):
        for h in range(NSA_HEADS):
            e[j, j * NSA_HEADS + h, h * HEAD_DIM:(h + 1) * HEAD_DIM] = 1.0
    return jnp.asarray(e, dtype=BF16)


def _proj_out(oa, o_cmp, o_slc, o_win, u_nsa, oc, x2d, w_bf, tm):
    rows = x2d.shape[0]
    gz_w = LANES + NSA_WIDTH
    gz_blk = (NSA_WIDTH + 6 * KV_W) // gz_w
    assert gz_blk * gz_w == NSA_WIDTH + 6 * KV_W
    row = lambda i: (i, 0)
    return pl.pallas_call(
        _proj_out_kernel,
        grid=(rows // tm,),
        in_specs=[pl.BlockSpec((tm, GLA_WIDTH), row),
                  pl.BlockSpec((tm, NSA_WIDTH), row),
                  pl.BlockSpec((tm, NSA_WIDTH), row),
                  pl.BlockSpec((tm, NSA_WIDTH), row),
                  pl.BlockSpec((tm, gz_w), lambda i: (i, gz_blk)),
                  pl.BlockSpec((tm, ML_WIDTH), row),
                  pl.BlockSpec((tm, D_MODEL), row),
                  pl.BlockSpec((D_MIX, D_MODEL), lambda i: (0, 0)),
                  pl.BlockSpec((3, LANES, NSA_WIDTH), lambda i: (0, 0, 0))],
        out_specs=pl.BlockSpec((tm, D_MODEL), row),
        out_shape=jax.ShapeDtypeStruct((rows, D_MODEL), F32),
        compiler_params=_cparams("parallel"),
        name="proj_out",
    )(oa, o_cmp, o_slc, o_win, u_nsa, oc, x2d, w_bf, _gate_expand())


def _gla_kernel(u_ref, st0_ref, wg_ref, bg_ref, ng_ref, gm_ref, o_ref, st_ref, *, bb, tl, c):
    @pl.when(pl.program_id(1) == 0)
    def _init():
        st_ref[...] = st0_ref[...]

    nk = GLA_HEADS * GLA_DK
    a_off = 2 * nk + GLA_WIDTH
    q, k, v, log_a = [], [], [], []
    for i in range(bb):
        q.append(u_ref[i, :, 0:nk] * (GLA_DK ** -0.5))
        k.append(u_ref[i, :, nk:2 * nk])
        v.append(u_ref[i, :, 2 * nk:2 * nk + GLA_WIDTH])
        pre = _dot(u_ref[i, :, a_off:a_off + LANES], wg_ref[...]) + bg_ref[...]
        log_a.append(_log_sigmoid(pre) * (1.0 / GLA_TAU))

    tril = (lax.broadcasted_iota(jnp.int32, (c, c), 0) >= lax.broadcasted_iota(jnp.int32, (c, c), 1)).astype(F32)
    hc = GLA_HEADS * c
    tri_h = (lax.broadcasted_iota(jnp.int32, (hc, c), 0) % c) >= lax.broadcasted_iota(jnp.int32, (hc, c), 1)
    k_head = lax.broadcasted_iota(jnp.int32, (1, nk), 1) // GLA_DK
    v_head = lax.broadcasted_iota(jnp.int32, (1, GLA_WIDTH), 1) // GLA_DV
    st_diag = (lax.broadcasted_iota(jnp.int32, (GLA_WIDTH, nk), 0) // GLA_DV
               == lax.broadcasted_iota(jnp.int32, (GLA_WIDTH, nk), 1) // GLA_DK)

    st = [st_ref[i] for i in range(bb)]
    outs = [[] for _ in range(bb)]
    for j in range(tl // c):
        sl = slice(j * c, (j + 1) * c)
        for i in range(bb):
            b = _dot_f32(tril, log_a[i][sl])
            blast = b[c - 1:c]
            qe = q[i][sl] * jnp.exp(b)
            ke = k[i][sl] * jnp.exp(-b)
            kl = k[i][sl] * jnp.exp(blast - b)
            vc = v[i][sl]
            qx = jnp.concatenate([jnp.where(k_head == h, qe, 0.0) for h in range(GLA_HEADS)], axis=0)
            a = jnp.where(tri_h, _dot_nt(qx, ke), 0.0)
            r = _dot(a, vc)
            o = _dot_nt(qe, st[i])
            for h in range(GLA_HEADS):
                o = o + jnp.where(v_head == h, r[h * c:(h + 1) * c], 0.0)
            st[i] = st[i] * jnp.exp(blast) + jnp.where(st_diag, _dot_tn(vc, kl), 0.0)
            outs[i].append(o)
    for i in range(bb):
        st_ref[i] = st[i]
        o = outs[i][0] if len(outs[i]) == 1 else jnp.concatenate(outs[i], axis=0)
        z = u_ref[i, :, a_off + LANES:a_off + LANES + GLA_WIDTH]
        o_ref[i] = _group_norm(o, gm_ref[...], ng_ref[...]) * _silu(z)


GLA_SEQS_PER_STEP = 4


def _gla(u_gla, st0, wg, bg, ng, b, l):
    tl = min(l, 128)
    c = min(l, 16)
    bb = math.gcd(b, GLA_SEQS_PER_STEP)
    nk = GLA_HEADS * GLA_DK
    u3 = u_gla.reshape(b, l, W_GLA)
    kern = functools.partial(_gla_kernel, bb=bb, tl=tl, c=c)
    return pl.pallas_call(
        kern,
        grid=(b // bb, l // tl),
        in_specs=[pl.BlockSpec((bb, tl, W_GLA), lambda bi, li: (bi, li, 0)),
                  pl.BlockSpec((bb, GLA_WIDTH, nk), lambda bi, li: (bi, 0, 0)),
                  pl.BlockSpec((LANES, nk), lambda bi, li: (0, 0)),
                  pl.BlockSpec((1, nk), lambda bi, li: (0, 0)),
                  pl.BlockSpec((1, GLA_WIDTH), lambda bi, li: (0, 0)),
                  pl.BlockSpec((GLA_WIDTH, GLA_WIDTH), lambda bi, li: (0, 0))],
        out_specs=[pl.BlockSpec((bb, tl, GLA_WIDTH), lambda bi, li: (bi, li, 0)),
                   pl.BlockSpec((bb, GLA_WIDTH, nk), lambda bi, li: (bi, 0, 0))],
        out_shape=[jax.ShapeDtypeStruct((b, l, GLA_WIDTH), F32),
                   jax.ShapeDtypeStruct((b, GLA_WIDTH, nk), F32)],
        compiler_params=_cparams("parallel", "arbitrary"),
        name="gla",
    )(u3, st0, wg, bg, ng, _group_mean_matrix(GLA_WIDTH))


def _gla_state_in(s):
    b = s.shape[0]
    st = jnp.swapaxes(s.astype(F32), 2, 3)
    eye = jnp.eye(GLA_HEADS, dtype=F32)
    full = st[:, :, :, None, :] * eye[None, :, None, :, None]
    return full.reshape(b, GLA_WIDTH, GLA_HEADS * GLA_DK)


def _gla_state_out(st):
    b = st.shape[0]
    full = st.reshape(b, GLA_HEADS, GLA_DV, GLA_HEADS, GLA_DK)
    diag = jnp.stack([full[:, h, :, h, :] for h in range(GLA_HEADS)], axis=1)
    return jnp.swapaxes(diag, 2, 3)


def _mlstm_kernel(u_ref, c0_ref, n0_ref, m0_ref, cv0_ref, cw_ref, cb_ref, gb_ref, ng_ref, gm_ref,
                  o_ref, c_ref, n_ref, m_ref, cv_ref, xp_sc, *, bb, tl):
    @pl.when(pl.program_id(1) == 0)
    def _init():
        c_ref[...] = c0_ref[...]
        n_ref[...] = n0_ref[...]
        m_ref[...] = m0_ref[...]
        xp_sc[:, 0:SUBLANES, :] = cv0_ref[...]

    pairs = ML_HEADS // 2
    state = [([c_ref[i, j] for j in range(pairs)],
              [n_ref[i, j:j + 1, :] for j in range(pairs)],
              [m_ref[i, h:h + 1, :] for h in range(ML_HEADS)]) for i in range(bb)]
    new_state = [_mlstm_seq(i, state[i], u_ref, cw_ref, cb_ref, gb_ref, ng_ref, gm_ref, o_ref, cv_ref, xp_sc, tl)
                 for i in range(bb)]
    for i in range(bb):
        cps, nps, mbs = new_state[i]
        for j in range(pairs):
            c_ref[i, j] = cps[j]
            n_ref[i, j:j + 1, :] = nps[j]
        for h in range(ML_HEADS):
            m_ref[i, h:h + 1, :] = mbs[h]


def _mlstm_seq(i, state, u_ref, cw_ref, cb_ref, gb_ref, ng_ref, gm_ref, o_ref, cv_ref, xp_sc, tl):
    cps, nps, mbs = state
    c = tl
    w2 = 2 * ML_WIDTH
    u = u_ref[i]
    xp_sc[i, SUBLANES:SUBLANES + tl, :] = u[:, 0:w2]
    conv = cb_ref[...]
    for w in range(CONV_W):
        off = SUBLANES - (CONV_W - 1) + w
        conv = conv + xp_sc[i, off:off + tl, :] * cw_ref[w:w + 1, :]
    tail = xp_sc[i, tl:tl + SUBLANES, :]
    xp_sc[i, 0:SUBLANES, :] = tail
    cv_ref[i] = tail

    qk = _silu(conv)
    mq = qk[:, 0:ML_WIDTH]
    mk = qk[:, ML_WIDTH:w2] * (ML_DH ** -0.5)
    mv = u[:, w2:w2 + ML_WIDTH]
    ifg = u[:, w2 + ML_WIDTH:w2 + ML_WIDTH + LANES] + gb_ref[...]
    logf = _log_sigmoid(ifg)
    og_off = w2 + ML_WIDTH + LANES
    og = _sigmoid(u[:, og_off:og_off + ML_WIDTH])
    zz = _silu(u[:, og_off + ML_WIDTH:og_off + 2 * ML_WIDTH])

    tri = lax.broadcasted_iota(jnp.int32, (c, c), 0) >= lax.broadcasted_iota(jnp.int32, (c, c), 1)
    lane = lax.broadcasted_iota(jnp.int32, (c, LANES), 1)
    low = lane < ML_DH
    low_row = lax.broadcasted_iota(jnp.int32, (1, LANES), 1) < ML_DH
    sq_row = lax.broadcasted_iota(jnp.int32, (LANES, LANES), 0)
    sq_col = lax.broadcasted_iota(jnp.int32, (LANES, LANES), 1)
    same_head = (sq_row < ML_DH) == (sq_col < ML_DH)

    def wide(x):
        return x[:, :c] if c <= LANES else jnp.concatenate([x] * (c // LANES), axis=1)

    fcum_all = _dot_f32(tri.astype(F32), logf)
    gates_t = jnp.where(lane < ML_HEADS, ifg, fcum_all).T
    new_cps, new_nps, new_mbs = [], [], []
    for j in range(ML_HEADS // 2):
        ps = slice(j * LANES, (j + 1) * LANES)
        q_s, k_s, v_s = mq[:, ps], mk[:, ps], mv[:, ps]
        acc = jnp.zeros((c, 2 * LANES), F32)
        per_head = []
        for hl in range(2):
            h = 2 * j + hl
            own = low if hl == 0 else jnp.logical_not(low)
            fc = jnp.broadcast_to(fcum_all[:, ML_HEADS + h:ML_HEADS + h + 1], (c, LANES))
            ii = jnp.broadcast_to(ifg[:, h:h + 1], (c, LANES))
            dm = jnp.where(tri, wide(fc) - gates_t[ML_HEADS + h:ML_HEADS + h + 1, :] + gates_t[h:h + 1, :], NEG_BIG)
            inter = fc + mbs[h]
            m = jnp.maximum(inter, jnp.max(dm, axis=-1, keepdims=True))
            sij = _dot_nt(jnp.where(own, q_s, 0.0), k_s) * jnp.exp(dm - wide(m))
            acc = acc + _dot(sij, jnp.concatenate([jnp.where(own, v_s, 0.0), jnp.where(own, 1.0, 0.0)], axis=1))
            m_last = m[c - 1:c]
            f_last = fc[c - 1:c]
            per_head.append((m, jnp.exp(inter - m), jnp.exp(f_last - fc + ii - m_last),
                             jnp.exp(f_last + mbs[h] - m_last)))
            new_mbs.append(m_last)
        m_p, w_p, wj_p = (jnp.where(low, per_head[0][t], per_head[1][t]) for t in range(3))
        dec_row = jnp.where(low_row, per_head[0][3], per_head[1][3])
        n_mat = jnp.where(same_head, jnp.broadcast_to(nps[j], (LANES, LANES)), 0.0)
        num = w_p * _dot_nt(q_s, cps[j]) + acc[:, 0:LANES]
        den = w_p * _dot_nt(q_s, n_mat) + acc[:, LANES:2 * LANES]
        hh = num / jnp.maximum(jnp.abs(den), jnp.exp(-m_p))
        hn = hh * lax.rsqrt(_dot_split(hh * hh, gm_ref[...]) + EPS) * ng_ref[...]
        o_ref[i, :, ps] = hn * og[:, ps] * zz[:, ps]
        dec_mat = jnp.where(sq_row < ML_DH, jnp.broadcast_to(per_head[0][3], (LANES, LANES)),
                            jnp.broadcast_to(per_head[1][3], (LANES, LANES)))
        new_cps.append(dec_mat * cps[j] + jnp.where(same_head, _dot_tn(wj_p * v_s, k_s), 0.0))
        new_nps.append(dec_row * nps[j] + jnp.sum(wj_p * k_s, axis=0, keepdims=True))
    return new_cps, new_nps, new_mbs


ML_SEQS_PER_STEP = 2


def _mlstm(u_ml, c0, n0, m0, conv0, cw, cb, gb, ng, b, l):
    tl = min(l, 256)
    bb = math.gcd(b, ML_SEQS_PER_STEP)
    pairs = ML_HEADS // 2
    w2 = 2 * ML_WIDTH
    u3 = u_ml.reshape(b, l, W_ML)
    eye2 = jnp.eye(2, dtype=F32)
    c0p = jnp.einsum('bphed,hk->bphekd', c0.astype(F32).reshape(b, pairs, 2, ML_DH, ML_DH), eye2)
    c0p = c0p.reshape(b, pairs, LANES, LANES)
    n0p = n0.astype(F32).reshape(b, pairs, LANES)
    m0b = jnp.broadcast_to(m0.astype(F32)[:, :, None], (b, ML_HEADS, LANES))
    cv0 = jnp.pad(conv0.astype(F32), ((0, 0), (SUBLANES - (CONV_W - 1), 0), (0, 0)))
    gbp = jnp.pad(gb.astype(F32).reshape(1, 2 * ML_HEADS), ((0, 0), (0, LANES - 2 * ML_HEADS)))
    kern = functools.partial(_mlstm_kernel, bb=bb, tl=tl)
    st = lambda bi, li: (bi, 0, 0)
    st4 = lambda bi, li: (bi, 0, 0, 0)
    cst = lambda bi, li: (0, 0)
    o, c_new, n_new, m_new, cv = pl.pallas_call(
        kern,
        grid=(b // bb, l // tl),
        in_specs=[pl.BlockSpec((bb, tl, W_ML), lambda bi, li: (bi, li, 0)),
                  pl.BlockSpec((bb, pairs, LANES, LANES), st4),
                  pl.BlockSpec((bb, pairs, LANES), st),
                  pl.BlockSpec((bb, ML_HEADS, LANES), st),
                  pl.BlockSpec((bb, SUBLANES, w2), st),
                  pl.BlockSpec((CONV_W, w2), cst),
                  pl.BlockSpec((1, w2), cst),
                  pl.BlockSpec((1, LANES), cst),
                  pl.BlockSpec((1, LANES), cst),
                  pl.BlockSpec((LANES, LANES), cst)],
        out_specs=[pl.BlockSpec((bb, tl, ML_WIDTH), lambda bi, li: (bi, li, 0)),
                   pl.BlockSpec((bb, pairs, LANES, LANES), st4),
                   pl.BlockSpec((bb, pairs, LANES), st),
                   pl.BlockSpec((bb, ML_HEADS, LANES), st),
                   pl.BlockSpec((bb, SUBLANES, w2), st)],
        out_shape=[jax.ShapeDtypeStruct((b, l, ML_WIDTH), F32),
                   jax.ShapeDtypeStruct((b, pairs, LANES, LANES), F32),
                   jax.ShapeDtypeStruct((b, pairs, LANES), F32),
                   jax.ShapeDtypeStruct((b, ML_HEADS, LANES), F32),
                   jax.ShapeDtypeStruct((b, SUBLANES, w2), F32)],
        scratch_shapes=[pltpu.VMEM((bb, tl + 2 * SUBLANES, w2), F32)],
        compiler_params=_cparams("parallel", "arbitrary"),
        name="mlstm",
    )(u3, c0p, n0p, m0b, cv0, cw.astype(F32), cb.astype(F32).reshape(1, w2), gbp,
      jnp.tile(ng.astype(F32).reshape(1, ML_DH), (1, 2)), _group_mean_matrix(LANES))
    c6 = c_new.reshape(b, pairs, 2, ML_DH, 2, ML_DH)
    c_out = jnp.stack([c6[:, :, h, :, h, :] for h in range(2)], axis=2).reshape(b, ML_HEADS, ML_DH, ML_DH)
    return (o, c_out, n_new.reshape(b, ML_HEADS, ML_DH), m_new[:, :, 0], cv[:, SUBLANES - (CONV_W - 1):, :])


def _rope_lanes(x, cos_t, sin_t):
    w = x.shape[1]
    half = ROT_DIM // 2
    reps = w // cos_t.shape[1]
    if reps > 1:
        cos_t = jnp.concatenate([cos_t] * reps, axis=1)
        sin_t = jnp.concatenate([sin_t] * reps, axis=1)
    lane = lax.broadcasted_iota(jnp.int32, x.shape, 1) % HEAD_DIM
    partner = jnp.where(lane < half, pltpu.roll(x, w - half, 1), pltpu.roll(x, half, 1))
    return x * cos_t + partner * sin_t


def _nsa_prep_kernel(u_ref, cos_ref, sin_ref, qg_ref, kg_ref, g4_ref, g1_ref, qn_ref, qr_ref, rows_ref, win_ref):
    u = u_ref[...]
    cos_t = cos_ref[...]
    sin_t = sin_ref[...]
    q = _group_norm(u[:, 0:NSA_WIDTH], g4_ref[...], qg_ref[...])
    qn_ref[...] = (q * ATTN_SCALE).astype(BF16)
    qr = _rope_lanes(q, cos_t, sin_t) * (ATTN_SCALE * LOG2E)
    low = lax.broadcasted_iota(jnp.int32, (qr.shape[0], LANES), 1) < HEAD_DIM
    for j in range(NSA_HEADS // 2):
        pair = qr[:, j * LANES:(j + 1) * LANES]
        swapped = pltpu.roll(pair, HEAD_DIM, 1)
        qr_ref[2 * j] = jnp.where(low, pair, swapped).astype(BF16)
        qr_ref[2 * j + 1] = jnp.where(low, swapped, pair).astype(BF16)
    kv = NSA_WIDTH
    k_slc = _rope_lanes(_group_norm(u[:, kv + 2 * KV_W:kv + 3 * KV_W], g1_ref[...], kg_ref[1:2, :]), cos_t, sin_t)
    k_win = _rope_lanes(_group_norm(u[:, kv + 4 * KV_W:kv + 5 * KV_W], g1_ref[...], kg_ref[2:3, :]), cos_t, sin_t)
    rows_ref[:, 0:2 * KV_W] = u[:, kv:kv + 2 * KV_W]
    rows_ref[:, 2 * KV_W:3 * KV_W] = k_slc
    rows_ref[:, 3 * KV_W:4 * KV_W] = u[:, kv + 3 * KV_W:kv + 4 * KV_W]
    win_ref[:, 0:KV_W] = k_win
    win_ref[:, KV_W:2 * KV_W] = u[:, kv + 5 * KV_W:kv + 6 * KV_W]


def _rope_tables(pos):
    half = ROT_DIM // 2
    inv = jnp.exp(-math.log(ROPE_THETA) * jnp.arange(half, dtype=F32) * 2.0 / ROT_DIM)
    ang = pos.astype(F32)[:, None] * inv[None, :]
    cos, sin = jnp.cos(ang), jnp.sin(ang)
    n = pos.shape[0]
    ones = jnp.ones((n, HEAD_DIM - ROT_DIM), F32)
    cos_h = jnp.concatenate([cos, cos, ones], axis=1)
    sin_h = jnp.concatenate([-sin, sin, 0.0 * ones], axis=1)
    reps = LANES // HEAD_DIM
    return jnp.tile(cos_h, (1, reps)), jnp.tile(sin_h, (1, reps))


def _nsa_prep(u_nsa, pos, q_g, k_g, l):
    rows = u_nsa.shape[0]
    tl = min(l, 256)
    nb = l // tl
    cos_t, sin_t = _rope_tables(pos)
    qg = jnp.tile(q_g.astype(F32).reshape(1, HEAD_DIM), (1, NSA_HEADS))
    kg = jnp.tile(k_g.astype(F32), (1, NSA_KV_HEADS))
    cst = lambda i: (0, 0)
    return pl.pallas_call(
        _nsa_prep_kernel,
        grid=(rows // tl,),
        in_specs=[pl.BlockSpec((tl, W_NSA), lambda i: (i, 0)),
                  pl.BlockSpec((tl, LANES), lambda i: (i % nb, 0)),
                  pl.BlockSpec((tl, LANES), lambda i: (i % nb, 0)),
                  pl.BlockSpec((1, NSA_WIDTH), cst),
                  pl.BlockSpec((3, KV_W), cst),
                  pl.BlockSpec((NSA_WIDTH, NSA_WIDTH), cst),
                  pl.BlockSpec((KV_W, KV_W), cst)],
        out_specs=[pl.BlockSpec((tl, NSA_WIDTH), lambda i: (i, 0)),
                   pl.BlockSpec((NSA_HEADS, tl, LANES), lambda i: (0, i, 0)),
                   pl.BlockSpec((tl, N_KV_SLOTS * KV_W), lambda i: (i, 0)),
                   pl.BlockSpec((tl, 2 * KV_W), lambda i: (i, 0))],
        out_shape=[jax.ShapeDtypeStruct((rows, NSA_WIDTH), BF16),
                   jax.ShapeDtypeStruct((NSA_HEADS, rows, LANES), BF16),
                   jax.ShapeDtypeStruct((rows, N_KV_SLOTS * KV_W), F32),
                   jax.ShapeDtypeStruct((rows, 2 * KV_W), F32)],
        compiler_params=_cparams("parallel"),
        name="nsa_prep",
    )(u_nsa, cos_t, sin_t, qg, kg, _group_mean_matrix(NSA_WIDTH), _group_mean_matrix(KV_W))


def _cmp_halves(xk_ref, xv_ref, w_ref, n_half):
    acc_k = jnp.zeros((n_half, 2 * KV_W), F32)
    acc_v = jnp.zeros((n_half, 2 * KV_W), F32)
    for s in range(CMP_STRIDE):
        acc_k = acc_k + _dot(xk_ref[pl.ds(s, n_half, stride=CMP_STRIDE), :], w_ref[0, s])
        acc_v = acc_v + _dot(xv_ref[pl.ds(s, n_half, stride=CMP_STRIDE), :], w_ref[1, s])
    return acc_k, acc_v


def _cmp_finish(acc_k, acc_v, bias_ref, kg_ref, g1_ref, kc_ref, vc_ref, n_half):
    valid = lax.broadcasted_iota(jnp.int32, (n_half, KV_W), 0) < n_half - 1

    def summary(acc, bias):
        return acc[:, 0:KV_W] + pltpu.roll(acc[:, KV_W:2 * KV_W], n_half - 1, 0) + bias

    kc = _group_norm(summary(acc_k, bias_ref[0:1, :]), g1_ref[...], kg_ref[...])
    kc_ref[0] = jnp.where(valid, kc, 0.0).astype(BF16)
    vc_ref[0] = jnp.where(valid, summary(acc_v, bias_ref[1:2, :]), 0.0).astype(BF16)


def _cmp_kv_kernel(xk_ref, xv_ref, w_ref, bias_ref, kg_ref, g1_ref, kc_ref, vc_ref, *, n_half):
    acc_k, acc_v = _cmp_halves(xk_ref.at[0], xv_ref.at[0], w_ref, n_half)
    _cmp_finish(acc_k, acc_v, bias_ref, kg_ref, g1_ref, kc_ref, vc_ref, n_half)


PAGES_PER_STEP = 16
CMP_PAGES_PER_STEP = 64


def _cmp_paged_kernel(pt_ref, *refs, npg, page, whole):
    page_refs = refs[:npg]
    if whole:
        w_ref, bias_ref, kg_ref, g1_ref, kc_ref, vc_ref, xk_sc, xv_sc = refs[npg:]
    else:
        w_ref, acck_ref, accv_ref, xk_sc, xv_sc = refs[npg:]
    for i in range(npg):
        xk_sc[i * page:(i + 1) * page, :] = page_refs[i][0, 0:KV_W, :].T
        xv_sc[i * page:(i + 1) * page, :] = page_refs[i][0, KV_W:2 * KV_W, :].T
    n_half = npg * page // CMP_STRIDE
    acc_k, acc_v = _cmp_halves(xk_sc, xv_sc, w_ref, n_half)
    if whole:
        _cmp_finish(acc_k, acc_v, bias_ref, kg_ref, g1_ref, kc_ref, vc_ref, n_half)
    else:
        acck_ref[0] = acc_k
        accv_ref[0] = acc_v


def _cmp_fin_kernel(acck_ref, accv_ref, bias_ref, kg_ref, g1_ref, kc_ref, vc_ref, *, n_half):
    _cmp_finish(acck_ref[0], accv_ref[0], bias_ref, kg_ref, g1_ref, kc_ref, vc_ref, n_half)


def _page_specs(npg, page, row_blk, pool_off):
    def spec(i):
        return pl.BlockSpec((1, 2 * KV_W, page), lambda bi, ji, pt: (pt[bi, ji * npg + i] + pool_off, row_blk, 0))
    return [spec(i) for i in range(npg)]


def _cmp_kv_paged(cache_t, page_table, pool_off, wcat, bias, kg0):
    b, n_pages = page_table.shape
    page = cache_t.shape[2]
    npg = math.gcd(n_pages, CMP_PAGES_PER_STEP)
    nh_step = npg * page // CMP_STRIDE
    n_half = n_pages * page // CMP_STRIDE
    kg = jnp.tile(kg0.astype(F32).reshape(1, HEAD_DIM), (1, NSA_KV_HEADS))
    if npg == n_pages:
        cst = lambda bi, ji, pt: (0, 0)
        return pl.pallas_call(
            functools.partial(_cmp_paged_kernel, npg=npg, page=page, whole=True),
            grid_spec=pltpu.PrefetchScalarGridSpec(
                num_scalar_prefetch=1, grid=(b, 1),
                in_specs=_page_specs(npg, page, 0, pool_off)
                + [pl.BlockSpec((2, CMP_STRIDE, KV_W, 2 * KV_W), lambda bi, ji, pt: (0, 0, 0, 0)),
                   pl.BlockSpec((2, KV_W), cst), pl.BlockSpec((1, KV_W), cst), pl.BlockSpec((KV_W, KV_W), cst)],
                out_specs=[pl.BlockSpec((1, n_half, KV_W), lambda bi, ji, pt: (bi, 0, 0)),
                           pl.BlockSpec((1, n_half, KV_W), lambda bi, ji, pt: (bi, 0, 0))],
                scratch_shapes=[pltpu.VMEM((npg * page, KV_W), F32), pltpu.VMEM((npg * page, KV_W), F32)]),
            out_shape=[jax.ShapeDtypeStruct((b, n_half, KV_W), BF16),
                       jax.ShapeDtypeStruct((b, n_half, KV_W), BF16)],
            compiler_params=_cparams("parallel", "arbitrary"),
            name="cmp_paged",
        )(page_table, *([cache_t] * npg), wcat, bias, kg, _group_mean_matrix(KV_W))
    kern = functools.partial(_cmp_paged_kernel, npg=npg, page=page, whole=False)
    acc_k, acc_v = pl.pallas_call(
        kern,
        grid_spec=pltpu.PrefetchScalarGridSpec(
            num_scalar_prefetch=1, grid=(b, n_pages // npg),
            in_specs=_page_specs(npg, page, 0, pool_off)
            + [pl.BlockSpec((2, CMP_STRIDE, KV_W, 2 * KV_W), lambda bi, ji, pt: (0, 0, 0, 0))],
            out_specs=[pl.BlockSpec((1, nh_step, 2 * KV_W), lambda bi, ji, pt: (bi, ji, 0)),
                       pl.BlockSpec((1, nh_step, 2 * KV_W), lambda bi, ji, pt: (bi, ji, 0))],
            scratch_shapes=[pltpu.VMEM((npg * page, KV_W), F32), pltpu.VMEM((npg * page, KV_W), F32)]),
        out_shape=[jax.ShapeDtypeStruct((b, n_half, 2 * KV_W), F32),
                   jax.ShapeDtypeStruct((b, n_half, 2 * KV_W), F32)],
        compiler_params=_cparams("parallel", "arbitrary"),
        name="cmp_paged",
    )(page_table, *([cache_t] * npg), wcat)
    blk = lambda bi: (bi, 0, 0)
    return pl.pallas_call(
        functools.partial(_cmp_fin_kernel, n_half=n_half),
        grid=(b,),
        in_specs=[pl.BlockSpec((1, n_half, 2 * KV_W), blk),
                  pl.BlockSpec((1, n_half, 2 * KV_W), blk),
                  pl.BlockSpec((2, KV_W), lambda bi: (0, 0)),
                  pl.BlockSpec((1, KV_W), lambda bi: (0, 0)),
                  pl.BlockSpec((KV_W, KV_W), lambda bi: (0, 0))],
        out_specs=[pl.BlockSpec((1, n_half, KV_W), blk), pl.BlockSpec((1, n_half, KV_W), blk)],
        out_shape=[jax.ShapeDtypeStruct((b, n_half, KV_W), BF16),
                   jax.ShapeDtypeStruct((b, n_half, KV_W), BF16)],
        compiler_params=_cparams("parallel"),
        name="cmp_fin",
    )(acc_k, acc_v, bias, kg, _group_mean_matrix(KV_W))


def _cmp_weights(cmp_pos, cmp_w):
    wf = cmp_w.astype(F32)
    eye_g = jnp.eye(NSA_KV_HEADS, dtype=F32)

    def bd(w):
        return jnp.einsum('ksde,gh->ksgdhe', w, eye_g).reshape(2, CMP_STRIDE, KV_W, KV_W)

    wcat = jnp.concatenate([bd(wf[:, :CMP_STRIDE]), bd(wf[:, CMP_STRIDE:])], axis=3).astype(BF16)
    bias = jnp.einsum('ksd,ksde->ke', cmp_pos.astype(F32), wf)
    return wcat, jnp.tile(bias, (1, NSA_KV_HEADS))


def _cmp_kv(x3, k_blk, v_blk, t_use, wcat, bias, kg0):
    b = x3.shape[0]
    n_half = t_use // CMP_STRIDE
    kern = functools.partial(_cmp_kv_kernel, n_half=n_half)
    kg = jnp.tile(kg0.astype(F32).reshape(1, HEAD_DIM), (1, NSA_KV_HEADS))
    return pl.pallas_call(
        kern,
        grid=(b,),
        in_specs=[pl.BlockSpec((1, t_use, KV_W), lambda bi: (bi, 0, k_blk)),
                  pl.BlockSpec((1, t_use, KV_W), lambda bi: (bi, 0, v_blk)),
                  pl.BlockSpec((2, CMP_STRIDE, KV_W, 2 * KV_W), lambda bi: (0, 0, 0, 0)),
                  pl.BlockSpec((2, KV_W), lambda bi: (0, 0)),
                  pl.BlockSpec((1, KV_W), lambda bi: (0, 0)),
                  pl.BlockSpec((KV_W, KV_W), lambda bi: (0, 0))],
        out_specs=[pl.BlockSpec((1, n_half, KV_W), lambda bi: (bi, 0, 0)),
                   pl.BlockSpec((1, n_half, KV_W), lambda bi: (bi, 0, 0))],
        out_shape=[jax.ShapeDtypeStruct((b, n_half, KV_W), BF16),
                   jax.ShapeDtypeStruct((b, n_half, KV_W), BF16)],
        compiler_params=_cparams("parallel"),
        name="cmp_kv",
    )(x3, x3, wcat, bias, kg, _group_mean_matrix(KV_W))


def _cmp_attn_kernel(qn_ref, kc_ref, vc_ref, ov_ref, o_ref, sel_ref, *, bb, tq, n_half, n_cmp, n_sel, nbp, pos0,
                     bias_out):
    biases = []
    qi = pl.program_id(1)
    nq = bb * tq
    pos_c = pos0 + qi * tq + lax.broadcasted_iota(jnp.int32, (tq, 1), 0)
    ncol = lax.broadcasted_iota(jnp.int32, (1, n_half), 1)
    cmask = (ncol * CMP_STRIDE + (CMP_BLOCK - 1) <= pos_c) & (ncol < n_cmp)
    pos_r = pos0 + qi * tq + lax.broadcasted_iota(jnp.int32, (1, nq), 1) % tq
    blk = lax.broadcasted_iota(jnp.int32, (nbp, 1), 0)
    cur = pos_r // SEL_BLOCK
    forced = (blk == 0) | (blk == cur) | (blk == cur - 1)
    valid = blk * SEL_BLOCK <= pos_r
    real = blk < n_sel
    for g in range(NSA_KV_HEADS):
        gs = slice(g * HEAD_DIM, (g + 1) * HEAD_DIM)
        psums = []
        for i in range(bb):
            kc = kc_ref[i, :, gs]
            vc = vc_ref[i, :, gs]
            psum = jnp.zeros((tq, n_half), F32)
            for h in range(NSA_HPG):
                hs = slice((g * NSA_HPG + h) * HEAD_DIM, (g * NSA_HPG + h + 1) * HEAD_DIM)
                s = _dot_nt(qn_ref[i, :, hs], kc)
                m = jnp.max(jnp.where(cmask, s, NEG_BIG), axis=-1, keepdims=True)
                m = jnp.where(m > 0.5 * NEG_BIG, m, 0.0)
                e = jnp.where(cmask, jnp.exp(s - m), 0.0)
                p = e / jnp.maximum(jnp.sum(e, axis=-1, keepdims=True), 1e-30)
                o_ref[i, :, hs] = _dot(p, vc)
                psum = psum + p
            psums.append(psum)
        psum = psums[0] if bb == 1 else jnp.concatenate(psums, axis=0)
        p_hi = psum.astype(BF16)
        p_lo = (psum - p_hi.astype(F32)).astype(BF16)
        ov = ov_ref[...]
        imp = _dot_nt(ov, p_hi) + _dot_nt(ov, p_lo)
        score = jnp.where(forced, 3e38, jnp.where(valid, imp, -1e38))
        score = jnp.where(real, score, -3e38)
        cnt = jnp.zeros((nbp, nq), F32)
        for jp in range(n_sel):
            rowv = score[jp:jp + 1, :]
            beats = (rowv > score) | ((rowv == score) & (blk > jp))
            cnt = cnt + jnp.where(beats, 1.0, 0.0)
        sel = (cnt < float(min(N_SELECT, n_sel))) & real
        if bias_out:
            sbg = jnp.where(sel, 0.0, NEG_BIG)
            if nbp < HEAD_DIM:
                sbg = jnp.concatenate([sbg, jnp.full((HEAD_DIM - nbp, nq), NEG_BIG, F32)], axis=0)
            biases.append(sbg)
        else:
            sel_ref[0, g * nbp:(g + 1) * nbp, :] = jnp.where(sel, 1.0, 0.0).astype(BF16)
    if bias_out:
        sel_ref[0] = jnp.concatenate(biases[::-1], axis=0).T.astype(BF16)


def _overlap_t(n_cmp, n_sel, n_half, nbp):
    a = SEL_BLOCK // CMP_STRIDE
    bb = CMP_BLOCK // CMP_STRIDE
    i = np.arange(n_half)[None, :]
    j = np.arange(nbp)[:, None]
    s = i - a * j + (bb - 1)
    cnt = np.maximum(np.minimum(np.minimum(s + 1, a + bb - 1 - s), min(a, bb)), 0)
    cnt = np.where((i < n_cmp) & (j < n_sel), cnt, 0)
    return jnp.asarray(cnt, dtype=BF16)


def _cmp_attn(qn3, kc, vc, pos0, t_len, bias_out=False):
    b, l, _ = qn3.shape
    n_half = kc.shape[1]
    n_cmp = n_half - 1
    n_sel = -(-t_len // SEL_BLOCK)
    nbp = _round_up(n_sel, 16)
    tq = min(l, LANES)
    bb = math.gcd(b, LANES // tq)
    kern = functools.partial(_cmp_attn_kernel, bb=bb, tq=tq, n_half=n_half, n_cmp=n_cmp, n_sel=n_sel, nbp=nbp,
                             pos0=pos0, bias_out=bias_out)
    if bias_out:
        assert bb == 1 and NSA_KV_HEADS == 2 and nbp <= HEAD_DIM
        sel_spec = pl.BlockSpec((1, tq, LANES), lambda bi, qi: (bi, qi, 0))
        sel_shape = jax.ShapeDtypeStruct((b, l, LANES), BF16)
    else:
        sel_spec = pl.BlockSpec((1, NSA_KV_HEADS * nbp, bb * tq), lambda bi, qi: (bi, 0, qi))
        sel_shape = jax.ShapeDtypeStruct((b // bb, NSA_KV_HEADS * nbp, bb * l), BF16)
    o_cmp, sel_t = pl.pallas_call(
        kern,
        grid=(b // bb, l // tq),
        in_specs=[pl.BlockSpec((bb, tq, NSA_WIDTH), lambda bi, qi: (bi, qi, 0)),
                  pl.BlockSpec((bb, n_half, KV_W), lambda bi, qi: (bi, 0, 0)),
                  pl.BlockSpec((bb, n_half, KV_W), lambda bi, qi: (bi, 0, 0)),
                  pl.BlockSpec((nbp, n_half), lambda bi, qi: (0, 0))],
        out_specs=[pl.BlockSpec((bb, tq, NSA_WIDTH), lambda bi, qi: (bi, qi, 0)), sel_spec],
        out_shape=[jax.ShapeDtypeStruct((b, l, NSA_WIDTH), F32), sel_shape],
        compiler_params=_cparams("parallel", "parallel"),
        name="cmp_attn",
    )(qn3, kc, vc, _overlap_t(n_cmp, n_sel, n_half, nbp))
    if bb > 1:
        sel_t = sel_t.reshape(b // bb, NSA_KV_HEADS * nbp, bb, l)
        sel_t = jnp.swapaxes(sel_t, 1, 2).reshape(b, NSA_KV_HEADS * nbp, l)
    return o_cmp, sel_t


SEL_TQ = 128
SEL_TK = 512
WIN_TQ = 256


def _group_lanes(shape, g):
    lane = lax.broadcasted_iota(jnp.int32, shape, len(shape) - 1)
    return (lane < HEAD_DIM) if g == 0 else (lane >= HEAD_DIM)


def _store_heads(o_ref, r, g, tq):
    lo = g * HEAD_DIM
    for h in range(NSA_HPG):
        hh = g * NSA_HPG + h
        o_ref[0, :, hh * HEAD_DIM:(hh + 1) * HEAD_DIM] = r[h * tq:(h + 1) * tq, lo:lo + HEAD_DIM]


def _flash_sel_kernel(qi_ref, ki_ref, q_ref, k_ref, v_ref, sb_ref, o_ref, m_sc, acc_sc, *, tq, tk):
    qi = qi_ref[pl.program_id(1)]
    ki = ki_ref[pl.program_id(1)]
    kmax = (qi * tq + tq - 1) // tk
    hq = NSA_HPG * tq

    @pl.when(ki == 0)
    def _init():
        m_sc[...] = jnp.full(m_sc.shape, NEG_BIG, F32)
        acc_sc[...] = jnp.zeros(acc_sc.shape, F32)

    def step(diagonal):
        key_blk = ki * (tk // SEL_BLOCK) + lax.broadcasted_iota(jnp.int32, (tk, LANES), 0) // SEL_BLOCK
        one_hot = jnp.where(lax.broadcasted_iota(jnp.int32, (tk, LANES), 1) % HEAD_DIM == key_blk, 1.0, 0.0).astype(BF16)
        kblk = k_ref[0].astype(BF16)
        vblk = v_ref[0].astype(BF16)
        sb4 = jnp.concatenate([sb_ref[0]] * NSA_HPG, axis=0)
        if diagonal:
            qpos = qi * tq + lax.broadcasted_iota(jnp.int32, (tq, tk), 0)
            kpos = ki * tk + lax.broadcasted_iota(jnp.int32, (tq, tk), 1)
            causal4 = jnp.concatenate([jnp.where(kpos <= qpos, 0.0, NEG_BIG)] * NSA_HPG, axis=0)
        m_prev = [m_sc[g] for g in range(NSA_KV_HEADS)]
        acc_prev = [acc_sc[g] for g in range(NSA_KV_HEADS)]
        m_out, acc_out = [], []
        for g in range(NSA_KV_HEADS):
            own_k = _group_lanes((tk, LANES), g)
            q4 = q_ref[g * NSA_HPG:(g + 1) * NSA_HPG].reshape(hq, LANES)
            qx = jnp.where(_group_lanes((hq, LANES), g), q4, sb4)
            kx = jnp.where(own_k, kblk, one_hot)
            s = lax.dot_general(qx, kx, (((1,), (1,)), ((), ())), preferred_element_type=F32)
            if diagonal:
                s = s + causal4
            m_new = jnp.maximum(m_prev[g], jnp.max(s, axis=-1, keepdims=True))
            p = jnp.exp2(s - m_new[:, :1]).astype(BF16)
            vx = jnp.where(own_k, vblk, 1.0)
            acc_out.append(jnp.exp2(m_prev[g] - m_new) * acc_prev[g] + jnp.dot(p, vx, preferred_element_type=F32))
            m_out.append(m_new)
        for g in range(NSA_KV_HEADS):
            m_sc[g] = m_out[g]
            acc_sc[g] = acc_out[g]

    @pl.when(ki < kmax)
    def _full():
        step(False)

    @pl.when(ki == kmax)
    def _last():
        step(True)
        for g in range(NSA_KV_HEADS):
            acc = acc_sc[g]
            _store_heads(o_ref, acc / pltpu.roll(acc, HEAD_DIM, 1), g, tq)


def _flash_sel(q8, rows3, selb):
    b, l, _ = rows3.shape
    tq, tk = min(SEL_TQ, l), min(SEL_TK, l)
    nq = l // tq
    pairs = [(qi, ki) for qi in range(nq) for ki in range((qi * tq + tq - 1) // tk + 1)]
    qi_tab = jnp.asarray([p[0] for p in pairs], jnp.int32)
    ki_tab = jnp.asarray([p[1] for p in pairs], jnp.int32)
    kv_idx = lambda blk: (lambda bi, si, qt, kt: (bi, kt[si], blk))
    return pl.pallas_call(
        functools.partial(_flash_sel_kernel, tq=tq, tk=tk),
        grid_spec=pltpu.PrefetchScalarGridSpec(
            num_scalar_prefetch=2, grid=(b, len(pairs)),
            in_specs=[pl.BlockSpec((NSA_HEADS, tq, LANES), lambda bi, si, qt, kt: (0, bi * nq + qt[si], 0)),
                      pl.BlockSpec((1, tk, KV_W), kv_idx(2)),
                      pl.BlockSpec((1, tk, KV_W), kv_idx(3)),
                      pl.BlockSpec((1, tq, LANES), lambda bi, si, qt, kt: (bi, qt[si], 0))],
            out_specs=pl.BlockSpec((1, tq, NSA_WIDTH), lambda bi, si, qt, kt: (bi, qt[si], 0)),
            scratch_shapes=[pltpu.VMEM((NSA_KV_HEADS, NSA_HPG * tq, LANES), F32),
                            pltpu.VMEM((NSA_KV_HEADS, NSA_HPG * tq, LANES), F32)]),
        out_shape=jax.ShapeDtypeStruct((b, l, NSA_WIDTH), F32),
        compiler_params=_cparams("parallel", "arbitrary"),
        name="flash_sel",
    )(qi_tab, ki_tab, q8, rows3, rows3, selb)


def _flash_win_kernel(q_ref, *refs, tq, back):
    nspan = back + 1
    k_refs, v_refs, o_ref = refs[:nspan], refs[nspan:2 * nspan], refs[2 * nspan]
    qi = pl.program_id(1)
    hq = NSA_HPG * tq
    span = nspan * tq
    qpos = qi * tq + lax.broadcasted_iota(jnp.int32, (tq, span), 0)
    kpos = (qi - back) * tq + lax.broadcasted_iota(jnp.int32, (tq, span), 1)
    ok = (kpos <= qpos) & (kpos > qpos - WINDOW) & (kpos >= 0)
    bias4 = jnp.concatenate([jnp.where(ok, 0.0, NEG_BIG)] * NSA_HPG, axis=0)
    kcat = jnp.concatenate([r[0] for r in k_refs], axis=0).astype(BF16)
    vcat = jnp.concatenate([r[0] for r in v_refs], axis=0).astype(BF16)
    for g in range(NSA_KV_HEADS):
        q4 = q_ref[g * NSA_HPG:(g + 1) * NSA_HPG].reshape(hq, LANES)
        qx = jnp.where(_group_lanes((hq, LANES), g), q4, 0.0)
        s = lax.dot_general(qx, kcat, (((1,), (1,)), ((), ())), preferred_element_type=F32) + bias4
        p = jnp.exp2(s - jnp.max(s, axis=-1, keepdims=True)).astype(BF16)
        vx = jnp.where(_group_lanes((span, LANES), g), vcat, 1.0)
        r = jnp.dot(p, vx, preferred_element_type=F32)
        _store_heads(o_ref, r / pltpu.roll(r, HEAD_DIM, 1), g, tq)


def _flash_win(q8, win3):
    b, l, _ = win3.shape
    tq = min(WIN_TQ, l)
    nq = l // tq
    back = -(-(WINDOW - 1) // tq)

    def kv_specs(blk):
        return [pl.BlockSpec((1, tq, KV_W), (lambda bi, qi, j=j: (bi, jnp.maximum(qi - back + j, 0), blk)))
                for j in range(back + 1)]

    return pl.pallas_call(
        functools.partial(_flash_win_kernel, tq=tq, back=back),
        grid=(b, nq),
        in_specs=[pl.BlockSpec((NSA_HEADS, tq, LANES), lambda bi, qi: (0, bi * nq + qi, 0))]
        + kv_specs(0) + kv_specs(1),
        out_specs=pl.BlockSpec((1, tq, NSA_WIDTH), lambda bi, qi: (bi, qi, 0)),
        out_shape=jax.ShapeDtypeStruct((b, l, NSA_WIDTH), F32),
        compiler_params=_cparams("parallel", "parallel"),
        name="flash_win",
    )(q8, *([win3] * (2 * (back + 1))))


def _softmax_update(s, m_prev, l_prev):
    m_new = jnp.maximum(m_prev, jnp.max(s, axis=-1, keepdims=True))
    alpha = jnp.exp2(m_prev - m_new)
    p = jnp.exp2(s - m_new[:, :1])
    return m_new, alpha, alpha * l_prev + jnp.sum(p, axis=-1, keepdims=True), p


def _paged_attn_kernel(pt_ref, *refs, npg, page, l_new, nbp):
    page_refs = refs[:npg]
    (qz_ref, sel_ref, kn_ref, vn_ref, win_ref, kwn_ref, vwn_ref,
     oslc_ref, owin_ref, wout_ref, m_sc, l_sc, acc_sc) = refs[npg:]
    ji = pl.program_id(1)
    nr = qz_ref.shape[1]
    span = npg * page

    @pl.when(ji == 0)
    def _init():
        m_sc[...] = jnp.full(m_sc.shape, NEG_BIG, F32)
        l_sc[...] = jnp.zeros(l_sc.shape, F32)
        acc_sc[...] = jnp.zeros(acc_sc.shape, F32)

    qz = qz_ref[0]
    kt = jnp.concatenate([page_refs[i][0, 0:KV_W, :] for i in range(npg)], axis=1).astype(BF16)
    vt = jnp.concatenate([page_refs[i][0, KV_W:2 * KV_W, :] for i in range(npg)], axis=1).astype(BF16)
    s = jnp.dot(qz, kt, preferred_element_type=F32)
    e_row = lax.broadcasted_iota(jnp.int32, (nbp, span), 0)
    e_col = lax.broadcasted_iota(jnp.int32, (nbp, span), 1)
    expand = jnp.where(e_row == ji * (span // SEL_BLOCK) + e_col // SEL_BLOCK, 1.0, 0.0).astype(BF16)
    picked = jnp.dot(sel_ref[0], expand, preferred_element_type=F32) > 0.5
    m_new, alpha, l_new_v, p = _softmax_update(jnp.where(picked, s, NEG_BIG), m_sc[...], l_sc[...])
    m_sc[...] = m_new
    l_sc[...] = l_new_v
    acc_sc[...] = alpha * acc_sc[...] + _dot_nt(p, vt)

    @pl.when(ji == pl.num_programs(1) - 1)
    def _fin():
        q_of_row = lax.broadcasted_iota(jnp.int32, (nr, l_new), 0) % l_new
        j_new = lax.broadcasted_iota(jnp.int32, (nr, l_new), 1)
        new_ok = j_new <= q_of_row
        sn = jnp.where(new_ok, _dot_nt(qz, kn_ref[0]), NEG_BIG)
        m2, a2, l2, p2 = _softmax_update(sn, m_sc[...], l_sc[...])
        oslc_ref[0] = (a2 * acc_sc[...] + _dot(p2, vn_ref[0])) / l2
        wlen = win_ref.shape[2]
        kw = win_ref[0, 0:KV_W, :]
        vw = win_ref[0, KV_W:2 * KV_W, :]
        i_old = lax.broadcasted_iota(jnp.int32, (nr, wlen), 1)
        q_old = lax.broadcasted_iota(jnp.int32, (nr, wlen), 0) % l_new
        sw = jnp.where(i_old + (WINDOW - wlen) > q_old, _dot(qz, kw), NEG_BIG)
        swn = jnp.where(new_ok, _dot_nt(qz, kwn_ref[0]), NEG_BIG)
        mw = jnp.maximum(jnp.max(sw, axis=-1, keepdims=True), jnp.max(swn, axis=-1, keepdims=True))
        pw = jnp.exp2(sw - mw)
        pwn = jnp.exp2(swn - mw)
        lw = jnp.sum(pw, axis=-1, keepdims=True) + jnp.sum(pwn, axis=-1, keepdims=True)
        owin_ref[0] = (_dot_nt(pw, vw) + _dot(pwn, vwn_ref[0])) / lw
        new_t = jnp.concatenate([kwn_ref[0], vwn_ref[0]], axis=1).T
        place = (lax.broadcasted_iota(jnp.int32, (l_new, wlen), 1)
                 == lax.broadcasted_iota(jnp.int32, (l_new, wlen), 0) + (wlen - l_new)).astype(F32)
        lane = lax.broadcasted_iota(jnp.int32, (2 * KV_W, wlen), 1)
        wout_ref[0] = jnp.where(lane < wlen - l_new, pltpu.roll(win_ref[0], wlen - l_new, 1), _dot_f32(new_t, place))


def _paged_attn(cache_t, page_table, pool_off, qz, sel_rows, rows3, win_t, win_off, win3):
    b, n_pages = page_table.shape
    page = cache_t.shape[2]
    npg = PAGES_PER_STEP
    nr = qz.shape[1]
    l_new = rows3.shape[1]
    nbp = sel_rows.shape[2]
    wlen = win_t.shape[2]
    kern = functools.partial(_paged_attn_kernel, npg=npg, page=page, l_new=l_new, nbp=nbp)
    per_b = lambda bi, ji, pt: (bi, 0, 0)
    return pl.pallas_call(
        kern,
        grid_spec=pltpu.PrefetchScalarGridSpec(
            num_scalar_prefetch=1, grid=(b, n_pages // npg),
            in_specs=_page_specs(npg, page, 1, pool_off)
            + [pl.BlockSpec((1, nr, KV_W), per_b),
               pl.BlockSpec((1, nr, nbp), per_b),
               pl.BlockSpec((1, l_new, KV_W), lambda bi, ji, pt: (bi, 0, 2)),
               pl.BlockSpec((1, l_new, KV_W), lambda bi, ji, pt: (bi, 0, 3)),
               pl.BlockSpec((1, 2 * KV_W, wlen), lambda bi, ji, pt: (bi + win_off, 0, 0)),
               pl.BlockSpec((1, l_new, KV_W), lambda bi, ji, pt: (bi, 0, 0)),
               pl.BlockSpec((1, l_new, KV_W), lambda bi, ji, pt: (bi, 0, 1))],
            out_specs=[pl.BlockSpec((1, nr, KV_W), per_b),
                       pl.BlockSpec((1, nr, KV_W), per_b),
                       pl.BlockSpec((1, 2 * KV_W, wlen), per_b)],
            scratch_shapes=[pltpu.VMEM((nr, LANES), F32), pltpu.VMEM((nr, LANES), F32),
                            pltpu.VMEM((nr, KV_W), F32)]),
        out_shape=[jax.ShapeDtypeStruct((b, nr, KV_W), F32),
                   jax.ShapeDtypeStruct((b, nr, KV_W), F32),
                   jax.ShapeDtypeStruct((b, 2 * KV_W, wlen), F32)],
        compiler_params=_cparams("parallel", "arbitrary"),
        name="paged_attn",
    )(page_table, *([cache_t] * npg), qz, sel_rows, rows3, rows3, win_t, win3, win3)


def _rms_norm(x, g, eps=EPS):
    xf = x.astype(F32)
    y = xf * lax.rsqrt(jnp.mean(xf * xf, axis=-1, keepdims=True) + eps)
    return (y * g.astype(F32)).astype(x.dtype)


def _rope(x, pos):
    half = ROT_DIM // 2
    inv = jnp.exp(-math.log(ROPE_THETA) * jnp.arange(half, dtype=F32) * 2.0 / ROT_DIM)
    ang = pos.astype(F32)[:, None] * inv[None, :]
    cos = jnp.cos(ang)[:, None, :]
    sin = jnp.sin(ang)[:, None, :]
    xf = x.astype(F32)
    x1, x2, rest = xf[..., :half], xf[..., half:ROT_DIM], xf[..., ROT_DIM:]
    return jnp.concatenate([x1 * cos - x2 * sin, x2 * cos + x1 * sin, rest], axis=-1).astype(x.dtype)


def _masked_softmax(s, mask):
    s = jnp.where(mask, s, -jnp.inf)
    m = jnp.max(s, axis=-1, keepdims=True)
    m = jnp.where(jnp.isfinite(m), m, 0.0)
    e = jnp.where(mask, jnp.exp(s - m), 0.0)
    return e / jnp.maximum(jnp.sum(e, axis=-1, keepdims=True), 1e-30)


def _cmp_summaries(x, pe, w):
    b, t, g, d = x.shape
    n_half = t // CMP_STRIDE
    halves = x[:, :n_half * CMP_STRIDE].astype(F32).reshape(b, n_half, CMP_STRIDE, g, d)
    wf = w.astype(F32)
    first = jnp.einsum('bnsgd,sde->bnge', halves[:, :-1], wf[:CMP_STRIDE])
    second = jnp.einsum('bnsgd,sde->bnge', halves[:, 1:], wf[CMP_STRIDE:])
    bias = jnp.einsum('sd,sde->e', pe.astype(F32), wf)
    return first + second + bias


def _overlap_matrix(n_cmp, n_sel):
    a = SEL_BLOCK // CMP_STRIDE
    bb = CMP_BLOCK // CMP_STRIDE
    i = jnp.arange(n_cmp)[:, None]
    j = jnp.arange(n_sel)[None, :]
    s = i - a * j + (bb - 1)
    cnt = jnp.minimum(jnp.minimum(s + 1, a + bb - 1 - s), min(a, bb))
    return jnp.maximum(cnt, 0).astype(F32)


def _nsa_selected_gather(q, k_hist, v_hist, idx, pos):
    b, l = q.shape[:2]
    t_len = k_hist.shape[1]
    n_sel = -(-t_len // SEL_BLOCK)
    n_k = idx.shape[-1]
    pad = n_sel * SEL_BLOCK - t_len

    def blocks(x):
        x = jnp.pad(x.astype(F32), ((0, 0), (0, pad), (0, 0), (0, 0)))
        return x.reshape(b, n_sel, SEL_BLOCK, NSA_KV_HEADS, HEAD_DIM).transpose(0, 3, 1, 2, 4)

    kb, vb = blocks(k_hist), blocks(v_hist)
    qb = math.gcd(l, Q_BLOCK)
    nq = l // qb
    qx = q.astype(F32).reshape(b * nq, qb, NSA_KV_HEADS, NSA_HPG, HEAD_DIM)
    ix = idx.reshape(b, NSA_KV_HEADS, nq, qb, n_k).transpose(0, 2, 1, 3, 4).reshape(b * nq, NSA_KV_HEADS, qb, n_k)
    px = jnp.tile(pos.reshape(nq, qb), (b, 1))
    bx = jnp.repeat(jnp.arange(b), nq)
    g_ix = jnp.arange(NSA_KV_HEADS)[:, None, None]
    offs = jnp.arange(SEL_BLOCK)
    n_keys = n_k * SEL_BLOCK

    def attend(args):
        qi, ii, pi, bi = args
        ks = kb[bi][g_ix, ii]
        vs = vb[bi][g_ix, ii]
        kpos = ii[..., None] * SEL_BLOCK + offs
        mask = (kpos <= pi[None, :, None, None]).reshape(NSA_KV_HEADS, 1, qb, n_keys)
        s = jnp.einsum('qghd,gqkjd->ghqkj', qi, ks).reshape(NSA_KV_HEADS, NSA_HPG, qb, n_keys) * ATTN_SCALE
        p = _masked_softmax(s, mask)
        return jnp.einsum('ghqn,gqnd->qghd', p, vs.reshape(NSA_KV_HEADS, qb, n_keys, HEAD_DIM))

    o = lax.map(attend, (qx, ix, px, bx))
    return o.reshape(b, l, NSA_HEADS, HEAD_DIM)


def _nsa_window(q, k_ctx, v_ctx, pos0):
    b, l = q.shape[:2]
    lc = k_ctx.shape[1]
    padw = ((0, 0), (WINDOW, 0), (0, 0), (0, 0))
    kp = jnp.pad(k_ctx.astype(F32), padw)
    vp = jnp.pad(v_ctx.astype(F32), padw)
    qb = math.gcd(l, Q_BLOCK)
    nq = l // qb
    span = WINDOW + qb
    qx = jnp.moveaxis(q.astype(F32).reshape(b, nq, qb, NSA_KV_HEADS, NSA_HPG, HEAD_DIM), 1, 0)
    starts = jnp.arange(nq) * qb
    first_pos = pos0 + l - lc

    def attend(args):
        qi, i0 = args
        kk = lax.dynamic_slice_in_dim(kp, i0 + lc - l, span, axis=1)
        vv = lax.dynamic_slice_in_dim(vp, i0 + lc - l, span, axis=1)
        kpos = pos0 + i0 - WINDOW + jnp.arange(span)
        qpos = pos0 + i0 + jnp.arange(qb)
        mask = ((kpos[None, :] >= first_pos) & (kpos[None, :] <= qpos[:, None])
                & (kpos[None, :] > qpos[:, None] - WINDOW))
        s = jnp.einsum('bqghd,bkgd->bghqk', qi, kk) * ATTN_SCALE
        p = _masked_softmax(s, mask)
        return jnp.einsum('bghqk,bkgd->bqghd', p, vv)

    o = lax.map(attend, (qx, starts))
    return jnp.moveaxis(o, 0, 1).reshape(b, l, NSA_HEADS, HEAD_DIM)


def _nsa_mixer_jax(n_q, n_kv, n_g, past_kv, win_buf, win_keep, pos0, q_g, k_g, cmp_pos, cmp_w):
    b, l, _ = n_q.shape
    pos = pos0 + jnp.arange(l)
    q = _rms_norm(n_q.reshape(b, l, NSA_HEADS, HEAD_DIM), q_g)
    q_rot = _rope(q, pos)
    kv = n_kv.reshape(b, l, 6, NSA_KV_HEADS, HEAD_DIM)
    k_slc = _rope(_rms_norm(kv[:, :, 2], k_g[1]), pos)
    k_win = _rope(_rms_norm(kv[:, :, 4], k_g[2]), pos)
    new_rows = jnp.stack([kv[:, :, 0], kv[:, :, 1], k_slc, kv[:, :, 3]], axis=2)
    new_win = jnp.stack([k_win, kv[:, :, 5]], axis=2)
    hist = jnp.concatenate([past_kv.astype(new_rows.dtype), new_rows], axis=1)
    ctx = jnp.concatenate([win_buf.astype(new_win.dtype), new_win], axis=1)
    t_len = hist.shape[1]

    kc = _rms_norm(_cmp_summaries(hist[:, :, 0], cmp_pos[0], cmp_w[0]), k_g[0])
    vc = _cmp_summaries(hist[:, :, 1], cmp_pos[1], cmp_w[1])
    n_cmp = kc.shape[1]
    qg = q.astype(F32).reshape(b, l, NSA_KV_HEADS, NSA_HPG, HEAD_DIM)
    s = jnp.einsum('blghd,bngd->bghln', qg, kc) * ATTN_SCALE
    cmp_end = jnp.arange(n_cmp) * CMP_STRIDE + CMP_BLOCK - 1
    p_cmp = _masked_softmax(s, cmp_end[None, :] <= pos[:, None])
    o_cmp = jnp.einsum('bghln,bngd->blghd', p_cmp, vc).reshape(b, l, NSA_HEADS, HEAD_DIM)

    n_sel = -(-t_len // SEL_BLOCK)
    imp = jnp.einsum('bghln,nj->bglj', p_cmp, _overlap_matrix(n_cmp, n_sel))
    blk = jnp.arange(n_sel)[None, :]
    cur = (pos // SEL_BLOCK)[:, None]
    valid = blk * SEL_BLOCK <= pos[:, None]
    forced = (blk == 0) | (blk == cur) | (blk == cur - 1)
    score = jnp.where(forced, jnp.inf, jnp.where(valid, imp, -jnp.inf))
    _, idx = lax.top_k(score, min(N_SELECT, n_sel))
    o_slc = _nsa_selected_gather(q_rot, hist[:, :, 2], hist[:, :, 3], idx, pos)
    o_win = _nsa_window(q_rot, ctx[:, :, 0], ctx[:, :, 1], pos0)

    gate = jax.nn.sigmoid(n_g.astype(F32)).reshape(b, l, 3, NSA_HEADS, 1)
    o = gate[:, :, 0] * o_cmp + gate[:, :, 1] * o_slc + gate[:, :, 2] * o_win
    return o.reshape(b, l, NSA_WIDTH), new_rows, ctx[:, ctx.shape[1] - win_keep:]


def _nsa_fresh(u_nsa, b, l, win_keep, q_g, k_g, cmp_pos, cmp_w):
    qn, qr, rows, win = _nsa_prep(u_nsa, jnp.arange(l), q_g, k_g, l)
    rows3 = rows.reshape(b, l, N_KV_SLOTS * KV_W)
    win3 = win.reshape(b, l, 2 * KV_W)
    wcat, bias = _cmp_weights(cmp_pos, cmp_w)
    t_use = (l // CMP_STRIDE) * CMP_STRIDE
    kc, vc = _cmp_kv(rows3, 0, 1, t_use, wcat, bias, k_g[0])
    o_cmp, selb = _cmp_attn(qn.reshape(b, l, NSA_WIDTH), kc, vc, 0, l, bias_out=True)
    o_slc = _flash_sel(qr, rows3, selb)
    o_win = _flash_win(qr, win3)
    branches = tuple(o.reshape(b * l, NSA_WIDTH) for o in (o_cmp, o_slc, o_win))
    new_rows = rows3.reshape(b, l, N_KV_SLOTS, NSA_KV_HEADS, HEAD_DIM)
    new_win = win3[:, l - win_keep:].reshape(b, win_keep, 2, NSA_KV_HEADS, HEAD_DIM)
    return branches, new_rows, new_win


def _nsa_paged(u_nsa, b, l, paged, q_g, k_g, cmp_pos, cmp_w):
    cache_t, page_table, pool_off, win_t, win_off = paged
    past_len = page_table.shape[1] * cache_t.shape[2]
    assert (past_len + l) // CMP_STRIDE == past_len // CMP_STRIDE and past_len % SEL_BLOCK == 0
    qn, qr, rows, win = _nsa_prep(u_nsa, past_len + jnp.arange(l), q_g, k_g, l)
    rows3 = rows.reshape(b, l, N_KV_SLOTS * KV_W)
    win3 = win.reshape(b, l, 2 * KV_W)
    wcat, bias = _cmp_weights(cmp_pos, cmp_w)
    kc, vc = _cmp_kv_paged(cache_t, page_table, pool_off, wcat, bias, k_g[0])
    o_cmp, sel_t = _cmp_attn(qn.reshape(b, l, NSA_WIDTH), kc, vc, past_len, past_len + l)
    nbp = sel_t.shape[1] // NSA_KV_HEADS
    sel_rows = jnp.swapaxes(sel_t.reshape(b, NSA_KV_HEADS, 1, nbp, l), 3, 4)
    sel_rows = jnp.broadcast_to(sel_rows, (b, NSA_KV_HEADS, NSA_HPG, l, nbp)).reshape(b, NSA_HEADS * l, nbp)
    q5 = jnp.transpose(qr[:, :, :HEAD_DIM].reshape(NSA_KV_HEADS, NSA_HPG, b, l, HEAD_DIM), (2, 0, 1, 3, 4))
    qz = jnp.einsum('bghqd,gk->bghqkd', q5, jnp.eye(NSA_KV_HEADS, dtype=q5.dtype)).reshape(b, NSA_HEADS * l, KV_W)
    o_slc_z, o_win_z, wout = _paged_attn(cache_t, page_table, pool_off, qz, sel_rows, rows3, win_t, win_off, win3)

    def own_group(o):
        o6 = o.reshape(b, NSA_KV_HEADS, NSA_HPG, l, NSA_KV_HEADS, HEAD_DIM)
        d = jnp.stack([o6[:, g, :, :, g, :] for g in range(NSA_KV_HEADS)], axis=1)
        return jnp.transpose(d, (0, 3, 1, 2, 4)).reshape(b * l, NSA_WIDTH)

    branches = (o_cmp.reshape(b * l, NSA_WIDTH), own_group(o_slc_z), own_group(o_win_z))
    new_rows = rows3.reshape(b, l, N_KV_SLOTS, NSA_KV_HEADS, HEAD_DIM)
    wlen = wout.shape[2]
    new_win = jnp.transpose(wout.reshape(b, 2, NSA_KV_HEADS, HEAD_DIM, wlen), (0, 4, 1, 2, 3))
    return branches, new_rows, new_win


def _hybrid_layer(x, paged, win_keep, s_gla, c_ml, n_ml, m_ml, conv_ml,
                  norm_g, w_in_pad, w_out_bf, gla_w_gate, gla_b_gate, gla_norm_g,
                  nsa_q_norm_g, nsa_k_norm_g, nsa_cmp_pos, nsa_cmp_w,
                  ml_conv_w, ml_conv_b, ml_gate_b, ml_norm_g):
    b, l, _ = x.shape
    rows = b * l
    tm = 256
    x2d = x.reshape(rows, D_MODEL)
    u_gla, u_nsa, u_ml = _proj_in(x2d, norm_g, w_in_pad, tm)

    nk = GLA_HEADS * GLA_DK
    wg = jnp.pad(gla_w_gate.astype(F32), ((0, LANES - GLA_RANK), (0, 0))).astype(BF16)
    o_a, st_new = _gla(u_gla, _gla_state_in(s_gla), wg, gla_b_gate.astype(F32).reshape(1, nk),
                       jnp.tile(gla_norm_g.astype(F32).reshape(1, GLA_DV), (1, GLA_HEADS)), b, l)
    s_new = _gla_state_out(st_new)

    if paged is None:
        o_nsa, new_rows, new_win = _nsa_fresh(u_nsa, b, l, win_keep, nsa_q_norm_g, nsa_k_norm_g,
                                              nsa_cmp_pos, nsa_cmp_w)
    else:
        o_nsa, new_rows, new_win = _nsa_paged(u_nsa, b, l, paged, nsa_q_norm_g, nsa_k_norm_g,
                                              nsa_cmp_pos, nsa_cmp_w)

    o_c, c_new, n_new, m_new, conv_new = _mlstm(u_ml, c_ml, n_ml, m_ml, conv_ml, ml_conv_w, ml_conv_b,
                                                ml_gate_b, ml_norm_g, b, l)

    y = _proj_out(o_a.reshape(rows, GLA_WIDTH), *o_nsa, u_nsa, o_c.reshape(rows, ML_WIDTH), x2d, w_out_bf, tm)
    return y.reshape(b, l, D_MODEL), new_rows, new_win, s_new, c_new, n_new, m_new, conv_new


def kernel(x_prompt, x_sample, cache_nsa_kv, state_nsa_win, state_gla, state_mlstm_C, state_mlstm_n,
           state_mlstm_m, state_mlstm_conv, page_table, norm_g, w_in, w_out, gla_w_gate, gla_b_gate,
           gla_norm_g, nsa_q_norm_g, nsa_k_norm_g, nsa_cmp_pos, nsa_cmp_w, ml_conv_w, ml_conv_b,
           ml_gate_b, ml_norm_g):
    bp, sp, _ = x_prompt.shape
    bs, _, _ = x_sample.shape
    depth = w_in.shape[0]
    n_pages = page_table.shape[1]
    past_len = n_pages * cache_nsa_kv.shape[2]
    dt = x_prompt.dtype
    zero_gla = jnp.zeros((bp, GLA_HEADS, GLA_DK, GLA_DV), F32)
    zero_c = jnp.zeros((bp, ML_HEADS, ML_DH, ML_DH), F32)
    zero_n = jnp.zeros((bp, ML_HEADS, ML_DH), F32)
    zero_m = jnp.zeros((bp, ML_HEADS), F32)
    zero_conv = jnp.zeros((bp, CONV_W - 1, 2 * ML_WIDTH), dt)
    keep_p = min(WINDOW, sp)
    keep_s = state_nsa_win.shape[2]
    n_pool = cache_nsa_kv.shape[1]
    cache_t = jnp.transpose(cache_nsa_kv, (0, 1, 3, 4, 5, 2)).reshape(
        depth * n_pool, N_KV_SLOTS * KV_W, cache_nsa_kv.shape[2]).astype(F32)
    win_t = jnp.transpose(state_nsa_win, (0, 1, 3, 4, 5, 2)).reshape(depth * bs, 2 * KV_W, keep_s).astype(F32)

    y_prompt, y_sample = x_prompt, x_sample
    p_layers, s_layers = [], []
    for layer in range(depth):
        w = (norm_g[layer], _pad_w_in(w_in[layer]), w_out[layer].astype(BF16), gla_w_gate[layer],
             gla_b_gate[layer], gla_norm_g[layer],
             nsa_q_norm_g[layer], nsa_k_norm_g[layer], nsa_cmp_pos[layer], nsa_cmp_w[layer],
             ml_conv_w[layer], ml_conv_b[layer], ml_gate_b[layer], ml_norm_g[layer])
        y_prompt, *p_new = _hybrid_layer(y_prompt, None, keep_p, zero_gla, zero_c, zero_n,
                                         zero_m, zero_conv, *w)
        paged = (cache_t, page_table, layer * n_pool, win_t, layer * bs)
        y_sample, *s_new = _hybrid_layer(y_sample, paged, keep_s,
                                         state_gla[layer], state_mlstm_C[layer], state_mlstm_n[layer],
                                         state_mlstm_m[layer], state_mlstm_conv[layer], *w)
        p_layers.append(p_new)
        s_layers.append(s_new)
    p_kv, p_win, p_gla, p_c, p_n, p_m, p_conv = [jnp.stack(z) for z in zip(*p_layers)]
    s_kv, s_win, s_gla, s_c, s_n, s_m, s_conv = [jnp.stack(z) for z in zip(*s_layers)]
    return (y_prompt, y_sample, p_kv, s_kv, p_win, s_win, p_gla, s_gla, p_c, s_c, p_n, s_n, p_m, s_m, p_conv, s_conv)
```

```python
import functools
import math

import jax
import jax.numpy as jnp
import numpy as np
from jax import lax
from jax.experimental import pallas as pl
from jax.experimental.pallas import tpu as pltpu

F32 = jnp.float32
BF16 = jnp.bfloat16
HIGHEST = lax.Precision.HIGHEST

D_MODEL = 1024
HEAD_DIM = 64
GLA_WIDTH = D_MODEL // 4
NSA_WIDTH = D_MODEL // 2
ML_WIDTH = D_MODEL - GLA_WIDTH - NSA_WIDTH
D_MIX = GLA_WIDTH + NSA_WIDTH + ML_WIDTH

GLA_HEADS = GLA_WIDTH // HEAD_DIM
GLA_DK = HEAD_DIM // 2
GLA_DV = HEAD_DIM
GLA_RANK = 16
GLA_TAU = 16.0
GLA_CHUNK = 64

NSA_HEADS = NSA_WIDTH // HEAD_DIM
NSA_KV_HEADS = 2
NSA_HPG = NSA_HEADS // NSA_KV_HEADS
CMP_BLOCK = 32
CMP_STRIDE = 16
SEL_BLOCK = 64
N_SELECT = 16
WINDOW = 512
Q_BLOCK = 128
N_KV_SLOTS = 4
ROT_DIM = HEAD_DIM // 4
ROPE_THETA = 500000.0
ATTN_SCALE = HEAD_DIM ** -0.5

ML_HEADS = ML_WIDTH // HEAD_DIM
ML_DH = HEAD_DIM
ML_CHUNK = 64
CONV_W = 4

SPLIT_SIZES = (GLA_HEADS * GLA_DK, GLA_HEADS * GLA_DK, GLA_WIDTH, GLA_RANK, GLA_WIDTH,
               NSA_WIDTH, 6 * NSA_KV_HEADS * HEAD_DIM, 3 * NSA_HEADS, NSA_WIDTH,
               2 * ML_WIDTH, ML_WIDTH, 2 * ML_HEADS, ML_WIDTH, ML_WIDTH)

LANES = 128
SUBLANES = 8
VMEM_LIMIT = 56 * 1024 * 1024
NEG_BIG = -1e30
EPS = 1e-6
LOG2E = math.log2(math.e)


def _round_up(n, m):
    return -(-n // m) * m


PAD_SIZES = tuple(_round_up(s, LANES) for s in SPLIT_SIZES)
D_IN_PAD = sum(PAD_SIZES)
W_GLA = sum(PAD_SIZES[0:5])
W_NSA = sum(PAD_SIZES[5:9])
W_ML = sum(PAD_SIZES[9:14])
KV_W = NSA_KV_HEADS * HEAD_DIM


def _dot(a, b):
    return jnp.dot(a.astype(BF16), b.astype(BF16), preferred_element_type=F32)


def _dot_nt(a, b):
    return lax.dot_general(a.astype(BF16), b.astype(BF16), (((1,), (1,)), ((), ())), preferred_element_type=F32)


def _dot_tn(a, b):
    return lax.dot_general(a.astype(BF16), b.astype(BF16), (((0,), (0,)), ((), ())), preferred_element_type=F32)


def _dot_f32(a, b):
    return jnp.dot(a, b, precision=HIGHEST, preferred_element_type=F32)


def _dot_split(a, b):
    a_hi = a.astype(BF16)
    a_lo = (a - a_hi.astype(F32)).astype(BF16)
    bb = b.astype(BF16)
    return jnp.dot(a_hi, bb, preferred_element_type=F32) + jnp.dot(a_lo, bb, preferred_element_type=F32)


def _log_sigmoid(x):
    return jnp.minimum(x, 0.0) - jnp.log1p(jnp.exp(-jnp.abs(x)))


def _sigmoid(x):
    return 1.0 / (1.0 + jnp.exp(-x))


def _silu(x):
    return x * _sigmoid(x)


def _group_mean_matrix(width):
    g = np.kron(np.eye(width // HEAD_DIM, dtype=np.float32), np.full((HEAD_DIM, HEAD_DIM), 1.0 / HEAD_DIM, np.float32))
    return jnp.asarray(g, dtype=BF16)


def _group_norm(x, gmat, gain):
    ms = _dot_split(x * x, gmat)
    return x * lax.rsqrt(ms + EPS) * gain


def _cparams(*sem):
    return pltpu.CompilerParams(dimension_semantics=sem, vmem_limit_bytes=VMEM_LIMIT)


def _proj_in_kernel(x_ref, g_ref, w_ref, ug_ref, un_ref, um_ref):
    x = x_ref[...]
    y = x * lax.rsqrt(jnp.mean(x * x, axis=-1, keepdims=True) + EPS) * g_ref[...]
    r = jnp.dot(y.astype(BF16), w_ref[...], preferred_element_type=F32)
    ug_ref[...] = r[:, 0:W_GLA]
    un_ref[...] = r[:, W_GLA:W_GLA + W_NSA]
    um_ref[...] = r[:, W_GLA + W_NSA:D_IN_PAD]


def _proj_in(x2d, g, w_pad, tm):
    rows = x2d.shape[0]
    return pl.pallas_call(
        _proj_in_kernel,
        grid=(rows // tm,),
        in_specs=[pl.BlockSpec((tm, D_MODEL), lambda i: (i, 0)),
                  pl.BlockSpec((1, D_MODEL), lambda i: (0, 0)),
                  pl.BlockSpec((D_MODEL, D_IN_PAD), lambda i: (0, 0))],
        out_specs=[pl.BlockSpec((tm, W_GLA), lambda i: (i, 0)),
                   pl.BlockSpec((tm, W_NSA), lambda i: (i, 0)),
                   pl.BlockSpec((tm, W_ML), lambda i: (i, 0))],
        out_shape=[jax.ShapeDtypeStruct((rows, W_GLA), F32),
                   jax.ShapeDtypeStruct((rows, W_NSA), F32),
                   jax.ShapeDtypeStruct((rows, W_ML), F32)],
        compiler_params=_cparams("parallel"),
        name="proj_in",
    )(x2d, g.reshape(1, D_MODEL), w_pad)


def _pad_w_in(w_in):
    parts = []
    off = 0
    for s, p in zip(SPLIT_SIZES, PAD_SIZES):
        seg = w_in[:, off:off + s]
        if p != s:
            seg = jnp.pad(seg, ((0, 0), (0, p - s)))
        parts.append(seg)
        off += s
    return jnp.concatenate(parts, axis=1).astype(BF16)


def _proj_out_kernel(oa_ref, ocmp_ref, oslc_ref, owin_ref, gz_ref, oc_ref, x_ref, w_ref, e_ref, y_ref):
    gate = _sigmoid(gz_ref[:, 0:LANES])
    ob = (_dot_split(gate, e_ref[0]) * ocmp_ref[...] + _dot_split(gate, e_ref[1]) * oslc_ref[...]
          + _dot_split(gate, e_ref[2]) * owin_ref[...]) * _silu(gz_ref[:, LANES:LANES + NSA_WIDTH])
    y = x_ref[...]
    y = y + _dot(oa_ref[...], w_ref[0:GLA_WIDTH, :])
    y = y + _dot(ob, w_ref[GLA_WIDTH:GLA_WIDTH + NSA_WIDTH, :])
    y = y + _dot(oc_ref[...], w_ref[GLA_WIDTH + NSA_WIDTH:D_MIX, :])
    y_ref[...] = y


def _gate_expand():
    e = np.zeros((3, LANES, NSA_WIDTH), np.float32)
    for j in range(3):
        for h in range(NSA_HEADS):
            e[j, j * NSA_HEADS + h, h * HEAD_DIM:(h + 1) * HEAD_DIM] = 1.0
    return jnp.asarray(e, dtype=BF16)


def _proj_out(oa, o_cmp, o_slc, o_win, u_nsa, oc, x2d, w_bf, tm):
    rows = x2d.shape[0]
    gz_w = LANES + NSA_WIDTH
    gz_blk = (NSA_WIDTH + 6 * KV_W) // gz_w
    assert gz_blk * gz_w == NSA_WIDTH + 6 * KV_W
    row = lambda i: (i, 0)
    return pl.pallas_call(
        _proj_out_kernel,
        grid=(rows // tm,),
        in_specs=[pl.BlockSpec((tm, GLA_WIDTH), row),
                  pl.BlockSpec((tm, NSA_WIDTH), row),
                  pl.BlockSpec((tm, NSA_WIDTH), row),
                  pl.BlockSpec((tm, NSA_WIDTH), row),
                  pl.BlockSpec((tm, gz_w), lambda i: (i, gz_blk)),
                  pl.BlockSpec((tm, ML_WIDTH), row),
                  pl.BlockSpec((tm, D_MODEL), row),
                  pl.BlockSpec((D_MIX, D_MODEL), lambda i: (0, 0)),
                  pl.BlockSpec((3, LANES, NSA_WIDTH), lambda i: (0, 0, 0))],
        out_specs=pl.BlockSpec((tm, D_MODEL), row),
        out_shape=jax.ShapeDtypeStruct((rows, D_MODEL), F32),
        compiler_params=_cparams("parallel"),
        name="proj_out",
    )(oa, o_cmp, o_slc, o_win, u_nsa, oc, x2d, w_bf, _gate_expand())


def _gla_kernel(u_ref, st0_ref, wg_ref, bg_ref, ng_ref, gm_ref, o_ref, st_ref, *, bb, tl, c):
    @pl.when(pl.program_id(1) == 0)
    def _init():
        st_ref[...] = st0_ref[...]

    nk = GLA_HEADS * GLA_DK
    a_off = 2 * nk + GLA_WIDTH
    q, k, v, log_a = [], [], [], []
    for i in range(bb):
        q.append(u_ref[i, :, 0:nk] * (GLA_DK ** -0.5))
        k.append(u_ref[i, :, nk:2 * nk])
        v.append(u_ref[i, :, 2 * nk:2 * nk + GLA_WIDTH])
        pre = _dot(u_ref[i, :, a_off:a_off + LANES], wg_ref[...]) + bg_ref[...]
        log_a.append(_log_sigmoid(pre) * (1.0 / GLA_TAU))

    tril = (lax.broadcasted_iota(jnp.int32, (c, c), 0) >= lax.broadcasted_iota(jnp.int32, (c, c), 1)).astype(F32)
    hc = GLA_HEADS * c
    tri_h = (lax.broadcasted_iota(jnp.int32, (hc, c), 0) % c) >= lax.broadcasted_iota(jnp.int32, (hc, c), 1)
    k_head = lax.broadcasted_iota(jnp.int32, (1, nk), 1) // GLA_DK
    v_head = lax.broadcasted_iota(jnp.int32, (1, GLA_WIDTH), 1) // GLA_DV
    st_diag = (lax.broadcasted_iota(jnp.int32, (GLA_WIDTH, nk), 0) // GLA_DV
               == lax.broadcasted_iota(jnp.int32, (GLA_WIDTH, nk), 1) // GLA_DK)

    st = [st_ref[i] for i in range(bb)]
    outs = [[] for _ in range(bb)]
    for j in range(tl // c):
        sl = slice(j * c, (j + 1) * c)
        for i in range(bb):
            b = _dot_f32(tril, log_a[i][sl])
            blast = b[c - 1:c]
            qe = q[i][sl] * jnp.exp(b)
            ke = k[i][sl] * jnp.exp(-b)
            kl = k[i][sl] * jnp.exp(blast - b)
            vc = v[i][sl]
            qx = jnp.concatenate([jnp.where(k_head == h, qe, 0.0) for h in range(GLA_HEADS)], axis=0)
            a = jnp.where(tri_h, _dot_nt(qx, ke), 0.0)
            r = _dot(a, vc)
            o = _dot_nt(qe, st[i])
            for h in range(GLA_HEADS):
                o = o + jnp.where(v_head == h, r[h * c:(h + 1) * c], 0.0)
            st[i] = st[i] * jnp.exp(blast) + jnp.where(st_diag, _dot_tn(vc, kl), 0.0)
            outs[i].append(o)
    for i in range(bb):
        st_ref[i] = st[i]
        o = outs[i][0] if len(outs[i]) == 1 else jnp.concatenate(outs[i], axis=0)
        z = u_ref[i, :, a_off + LANES:a_off + LANES + GLA_WIDTH]
        o_ref[i] = _group_norm(o, gm_ref[...], ng_ref[...]) * _silu(z)


GLA_SEQS_PER_STEP = 4


def _gla(u_gla, st0, wg, bg, ng, b, l):
    tl = min(l, 128)
    c = min(l, 16)
    bb = math.gcd(b, GLA_SEQS_PER_STEP)
    nk = GLA_HEADS * GLA_DK
    u3 = u_gla.reshape(b, l, W_GLA)
    kern = functools.partial(_gla_kernel, bb=bb, tl=tl, c=c)
    return pl.pallas_call(
        kern,
        grid=(b // bb, l // tl),
        in_specs=[pl.BlockSpec((bb, tl, W_GLA), lambda bi, li: (bi, li, 0)),
                  pl.BlockSpec((bb, GLA_WIDTH, nk), lambda bi, li: (bi, 0, 0)),
                  pl.BlockSpec((LANES, nk), lambda bi, li: (0, 0)),
                  pl.BlockSpec((1, nk), lambda bi, li: (0, 0)),
                  pl.BlockSpec((1, GLA_WIDTH), lambda bi, li: (0, 0)),
                  pl.BlockSpec((GLA_WIDTH, GLA_WIDTH), lambda bi, li: (0, 0))],
        out_specs=[pl.BlockSpec((bb, tl, GLA_WIDTH), lambda bi, li: (bi, li, 0)),
                   pl.BlockSpec((bb, GLA_WIDTH, nk), lambda bi, li: (bi, 0, 0))],
        out_shape=[jax.ShapeDtypeStruct((b, l, GLA_WIDTH), F32),
                   jax.ShapeDtypeStruct((b, GLA_WIDTH, nk), F32)],
        compiler_params=_cparams("parallel", "arbitrary"),
        name="gla",
    )(u3, st0, wg, bg, ng, _group_mean_matrix(GLA_WIDTH))


def _gla_state_in(s):
    b = s.shape[0]
    st = jnp.swapaxes(s.astype(F32), 2, 3)
    eye = jnp.eye(GLA_HEADS, dtype=F32)
    full = st[:, :, :, None, :] * eye[None, :, None, :, None]
    return full.reshape(b, GLA_WIDTH, GLA_HEADS * GLA_DK)


def _gla_state_out(st):
    b = st.shape[0]
    full = st.reshape(b, GLA_HEADS, GLA_DV, GLA_HEADS, GLA_DK)
    diag = jnp.stack([full[:, h, :, h, :] for h in range(GLA_HEADS)], axis=1)
    return jnp.swapaxes(diag, 2, 3)


def _mlstm_kernel(u_ref, c0_ref, n0_ref, m0_ref, cv0_ref, cw_ref, cb_ref, gb_ref, ng_ref, gm_ref,
                  o_ref, c_ref, n_ref, m_ref, cv_ref, xp_sc, *, bb, tl):
    @pl.when(pl.program_id(1) == 0)
    def _init():
        c_ref[...] = c0_ref[...]
        n_ref[...] = n0_ref[...]
        m_ref[...] = m0_ref[...]
        xp_sc[:, 0:SUBLANES, :] = cv0_ref[...]

    pairs = ML_HEADS // 2
    state = [([c_ref[i, j] for j in range(pairs)],
              [n_ref[i, j:j + 1, :] for j in range(pairs)],
              [m_ref[i, h:h + 1, :] for h in range(ML_HEADS)]) for i in range(bb)]
    new_state = [_mlstm_seq(i, state[i], u_ref, cw_ref, cb_ref, gb_ref, ng_ref, gm_ref, o_ref, cv_ref, xp_sc, tl)
                 for i in range(bb)]
    for i in range(bb):
        cps, nps, mbs = new_state[i]
        for j in range(pairs):
            c_ref[i, j] = cps[j]
            n_ref[i, j:j + 1, :] = nps[j]
        for h in range(ML_HEADS):
            m_ref[i, h:h + 1, :] = mbs[h]


def _mlstm_seq(i, state, u_ref, cw_ref, cb_ref, gb_ref, ng_ref, gm_ref, o_ref, cv_ref, xp_sc, tl):
    cps, nps, mbs = state
    c = tl
    w2 = 2 * ML_WIDTH
    u = u_ref[i]
    xp_sc[i, SUBLANES:SUBLANES + tl, :] = u[:, 0:w2]
    conv = cb_ref[...]
    for w in range(CONV_W):
        off = SUBLANES - (CONV_W - 1) + w
        conv = conv + xp_sc[i, off:off + tl, :] * cw_ref[w:w + 1, :]
    tail = xp_sc[i, tl:tl + SUBLANES, :]
    xp_sc[i, 0:SUBLANES, :] = tail
    cv_ref[i] = tail

    qk = _silu(conv)
    mq = qk[:, 0:ML_WIDTH]
    mk = qk[:, ML_WIDTH:w2] * (ML_DH ** -0.5)
    mv = u[:, w2:w2 + ML_WIDTH]
    ifg = u[:, w2 + ML_WIDTH:w2 + ML_WIDTH + LANES] + gb_ref[...]
    logf = _log_sigmoid(ifg)
    og_off = w2 + ML_WIDTH + LANES
    og = _sigmoid(u[:, og_off:og_off + ML_WIDTH])
    zz = _silu(u[:, og_off + ML_WIDTH:og_off + 2 * ML_WIDTH])

    tri = lax.broadcasted_iota(jnp.int32, (c, c), 0) >= lax.broadcasted_iota(jnp.int32, (c, c), 1)
    lane = lax.broadcasted_iota(jnp.int32, (c, LANES), 1)
    low = lane < ML_DH
    low_row = lax.broadcasted_iota(jnp.int32, (1, LANES), 1) < ML_DH
    sq_row = lax.broadcasted_iota(jnp.int32, (LANES, LANES), 0)
    sq_col = lax.broadcasted_iota(jnp.int32, (LANES, LANES), 1)
    same_head = (sq_row < ML_DH) == (sq_col < ML_DH)

    def wide(x):
        return x[:, :c] if c <= LANES else jnp.concatenate([x] * (c // LANES), axis=1)

    fcum_all = _dot_f32(tri.astype(F32), logf)
    gates_t = jnp.where(lane < ML_HEADS, ifg, fcum_all).T
    new_cps, new_nps, new_mbs = [], [], []
    for j in range(ML_HEADS // 2):
        ps = slice(j * LANES, (j + 1) * LANES)
        q_s, k_s, v_s = mq[:, ps], mk[:, ps], mv[:, ps]
        acc = jnp.zeros((c, 2 * LANES), F32)
        per_head = []
        for hl in range(2):
            h = 2 * j + hl
            own = low if hl == 0 else jnp.logical_not(low)
            fc = jnp.broadcast_to(fcum_all[:, ML_HEADS + h:ML_HEADS + h + 1], (c, LANES))
            ii = jnp.broadcast_to(ifg[:, h:h + 1], (c, LANES))
            dm = jnp.where(tri, wide(fc) - gates_t[ML_HEADS + h:ML_HEADS + h + 1, :] + gates_t[h:h + 1, :], NEG_BIG)
            inter = fc + mbs[h]
            m = jnp.maximum(inter, jnp.max(dm, axis=-1, keepdims=True))
            sij = _dot_nt(jnp.where(own, q_s, 0.0), k_s) * jnp.exp(dm - wide(m))
            acc = acc + _dot(sij, jnp.concatenate([jnp.where(own, v_s, 0.0), jnp.where(own, 1.0, 0.0)], axis=1))
            m_last = m[c - 1:c]
            f_last = fc[c - 1:c]
            per_head.append((m, jnp.exp(inter - m), jnp.exp(f_last - fc + ii - m_last),
                             jnp.exp(f_last + mbs[h] - m_last)))
            new_mbs.append(m_last)
        m_p, w_p, wj_p = (jnp.where(low, per_head[0][t], per_head[1][t]) for t in range(3))
        dec_row = jnp.where(low_row, per_head[0][3], per_head[1][3])
        n_mat = jnp.where(same_head, jnp.broadcast_to(nps[j], (LANES, LANES)), 0.0)
        num = w_p * _dot_nt(q_s, cps[j]) + acc[:, 0:LANES]
        den = w_p * _dot_nt(q_s, n_mat) + acc[:, LANES:2 * LANES]
        hh = num / jnp.maximum(jnp.abs(den), jnp.exp(-m_p))
        hn = hh * lax.rsqrt(_dot_split(hh * hh, gm_ref[...]) + EPS) * ng_ref[...]
        o_ref[i, :, ps] = hn * og[:, ps] * zz[:, ps]
        dec_mat = jnp.where(sq_row < ML_DH, jnp.broadcast_to(per_head[0][3], (LANES, LANES)),
                            jnp.broadcast_to(per_head[1][3], (LANES, LANES)))
        new_cps.append(dec_mat * cps[j] + jnp.where(same_head, _dot_tn(wj_p * v_s, k_s), 0.0))
        new_nps.append(dec_row * nps[j] + jnp.sum(wj_p * k_s, axis=0, keepdims=True))
    return new_cps, new_nps, new_mbs


ML_SEQS_PER_STEP = 2


def _mlstm(u_ml, c0, n0, m0, conv0, cw, cb, gb, ng, b, l):
    tl = min(l, 256)
    bb = math.gcd(b, ML_SEQS_PER_STEP)
    pairs = ML_HEADS // 2
    w2 = 2 * ML_WIDTH
    u3 = u_ml.reshape(b, l, W_ML)
    eye2 = jnp.eye(2, dtype=F32)
    c0p = jnp.einsum('bphed,hk->bphekd', c0.astype(F32).reshape(b, pairs, 2, ML_DH, ML_DH), eye2)
    c0p = c0p.reshape(b, pairs, LANES, LANES)
    n0p = n0.astype(F32).reshape(b, pairs, LANES)
    m0b = jnp.broadcast_to(m0.astype(F32)[:, :, None], (b, ML_HEADS, LANES))
    cv0 = jnp.pad(conv0.astype(F32), ((0, 0), (SUBLANES - (CONV_W - 1), 0), (0, 0)))
    gbp = jnp.pad(gb.astype(F32).reshape(1, 2 * ML_HEADS), ((0, 0), (0, LANES - 2 * ML_HEADS)))
    kern = functools.partial(_mlstm_kernel, bb=bb, tl=tl)
    st = lambda bi, li: (bi, 0, 0)
    st4 = lambda bi, li: (bi, 0, 0, 0)
    cst = lambda bi, li: (0, 0)
    o, c_new, n_new, m_new, cv = pl.pallas_call(
        kern,
        grid=(b // bb, l // tl),
        in_specs=[pl.BlockSpec((bb, tl, W_ML), lambda bi, li: (bi, li, 0)),
                  pl.BlockSpec((bb, pairs, LANES, LANES), st4),
                  pl.BlockSpec((bb, pairs, LANES), st),
                  pl.BlockSpec((bb, ML_HEADS, LANES), st),
                  pl.BlockSpec((bb, SUBLANES, w2), st),
                  pl.BlockSpec((CONV_W, w2), cst),
                  pl.BlockSpec((1, w2), cst),
                  pl.BlockSpec((1, LANES), cst),
                  pl.BlockSpec((1, LANES), cst),
                  pl.BlockSpec((LANES, LANES), cst)],
        out_specs=[pl.BlockSpec((bb, tl, ML_WIDTH), lambda bi, li: (bi, li, 0)),
                   pl.BlockSpec((bb, pairs, LANES, LANES), st4),
                   pl.BlockSpec((bb, pairs, LANES), st),
                   pl.BlockSpec((bb, ML_HEADS, LANES), st),
                   pl.BlockSpec((bb, SUBLANES, w2), st)],
        out_shape=[jax.ShapeDtypeStruct((b, l, ML_WIDTH), F32),
                   jax.ShapeDtypeStruct((b, pairs, LANES, LANES), F32),
                   jax.ShapeDtypeStruct((b, pairs, LANES), F32),
                   jax.ShapeDtypeStruct((b, ML_HEADS, LANES), F32),
                   jax.ShapeDtypeStruct((b, SUBLANES, w2), F32)],
        scratch_shapes=[pltpu.VMEM((bb, tl + 2 * SUBLANES, w2), F32)],
        compiler_params=_cparams("parallel", "arbitrary"),
        name="mlstm",
    )(u3, c0p, n0p, m0b, cv0, cw.astype(F32), cb.astype(F32).reshape(1, w2), gbp,
      jnp.tile(ng.astype(F32).reshape(1, ML_DH), (1, 2)), _group_mean_matrix(LANES))
    c6 = c_new.reshape(b, pairs, 2, ML_DH, 2, ML_DH)
    c_out = jnp.stack([c6[:, :, h, :, h, :] for h in range(2)], axis=2).reshape(b, ML_HEADS, ML_DH, ML_DH)
    return (o, c_out, n_new.reshape(b, ML_HEADS, ML_DH), m_new[:, :, 0], cv[:, SUBLANES - (CONV_W - 1):, :])


def _rope_lanes(x, cos_t, sin_t):
    w = x.shape[1]
    half = ROT_DIM // 2
    reps = w // cos_t.shape[1]
    if reps > 1:
        cos_t = jnp.concatenate([cos_t] * reps, axis=1)
        sin_t = jnp.concatenate([sin_t] * reps, axis=1)
    lane = lax.broadcasted_iota(jnp.int32, x.shape, 1) % HEAD_DIM
    partner = jnp.where(lane < half, pltpu.roll(x, w - half, 1), pltpu.roll(x, half, 1))
    return x * cos_t + partner * sin_t


def _nsa_prep_kernel(u_ref, cos_ref, sin_ref, qg_ref, kg_ref, g4_ref, g1_ref, qn_ref, qr_ref, rows_ref, win_ref):
    u = u_ref[...]
    cos_t = cos_ref[...]
    sin_t = sin_ref[...]
    q = _group_norm(u[:, 0:NSA_WIDTH], g4_ref[...], qg_ref[...])
    qn_ref[...] = (q * ATTN_SCALE).astype(BF16)
    qr = _rope_lanes(q, cos_t, sin_t) * (ATTN_SCALE * LOG2E)
    low = lax.broadcasted_iota(jnp.int32, (qr.shape[0], LANES), 1) < HEAD_DIM
    for j in range(NSA_HEADS // 2):
        pair = qr[:, j * LANES:(j + 1) * LANES]
        swapped = pltpu.roll(pair, HEAD_DIM, 1)
        qr_ref[2 * j] = jnp.where(low, pair, swapped).astype(BF16)
        qr_ref[2 * j + 1] = jnp.where(low, swapped, pair).astype(BF16)
    kv = NSA_WIDTH
    k_slc = _rope_lanes(_group_norm(u[:, kv + 2 * KV_W:kv + 3 * KV_W], g1_ref[...], kg_ref[1:2, :]), cos_t, sin_t)
    k_win = _rope_lanes(_group_norm(u[:, kv + 4 * KV_W:kv + 5 * KV_W], g1_ref[...], kg_ref[2:3, :]), cos_t, sin_t)
    rows_ref[:, 0:2 * KV_W] = u[:, kv:kv + 2 * KV_W]
    rows_ref[:, 2 * KV_W:3 * KV_W] = k_slc
    rows_ref[:, 3 * KV_W:4 * KV_W] = u[:, kv + 3 * KV_W:kv + 4 * KV_W]
    win_ref[:, 0:KV_W] = k_win
    win_ref[:, KV_W:2 * KV_W] = u[:, kv + 5 * KV_W:kv + 6 * KV_W]


def _rope_tables(pos):
    half = ROT_DIM // 2
    inv = jnp.exp(-math.log(ROPE_THETA) * jnp.arange(half, dtype=F32) * 2.0 / ROT_DIM)
    ang = pos.astype(F32)[:, None] * inv[None, :]
    cos, sin = jnp.cos(ang), jnp.sin(ang)
    n = pos.shape[0]
    ones = jnp.ones((n, HEAD_DIM - ROT_DIM), F32)
    cos_h = jnp.concatenate([cos, cos, ones], axis=1)
    sin_h = jnp.concatenate([-sin, sin, 0.0 * ones], axis=1)
    reps = LANES // HEAD_DIM
    return jnp.tile(cos_h, (1, reps)), jnp.tile(sin_h, (1, reps))


def _nsa_prep(u_nsa, pos, q_g, k_g, l):
    rows = u_nsa.shape[0]
    tl = min(l, 256)
    nb = l // tl
    cos_t, sin_t = _rope_tables(pos)
    qg = jnp.tile(q_g.astype(F32).reshape(1, HEAD_DIM), (1, NSA_HEADS))
    kg = jnp.tile(k_g.astype(F32), (1, NSA_KV_HEADS))
    cst = lambda i: (0, 0)
    return pl.pallas_call(
        _nsa_prep_kernel,
        grid=(rows // tl,),
        in_specs=[pl.BlockSpec((tl, W_NSA), lambda i: (i, 0)),
                  pl.BlockSpec((tl, LANES), lambda i: (i % nb, 0)),
                  pl.BlockSpec((tl, LANES), lambda i: (i % nb, 0)),
                  pl.BlockSpec((1, NSA_WIDTH), cst),
                  pl.BlockSpec((3, KV_W), cst),
                  pl.BlockSpec((NSA_WIDTH, NSA_WIDTH), cst),
                  pl.BlockSpec((KV_W, KV_W), cst)],
        out_specs=[pl.BlockSpec((tl, NSA_WIDTH), lambda i: (i, 0)),
                   pl.BlockSpec((NSA_HEADS, tl, LANES), lambda i: (0, i, 0)),
                   pl.BlockSpec((tl, N_KV_SLOTS * KV_W), lambda i: (i, 0)),
                   pl.BlockSpec((tl, 2 * KV_W), lambda i: (i, 0))],
        out_shape=[jax.ShapeDtypeStruct((rows, NSA_WIDTH), BF16),
                   jax.ShapeDtypeStruct((NSA_HEADS, rows, LANES), BF16),
                   jax.ShapeDtypeStruct((rows, N_KV_SLOTS * KV_W), F32),
                   jax.ShapeDtypeStruct((rows, 2 * KV_W), F32)],
        compiler_params=_cparams("parallel"),
        name="nsa_prep",
    )(u_nsa, cos_t, sin_t, qg, kg, _group_mean_matrix(NSA_WIDTH), _group_mean_matrix(KV_W))


def _cmp_halves(xk_ref, xv_ref, w_ref, n_half):
    acc_k = jnp.zeros((n_half, 2 * KV_W), F32)
    acc_v = jnp.zeros((n_half, 2 * KV_W), F32)
    for s in range(CMP_STRIDE):
        acc_k = acc_k + _dot(xk_ref[pl.ds(s, n_half, stride=CMP_STRIDE), :], w_ref[0, s])
        acc_v = acc_v + _dot(xv_ref[pl.ds(s, n_half, stride=CMP_STRIDE), :], w_ref[1, s])
    return acc_k, acc_v


def _cmp_finish(acc_k, acc_v, bias_ref, kg_ref, g1_ref, kc_ref, vc_ref, n_half):
    valid = lax.broadcasted_iota(jnp.int32, (n_half, KV_W), 0) < n_half - 1

    def summary(acc, bias):
        return acc[:, 0:KV_W] + pltpu.roll(acc[:, KV_W:2 * KV_W], n_half - 1, 0) + bias

    kc = _group_norm(summary(acc_k, bias_ref[0:1, :]), g1_ref[...], kg_ref[...])
    kc_ref[0] = jnp.where(valid, kc, 0.0).astype(BF16)
    vc_ref[0] = jnp.where(valid, summary(acc_v, bias_ref[1:2, :]), 0.0).astype(BF16)


def _cmp_kv_kernel(xk_ref, xv_ref, w_ref, bias_ref, kg_ref, g1_ref, kc_ref, vc_ref, *, n_half):
    acc_k, acc_v = _cmp_halves(xk_ref.at[0], xv_ref.at[0], w_ref, n_half)
    _cmp_finish(acc_k, acc_v, bias_ref, kg_ref, g1_ref, kc_ref, vc_ref, n_half)


PAGES_PER_STEP = 16
CMP_PAGES_PER_STEP = 64


def _cmp_paged_kernel(pt_ref, *refs, npg, page, whole):
    page_refs = refs[:npg]
    if whole:
        w_ref, bias_ref, kg_ref, g1_ref, kc_ref, vc_ref, xk_sc, xv_sc = refs[npg:]
    else:
        w_ref, acck_ref, accv_ref, xk_sc, xv_sc = refs[npg:]
    for i in range(npg):
        xk_sc[i * page:(i + 1) * page, :] = page_refs[i][0, 0:KV_W, :].T
        xv_sc[i * page:(i + 1) * page, :] = page_refs[i][0, KV_W:2 * KV_W, :].T
    n_half = npg * page // CMP_STRIDE
    acc_k, acc_v = _cmp_halves(xk_sc, xv_sc, w_ref, n_half)
    if whole:
        _cmp_finish(acc_k, acc_v, bias_ref, kg_ref, g1_ref, kc_ref, vc_ref, n_half)
    else:
        acck_ref[0] = acc_k
        accv_ref[0] = acc_v


def _cmp_fin_kernel(acck_ref, accv_ref, bias_ref, kg_ref, g1_ref, kc_ref, vc_ref, *, n_half):
    _cmp_finish(acck_ref[0], accv_ref[0], bias_ref, kg_ref, g1_ref, kc_ref, vc_ref, n_half)


def _page_specs(npg, page, row_blk, pool_off):
    def spec(i):
        return pl.BlockSpec((1, 2 * KV_W, page), lambda bi, ji, pt: (pt[bi, ji * npg + i] + pool_off, row_blk, 0))
    return [spec(i) for i in range(npg)]


def _cmp_kv_paged(cache_t, page_table, pool_off, wcat, bias, kg0):
    b, n_pages = page_table.shape
    page = cache_t.shape[2]
    npg = math.gcd(n_pages, CMP_PAGES_PER_STEP)
    nh_step = npg * page // CMP_STRIDE
    n_half = n_pages * page // CMP_STRIDE
    kg = jnp.tile(kg0.astype(F32).reshape(1, HEAD_DIM), (1, NSA_KV_HEADS))
    if npg == n_pages:
        cst = lambda bi, ji, pt: (0, 0)
        return pl.pallas_call(
            functools.partial(_cmp_paged_kernel, npg=npg, page=page, whole=True),
            grid_spec=pltpu.PrefetchScalarGridSpec(
                num_scalar_prefetch=1, grid=(b, 1),
                in_specs=_page_specs(npg, page, 0, pool_off)
                + [pl.BlockSpec((2, CMP_STRIDE, KV_W, 2 * KV_W), lambda bi, ji, pt: (0, 0, 0, 0)),
                   pl.BlockSpec((2, KV_W), cst), pl.BlockSpec((1, KV_W), cst), pl.BlockSpec((KV_W, KV_W), cst)],
                out_specs=[pl.BlockSpec((1, n_half, KV_W), lambda bi, ji, pt: (bi, 0, 0)),
                           pl.BlockSpec((1, n_half, KV_W), lambda bi, ji, pt: (bi, 0, 0))],
                scratch_shapes=[pltpu.VMEM((npg * page, KV_W), F32), pltpu.VMEM((npg * page, KV_W), F32)]),
            out_shape=[jax.ShapeDtypeStruct((b, n_half, KV_W), BF16),
                       jax.ShapeDtypeStruct((b, n_half, KV_W), BF16)],
            compiler_params=_cparams("parallel", "arbitrary"),
            name="cmp_paged",
        )(page_table, *([cache_t] * npg), wcat, bias, kg, _group_mean_matrix(KV_W))
    kern = functools.partial(_cmp_paged_kernel, npg=npg, page=page, whole=False)
    acc_k, acc_v = pl.pallas_call(
        kern,
        grid_spec=pltpu.PrefetchScalarGridSpec(
            num_scalar_prefetch=1, grid=(b, n_pages // npg),
            in_specs=_page_specs(npg, page, 0, pool_off)
            + [pl.BlockSpec((2, CMP_STRIDE, KV_W, 2 * KV_W), lambda bi, ji, pt: (0, 0, 0, 0))],
            out_specs=[pl.BlockSpec((1, nh_step, 2 * KV_W), lambda bi, ji, pt: (bi, ji, 0)),
                       pl.BlockSpec((1, nh_step, 2 * KV_W), lambda bi, ji, pt: (bi, ji, 0))],
            scratch_shapes=[pltpu.VMEM((npg * page, KV_W), F32), pltpu.VMEM((npg * page, KV_W), F32)]),
        out_shape=[jax.ShapeDtypeStruct((b, n_half, 2 * KV_W), F32),
                   jax.ShapeDtypeStruct((b, n_half, 2 * KV_W), F32)],
        compiler_params=_cparams("parallel", "arbitrary"),
        name="cmp_paged",
    )(page_table, *([cache_t] * npg), wcat)
    blk = lambda bi: (bi, 0, 0)
    return pl.pallas_call(
        functools.partial(_cmp_fin_kernel, n_half=n_half),
        grid=(b,),
        in_specs=[pl.BlockSpec((1, n_half, 2 * KV_W), blk),
                  pl.BlockSpec((1, n_half, 2 * KV_W), blk),
                  pl.BlockSpec((2, KV_W), lambda bi: (0, 0)),
                  pl.BlockSpec((1, KV_W), lambda bi: (0, 0)),
                  pl.BlockSpec((KV_W, KV_W), lambda bi: (0, 0))],
        out_specs=[pl.BlockSpec((1, n_half, KV_W), blk), pl.BlockSpec((1, n_half, KV_W), blk)],
        out_shape=[jax.ShapeDtypeStruct((b, n_half, KV_W), BF16),
                   jax.ShapeDtypeStruct((b, n_half, KV_W), BF16)],
        compiler_params=_cparams("parallel"),
        name="cmp_fin",
    )(acc_k, acc_v, bias, kg, _group_mean_matrix(KV_W))


def _cmp_weights(cmp_pos, cmp_w):
    wf = cmp_w.astype(F32)
    eye_g = jnp.eye(NSA_KV_HEADS, dtype=F32)

    def bd(w):
        return jnp.einsum('ksde,gh->ksgdhe', w, eye_g).reshape(2, CMP_STRIDE, KV_W, KV_W)

    wcat = jnp.concatenate([bd(wf[:, :CMP_STRIDE]), bd(wf[:, CMP_STRIDE:])], axis=3).astype(BF16)
    bias = jnp.einsum('ksd,ksde->ke', cmp_pos.astype(F32), wf)
    return wcat, jnp.tile(bias, (1, NSA_KV_HEADS))


def _cmp_kv(x3, k_blk, v_blk, t_use, wcat, bias, kg0):
    b = x3.shape[0]
    n_half = t_use // CMP_STRIDE
    kern = functools.partial(_cmp_kv_kernel, n_half=n_half)
    kg = jnp.tile(kg0.astype(F32).reshape(1, HEAD_DIM), (1, NSA_KV_HEADS))
    return pl.pallas_call(
        kern,
        grid=(b,),
        in_specs=[pl.BlockSpec((1, t_use, KV_W), lambda bi: (bi, 0, k_blk)),
                  pl.BlockSpec((1, t_use, KV_W), lambda bi: (bi, 0, v_blk)),
                  pl.BlockSpec((2, CMP_STRIDE, KV_W, 2 * KV_W), lambda bi: (0, 0, 0, 0)),
                  pl.BlockSpec((2, KV_W), lambda bi: (0, 0)),
                  pl.BlockSpec((1, KV_W), lambda bi: (0, 0)),
                  pl.BlockSpec((KV_W, KV_W), lambda bi: (0, 0))],
        out_specs=[pl.BlockSpec((1, n_half, KV_W), lambda bi: (bi, 0, 0)),
                   pl.BlockSpec((1, n_half, KV_W), lambda bi: (bi, 0, 0))],
        out_shape=[jax.ShapeDtypeStruct((b, n_half, KV_W), BF16),
                   jax.ShapeDtypeStruct((b, n_half, KV_W), BF16)],
        compiler_params=_cparams("parallel"),
        name="cmp_kv",
    )(x3, x3, wcat, bias, kg, _group_mean_matrix(KV_W))


def _cmp_attn_kernel(qn_ref, kc_ref, vc_ref, ov_ref, o_ref, sel_ref, *, bb, tq, n_half, n_cmp, n_sel, nbp, pos0,
                     bias_out):
    biases = []
    qi = pl.program_id(1)
    nq = bb * tq
    pos_c = pos0 + qi * tq + lax.broadcasted_iota(jnp.int32, (tq, 1), 0)
    ncol = lax.broadcasted_iota(jnp.int32, (1, n_half), 1)
    cmask = (ncol * CMP_STRIDE + (CMP_BLOCK - 1) <= pos_c) & (ncol < n_cmp)
    pos_r = pos0 + qi * tq + lax.broadcasted_iota(jnp.int32, (1, nq), 1) % tq
    blk = lax.broadcasted_iota(jnp.int32, (nbp, 1), 0)
    cur = pos_r // SEL_BLOCK
    forced = (blk == 0) | (blk == cur) | (blk == cur - 1)
    valid = blk * SEL_BLOCK <= pos_r
    real = blk < n_sel
    for g in range(NSA_KV_HEADS):
        gs = slice(g * HEAD_DIM, (g + 1) * HEAD_DIM)
        psums = []
        for i in range(bb):
            kc = kc_ref[i, :, gs]
            vc = vc_ref[i, :, gs]
            psum = jnp.zeros((tq, n_half), F32)
            for h in range(NSA_HPG):
                hs = slice((g * NSA_HPG + h) * HEAD_DIM, (g * NSA_HPG + h + 1) * HEAD_DIM)
                s = _dot_nt(qn_ref[i, :, hs], kc)
                m = jnp.max(jnp.where(cmask, s, NEG_BIG), axis=-1, keepdims=True)
                m = jnp.where(m > 0.5 * NEG_BIG, m, 0.0)
                e = jnp.where(cmask, jnp.exp(s - m), 0.0)
                p = e / jnp.maximum(jnp.sum(e, axis=-1, keepdims=True), 1e-30)
                o_ref[i, :, hs] = _dot(p, vc)
                psum = psum + p
            psums.append(psum)
        psum = psums[0] if bb == 1 else jnp.concatenate(psums, axis=0)
        p_hi = psum.astype(BF16)
        p_lo = (psum - p_hi.astype(F32)).astype(BF16)
        ov = ov_ref[...]
        imp = _dot_nt(ov, p_hi) + _dot_nt(ov, p_lo)
        score = jnp.where(forced, 3e38, jnp.where(valid, imp, -1e38))
        score = jnp.where(real, score, -3e38)
        cnt = jnp.zeros((nbp, nq), F32)
        for jp in range(n_sel):
            rowv = score[jp:jp + 1, :]
            beats = (rowv > score) | ((rowv == score) & (blk > jp))
            cnt = cnt + jnp.where(beats, 1.0, 0.0)
        sel = (cnt < float(min(N_SELECT, n_sel))) & real
        if bias_out:
            sbg = jnp.where(sel, 0.0, NEG_BIG)
            if nbp < HEAD_DIM:
                sbg = jnp.concatenate([sbg, jnp.full((HEAD_DIM - nbp, nq), NEG_BIG, F32)], axis=0)
            biases.append(sbg)
        else:
            sel_ref[0, g * nbp:(g + 1) * nbp, :] = jnp.where(sel, 1.0, 0.0).astype(BF16)
    if bias_out:
        sel_ref[0] = jnp.concatenate(biases[::-1], axis=0).T.astype(BF16)


def _overlap_t(n_cmp, n_sel, n_half, nbp):
    a = SEL_BLOCK // CMP_STRIDE
    bb = CMP_BLOCK // CMP_STRIDE
    i = np.arange(n_half)[None, :]
    j = np.arange(nbp)[:, None]
    s = i - a * j + (bb - 1)
    cnt = np.maximum(np.minimum(np.minimum(s + 1, a + bb - 1 - s), min(a, bb)), 0)
    cnt = np.where((i < n_cmp) & (j < n_sel), cnt, 0)
    return jnp.asarray(cnt, dtype=BF16)


def _cmp_attn(qn3, kc, vc, pos0, t_len, bias_out=False):
    b, l, _ = qn3.shape
    n_half = kc.shape[1]
    n_cmp = n_half - 1
    n_sel = -(-t_len // SEL_BLOCK)
    nbp = _round_up(n_sel, 16)
    tq = min(l, LANES)
    bb = math.gcd(b, LANES // tq)
    kern = functools.partial(_cmp_attn_kernel, bb=bb, tq=tq, n_half=n_half, n_cmp=n_cmp, n_sel=n_sel, nbp=nbp,
                             pos0=pos0, bias_out=bias_out)
    if bias_out:
        assert bb == 1 and NSA_KV_HEADS == 2 and nbp <= HEAD_DIM
        sel_spec = pl.BlockSpec((1, tq, LANES), lambda bi, qi: (bi, qi, 0))
        sel_shape = jax.ShapeDtypeStruct((b, l, LANES), BF16)
    else:
        sel_spec = pl.BlockSpec((1, NSA_KV_HEADS * nbp, bb * tq), lambda bi, qi: (bi, 0, qi))
        sel_shape = jax.ShapeDtypeStruct((b // bb, NSA_KV_HEADS * nbp, bb * l), BF16)
    o_cmp, sel_t = pl.pallas_call(
        kern,
        grid=(b // bb, l // tq),
        in_specs=[pl.BlockSpec((bb, tq, NSA_WIDTH), lambda bi, qi: (bi, qi, 0)),
                  pl.BlockSpec((bb, n_half, KV_W), lambda bi, qi: (bi, 0, 0)),
                  pl.BlockSpec((bb, n_half, KV_W), lambda bi, qi: (bi, 0, 0)),
                  pl.BlockSpec((nbp, n_half), lambda bi, qi: (0, 0))],
        out_specs=[pl.BlockSpec((bb, tq, NSA_WIDTH), lambda bi, qi: (bi, qi, 0)), sel_spec],
        out_shape=[jax.ShapeDtypeStruct((b, l, NSA_WIDTH), F32), sel_shape],
        compiler_params=_cparams("parallel", "parallel"),
        name="cmp_attn",
    )(qn3, kc, vc, _overlap_t(n_cmp, n_sel, n_half, nbp))
    if bb > 1:
        sel_t = sel_t.reshape(b // bb, NSA_KV_HEADS * nbp, bb, l)
        sel_t = jnp.swapaxes(sel_t, 1, 2).reshape(b, NSA_KV_HEADS * nbp, l)
    return o_cmp, sel_t


SEL_TQ = 128
SEL_TK = 1024
WIN_TQ = 256


def _group_lanes(shape, g):
    lane = lax.broadcasted_iota(jnp.int32, shape, len(shape) - 1)
    return (lane < HEAD_DIM) if g == 0 else (lane >= HEAD_DIM)


def _store_heads(o_ref, r, g, tq):
    lo = g * HEAD_DIM
    for h in range(NSA_HPG):
        hh = g * NSA_HPG + h
        o_ref[0, :, hh * HEAD_DIM:(hh + 1) * HEAD_DIM] = r[h * tq:(h + 1) * tq, lo:lo + HEAD_DIM]


def _flash_sel_kernel(qi_ref, ki_ref, q_ref, k_ref, v_ref, sb_ref, o_ref, m_sc, acc_sc, *, tq, tk):
    qi = qi_ref[pl.program_id(1)]
    ki = ki_ref[pl.program_id(1)]
    kmax = (qi * tq + tq - 1) // tk
    hq = NSA_HPG * tq

    @pl.when(ki == 0)
    def _init():
        m_sc[...] = jnp.full(m_sc.shape, NEG_BIG, F32)
        acc_sc[...] = jnp.zeros(acc_sc.shape, F32)

    def step(diagonal):
        key_blk = ki * (tk // SEL_BLOCK) + lax.broadcasted_iota(jnp.int32, (tk, LANES), 0) // SEL_BLOCK
        one_hot = jnp.where(lax.broadcasted_iota(jnp.int32, (tk, LANES), 1) % HEAD_DIM == key_blk, 1.0, 0.0).astype(BF16)
        kblk = k_ref[0].astype(BF16)
        vblk = v_ref[0].astype(BF16)
        sb4 = jnp.concatenate([sb_ref[0]] * NSA_HPG, axis=0)
        if diagonal:
            qpos = qi * tq + lax.broadcasted_iota(jnp.int32, (tq, tk), 0)
            kpos = ki * tk + lax.broadcasted_iota(jnp.int32, (tq, tk), 1)
            causal4 = jnp.concatenate([jnp.where(kpos <= qpos, 0.0, NEG_BIG)] * NSA_HPG, axis=0)
        m_prev = [m_sc[g] for g in range(NSA_KV_HEADS)]
        acc_prev = [acc_sc[g] for g in range(NSA_KV_HEADS)]
        m_out, acc_out = [], []
        for g in range(NSA_KV_HEADS):
            own_k = _group_lanes((tk, LANES), g)
            q4 = q_ref[g * NSA_HPG:(g + 1) * NSA_HPG].reshape(hq, LANES)
            qx = jnp.where(_group_lanes((hq, LANES), g), q4, sb4)
            kx = jnp.where(own_k, kblk, one_hot)
            s = lax.dot_general(qx, kx, (((1,), (1,)), ((), ())), preferred_element_type=F32)
            if diagonal:
                s = s + causal4
            m_new = jnp.maximum(m_prev[g], jnp.max(s, axis=-1, keepdims=True))
            p = jnp.exp2(s - m_new[:, :1]).astype(BF16)
            vx = jnp.where(own_k, vblk, 1.0)
            acc_out.append(jnp.exp2(m_prev[g] - m_new) * acc_prev[g] + jnp.dot(p, vx, preferred_element_type=F32))
            m_out.append(m_new)
        for g in range(NSA_KV_HEADS):
            m_sc[g] = m_out[g]
            acc_sc[g] = acc_out[g]

    @pl.when(ki < kmax)
    def _full():
        step(False)

    @pl.when(ki == kmax)
    def _last():
        step(True)
        for g in range(NSA_KV_HEADS):
            acc = acc_sc[g]
            _store_heads(o_ref, acc / pltpu.roll(acc, HEAD_DIM, 1), g, tq)


def _flash_sel(q8, rows3, selb):
    b, l, _ = rows3.shape
    tq, tk = min(SEL_TQ, l), min(SEL_TK, l)
    nq = l // tq
    pairs = [(qi, ki) for qi in range(nq) for ki in range((qi * tq + tq - 1) // tk + 1)]
    qi_tab = jnp.asarray([p[0] for p in pairs], jnp.int32)
    ki_tab = jnp.asarray([p[1] for p in pairs], jnp.int32)
    kv_idx = lambda blk: (lambda bi, si, qt, kt: (bi, kt[si], blk))
    return pl.pallas_call(
        functools.partial(_flash_sel_kernel, tq=tq, tk=tk),
        grid_spec=pltpu.PrefetchScalarGridSpec(
            num_scalar_prefetch=2, grid=(b, len(pairs)),
            in_specs=[pl.BlockSpec((NSA_HEADS, tq, LANES), lambda bi, si, qt, kt: (0, bi * nq + qt[si], 0)),
                      pl.BlockSpec((1, tk, KV_W), kv_idx(2)),
                      pl.BlockSpec((1, tk, KV_W), kv_idx(3)),
                      pl.BlockSpec((1, tq, LANES), lambda bi, si, qt, kt: (bi, qt[si], 0))],
            out_specs=pl.BlockSpec((1, tq, NSA_WIDTH), lambda bi, si, qt, kt: (bi, qt[si], 0)),
            scratch_shapes=[pltpu.VMEM((NSA_KV_HEADS, NSA_HPG * tq, LANES), F32),
                            pltpu.VMEM((NSA_KV_HEADS, NSA_HPG * tq, LANES), F32)]),
        out_shape=jax.ShapeDtypeStruct((b, l, NSA_WIDTH), F32),
        compiler_params=_cparams("parallel", "arbitrary"),
        name="flash_sel",
    )(qi_tab, ki_tab, q8, rows3, rows3, selb)


def _flash_win_kernel(q_ref, *refs, tq, back):
    nspan = back + 1
    k_refs, v_refs, o_ref = refs[:nspan], refs[nspan:2 * nspan], refs[2 * nspan]
    qi = pl.program_id(1)
    hq = NSA_HPG * tq
    span = nspan * tq
    qpos = qi * tq + lax.broadcasted_iota(jnp.int32, (tq, span), 0)
    kpos = (qi - back) * tq + lax.broadcasted_iota(jnp.int32, (tq, span), 1)
    ok = (kpos <= qpos) & (kpos > qpos - WINDOW) & (kpos >= 0)
    bias4 = jnp.concatenate([jnp.where(ok, 0.0, NEG_BIG)] * NSA_HPG, axis=0)
    kcat = jnp.concatenate([r[0] for r in k_refs], axis=0).astype(BF16)
    vcat = jnp.concatenate([r[0] for r in v_refs], axis=0).astype(BF16)
    for g in range(NSA_KV_HEADS):
        q4 = q_ref[g * NSA_HPG:(g + 1) * NSA_HPG].reshape(hq, LANES)
        qx = jnp.where(_group_lanes((hq, LANES), g), q4, 0.0)
        s = lax.dot_general(qx, kcat, (((1,), (1,)), ((), ())), preferred_element_type=F32) + bias4
        p = jnp.exp2(s - jnp.max(s, axis=-1, keepdims=True)).astype(BF16)
        vx = jnp.where(_group_lanes((span, LANES), g), vcat, 1.0)
        r = jnp.dot(p, vx, preferred_element_type=F32)
        _store_heads(o_ref, r / pltpu.roll(r, HEAD_DIM, 1), g, tq)


def _flash_win(q8, win3):
    b, l, _ = win3.shape
    tq = min(WIN_TQ, l)
    nq = l // tq
    back = -(-(WINDOW - 1) // tq)

    def kv_specs(blk):
        return [pl.BlockSpec((1, tq, KV_W), (lambda bi, qi, j=j: (bi, jnp.maximum(qi - back + j, 0), blk)))
                for j in range(back + 1)]

    return pl.pallas_call(
        functools.partial(_flash_win_kernel, tq=tq, back=back),
        grid=(b, nq),
        in_specs=[pl.BlockSpec((NSA_HEADS, tq, LANES), lambda bi, qi: (0, bi * nq + qi, 0))]
        + kv_specs(0) + kv_specs(1),
        out_specs=pl.BlockSpec((1, tq, NSA_WIDTH), lambda bi, qi: (bi, qi, 0)),
        out_shape=jax.ShapeDtypeStruct((b, l, NSA_WIDTH), F32),
        compiler_params=_cparams("parallel", "parallel"),
        name="flash_win",
    )(q8, *([win3] * (2 * (back + 1))))


def _softmax_update(s, m_prev, l_prev):
    m_new = jnp.maximum(m_prev, jnp.max(s, axis=-1, keepdims=True))
    alpha = jnp.exp2(m_prev - m_new)
    p = jnp.exp2(s - m_new[:, :1])
    return m_new, alpha, alpha * l_prev + jnp.sum(p, axis=-1, keepdims=True), p


def _paged_attn_kernel(pt_ref, *refs, npg, page, l_new, nbp):
    page_refs = refs[:npg]
    (qz_ref, sel_ref, kn_ref, vn_ref, win_ref, kwn_ref, vwn_ref,
     oslc_ref, owin_ref, wout_ref, m_sc, l_sc, acc_sc) = refs[npg:]
    ji = pl.program_id(1)
    nr = qz_ref.shape[1]
    span = npg * page

    @pl.when(ji == 0)
    def _init():
        m_sc[...] = jnp.full(m_sc.shape, NEG_BIG, F32)
        l_sc[...] = jnp.zeros(l_sc.shape, F32)
        acc_sc[...] = jnp.zeros(acc_sc.shape, F32)

    qz = qz_ref[0]
    kt = jnp.concatenate([page_refs[i][0, 0:KV_W, :] for i in range(npg)], axis=1).astype(BF16)
    vt = jnp.concatenate([page_refs[i][0, KV_W:2 * KV_W, :] for i in range(npg)], axis=1).astype(BF16)
    s = jnp.dot(qz, kt, preferred_element_type=F32)
    e_row = lax.broadcasted_iota(jnp.int32, (nbp, span), 0)
    e_col = lax.broadcasted_iota(jnp.int32, (nbp, span), 1)
    expand = jnp.where(e_row == ji * (span // SEL_BLOCK) + e_col // SEL_BLOCK, 1.0, 0.0).astype(BF16)
    picked = jnp.dot(sel_ref[0], expand, preferred_element_type=F32) > 0.5
    m_new, alpha, l_new_v, p = _softmax_update(jnp.where(picked, s, NEG_BIG), m_sc[...], l_sc[...])
    m_sc[...] = m_new
    l_sc[...] = l_new_v
    acc_sc[...] = alpha * acc_sc[...] + _dot_nt(p, vt)

    @pl.when(ji == pl.num_programs(1) - 1)
    def _fin():
        q_of_row = lax.broadcasted_iota(jnp.int32, (nr, l_new), 0) % l_new
        j_new = lax.broadcasted_iota(jnp.int32, (nr, l_new), 1)
        new_ok = j_new <= q_of_row
        sn = jnp.where(new_ok, _dot_nt(qz, kn_ref[0]), NEG_BIG)
        m2, a2, l2, p2 = _softmax_update(sn, m_sc[...], l_sc[...])
        oslc_ref[0] = (a2 * acc_sc[...] + _dot(p2, vn_ref[0])) / l2
        wlen = win_ref.shape[2]
        kw = win_ref[0, 0:KV_W, :]
        vw = win_ref[0, KV_W:2 * KV_W, :]
        i_old = lax.broadcasted_iota(jnp.int32, (nr, wlen), 1)
        q_old = lax.broadcasted_iota(jnp.int32, (nr, wlen), 0) % l_new
        sw = jnp.where(i_old + (WINDOW - wlen) > q_old, _dot(qz, kw), NEG_BIG)
        swn = jnp.where(new_ok, _dot_nt(qz, kwn_ref[0]), NEG_BIG)
        mw = jnp.maximum(jnp.max(sw, axis=-1, keepdims=True), jnp.max(swn, axis=-1, keepdims=True))
        pw = jnp.exp2(sw - mw)
        pwn = jnp.exp2(swn - mw)
        lw = jnp.sum(pw, axis=-1, keepdims=True) + jnp.sum(pwn, axis=-1, keepdims=True)
        owin_ref[0] = (_dot_nt(pw, vw) + _dot(pwn, vwn_ref[0])) / lw
        new_t = jnp.concatenate([kwn_ref[0], vwn_ref[0]], axis=1).T
        place = (lax.broadcasted_iota(jnp.int32, (l_new, wlen), 1)
                 == lax.broadcasted_iota(jnp.int32, (l_new, wlen), 0) + (wlen - l_new)).astype(F32)
        lane = lax.broadcasted_iota(jnp.int32, (2 * KV_W, wlen), 1)
        wout_ref[0] = jnp.where(lane < wlen - l_new, pltpu.roll(win_ref[0], wlen - l_new, 1), _dot_f32(new_t, place))


def _paged_attn(cache_t, page_table, pool_off, qz, sel_rows, rows3, win_t, win_off, win3):
    b, n_pages = page_table.shape
    page = cache_t.shape[2]
    npg = PAGES_PER_STEP
    nr = qz.shape[1]
    l_new = rows3.shape[1]
    nbp = sel_rows.shape[2]
    wlen = win_t.shape[2]
    kern = functools.partial(_paged_attn_kernel, npg=npg, page=page, l_new=l_new, nbp=nbp)
    per_b = lambda bi, ji, pt: (bi, 0, 0)
    return pl.pallas_call(
        kern,
        grid_spec=pltpu.PrefetchScalarGridSpec(
            num_scalar_prefetch=1, grid=(b, n_pages // npg),
            in_specs=_page_specs(npg, page, 1, pool_off)
            + [pl.BlockSpec((1, nr, KV_W), per_b),
               pl.BlockSpec((1, nr, nbp), per_b),
               pl.BlockSpec((1, l_new, KV_W), lambda bi, ji, pt: (bi, 0, 2)),
               pl.BlockSpec((1, l_new, KV_W), lambda bi, ji, pt: (bi, 0, 3)),
               pl.BlockSpec((1, 2 * KV_W, wlen), lambda bi, ji, pt: (bi + win_off, 0, 0)),
               pl.BlockSpec((1, l_new, KV_W), lambda bi, ji, pt: (bi, 0, 0)),
               pl.BlockSpec((1, l_new, KV_W), lambda bi, ji, pt: (bi, 0, 1))],
            out_specs=[pl.BlockSpec((1, nr, KV_W), per_b),
                       pl.BlockSpec((1, nr, KV_W), per_b),
                       pl.BlockSpec((1, 2 * KV_W, wlen), per_b)],
            scratch_shapes=[pltpu.VMEM((nr, LANES), F32), pltpu.VMEM((nr, LANES), F32),
                            pltpu.VMEM((nr, KV_W), F32)]),
        out_shape=[jax.ShapeDtypeStruct((b, nr, KV_W), F32),
                   jax.ShapeDtypeStruct((b, nr, KV_W), F32),
                   jax.ShapeDtypeStruct((b, 2 * KV_W, wlen), F32)],
        compiler_params=_cparams("parallel", "arbitrary"),
        name="paged_attn",
    )(page_table, *([cache_t] * npg), qz, sel_rows, rows3, rows3, win_t, win3, win3)


def _nsa_fresh(u_nsa, b, l, win_keep, q_g, k_g, cmp_pos, cmp_w):
    qn, qr, rows, win = _nsa_prep(u_nsa, jnp.arange(l), q_g, k_g, l)
    rows3 = rows.reshape(b, l, N_KV_SLOTS * KV_W)
    win3 = win.reshape(b, l, 2 * KV_W)
    wcat, bias = _cmp_weights(cmp_pos, cmp_w)
    t_use = (l // CMP_STRIDE) * CMP_STRIDE
    kc, vc = _cmp_kv(rows3, 0, 1, t_use, wcat, bias, k_g[0])
    o_cmp, selb = _cmp_attn(qn.reshape(b, l, NSA_WIDTH), kc, vc, 0, l, bias_out=True)
    o_slc = _flash_sel(qr, rows3, selb)
    o_win = _flash_win(qr, win3)
    branches = tuple(o.reshape(b * l, NSA_WIDTH) for o in (o_cmp, o_slc, o_win))
    new_rows = rows3.reshape(b, l, N_KV_SLOTS, NSA_KV_HEADS, HEAD_DIM)
    new_win = win3[:, l - win_keep:].reshape(b, win_keep, 2, NSA_KV_HEADS, HEAD_DIM)
    return branches, new_rows, new_win


def _nsa_paged(u_nsa, b, l, paged, q_g, k_g, cmp_pos, cmp_w):
    cache_t, page_table, pool_off, win_t, win_off = paged
    past_len = page_table.shape[1] * cache_t.shape[2]
    assert (past_len + l) // CMP_STRIDE == past_len // CMP_STRIDE and past_len % SEL_BLOCK == 0
    qn, qr, rows, win = _nsa_prep(u_nsa, past_len + jnp.arange(l), q_g, k_g, l)
    rows3 = rows.reshape(b, l, N_KV_SLOTS * KV_W)
    win3 = win.reshape(b, l, 2 * KV_W)
    wcat, bias = _cmp_weights(cmp_pos, cmp_w)
    kc, vc = _cmp_kv_paged(cache_t, page_table, pool_off, wcat, bias, k_g[0])
    o_cmp, sel_t = _cmp_attn(qn.reshape(b, l, NSA_WIDTH), kc, vc, past_len, past_len + l)
    nbp = sel_t.shape[1] // NSA_KV_HEADS
    sel_rows = jnp.swapaxes(sel_t.reshape(b, NSA_KV_HEADS, 1, nbp, l), 3, 4)
    sel_rows = jnp.broadcast_to(sel_rows, (b, NSA_KV_HEADS, NSA_HPG, l, nbp)).reshape(b, NSA_HEADS * l, nbp)
    q5 = jnp.transpose(qr[:, :, :HEAD_DIM].reshape(NSA_KV_HEADS, NSA_HPG, b, l, HEAD_DIM), (2, 0, 1, 3, 4))
    qz = jnp.einsum('bghqd,gk->bghqkd', q5, jnp.eye(NSA_KV_HEADS, dtype=q5.dtype)).reshape(b, NSA_HEADS * l, KV_W)
    o_slc_z, o_win_z, wout = _paged_attn(cache_t, page_table, pool_off, qz, sel_rows, rows3, win_t, win_off, win3)

    def own_group(o):
        o6 = o.reshape(b, NSA_KV_HEADS, NSA_HPG, l, NSA_KV_HEADS, HEAD_DIM)
        d = jnp.stack([o6[:, g, :, :, g, :] for g in range(NSA_KV_HEADS)], axis=1)
        return jnp.transpose(d, (0, 3, 1, 2, 4)).reshape(b * l, NSA_WIDTH)

    branches = (o_cmp.reshape(b * l, NSA_WIDTH), own_group(o_slc_z), own_group(o_win_z))
    new_rows = rows3.reshape(b, l, N_KV_SLOTS, NSA_KV_HEADS, HEAD_DIM)
    wlen = wout.shape[2]
    new_win = jnp.transpose(wout.reshape(b, 2, NSA_KV_HEADS, HEAD_DIM, wlen), (0, 4, 1, 2, 3))
    return branches, new_rows, new_win


def _hybrid_layer(x, paged, win_keep, s_gla, c_ml, n_ml, m_ml, conv_ml,
                  norm_g, w_in_pad, w_out_bf, gla_w_gate, gla_b_gate, gla_norm_g,
                  nsa_q_norm_g, nsa_k_norm_g, nsa_cmp_pos, nsa_cmp_w,
                  ml_conv_w, ml_conv_b, ml_gate_b, ml_norm_g):
    b, l, _ = x.shape
    rows = b * l
    tm = 256
    x2d = x.reshape(rows, D_MODEL)
    u_gla, u_nsa, u_ml = _proj_in(x2d, norm_g, w_in_pad, tm)

    nk = GLA_HEADS * GLA_DK
    wg = jnp.pad(gla_w_gate.astype(F32), ((0, LANES - GLA_RANK), (0, 0))).astype(BF16)
    o_a, st_new = _gla(u_gla, _gla_state_in(s_gla), wg, gla_b_gate.astype(F32).reshape(1, nk),
                       jnp.tile(gla_norm_g.astype(F32).reshape(1, GLA_DV), (1, GLA_HEADS)), b, l)
    s_new = _gla_state_out(st_new)

    if paged is None:
        o_nsa, new_rows, new_win = _nsa_fresh(u_nsa, b, l, win_keep, nsa_q_norm_g, nsa_k_norm_g,
                                              nsa_cmp_pos, nsa_cmp_w)
    else:
        o_nsa, new_rows, new_win = _nsa_paged(u_nsa, b, l, paged, nsa_q_norm_g, nsa_k_norm_g,
                                              nsa_cmp_pos, nsa_cmp_w)

    o_c, c_new, n_new, m_new, conv_new = _mlstm(u_ml, c_ml, n_ml, m_ml, conv_ml, ml_conv_w, ml_conv_b,
                                                ml_gate_b, ml_norm_g, b, l)

    y = _proj_out(o_a.reshape(rows, GLA_WIDTH), *o_nsa, u_nsa, o_c.reshape(rows, ML_WIDTH), x2d, w_out_bf, tm)
    return y.reshape(b, l, D_MODEL), new_rows, new_win, s_new, c_new, n_new, m_new, conv_new


def kernel(x_prompt, x_sample, cache_nsa_kv, state_nsa_win, state_gla, state_mlstm_C, state_mlstm_n,
           state_mlstm_m, state_mlstm_conv, page_table, norm_g, w_in, w_out, gla_w_gate, gla_b_gate,
           gla_norm_g, nsa_q_norm_g, nsa_k_norm_g, nsa_cmp_pos, nsa_cmp_w, ml_conv_w, ml_conv_b,
           ml_gate_b, ml_norm_g):
    bp, sp, _ = x_prompt.shape
    bs, _, _ = x_sample.shape
    depth = w_in.shape[0]
    dt = x_prompt.dtype
    zero_gla = jnp.zeros((bp, GLA_HEADS, GLA_DK, GLA_DV), F32)
    zero_c = jnp.zeros((bp, ML_HEADS, ML_DH, ML_DH), F32)
    zero_n = jnp.zeros((bp, ML_HEADS, ML_DH), F32)
    zero_m = jnp.zeros((bp, ML_HEADS), F32)
    zero_conv = jnp.zeros((bp, CONV_W - 1, 2 * ML_WIDTH), dt)
    keep_p = min(WINDOW, sp)
    keep_s = state_nsa_win.shape[2]
    n_pool = cache_nsa_kv.shape[1]
    cache_t = jnp.transpose(cache_nsa_kv, (0, 1, 3, 4, 5, 2)).reshape(
        depth * n_pool, N_KV_SLOTS * KV_W, cache_nsa_kv.shape[2]).astype(F32)
    win_t = jnp.transpose(state_nsa_win, (0, 1, 3, 4, 5, 2)).reshape(depth * bs, 2 * KV_W, keep_s).astype(F32)

    y_prompt, y_sample = x_prompt, x_sample
    p_layers, s_layers = [], []
    for layer in range(depth):
        w = (norm_g[layer], _pad_w_in(w_in[layer]), w_out[layer].astype(BF16), gla_w_gate[layer],
             gla_b_gate[layer], gla_norm_g[layer],
             nsa_q_norm_g[layer], nsa_k_norm_g[layer], nsa_cmp_pos[layer], nsa_cmp_w[layer],
             ml_conv_w[layer], ml_conv_b[layer], ml_gate_b[layer], ml_norm_g[layer])
        y_prompt, *p_new = _hybrid_layer(y_prompt, None, keep_p, zero_gla, zero_c, zero_n,
                                         zero_m, zero_conv, *w)
        paged = (cache_t, page_table, layer * n_pool, win_t, layer * bs)
        y_sample, *s_new = _hybrid_layer(y_sample, paged, keep_s,
                                         state_gla[layer], state_mlstm_C[layer], state_mlstm_n[layer],
                                         state_mlstm_m[layer], state_mlstm_conv[layer], *w)
        p_layers.append(p_new)
        s_layers.append(s_new)
    p_kv, p_win, p_gla, p_c, p_n, p_m, p_conv = [jnp.stack(z) for z in zip(*p_layers)]
    s_kv, s_win, s_gla, s_c, s_n, s_m, s_conv = [jnp.stack(z) for z in zip(*s_layers)]
    return (y_prompt, y_sample, p_kv, s_kv, p_win, s_win, p_gla, s_gla, p_c, s_c, p_n, s_n, p_m, s_m, p_conv, s_conv)
```

```python
import functools
import math

import jax
import jax.numpy as jnp
import numpy as np
from jax import lax
from jax.experimental import pallas as pl
from jax.experimental.pallas import tpu as pltpu

F32 = jnp.float32
BF16 = jnp.bfloat16
HIGHEST = lax.Precision.HIGHEST

D_MODEL = 1024
HEAD_DIM = 64
GLA_WIDTH = D_MODEL // 4
NSA_WIDTH = D_MODEL // 2
ML_WIDTH = D_MODEL - GLA_WIDTH - NSA_WIDTH
D_MIX = GLA_WIDTH + NSA_WIDTH + ML_WIDTH

GLA_HEADS = GLA_WIDTH // HEAD_DIM
GLA_DK = HEAD_DIM // 2
GLA_DV = HEAD_DIM
GLA_RANK = 16
GLA_TAU = 16.0
GLA_CHUNK = 64

NSA_HEADS = NSA_WIDTH // HEAD_DIM
NSA_KV_HEADS = 2
NSA_HPG = NSA_HEADS // NSA_KV_HEADS
CMP_BLOCK = 32
CMP_STRIDE = 16
SEL_BLOCK = 64
N_SELECT = 16
WINDOW = 512
Q_BLOCK = 128
N_KV_SLOTS = 4
ROT_DIM = HEAD_DIM // 4
ROPE_THETA = 500000.0
ATTN_SCALE = HEAD_DIM ** -0.5

ML_HEADS = ML_WIDTH // HEAD_DIM
ML_DH = HEAD_DIM
ML_CHUNK = 64
CONV_W = 4

SPLIT_SIZES = (GLA_HEADS * GLA_DK, GLA_HEADS * GLA_DK, GLA_WIDTH, GLA_RANK, GLA_WIDTH,
               NSA_WIDTH, 6 * NSA_KV_HEADS * HEAD_DIM, 3 * NSA_HEADS, NSA_WIDTH,
               2 * ML_WIDTH, ML_WIDTH, 2 * ML_HEADS, ML_WIDTH, ML_WIDTH)

LANES = 128
SUBLANES = 8
VMEM_LIMIT = 56 * 1024 * 1024
NEG_BIG = -1e30
EPS = 1e-6
LOG2E = math.log2(math.e)


def _round_up(n, m):
    return -(-n // m) * m


PAD_SIZES = tuple(_round_up(s, LANES) for s in SPLIT_SIZES)
D_IN_PAD = sum(PAD_SIZES)
W_GLA = sum(PAD_SIZES[0:5])
W_NSA = sum(PAD_SIZES[5:9])
W_ML = sum(PAD_SIZES[9:14])
KV_W = NSA_KV_HEADS * HEAD_DIM


def _dot(a, b):
    return jnp.dot(a.astype(BF16), b.astype(BF16), preferred_element_type=F32)


def _dot_nt(a, b):
    return lax.dot_general(a.astype(BF16), b.astype(BF16), (((1,), (1,)), ((), ())), preferred_element_type=F32)


def _dot_tn(a, b):
    return lax.dot_general(a.astype(BF16), b.astype(BF16), (((0,), (0,)), ((), ())), preferred_element_type=F32)


def _dot_f32(a, b):
    return jnp.dot(a, b, precision=HIGHEST, preferred_element_type=F32)


def _dot_split(a, b):
    a_hi = a.astype(BF16)
    a_lo = (a - a_hi.astype(F32)).astype(BF16)
    bb = b.astype(BF16)
    return jnp.dot(a_hi, bb, preferred_element_type=F32) + jnp.dot(a_lo, bb, preferred_element_type=F32)


def _log_sigmoid(x):
    return jnp.minimum(x, 0.0) - jnp.log1p(jnp.exp(-jnp.abs(x)))


def _sigmoid(x):
    return 1.0 / (1.0 + jnp.exp(-x))


def _silu(x):
    return x * _sigmoid(x)


def _group_mean_matrix(width):
    g = np.kron(np.eye(width // HEAD_DIM, dtype=np.float32), np.full((HEAD_DIM, HEAD_DIM), 1.0 / HEAD_DIM, np.float32))
    return jnp.asarray(g, dtype=BF16)


def _group_norm(x, gmat, gain):
    ms = _dot_split(x * x, gmat)
    return x * lax.rsqrt(ms + EPS) * gain


def _cparams(*sem):
    return pltpu.CompilerParams(dimension_semantics=sem, vmem_limit_bytes=VMEM_LIMIT)


def _proj_in_kernel(x_ref, g_ref, w_ref, ug_ref, un_ref, um_ref):
    x = x_ref[...]
    y = x * lax.rsqrt(jnp.mean(x * x, axis=-1, keepdims=True) + EPS) * g_ref[...]
    r = jnp.dot(y.astype(BF16), w_ref[...], preferred_element_type=F32)
    ug_ref[...] = r[:, 0:W_GLA]
    un_ref[...] = r[:, W_GLA:W_GLA + W_NSA]
    um_ref[...] = r[:, W_GLA + W_NSA:D_IN_PAD]


def _proj_in(x2d, g, w_pad, tm):
    rows = x2d.shape[0]
    return pl.pallas_call(
        _proj_in_kernel,
        grid=(rows // tm,),
        in_specs=[pl.BlockSpec((tm, D_MODEL), lambda i: (i, 0)),
                  pl.BlockSpec((1, D_MODEL), lambda i: (0, 0)),
                  pl.BlockSpec((D_MODEL, D_IN_PAD), lambda i: (0, 0))],
        out_specs=[pl.BlockSpec((tm, W_GLA), lambda i: (i, 0)),
                   pl.BlockSpec((tm, W_NSA), lambda i: (i, 0)),
                   pl.BlockSpec((tm, W_ML), lambda i: (i, 0))],
        out_shape=[jax.ShapeDtypeStruct((rows, W_GLA), F32),
                   jax.ShapeDtypeStruct((rows, W_NSA), F32),
                   jax.ShapeDtypeStruct((rows, W_ML), F32)],
        compiler_params=_cparams("parallel"),
        name="proj_in",
    )(x2d, g.reshape(1, D_MODEL), w_pad)


def _pad_w_in(w_in):
    parts = []
    off = 0
    for s, p in zip(SPLIT_SIZES, PAD_SIZES):
        seg = w_in[:, off:off + s]
        if p != s:
            seg = jnp.pad(seg, ((0, 0), (0, p - s)))
        parts.append(seg)
        off += s
    return jnp.concatenate(parts, axis=1).astype(BF16)


def _proj_out_kernel(oa_ref, ocmp_ref, oslc_ref, owin_ref, gz_ref, oc_ref, x_ref, w_ref, e_ref, y_ref):
    gate = _sigmoid(gz_ref[:, 0:LANES])
    ob = (_dot_split(gate, e_ref[0]) * ocmp_ref[...] + _dot_split(gate, e_ref[1]) * oslc_ref[...]
          + _dot_split(gate, e_ref[2]) * owin_ref[...]) * _silu(gz_ref[:, LANES:LANES + NSA_WIDTH])
    y = x_ref[...]
    y = y + _dot(oa_ref[...], w_ref[0:GLA_WIDTH, :])
    y = y + _dot(ob, w_ref[GLA_WIDTH:GLA_WIDTH + NSA_WIDTH, :])
    y = y + _dot(oc_ref[...], w_ref[GLA_WIDTH + NSA_WIDTH:D_MIX, :])
    y_ref[...] = y


def _gate_expand():
    e = np.zeros((3, LANES, NSA_WIDTH), np.float32)
    for j in range(3):
        for h in range(NSA_HEADS):
            e[j, j * NSA_HEADS + h, h * HEAD_DIM:(h + 1) * HEAD_DIM] = 1.0
    return jnp.asarray(e, dtype=BF16)


def _proj_out(oa, o_cmp, o_slc, o_win, u_nsa, oc, x2d, w_bf, tm):
    rows = x2d.shape[0]
    gz_w = LANES + NSA_WIDTH
    gz_blk = (NSA_WIDTH + 6 * KV_W) // gz_w
    assert gz_blk * gz_w == NSA_WIDTH + 6 * KV_W
    row = lambda i: (i, 0)
    return pl.pallas_call(
        _proj_out_kernel,
        grid=(rows // tm,),
        in_specs=[pl.BlockSpec((tm, GLA_WIDTH), row),
                  pl.BlockSpec((tm, NSA_WIDTH), row),
                  pl.BlockSpec((tm, NSA_WIDTH), row),
                  pl.BlockSpec((tm, NSA_WIDTH), row),
                  pl.BlockSpec((tm, gz_w), lambda i: (i, gz_blk)),
                  pl.BlockSpec((tm, ML_WIDTH), row),
                  pl.BlockSpec((tm, D_MODEL), row),
                  pl.BlockSpec((D_MIX, D_MODEL), lambda i: (0, 0)),
                  pl.BlockSpec((3, LANES, NSA_WIDTH), lambda i: (0, 0, 0))],
        out_specs=pl.BlockSpec((tm, D_MODEL), row),
        out_shape=jax.ShapeDtypeStruct((rows, D_MODEL), F32),
        compiler_params=_cparams("parallel"),
        name="proj_out",
    )(oa, o_cmp, o_slc, o_win, u_nsa, oc, x2d, w_bf, _gate_expand())


def _gla_kernel(u_ref, st0_ref, wg_ref, bg_ref, ng_ref, gm_ref, o_ref, st_ref, *, bb, tl, c):
    @pl.when(pl.program_id(1) == 0)
    def _init():
        st_ref[...] = st0_ref[...]

    nk = GLA_HEADS * GLA_DK
    a_off = 2 * nk + GLA_WIDTH
    q, k, v, log_a = [], [], [], []
    for i in range(bb):
        q.append(u_ref[i, :, 0:nk] * (GLA_DK ** -0.5))
        k.append(u_ref[i, :, nk:2 * nk])
        v.append(u_ref[i, :, 2 * nk:2 * nk + GLA_WIDTH])
        pre = _dot(u_ref[i, :, a_off:a_off + LANES], wg_ref[...]) + bg_ref[...]
        log_a.append(_log_sigmoid(pre) * (1.0 / GLA_TAU))

    tril = (lax.broadcasted_iota(jnp.int32, (c, c), 0) >= lax.broadcasted_iota(jnp.int32, (c, c), 1)).astype(F32)
    hc = GLA_HEADS * c
    tri_h = (lax.broadcasted_iota(jnp.int32, (hc, c), 0) % c) >= lax.broadcasted_iota(jnp.int32, (hc, c), 1)
    k_head = lax.broadcasted_iota(jnp.int32, (1, nk), 1) // GLA_DK
    v_head = lax.broadcasted_iota(jnp.int32, (1, GLA_WIDTH), 1) // GLA_DV
    st_diag = (lax.broadcasted_iota(jnp.int32, (GLA_WIDTH, nk), 0) // GLA_DV
               == lax.broadcasted_iota(jnp.int32, (GLA_WIDTH, nk), 1) // GLA_DK)

    st = [st_ref[i] for i in range(bb)]
    outs = [[] for _ in range(bb)]
    for j in range(tl // c):
        sl = slice(j * c, (j + 1) * c)
        for i in range(bb):
            b = _dot_f32(tril, log_a[i][sl])
            blast = b[c - 1:c]
            qe = q[i][sl] * jnp.exp(b)
            ke = k[i][sl] * jnp.exp(-b)
            kl = k[i][sl] * jnp.exp(blast - b)
            vc = v[i][sl]
            qx = jnp.concatenate([jnp.where(k_head == h, qe, 0.0) for h in range(GLA_HEADS)], axis=0)
            a = jnp.where(tri_h, _dot_nt(qx, ke), 0.0)
            r = _dot(a, vc)
            o = _dot_nt(qe, st[i])
            for h in range(GLA_HEADS):
                o = o + jnp.where(v_head == h, r[h * c:(h + 1) * c], 0.0)
            st[i] = st[i] * jnp.exp(blast) + jnp.where(st_diag, _dot_tn(vc, kl), 0.0)
            outs[i].append(o)
    for i in range(bb):
        st_ref[i] = st[i]
        o = outs[i][0] if len(outs[i]) == 1 else jnp.concatenate(outs[i], axis=0)
        z = u_ref[i, :, a_off + LANES:a_off + LANES + GLA_WIDTH]
        o_ref[i] = _group_norm(o, gm_ref[...], ng_ref[...]) * _silu(z)


GLA_SEQS_PER_STEP = 4


def _gla(u_gla, st0, wg, bg, ng, b, l):
    tl = min(l, 128)
    c = min(l, 16)
    bb = math.gcd(b, GLA_SEQS_PER_STEP)
    nk = GLA_HEADS * GLA_DK
    u3 = u_gla.reshape(b, l, W_GLA)
    kern = functools.partial(_gla_kernel, bb=bb, tl=tl, c=c)
    return pl.pallas_call(
        kern,
        grid=(b // bb, l // tl),
        in_specs=[pl.BlockSpec((bb, tl, W_GLA), lambda bi, li: (bi, li, 0)),
                  pl.BlockSpec((bb, GLA_WIDTH, nk), lambda bi, li: (bi, 0, 0)),
                  pl.BlockSpec((LANES, nk), lambda bi, li: (0, 0)),
                  pl.BlockSpec((1, nk), lambda bi, li: (0, 0)),
                  pl.BlockSpec((1, GLA_WIDTH), lambda bi, li: (0, 0)),
                  pl.BlockSpec((GLA_WIDTH, GLA_WIDTH), lambda bi, li: (0, 0))],
        out_specs=[pl.BlockSpec((bb, tl, GLA_WIDTH), lambda bi, li: (bi, li, 0)),
                   pl.BlockSpec((bb, GLA_WIDTH, nk), lambda bi, li: (bi, 0, 0))],
        out_shape=[jax.ShapeDtypeStruct((b, l, GLA_WIDTH), F32),
                   jax.ShapeDtypeStruct((b, GLA_WIDTH, nk), F32)],
        compiler_params=_cparams("parallel", "arbitrary"),
        name="gla",
    )(u3, st0, wg, bg, ng, _group_mean_matrix(GLA_WIDTH))


def _gla_state_in(s):
    b = s.shape[0]
    st = jnp.swapaxes(s.astype(F32), 2, 3)
    eye = jnp.eye(GLA_HEADS, dtype=F32)
    full = st[:, :, :, None, :] * eye[None, :, None, :, None]
    return full.reshape(b, GLA_WIDTH, GLA_HEADS * GLA_DK)


def _gla_state_out(st):
    b = st.shape[0]
    full = st.reshape(b, GLA_HEADS, GLA_DV, GLA_HEADS, GLA_DK)
    diag = jnp.stack([full[:, h, :, h, :] for h in range(GLA_HEADS)], axis=1)
    return jnp.swapaxes(diag, 2, 3)


def _mlstm_kernel(u_ref, c0_ref, n0_ref, m0_ref, cv0_ref, cw_ref, cb_ref, gb_ref, ng_ref, gm_ref,
                  o_ref, c_ref, n_ref, m_ref, cv_ref, xp_sc, *, bb, tl):
    @pl.when(pl.program_id(1) == 0)
    def _init():
        c_ref[...] = c0_ref[...]
        n_ref[...] = n0_ref[...]
        m_ref[...] = m0_ref[...]
        xp_sc[:, 0:SUBLANES, :] = cv0_ref[...]

    pairs = ML_HEADS // 2
    state = [([c_ref[i, j] for j in range(pairs)],
              [n_ref[i, j:j + 1, :] for j in range(pairs)],
              [m_ref[i, h:h + 1, :] for h in range(ML_HEADS)]) for i in range(bb)]
    new_state = [_mlstm_seq(i, state[i], u_ref, cw_ref, cb_ref, gb_ref, ng_ref, gm_ref, o_ref, cv_ref, xp_sc, tl)
                 for i in range(bb)]
    for i in range(bb):
        cps, nps, mbs = new_state[i]
        for j in range(pairs):
            c_ref[i, j] = cps[j]
            n_ref[i, j:j + 1, :] = nps[j]
        for h in range(ML_HEADS):
            m_ref[i, h:h + 1, :] = mbs[h]


def _mlstm_seq(i, state, u_ref, cw_ref, cb_ref, gb_ref, ng_ref, gm_ref, o_ref, cv_ref, xp_sc, tl):
    cps, nps, mbs = state
    c = tl
    w2 = 2 * ML_WIDTH
    u = u_ref[i]
    xp_sc[i, SUBLANES:SUBLANES + tl, :] = u[:, 0:w2]
    conv = cb_ref[...]
    for w in range(CONV_W):
        off = SUBLANES - (CONV_W - 1) + w
        conv = conv + xp_sc[i, off:off + tl, :] * cw_ref[w:w + 1, :]
    tail = xp_sc[i, tl:tl + SUBLANES, :]
    xp_sc[i, 0:SUBLANES, :] = tail
    cv_ref[i] = tail

    qk = _silu(conv)
    mq = qk[:, 0:ML_WIDTH]
    mk = qk[:, ML_WIDTH:w2] * (ML_DH ** -0.5)
    mv = u[:, w2:w2 + ML_WIDTH]
    ifg = u[:, w2 + ML_WIDTH:w2 + ML_WIDTH + LANES] + gb_ref[...]
    logf = _log_sigmoid(ifg)
    og_off = w2 + ML_WIDTH + LANES
    og = _sigmoid(u[:, og_off:og_off + ML_WIDTH])
    zz = _silu(u[:, og_off + ML_WIDTH:og_off + 2 * ML_WIDTH])

    tri = lax.broadcasted_iota(jnp.int32, (c, c), 0) >= lax.broadcasted_iota(jnp.int32, (c, c), 1)
    lane = lax.broadcasted_iota(jnp.int32, (c, LANES), 1)
    low = lane < ML_DH
    low_row = lax.broadcasted_iota(jnp.int32, (1, LANES), 1) < ML_DH
    sq_row = lax.broadcasted_iota(jnp.int32, (LANES, LANES), 0)
    sq_col = lax.broadcasted_iota(jnp.int32, (LANES, LANES), 1)
    same_head = (sq_row < ML_DH) == (sq_col < ML_DH)

    def wide(x):
        return x[:, :c] if c <= LANES else jnp.concatenate([x] * (c // LANES), axis=1)

    fcum_all = _dot_f32(tri.astype(F32), logf)
    gates_t = jnp.where(lane < ML_HEADS, ifg, fcum_all).T
    new_cps, new_nps, new_mbs = [], [], []
    for j in range(ML_HEADS // 2):
        ps = slice(j * LANES, (j + 1) * LANES)
        q_s, k_s, v_s = mq[:, ps], mk[:, ps], mv[:, ps]
        acc = jnp.zeros((c, 2 * LANES), F32)
        per_head = []
        for hl in range(2):
            h = 2 * j + hl
            own = low if hl == 0 else jnp.logical_not(low)
            fc = jnp.broadcast_to(fcum_all[:, ML_HEADS + h:ML_HEADS + h + 1], (c, LANES))
            ii = jnp.broadcast_to(ifg[:, h:h + 1], (c, LANES))
            dm = jnp.where(tri, wide(fc) - gates_t[ML_HEADS + h:ML_HEADS + h + 1, :] + gates_t[h:h + 1, :], NEG_BIG)
            inter = fc + mbs[h]
            m = jnp.maximum(inter, jnp.max(dm, axis=-1, keepdims=True))
            sij = _dot_nt(jnp.where(own, q_s, 0.0), k_s) * jnp.exp(dm - wide(m))
            acc = acc + _dot(sij, jnp.concatenate([jnp.where(own, v_s, 0.0), jnp.where(own, 1.0, 0.0)], axis=1))
            m_last = m[c - 1:c]
            f_last = fc[c - 1:c]
            per_head.append((m, jnp.exp(inter - m), jnp.exp(f_last - fc + ii - m_last),
                             jnp.exp(f_last + mbs[h] - m_last)))
            new_mbs.append(m_last)
        m_p, w_p, wj_p = (jnp.where(low, per_head[0][t], per_head[1][t]) for t in range(3))
        dec_row = jnp.where(low_row, per_head[0][3], per_head[1][3])
        n_mat = jnp.where(same_head, jnp.broadcast_to(nps[j], (LANES, LANES)), 0.0)
        num = w_p * _dot_nt(q_s, cps[j]) + acc[:, 0:LANES]
        den = w_p * _dot_nt(q_s, n_mat) + acc[:, LANES:2 * LANES]
        hh = num / jnp.maximum(jnp.abs(den), jnp.exp(-m_p))
        hn = hh * lax.rsqrt(_dot_split(hh * hh, gm_ref[...]) + EPS) * ng_ref[...]
        o_ref[i, :, ps] = hn * og[:, ps] * zz[:, ps]
        dec_mat = jnp.where(sq_row < ML_DH, jnp.broadcast_to(per_head[0][3], (LANES, LANES)),
                            jnp.broadcast_to(per_head[1][3], (LANES, LANES)))
        new_cps.append(dec_mat * cps[j] + jnp.where(same_head, _dot_tn(wj_p * v_s, k_s), 0.0))
        new_nps.append(dec_row * nps[j] + jnp.sum(wj_p * k_s, axis=0, keepdims=True))
    return new_cps, new_nps, new_mbs


ML_SEQS_PER_STEP = 2


def _mlstm(u_ml, c0, n0, m0, conv0, cw, cb, gb, ng, b, l):
    tl = min(l, 256)
    bb = math.gcd(b, ML_SEQS_PER_STEP)
    pairs = ML_HEADS // 2
    w2 = 2 * ML_WIDTH
    u3 = u_ml.reshape(b, l, W_ML)
    eye2 = jnp.eye(2, dtype=F32)
    c0p = jnp.einsum('bphed,hk->bphekd', c0.astype(F32).reshape(b, pairs, 2, ML_DH, ML_DH), eye2)
    c0p = c0p.reshape(b, pairs, LANES, LANES)
    n0p = n0.astype(F32).reshape(b, pairs, LANES)
    m0b = jnp.broadcast_to(m0.astype(F32)[:, :, None], (b, ML_HEADS, LANES))
    cv0 = jnp.pad(conv0.astype(F32), ((0, 0), (SUBLANES - (CONV_W - 1), 0), (0, 0)))
    gbp = jnp.pad(gb.astype(F32).reshape(1, 2 * ML_HEADS), ((0, 0), (0, LANES - 2 * ML_HEADS)))
    kern = functools.partial(_mlstm_kernel, bb=bb, tl=tl)
    st = lambda bi, li: (bi, 0, 0)
    st4 = lambda bi, li: (bi, 0, 0, 0)
    cst = lambda bi, li: (0, 0)
    o, c_new, n_new, m_new, cv = pl.pallas_call(
        kern,
        grid=(b // bb, l // tl),
        in_specs=[pl.BlockSpec((bb, tl, W_ML), lambda bi, li: (bi, li, 0)),
                  pl.BlockSpec((bb, pairs, LANES, LANES), st4),
                  pl.BlockSpec((bb, pairs, LANES), st),
                  pl.BlockSpec((bb, ML_HEADS, LANES), st),
                  pl.BlockSpec((bb, SUBLANES, w2), st),
                  pl.BlockSpec((CONV_W, w2), cst),
                  pl.BlockSpec((1, w2), cst),
                  pl.BlockSpec((1, LANES), cst),
                  pl.BlockSpec((1, LANES), cst),
                  pl.BlockSpec((LANES, LANES), cst)],
        out_specs=[pl.BlockSpec((bb, tl, ML_WIDTH), lambda bi, li: (bi, li, 0)),
                   pl.BlockSpec((bb, pairs, LANES, LANES), st4),
                   pl.BlockSpec((bb, pairs, LANES), st),
                   pl.BlockSpec((bb, ML_HEADS, LANES), st),
                   pl.BlockSpec((bb, SUBLANES, w2), st)],
        out_shape=[jax.ShapeDtypeStruct((b, l, ML_WIDTH), F32),
                   jax.ShapeDtypeStruct((b, pairs, LANES, LANES), F32),
                   jax.ShapeDtypeStruct((b, pairs, LANES), F32),
                   jax.ShapeDtypeStruct((b, ML_HEADS, LANES), F32),
                   jax.ShapeDtypeStruct((b, SUBLANES, w2), F32)],
        scratch_shapes=[pltpu.VMEM((bb, tl + 2 * SUBLANES, w2), F32)],
        compiler_params=_cparams("parallel", "arbitrary"),
        name="mlstm",
    )(u3, c0p, n0p, m0b, cv0, cw.astype(F32), cb.astype(F32).reshape(1, w2), gbp,
      jnp.tile(ng.astype(F32).reshape(1, ML_DH), (1, 2)), _group_mean_matrix(LANES))
    c6 = c_new.reshape(b, pairs, 2, ML_DH, 2, ML_DH)
    c_out = jnp.stack([c6[:, :, h, :, h, :] for h in range(2)], axis=2).reshape(b, ML_HEADS, ML_DH, ML_DH)
    return (o, c_out, n_new.reshape(b, ML_HEADS, ML_DH), m_new[:, :, 0], cv[:, SUBLANES - (CONV_W - 1):, :])


def _rope_lanes(x, cos_t, sin_t):
    w = x.shape[1]
    half = ROT_DIM // 2
    reps = w // cos_t.shape[1]
    if reps > 1:
        cos_t = jnp.concatenate([cos_t] * reps, axis=1)
        sin_t = jnp.concatenate([sin_t] * reps, axis=1)
    lane = lax.broadcasted_iota(jnp.int32, x.shape, 1) % HEAD_DIM
    partner = jnp.where(lane < half, pltpu.roll(x, w - half, 1), pltpu.roll(x, half, 1))
    return x * cos_t + partner * sin_t


def _nsa_prep_kernel(u_ref, cos_ref, sin_ref, qg_ref, kg_ref, g4_ref, g1_ref, qn_ref, qr_ref, rows_ref, win_ref):
    u = u_ref[...]
    cos_t = cos_ref[...]
    sin_t = sin_ref[...]
    q = _group_norm(u[:, 0:NSA_WIDTH], g4_ref[...], qg_ref[...])
    qn_ref[...] = (q * ATTN_SCALE).astype(BF16)
    qr = _rope_lanes(q, cos_t, sin_t) * (ATTN_SCALE * LOG2E)
    low = lax.broadcasted_iota(jnp.int32, (qr.shape[0], LANES), 1) < HEAD_DIM
    for j in range(NSA_HEADS // 2):
        pair = qr[:, j * LANES:(j + 1) * LANES]
        swapped = pltpu.roll(pair, HEAD_DIM, 1)
        qr_ref[2 * j] = jnp.where(low, pair, swapped).astype(BF16)
        qr_ref[2 * j + 1] = jnp.where(low, swapped, pair).astype(BF16)
    kv = NSA_WIDTH
    k_slc = _rope_lanes(_group_norm(u[:, kv + 2 * KV_W:kv + 3 * KV_W], g1_ref[...], kg_ref[1:2, :]), cos_t, sin_t)
    k_win = _rope_lanes(_group_norm(u[:, kv + 4 * KV_W:kv + 5 * KV_W], g1_ref[...], kg_ref[2:3, :]), cos_t, sin_t)
    rows_ref[:, 0:2 * KV_W] = u[:, kv:kv + 2 * KV_W]
    rows_ref[:, 2 * KV_W:3 * KV_W] = k_slc
    rows_ref[:, 3 * KV_W:4 * KV_W] = u[:, kv + 3 * KV_W:kv + 4 * KV_W]
    win_ref[:, 0:KV_W] = k_win
    win_ref[:, KV_W:2 * KV_W] = u[:, kv + 5 * KV_W:kv + 6 * KV_W]


def _rope_tables(pos):
    half = ROT_DIM // 2
    inv = jnp.exp(-math.log(ROPE_THETA) * jnp.arange(half, dtype=F32) * 2.0 / ROT_DIM)
    ang = pos.astype(F32)[:, None] * inv[None, :]
    cos, sin = jnp.cos(ang), jnp.sin(ang)
    n = pos.shape[0]
    ones = jnp.ones((n, HEAD_DIM - ROT_DIM), F32)
    cos_h = jnp.concatenate([cos, cos, ones], axis=1)
    sin_h = jnp.concatenate([-sin, sin, 0.0 * ones], axis=1)
    reps = LANES // HEAD_DIM
    return jnp.tile(cos_h, (1, reps)), jnp.tile(sin_h, (1, reps))


def _nsa_prep(u_nsa, pos, q_g, k_g, l):
    rows = u_nsa.shape[0]
    tl = min(l, 256)
    nb = l // tl
    cos_t, sin_t = _rope_tables(pos)
    qg = jnp.tile(q_g.astype(F32).reshape(1, HEAD_DIM), (1, NSA_HEADS))
    kg = jnp.tile(k_g.astype(F32), (1, NSA_KV_HEADS))
    cst = lambda i: (0, 0)
    return pl.pallas_call(
        _nsa_prep_kernel,
        grid=(rows // tl,),
        in_specs=[pl.BlockSpec((tl, W_NSA), lambda i: (i, 0)),
                  pl.BlockSpec((tl, LANES), lambda i: (i % nb, 0)),
                  pl.BlockSpec((tl, LANES), lambda i: (i % nb, 0)),
                  pl.BlockSpec((1, NSA_WIDTH), cst),
                  pl.BlockSpec((3, KV_W), cst),
                  pl.BlockSpec((NSA_WIDTH, NSA_WIDTH), cst),
                  pl.BlockSpec((KV_W, KV_W), cst)],
        out_specs=[pl.BlockSpec((tl, NSA_WIDTH), lambda i: (i, 0)),
                   pl.BlockSpec((NSA_HEADS, tl, LANES), lambda i: (0, i, 0)),
                   pl.BlockSpec((tl, N_KV_SLOTS * KV_W), lambda i: (i, 0)),
                   pl.BlockSpec((tl, 2 * KV_W), lambda i: (i, 0))],
        out_shape=[jax.ShapeDtypeStruct((rows, NSA_WIDTH), BF16),
                   jax.ShapeDtypeStruct((NSA_HEADS, rows, LANES), BF16),
                   jax.ShapeDtypeStruct((rows, N_KV_SLOTS * KV_W), F32),
                   jax.ShapeDtypeStruct((rows, 2 * KV_W), F32)],
        compiler_params=_cparams("parallel"),
        name="nsa_prep",
    )(u_nsa, cos_t, sin_t, qg, kg, _group_mean_matrix(NSA_WIDTH), _group_mean_matrix(KV_W))


def _cmp_halves(xk_ref, xv_ref, w_ref, n_half):
    acc_k = jnp.zeros((n_half, 2 * KV_W), F32)
    acc_v = jnp.zeros((n_half, 2 * KV_W), F32)
    for s in range(CMP_STRIDE):
        acc_k = acc_k + _dot(xk_ref[pl.ds(s, n_half, stride=CMP_STRIDE), :], w_ref[0, s])
        acc_v = acc_v + _dot(xv_ref[pl.ds(s, n_half, stride=CMP_STRIDE), :], w_ref[1, s])
    return acc_k, acc_v


def _cmp_finish(acc_k, acc_v, bias_ref, kg_ref, g1_ref, kc_ref, vc_ref, n_half):
    valid = lax.broadcasted_iota(jnp.int32, (n_half, KV_W), 0) < n_half - 1

    def summary(acc, bias):
        return acc[:, 0:KV_W] + pltpu.roll(acc[:, KV_W:2 * KV_W], n_half - 1, 0) + bias

    kc = _group_norm(summary(acc_k, bias_ref[0:1, :]), g1_ref[...], kg_ref[...])
    kc_ref[0] = jnp.where(valid, kc, 0.0).astype(BF16)
    vc_ref[0] = jnp.where(valid, summary(acc_v, bias_ref[1:2, :]), 0.0).astype(BF16)


def _cmp_kv_kernel(xk_ref, xv_ref, w_ref, bias_ref, kg_ref, g1_ref, kc_ref, vc_ref, *, n_half):
    acc_k, acc_v = _cmp_halves(xk_ref.at[0], xv_ref.at[0], w_ref, n_half)
    _cmp_finish(acc_k, acc_v, bias_ref, kg_ref, g1_ref, kc_ref, vc_ref, n_half)


PAGES_PER_STEP = 16
CMP_PAGES_PER_STEP = 64


def _cmp_paged_kernel(pt_ref, *refs, npg, page, whole):
    page_refs = refs[:npg]
    if whole:
        w_ref, bias_ref, kg_ref, g1_ref, kc_ref, vc_ref, xk_sc, xv_sc = refs[npg:]
    else:
        w_ref, acck_ref, accv_ref, xk_sc, xv_sc = refs[npg:]
    for i in range(npg):
        xk_sc[i * page:(i + 1) * page, :] = page_refs[i][0, 0:KV_W, :].T
        xv_sc[i * page:(i + 1) * page, :] = page_refs[i][0, KV_W:2 * KV_W, :].T
    n_half = npg * page // CMP_STRIDE
    acc_k, acc_v = _cmp_halves(xk_sc, xv_sc, w_ref, n_half)
    if whole:
        _cmp_finish(acc_k, acc_v, bias_ref, kg_ref, g1_ref, kc_ref, vc_ref, n_half)
    else:
        acck_ref[0] = acc_k
        accv_ref[0] = acc_v


def _cmp_fin_kernel(acck_ref, accv_ref, bias_ref, kg_ref, g1_ref, kc_ref, vc_ref, *, n_half):
    _cmp_finish(acck_ref[0], accv_ref[0], bias_ref, kg_ref, g1_ref, kc_ref, vc_ref, n_half)


def _page_specs(npg, page, row_blk, pool_off):
    def spec(i):
        return pl.BlockSpec((1, 2 * KV_W, page), lambda bi, ji, pt: (pt[bi, ji * npg + i] + pool_off, row_blk, 0))
    return [spec(i) for i in range(npg)]


def _cmp_kv_paged(cache_t, page_table, pool_off, wcat, bias, kg0):
    b, n_pages = page_table.shape
    page = cache_t.shape[2]
    npg = math.gcd(n_pages, CMP_PAGES_PER_STEP)
    nh_step = npg * page // CMP_STRIDE
    n_half = n_pages * page // CMP_STRIDE
    kg = jnp.tile(kg0.astype(F32).reshape(1, HEAD_DIM), (1, NSA_KV_HEADS))
    if npg == n_pages:
        cst = lambda bi, ji, pt: (0, 0)
        return pl.pallas_call(
            functools.partial(_cmp_paged_kernel, npg=npg, page=page, whole=True),
            grid_spec=pltpu.PrefetchScalarGridSpec(
                num_scalar_prefetch=1, grid=(b, 1),
                in_specs=_page_specs(npg, page, 0, pool_off)
                + [pl.BlockSpec((2, CMP_STRIDE, KV_W, 2 * KV_W), lambda bi, ji, pt: (0, 0, 0, 0)),
                   pl.BlockSpec((2, KV_W), cst), pl.BlockSpec((1, KV_W), cst), pl.BlockSpec((KV_W, KV_W), cst)],
                out_specs=[pl.BlockSpec((1, n_half, KV_W), lambda bi, ji, pt: (bi, 0, 0)),
                           pl.BlockSpec((1, n_half, KV_W), lambda bi, ji, pt: (bi, 0, 0))],
                scratch_shapes=[pltpu.VMEM((npg * page, KV_W), F32), pltpu.VMEM((npg * page, KV_W), F32)]),
            out_shape=[jax.ShapeDtypeStruct((b, n_half, KV_W), BF16),
                       jax.ShapeDtypeStruct((b, n_half, KV_W), BF16)],
            compiler_params=_cparams("parallel", "arbitrary"),
            name="cmp_paged",
        )(page_table, *([cache_t] * npg), wcat, bias, kg, _group_mean_matrix(KV_W))
    kern = functools.partial(_cmp_paged_kernel, npg=npg, page=page, whole=False)
    acc_k, acc_v = pl.pallas_call(
        kern,
        grid_spec=pltpu.PrefetchScalarGridSpec(
            num_scalar_prefetch=1, grid=(b, n_pages // npg),
            in_specs=_page_specs(npg, page, 0, pool_off)
            + [pl.BlockSpec((2, CMP_STRIDE, KV_W, 2 * KV_W), lambda bi, ji, pt: (0, 0, 0, 0))],
            out_specs=[pl.BlockSpec((1, nh_step, 2 * KV_W), lambda bi, ji, pt: (bi, ji, 0)),
                       pl.BlockSpec((1, nh_step, 2 * KV_W), lambda bi, ji, pt: (bi, ji, 0))],
            scratch_shapes=[pltpu.VMEM((npg * page, KV_W), F32), pltpu.VMEM((npg * page, KV_W), F32)]),
        out_shape=[jax.ShapeDtypeStruct((b, n_half, 2 * KV_W), F32),
                   jax.ShapeDtypeStruct((b, n_half, 2 * KV_W), F32)],
        compiler_params=_cparams("parallel", "arbitrary"),
        name="cmp_paged",
    )(page_table, *([cache_t] * npg), wcat)
    blk = lambda bi: (bi, 0, 0)
    return pl.pallas_call(
        functools.partial(_cmp_fin_kernel, n_half=n_half),
        grid=(b,),
        in_specs=[pl.BlockSpec((1, n_half, 2 * KV_W), blk),
                  pl.BlockSpec((1, n_half, 2 * KV_W), blk),
                  pl.BlockSpec((2, KV_W), lambda bi: (0, 0)),
                  pl.BlockSpec((1, KV_W), lambda bi: (0, 0)),
                  pl.BlockSpec((KV_W, KV_W), lambda bi: (0, 0))],
        out_specs=[pl.BlockSpec((1, n_half, KV_W), blk), pl.BlockSpec((1, n_half, KV_W), blk)],
        out_shape=[jax.ShapeDtypeStruct((b, n_half, KV_W), BF16),
                   jax.ShapeDtypeStruct((b, n_half, KV_W), BF16)],
        compiler_params=_cparams("parallel"),
        name="cmp_fin",
    )(acc_k, acc_v, bias, kg, _group_mean_matrix(KV_W))


def _cmp_weights(cmp_pos, cmp_w):
    wf = cmp_w.astype(F32)
    eye_g = jnp.eye(NSA_KV_HEADS, dtype=F32)

    def bd(w):
        return jnp.einsum('ksde,gh->ksgdhe', w, eye_g).reshape(2, CMP_STRIDE, KV_W, KV_W)

    wcat = jnp.concatenate([bd(wf[:, :CMP_STRIDE]), bd(wf[:, CMP_STRIDE:])], axis=3).astype(BF16)
    bias = jnp.einsum('ksd,ksde->ke', cmp_pos.astype(F32), wf)
    return wcat, jnp.tile(bias, (1, NSA_KV_HEADS))


def _cmp_kv(x3, k_blk, v_blk, t_use, wcat, bias, kg0):
    b = x3.shape[0]
    n_half = t_use // CMP_STRIDE
    kern = functools.partial(_cmp_kv_kernel, n_half=n_half)
    kg = jnp.tile(kg0.astype(F32).reshape(1, HEAD_DIM), (1, NSA_KV_HEADS))
    return pl.pallas_call(
        kern,
        grid=(b,),
        in_specs=[pl.BlockSpec((1, t_use, KV_W), lambda bi: (bi, 0, k_blk)),
                  pl.BlockSpec((1, t_use, KV_W), lambda bi: (bi, 0, v_blk)),
                  pl.BlockSpec((2, CMP_STRIDE, KV_W, 2 * KV_W), lambda bi: (0, 0, 0, 0)),
                  pl.BlockSpec((2, KV_W), lambda bi: (0, 0)),
                  pl.BlockSpec((1, KV_W), lambda bi: (0, 0)),
                  pl.BlockSpec((KV_W, KV_W), lambda bi: (0, 0))],
        out_specs=[pl.BlockSpec((1, n_half, KV_W), lambda bi: (bi, 0, 0)),
                   pl.BlockSpec((1, n_half, KV_W), lambda bi: (bi, 0, 0))],
        out_shape=[jax.ShapeDtypeStruct((b, n_half, KV_W), BF16),
                   jax.ShapeDtypeStruct((b, n_half, KV_W), BF16)],
        compiler_params=_cparams("parallel"),
        name="cmp_kv",
    )(x3, x3, wcat, bias, kg, _group_mean_matrix(KV_W))


def _cmp_attn_kernel(qn_ref, kc_ref, vc_ref, ov_ref, o_ref, sel_ref, *, bb, tq, n_half, n_cmp, n_sel, nbp, pos0,
                     bias_out):
    biases = []
    qi = pl.program_id(1)
    nq = bb * tq
    pos_c = pos0 + qi * tq + lax.broadcasted_iota(jnp.int32, (tq, 1), 0)
    ncol = lax.broadcasted_iota(jnp.int32, (1, n_half), 1)
    cmask = (ncol * CMP_STRIDE + (CMP_BLOCK - 1) <= pos_c) & (ncol < n_cmp)
    pos_r = pos0 + qi * tq + lax.broadcasted_iota(jnp.int32, (1, nq), 1) % tq
    blk = lax.broadcasted_iota(jnp.int32, (nbp, 1), 0)
    cur = pos_r // SEL_BLOCK
    forced = (blk == 0) | (blk == cur) | (blk == cur - 1)
    valid = blk * SEL_BLOCK <= pos_r
    real = blk < n_sel
    for g in range(NSA_KV_HEADS):
        gs = slice(g * HEAD_DIM, (g + 1) * HEAD_DIM)
        psums = []
        for i in range(bb):
            kc = kc_ref[i, :, gs]
            vc = vc_ref[i, :, gs]
            psum = jnp.zeros((tq, n_half), F32)
            for h in range(NSA_HPG):
                hs = slice((g * NSA_HPG + h) * HEAD_DIM, (g * NSA_HPG + h + 1) * HEAD_DIM)
                s = _dot_nt(qn_ref[i, :, hs], kc)
                m = jnp.max(jnp.where(cmask, s, NEG_BIG), axis=-1, keepdims=True)
                m = jnp.where(m > 0.5 * NEG_BIG, m, 0.0)
                e = jnp.where(cmask, jnp.exp(s - m), 0.0)
                p = e / jnp.maximum(jnp.sum(e, axis=-1, keepdims=True), 1e-30)
                o_ref[i, :, hs] = _dot(p, vc)
                psum = psum + p
            psums.append(psum)
        psum = psums[0] if bb == 1 else jnp.concatenate(psums, axis=0)
        p_hi = psum.astype(BF16)
        p_lo = (psum - p_hi.astype(F32)).astype(BF16)
        ov = ov_ref[...]
        imp = _dot_nt(ov, p_hi) + _dot_nt(ov, p_lo)
        score = jnp.where(forced, 3e38, jnp.where(valid, imp, -1e38))
        score = jnp.where(real, score, -3e38)
        cnt = jnp.zeros((nbp, nq), F32)
        for jp in range(n_sel):
            rowv = score[jp:jp + 1, :]
            beats = (rowv > score) | ((rowv == score) & (blk > jp))
            cnt = cnt + jnp.where(beats, 1.0, 0.0)
        sel = (cnt < float(min(N_SELECT, n_sel))) & real
        if bias_out:
            sbg = jnp.where(sel, 0.0, NEG_BIG)
            if nbp < HEAD_DIM:
                sbg = jnp.concatenate([sbg, jnp.full((HEAD_DIM - nbp, nq), NEG_BIG, F32)], axis=0)
            biases.append(sbg)
        else:
            sel_ref[0, g * nbp:(g + 1) * nbp, :] = jnp.where(sel, 1.0, 0.0).astype(BF16)
    if bias_out:
        sel_ref[0] = jnp.concatenate(biases[::-1], axis=0).T.astype(BF16)


def _overlap_t(n_cmp, n_sel, n_half, nbp):
    a = SEL_BLOCK // CMP_STRIDE
    bb = CMP_BLOCK // CMP_STRIDE
    i = np.arange(n_half)[None, :]
    j = np.arange(nbp)[:, None]
    s = i - a * j + (bb - 1)
    cnt = np.maximum(np.minimum(np.minimum(s + 1, a + bb - 1 - s), min(a, bb)), 0)
    cnt = np.where((i < n_cmp) & (j < n_sel), cnt, 0)
    return jnp.asarray(cnt, dtype=BF16)


CMP_TQ = 256


def _cmp_attn(qn3, kc, vc, pos0, t_len, bias_out=False):
    b, l, _ = qn3.shape
    n_half = kc.shape[1]
    n_cmp = n_half - 1
    n_sel = -(-t_len // SEL_BLOCK)
    nbp = _round_up(n_sel, 16)
    tq = min(l, CMP_TQ)
    bb = math.gcd(b, max(LANES // tq, 1))
    kern = functools.partial(_cmp_attn_kernel, bb=bb, tq=tq, n_half=n_half, n_cmp=n_cmp, n_sel=n_sel, nbp=nbp,
                             pos0=pos0, bias_out=bias_out)
    if bias_out:
        assert bb == 1 and NSA_KV_HEADS == 2 and nbp <= HEAD_DIM
        sel_spec = pl.BlockSpec((1, tq, LANES), lambda bi, qi: (bi, qi, 0))
        sel_shape = jax.ShapeDtypeStruct((b, l, LANES), BF16)
    else:
        sel_spec = pl.BlockSpec((1, NSA_KV_HEADS * nbp, bb * tq), lambda bi, qi: (bi, 0, qi))
        sel_shape = jax.ShapeDtypeStruct((b // bb, NSA_KV_HEADS * nbp, bb * l), BF16)
    o_cmp, sel_t = pl.pallas_call(
        kern,
        grid=(b // bb, l // tq),
        in_specs=[pl.BlockSpec((bb, tq, NSA_WIDTH), lambda bi, qi: (bi, qi, 0)),
                  pl.BlockSpec((bb, n_half, KV_W), lambda bi, qi: (bi, 0, 0)),
                  pl.BlockSpec((bb, n_half, KV_W), lambda bi, qi: (bi, 0, 0)),
                  pl.BlockSpec((nbp, n_half), lambda bi, qi: (0, 0))],
        out_specs=[pl.BlockSpec((bb, tq, NSA_WIDTH), lambda bi, qi: (bi, qi, 0)), sel_spec],
        out_shape=[jax.ShapeDtypeStruct((b, l, NSA_WIDTH), F32), sel_shape],
        compiler_params=_cparams("parallel", "parallel"),
        name="cmp_attn",
    )(qn3, kc, vc, _overlap_t(n_cmp, n_sel, n_half, nbp))
    if bb > 1:
        sel_t = sel_t.reshape(b // bb, NSA_KV_HEADS * nbp, bb, l)
        sel_t = jnp.swapaxes(sel_t, 1, 2).reshape(b, NSA_KV_HEADS * nbp, l)
    return o_cmp, sel_t


SEL_TQ = 128
SEL_TK = 2048
WIN_TQ = 256


def _group_lanes(shape, g):
    lane = lax.broadcasted_iota(jnp.int32, shape, len(shape) - 1)
    return (lane < HEAD_DIM) if g == 0 else (lane >= HEAD_DIM)


def _store_heads(o_ref, r, g, tq):
    lo = g * HEAD_DIM
    for h in range(NSA_HPG):
        hh = g * NSA_HPG + h
        o_ref[0, :, hh * HEAD_DIM:(hh + 1) * HEAD_DIM] = r[h * tq:(h + 1) * tq, lo:lo + HEAD_DIM]


def _flash_sel_kernel(qi_ref, ki_ref, q_ref, k_ref, v_ref, sb_ref, o_ref, m_sc, acc_sc, *, tq, tk):
    qi = qi_ref[pl.program_id(1)]
    ki = ki_ref[pl.program_id(1)]
    kmax = (qi * tq + tq - 1) // tk
    hq = NSA_HPG * tq

    @pl.when(ki == 0)
    def _init():
        m_sc[...] = jnp.full(m_sc.shape, NEG_BIG, F32)
        acc_sc[...] = jnp.zeros(acc_sc.shape, F32)

    def step(diagonal):
        key_blk = ki * (tk // SEL_BLOCK) + lax.broadcasted_iota(jnp.int32, (tk, LANES), 0) // SEL_BLOCK
        one_hot = jnp.where(lax.broadcasted_iota(jnp.int32, (tk, LANES), 1) % HEAD_DIM == key_blk, 1.0, 0.0).astype(BF16)
        kblk = k_ref[0].astype(BF16)
        vblk = v_ref[0].astype(BF16)
        sb4 = jnp.concatenate([sb_ref[0]] * NSA_HPG, axis=0)
        if diagonal:
            qpos = qi * tq + lax.broadcasted_iota(jnp.int32, (tq, tk), 0)
            kpos = ki * tk + lax.broadcasted_iota(jnp.int32, (tq, tk), 1)
            causal4 = jnp.concatenate([jnp.where(kpos <= qpos, 0.0, NEG_BIG)] * NSA_HPG, axis=0)
        m_prev = [m_sc[g] for g in range(NSA_KV_HEADS)]
        acc_prev = [acc_sc[g] for g in range(NSA_KV_HEADS)]
        m_out, acc_out = [], []
        for g in range(NSA_KV_HEADS):
            own_k = _group_lanes((tk, LANES), g)
            q4 = q_ref[g * NSA_HPG:(g + 1) * NSA_HPG].reshape(hq, LANES)
            qx = jnp.where(_group_lanes((hq, LANES), g), q4, sb4)
            kx = jnp.where(own_k, kblk, one_hot)
            s = lax.dot_general(qx, kx, (((1,), (1,)), ((), ())), preferred_element_type=F32)
            if diagonal:
                s = s + causal4
            m_new = jnp.maximum(m_prev[g], jnp.max(s, axis=-1, keepdims=True))
            p = jnp.exp2(s - m_new[:, :1]).astype(BF16)
            vx = jnp.where(own_k, vblk, 1.0)
            acc_out.append(jnp.exp2(m_prev[g] - m_new) * acc_prev[g] + jnp.dot(p, vx, preferred_element_type=F32))
            m_out.append(m_new)
        for g in range(NSA_KV_HEADS):
            m_sc[g] = m_out[g]
            acc_sc[g] = acc_out[g]

    @pl.when(ki < kmax)
    def _full():
        step(False)

    @pl.when(ki == kmax)
    def _last():
        step(True)
        for g in range(NSA_KV_HEADS):
            acc = acc_sc[g]
            _store_heads(o_ref, acc / pltpu.roll(acc, HEAD_DIM, 1), g, tq)


def _flash_sel(q8, rows3, selb):
    b, l, _ = rows3.shape
    tq, tk = min(SEL_TQ, l), min(SEL_TK, l)
    nq = l // tq
    pairs = [(qi, ki) for qi in range(nq) for ki in range((qi * tq + tq - 1) // tk + 1)]
    qi_tab = jnp.asarray([p[0] for p in pairs], jnp.int32)
    ki_tab = jnp.asarray([p[1] for p in pairs], jnp.int32)
    kv_idx = lambda blk: (lambda bi, si, qt, kt: (bi, kt[si], blk))
    return pl.pallas_call(
        functools.partial(_flash_sel_kernel, tq=tq, tk=tk),
        grid_spec=pltpu.PrefetchScalarGridSpec(
            num_scalar_prefetch=2, grid=(b, len(pairs)),
            in_specs=[pl.BlockSpec((NSA_HEADS, tq, LANES), lambda bi, si, qt, kt: (0, bi * nq + qt[si], 0)),
                      pl.BlockSpec((1, tk, KV_W), kv_idx(2)),
                      pl.BlockSpec((1, tk, KV_W), kv_idx(3)),
                      pl.BlockSpec((1, tq, LANES), lambda bi, si, qt, kt: (bi, qt[si], 0))],
            out_specs=pl.BlockSpec((1, tq, NSA_WIDTH), lambda bi, si, qt, kt: (bi, qt[si], 0)),
            scratch_shapes=[pltpu.VMEM((NSA_KV_HEADS, NSA_HPG * tq, LANES), F32),
                            pltpu.VMEM((NSA_KV_HEADS, NSA_HPG * tq, LANES), F32)]),
        out_shape=jax.ShapeDtypeStruct((b, l, NSA_WIDTH), F32),
        compiler_params=_cparams("parallel", "arbitrary"),
        name="flash_sel",
    )(qi_tab, ki_tab, q8, rows3, rows3, selb)


def _flash_win_kernel(q_ref, *refs, tq, back):
    nspan = back + 1
    k_refs, v_refs, o_ref = refs[:nspan], refs[nspan:2 * nspan], refs[2 * nspan]
    qi = pl.program_id(1)
    hq = NSA_HPG * tq
    span = nspan * tq
    qpos = qi * tq + lax.broadcasted_iota(jnp.int32, (tq, span), 0)
    kpos = (qi - back) * tq + lax.broadcasted_iota(jnp.int32, (tq, span), 1)
    ok = (kpos <= qpos) & (kpos > qpos - WINDOW) & (kpos >= 0)
    bias4 = jnp.concatenate([jnp.where(ok, 0.0, NEG_BIG)] * NSA_HPG, axis=0)
    kcat = jnp.concatenate([r[0] for r in k_refs], axis=0).astype(BF16)
    vcat = jnp.concatenate([r[0] for r in v_refs], axis=0).astype(BF16)
    for g in range(NSA_KV_HEADS):
        q4 = q_ref[g * NSA_HPG:(g + 1) * NSA_HPG].reshape(hq, LANES)
        qx = jnp.where(_group_lanes((hq, LANES), g), q4, 0.0)
        s = lax.dot_general(qx, kcat, (((1,), (1,)), ((), ())), preferred_element_type=F32) + bias4
        p = jnp.exp2(s - jnp.max(s, axis=-1, keepdims=True)).astype(BF16)
        vx = jnp.where(_group_lanes((span, LANES), g), vcat, 1.0)
        r = jnp.dot(p, vx, preferred_element_type=F32)
        _store_heads(o_ref, r / pltpu.roll(r, HEAD_DIM, 1), g, tq)


def _flash_win(q8, win3):
    b, l, _ = win3.shape
    tq = min(WIN_TQ, l)
    nq = l // tq
    back = -(-(WINDOW - 1) // tq)

    def kv_specs(blk):
        return [pl.BlockSpec((1, tq, KV_W), (lambda bi, qi, j=j: (bi, jnp.maximum(qi - back + j, 0), blk)))
                for j in range(back + 1)]

    return pl.pallas_call(
        functools.partial(_flash_win_kernel, tq=tq, back=back),
        grid=(b, nq),
        in_specs=[pl.BlockSpec((NSA_HEADS, tq, LANES), lambda bi, qi: (0, bi * nq + qi, 0))]
        + kv_specs(0) + kv_specs(1),
        out_specs=pl.BlockSpec((1, tq, NSA_WIDTH), lambda bi, qi: (bi, qi, 0)),
        out_shape=jax.ShapeDtypeStruct((b, l, NSA_WIDTH), F32),
        compiler_params=_cparams("parallel", "parallel"),
        name="flash_win",
    )(q8, *([win3] * (2 * (back + 1))))


def _softmax_update(s, m_prev, l_prev):
    m_new = jnp.maximum(m_prev, jnp.max(s, axis=-1, keepdims=True))
    alpha = jnp.exp2(m_prev - m_new)
    p = jnp.exp2(s - m_new[:, :1])
    return m_new, alpha, alpha * l_prev + jnp.sum(p, axis=-1, keepdims=True), p


def _paged_attn_kernel(pt_ref, *refs, npg, page, l_new, nbp):
    page_refs = refs[:npg]
    (qz_ref, sel_ref, kn_ref, vn_ref, win_ref, kwn_ref, vwn_ref,
     oslc_ref, owin_ref, wout_ref, m_sc, l_sc, acc_sc) = refs[npg:]
    ji = pl.program_id(1)
    nr = qz_ref.shape[1]
    span = npg * page

    @pl.when(ji == 0)
    def _init():
        m_sc[...] = jnp.full(m_sc.shape, NEG_BIG, F32)
        l_sc[...] = jnp.zeros(l_sc.shape, F32)
        acc_sc[...] = jnp.zeros(acc_sc.shape, F32)

    qz = qz_ref[0]
    kt = jnp.concatenate([page_refs[i][0, 0:KV_W, :] for i in range(npg)], axis=1).astype(BF16)
    vt = jnp.concatenate([page_refs[i][0, KV_W:2 * KV_W, :] for i in range(npg)], axis=1).astype(BF16)
    s = jnp.dot(qz, kt, preferred_element_type=F32)
    e_row = lax.broadcasted_iota(jnp.int32, (nbp, span), 0)
    e_col = lax.broadcasted_iota(jnp.int32, (nbp, span), 1)
    expand = jnp.where(e_row == ji * (span // SEL_BLOCK) + e_col // SEL_BLOCK, 1.0, 0.0).astype(BF16)
    picked = jnp.dot(sel_ref[0], expand, preferred_element_type=F32) > 0.5
    m_new, alpha, l_new_v, p = _softmax_update(jnp.where(picked, s, NEG_BIG), m_sc[...], l_sc[...])
    m_sc[...] = m_new
    l_sc[...] = l_new_v
    acc_sc[...] = alpha * acc_sc[...] + _dot_nt(p, vt)

    @pl.when(ji == pl.num_programs(1) - 1)
    def _fin():
        q_of_row = lax.broadcasted_iota(jnp.int32, (nr, l_new), 0) % l_new
        j_new = lax.broadcasted_iota(jnp.int32, (nr, l_new), 1)
        new_ok = j_new <= q_of_row
        sn = jnp.where(new_ok, _dot_nt(qz, kn_ref[0]), NEG_BIG)
        m2, a2, l2, p2 = _softmax_update(sn, m_sc[...], l_sc[...])
        oslc_ref[0] = (a2 * acc_sc[...] + _dot(p2, vn_ref[0])) / l2
        wlen = win_ref.shape[2]
        kw = win_ref[0, 0:KV_W, :]
        vw = win_ref[0, KV_W:2 * KV_W, :]
        i_old = lax.broadcasted_iota(jnp.int32, (nr, wlen), 1)
        q_old = lax.broadcasted_iota(jnp.int32, (nr, wlen), 0) % l_new
        sw = jnp.where(i_old + (WINDOW - wlen) > q_old, _dot(qz, kw), NEG_BIG)
        swn = jnp.where(new_ok, _dot_nt(qz, kwn_ref[0]), NEG_BIG)
        mw = jnp.maximum(jnp.max(sw, axis=-1, keepdims=True), jnp.max(swn, axis=-1, keepdims=True))
        pw = jnp.exp2(sw - mw)
        pwn = jnp.exp2(swn - mw)
        lw = jnp.sum(pw, axis=-1, keepdims=True) + jnp.sum(pwn, axis=-1, keepdims=True)
        owin_ref[0] = (_dot_nt(pw, vw) + _dot(pwn, vwn_ref[0])) / lw
        new_t = jnp.concatenate([kwn_ref[0], vwn_ref[0]], axis=1).T
        place = (lax.broadcasted_iota(jnp.int32, (l_new, wlen), 1)
                 == lax.broadcasted_iota(jnp.int32, (l_new, wlen), 0) + (wlen - l_new)).astype(F32)
        lane = lax.broadcasted_iota(jnp.int32, (2 * KV_W, wlen), 1)
        wout_ref[0] = jnp.where(lane < wlen - l_new, pltpu.roll(win_ref[0], wlen - l_new, 1), _dot_f32(new_t, place))


def _paged_attn(cache_t, page_table, pool_off, qz, sel_rows, rows3, win_t, win_off, win3):
    b, n_pages = page_table.shape
    page = cache_t.shape[2]
    npg = PAGES_PER_STEP
    nr = qz.shape[1]
    l_new = rows3.shape[1]
    nbp = sel_rows.shape[2]
    wlen = win_t.shape[2]
    kern = functools.partial(_paged_attn_kernel, npg=npg, page=page, l_new=l_new, nbp=nbp)
    per_b = lambda bi, ji, pt: (bi, 0, 0)
    return pl.pallas_call(
        kern,
        grid_spec=pltpu.PrefetchScalarGridSpec(
            num_scalar_prefetch=1, grid=(b, n_pages // npg),
            in_specs=_page_specs(npg, page, 1, pool_off)
            + [pl.BlockSpec((1, nr, KV_W), per_b),
               pl.BlockSpec((1, nr, nbp), per_b),
               pl.BlockSpec((1, l_new, KV_W), lambda bi, ji, pt: (bi, 0, 2)),
               pl.BlockSpec((1, l_new, KV_W), lambda bi, ji, pt: (bi, 0, 3)),
               pl.BlockSpec((1, 2 * KV_W, wlen), lambda bi, ji, pt: (bi + win_off, 0, 0)),
               pl.BlockSpec((1, l_new, KV_W), lambda bi, ji, pt: (bi, 0, 0)),
               pl.BlockSpec((1, l_new, KV_W), lambda bi, ji, pt: (bi, 0, 1))],
            out_specs=[pl.BlockSpec((1, nr, KV_W), per_b),
                       pl.BlockSpec((1, nr, KV_W), per_b),
                       pl.BlockSpec((1, 2 * KV_W, wlen), per_b)],
            scratch_shapes=[pltpu.VMEM((nr, LANES), F32), pltpu.VMEM((nr, LANES), F32),
                            pltpu.VMEM((nr, KV_W), F32)]),
        out_shape=[jax.ShapeDtypeStruct((b, nr, KV_W), F32),
                   jax.ShapeDtypeStruct((b, nr, KV_W), F32),
                   jax.ShapeDtypeStruct((b, 2 * KV_W, wlen), F32)],
        compiler_params=_cparams("parallel", "arbitrary"),
        name="paged_attn",
    )(page_table, *([cache_t] * npg), qz, sel_rows, rows3, rows3, win_t, win3, win3)


def _nsa_fresh(u_nsa, b, l, win_keep, q_g, k_g, cmp_pos, cmp_w):
    qn, qr, rows, win = _nsa_prep(u_nsa, jnp.arange(l), q_g, k_g, l)
    rows3 = rows.reshape(b, l, N_KV_SLOTS * KV_W)
    win3 = win.reshape(b, l, 2 * KV_W)
    wcat, bias = _cmp_weights(cmp_pos, cmp_w)
    t_use = (l // CMP_STRIDE) * CMP_STRIDE
    kc, vc = _cmp_kv(rows3, 0, 1, t_use, wcat, bias, k_g[0])
    o_cmp, selb = _cmp_attn(qn.reshape(b, l, NSA_WIDTH), kc, vc, 0, l, bias_out=True)
    o_slc = _flash_sel(qr, rows3, selb)
    o_win = _flash_win(qr, win3)
    branches = tuple(o.reshape(b * l, NSA_WIDTH) for o in (o_cmp, o_slc, o_win))
    new_rows = rows3.reshape(b, l, N_KV_SLOTS, NSA_KV_HEADS, HEAD_DIM)
    new_win = win3[:, l - win_keep:].reshape(b, win_keep, 2, NSA_KV_HEADS, HEAD_DIM)
    return branches, new_rows, new_win


def _nsa_paged(u_nsa, b, l, paged, q_g, k_g, cmp_pos, cmp_w):
    cache_t, page_table, pool_off, win_t, win_off = paged
    past_len = page_table.shape[1] * cache_t.shape[2]
    assert (past_len + l) // CMP_STRIDE == past_len // CMP_STRIDE and past_len % SEL_BLOCK == 0
    qn, qr, rows, win = _nsa_prep(u_nsa, past_len + jnp.arange(l), q_g, k_g, l)
    rows3 = rows.reshape(b, l, N_KV_SLOTS * KV_W)
    win3 = win.reshape(b, l, 2 * KV_W)
    wcat, bias = _cmp_weights(cmp_pos, cmp_w)
    kc, vc = _cmp_kv_paged(cache_t, page_table, pool_off, wcat, bias, k_g[0])
    o_cmp, sel_t = _cmp_attn(qn.reshape(b, l, NSA_WIDTH), kc, vc, past_len, past_len + l)
    nbp = sel_t.shape[1] // NSA_KV_HEADS
    sel_rows = jnp.swapaxes(sel_t.reshape(b, NSA_KV_HEADS, 1, nbp, l), 3, 4)
    sel_rows = jnp.broadcast_to(sel_rows, (b, NSA_KV_HEADS, NSA_HPG, l, nbp)).reshape(b, NSA_HEADS * l, nbp)
    q5 = jnp.transpose(qr[:, :, :HEAD_DIM].reshape(NSA_KV_HEADS, NSA_HPG, b, l, HEAD_DIM), (2, 0, 1, 3, 4))
    qz = jnp.einsum('bghqd,gk->bghqkd', q5, jnp.eye(NSA_KV_HEADS, dtype=q5.dtype)).reshape(b, NSA_HEADS * l, KV_W)
    o_slc_z, o_win_z, wout = _paged_attn(cache_t, page_table, pool_off, qz, sel_rows, rows3, win_t, win_off, win3)

    def own_group(o):
        o6 = o.reshape(b, NSA_KV_HEADS, NSA_HPG, l, NSA_KV_HEADS, HEAD_DIM)
        d = jnp.stack([o6[:, g, :, :, g, :] for g in range(NSA_KV_HEADS)], axis=1)
        return jnp.transpose(d, (0, 3, 1, 2, 4)).reshape(b * l, NSA_WIDTH)

    branches = (o_cmp.reshape(b * l, NSA_WIDTH), own_group(o_slc_z), own_group(o_win_z))
    new_rows = rows3.reshape(b, l, N_KV_SLOTS, NSA_KV_HEADS, HEAD_DIM)
    wlen = wout.shape[2]
    new_win = jnp.transpose(wout.reshape(b, 2, NSA_KV_HEADS, HEAD_DIM, wlen), (0, 4, 1, 2, 3))
    return branches, new_rows, new_win


PROJ_ROWS = 512


def _hybrid_layer(x, paged, win_keep, s_gla, c_ml, n_ml, m_ml, conv_ml,
                  norm_g, w_in_pad, w_out_bf, gla_w_gate, gla_b_gate, gla_norm_g,
                  nsa_q_norm_g, nsa_k_norm_g, nsa_cmp_pos, nsa_cmp_w,
                  ml_conv_w, ml_conv_b, ml_gate_b, ml_norm_g):
    b, l, _ = x.shape
    rows = b * l
    tm = math.gcd(rows, PROJ_ROWS)
    x2d = x.reshape(rows, D_MODEL)
    u_gla, u_nsa, u_ml = _proj_in(x2d, norm_g, w_in_pad, tm)

    nk = GLA_HEADS * GLA_DK
    wg = jnp.pad(gla_w_gate.astype(F32), ((0, LANES - GLA_RANK), (0, 0))).astype(BF16)
    o_a, st_new = _gla(u_gla, _gla_state_in(s_gla), wg, gla_b_gate.astype(F32).reshape(1, nk),
                       jnp.tile(gla_norm_g.astype(F32).reshape(1, GLA_DV), (1, GLA_HEADS)), b, l)
    s_new = _gla_state_out(st_new)

    if paged is None:
        o_nsa, new_rows, new_win = _nsa_fresh(u_nsa, b, l, win_keep, nsa_q_norm_g, nsa_k_norm_g,
                                              nsa_cmp_pos, nsa_cmp_w)
    else:
        o_nsa, new_rows, new_win = _nsa_paged(u_nsa, b, l, paged, nsa_q_norm_g, nsa_k_norm_g,
                                              nsa_cmp_pos, nsa_cmp_w)

    o_c, c_new, n_new, m_new, conv_new = _mlstm(u_ml, c_ml, n_ml, m_ml, conv_ml, ml_conv_w, ml_conv_b,
                                                ml_gate_b, ml_norm_g, b, l)

    y = _proj_out(o_a.reshape(rows, GLA_WIDTH), *o_nsa, u_nsa, o_c.reshape(rows, ML_WIDTH), x2d, w_out_bf, tm)
    return y.reshape(b, l, D_MODEL), new_rows, new_win, s_new, c_new, n_new, m_new, conv_new


def kernel(x_prompt, x_sample, cache_nsa_kv, state_nsa_win, state_gla, state_mlstm_C, state_mlstm_n,
           state_mlstm_m, state_mlstm_conv, page_table, norm_g, w_in, w_out, gla_w_gate, gla_b_gate,
           gla_norm_g, nsa_q_norm_g, nsa_k_norm_g, nsa_cmp_pos, nsa_cmp_w, ml_conv_w, ml_conv_b,
           ml_gate_b, ml_norm_g):
    bp, sp, _ = x_prompt.shape
    bs, _, _ = x_sample.shape
    depth = w_in.shape[0]
    dt = x_prompt.dtype
    zero_gla = jnp.zeros((bp, GLA_HEADS, GLA_DK, GLA_DV), F32)
    zero_c = jnp.zeros((bp, ML_HEADS, ML_DH, ML_DH), F32)
    zero_n = jnp.zeros((bp, ML_HEADS, ML_DH), F32)
    zero_m = jnp.zeros((bp, ML_HEADS), F32)
    zero_conv = jnp.zeros((bp, CONV_W - 1, 2 * ML_WIDTH), dt)
    keep_p = min(WINDOW, sp)
    keep_s = state_nsa_win.shape[2]
    n_pool = cache_nsa_kv.shape[1]
    cache_t = jnp.transpose(cache_nsa_kv, (0, 1, 3, 4, 5, 2)).reshape(
        depth * n_pool, N_KV_SLOTS * KV_W, cache_nsa_kv.shape[2]).astype(F32)
    win_t = jnp.transpose(state_nsa_win, (0, 1, 3, 4, 5, 2)).reshape(depth * bs, 2 * KV_W, keep_s).astype(F32)

    y_prompt, y_sample = x_prompt, x_sample
    p_layers, s_layers = [], []
    for layer in range(depth):
        w = (norm_g[layer], _pad_w_in(w_in[layer]), w_out[layer].astype(BF16), gla_w_gate[layer],
             gla_b_gate[layer], gla_norm_g[layer],
             nsa_q_norm_g[layer], nsa_k_norm_g[layer], nsa_cmp_pos[layer], nsa_cmp_w[layer],
             ml_conv_w[layer], ml_conv_b[layer], ml_gate_b[layer], ml_norm_g[layer])
        y_prompt, *p_new = _hybrid_layer(y_prompt, None, keep_p, zero_gla, zero_c, zero_n,
                                         zero_m, zero_conv, *w)
        paged = (cache_t, page_table, layer * n_pool, win_t, layer * bs)
        y_sample, *s_new = _hybrid_layer(y_sample, paged, keep_s,
                                         state_gla[layer], state_mlstm_C[layer], state_mlstm_n[layer],
                                         state_mlstm_m[layer], state_mlstm_conv[layer], *w)
        p_layers.append(p_new)
        s_layers.append(s_new)
    p_kv, p_win, p_gla, p_c, p_n, p_m, p_conv = [jnp.stack(z) for z in zip(*p_layers)]
    s_kv, s_win, s_gla, s_c, s_n, s_m, s_conv = [jnp.stack(z) for z in zip(*s_layers)]
    return (y_prompt, y_sample, p_kv, s_kv, p_win, s_win, p_gla, s_gla, p_c, s_c, p_n, s_n, p_m, s_m, p_conv, s_conv)
```

```python
import functools
import math

import jax
import jax.numpy as jnp
import numpy as np
from jax import lax
from jax.experimental import pallas as pl
from jax.experimental.pallas import tpu as pltpu

F32 = jnp.float32
BF16 = jnp.bfloat16
HIGHEST = lax.Precision.HIGHEST

D_MODEL = 1024
HEAD_DIM = 64
GLA_WIDTH = D_MODEL // 4
NSA_WIDTH = D_MODEL // 2
ML_WIDTH = D_MODEL - GLA_WIDTH - NSA_WIDTH
D_MIX = GLA_WIDTH + NSA_WIDTH + ML_WIDTH

GLA_HEADS = GLA_WIDTH // HEAD_DIM
GLA_DK = HEAD_DIM // 2
GLA_DV = HEAD_DIM
GLA_RANK = 16
GLA_TAU = 16.0
GLA_CHUNK = 64

NSA_HEADS = NSA_WIDTH // HEAD_DIM
NSA_KV_HEADS = 2
NSA_HPG = NSA_HEADS // NSA_KV_HEADS
CMP_BLOCK = 32
CMP_STRIDE = 16
SEL_BLOCK = 64
N_SELECT = 16
WINDOW = 512
Q_BLOCK = 128
N_KV_SLOTS = 4
ROT_DIM = HEAD_DIM // 4
ROPE_THETA = 500000.0
ATTN_SCALE = HEAD_DIM ** -0.5

ML_HEADS = ML_WIDTH // HEAD_DIM
ML_DH = HEAD_DIM
ML_CHUNK = 64
CONV_W = 4

SPLIT_SIZES = (GLA_HEADS * GLA_DK, GLA_HEADS * GLA_DK, GLA_WIDTH, GLA_RANK, GLA_WIDTH,
               NSA_WIDTH, 6 * NSA_KV_HEADS * HEAD_DIM, 3 * NSA_HEADS, NSA_WIDTH,
               2 * ML_WIDTH, ML_WIDTH, 2 * ML_HEADS, ML_WIDTH, ML_WIDTH)

LANES = 128
SUBLANES = 8
VMEM_LIMIT = 56 * 1024 * 1024
NEG_BIG = -1e30
EPS = 1e-6
LOG2E = math.log2(math.e)


def _round_up(n, m):
    return -(-n // m) * m


PAD_SIZES = tuple(_round_up(s, LANES) for s in SPLIT_SIZES)
D_IN_PAD = sum(PAD_SIZES)
W_GLA = sum(PAD_SIZES[0:5])
W_NSA = sum(PAD_SIZES[5:9])
W_ML = sum(PAD_SIZES[9:14])
KV_W = NSA_KV_HEADS * HEAD_DIM


def _dot(a, b):
    return jnp.dot(a.astype(BF16), b.astype(BF16), preferred_element_type=F32)


def _dot_nt(a, b):
    return lax.dot_general(a.astype(BF16), b.astype(BF16), (((1,), (1,)), ((), ())), preferred_element_type=F32)


def _dot_tn(a, b):
    return lax.dot_general(a.astype(BF16), b.astype(BF16), (((0,), (0,)), ((), ())), preferred_element_type=F32)


def _dot_f32(a, b):
    return jnp.dot(a, b, precision=HIGHEST, preferred_element_type=F32)


def _dot_split(a, b):
    a_hi = a.astype(BF16)
    a_lo = (a - a_hi.astype(F32)).astype(BF16)
    bb = b.astype(BF16)
    return jnp.dot(a_hi, bb, preferred_element_type=F32) + jnp.dot(a_lo, bb, preferred_element_type=F32)


def _log_sigmoid(x):
    return jnp.minimum(x, 0.0) - jnp.log1p(jnp.exp(-jnp.abs(x)))


def _sigmoid(x):
    return 1.0 / (1.0 + jnp.exp(-x))


def _silu(x):
    return x * _sigmoid(x)


def _group_mean_matrix(width):
    g = np.kron(np.eye(width // HEAD_DIM, dtype=np.float32), np.full((HEAD_DIM, HEAD_DIM), 1.0 / HEAD_DIM, np.float32))
    return jnp.asarray(g, dtype=BF16)


def _group_norm(x, gmat, gain):
    ms = _dot_split(x * x, gmat)
    return x * lax.rsqrt(ms + EPS) * gain


def _cparams(*sem):
    return pltpu.CompilerParams(dimension_semantics=sem, vmem_limit_bytes=VMEM_LIMIT)


def _proj_in_kernel(x_ref, g_ref, w_ref, ug_ref, un_ref, um_ref):
    x = x_ref[...]
    y = x * lax.rsqrt(jnp.mean(x * x, axis=-1, keepdims=True) + EPS) * g_ref[...]
    r = jnp.dot(y.astype(BF16), w_ref[...], preferred_element_type=F32)
    ug_ref[...] = r[:, 0:W_GLA]
    un_ref[...] = r[:, W_GLA:W_GLA + W_NSA]
    um_ref[...] = r[:, W_GLA + W_NSA:D_IN_PAD]


def _proj_in(x2d, g, w_pad, tm):
    rows = x2d.shape[0]
    return pl.pallas_call(
        _proj_in_kernel,
        grid=(rows // tm,),
        in_specs=[pl.BlockSpec((tm, D_MODEL), lambda i: (i, 0)),
                  pl.BlockSpec((1, D_MODEL), lambda i: (0, 0)),
                  pl.BlockSpec((D_MODEL, D_IN_PAD), lambda i: (0, 0))],
        out_specs=[pl.BlockSpec((tm, W_GLA), lambda i: (i, 0)),
                   pl.BlockSpec((tm, W_NSA), lambda i: (i, 0)),
                   pl.BlockSpec((tm, W_ML), lambda i: (i, 0))],
        out_shape=[jax.ShapeDtypeStruct((rows, W_GLA), F32),
                   jax.ShapeDtypeStruct((rows, W_NSA), F32),
                   jax.ShapeDtypeStruct((rows, W_ML), F32)],
        compiler_params=_cparams("parallel"),
        name="proj_in",
    )(x2d, g.reshape(1, D_MODEL), w_pad)


def _pad_w_in(w_in):
    parts = []
    off = 0
    for s, p in zip(SPLIT_SIZES, PAD_SIZES):
        seg = w_in[:, off:off + s]
        if p != s:
            seg = jnp.pad(seg, ((0, 0), (0, p - s)))
        parts.append(seg)
        off += s
    return jnp.concatenate(parts, axis=1).astype(BF16)


def _proj_out_kernel(oa_ref, ocmp_ref, oslc_ref, owin_ref, gz_ref, oc_ref, x_ref, w_ref, e_ref, y_ref):
    gate = _sigmoid(gz_ref[:, 0:LANES])
    ob = (_dot_split(gate, e_ref[0]) * ocmp_ref[...] + _dot_split(gate, e_ref[1]) * oslc_ref[...]
          + _dot_split(gate, e_ref[2]) * owin_ref[...]) * _silu(gz_ref[:, LANES:LANES + NSA_WIDTH])
    y = x_ref[...]
    y = y + _dot(oa_ref[...], w_ref[0:GLA_WIDTH, :])
    y = y + _dot(ob, w_ref[GLA_WIDTH:GLA_WIDTH + NSA_WIDTH, :])
    y = y + _dot(oc_ref[...], w_ref[GLA_WIDTH + NSA_WIDTH:D_MIX, :])
    y_ref[...] = y


def _gate_expand():
    e = np.zeros((3, LANES, NSA_WIDTH), np.float32)
    for j in range(3):
        for h in range(NSA_HEADS):
            e[j, j * NSA_HEADS + h, h * HEAD_DIM:(h + 1) * HEAD_DIM] = 1.0
    return jnp.asarray(e, dtype=BF16)


def _proj_out(oa, o_cmp, o_slc, o_win, u_nsa, oc, x2d, w_bf, tm):
    rows = x2d.shape[0]
    gz_w = LANES + NSA_WIDTH
    gz_blk = (NSA_WIDTH + 6 * KV_W) // gz_w
    assert gz_blk * gz_w == NSA_WIDTH + 6 * KV_W
    row = lambda i: (i, 0)
    return pl.pallas_call(
        _proj_out_kernel,
        grid=(rows // tm,),
        in_specs=[pl.BlockSpec((tm, GLA_WIDTH), row),
                  pl.BlockSpec((tm, NSA_WIDTH), row),
                  pl.BlockSpec((tm, NSA_WIDTH), row),
                  pl.BlockSpec((tm, NSA_WIDTH), row),
                  pl.BlockSpec((tm, gz_w), lambda i: (i, gz_blk)),
                  pl.BlockSpec((tm, ML_WIDTH), row),
                  pl.BlockSpec((tm, D_MODEL), row),
                  pl.BlockSpec((D_MIX, D_MODEL), lambda i: (0, 0)),
                  pl.BlockSpec((3, LANES, NSA_WIDTH), lambda i: (0, 0, 0))],
        out_specs=pl.BlockSpec((tm, D_MODEL), row),
        out_shape=jax.ShapeDtypeStruct((rows, D_MODEL), F32),
        compiler_params=_cparams("parallel"),
        name="proj_out",
    )(oa, o_cmp, o_slc, o_win, u_nsa, oc, x2d, w_bf, _gate_expand())


def _gla_kernel(u_ref, st0_ref, wg_ref, bg_ref, ng_ref, gm_ref, o_ref, st_ref, *, bb, tl, c):
    @pl.when(pl.program_id(1) == 0)
    def _init():
        st_ref[...] = st0_ref[...]

    nk = GLA_HEADS * GLA_DK
    a_off = 2 * nk + GLA_WIDTH
    q, k, v, log_a = [], [], [], []
    for i in range(bb):
        q.append(u_ref[i, :, 0:nk] * (GLA_DK ** -0.5))
        k.append(u_ref[i, :, nk:2 * nk])
        v.append(u_ref[i, :, 2 * nk:2 * nk + GLA_WIDTH])
        pre = _dot(u_ref[i, :, a_off:a_off + LANES], wg_ref[...]) + bg_ref[...]
        log_a.append(_log_sigmoid(pre) * (1.0 / GLA_TAU))

    tril = (lax.broadcasted_iota(jnp.int32, (c, c), 0) >= lax.broadcasted_iota(jnp.int32, (c, c), 1)).astype(F32)
    hc = GLA_HEADS * c
    tri_h = (lax.broadcasted_iota(jnp.int32, (hc, c), 0) % c) >= lax.broadcasted_iota(jnp.int32, (hc, c), 1)
    k_head = lax.broadcasted_iota(jnp.int32, (1, nk), 1) // GLA_DK
    v_head = lax.broadcasted_iota(jnp.int32, (1, GLA_WIDTH), 1) // GLA_DV
    st_diag = (lax.broadcasted_iota(jnp.int32, (GLA_WIDTH, nk), 0) // GLA_DV
               == lax.broadcasted_iota(jnp.int32, (GLA_WIDTH, nk), 1) // GLA_DK)

    st = [st_ref[i] for i in range(bb)]
    outs = [[] for _ in range(bb)]
    for j in range(tl // c):
        sl = slice(j * c, (j + 1) * c)
        for i in range(bb):
            b = _dot_f32(tril, log_a[i][sl])
            blast = b[c - 1:c]
            qe = q[i][sl] * jnp.exp(b)
            ke = k[i][sl] * jnp.exp(-b)
            kl = k[i][sl] * jnp.exp(blast - b)
            vc = v[i][sl]
            qx = jnp.concatenate([jnp.where(k_head == h, qe, 0.0) for h in range(GLA_HEADS)], axis=0)
            a = jnp.where(tri_h, _dot_nt(qx, ke), 0.0)
            r = _dot(a, vc)
            o = _dot_nt(qe, st[i])
            for h in range(GLA_HEADS):
                o = o + jnp.where(v_head == h, r[h * c:(h + 1) * c], 0.0)
            st[i] = st[i] * jnp.exp(blast) + jnp.where(st_diag, _dot_tn(vc, kl), 0.0)
            outs[i].append(o)
    for i in range(bb):
        st_ref[i] = st[i]
        o = outs[i][0] if len(outs[i]) == 1 else jnp.concatenate(outs[i], axis=0)
        z = u_ref[i, :, a_off + LANES:a_off + LANES + GLA_WIDTH]
        o_ref[i] = _group_norm(o, gm_ref[...], ng_ref[...]) * _silu(z)


GLA_SEQS_PER_STEP = 4


def _gla(u_gla, st0, wg, bg, ng, b, l):
    tl = min(l, 128)
    c = min(l, 16)
    bb = math.gcd(b, GLA_SEQS_PER_STEP)
    nk = GLA_HEADS * GLA_DK
    u3 = u_gla.reshape(b, l, W_GLA)
    kern = functools.partial(_gla_kernel, bb=bb, tl=tl, c=c)
    return pl.pallas_call(
        kern,
        grid=(b // bb, l // tl),
        in_specs=[pl.BlockSpec((bb, tl, W_GLA), lambda bi, li: (bi, li, 0)),
                  pl.BlockSpec((bb, GLA_WIDTH, nk), lambda bi, li: (bi, 0, 0)),
                  pl.BlockSpec((LANES, nk), lambda bi, li: (0, 0)),
                  pl.BlockSpec((1, nk), lambda bi, li: (0, 0)),
                  pl.BlockSpec((1, GLA_WIDTH), lambda bi, li: (0, 0)),
                  pl.BlockSpec((GLA_WIDTH, GLA_WIDTH), lambda bi, li: (0, 0))],
        out_specs=[pl.BlockSpec((bb, tl, GLA_WIDTH), lambda bi, li: (bi, li, 0)),
                   pl.BlockSpec((bb, GLA_WIDTH, nk), lambda bi, li: (bi, 0, 0))],
        out_shape=[jax.ShapeDtypeStruct((b, l, GLA_WIDTH), F32),
                   jax.ShapeDtypeStruct((b, GLA_WIDTH, nk), F32)],
        compiler_params=_cparams("parallel", "arbitrary"),
        name="gla",
    )(u3, st0, wg, bg, ng, _group_mean_matrix(GLA_WIDTH))


def _gla_state_in(s):
    b = s.shape[0]
    st = jnp.swapaxes(s.astype(F32), 2, 3)
    eye = jnp.eye(GLA_HEADS, dtype=F32)
    full = st[:, :, :, None, :] * eye[None, :, None, :, None]
    return full.reshape(b, GLA_WIDTH, GLA_HEADS * GLA_DK)


def _gla_state_out(st):
    b = st.shape[0]
    full = st.reshape(b, GLA_HEADS, GLA_DV, GLA_HEADS, GLA_DK)
    diag = jnp.stack([full[:, h, :, h, :] for h in range(GLA_HEADS)], axis=1)
    return jnp.swapaxes(diag, 2, 3)


def _mlstm_kernel(u_ref, c0_ref, n0_ref, m0_ref, cv0_ref, cw_ref, cb_ref, gb_ref, ng_ref, gm_ref,
                  o_ref, c_ref, n_ref, m_ref, cv_ref, xp_sc, *, bb, tl):
    @pl.when(pl.program_id(1) == 0)
    def _init():
        c_ref[...] = c0_ref[...]
        n_ref[...] = n0_ref[...]
        m_ref[...] = m0_ref[...]
        xp_sc[:, 0:SUBLANES, :] = cv0_ref[...]

    pairs = ML_HEADS // 2
    state = [([c_ref[i, j] for j in range(pairs)],
              [n_ref[i, j:j + 1, :] for j in range(pairs)],
              [m_ref[i, h:h + 1, :] for h in range(ML_HEADS)]) for i in range(bb)]
    new_state = [_mlstm_seq(i, state[i], u_ref, cw_ref, cb_ref, gb_ref, ng_ref, gm_ref, o_ref, cv_ref, xp_sc, tl)
                 for i in range(bb)]
    for i in range(bb):
        cps, nps, mbs = new_state[i]
        for j in range(pairs):
            c_ref[i, j] = cps[j]
            n_ref[i, j:j + 1, :] = nps[j]
        for h in range(ML_HEADS):
            m_ref[i, h:h + 1, :] = mbs[h]


def _mlstm_seq(i, state, u_ref, cw_ref, cb_ref, gb_ref, ng_ref, gm_ref, o_ref, cv_ref, xp_sc, tl):
    cps, nps, mbs = state
    c = tl
    w2 = 2 * ML_WIDTH
    u = u_ref[i]
    xp_sc[i, SUBLANES:SUBLANES + tl, :] = u[:, 0:w2]
    conv = cb_ref[...]
    for w in range(CONV_W):
        off = SUBLANES - (CONV_W - 1) + w
        conv = conv + xp_sc[i, off:off + tl, :] * cw_ref[w:w + 1, :]
    tail = xp_sc[i, tl:tl + SUBLANES, :]
    xp_sc[i, 0:SUBLANES, :] = tail
    cv_ref[i] = tail

    qk = _silu(conv)
    mq = qk[:, 0:ML_WIDTH]
    mk = qk[:, ML_WIDTH:w2] * (ML_DH ** -0.5)
    mv = u[:, w2:w2 + ML_WIDTH]
    ifg = u[:, w2 + ML_WIDTH:w2 + ML_WIDTH + LANES] + gb_ref[...]
    logf = _log_sigmoid(ifg)
    og_off = w2 + ML_WIDTH + LANES
    og = _sigmoid(u[:, og_off:og_off + ML_WIDTH])
    zz = _silu(u[:, og_off + ML_WIDTH:og_off + 2 * ML_WIDTH])

    tri = lax.broadcasted_iota(jnp.int32, (c, c), 0) >= lax.broadcasted_iota(jnp.int32, (c, c), 1)
    lane = lax.broadcasted_iota(jnp.int32, (c, LANES), 1)
    low = lane < ML_DH
    low_row = lax.broadcasted_iota(jnp.int32, (1, LANES), 1) < ML_DH
    sq_row = lax.broadcasted_iota(jnp.int32, (LANES, LANES), 0)
    sq_col = lax.broadcasted_iota(jnp.int32, (LANES, LANES), 1)
    same_head = (sq_row < ML_DH) == (sq_col < ML_DH)

    def wide(x):
        return x[:, :c] if c <= LANES else jnp.concatenate([x] * (c // LANES), axis=1)

    fcum_all = _dot_f32(tri.astype(F32), logf)
    gates_t = jnp.where(lane < ML_HEADS, ifg, fcum_all).T
    new_cps, new_nps, new_mbs = [], [], []
    for j in range(ML_HEADS // 2):
        ps = slice(j * LANES, (j + 1) * LANES)
        q_s, k_s, v_s = mq[:, ps], mk[:, ps], mv[:, ps]
        acc = jnp.zeros((c, 2 * LANES), F32)
        per_head = []
        for hl in range(2):
            h = 2 * j + hl
            own = low if hl == 0 else jnp.logical_not(low)
            fc = jnp.broadcast_to(fcum_all[:, ML_HEADS + h:ML_HEADS + h + 1], (c, LANES))
            ii = jnp.broadcast_to(ifg[:, h:h + 1], (c, LANES))
            dm = jnp.where(tri, wide(fc) - gates_t[ML_HEADS + h:ML_HEADS + h + 1, :] + gates_t[h:h + 1, :], NEG_BIG)
            inter = fc + mbs[h]
            m = jnp.maximum(inter, jnp.max(dm, axis=-1, keepdims=True))
            sij = _dot_nt(jnp.where(own, q_s, 0.0), k_s) * jnp.exp(dm - wide(m))
            acc = acc + _dot(sij, jnp.concatenate([jnp.where(own, v_s, 0.0), jnp.where(own, 1.0, 0.0)], axis=1))
            m_last = m[c - 1:c]
            f_last = fc[c - 1:c]
            per_head.append((m, jnp.exp(inter - m), jnp.exp(f_last - fc + ii - m_last),
                             jnp.exp(f_last + mbs[h] - m_last)))
            new_mbs.append(m_last)
        m_p, w_p, wj_p = (jnp.where(low, per_head[0][t], per_head[1][t]) for t in range(3))
        dec_row = jnp.where(low_row, per_head[0][3], per_head[1][3])
        n_mat = jnp.where(same_head, jnp.broadcast_to(nps[j], (LANES, LANES)), 0.0)
        num = w_p * _dot_nt(q_s, cps[j]) + acc[:, 0:LANES]
        den = w_p * _dot_nt(q_s, n_mat) + acc[:, LANES:2 * LANES]
        hh = num / jnp.maximum(jnp.abs(den), jnp.exp(-m_p))
        hn = hh * lax.rsqrt(_dot_split(hh * hh, gm_ref[...]) + EPS) * ng_ref[...]
        o_ref[i, :, ps] = hn * og[:, ps] * zz[:, ps]
        dec_mat = jnp.where(sq_row < ML_DH, jnp.broadcast_to(per_head[0][3], (LANES, LANES)),
                            jnp.broadcast_to(per_head[1][3], (LANES, LANES)))
        new_cps.append(dec_mat * cps[j] + jnp.where(same_head, _dot_tn(wj_p * v_s, k_s), 0.0))
        new_nps.append(dec_row * nps[j] + jnp.sum(wj_p * k_s, axis=0, keepdims=True))
    return new_cps, new_nps, new_mbs


ML_SEQS_PER_STEP = 2


def _mlstm(u_ml, c0, n0, m0, conv0, cw, cb, gb, ng, b, l):
    tl = min(l, 256)
    bb = math.gcd(b, ML_SEQS_PER_STEP)
    pairs = ML_HEADS // 2
    w2 = 2 * ML_WIDTH
    u3 = u_ml.reshape(b, l, W_ML)
    eye2 = jnp.eye(2, dtype=F32)
    c0p = jnp.einsum('bphed,hk->bphekd', c0.astype(F32).reshape(b, pairs, 2, ML_DH, ML_DH), eye2)
    c0p = c0p.reshape(b, pairs, LANES, LANES)
    n0p = n0.astype(F32).reshape(b, pairs, LANES)
    m0b = jnp.broadcast_to(m0.astype(F32)[:, :, None], (b, ML_HEADS, LANES))
    cv0 = jnp.pad(conv0.astype(F32), ((0, 0), (SUBLANES - (CONV_W - 1), 0), (0, 0)))
    gbp = jnp.pad(gb.astype(F32).reshape(1, 2 * ML_HEADS), ((0, 0), (0, LANES - 2 * ML_HEADS)))
    kern = functools.partial(_mlstm_kernel, bb=bb, tl=tl)
    st = lambda bi, li: (bi, 0, 0)
    st4 = lambda bi, li: (bi, 0, 0, 0)
    cst = lambda bi, li: (0, 0)
    o, c_new, n_new, m_new, cv = pl.pallas_call(
        kern,
        grid=(b // bb, l // tl),
        in_specs=[pl.BlockSpec((bb, tl, W_ML), lambda bi, li: (bi, li, 0)),
                  pl.BlockSpec((bb, pairs, LANES, LANES), st4),
                  pl.BlockSpec((bb, pairs, LANES), st),
                  pl.BlockSpec((bb, ML_HEADS, LANES), st),
                  pl.BlockSpec((bb, SUBLANES, w2), st),
                  pl.BlockSpec((CONV_W, w2), cst),
                  pl.BlockSpec((1, w2), cst),
                  pl.BlockSpec((1, LANES), cst),
                  pl.BlockSpec((1, LANES), cst),
                  pl.BlockSpec((LANES, LANES), cst)],
        out_specs=[pl.BlockSpec((bb, tl, ML_WIDTH), lambda bi, li: (bi, li, 0)),
                   pl.BlockSpec((bb, pairs, LANES, LANES), st4),
                   pl.BlockSpec((bb, pairs, LANES), st),
                   pl.BlockSpec((bb, ML_HEADS, LANES), st),
                   pl.BlockSpec((bb, SUBLANES, w2), st)],
        out_shape=[jax.ShapeDtypeStruct((b, l, ML_WIDTH), F32),
                   jax.ShapeDtypeStruct((b, pairs, LANES, LANES), F32),
                   jax.ShapeDtypeStruct((b, pairs, LANES), F32),
                   jax.ShapeDtypeStruct((b, ML_HEADS, LANES), F32),
                   jax.ShapeDtypeStruct((b, SUBLANES, w2), F32)],
        scratch_shapes=[pltpu.VMEM((bb, tl + 2 * SUBLANES, w2), F32)],
        compiler_params=_cparams("parallel", "arbitrary"),
        name="mlstm",
    )(u3, c0p, n0p, m0b, cv0, cw.astype(F32), cb.astype(F32).reshape(1, w2), gbp,
      jnp.tile(ng.astype(F32).reshape(1, ML_DH), (1, 2)), _group_mean_matrix(LANES))
    c6 = c_new.reshape(b, pairs, 2, ML_DH, 2, ML_DH)
    c_out = jnp.stack([c6[:, :, h, :, h, :] for h in range(2)], axis=2).reshape(b, ML_HEADS, ML_DH, ML_DH)
    return (o, c_out, n_new.reshape(b, ML_HEADS, ML_DH), m_new[:, :, 0], cv[:, SUBLANES - (CONV_W - 1):, :])


def _rope_lanes(x, cos_t, sin_t):
    w = x.shape[1]
    half = ROT_DIM // 2
    reps = w // cos_t.shape[1]
    if reps > 1:
        cos_t = jnp.concatenate([cos_t] * reps, axis=1)
        sin_t = jnp.concatenate([sin_t] * reps, axis=1)
    lane = lax.broadcasted_iota(jnp.int32, x.shape, 1) % HEAD_DIM
    partner = jnp.where(lane < half, pltpu.roll(x, w - half, 1), pltpu.roll(x, half, 1))
    return x * cos_t + partner * sin_t


def _nsa_prep_kernel(u_ref, cos_ref, sin_ref, qg_ref, kg_ref, g4_ref, g1_ref, qn_ref, qr_ref, rows_ref, win_ref):
    u = u_ref[...]
    cos_t = cos_ref[...]
    sin_t = sin_ref[...]
    q = _group_norm(u[:, 0:NSA_WIDTH], g4_ref[...], qg_ref[...])
    qn_ref[...] = (q * ATTN_SCALE).astype(BF16)
    qr = _rope_lanes(q, cos_t, sin_t) * (ATTN_SCALE * LOG2E)
    low = lax.broadcasted_iota(jnp.int32, (qr.shape[0], LANES), 1) < HEAD_DIM
    for j in range(NSA_HEADS // 2):
        pair = qr[:, j * LANES:(j + 1) * LANES]
        swapped = pltpu.roll(pair, HEAD_DIM, 1)
        qr_ref[2 * j] = jnp.where(low, pair, swapped).astype(BF16)
        qr_ref[2 * j + 1] = jnp.where(low, swapped, pair).astype(BF16)
    kv = NSA_WIDTH
    k_slc = _rope_lanes(_group_norm(u[:, kv + 2 * KV_W:kv + 3 * KV_W], g1_ref[...], kg_ref[1:2, :]), cos_t, sin_t)
    k_win = _rope_lanes(_group_norm(u[:, kv + 4 * KV_W:kv + 5 * KV_W], g1_ref[...], kg_ref[2:3, :]), cos_t, sin_t)
    rows_ref[:, 0:2 * KV_W] = u[:, kv:kv + 2 * KV_W]
    rows_ref[:, 2 * KV_W:3 * KV_W] = k_slc
    rows_ref[:, 3 * KV_W:4 * KV_W] = u[:, kv + 3 * KV_W:kv + 4 * KV_W]
    win_ref[:, 0:KV_W] = k_win
    win_ref[:, KV_W:2 * KV_W] = u[:, kv + 5 * KV_W:kv + 6 * KV_W]


def _rope_tables(pos):
    half = ROT_DIM // 2
    inv = jnp.exp(-math.log(ROPE_THETA) * jnp.arange(half, dtype=F32) * 2.0 / ROT_DIM)
    ang = pos.astype(F32)[:, None] * inv[None, :]
    cos, sin = jnp.cos(ang), jnp.sin(ang)
    n = pos.shape[0]
    ones = jnp.ones((n, HEAD_DIM - ROT_DIM), F32)
    cos_h = jnp.concatenate([cos, cos, ones], axis=1)
    sin_h = jnp.concatenate([-sin, sin, 0.0 * ones], axis=1)
    reps = LANES // HEAD_DIM
    return jnp.tile(cos_h, (1, reps)), jnp.tile(sin_h, (1, reps))


def _nsa_prep(u_nsa, pos, q_g, k_g, l):
    rows = u_nsa.shape[0]
    tl = min(l, 256)
    nb = l // tl
    cos_t, sin_t = _rope_tables(pos)
    qg = jnp.tile(q_g.astype(F32).reshape(1, HEAD_DIM), (1, NSA_HEADS))
    kg = jnp.tile(k_g.astype(F32), (1, NSA_KV_HEADS))
    cst = lambda i: (0, 0)
    return pl.pallas_call(
        _nsa_prep_kernel,
        grid=(rows // tl,),
        in_specs=[pl.BlockSpec((tl, W_NSA), lambda i: (i, 0)),
                  pl.BlockSpec((tl, LANES), lambda i: (i % nb, 0)),
                  pl.BlockSpec((tl, LANES), lambda i: (i % nb, 0)),
                  pl.BlockSpec((1, NSA_WIDTH), cst),
                  pl.BlockSpec((3, KV_W), cst),
                  pl.BlockSpec((NSA_WIDTH, NSA_WIDTH), cst),
                  pl.BlockSpec((KV_W, KV_W), cst)],
        out_specs=[pl.BlockSpec((tl, NSA_WIDTH), lambda i: (i, 0)),
                   pl.BlockSpec((NSA_HEADS, tl, LANES), lambda i: (0, i, 0)),
                   pl.BlockSpec((tl, N_KV_SLOTS * KV_W), lambda i: (i, 0)),
                   pl.BlockSpec((tl, 2 * KV_W), lambda i: (i, 0))],
        out_shape=[jax.ShapeDtypeStruct((rows, NSA_WIDTH), BF16),
                   jax.ShapeDtypeStruct((NSA_HEADS, rows, LANES), BF16),
                   jax.ShapeDtypeStruct((rows, N_KV_SLOTS * KV_W), F32),
                   jax.ShapeDtypeStruct((rows, 2 * KV_W), F32)],
        compiler_params=_cparams("parallel"),
        name="nsa_prep",
    )(u_nsa, cos_t, sin_t, qg, kg, _group_mean_matrix(NSA_WIDTH), _group_mean_matrix(KV_W))


def _cmp_halves(xk_ref, xv_ref, w_ref, n_half):
    acc_k = jnp.zeros((n_half, 2 * KV_W), F32)
    acc_v = jnp.zeros((n_half, 2 * KV_W), F32)
    for s in range(CMP_STRIDE):
        acc_k = acc_k + _dot(xk_ref[pl.ds(s, n_half, stride=CMP_STRIDE), :], w_ref[0, s])
        acc_v = acc_v + _dot(xv_ref[pl.ds(s, n_half, stride=CMP_STRIDE), :], w_ref[1, s])
    return acc_k, acc_v


def _cmp_finish(acc_k, acc_v, bias_ref, kg_ref, g1_ref, kc_ref, vc_ref, n_half):
    valid = lax.broadcasted_iota(jnp.int32, (n_half, KV_W), 0) < n_half - 1

    def summary(acc, bias):
        return acc[:, 0:KV_W] + pltpu.roll(acc[:, KV_W:2 * KV_W], n_half - 1, 0) + bias

    kc = _group_norm(summary(acc_k, bias_ref[0:1, :]), g1_ref[...], kg_ref[...])
    kc_ref[0] = jnp.where(valid, kc, 0.0).astype(BF16)
    vc_ref[0] = jnp.where(valid, summary(acc_v, bias_ref[1:2, :]), 0.0).astype(BF16)


def _cmp_kv_kernel(xk_ref, xv_ref, w_ref, bias_ref, kg_ref, g1_ref, kc_ref, vc_ref, *, n_half):
    acc_k, acc_v = _cmp_halves(xk_ref.at[0], xv_ref.at[0], w_ref, n_half)
    _cmp_finish(acc_k, acc_v, bias_ref, kg_ref, g1_ref, kc_ref, vc_ref, n_half)


PAGES_PER_STEP = 64


def _cmp_paged_kernel(pt_ref, *refs, npg, page, whole):
    page_refs = refs[:npg]
    if whole:
        w_ref, bias_ref, kg_ref, g1_ref, kc_ref, vc_ref, xk_sc, xv_sc = refs[npg:]
    else:
        w_ref, acck_ref, accv_ref, xk_sc, xv_sc = refs[npg:]
    for i in range(npg):
        xk_sc[i * page:(i + 1) * page, :] = page_refs[i][0, 0:KV_W, :].T
        xv_sc[i * page:(i + 1) * page, :] = page_refs[i][0, KV_W:2 * KV_W, :].T
    n_half = npg * page // CMP_STRIDE
    acc_k, acc_v = _cmp_halves(xk_sc, xv_sc, w_ref, n_half)
    if whole:
        _cmp_finish(acc_k, acc_v, bias_ref, kg_ref, g1_ref, kc_ref, vc_ref, n_half)
    else:
        acck_ref[0] = acc_k
        accv_ref[0] = acc_v


def _cmp_fin_kernel(acck_ref, accv_ref, bias_ref, kg_ref, g1_ref, kc_ref, vc_ref, *, n_half):
    _cmp_finish(acck_ref[0], accv_ref[0], bias_ref, kg_ref, g1_ref, kc_ref, vc_ref, n_half)


def _page_specs(npg, page, row_blk, pool_off):
    def spec(i):
        return pl.BlockSpec((1, 2 * KV_W, page), lambda bi, ji, pt: (pt[bi, ji * npg + i] + pool_off, row_blk, 0))
    return [spec(i) for i in range(npg)]


def _cmp_kv_paged(cache_t, page_table, pool_off, wcat, bias, kg0):
    b, n_pages = page_table.shape
    page = cache_t.shape[2]
    npg = math.gcd(n_pages, PAGES_PER_STEP)
    nh_step = npg * page // CMP_STRIDE
    n_half = n_pages * page // CMP_STRIDE
    kg = jnp.tile(kg0.astype(F32).reshape(1, HEAD_DIM), (1, NSA_KV_HEADS))
    if npg == n_pages:
        cst = lambda bi, ji, pt: (0, 0)
        return pl.pallas_call(
            functools.partial(_cmp_paged_kernel, npg=npg, page=page, whole=True),
            grid_spec=pltpu.PrefetchScalarGridSpec(
                num_scalar_prefetch=1, grid=(b, 1),
                in_specs=_page_specs(npg, page, 0, pool_off)
                + [pl.BlockSpec((2, CMP_STRIDE, KV_W, 2 * KV_W), lambda bi, ji, pt: (0, 0, 0, 0)),
                   pl.BlockSpec((2, KV_W), cst), pl.BlockSpec((1, KV_W), cst), pl.BlockSpec((KV_W, KV_W), cst)],
                out_specs=[pl.BlockSpec((1, n_half, KV_W), lambda bi, ji, pt: (bi, 0, 0)),
                           pl.BlockSpec((1, n_half, KV_W), lambda bi, ji, pt: (bi, 0, 0))],
                scratch_shapes=[pltpu.VMEM((npg * page, KV_W), F32), pltpu.VMEM((npg * page, KV_W), F32)]),
            out_shape=[jax.ShapeDtypeStruct((b, n_half, KV_W), BF16),
                       jax.ShapeDtypeStruct((b, n_half, KV_W), BF16)],
            compiler_params=_cparams("parallel", "arbitrary"),
            name="cmp_paged",
        )(page_table, *([cache_t] * npg), wcat, bias, kg, _group_mean_matrix(KV_W))
    kern = functools.partial(_cmp_paged_kernel, npg=npg, page=page, whole=False)
    acc_k, acc_v = pl.pallas_call(
        kern,
        grid_spec=pltpu.PrefetchScalarGridSpec(
            num_scalar_prefetch=1, grid=(b, n_pages // npg),
            in_specs=_page_specs(npg, page, 0, pool_off)
            + [pl.BlockSpec((2, CMP_STRIDE, KV_W, 2 * KV_W), lambda bi, ji, pt: (0, 0, 0, 0))],
            out_specs=[pl.BlockSpec((1, nh_step, 2 * KV_W), lambda bi, ji, pt: (bi, ji, 0)),
                       pl.BlockSpec((1, nh_step, 2 * KV_W), lambda bi, ji, pt: (bi, ji, 0))],
            scratch_shapes=[pltpu.VMEM((npg * page, KV_W), F32), pltpu.VMEM((npg * page, KV_W), F32)]),
        out_shape=[jax.ShapeDtypeStruct((b, n_half, 2 * KV_W), F32),
                   jax.ShapeDtypeStruct((b, n_half, 2 * KV_W), F32)],
        compiler_params=_cparams("parallel", "arbitrary"),
        name="cmp_paged",
    )(page_table, *([cache_t] * npg), wcat)
    blk = lambda bi: (bi, 0, 0)
    return pl.pallas_call(
        functools.partial(_cmp_fin_kernel, n_half=n_half),
        grid=(b,),
        in_specs=[pl.BlockSpec((1, n_half, 2 * KV_W), blk),
                  pl.BlockSpec((1, n_half, 2 * KV_W), blk),
                  pl.BlockSpec((2, KV_W), lambda bi: (0, 0)),
                  pl.BlockSpec((1, KV_W), lambda bi: (0, 0)),
                  pl.BlockSpec((KV_W, KV_W), lambda bi: (0, 0))],
        out_specs=[pl.BlockSpec((1, n_half, KV_W), blk), pl.BlockSpec((1, n_half, KV_W), blk)],
        out_shape=[jax.ShapeDtypeStruct((b, n_half, KV_W), BF16),
                   jax.ShapeDtypeStruct((b, n_half, KV_W), BF16)],
        compiler_params=_cparams("parallel"),
        name="cmp_fin",
    )(acc_k, acc_v, bias, kg, _group_mean_matrix(KV_W))


def _cmp_weights(cmp_pos, cmp_w):
    wf = cmp_w.astype(F32)
    eye_g = jnp.eye(NSA_KV_HEADS, dtype=F32)

    def bd(w):
        return jnp.einsum('ksde,gh->ksgdhe', w, eye_g).reshape(2, CMP_STRIDE, KV_W, KV_W)

    wcat = jnp.concatenate([bd(wf[:, :CMP_STRIDE]), bd(wf[:, CMP_STRIDE:])], axis=3).astype(BF16)
    bias = jnp.einsum('ksd,ksde->ke', cmp_pos.astype(F32), wf)
    return wcat, jnp.tile(bias, (1, NSA_KV_HEADS))


def _cmp_kv(x3, k_blk, v_blk, t_use, wcat, bias, kg0):
    b = x3.shape[0]
    n_half = t_use // CMP_STRIDE
    kern = functools.partial(_cmp_kv_kernel, n_half=n_half)
    kg = jnp.tile(kg0.astype(F32).reshape(1, HEAD_DIM), (1, NSA_KV_HEADS))
    return pl.pallas_call(
        kern,
        grid=(b,),
        in_specs=[pl.BlockSpec((1, t_use, KV_W), lambda bi: (bi, 0, k_blk)),
                  pl.BlockSpec((1, t_use, KV_W), lambda bi: (bi, 0, v_blk)),
                  pl.BlockSpec((2, CMP_STRIDE, KV_W, 2 * KV_W), lambda bi: (0, 0, 0, 0)),
                  pl.BlockSpec((2, KV_W), lambda bi: (0, 0)),
                  pl.BlockSpec((1, KV_W), lambda bi: (0, 0)),
                  pl.BlockSpec((KV_W, KV_W), lambda bi: (0, 0))],
        out_specs=[pl.BlockSpec((1, n_half, KV_W), lambda bi: (bi, 0, 0)),
                   pl.BlockSpec((1, n_half, KV_W), lambda bi: (bi, 0, 0))],
        out_shape=[jax.ShapeDtypeStruct((b, n_half, KV_W), BF16),
                   jax.ShapeDtypeStruct((b, n_half, KV_W), BF16)],
        compiler_params=_cparams("parallel"),
        name="cmp_kv",
    )(x3, x3, wcat, bias, kg, _group_mean_matrix(KV_W))


def _cmp_attn_kernel(qn_ref, kc_ref, vc_ref, ov_ref, o_ref, sel_ref, *, bb, tq, n_half, n_cmp, n_sel, nbp, pos0,
                     bias_out):
    biases = []
    qi = pl.program_id(1)
    nq = bb * tq
    pos_c = pos0 + qi * tq + lax.broadcasted_iota(jnp.int32, (tq, 1), 0)
    ncol = lax.broadcasted_iota(jnp.int32, (1, n_half), 1)
    cmask = (ncol * CMP_STRIDE + (CMP_BLOCK - 1) <= pos_c) & (ncol < n_cmp)
    pos_r = pos0 + qi * tq + lax.broadcasted_iota(jnp.int32, (1, nq), 1) % tq
    blk = lax.broadcasted_iota(jnp.int32, (nbp, 1), 0)
    cur = pos_r // SEL_BLOCK
    forced = (blk == 0) | (blk == cur) | (blk == cur - 1)
    valid = blk * SEL_BLOCK <= pos_r
    real = blk < n_sel
    for g in range(NSA_KV_HEADS):
        gs = slice(g * HEAD_DIM, (g + 1) * HEAD_DIM)
        psums = []
        for i in range(bb):
            kc = kc_ref[i, :, gs]
            vc = vc_ref[i, :, gs]
            psum = jnp.zeros((tq, n_half), F32)
            for h in range(NSA_HPG):
                hs = slice((g * NSA_HPG + h) * HEAD_DIM, (g * NSA_HPG + h + 1) * HEAD_DIM)
                s = _dot_nt(qn_ref[i, :, hs], kc)
                m = jnp.max(jnp.where(cmask, s, NEG_BIG), axis=-1, keepdims=True)
                m = jnp.where(m > 0.5 * NEG_BIG, m, 0.0)
                e = jnp.where(cmask, jnp.exp(s - m), 0.0)
                p = e / jnp.maximum(jnp.sum(e, axis=-1, keepdims=True), 1e-30)
                o_ref[i, :, hs] = _dot(p, vc)
                psum = psum + p
            psums.append(psum)
        psum = psums[0] if bb == 1 else jnp.concatenate(psums, axis=0)
        p_hi = psum.astype(BF16)
        p_lo = (psum - p_hi.astype(F32)).astype(BF16)
        ov = ov_ref[...]
        imp = _dot_nt(ov, p_hi) + _dot_nt(ov, p_lo)
        score = jnp.where(forced, 3e38, jnp.where(valid, imp, -1e38))
        score = jnp.where(real, score, -3e38)
        cnt = jnp.zeros((nbp, nq), F32)
        for jp in range(n_sel):
            rowv = score[jp:jp + 1, :]
            beats = (rowv > score) | ((rowv == score) & (blk > jp))
            cnt = cnt + jnp.where(beats, 1.0, 0.0)
        sel = (cnt < float(min(N_SELECT, n_sel))) & real
        if bias_out:
            sbg = jnp.where(sel, 0.0, NEG_BIG)
            if nbp < HEAD_DIM:
                sbg = jnp.concatenate([sbg, jnp.full((HEAD_DIM - nbp, nq), NEG_BIG, F32)], axis=0)
            biases.append(sbg)
        else:
            sel_ref[0, g * nbp:(g + 1) * nbp, :] = jnp.where(sel, 1.0, 0.0).astype(BF16)
    if bias_out:
        sel_ref[0] = jnp.concatenate(biases[::-1], axis=0).T.astype(BF16)


def _overlap_t(n_cmp, n_sel, n_half, nbp):
    a = SEL_BLOCK // CMP_STRIDE
    bb = CMP_BLOCK // CMP_STRIDE
    i = np.arange(n_half)[None, :]
    j = np.arange(nbp)[:, None]
    s = i - a * j + (bb - 1)
    cnt = np.maximum(np.minimum(np.minimum(s + 1, a + bb - 1 - s), min(a, bb)), 0)
    cnt = np.where((i < n_cmp) & (j < n_sel), cnt, 0)
    return jnp.asarray(cnt, dtype=BF16)


CMP_TQ = 256


def _cmp_attn(qn3, kc, vc, pos0, t_len, bias_out=False):
    b, l, _ = qn3.shape
    n_half = kc.shape[1]
    n_cmp = n_half - 1
    n_sel = -(-t_len // SEL_BLOCK)
    nbp = _round_up(n_sel, 16)
    tq = min(l, CMP_TQ)
    bb = math.gcd(b, max(LANES // tq, 1))
    kern = functools.partial(_cmp_attn_kernel, bb=bb, tq=tq, n_half=n_half, n_cmp=n_cmp, n_sel=n_sel, nbp=nbp,
                             pos0=pos0, bias_out=bias_out)
    if bias_out:
        assert bb == 1 and NSA_KV_HEADS == 2 and nbp <= HEAD_DIM
        sel_spec = pl.BlockSpec((1, tq, LANES), lambda bi, qi: (bi, qi, 0))
        sel_shape = jax.ShapeDtypeStruct((b, l, LANES), BF16)
    else:
        sel_spec = pl.BlockSpec((1, NSA_KV_HEADS * nbp, bb * tq), lambda bi, qi: (bi, 0, qi))
        sel_shape = jax.ShapeDtypeStruct((b // bb, NSA_KV_HEADS * nbp, bb * l), BF16)
    o_cmp, sel_t = pl.pallas_call(
        kern,
        grid=(b // bb, l // tq),
        in_specs=[pl.BlockSpec((bb, tq, NSA_WIDTH), lambda bi, qi: (bi, qi, 0)),
                  pl.BlockSpec((bb, n_half, KV_W), lambda bi, qi: (bi, 0, 0)),
                  pl.BlockSpec((bb, n_half, KV_W), lambda bi, qi: (bi, 0, 0)),
                  pl.BlockSpec((nbp, n_half), lambda bi, qi: (0, 0))],
        out_specs=[pl.BlockSpec((bb, tq, NSA_WIDTH), lambda bi, qi: (bi, qi, 0)), sel_spec],
        out_shape=[jax.ShapeDtypeStruct((b, l, NSA_WIDTH), F32), sel_shape],
        compiler_params=_cparams("parallel", "parallel"),
        name="cmp_attn",
    )(qn3, kc, vc, _overlap_t(n_cmp, n_sel, n_half, nbp))
    if bb > 1:
        sel_t = sel_t.reshape(b // bb, NSA_KV_HEADS * nbp, bb, l)
        sel_t = jnp.swapaxes(sel_t, 1, 2).reshape(b, NSA_KV_HEADS * nbp, l)
    return o_cmp, sel_t


SEL_TQ = 128
SEL_TK = 2048
WIN_TQ = 256


def _group_lanes(shape, g):
    lane = lax.broadcasted_iota(jnp.int32, shape, len(shape) - 1)
    return (lane < HEAD_DIM) if g == 0 else (lane >= HEAD_DIM)


def _store_heads(o_ref, r, g, tq):
    lo = g * HEAD_DIM
    for h in range(NSA_HPG):
        hh = g * NSA_HPG + h
        o_ref[0, :, hh * HEAD_DIM:(hh + 1) * HEAD_DIM] = r[h * tq:(h + 1) * tq, lo:lo + HEAD_DIM]


def _flash_sel_kernel(qi_ref, ki_ref, q_ref, k_ref, v_ref, sb_ref, o_ref, m_sc, acc_sc, *, tq, tk):
    qi = qi_ref[pl.program_id(1)]
    ki = ki_ref[pl.program_id(1)]
    kmax = (qi * tq + tq - 1) // tk
    hq = NSA_HPG * tq

    @pl.when(ki == 0)
    def _init():
        m_sc[...] = jnp.full(m_sc.shape, NEG_BIG, F32)
        acc_sc[...] = jnp.zeros(acc_sc.shape, F32)

    def step(diagonal):
        key_blk = ki * (tk // SEL_BLOCK) + lax.broadcasted_iota(jnp.int32, (tk, LANES), 0) // SEL_BLOCK
        one_hot = jnp.where(lax.broadcasted_iota(jnp.int32, (tk, LANES), 1) % HEAD_DIM == key_blk, 1.0, 0.0).astype(BF16)
        kblk = k_ref[0].astype(BF16)
        vblk = v_ref[0].astype(BF16)
        sb4 = jnp.concatenate([sb_ref[0]] * NSA_HPG, axis=0)
        if diagonal:
            qpos = qi * tq + lax.broadcasted_iota(jnp.int32, (tq, tk), 0)
            kpos = ki * tk + lax.broadcasted_iota(jnp.int32, (tq, tk), 1)
            causal4 = jnp.concatenate([jnp.where(kpos <= qpos, 0.0, NEG_BIG)] * NSA_HPG, axis=0)
        m_prev = [m_sc[g] for g in range(NSA_KV_HEADS)]
        acc_prev = [acc_sc[g] for g in range(NSA_KV_HEADS)]
        m_out, acc_out = [], []
        for g in range(NSA_KV_HEADS):
            own_k = _group_lanes((tk, LANES), g)
            q4 = q_ref[g * NSA_HPG:(g + 1) * NSA_HPG].reshape(hq, LANES)
            qx = jnp.where(_group_lanes((hq, LANES), g), q4, sb4)
            kx = jnp.where(own_k, kblk, one_hot)
            s = lax.dot_general(qx, kx, (((1,), (1,)), ((), ())), preferred_element_type=F32)
            if diagonal:
                s = s + causal4
            m_new = jnp.maximum(m_prev[g], jnp.max(s, axis=-1, keepdims=True))
            p = jnp.exp2(s - m_new[:, :1]).astype(BF16)
            vx = jnp.where(own_k, vblk, 1.0)
            acc_out.append(jnp.exp2(m_prev[g] - m_new) * acc_prev[g] + jnp.dot(p, vx, preferred_element_type=F32))
            m_out.append(m_new)
        for g in range(NSA_KV_HEADS):
            m_sc[g] = m_out[g]
            acc_sc[g] = acc_out[g]

    @pl.when(ki < kmax)
    def _full():
        step(False)

    @pl.when(ki == kmax)
    def _last():
        step(True)
        for g in range(NSA_KV_HEADS):
            acc = acc_sc[g]
            _store_heads(o_ref, acc / pltpu.roll(acc, HEAD_DIM, 1), g, tq)


def _flash_sel(q8, rows3, selb):
    b, l, _ = rows3.shape
    tq, tk = min(SEL_TQ, l), min(SEL_TK, l)
    nq = l // tq
    pairs = [(qi, ki) for qi in range(nq) for ki in range((qi * tq + tq - 1) // tk + 1)]
    qi_tab = jnp.asarray([p[0] for p in pairs], jnp.int32)
    ki_tab = jnp.asarray([p[1] for p in pairs], jnp.int32)
    kv_idx = lambda blk: (lambda bi, si, qt, kt: (bi, kt[si], blk))
    return pl.pallas_call(
        functools.partial(_flash_sel_kernel, tq=tq, tk=tk),
        grid_spec=pltpu.PrefetchScalarGridSpec(
            num_scalar_prefetch=2, grid=(b, len(pairs)),
            in_specs=[pl.BlockSpec((NSA_HEADS, tq, LANES), lambda bi, si, qt, kt: (0, bi * nq + qt[si], 0)),
                      pl.BlockSpec((1, tk, KV_W), kv_idx(2)),
                      pl.BlockSpec((1, tk, KV_W), kv_idx(3)),
                      pl.BlockSpec((1, tq, LANES), lambda bi, si, qt, kt: (bi, qt[si], 0))],
            out_specs=pl.BlockSpec((1, tq, NSA_WIDTH), lambda bi, si, qt, kt: (bi, qt[si], 0)),
            scratch_shapes=[pltpu.VMEM((NSA_KV_HEADS, NSA_HPG * tq, LANES), F32),
                            pltpu.VMEM((NSA_KV_HEADS, NSA_HPG * tq, LANES), F32)]),
        out_shape=jax.ShapeDtypeStruct((b, l, NSA_WIDTH), F32),
        compiler_params=_cparams("parallel", "arbitrary"),
        name="flash_sel",
    )(qi_tab, ki_tab, q8, rows3, rows3, selb)


def _flash_win_kernel(q_ref, *refs, tq, back):
    nspan = back + 1
    k_refs, v_refs, o_ref = refs[:nspan], refs[nspan:2 * nspan], refs[2 * nspan]
    qi = pl.program_id(1)
    hq = NSA_HPG * tq
    span = nspan * tq
    qpos = qi * tq + lax.broadcasted_iota(jnp.int32, (tq, span), 0)
    kpos = (qi - back) * tq + lax.broadcasted_iota(jnp.int32, (tq, span), 1)
    ok = (kpos <= qpos) & (kpos > qpos - WINDOW) & (kpos >= 0)
    bias4 = jnp.concatenate([jnp.where(ok, 0.0, NEG_BIG)] * NSA_HPG, axis=0)
    kcat = jnp.concatenate([r[0] for r in k_refs], axis=0).astype(BF16)
    vcat = jnp.concatenate([r[0] for r in v_refs], axis=0).astype(BF16)
    for g in range(NSA_KV_HEADS):
        q4 = q_ref[g * NSA_HPG:(g + 1) * NSA_HPG].reshape(hq, LANES)
        qx = jnp.where(_group_lanes((hq, LANES), g), q4, 0.0)
        s = lax.dot_general(qx, kcat, (((1,), (1,)), ((), ())), preferred_element_type=F32) + bias4
        p = jnp.exp2(s - jnp.max(s, axis=-1, keepdims=True)).astype(BF16)
        vx = jnp.where(_group_lanes((span, LANES), g), vcat, 1.0)
        r = jnp.dot(p, vx, preferred_element_type=F32)
        _store_heads(o_ref, r / pltpu.roll(r, HEAD_DIM, 1), g, tq)


def _flash_win(q8, win3):
    b, l, _ = win3.shape
    tq = min(WIN_TQ, l)
    nq = l // tq
    back = -(-(WINDOW - 1) // tq)

    def kv_specs(blk):
        return [pl.BlockSpec((1, tq, KV_W), (lambda bi, qi, j=j: (bi, jnp.maximum(qi - back + j, 0), blk)))
                for j in range(back + 1)]

    return pl.pallas_call(
        functools.partial(_flash_win_kernel, tq=tq, back=back),
        grid=(b, nq),
        in_specs=[pl.BlockSpec((NSA_HEADS, tq, LANES), lambda bi, qi: (0, bi * nq + qi, 0))]
        + kv_specs(0) + kv_specs(1),
        out_specs=pl.BlockSpec((1, tq, NSA_WIDTH), lambda bi, qi: (bi, qi, 0)),
        out_shape=jax.ShapeDtypeStruct((b, l, NSA_WIDTH), F32),
        compiler_params=_cparams("parallel", "parallel"),
        name="flash_win",
    )(q8, *([win3] * (2 * (back + 1))))


def _softmax_update(s, m_prev, l_prev):
    m_new = jnp.maximum(m_prev, jnp.max(s, axis=-1, keepdims=True))
    alpha = jnp.exp2(m_prev - m_new)
    p = jnp.exp2(s - m_new[:, :1])
    return m_new, alpha, alpha * l_prev + jnp.sum(p, axis=-1, keepdims=True), p


def _paged_attn_kernel(pt_ref, *refs, npg, page, l_new, nbp):
    page_refs = refs[:npg]
    (qz_ref, sel_ref, kn_ref, vn_ref, win_ref, kwn_ref, vwn_ref,
     oslc_ref, owin_ref, wout_ref, m_sc, l_sc, acc_sc) = refs[npg:]
    ji = pl.program_id(1)
    nr = qz_ref.shape[1]
    span = npg * page

    @pl.when(ji == 0)
    def _init():
        m_sc[...] = jnp.full(m_sc.shape, NEG_BIG, F32)
        l_sc[...] = jnp.zeros(l_sc.shape, F32)
        acc_sc[...] = jnp.zeros(acc_sc.shape, F32)

    qz = qz_ref[0]
    kt = jnp.concatenate([page_refs[i][0, 0:KV_W, :] for i in range(npg)], axis=1).astype(BF16)
    vt = jnp.concatenate([page_refs[i][0, KV_W:2 * KV_W, :] for i in range(npg)], axis=1).astype(BF16)
    s = jnp.dot(qz, kt, preferred_element_type=F32)
    e_row = lax.broadcasted_iota(jnp.int32, (nbp, span), 0)
    e_col = lax.broadcasted_iota(jnp.int32, (nbp, span), 1)
    expand = jnp.where(e_row == ji * (span // SEL_BLOCK) + e_col // SEL_BLOCK, 1.0, 0.0).astype(BF16)
    picked = jnp.dot(sel_ref[0], expand, preferred_element_type=F32) > 0.5
    m_new, alpha, l_new_v, p = _softmax_update(jnp.where(picked, s, NEG_BIG), m_sc[...], l_sc[...])
    m_sc[...] = m_new
    l_sc[...] = l_new_v
    acc_sc[...] = alpha * acc_sc[...] + _dot_nt(p, vt)

    @pl.when(ji == pl.num_programs(1) - 1)
    def _fin():
        q_of_row = lax.broadcasted_iota(jnp.int32, (nr, l_new), 0) % l_new
        j_new = lax.broadcasted_iota(jnp.int32, (nr, l_new), 1)
        new_ok = j_new <= q_of_row
        sn = jnp.where(new_ok, _dot_nt(qz, kn_ref[0]), NEG_BIG)
        m2, a2, l2, p2 = _softmax_update(sn, m_sc[...], l_sc[...])
        oslc_ref[0] = (a2 * acc_sc[...] + _dot(p2, vn_ref[0])) / l2
        wlen = win_ref.shape[2]
        kw = win_ref[0, 0:KV_W, :]
        vw = win_ref[0, KV_W:2 * KV_W, :]
        i_old = lax.broadcasted_iota(jnp.int32, (nr, wlen), 1)
        q_old = lax.broadcasted_iota(jnp.int32, (nr, wlen), 0) % l_new
        sw = jnp.where(i_old + (WINDOW - wlen) > q_old, _dot(qz, kw), NEG_BIG)
        swn = jnp.where(new_ok, _dot_nt(qz, kwn_ref[0]), NEG_BIG)
        mw = jnp.maximum(jnp.max(sw, axis=-1, keepdims=True), jnp.max(swn, axis=-1, keepdims=True))
        pw = jnp.exp2(sw - mw)
        pwn = jnp.exp2(swn - mw)
        lw = jnp.sum(pw, axis=-1, keepdims=True) + jnp.sum(pwn, axis=-1, keepdims=True)
        owin_ref[0] = (_dot_nt(pw, vw) + _dot(pwn, vwn_ref[0])) / lw
        new_t = jnp.concatenate([kwn_ref[0], vwn_ref[0]], axis=1).T
        place = (lax.broadcasted_iota(jnp.int32, (l_new, wlen), 1)
                 == lax.broadcasted_iota(jnp.int32, (l_new, wlen), 0) + (wlen - l_new)).astype(F32)
        lane = lax.broadcasted_iota(jnp.int32, (2 * KV_W, wlen), 1)
        wout_ref[0] = jnp.where(lane < wlen - l_new, pltpu.roll(win_ref[0], wlen - l_new, 1), _dot_f32(new_t, place))


def _paged_attn(cache_t, page_table, pool_off, qz, sel_rows, rows3, win_t, win_off, win3):
    b, n_pages = page_table.shape
    page = cache_t.shape[2]
    npg = math.gcd(n_pages, PAGES_PER_STEP)
    nr = qz.shape[1]
    l_new = rows3.shape[1]
    nbp = sel_rows.shape[2]
    wlen = win_t.shape[2]
    kern = functools.partial(_paged_attn_kernel, npg=npg, page=page, l_new=l_new, nbp=nbp)
    per_b = lambda bi, ji, pt: (bi, 0, 0)
    return pl.pallas_call(
        kern,
        grid_spec=pltpu.PrefetchScalarGridSpec(
            num_scalar_prefetch=1, grid=(b, n_pages // npg),
            in_specs=_page_specs(npg, page, 1, pool_off)
            + [pl.BlockSpec((1, nr, KV_W), per_b),
               pl.BlockSpec((1, nr, nbp), per_b),
               pl.BlockSpec((1, l_new, KV_W), lambda bi, ji, pt: (bi, 0, 2)),
               pl.BlockSpec((1, l_new, KV_W), lambda bi, ji, pt: (bi, 0, 3)),
               pl.BlockSpec((1, 2 * KV_W, wlen), lambda bi, ji, pt: (bi + win_off, 0, 0)),
               pl.BlockSpec((1, l_new, KV_W), lambda bi, ji, pt: (bi, 0, 0)),
               pl.BlockSpec((1, l_new, KV_W), lambda bi, ji, pt: (bi, 0, 1))],
            out_specs=[pl.BlockSpec((1, nr, KV_W), per_b),
                       pl.BlockSpec((1, nr, KV_W), per_b),
                       pl.BlockSpec((1, 2 * KV_W, wlen), per_b)],
            scratch_shapes=[pltpu.VMEM((nr, LANES), F32), pltpu.VMEM((nr, LANES), F32),
                            pltpu.VMEM((nr, KV_W), F32)]),
        out_shape=[jax.ShapeDtypeStruct((b, nr, KV_W), F32),
                   jax.ShapeDtypeStruct((b, nr, KV_W), F32),
                   jax.ShapeDtypeStruct((b, 2 * KV_W, wlen), F32)],
        compiler_params=_cparams("parallel", "arbitrary"),
        name="paged_attn",
    )(page_table, *([cache_t] * npg), qz, sel_rows, rows3, rows3, win_t, win3, win3)


def _nsa_fresh(u_nsa, b, l, win_keep, q_g, k_g, cmp_pos, cmp_w):
    qn, qr, rows, win = _nsa_prep(u_nsa, jnp.arange(l), q_g, k_g, l)
    rows3 = rows.reshape(b, l, N_KV_SLOTS * KV_W)
    win3 = win.reshape(b, l, 2 * KV_W)
    wcat, bias = _cmp_weights(cmp_pos, cmp_w)
    t_use = (l // CMP_STRIDE) * CMP_STRIDE
    kc, vc = _cmp_kv(rows3, 0, 1, t_use, wcat, bias, k_g[0])
    o_cmp, selb = _cmp_attn(qn.reshape(b, l, NSA_WIDTH), kc, vc, 0, l, bias_out=True)
    o_slc = _flash_sel(qr, rows3, selb)
    o_win = _flash_win(qr, win3)
    branches = tuple(o.reshape(b * l, NSA_WIDTH) for o in (o_cmp, o_slc, o_win))
    new_rows = rows3.reshape(b, l, N_KV_SLOTS, NSA_KV_HEADS, HEAD_DIM)
    new_win = win3[:, l - win_keep:].reshape(b, win_keep, 2, NSA_KV_HEADS, HEAD_DIM)
    return branches, new_rows, new_win


def _nsa_paged(u_nsa, b, l, paged, q_g, k_g, cmp_pos, cmp_w):
    cache_t, page_table, pool_off, win_t, win_off = paged
    past_len = page_table.shape[1] * cache_t.shape[2]
    assert (past_len + l) // CMP_STRIDE == past_len // CMP_STRIDE and past_len % SEL_BLOCK == 0
    qn, qr, rows, win = _nsa_prep(u_nsa, past_len + jnp.arange(l), q_g, k_g, l)
    rows3 = rows.reshape(b, l, N_KV_SLOTS * KV_W)
    win3 = win.reshape(b, l, 2 * KV_W)
    wcat, bias = _cmp_weights(cmp_pos, cmp_w)
    kc, vc = _cmp_kv_paged(cache_t, page_table, pool_off, wcat, bias, k_g[0])
    o_cmp, sel_t = _cmp_attn(qn.reshape(b, l, NSA_WIDTH), kc, vc, past_len, past_len + l)
    nbp = sel_t.shape[1] // NSA_KV_HEADS
    sel_rows = jnp.swapaxes(sel_t.reshape(b, NSA_KV_HEADS, 1, nbp, l), 3, 4)
    sel_rows = jnp.broadcast_to(sel_rows, (b, NSA_KV_HEADS, NSA_HPG, l, nbp)).reshape(b, NSA_HEADS * l, nbp)
    q5 = jnp.transpose(qr[:, :, :HEAD_DIM].reshape(NSA_KV_HEADS, NSA_HPG, b, l, HEAD_DIM), (2, 0, 1, 3, 4))
    qz = jnp.einsum('bghqd,gk->bghqkd', q5, jnp.eye(NSA_KV_HEADS, dtype=q5.dtype)).reshape(b, NSA_HEADS * l, KV_W)
    o_slc_z, o_win_z, wout = _paged_attn(cache_t, page_table, pool_off, qz, sel_rows, rows3, win_t, win_off, win3)

    def own_group(o):
        o6 = o.reshape(b, NSA_KV_HEADS, NSA_HPG, l, NSA_KV_HEADS, HEAD_DIM)
        d = jnp.stack([o6[:, g, :, :, g, :] for g in range(NSA_KV_HEADS)], axis=1)
        return jnp.transpose(d, (0, 3, 1, 2, 4)).reshape(b * l, NSA_WIDTH)

    branches = (o_cmp.reshape(b * l, NSA_WIDTH), own_group(o_slc_z), own_group(o_win_z))
    new_rows = rows3.reshape(b, l, N_KV_SLOTS, NSA_KV_HEADS, HEAD_DIM)
    wlen = wout.shape[2]
    new_win = jnp.transpose(wout.reshape(b, 2, NSA_KV_HEADS, HEAD_DIM, wlen), (0, 4, 1, 2, 3))
    return branches, new_rows, new_win


PROJ_ROWS = 512


def _hybrid_layer(x, paged, win_keep, s_gla, c_ml, n_ml, m_ml, conv_ml,
                  norm_g, w_in_pad, w_out_bf, gla_w_gate, gla_b_gate, gla_norm_g,
                  nsa_q_norm_g, nsa_k_norm_g, nsa_cmp_pos, nsa_cmp_w,
                  ml_conv_w, ml_conv_b, ml_gate_b, ml_norm_g):
    b, l, _ = x.shape
    rows = b * l
    tm = math.gcd(rows, PROJ_ROWS)
    x2d = x.reshape(rows, D_MODEL)
    u_gla, u_nsa, u_ml = _proj_in(x2d, norm_g, w_in_pad, tm)

    nk = GLA_HEADS * GLA_DK
    wg = jnp.pad(gla_w_gate.astype(F32), ((0, LANES - GLA_RANK), (0, 0))).astype(BF16)
    o_a, st_new = _gla(u_gla, _gla_state_in(s_gla), wg, gla_b_gate.astype(F32).reshape(1, nk),
                       jnp.tile(gla_norm_g.astype(F32).reshape(1, GLA_DV), (1, GLA_HEADS)), b, l)
    s_new = _gla_state_out(st_new)

    if paged is None:
        o_nsa, new_rows, new_win = _nsa_fresh(u_nsa, b, l, win_keep, nsa_q_norm_g, nsa_k_norm_g,
                                              nsa_cmp_pos, nsa_cmp_w)
    else:
        o_nsa, new_rows, new_win = _nsa_paged(u_nsa, b, l, paged, nsa_q_norm_g, nsa_k_norm_g,
                                              nsa_cmp_pos, nsa_cmp_w)

    o_c, c_new, n_new, m_new, conv_new = _mlstm(u_ml, c_ml, n_ml, m_ml, conv_ml, ml_conv_w, ml_conv_b,
                                                ml_gate_b, ml_norm_g, b, l)

    y = _proj_out(o_a.reshape(rows, GLA_WIDTH), *o_nsa, u_nsa, o_c.reshape(rows, ML_WIDTH), x2d, w_out_bf, tm)
    return y.reshape(b, l, D_MODEL), new_rows, new_win, s_new, c_new, n_new, m_new, conv_new


def kernel(x_prompt, x_sample, cache_nsa_kv, state_nsa_win, state_gla, state_mlstm_C, state_mlstm_n,
           state_mlstm_m, state_mlstm_conv, page_table, norm_g, w_in, w_out, gla_w_gate, gla_b_gate,
           gla_norm_g, nsa_q_norm_g, nsa_k_norm_g, nsa_cmp_pos, nsa_cmp_w, ml_conv_w, ml_conv_b,
           ml_gate_b, ml_norm_g):
    bp, sp, _ = x_prompt.shape
    bs, _, _ = x_sample.shape
    depth = w_in.shape[0]
    dt = x_prompt.dtype
    zero_gla = jnp.zeros((bp, GLA_HEADS, GLA_DK, GLA_DV), F32)
    zero_c = jnp.zeros((bp, ML_HEADS, ML_DH, ML_DH), F32)
    zero_n = jnp.zeros((bp, ML_HEADS, ML_DH), F32)
    zero_m = jnp.zeros((bp, ML_HEADS), F32)
    zero_conv = jnp.zeros((bp, CONV_W - 1, 2 * ML_WIDTH), dt)
    keep_p = min(WINDOW, sp)
    keep_s = state_nsa_win.shape[2]
    n_pool = cache_nsa_kv.shape[1]
    cache_t = jnp.transpose(cache_nsa_kv, (0, 1, 3, 4, 5, 2)).reshape(
        depth * n_pool, N_KV_SLOTS * KV_W, cache_nsa_kv.shape[2]).astype(F32)
    win_t = jnp.transpose(state_nsa_win, (0, 1, 3, 4, 5, 2)).reshape(depth * bs, 2 * KV_W, keep_s).astype(F32)

    y_prompt, y_sample = x_prompt, x_sample
    p_layers, s_layers = [], []
    for layer in range(depth):
        w = (norm_g[layer], _pad_w_in(w_in[layer]), w_out[layer].astype(BF16), gla_w_gate[layer],
             gla_b_gate[layer], gla_norm_g[layer],
             nsa_q_norm_g[layer], nsa_k_norm_g[layer], nsa_cmp_pos[layer], nsa_cmp_w[layer],
             ml_conv_w[layer], ml_conv_b[layer], ml_gate_b[layer], ml_norm_g[layer])
        y_prompt, *p_new = _hybrid_layer(y_prompt, None, keep_p, zero_gla, zero_c, zero_n,
                                         zero_m, zero_conv, *w)
        paged = (cache_t, page_table, layer * n_pool, win_t, layer * bs)
        y_sample, *s_new = _hybrid_layer(y_sample, paged, keep_s,
                                         state_gla[layer], state_mlstm_C[layer], state_mlstm_n[layer],
                                         state_mlstm_m[layer], state_mlstm_conv[layer], *w)
        p_layers.append(p_new)
        s_layers.append(s_new)
    p_kv, p_win, p_gla, p_c, p_n, p_m, p_conv = [jnp.stack(z) for z in zip(*p_layers)]
    s_kv, s_win, s_gla, s_c, s_n, s_m, s_conv = [jnp.stack(z) for z in zip(*s_layers)]
    return (y_prompt, y_sample, p_kv, s_kv, p_win, s_win, p_gla, s_gla, p_c, s_c, p_n, s_n, p_m, s_m, p_conv, s_conv)
```

```python
import functools
import math

import jax
import jax.numpy as jnp
import numpy as np
from jax import lax
from jax.experimental import pallas as pl
from jax.experimental.pallas import tpu as pltpu

F32 = jnp.float32
BF16 = jnp.bfloat16
HIGHEST = lax.Precision.HIGHEST

D_MODEL = 1024
HEAD_DIM = 64
GLA_WIDTH = D_MODEL // 4
NSA_WIDTH = D_MODEL // 2
ML_WIDTH = D_MODEL - GLA_WIDTH - NSA_WIDTH
D_MIX = GLA_WIDTH + NSA_WIDTH + ML_WIDTH

GLA_HEADS = GLA_WIDTH // HEAD_DIM
GLA_DK = HEAD_DIM // 2
GLA_DV = HEAD_DIM
GLA_RANK = 16
GLA_TAU = 16.0
GLA_CHUNK = 64

NSA_HEADS = NSA_WIDTH // HEAD_DIM
NSA_KV_HEADS = 2
NSA_HPG = NSA_HEADS // NSA_KV_HEADS
CMP_BLOCK = 32
CMP_STRIDE = 16
SEL_BLOCK = 64
N_SELECT = 16
WINDOW = 512
Q_BLOCK = 128
N_KV_SLOTS = 4
ROT_DIM = HEAD_DIM // 4
ROPE_THETA = 500000.0
ATTN_SCALE = HEAD_DIM ** -0.5

ML_HEADS = ML_WIDTH // HEAD_DIM
ML_DH = HEAD_DIM
ML_CHUNK = 64
CONV_W = 4

SPLIT_SIZES = (GLA_HEADS * GLA_DK, GLA_HEADS * GLA_DK, GLA_WIDTH, GLA_RANK, GLA_WIDTH,
               NSA_WIDTH, 6 * NSA_KV_HEADS * HEAD_DIM, 3 * NSA_HEADS, NSA_WIDTH,
               2 * ML_WIDTH, ML_WIDTH, 2 * ML_HEADS, ML_WIDTH, ML_WIDTH)

LANES = 128
SUBLANES = 8
VMEM_LIMIT = 56 * 1024 * 1024
NEG_BIG = -1e30
EPS = 1e-6
LOG2E = math.log2(math.e)


def _round_up(n, m):
    return -(-n // m) * m


PAD_SIZES = tuple(_round_up(s, LANES) for s in SPLIT_SIZES)
D_IN_PAD = sum(PAD_SIZES)
W_GLA = sum(PAD_SIZES[0:5])
W_NSA = sum(PAD_SIZES[5:9])
W_ML = sum(PAD_SIZES[9:14])
KV_W = NSA_KV_HEADS * HEAD_DIM


def _dot(a, b):
    return jnp.dot(a.astype(BF16), b.astype(BF16), preferred_element_type=F32)


def _dot_nt(a, b):
    return lax.dot_general(a.astype(BF16), b.astype(BF16), (((1,), (1,)), ((), ())), preferred_element_type=F32)


def _dot_tn(a, b):
    return lax.dot_general(a.astype(BF16), b.astype(BF16), (((0,), (0,)), ((), ())), preferred_element_type=F32)


def _dot_f32(a, b):
    return jnp.dot(a, b, precision=HIGHEST, preferred_element_type=F32)


def _dot_split(a, b):
    a_hi = a.astype(BF16)
    a_lo = (a - a_hi.astype(F32)).astype(BF16)
    bb = b.astype(BF16)
    return jnp.dot(a_hi, bb, preferred_element_type=F32) + jnp.dot(a_lo, bb, preferred_element_type=F32)


def _log_sigmoid(x):
    return jnp.minimum(x, 0.0) - jnp.log1p(jnp.exp(-jnp.abs(x)))


def _sigmoid(x):
    return 1.0 / (1.0 + jnp.exp(-x))


def _silu(x):
    return x * _sigmoid(x)


def _group_mean_matrix(width):
    g = np.kron(np.eye(width // HEAD_DIM, dtype=np.float32), np.full((HEAD_DIM, HEAD_DIM), 1.0 / HEAD_DIM, np.float32))
    return jnp.asarray(g, dtype=BF16)


def _group_norm(x, gmat, gain):
    ms = _dot_split(x * x, gmat)
    return x * lax.rsqrt(ms + EPS) * gain


def _cparams(*sem):
    return pltpu.CompilerParams(dimension_semantics=sem, vmem_limit_bytes=VMEM_LIMIT)


def _proj_in_kernel(x_ref, g_ref, w_ref, ug_ref, un_ref, um_ref):
    x = x_ref[...]
    y = x * lax.rsqrt(jnp.mean(x * x, axis=-1, keepdims=True) + EPS) * g_ref[...]
    r = jnp.dot(y.astype(BF16), w_ref[...], preferred_element_type=F32)
    ug_ref[...] = r[:, 0:W_GLA]
    un_ref[...] = r[:, W_GLA:W_GLA + W_NSA]
    um_ref[...] = r[:, W_GLA + W_NSA:D_IN_PAD]


def _proj_in(x2d, g, w_pad, tm):
    rows = x2d.shape[0]
    return pl.pallas_call(
        _proj_in_kernel,
        grid=(rows // tm,),
        in_specs=[pl.BlockSpec((tm, D_MODEL), lambda i: (i, 0)),
                  pl.BlockSpec((1, D_MODEL), lambda i: (0, 0)),
                  pl.BlockSpec((D_MODEL, D_IN_PAD), lambda i: (0, 0))],
        out_specs=[pl.BlockSpec((tm, W_GLA), lambda i: (i, 0)),
                   pl.BlockSpec((tm, W_NSA), lambda i: (i, 0)),
                   pl.BlockSpec((tm, W_ML), lambda i: (i, 0))],
        out_shape=[jax.ShapeDtypeStruct((rows, W_GLA), F32),
                   jax.ShapeDtypeStruct((rows, W_NSA), F32),
                   jax.ShapeDtypeStruct((rows, W_ML), F32)],
        compiler_params=_cparams("parallel"),
        name="proj_in",
    )(x2d, g.reshape(1, D_MODEL), w_pad)


def _pad_w_in(w_in):
    parts = []
    off = 0
    for s, p in zip(SPLIT_SIZES, PAD_SIZES):
        seg = w_in[:, off:off + s]
        if p != s:
            seg = jnp.pad(seg, ((0, 0), (0, p - s)))
        parts.append(seg)
        off += s
    return jnp.concatenate(parts, axis=1).astype(BF16)


def _proj_out_kernel(oa_ref, ocmp_ref, oslc_ref, owin_ref, gz_ref, oc_ref, x_ref, w_ref, e_ref, y_ref):
    gate = _sigmoid(gz_ref[:, 0:LANES])
    ob = (_dot_split(gate, e_ref[0]) * ocmp_ref[...] + _dot_split(gate, e_ref[1]) * oslc_ref[...]
          + _dot_split(gate, e_ref[2]) * owin_ref[...]) * _silu(gz_ref[:, LANES:LANES + NSA_WIDTH])
    y = x_ref[...]
    y = y + _dot(oa_ref[...], w_ref[0:GLA_WIDTH, :])
    y = y + _dot(ob, w_ref[GLA_WIDTH:GLA_WIDTH + NSA_WIDTH, :])
    y = y + _dot(oc_ref[...], w_ref[GLA_WIDTH + NSA_WIDTH:D_MIX, :])
    y_ref[...] = y


def _gate_expand():
    e = np.zeros((3, LANES, NSA_WIDTH), np.float32)
    for j in range(3):
        for h in range(NSA_HEADS):
            e[j, j * NSA_HEADS + h, h * HEAD_DIM:(h + 1) * HEAD_DIM] = 1.0
    return jnp.asarray(e, dtype=BF16)


def _proj_out(oa, o_cmp, o_slc, o_win, u_nsa, oc, x2d, w_bf, tm):
    rows = x2d.shape[0]
    gz_w = LANES + NSA_WIDTH
    gz_blk = (NSA_WIDTH + 6 * KV_W) // gz_w
    assert gz_blk * gz_w == NSA_WIDTH + 6 * KV_W
    row = lambda i: (i, 0)
    return pl.pallas_call(
        _proj_out_kernel,
        grid=(rows // tm,),
        in_specs=[pl.BlockSpec((tm, GLA_WIDTH), row),
                  pl.BlockSpec((tm, NSA_WIDTH), row),
                  pl.BlockSpec((tm, NSA_WIDTH), row),
                  pl.BlockSpec((tm, NSA_WIDTH), row),
                  pl.BlockSpec((tm, gz_w), lambda i: (i, gz_blk)),
                  pl.BlockSpec((tm, ML_WIDTH), row),
                  pl.BlockSpec((tm, D_MODEL), row),
                  pl.BlockSpec((D_MIX, D_MODEL), lambda i: (0, 0)),
                  pl.BlockSpec((3, LANES, NSA_WIDTH), lambda i: (0, 0, 0))],
        out_specs=pl.BlockSpec((tm, D_MODEL), row),
        out_shape=jax.ShapeDtypeStruct((rows, D_MODEL), F32),
        compiler_params=_cparams("parallel"),
        name="proj_out",
    )(oa, o_cmp, o_slc, o_win, u_nsa, oc, x2d, w_bf, _gate_expand())


def _gla_kernel(u_ref, st0_ref, wg_ref, bg_ref, ng_ref, gm_ref, o_ref, st_ref, *, bb, tl, c):
    @pl.when(pl.program_id(1) == 0)
    def _init():
        st_ref[...] = st0_ref[...]

    nk = GLA_HEADS * GLA_DK
    a_off = 2 * nk + GLA_WIDTH
    q, k, v, log_a = [], [], [], []
    for i in range(bb):
        q.append(u_ref[i, :, 0:nk] * (GLA_DK ** -0.5))
        k.append(u_ref[i, :, nk:2 * nk])
        v.append(u_ref[i, :, 2 * nk:2 * nk + GLA_WIDTH])
        pre = _dot(u_ref[i, :, a_off:a_off + LANES], wg_ref[...]) + bg_ref[...]
        log_a.append(_log_sigmoid(pre) * (1.0 / GLA_TAU))

    row_c = lax.broadcasted_iota(jnp.int32, (c, nk), 0)
    shifts = [1 << t for t in range(c.bit_length()) if (1 << t) < c]
    hc = GLA_HEADS * c
    tri_h = (lax.broadcasted_iota(jnp.int32, (hc, c), 0) % c) >= lax.broadcasted_iota(jnp.int32, (hc, c), 1)
    k_head = lax.broadcasted_iota(jnp.int32, (1, nk), 1) // GLA_DK
    v_head = lax.broadcasted_iota(jnp.int32, (1, GLA_WIDTH), 1) // GLA_DV
    st_diag = (lax.broadcasted_iota(jnp.int32, (GLA_WIDTH, nk), 0) // GLA_DV
               == lax.broadcasted_iota(jnp.int32, (GLA_WIDTH, nk), 1) // GLA_DK)

    st = [st_ref[i] for i in range(bb)]
    outs = [[] for _ in range(bb)]
    for j in range(tl // c):
        sl = slice(j * c, (j + 1) * c)
        for i in range(bb):
            b = log_a[i][sl]
            for sh in shifts:
                b = b + jnp.where(row_c >= sh, pltpu.roll(b, sh, 0), 0.0)
            blast = b[c - 1:c]
            qe = q[i][sl] * jnp.exp(b)
            ke = k[i][sl] * jnp.exp(-b)
            kl = k[i][sl] * jnp.exp(blast - b)
            vc = v[i][sl]
            qx = jnp.concatenate([jnp.where(k_head == h, qe, 0.0) for h in range(GLA_HEADS)], axis=0)
            a = jnp.where(tri_h, _dot_nt(qx, ke), 0.0)
            r = _dot(a, vc)
            o = _dot_nt(qe, st[i])
            for h in range(GLA_HEADS):
                o = o + jnp.where(v_head == h, r[h * c:(h + 1) * c], 0.0)
            st[i] = st[i] * jnp.exp(blast) + jnp.where(st_diag, _dot_tn(vc, kl), 0.0)
            outs[i].append(o)
    for i in range(bb):
        st_ref[i] = st[i]
        o = outs[i][0] if len(outs[i]) == 1 else jnp.concatenate(outs[i], axis=0)
        z = u_ref[i, :, a_off + LANES:a_off + LANES + GLA_WIDTH]
        o_ref[i] = _group_norm(o, gm_ref[...], ng_ref[...]) * _silu(z)


GLA_SEQS_PER_STEP = 4


def _gla(u_gla, st0, wg, bg, ng, b, l):
    tl = min(l, 128)
    c = min(l, 16)
    bb = math.gcd(b, GLA_SEQS_PER_STEP)
    nk = GLA_HEADS * GLA_DK
    u3 = u_gla.reshape(b, l, W_GLA)
    kern = functools.partial(_gla_kernel, bb=bb, tl=tl, c=c)
    return pl.pallas_call(
        kern,
        grid=(b // bb, l // tl),
        in_specs=[pl.BlockSpec((bb, tl, W_GLA), lambda bi, li: (bi, li, 0)),
                  pl.BlockSpec((bb, GLA_WIDTH, nk), lambda bi, li: (bi, 0, 0)),
                  pl.BlockSpec((LANES, nk), lambda bi, li: (0, 0)),
                  pl.BlockSpec((1, nk), lambda bi, li: (0, 0)),
                  pl.BlockSpec((1, GLA_WIDTH), lambda bi, li: (0, 0)),
                  pl.BlockSpec((GLA_WIDTH, GLA_WIDTH), lambda bi, li: (0, 0))],
        out_specs=[pl.BlockSpec((bb, tl, GLA_WIDTH), lambda bi, li: (bi, li, 0)),
                   pl.BlockSpec((bb, GLA_WIDTH, nk), lambda bi, li: (bi, 0, 0))],
        out_shape=[jax.ShapeDtypeStruct((b, l, GLA_WIDTH), F32),
                   jax.ShapeDtypeStruct((b, GLA_WIDTH, nk), F32)],
        compiler_params=_cparams("parallel", "arbitrary"),
        name="gla",
    )(u3, st0, wg, bg, ng, _group_mean_matrix(GLA_WIDTH))


def _gla_state_in(s):
    b = s.shape[0]
    st = jnp.swapaxes(s.astype(F32), 2, 3)
    eye = jnp.eye(GLA_HEADS, dtype=F32)
    full = st[:, :, :, None, :] * eye[None, :, None, :, None]
    return full.reshape(b, GLA_WIDTH, GLA_HEADS * GLA_DK)


def _gla_state_out(st):
    b = st.shape[0]
    full = st.reshape(b, GLA_HEADS, GLA_DV, GLA_HEADS, GLA_DK)
    diag = jnp.stack([full[:, h, :, h, :] for h in range(GLA_HEADS)], axis=1)
    return jnp.swapaxes(diag, 2, 3)


def _mlstm_kernel(u_ref, c0_ref, n0_ref, m0_ref, cv0_ref, cw_ref, cb_ref, gb_ref, ng_ref, gm_ref,
                  o_ref, c_ref, n_ref, m_ref, cv_ref, xp_sc, *, bb, tl):
    @pl.when(pl.program_id(1) == 0)
    def _init():
        c_ref[...] = c0_ref[...]
        n_ref[...] = n0_ref[...]
        m_ref[...] = m0_ref[...]
        xp_sc[:, 0:SUBLANES, :] = cv0_ref[...]

    pairs = ML_HEADS // 2
    state = [([c_ref[i, j] for j in range(pairs)],
              [n_ref[i, j:j + 1, :] for j in range(pairs)],
              [m_ref[i, h:h + 1, :] for h in range(ML_HEADS)]) for i in range(bb)]
    new_state = [_mlstm_seq(i, state[i], u_ref, cw_ref, cb_ref, gb_ref, ng_ref, gm_ref, o_ref, cv_ref, xp_sc, tl)
                 for i in range(bb)]
    for i in range(bb):
        cps, nps, mbs = new_state[i]
        for j in range(pairs):
            c_ref[i, j] = cps[j]
            n_ref[i, j:j + 1, :] = nps[j]
        for h in range(ML_HEADS):
            m_ref[i, h:h + 1, :] = mbs[h]


def _mlstm_seq(i, state, u_ref, cw_ref, cb_ref, gb_ref, ng_ref, gm_ref, o_ref, cv_ref, xp_sc, tl):
    cps, nps, mbs = state
    c = tl
    w2 = 2 * ML_WIDTH
    u = u_ref[i]
    xp_sc[i, SUBLANES:SUBLANES + tl, :] = u[:, 0:w2]
    conv = cb_ref[...]
    for w in range(CONV_W):
        off = SUBLANES - (CONV_W - 1) + w
        conv = conv + xp_sc[i, off:off + tl, :] * cw_ref[w:w + 1, :]
    tail = xp_sc[i, tl:tl + SUBLANES, :]
    xp_sc[i, 0:SUBLANES, :] = tail
    cv_ref[i] = tail

    qk = _silu(conv)
    mq = qk[:, 0:ML_WIDTH]
    mk = qk[:, ML_WIDTH:w2] * (ML_DH ** -0.5)
    mv = u[:, w2:w2 + ML_WIDTH]
    ifg = u[:, w2 + ML_WIDTH:w2 + ML_WIDTH + LANES] + gb_ref[...]
    logf = _log_sigmoid(ifg)
    og_off = w2 + ML_WIDTH + LANES
    og = _sigmoid(u[:, og_off:og_off + ML_WIDTH])
    zz = _silu(u[:, og_off + ML_WIDTH:og_off + 2 * ML_WIDTH])

    tri = lax.broadcasted_iota(jnp.int32, (c, c), 0) >= lax.broadcasted_iota(jnp.int32, (c, c), 1)
    lane = lax.broadcasted_iota(jnp.int32, (c, LANES), 1)
    low = lane < ML_DH
    low_row = lax.broadcasted_iota(jnp.int32, (1, LANES), 1) < ML_DH
    sq_row = lax.broadcasted_iota(jnp.int32, (LANES, LANES), 0)
    sq_col = lax.broadcasted_iota(jnp.int32, (LANES, LANES), 1)
    same_head = (sq_row < ML_DH) == (sq_col < ML_DH)

    def wide(x):
        return x[:, :c] if c <= LANES else jnp.concatenate([x] * (c // LANES), axis=1)

    fcum_all = _dot_f32(tri.astype(F32), logf)
    gates_t = jnp.where(lane < ML_HEADS, ifg, fcum_all).T
    new_cps, new_nps, new_mbs = [], [], []
    for j in range(ML_HEADS // 2):
        ps = slice(j * LANES, (j + 1) * LANES)
        q_s, k_s, v_s = mq[:, ps], mk[:, ps], mv[:, ps]
        acc = jnp.zeros((c, 2 * LANES), F32)
        per_head = []
        for hl in range(2):
            h = 2 * j + hl
            own = low if hl == 0 else jnp.logical_not(low)
            fc = jnp.broadcast_to(fcum_all[:, ML_HEADS + h:ML_HEADS + h + 1], (c, LANES))
            ii = jnp.broadcast_to(ifg[:, h:h + 1], (c, LANES))
            dm = jnp.where(tri, wide(fc) - gates_t[ML_HEADS + h:ML_HEADS + h + 1, :] + gates_t[h:h + 1, :], NEG_BIG)
            inter = fc + mbs[h]
            m = jnp.maximum(inter, jnp.max(dm, axis=-1, keepdims=True))
            sij = _dot_nt(jnp.where(own, q_s, 0.0), k_s) * jnp.exp(dm - wide(m))
            acc = acc + _dot(sij, jnp.concatenate([jnp.where(own, v_s, 0.0), jnp.where(own, 1.0, 0.0)], axis=1))
            m_last = m[c - 1:c]
            f_last = fc[c - 1:c]
            per_head.append((m, jnp.exp(inter - m), jnp.exp(f_last - fc + ii - m_last),
                             jnp.exp(f_last + mbs[h] - m_last)))
            new_mbs.append(m_last)
        m_p, w_p, wj_p = (jnp.where(low, per_head[0][t], per_head[1][t]) for t in range(3))
        dec_row = jnp.where(low_row, per_head[0][3], per_head[1][3])
        n_mat = jnp.where(same_head, jnp.broadcast_to(nps[j], (LANES, LANES)), 0.0)
        num = w_p * _dot_nt(q_s, cps[j]) + acc[:, 0:LANES]
        den = w_p * _dot_nt(q_s, n_mat) + acc[:, LANES:2 * LANES]
        hh = num / jnp.maximum(jnp.abs(den), jnp.exp(-m_p))
        hn = hh * lax.rsqrt(_dot_split(hh * hh, gm_ref[...]) + EPS) * ng_ref[...]
        o_ref[i, :, ps] = hn * og[:, ps] * zz[:, ps]
        dec_mat = jnp.where(sq_row < ML_DH, jnp.broadcast_to(per_head[0][3], (LANES, LANES)),
                            jnp.broadcast_to(per_head[1][3], (LANES, LANES)))
        new_cps.append(dec_mat * cps[j] + jnp.where(same_head, _dot_tn(wj_p * v_s, k_s), 0.0))
        new_nps.append(dec_row * nps[j] + jnp.sum(wj_p * k_s, axis=0, keepdims=True))
    return new_cps, new_nps, new_mbs


ML_SEQS_PER_STEP = 2


def _mlstm(u_ml, c0, n0, m0, conv0, cw, cb, gb, ng, b, l):
    tl = min(l, 256)
    bb = math.gcd(b, ML_SEQS_PER_STEP)
    pairs = ML_HEADS // 2
    w2 = 2 * ML_WIDTH
    u3 = u_ml.reshape(b, l, W_ML)
    eye2 = jnp.eye(2, dtype=F32)
    c0p = jnp.einsum('bphed,hk->bphekd', c0.astype(F32).reshape(b, pairs, 2, ML_DH, ML_DH), eye2)
    c0p = c0p.reshape(b, pairs, LANES, LANES)
    n0p = n0.astype(F32).reshape(b, pairs, LANES)
    m0b = jnp.broadcast_to(m0.astype(F32)[:, :, None], (b, ML_HEADS, LANES))
    cv0 = jnp.pad(conv0.astype(F32), ((0, 0), (SUBLANES - (CONV_W - 1), 0), (0, 0)))
    gbp = jnp.pad(gb.astype(F32).reshape(1, 2 * ML_HEADS), ((0, 0), (0, LANES - 2 * ML_HEADS)))
    kern = functools.partial(_mlstm_kernel, bb=bb, tl=tl)
    st = lambda bi, li: (bi, 0, 0)
    st4 = lambda bi, li: (bi, 0, 0, 0)
    cst = lambda bi, li: (0, 0)
    o, c_new, n_new, m_new, cv = pl.pallas_call(
        kern,
        grid=(b // bb, l // tl),
        in_specs=[pl.BlockSpec((bb, tl, W_ML), lambda bi, li: (bi, li, 0)),
                  pl.BlockSpec((bb, pairs, LANES, LANES), st4),
                  pl.BlockSpec((bb, pairs, LANES), st),
                  pl.BlockSpec((bb, ML_HEADS, LANES), st),
                  pl.BlockSpec((bb, SUBLANES, w2), st),
                  pl.BlockSpec((CONV_W, w2), cst),
                  pl.BlockSpec((1, w2), cst),
                  pl.BlockSpec((1, LANES), cst),
                  pl.BlockSpec((1, LANES), cst),
                  pl.BlockSpec((LANES, LANES), cst)],
        out_specs=[pl.BlockSpec((bb, tl, ML_WIDTH), lambda bi, li: (bi, li, 0)),
                   pl.BlockSpec((bb, pairs, LANES, LANES), st4),
                   pl.BlockSpec((bb, pairs, LANES), st),
                   pl.BlockSpec((bb, ML_HEADS, LANES), st),
                   pl.BlockSpec((bb, SUBLANES, w2), st)],
        out_shape=[jax.ShapeDtypeStruct((b, l, ML_WIDTH), F32),
                   jax.ShapeDtypeStruct((b, pairs, LANES, LANES), F32),
                   jax.ShapeDtypeStruct((b, pairs, LANES), F32),
                   jax.ShapeDtypeStruct((b, ML_HEADS, LANES), F32),
                   jax.ShapeDtypeStruct((b, SUBLANES, w2), F32)],
        scratch_shapes=[pltpu.VMEM((bb, tl + 2 * SUBLANES, w2), F32)],
        compiler_params=_cparams("parallel", "arbitrary"),
        name="mlstm",
    )(u3, c0p, n0p, m0b, cv0, cw.astype(F32), cb.astype(F32).reshape(1, w2), gbp,
      jnp.tile(ng.astype(F32).reshape(1, ML_DH), (1, 2)), _group_mean_matrix(LANES))
    c6 = c_new.reshape(b, pairs, 2, ML_DH, 2, ML_DH)
    c_out = jnp.stack([c6[:, :, h, :, h, :] for h in range(2)], axis=2).reshape(b, ML_HEADS, ML_DH, ML_DH)
    return (o, c_out, n_new.reshape(b, ML_HEADS, ML_DH), m_new[:, :, 0], cv[:, SUBLANES - (CONV_W - 1):, :])


def _rope_lanes(x, cos_t, sin_t):
    w = x.shape[1]
    half = ROT_DIM // 2
    reps = w // cos_t.shape[1]
    if reps > 1:
        cos_t = jnp.concatenate([cos_t] * reps, axis=1)
        sin_t = jnp.concatenate([sin_t] * reps, axis=1)
    lane = lax.broadcasted_iota(jnp.int32, x.shape, 1) % HEAD_DIM
    partner = jnp.where(lane < half, pltpu.roll(x, w - half, 1), pltpu.roll(x, half, 1))
    return x * cos_t + partner * sin_t


def _nsa_prep_kernel(u_ref, cos_ref, sin_ref, qg_ref, kg_ref, g4_ref, g1_ref, qn_ref, qr_ref, rows_ref, win_ref):
    u = u_ref[...]
    cos_t = cos_ref[...]
    sin_t = sin_ref[...]
    q = _group_norm(u[:, 0:NSA_WIDTH], g4_ref[...], qg_ref[...])
    qn_ref[...] = (q * ATTN_SCALE).astype(BF16)
    qr = _rope_lanes(q, cos_t, sin_t) * (ATTN_SCALE * LOG2E)
    low = lax.broadcasted_iota(jnp.int32, (qr.shape[0], LANES), 1) < HEAD_DIM
    for j in range(NSA_HEADS // 2):
        pair = qr[:, j * LANES:(j + 1) * LANES]
        swapped = pltpu.roll(pair, HEAD_DIM, 1)
        qr_ref[2 * j] = jnp.where(low, pair, swapped).astype(BF16)
        qr_ref[2 * j + 1] = jnp.where(low, swapped, pair).astype(BF16)
    kv = NSA_WIDTH
    k_slc = _rope_lanes(_group_norm(u[:, kv + 2 * KV_W:kv + 3 * KV_W], g1_ref[...], kg_ref[1:2, :]), cos_t, sin_t)
    k_win = _rope_lanes(_group_norm(u[:, kv + 4 * KV_W:kv + 5 * KV_W], g1_ref[...], kg_ref[2:3, :]), cos_t, sin_t)
    rows_ref[:, 0:2 * KV_W] = u[:, kv:kv + 2 * KV_W]
    rows_ref[:, 2 * KV_W:3 * KV_W] = k_slc
    rows_ref[:, 3 * KV_W:4 * KV_W] = u[:, kv + 3 * KV_W:kv + 4 * KV_W]
    win_ref[:, 0:KV_W] = k_win
    win_ref[:, KV_W:2 * KV_W] = u[:, kv + 5 * KV_W:kv + 6 * KV_W]


def _rope_tables(pos):
    half = ROT_DIM // 2
    inv = jnp.exp(-math.log(ROPE_THETA) * jnp.arange(half, dtype=F32) * 2.0 / ROT_DIM)
    ang = pos.astype(F32)[:, None] * inv[None, :]
    cos, sin = jnp.cos(ang), jnp.sin(ang)
    n = pos.shape[0]
    ones = jnp.ones((n, HEAD_DIM - ROT_DIM), F32)
    cos_h = jnp.concatenate([cos, cos, ones], axis=1)
    sin_h = jnp.concatenate([-sin, sin, 0.0 * ones], axis=1)
    reps = LANES // HEAD_DIM
    return jnp.tile(cos_h, (1, reps)), jnp.tile(sin_h, (1, reps))


def _nsa_prep(u_nsa, pos, q_g, k_g, l):
    rows = u_nsa.shape[0]
    tl = min(l, 256)
    nb = l // tl
    cos_t, sin_t = _rope_tables(pos)
    qg = jnp.tile(q_g.astype(F32).reshape(1, HEAD_DIM), (1, NSA_HEADS))
    kg = jnp.tile(k_g.astype(F32), (1, NSA_KV_HEADS))
    cst = lambda i: (0, 0)
    return pl.pallas_call(
        _nsa_prep_kernel,
        grid=(rows // tl,),
        in_specs=[pl.BlockSpec((tl, W_NSA), lambda i: (i, 0)),
                  pl.BlockSpec((tl, LANES), lambda i: (i % nb, 0)),
                  pl.BlockSpec((tl, LANES), lambda i: (i % nb, 0)),
                  pl.BlockSpec((1, NSA_WIDTH), cst),
                  pl.BlockSpec((3, KV_W), cst),
                  pl.BlockSpec((NSA_WIDTH, NSA_WIDTH), cst),
                  pl.BlockSpec((KV_W, KV_W), cst)],
        out_specs=[pl.BlockSpec((tl, NSA_WIDTH), lambda i: (i, 0)),
                   pl.BlockSpec((NSA_HEADS, tl, LANES), lambda i: (0, i, 0)),
                   pl.BlockSpec((tl, N_KV_SLOTS * KV_W), lambda i: (i, 0)),
                   pl.BlockSpec((tl, 2 * KV_W), lambda i: (i, 0))],
        out_shape=[jax.ShapeDtypeStruct((rows, NSA_WIDTH), BF16),
                   jax.ShapeDtypeStruct((NSA_HEADS, rows, LANES), BF16),
                   jax.ShapeDtypeStruct((rows, N_KV_SLOTS * KV_W), F32),
                   jax.ShapeDtypeStruct((rows, 2 * KV_W), F32)],
        compiler_params=_cparams("parallel"),
        name="nsa_prep",
    )(u_nsa, cos_t, sin_t, qg, kg, _group_mean_matrix(NSA_WIDTH), _group_mean_matrix(KV_W))


def _cmp_halves(xk_ref, xv_ref, w_ref, n_half):
    acc_k = jnp.zeros((n_half, 2 * KV_W), F32)
    acc_v = jnp.zeros((n_half, 2 * KV_W), F32)
    for s in range(CMP_STRIDE):
        acc_k = acc_k + _dot(xk_ref[pl.ds(s, n_half, stride=CMP_STRIDE), :], w_ref[0, s])
        acc_v = acc_v + _dot(xv_ref[pl.ds(s, n_half, stride=CMP_STRIDE), :], w_ref[1, s])
    return acc_k, acc_v


def _cmp_finish(acc_k, acc_v, bias_ref, kg_ref, g1_ref, kc_ref, vc_ref, n_half):
    valid = lax.broadcasted_iota(jnp.int32, (n_half, KV_W), 0) < n_half - 1

    def summary(acc, bias):
        return acc[:, 0:KV_W] + pltpu.roll(acc[:, KV_W:2 * KV_W], n_half - 1, 0) + bias

    kc = _group_norm(summary(acc_k, bias_ref[0:1, :]), g1_ref[...], kg_ref[...])
    kc_ref[0] = jnp.where(valid, kc, 0.0).astype(BF16)
    vc_ref[0] = jnp.where(valid, summary(acc_v, bias_ref[1:2, :]), 0.0).astype(BF16)


def _cmp_kv_kernel(xk_ref, xv_ref, w_ref, bias_ref, kg_ref, g1_ref, kc_ref, vc_ref, *, n_half):
    acc_k, acc_v = _cmp_halves(xk_ref.at[0], xv_ref.at[0], w_ref, n_half)
    _cmp_finish(acc_k, acc_v, bias_ref, kg_ref, g1_ref, kc_ref, vc_ref, n_half)


PAGES_PER_STEP = 64


def _cmp_paged_kernel(pt_ref, *refs, npg, page, whole):
    page_refs = refs[:npg]
    if whole:
        w_ref, bias_ref, kg_ref, g1_ref, kc_ref, vc_ref, xk_sc, xv_sc = refs[npg:]
    else:
        w_ref, acck_ref, accv_ref, xk_sc, xv_sc = refs[npg:]
    for i in range(npg):
        xk_sc[i * page:(i + 1) * page, :] = page_refs[i][0, 0:KV_W, :].T
        xv_sc[i * page:(i + 1) * page, :] = page_refs[i][0, KV_W:2 * KV_W, :].T
    n_half = npg * page // CMP_STRIDE
    acc_k, acc_v = _cmp_halves(xk_sc, xv_sc, w_ref, n_half)
    if whole:
        _cmp_finish(acc_k, acc_v, bias_ref, kg_ref, g1_ref, kc_ref, vc_ref, n_half)
    else:
        acck_ref[0] = acc_k
        accv_ref[0] = acc_v


def _cmp_fin_kernel(acck_ref, accv_ref, bias_ref, kg_ref, g1_ref, kc_ref, vc_ref, *, n_half):
    _cmp_finish(acck_ref[0], accv_ref[0], bias_ref, kg_ref, g1_ref, kc_ref, vc_ref, n_half)


def _page_specs(npg, page, row_blk, pool_off):
    def spec(i):
        return pl.BlockSpec((1, 2 * KV_W, page), lambda bi, ji, pt: (pt[bi, ji * npg + i] + pool_off, row_blk, 0))
    return [spec(i) for i in range(npg)]


def _cmp_kv_paged(cache_t, page_table, pool_off, wcat, bias, kg0):
    b, n_pages = page_table.shape
    page = cache_t.shape[2]
    npg = math.gcd(n_pages, PAGES_PER_STEP)
    nh_step = npg * page // CMP_STRIDE
    n_half = n_pages * page // CMP_STRIDE
    kg = jnp.tile(kg0.astype(F32).reshape(1, HEAD_DIM), (1, NSA_KV_HEADS))
    if npg == n_pages:
        cst = lambda bi, ji, pt: (0, 0)
        return pl.pallas_call(
            functools.partial(_cmp_paged_kernel, npg=npg, page=page, whole=True),
            grid_spec=pltpu.PrefetchScalarGridSpec(
                num_scalar_prefetch=1, grid=(b, 1),
                in_specs=_page_specs(npg, page, 0, pool_off)
                + [pl.BlockSpec((2, CMP_STRIDE, KV_W, 2 * KV_W), lambda bi, ji, pt: (0, 0, 0, 0)),
                   pl.BlockSpec((2, KV_W), cst), pl.BlockSpec((1, KV_W), cst), pl.BlockSpec((KV_W, KV_W), cst)],
                out_specs=[pl.BlockSpec((1, n_half, KV_W), lambda bi, ji, pt: (bi, 0, 0)),
                           pl.BlockSpec((1, n_half, KV_W), lambda bi, ji, pt: (bi, 0, 0))],
                scratch_shapes=[pltpu.VMEM((npg * page, KV_W), F32), pltpu.VMEM((npg * page, KV_W), F32)]),
            out_shape=[jax.ShapeDtypeStruct((b, n_half, KV_W), BF16),
                       jax.ShapeDtypeStruct((b, n_half, KV_W), BF16)],
            compiler_params=_cparams("parallel", "arbitrary"),
            name="cmp_paged",
        )(page_table, *([cache_t] * npg), wcat, bias, kg, _group_mean_matrix(KV_W))
    kern = functools.partial(_cmp_paged_kernel, npg=npg, page=page, whole=False)
    acc_k, acc_v = pl.pallas_call(
        kern,
        grid_spec=pltpu.PrefetchScalarGridSpec(
            num_scalar_prefetch=1, grid=(b, n_pages // npg),
            in_specs=_page_specs(npg, page, 0, pool_off)
            + [pl.BlockSpec((2, CMP_STRIDE, KV_W, 2 * KV_W), lambda bi, ji, pt: (0, 0, 0, 0))],
            out_specs=[pl.BlockSpec((1, nh_step, 2 * KV_W), lambda bi, ji, pt: (bi, ji, 0)),
                       pl.BlockSpec((1, nh_step, 2 * KV_W), lambda bi, ji, pt: (bi, ji, 0))],
            scratch_shapes=[pltpu.VMEM((npg * page, KV_W), F32), pltpu.VMEM((npg * page, KV_W), F32)]),
        out_shape=[jax.ShapeDtypeStruct((b, n_half, 2 * KV_W), F32),
                   jax.ShapeDtypeStruct((b, n_half, 2 * KV_W), F32)],
        compiler_params=_cparams("parallel", "arbitrary"),
        name="cmp_paged",
    )(page_table, *([cache_t] * npg), wcat)
    blk = lambda bi: (bi, 0, 0)
    return pl.pallas_call(
        functools.partial(_cmp_fin_kernel, n_half=n_half),
        grid=(b,),
        in_specs=[pl.BlockSpec((1, n_half, 2 * KV_W), blk),
                  pl.BlockSpec((1, n_half, 2 * KV_W), blk),
                  pl.BlockSpec((2, KV_W), lambda bi: (0, 0)),
                  pl.BlockSpec((1, KV_W), lambda bi: (0, 0)),
                  pl.BlockSpec((KV_W, KV_W), lambda bi: (0, 0))],
        out_specs=[pl.BlockSpec((1, n_half, KV_W), blk), pl.BlockSpec((1, n_half, KV_W), blk)],
        out_shape=[jax.ShapeDtypeStruct((b, n_half, KV_W), BF16),
                   jax.ShapeDtypeStruct((b, n_half, KV_W), BF16)],
        compiler_params=_cparams("parallel"),
        name="cmp_fin",
    )(acc_k, acc_v, bias, kg, _group_mean_matrix(KV_W))


def _cmp_weights(cmp_pos, cmp_w):
    wf = cmp_w.astype(F32)
    eye_g = jnp.eye(NSA_KV_HEADS, dtype=F32)

    def bd(w):
        return jnp.einsum('ksde,gh->ksgdhe', w, eye_g).reshape(2, CMP_STRIDE, KV_W, KV_W)

    wcat = jnp.concatenate([bd(wf[:, :CMP_STRIDE]), bd(wf[:, CMP_STRIDE:])], axis=3).astype(BF16)
    bias = jnp.einsum('ksd,ksde->ke', cmp_pos.astype(F32), wf)
    return wcat, jnp.tile(bias, (1, NSA_KV_HEADS))


def _cmp_kv(x3, k_blk, v_blk, t_use, wcat, bias, kg0):
    b = x3.shape[0]
    n_half = t_use // CMP_STRIDE
    kern = functools.partial(_cmp_kv_kernel, n_half=n_half)
    kg = jnp.tile(kg0.astype(F32).reshape(1, HEAD_DIM), (1, NSA_KV_HEADS))
    return pl.pallas_call(
        kern,
        grid=(b,),
        in_specs=[pl.BlockSpec((1, t_use, KV_W), lambda bi: (bi, 0, k_blk)),
                  pl.BlockSpec((1, t_use, KV_W), lambda bi: (bi, 0, v_blk)),
                  pl.BlockSpec((2, CMP_STRIDE, KV_W, 2 * KV_W), lambda bi: (0, 0, 0, 0)),
                  pl.BlockSpec((2, KV_W), lambda bi: (0, 0)),
                  pl.BlockSpec((1, KV_W), lambda bi: (0, 0)),
                  pl.BlockSpec((KV_W, KV_W), lambda bi: (0, 0))],
        out_specs=[pl.BlockSpec((1, n_half, KV_W), lambda bi: (bi, 0, 0)),
                   pl.BlockSpec((1, n_half, KV_W), lambda bi: (bi, 0, 0))],
        out_shape=[jax.ShapeDtypeStruct((b, n_half, KV_W), BF16),
                   jax.ShapeDtypeStruct((b, n_half, KV_W), BF16)],
        compiler_params=_cparams("parallel"),
        name="cmp_kv",
    )(x3, x3, wcat, bias, kg, _group_mean_matrix(KV_W))


def _cmp_attn_kernel(qn_ref, kc_ref, vc_ref, ov_ref, o_ref, sel_ref, *, bb, tq, n_half, n_cmp, n_sel, nbp, pos0,
                     bias_out):
    biases = []
    qi = pl.program_id(1)
    nq = bb * tq
    pos_c = pos0 + qi * tq + lax.broadcasted_iota(jnp.int32, (tq, 1), 0)
    ncol = lax.broadcasted_iota(jnp.int32, (1, n_half), 1)
    cmask = (ncol * CMP_STRIDE + (CMP_BLOCK - 1) <= pos_c) & (ncol < n_cmp)
    pos_r = pos0 + qi * tq + lax.broadcasted_iota(jnp.int32, (1, nq), 1) % tq
    blk = lax.broadcasted_iota(jnp.int32, (nbp, 1), 0)
    cur = pos_r // SEL_BLOCK
    forced = (blk == 0) | (blk == cur) | (blk == cur - 1)
    valid = blk * SEL_BLOCK <= pos_r
    real = blk < n_sel
    for g in range(NSA_KV_HEADS):
        gs = slice(g * HEAD_DIM, (g + 1) * HEAD_DIM)
        psums = []
        for i in range(bb):
            kc = kc_ref[i, :, gs]
            vc = vc_ref[i, :, gs]
            psum = jnp.zeros((tq, n_half), F32)
            for h in range(NSA_HPG):
                hs = slice((g * NSA_HPG + h) * HEAD_DIM, (g * NSA_HPG + h + 1) * HEAD_DIM)
                s = _dot_nt(qn_ref[i, :, hs], kc)
                m = jnp.max(jnp.where(cmask, s, NEG_BIG), axis=-1, keepdims=True)
                m = jnp.where(m > 0.5 * NEG_BIG, m, 0.0)
                e = jnp.where(cmask, jnp.exp(s - m), 0.0)
                p = e / jnp.maximum(jnp.sum(e, axis=-1, keepdims=True), 1e-30)
                o_ref[i, :, hs] = _dot(p, vc)
                psum = psum + p
            psums.append(psum)
        psum = psums[0] if bb == 1 else jnp.concatenate(psums, axis=0)
        p_hi = psum.astype(BF16)
        p_lo = (psum - p_hi.astype(F32)).astype(BF16)
        ov = ov_ref[...]
        imp = _dot_nt(ov, p_hi) + _dot_nt(ov, p_lo)
        score = jnp.where(forced, 3e38, jnp.where(valid, imp, -1e38))
        score = jnp.where(real, score, -3e38)
        cnt = jnp.zeros((nbp, nq), F32)
        for jp in range(n_sel):
            rowv = score[jp:jp + 1, :]
            beats = (rowv > score) | ((rowv == score) & (blk > jp))
            cnt = cnt + jnp.where(beats, 1.0, 0.0)
        sel = (cnt < float(min(N_SELECT, n_sel))) & real
        if bias_out:
            sbg = jnp.where(sel, 0.0, NEG_BIG)
            if nbp < HEAD_DIM:
                sbg = jnp.concatenate([sbg, jnp.full((HEAD_DIM - nbp, nq), NEG_BIG, F32)], axis=0)
            biases.append(sbg)
        else:
            sel_ref[0, g * nbp:(g + 1) * nbp, :] = jnp.where(sel, 1.0, 0.0).astype(BF16)
    if bias_out:
        sel_ref[0] = jnp.concatenate(biases[::-1], axis=0).T.astype(BF16)


def _overlap_t(n_cmp, n_sel, n_half, nbp):
    a = SEL_BLOCK // CMP_STRIDE
    bb = CMP_BLOCK // CMP_STRIDE
    i = np.arange(n_half)[None, :]
    j = np.arange(nbp)[:, None]
    s = i - a * j + (bb - 1)
    cnt = np.maximum(np.minimum(np.minimum(s + 1, a + bb - 1 - s), min(a, bb)), 0)
    cnt = np.where((i < n_cmp) & (j < n_sel), cnt, 0)
    return jnp.asarray(cnt, dtype=BF16)


CMP_TQ = 256


def _cmp_attn(qn3, kc, vc, pos0, t_len, bias_out=False):
    b, l, _ = qn3.shape
    n_half = kc.shape[1]
    n_cmp = n_half - 1
    n_sel = -(-t_len // SEL_BLOCK)
    nbp = _round_up(n_sel, 16)
    tq = min(l, CMP_TQ)
    bb = math.gcd(b, max(LANES // tq, 1))
    kern = functools.partial(_cmp_attn_kernel, bb=bb, tq=tq, n_half=n_half, n_cmp=n_cmp, n_sel=n_sel, nbp=nbp,
                             pos0=pos0, bias_out=bias_out)
    if bias_out:
        assert bb == 1 and NSA_KV_HEADS == 2 and nbp <= HEAD_DIM
        sel_spec = pl.BlockSpec((1, tq, LANES), lambda bi, qi: (bi, qi, 0))
        sel_shape = jax.ShapeDtypeStruct((b, l, LANES), BF16)
    else:
        sel_spec = pl.BlockSpec((1, NSA_KV_HEADS * nbp, bb * tq), lambda bi, qi: (bi, 0, qi))
        sel_shape = jax.ShapeDtypeStruct((b // bb, NSA_KV_HEADS * nbp, bb * l), BF16)
    o_cmp, sel_t = pl.pallas_call(
        kern,
        grid=(b // bb, l // tq),
        in_specs=[pl.BlockSpec((bb, tq, NSA_WIDTH), lambda bi, qi: (bi, qi, 0)),
                  pl.BlockSpec((bb, n_half, KV_W), lambda bi, qi: (bi, 0, 0)),
                  pl.BlockSpec((bb, n_half, KV_W), lambda bi, qi: (bi, 0, 0)),
                  pl.BlockSpec((nbp, n_half), lambda bi, qi: (0, 0))],
        out_specs=[pl.BlockSpec((bb, tq, NSA_WIDTH), lambda bi, qi: (bi, qi, 0)), sel_spec],
        out_shape=[jax.ShapeDtypeStruct((b, l, NSA_WIDTH), F32), sel_shape],
        compiler_params=_cparams("parallel", "parallel"),
        name="cmp_attn",
    )(qn3, kc, vc, _overlap_t(n_cmp, n_sel, n_half, nbp))
    if bb > 1:
        sel_t = sel_t.reshape(b // bb, NSA_KV_HEADS * nbp, bb, l)
        sel_t = jnp.swapaxes(sel_t, 1, 2).reshape(b, NSA_KV_HEADS * nbp, l)
    return o_cmp, sel_t


SEL_TQ = 128
SEL_TK = 2048
WIN_TQ = 256


def _group_lanes(shape, g):
    lane = lax.broadcasted_iota(jnp.int32, shape, len(shape) - 1)
    return (lane < HEAD_DIM) if g == 0 else (lane >= HEAD_DIM)


def _store_heads(o_ref, r, g, tq):
    lo = g * HEAD_DIM
    for h in range(NSA_HPG):
        hh = g * NSA_HPG + h
        o_ref[0, :, hh * HEAD_DIM:(hh + 1) * HEAD_DIM] = r[h * tq:(h + 1) * tq, lo:lo + HEAD_DIM]


def _flash_sel_kernel(qi_ref, ki_ref, q_ref, k_ref, v_ref, sb_ref, o_ref, m_sc, acc_sc, *, tq, tk):
    qi = qi_ref[pl.program_id(1)]
    ki = ki_ref[pl.program_id(1)]
    kmax = (qi * tq + tq - 1) // tk
    hq = NSA_HPG * tq

    @pl.when(ki == 0)
    def _init():
        m_sc[...] = jnp.full(m_sc.shape, NEG_BIG, F32)
        acc_sc[...] = jnp.zeros(acc_sc.shape, F32)

    def step(diagonal):
        key_blk = ki * (tk // SEL_BLOCK) + lax.broadcasted_iota(jnp.int32, (tk, LANES), 0) // SEL_BLOCK
        one_hot = jnp.where(lax.broadcasted_iota(jnp.int32, (tk, LANES), 1) % HEAD_DIM == key_blk, 1.0, 0.0).astype(BF16)
        kblk = k_ref[0].astype(BF16)
        vblk = v_ref[0].astype(BF16)
        sb4 = jnp.concatenate([sb_ref[0]] * NSA_HPG, axis=0)
        if diagonal:
            qpos = qi * tq + lax.broadcasted_iota(jnp.int32, (tq, tk), 0)
            kpos = ki * tk + lax.broadcasted_iota(jnp.int32, (tq, tk), 1)
            causal4 = jnp.concatenate([jnp.where(kpos <= qpos, 0.0, NEG_BIG)] * NSA_HPG, axis=0)
        m_prev = [m_sc[g] for g in range(NSA_KV_HEADS)]
        acc_prev = [acc_sc[g] for g in range(NSA_KV_HEADS)]
        m_out, acc_out = [], []
        for g in range(NSA_KV_HEADS):
            own_k = _group_lanes((tk, LANES), g)
            q4 = q_ref[g * NSA_HPG:(g + 1) * NSA_HPG].reshape(hq, LANES)
            qx = jnp.where(_group_lanes((hq, LANES), g), q4, sb4)
            kx = jnp.where(own_k, kblk, one_hot)
            s = lax.dot_general(qx, kx, (((1,), (1,)), ((), ())), preferred_element_type=F32)
            if diagonal:
                s = s + causal4
            m_new = jnp.maximum(m_prev[g], jnp.max(s, axis=-1, keepdims=True))
            p = jnp.exp2(s - m_new[:, :1]).astype(BF16)
            vx = jnp.where(own_k, vblk, 1.0)
            acc_out.append(jnp.exp2(m_prev[g] - m_new) * acc_prev[g] + jnp.dot(p, vx, preferred_element_type=F32))
            m_out.append(m_new)
        for g in range(NSA_KV_HEADS):
            m_sc[g] = m_out[g]
            acc_sc[g] = acc_out[g]

    @pl.when(ki < kmax)
    def _full():
        step(False)

    @pl.when(ki == kmax)
    def _last():
        step(True)
        for g in range(NSA_KV_HEADS):
            acc = acc_sc[g]
            _store_heads(o_ref, acc / pltpu.roll(acc, HEAD_DIM, 1), g, tq)


def _flash_sel(q8, rows3, selb):
    b, l, _ = rows3.shape
    tq, tk = min(SEL_TQ, l), min(SEL_TK, l)
    nq = l // tq
    pairs = [(qi, ki) for qi in range(nq) for ki in range((qi * tq + tq - 1) // tk + 1)]
    qi_tab = jnp.asarray([p[0] for p in pairs], jnp.int32)
    ki_tab = jnp.asarray([p[1] for p in pairs], jnp.int32)
    kv_idx = lambda blk: (lambda bi, si, qt, kt: (bi, kt[si], blk))
    return pl.pallas_call(
        functools.partial(_flash_sel_kernel, tq=tq, tk=tk),
        grid_spec=pltpu.PrefetchScalarGridSpec(
            num_scalar_prefetch=2, grid=(b, len(pairs)),
            in_specs=[pl.BlockSpec((NSA_HEADS, tq, LANES), lambda bi, si, qt, kt: (0, bi * nq + qt[si], 0)),
                      pl.BlockSpec((1, tk, KV_W), kv_idx(2)),
                      pl.BlockSpec((1, tk, KV_W), kv_idx(3)),
                      pl.BlockSpec((1, tq, LANES), lambda bi, si, qt, kt: (bi, qt[si], 0))],
            out_specs=pl.BlockSpec((1, tq, NSA_WIDTH), lambda bi, si, qt, kt: (bi, qt[si], 0)),
            scratch_shapes=[pltpu.VMEM((NSA_KV_HEADS, NSA_HPG * tq, LANES), F32),
                            pltpu.VMEM((NSA_KV_HEADS, NSA_HPG * tq, LANES), F32)]),
        out_shape=jax.ShapeDtypeStruct((b, l, NSA_WIDTH), F32),
        compiler_params=_cparams("parallel", "arbitrary"),
        name="flash_sel",
    )(qi_tab, ki_tab, q8, rows3, rows3, selb)


def _flash_win_kernel(q_ref, *refs, tq, back):
    nspan = back + 1
    k_refs, v_refs, o_ref = refs[:nspan], refs[nspan:2 * nspan], refs[2 * nspan]
    qi = pl.program_id(1)
    hq = NSA_HPG * tq
    span = nspan * tq
    qpos = qi * tq + lax.broadcasted_iota(jnp.int32, (tq, span), 0)
    kpos = (qi - back) * tq + lax.broadcasted_iota(jnp.int32, (tq, span), 1)
    ok = (kpos <= qpos) & (kpos > qpos - WINDOW) & (kpos >= 0)
    bias4 = jnp.concatenate([jnp.where(ok, 0.0, NEG_BIG)] * NSA_HPG, axis=0)
    kcat = jnp.concatenate([r[0] for r in k_refs], axis=0).astype(BF16)
    vcat = jnp.concatenate([r[0] for r in v_refs], axis=0).astype(BF16)
    for g in range(NSA_KV_HEADS):
        q4 = q_ref[g * NSA_HPG:(g + 1) * NSA_HPG].reshape(hq, LANES)
        qx = jnp.where(_group_lanes((hq, LANES), g), q4, 0.0)
        s = lax.dot_general(qx, kcat, (((1,), (1,)), ((), ())), preferred_element_type=F32) + bias4
        p = jnp.exp2(s - jnp.max(s, axis=-1, keepdims=True)).astype(BF16)
        vx = jnp.where(_group_lanes((span, LANES), g), vcat, 1.0)
        r = jnp.dot(p, vx, preferred_element_type=F32)
        _store_heads(o_ref, r / pltpu.roll(r, HEAD_DIM, 1), g, tq)


def _flash_win(q8, win3):
    b, l, _ = win3.shape
    tq = min(WIN_TQ, l)
    nq = l // tq
    back = -(-(WINDOW - 1) // tq)

    def kv_specs(blk):
        return [pl.BlockSpec((1, tq, KV_W), (lambda bi, qi, j=j: (bi, jnp.maximum(qi - back + j, 0), blk)))
                for j in range(back + 1)]

    return pl.pallas_call(
        functools.partial(_flash_win_kernel, tq=tq, back=back),
        grid=(b, nq),
        in_specs=[pl.BlockSpec((NSA_HEADS, tq, LANES), lambda bi, qi: (0, bi * nq + qi, 0))]
        + kv_specs(0) + kv_specs(1),
        out_specs=pl.BlockSpec((1, tq, NSA_WIDTH), lambda bi, qi: (bi, qi, 0)),
        out_shape=jax.ShapeDtypeStruct((b, l, NSA_WIDTH), F32),
        compiler_params=_cparams("parallel", "parallel"),
        name="flash_win",
    )(q8, *([win3] * (2 * (back + 1))))


def _softmax_update(s, m_prev, l_prev):
    m_new = jnp.maximum(m_prev, jnp.max(s, axis=-1, keepdims=True))
    alpha = jnp.exp2(m_prev - m_new)
    p = jnp.exp2(s - m_new[:, :1])
    return m_new, alpha, alpha * l_prev + jnp.sum(p, axis=-1, keepdims=True), p


def _paged_attn_kernel(pt_ref, *refs, npg, page, l_new, nbp):
    page_refs = refs[:npg]
    (qz_ref, sel_ref, kn_ref, vn_ref, win_ref, kwn_ref, vwn_ref,
     oslc_ref, owin_ref, wout_ref, m_sc, l_sc, acc_sc) = refs[npg:]
    ji = pl.program_id(1)
    nr = qz_ref.shape[1]
    span = npg * page

    @pl.when(ji == 0)
    def _init():
        m_sc[...] = jnp.full(m_sc.shape, NEG_BIG, F32)
        l_sc[...] = jnp.zeros(l_sc.shape, F32)
        acc_sc[...] = jnp.zeros(acc_sc.shape, F32)

    qz = qz_ref[0]
    kt = jnp.concatenate([page_refs[i][0, 0:KV_W, :] for i in range(npg)], axis=1).astype(BF16)
    vt = jnp.concatenate([page_refs[i][0, KV_W:2 * KV_W, :] for i in range(npg)], axis=1).astype(BF16)
    s = jnp.dot(qz, kt, preferred_element_type=F32)
    e_row = lax.broadcasted_iota(jnp.int32, (nbp, span), 0)
    e_col = lax.broadcasted_iota(jnp.int32, (nbp, span), 1)
    expand = jnp.where(e_row == ji * (span // SEL_BLOCK) + e_col // SEL_BLOCK, 1.0, 0.0).astype(BF16)
    picked = jnp.dot(sel_ref[0], expand, preferred_element_type=F32) > 0.5
    m_new, alpha, l_new_v, p = _softmax_update(jnp.where(picked, s, NEG_BIG), m_sc[...], l_sc[...])
    m_sc[...] = m_new
    l_sc[...] = l_new_v
    acc_sc[...] = alpha * acc_sc[...] + _dot_nt(p, vt)

    @pl.when(ji == pl.num_programs(1) - 1)
    def _fin():
        q_of_row = lax.broadcasted_iota(jnp.int32, (nr, l_new), 0) % l_new
        j_new = lax.broadcasted_iota(jnp.int32, (nr, l_new), 1)
        new_ok = j_new <= q_of_row
        sn = jnp.where(new_ok, _dot_nt(qz, kn_ref[0]), NEG_BIG)
        m2, a2, l2, p2 = _softmax_update(sn, m_sc[...], l_sc[...])
        oslc_ref[0] = (a2 * acc_sc[...] + _dot(p2, vn_ref[0])) / l2
        wlen = win_ref.shape[2]
        kw = win_ref[0, 0:KV_W, :]
        vw = win_ref[0, KV_W:2 * KV_W, :]
        i_old = lax.broadcasted_iota(jnp.int32, (nr, wlen), 1)
        q_old = lax.broadcasted_iota(jnp.int32, (nr, wlen), 0) % l_new
        sw = jnp.where(i_old + (WINDOW - wlen) > q_old, _dot(qz, kw), NEG_BIG)
        swn = jnp.where(new_ok, _dot_nt(qz, kwn_ref[0]), NEG_BIG)
        mw = jnp.maximum(jnp.max(sw, axis=-1, keepdims=True), jnp.max(swn, axis=-1, keepdims=True))
        pw = jnp.exp2(sw - mw)
        pwn = jnp.exp2(swn - mw)
        lw = jnp.sum(pw, axis=-1, keepdims=True) + jnp.sum(pwn, axis=-1, keepdims=True)
        owin_ref[0] = (_dot_nt(pw, vw) + _dot(pwn, vwn_ref[0])) / lw
        new_t = jnp.concatenate([kwn_ref[0], vwn_ref[0]], axis=1).T
        place = (lax.broadcasted_iota(jnp.int32, (l_new, wlen), 1)
                 == lax.broadcasted_iota(jnp.int32, (l_new, wlen), 0) + (wlen - l_new)).astype(F32)
        lane = lax.broadcasted_iota(jnp.int32, (2 * KV_W, wlen), 1)
        wout_ref[0] = jnp.where(lane < wlen - l_new, pltpu.roll(win_ref[0], wlen - l_new, 1), _dot_f32(new_t, place))


def _paged_attn(cache_t, page_table, pool_off, qz, sel_rows, rows3, win_t, win_off, win3):
    b, n_pages = page_table.shape
    page = cache_t.shape[2]
    npg = math.gcd(n_pages, PAGES_PER_STEP)
    nr = qz.shape[1]
    l_new = rows3.shape[1]
    nbp = sel_rows.shape[2]
    wlen = win_t.shape[2]
    kern = functools.partial(_paged_attn_kernel, npg=npg, page=page, l_new=l_new, nbp=nbp)
    per_b = lambda bi, ji, pt: (bi, 0, 0)
    return pl.pallas_call(
        kern,
        grid_spec=pltpu.PrefetchScalarGridSpec(
            num_scalar_prefetch=1, grid=(b, n_pages // npg),
            in_specs=_page_specs(npg, page, 1, pool_off)
            + [pl.BlockSpec((1, nr, KV_W), per_b),
               pl.BlockSpec((1, nr, nbp), per_b),
               pl.BlockSpec((1, l_new, KV_W), lambda bi, ji, pt: (bi, 0, 2)),
               pl.BlockSpec((1, l_new, KV_W), lambda bi, ji, pt: (bi, 0, 3)),
               pl.BlockSpec((1, 2 * KV_W, wlen), lambda bi, ji, pt: (bi + win_off, 0, 0)),
               pl.BlockSpec((1, l_new, KV_W), lambda bi, ji, pt: (bi, 0, 0)),
               pl.BlockSpec((1, l_new, KV_W), lambda bi, ji, pt: (bi, 0, 1))],
            out_specs=[pl.BlockSpec((1, nr, KV_W), per_b),
                       pl.BlockSpec((1, nr, KV_W), per_b),
                       pl.BlockSpec((1, 2 * KV_W, wlen), per_b)],
            scratch_shapes=[pltpu.VMEM((nr, LANES), F32), pltpu.VMEM((nr, LANES), F32),
                            pltpu.VMEM((nr, KV_W), F32)]),
        out_shape=[jax.ShapeDtypeStruct((b, nr, KV_W), F32),
                   jax.ShapeDtypeStruct((b, nr, KV_W), F32),
                   jax.ShapeDtypeStruct((b, 2 * KV_W, wlen), F32)],
        compiler_params=_cparams("parallel", "arbitrary"),
        name="paged_attn",
    )(page_table, *([cache_t] * npg), qz, sel_rows, rows3, rows3, win_t, win3, win3)


def _nsa_fresh(u_nsa, b, l, win_keep, q_g, k_g, cmp_pos, cmp_w):
    qn, qr, rows, win = _nsa_prep(u_nsa, jnp.arange(l), q_g, k_g, l)
    rows3 = rows.reshape(b, l, N_KV_SLOTS * KV_W)
    win3 = win.reshape(b, l, 2 * KV_W)
    wcat, bias = _cmp_weights(cmp_pos, cmp_w)
    t_use = (l // CMP_STRIDE) * CMP_STRIDE
    kc, vc = _cmp_kv(rows3, 0, 1, t_use, wcat, bias, k_g[0])
    o_cmp, selb = _cmp_attn(qn.reshape(b, l, NSA_WIDTH), kc, vc, 0, l, bias_out=True)
    o_slc = _flash_sel(qr, rows3, selb)
    o_win = _flash_win(qr, win3)
    branches = tuple(o.reshape(b * l, NSA_WIDTH) for o in (o_cmp, o_slc, o_win))
    new_rows = rows3.reshape(b, l, N_KV_SLOTS, NSA_KV_HEADS, HEAD_DIM)
    new_win = win3[:, l - win_keep:].reshape(b, win_keep, 2, NSA_KV_HEADS, HEAD_DIM)
    return branches, new_rows, new_win


def _nsa_paged(u_nsa, b, l, paged, q_g, k_g, cmp_pos, cmp_w):
    cache_t, page_table, pool_off, win_t, win_off = paged
    past_len = page_table.shape[1] * cache_t.shape[2]
    assert (past_len + l) // CMP_STRIDE == past_len // CMP_STRIDE and past_len % SEL_BLOCK == 0
    qn, qr, rows, win = _nsa_prep(u_nsa, past_len + jnp.arange(l), q_g, k_g, l)
    rows3 = rows.reshape(b, l, N_KV_SLOTS * KV_W)
    win3 = win.reshape(b, l, 2 * KV_W)
    wcat, bias = _cmp_weights(cmp_pos, cmp_w)
    kc, vc = _cmp_kv_paged(cache_t, page_table, pool_off, wcat, bias, k_g[0])
    o_cmp, sel_t = _cmp_attn(qn.reshape(b, l, NSA_WIDTH), kc, vc, past_len, past_len + l)
    nbp = sel_t.shape[1] // NSA_KV_HEADS
    sel_rows = jnp.swapaxes(sel_t.reshape(b, NSA_KV_HEADS, 1, nbp, l), 3, 4)
    sel_rows = jnp.broadcast_to(sel_rows, (b, NSA_KV_HEADS, NSA_HPG, l, nbp)).reshape(b, NSA_HEADS * l, nbp)
    q5 = jnp.transpose(qr[:, :, :HEAD_DIM].reshape(NSA_KV_HEADS, NSA_HPG, b, l, HEAD_DIM), (2, 0, 1, 3, 4))
    qz = jnp.einsum('bghqd,gk->bghqkd', q5, jnp.eye(NSA_KV_HEADS, dtype=q5.dtype)).reshape(b, NSA_HEADS * l, KV_W)
    o_slc_z, o_win_z, wout = _paged_attn(cache_t, page_table, pool_off, qz, sel_rows, rows3, win_t, win_off, win3)

    def own_group(o):
        o6 = o.reshape(b, NSA_KV_HEADS, NSA_HPG, l, NSA_KV_HEADS, HEAD_DIM)
        d = jnp.stack([o6[:, g, :, :, g, :] for g in range(NSA_KV_HEADS)], axis=1)
        return jnp.transpose(d, (0, 3, 1, 2, 4)).reshape(b * l, NSA_WIDTH)

    branches = (o_cmp.reshape(b * l, NSA_WIDTH), own_group(o_slc_z), own_group(o_win_z))
    new_rows = rows3.reshape(b, l, N_KV_SLOTS, NSA_KV_HEADS, HEAD_DIM)
    wlen = wout.shape[2]
    new_win = jnp.transpose(wout.reshape(b, 2, NSA_KV_HEADS, HEAD_DIM, wlen), (0, 4, 1, 2, 3))
    return branches, new_rows, new_win


PROJ_ROWS = 512


def _hybrid_layer(x, paged, win_keep, s_gla, c_ml, n_ml, m_ml, conv_ml,
                  norm_g, w_in_pad, w_out_bf, gla_w_gate, gla_b_gate, gla_norm_g,
                  nsa_q_norm_g, nsa_k_norm_g, nsa_cmp_pos, nsa_cmp_w,
                  ml_conv_w, ml_conv_b, ml_gate_b, ml_norm_g):
    b, l, _ = x.shape
    rows = b * l
    tm = math.gcd(rows, PROJ_ROWS)
    x2d = x.reshape(rows, D_MODEL)
    u_gla, u_nsa, u_ml = _proj_in(x2d, norm_g, w_in_pad, tm)

    nk = GLA_HEADS * GLA_DK
    wg = jnp.pad(gla_w_gate.astype(F32), ((0, LANES - GLA_RANK), (0, 0))).astype(BF16)
    o_a, st_new = _gla(u_gla, _gla_state_in(s_gla), wg, gla_b_gate.astype(F32).reshape(1, nk),
                       jnp.tile(gla_norm_g.astype(F32).reshape(1, GLA_DV), (1, GLA_HEADS)), b, l)
    s_new = _gla_state_out(st_new)

    if paged is None:
        o_nsa, new_rows, new_win = _nsa_fresh(u_nsa, b, l, win_keep, nsa_q_norm_g, nsa_k_norm_g,
                                              nsa_cmp_pos, nsa_cmp_w)
    else:
        o_nsa, new_rows, new_win = _nsa_paged(u_nsa, b, l, paged, nsa_q_norm_g, nsa_k_norm_g,
                                              nsa_cmp_pos, nsa_cmp_w)

    o_c, c_new, n_new, m_new, conv_new = _mlstm(u_ml, c_ml, n_ml, m_ml, conv_ml, ml_conv_w, ml_conv_b,
                                                ml_gate_b, ml_norm_g, b, l)

    y = _proj_out(o_a.reshape(rows, GLA_WIDTH), *o_nsa, u_nsa, o_c.reshape(rows, ML_WIDTH), x2d, w_out_bf, tm)
    return y.reshape(b, l, D_MODEL), new_rows, new_win, s_new, c_new, n_new, m_new, conv_new


def kernel(x_prompt, x_sample, cache_nsa_kv, state_nsa_win, state_gla, state_mlstm_C, state_mlstm_n,
           state_mlstm_m, state_mlstm_conv, page_table, norm_g, w_in, w_out, gla_w_gate, gla_b_gate,
           gla_norm_g, nsa_q_norm_g, nsa_k_norm_g, nsa_cmp_pos, nsa_cmp_w, ml_conv_w, ml_conv_b,
           ml_gate_b, ml_norm_g):
    bp, sp, _ = x_prompt.shape
    bs, _, _ = x_sample.shape
    depth = w_in.shape[0]
    dt = x_prompt.dtype
    zero_gla = jnp.zeros((bp, GLA_HEADS, GLA_DK, GLA_DV), F32)
    zero_c = jnp.zeros((bp, ML_HEADS, ML_DH, ML_DH), F32)
    zero_n = jnp.zeros((bp, ML_HEADS, ML_DH), F32)
    zero_m = jnp.zeros((bp, ML_HEADS), F32)
    zero_conv = jnp.zeros((bp, CONV_W - 1, 2 * ML_WIDTH), dt)
    keep_p = min(WINDOW, sp)
    keep_s = state_nsa_win.shape[2]
    n_pool = cache_nsa_kv.shape[1]
    cache_t = jnp.transpose(cache_nsa_kv, (0, 1, 3, 4, 5, 2)).reshape(
        depth * n_pool, N_KV_SLOTS * KV_W, cache_nsa_kv.shape[2]).astype(F32)
    win_t = jnp.transpose(state_nsa_win, (0, 1, 3, 4, 5, 2)).reshape(depth * bs, 2 * KV_W, keep_s).astype(F32)

    y_prompt, y_sample = x_prompt, x_sample
    p_layers, s_layers = [], []
    for layer in range(depth):
        w = (norm_g[layer], _pad_w_in(w_in[layer]), w_out[layer].astype(BF16), gla_w_gate[layer],
             gla_b_gate[layer], gla_norm_g[layer],
             nsa_q_norm_g[layer], nsa_k_norm_g[layer], nsa_cmp_pos[layer], nsa_cmp_w[layer],
             ml_conv_w[layer], ml_conv_b[layer], ml_gate_b[layer], ml_norm_g[layer])
        y_prompt, *p_new = _hybrid_layer(y_prompt, None, keep_p, zero_gla, zero_c, zero_n,
                                         zero_m, zero_conv, *w)
        paged = (cache_t, page_table, layer * n_pool, win_t, layer * bs)
        y_sample, *s_new = _hybrid_layer(y_sample, paged, keep_s,
                                         state_gla[layer], state_mlstm_C[layer], state_mlstm_n[layer],
                                         state_mlstm_m[layer], state_mlstm_conv[layer], *w)
        p_layers.append(p_new)
        s_layers.append(s_new)
    p_kv, p_win, p_gla, p_c, p_n, p_m, p_conv = [jnp.stack(z) for z in zip(*p_layers)]
    s_kv, s_win, s_gla, s_c, s_n, s_m, s_conv = [jnp.stack(z) for z in zip(*s_layers)]
    return (y_prompt, y_sample, p_kv, s_kv, p_win, s_win, p_gla, s_gla, p_c, s_c, p_n, s_n, p_m, s_m, p_conv, s_conv)
```

```python
import functools
import math

import jax
import jax.numpy as jnp
import numpy as np
from jax import lax
from jax.experimental import pallas as pl
from jax.experimental.pallas import tpu as pltpu

F32 = jnp.float32
BF16 = jnp.bfloat16
HIGHEST = lax.Precision.HIGHEST

D_MODEL = 1024
HEAD_DIM = 64
GLA_WIDTH = D_MODEL // 4
NSA_WIDTH = D_MODEL // 2
ML_WIDTH = D_MODEL - GLA_WIDTH - NSA_WIDTH
D_MIX = GLA_WIDTH + NSA_WIDTH + ML_WIDTH

GLA_HEADS = GLA_WIDTH // HEAD_DIM
GLA_DK = HEAD_DIM // 2
GLA_DV = HEAD_DIM
GLA_RANK = 16
GLA_TAU = 16.0
GLA_CHUNK = 64

NSA_HEADS = NSA_WIDTH // HEAD_DIM
NSA_KV_HEADS = 2
NSA_HPG = NSA_HEADS // NSA_KV_HEADS
CMP_BLOCK = 32
CMP_STRIDE = 16
SEL_BLOCK = 64
N_SELECT = 16
WINDOW = 512
Q_BLOCK = 128
N_KV_SLOTS = 4
ROT_DIM = HEAD_DIM // 4
ROPE_THETA = 500000.0
ATTN_SCALE = HEAD_DIM ** -0.5

ML_HEADS = ML_WIDTH // HEAD_DIM
ML_DH = HEAD_DIM
ML_CHUNK = 64
CONV_W = 4

SPLIT_SIZES = (GLA_HEADS * GLA_DK, GLA_HEADS * GLA_DK, GLA_WIDTH, GLA_RANK, GLA_WIDTH,
               NSA_WIDTH, 6 * NSA_KV_HEADS * HEAD_DIM, 3 * NSA_HEADS, NSA_WIDTH,
               2 * ML_WIDTH, ML_WIDTH, 2 * ML_HEADS, ML_WIDTH, ML_WIDTH)

LANES = 128
SUBLANES = 8
VMEM_LIMIT = 56 * 1024 * 1024
NEG_BIG = -1e30
EPS = 1e-6
LOG2E = math.log2(math.e)


def _round_up(n, m):
    return -(-n // m) * m


PAD_SIZES = tuple(_round_up(s, LANES) for s in SPLIT_SIZES)
D_IN_PAD = sum(PAD_SIZES)
W_GLA = sum(PAD_SIZES[0:5])
W_NSA = sum(PAD_SIZES[5:9])
W_ML = sum(PAD_SIZES[9:14])
KV_W = NSA_KV_HEADS * HEAD_DIM


def _dot(a, b):
    return jnp.dot(a.astype(BF16), b.astype(BF16), preferred_element_type=F32)


def _dot_nt(a, b):
    return lax.dot_general(a.astype(BF16), b.astype(BF16), (((1,), (1,)), ((), ())), preferred_element_type=F32)


def _dot_tn(a, b):
    return lax.dot_general(a.astype(BF16), b.astype(BF16), (((0,), (0,)), ((), ())), preferred_element_type=F32)


def _dot_f32(a, b):
    return jnp.dot(a, b, precision=HIGHEST, preferred_element_type=F32)


def _dot_split(a, b):
    a_hi = a.astype(BF16)
    a_lo = (a - a_hi.astype(F32)).astype(BF16)
    bb = b.astype(BF16)
    return jnp.dot(a_hi, bb, preferred_element_type=F32) + jnp.dot(a_lo, bb, preferred_element_type=F32)


def _log_sigmoid(x):
    return jnp.minimum(x, 0.0) - jnp.log1p(jnp.exp(-jnp.abs(x)))


def _sigmoid(x):
    return 1.0 / (1.0 + jnp.exp(-x))


def _silu(x):
    return x * _sigmoid(x)


def _group_mean_matrix(width):
    g = np.kron(np.eye(width // HEAD_DIM, dtype=np.float32), np.full((HEAD_DIM, HEAD_DIM), 1.0 / HEAD_DIM, np.float32))
    return jnp.asarray(g, dtype=BF16)


def _group_norm(x, gmat, gain):
    ms = _dot_split(x * x, gmat)
    return x * lax.rsqrt(ms + EPS) * gain


def _cparams(*sem):
    return pltpu.CompilerParams(dimension_semantics=sem, vmem_limit_bytes=VMEM_LIMIT)


def _proj_in_kernel(x_ref, g_ref, w_ref, ug_ref, un_ref, um_ref):
    x = x_ref[...]
    y = x * lax.rsqrt(jnp.mean(x * x, axis=-1, keepdims=True) + EPS) * g_ref[...]
    r = jnp.dot(y.astype(BF16), w_ref[...], preferred_element_type=F32)
    ug_ref[...] = r[:, 0:W_GLA]
    un_ref[...] = r[:, W_GLA:W_GLA + W_NSA]
    um_ref[...] = r[:, W_GLA + W_NSA:D_IN_PAD]


def _proj_in(x2d, g, w_pad, tm):
    rows = x2d.shape[0]
    return pl.pallas_call(
        _proj_in_kernel,
        grid=(rows // tm,),
        in_specs=[pl.BlockSpec((tm, D_MODEL), lambda i: (i, 0)),
                  pl.BlockSpec((1, D_MODEL), lambda i: (0, 0)),
                  pl.BlockSpec((D_MODEL, D_IN_PAD), lambda i: (0, 0))],
        out_specs=[pl.BlockSpec((tm, W_GLA), lambda i: (i, 0)),
                   pl.BlockSpec((tm, W_NSA), lambda i: (i, 0)),
                   pl.BlockSpec((tm, W_ML), lambda i: (i, 0))],
        out_shape=[jax.ShapeDtypeStruct((rows, W_GLA), F32),
                   jax.ShapeDtypeStruct((rows, W_NSA), F32),
                   jax.ShapeDtypeStruct((rows, W_ML), F32)],
        compiler_params=_cparams("parallel"),
        name="proj_in",
    )(x2d, g.reshape(1, D_MODEL), w_pad)


def _pad_w_in(w_in):
    parts = []
    off = 0
    for s, p in zip(SPLIT_SIZES, PAD_SIZES):
        seg = w_in[:, off:off + s]
        if p != s:
            seg = jnp.pad(seg, ((0, 0), (0, p - s)))
        parts.append(seg)
        off += s
    return jnp.concatenate(parts, axis=1).astype(BF16)


def _proj_out_kernel(oa_ref, ocmp_ref, oslc_ref, owin_ref, gz_ref, oc_ref, x_ref, w_ref, e_ref, y_ref):
    gate = _sigmoid(gz_ref[:, 0:LANES])
    ob = (_dot_split(gate, e_ref[0]) * ocmp_ref[...] + _dot_split(gate, e_ref[1]) * oslc_ref[...]
          + _dot_split(gate, e_ref[2]) * owin_ref[...]) * _silu(gz_ref[:, LANES:LANES + NSA_WIDTH])
    y = x_ref[...]
    y = y + _dot(oa_ref[...], w_ref[0:GLA_WIDTH, :])
    y = y + _dot(ob, w_ref[GLA_WIDTH:GLA_WIDTH + NSA_WIDTH, :])
    y = y + _dot(oc_ref[...], w_ref[GLA_WIDTH + NSA_WIDTH:D_MIX, :])
    y_ref[...] = y


def _gate_expand():
    e = np.zeros((3, LANES, NSA_WIDTH), np.float32)
    for j in range(3):
        for h in range(NSA_HEADS):
            e[j, j * NSA_HEADS + h, h * HEAD_DIM:(h + 1) * HEAD_DIM] = 1.0
    return jnp.asarray(e, dtype=BF16)


def _proj_out(oa, o_cmp, o_slc, o_win, u_nsa, oc, x2d, w_bf, tm):
    rows = x2d.shape[0]
    gz_w = LANES + NSA_WIDTH
    gz_blk = (NSA_WIDTH + 6 * KV_W) // gz_w
    assert gz_blk * gz_w == NSA_WIDTH + 6 * KV_W
    row = lambda i: (i, 0)
    return pl.pallas_call(
        _proj_out_kernel,
        grid=(rows // tm,),
        in_specs=[pl.BlockSpec((tm, GLA_WIDTH), row),
                  pl.BlockSpec((tm, NSA_WIDTH), row),
                  pl.BlockSpec((tm, NSA_WIDTH), row),
                  pl.BlockSpec((tm, NSA_WIDTH), row),
                  pl.BlockSpec((tm, gz_w), lambda i: (i, gz_blk)),
                  pl.BlockSpec((tm, ML_WIDTH), row),
                  pl.BlockSpec((tm, D_MODEL), row),
                  pl.BlockSpec((D_MIX, D_MODEL), lambda i: (0, 0)),
                  pl.BlockSpec((3, LANES, NSA_WIDTH), lambda i: (0, 0, 0))],
        out_specs=pl.BlockSpec((tm, D_MODEL), row),
        out_shape=jax.ShapeDtypeStruct((rows, D_MODEL), F32),
        compiler_params=_cparams("parallel"),
        name="proj_out",
    )(oa, o_cmp, o_slc, o_win, u_nsa, oc, x2d, w_bf, _gate_expand())


def _gla_kernel(u_ref, st0_ref, wg_ref, bg_ref, ng_ref, gm_ref, o_ref, st_ref, *, bb, tl, c):
    @pl.when(pl.program_id(1) == 0)
    def _init():
        st_ref[...] = st0_ref[...]

    nk = GLA_HEADS * GLA_DK
    a_off = 2 * nk + GLA_WIDTH
    q, k, v, log_a = [], [], [], []
    for i in range(bb):
        q.append(u_ref[i, :, 0:nk] * (GLA_DK ** -0.5))
        k.append(u_ref[i, :, nk:2 * nk])
        v.append(u_ref[i, :, 2 * nk:2 * nk + GLA_WIDTH])
        pre = _dot(u_ref[i, :, a_off:a_off + LANES], wg_ref[...]) + bg_ref[...]
        log_a.append(_log_sigmoid(pre) * (1.0 / GLA_TAU))

    row_c = lax.broadcasted_iota(jnp.int32, (c, nk), 0)
    shifts = [1 << t for t in range(c.bit_length()) if (1 << t) < c]
    hc = GLA_HEADS * c
    tri_h = (lax.broadcasted_iota(jnp.int32, (hc, c), 0) % c) >= lax.broadcasted_iota(jnp.int32, (hc, c), 1)
    k_head = lax.broadcasted_iota(jnp.int32, (1, nk), 1) // GLA_DK
    v_head = lax.broadcasted_iota(jnp.int32, (1, GLA_WIDTH), 1) // GLA_DV
    st_diag = (lax.broadcasted_iota(jnp.int32, (GLA_WIDTH, nk), 0) // GLA_DV
               == lax.broadcasted_iota(jnp.int32, (GLA_WIDTH, nk), 1) // GLA_DK)

    st = [st_ref[i] for i in range(bb)]
    outs = [[] for _ in range(bb)]
    for j in range(tl // c):
        sl = slice(j * c, (j + 1) * c)
        for i in range(bb):
            b = log_a[i][sl]
            for sh in shifts:
                b = b + jnp.where(row_c >= sh, pltpu.roll(b, sh, 0), 0.0)
            blast = b[c - 1:c]
            qe = q[i][sl] * jnp.exp(b)
            ke = k[i][sl] * jnp.exp(-b)
            kl = k[i][sl] * jnp.exp(blast - b)
            vc = v[i][sl]
            qx = jnp.concatenate([jnp.where(k_head == h, qe, 0.0) for h in range(GLA_HEADS)], axis=0)
            a = jnp.where(tri_h, _dot_nt(qx, ke), 0.0)
            r = _dot(a, vc)
            o = _dot_nt(qe, st[i])
            for h in range(GLA_HEADS):
                o = o + jnp.where(v_head == h, r[h * c:(h + 1) * c], 0.0)
            st[i] = st[i] * jnp.exp(blast) + jnp.where(st_diag, _dot_tn(vc, kl), 0.0)
            outs[i].append(o)
    for i in range(bb):
        st_ref[i] = st[i]
        o = outs[i][0] if len(outs[i]) == 1 else jnp.concatenate(outs[i], axis=0)
        z = u_ref[i, :, a_off + LANES:a_off + LANES + GLA_WIDTH]
        o_ref[i] = _group_norm(o, gm_ref[...], ng_ref[...]) * _silu(z)


GLA_SEQS_PER_STEP = 4


def _gla(u_gla, st0, wg, bg, ng, b, l):
    tl = min(l, 128)
    c = min(l, 16)
    bb = math.gcd(b, GLA_SEQS_PER_STEP)
    nk = GLA_HEADS * GLA_DK
    u3 = u_gla.reshape(b, l, W_GLA)
    kern = functools.partial(_gla_kernel, bb=bb, tl=tl, c=c)
    return pl.pallas_call(
        kern,
        grid=(b // bb, l // tl),
        in_specs=[pl.BlockSpec((bb, tl, W_GLA), lambda bi, li: (bi, li, 0)),
                  pl.BlockSpec((bb, GLA_WIDTH, nk), lambda bi, li: (bi, 0, 0)),
                  pl.BlockSpec((LANES, nk), lambda bi, li: (0, 0)),
                  pl.BlockSpec((1, nk), lambda bi, li: (0, 0)),
                  pl.BlockSpec((1, GLA_WIDTH), lambda bi, li: (0, 0)),
                  pl.BlockSpec((GLA_WIDTH, GLA_WIDTH), lambda bi, li: (0, 0))],
        out_specs=[pl.BlockSpec((bb, tl, GLA_WIDTH), lambda bi, li: (bi, li, 0)),
                   pl.BlockSpec((bb, GLA_WIDTH, nk), lambda bi, li: (bi, 0, 0))],
        out_shape=[jax.ShapeDtypeStruct((b, l, GLA_WIDTH), F32),
                   jax.ShapeDtypeStruct((b, GLA_WIDTH, nk), F32)],
        compiler_params=_cparams("parallel", "arbitrary"),
        name="gla",
    )(u3, st0, wg, bg, ng, _group_mean_matrix(GLA_WIDTH))


def _gla_state_in(s):
    b = s.shape[0]
    st = jnp.swapaxes(s.astype(F32), 2, 3)
    eye = jnp.eye(GLA_HEADS, dtype=F32)
    full = st[:, :, :, None, :] * eye[None, :, None, :, None]
    return full.reshape(b, GLA_WIDTH, GLA_HEADS * GLA_DK)


def _gla_state_out(st):
    b = st.shape[0]
    full = st.reshape(b, GLA_HEADS, GLA_DV, GLA_HEADS, GLA_DK)
    diag = jnp.stack([full[:, h, :, h, :] for h in range(GLA_HEADS)], axis=1)
    return jnp.swapaxes(diag, 2, 3)


def _mlstm_kernel(u_ref, c0_ref, n0_ref, m0_ref, cv0_ref, cw_ref, cb_ref, gb_ref, ng_ref, gm_ref,
                  o_ref, c_ref, n_ref, m_ref, cv_ref, xp_sc, *, bb, tl):
    @pl.when(pl.program_id(1) == 0)
    def _init():
        c_ref[...] = c0_ref[...]
        n_ref[...] = n0_ref[...]
        m_ref[...] = m0_ref[...]
        xp_sc[:, 0:SUBLANES, :] = cv0_ref[...]

    pairs = ML_HEADS // 2
    state = [([c_ref[i, j] for j in range(pairs)],
              [n_ref[i, j:j + 1, :] for j in range(pairs)],
              [m_ref[i, h:h + 1, :] for h in range(ML_HEADS)]) for i in range(bb)]
    new_state = [_mlstm_seq(i, state[i], u_ref, cw_ref, cb_ref, gb_ref, ng_ref, gm_ref, o_ref, cv_ref, xp_sc, tl)
                 for i in range(bb)]
    for i in range(bb):
        cps, nps, mbs = new_state[i]
        for j in range(pairs):
            c_ref[i, j] = cps[j]
            n_ref[i, j:j + 1, :] = nps[j]
        for h in range(ML_HEADS):
            m_ref[i, h:h + 1, :] = mbs[h]


def _mlstm_seq(i, state, u_ref, cw_ref, cb_ref, gb_ref, ng_ref, gm_ref, o_ref, cv_ref, xp_sc, tl):
    cps, nps, mbs = state
    c = tl
    w2 = 2 * ML_WIDTH
    u = u_ref[i]
    xp_sc[i, SUBLANES:SUBLANES + tl, :] = u[:, 0:w2]
    conv = cb_ref[...]
    for w in range(CONV_W):
        off = SUBLANES - (CONV_W - 1) + w
        conv = conv + xp_sc[i, off:off + tl, :] * cw_ref[w:w + 1, :]
    tail = xp_sc[i, tl:tl + SUBLANES, :]
    xp_sc[i, 0:SUBLANES, :] = tail
    cv_ref[i] = tail

    qk = _silu(conv)
    mq = qk[:, 0:ML_WIDTH]
    mk = qk[:, ML_WIDTH:w2] * (ML_DH ** -0.5)
    mv = u[:, w2:w2 + ML_WIDTH]
    ifg = u[:, w2 + ML_WIDTH:w2 + ML_WIDTH + LANES] + gb_ref[...]
    logf = _log_sigmoid(ifg)
    og_off = w2 + ML_WIDTH + LANES
    og = _sigmoid(u[:, og_off:og_off + ML_WIDTH])
    zz = _silu(u[:, og_off + ML_WIDTH:og_off + 2 * ML_WIDTH])

    tri = lax.broadcasted_iota(jnp.int32, (c, c), 0) >= lax.broadcasted_iota(jnp.int32, (c, c), 1)
    lane = lax.broadcasted_iota(jnp.int32, (c, LANES), 1)
    low = lane < ML_DH
    low_row = lax.broadcasted_iota(jnp.int32, (1, LANES), 1) < ML_DH
    sq_row = lax.broadcasted_iota(jnp.int32, (LANES, LANES), 0)
    sq_col = lax.broadcasted_iota(jnp.int32, (LANES, LANES), 1)
    same_head = (sq_row < ML_DH) == (sq_col < ML_DH)

    def wide(x):
        return x[:, :c] if c <= LANES else jnp.concatenate([x] * (c // LANES), axis=1)

    fcum_all = logf
    row_l = lax.broadcasted_iota(jnp.int32, (c, LANES), 0)
    for sh in [1 << t for t in range(c.bit_length()) if (1 << t) < c]:
        fcum_all = fcum_all + jnp.where(row_l >= sh, pltpu.roll(fcum_all, sh, 0), 0.0)
    gates_t = jnp.where(lane < ML_HEADS, ifg, fcum_all).T
    new_cps, new_nps, new_mbs = [], [], []
    for j in range(ML_HEADS // 2):
        ps = slice(j * LANES, (j + 1) * LANES)
        q_s, k_s, v_s = mq[:, ps], mk[:, ps], mv[:, ps]
        acc = jnp.zeros((c, 2 * LANES), F32)
        per_head = []
        for hl in range(2):
            h = 2 * j + hl
            own = low if hl == 0 else jnp.logical_not(low)
            fc = jnp.broadcast_to(fcum_all[:, ML_HEADS + h:ML_HEADS + h + 1], (c, LANES))
            ii = jnp.broadcast_to(ifg[:, h:h + 1], (c, LANES))
            dm = jnp.where(tri, wide(fc) - gates_t[ML_HEADS + h:ML_HEADS + h + 1, :] + gates_t[h:h + 1, :], NEG_BIG)
            inter = fc + mbs[h]
            m = jnp.maximum(inter, jnp.max(dm, axis=-1, keepdims=True))
            sij = _dot_nt(jnp.where(own, q_s, 0.0), k_s) * jnp.exp(dm - wide(m))
            acc = acc + _dot(sij, jnp.concatenate([jnp.where(own, v_s, 0.0), jnp.where(own, 1.0, 0.0)], axis=1))
            m_last = m[c - 1:c]
            f_last = fc[c - 1:c]
            per_head.append((m, jnp.exp(inter - m), jnp.exp(f_last - fc + ii - m_last),
                             jnp.exp(f_last + mbs[h] - m_last)))
            new_mbs.append(m_last)
        m_p, w_p, wj_p = (jnp.where(low, per_head[0][t], per_head[1][t]) for t in range(3))
        dec_row = jnp.where(low_row, per_head[0][3], per_head[1][3])
        n_mat = jnp.where(same_head, jnp.broadcast_to(nps[j], (LANES, LANES)), 0.0)
        num = w_p * _dot_nt(q_s, cps[j]) + acc[:, 0:LANES]
        den = w_p * _dot_nt(q_s, n_mat) + acc[:, LANES:2 * LANES]
        hh = num / jnp.maximum(jnp.abs(den), jnp.exp(-m_p))
        hn = hh * lax.rsqrt(_dot_split(hh * hh, gm_ref[...]) + EPS) * ng_ref[...]
        o_ref[i, :, ps] = hn * og[:, ps] * zz[:, ps]
        dec_mat = jnp.where(sq_row < ML_DH, jnp.broadcast_to(per_head[0][3], (LANES, LANES)),
                            jnp.broadcast_to(per_head[1][3], (LANES, LANES)))
        new_cps.append(dec_mat * cps[j] + jnp.where(same_head, _dot_tn(wj_p * v_s, k_s), 0.0))
        new_nps.append(dec_row * nps[j] + jnp.sum(wj_p * k_s, axis=0, keepdims=True))
    return new_cps, new_nps, new_mbs


ML_SEQS_PER_STEP = 2


def _mlstm(u_ml, c0, n0, m0, conv0, cw, cb, gb, ng, b, l):
    tl = min(l, 256)
    bb = math.gcd(b, ML_SEQS_PER_STEP)
    pairs = ML_HEADS // 2
    w2 = 2 * ML_WIDTH
    u3 = u_ml.reshape(b, l, W_ML)
    eye2 = jnp.eye(2, dtype=F32)
    c0p = jnp.einsum('bphed,hk->bphekd', c0.astype(F32).reshape(b, pairs, 2, ML_DH, ML_DH), eye2)
    c0p = c0p.reshape(b, pairs, LANES, LANES)
    n0p = n0.astype(F32).reshape(b, pairs, LANES)
    m0b = jnp.broadcast_to(m0.astype(F32)[:, :, None], (b, ML_HEADS, LANES))
    cv0 = jnp.pad(conv0.astype(F32), ((0, 0), (SUBLANES - (CONV_W - 1), 0), (0, 0)))
    gbp = jnp.pad(gb.astype(F32).reshape(1, 2 * ML_HEADS), ((0, 0), (0, LANES - 2 * ML_HEADS)))
    kern = functools.partial(_mlstm_kernel, bb=bb, tl=tl)
    st = lambda bi, li: (bi, 0, 0)
    st4 = lambda bi, li: (bi, 0, 0, 0)
    cst = lambda bi, li: (0, 0)
    o, c_new, n_new, m_new, cv = pl.pallas_call(
        kern,
        grid=(b // bb, l // tl),
        in_specs=[pl.BlockSpec((bb, tl, W_ML), lambda bi, li: (bi, li, 0)),
                  pl.BlockSpec((bb, pairs, LANES, LANES), st4),
                  pl.BlockSpec((bb, pairs, LANES), st),
                  pl.BlockSpec((bb, ML_HEADS, LANES), st),
                  pl.BlockSpec((bb, SUBLANES, w2), st),
                  pl.BlockSpec((CONV_W, w2), cst),
                  pl.BlockSpec((1, w2), cst),
                  pl.BlockSpec((1, LANES), cst),
                  pl.BlockSpec((1, LANES), cst),
                  pl.BlockSpec((LANES, LANES), cst)],
        out_specs=[pl.BlockSpec((bb, tl, ML_WIDTH), lambda bi, li: (bi, li, 0)),
                   pl.BlockSpec((bb, pairs, LANES, LANES), st4),
                   pl.BlockSpec((bb, pairs, LANES), st),
                   pl.BlockSpec((bb, ML_HEADS, LANES), st),
                   pl.BlockSpec((bb, SUBLANES, w2), st)],
        out_shape=[jax.ShapeDtypeStruct((b, l, ML_WIDTH), F32),
                   jax.ShapeDtypeStruct((b, pairs, LANES, LANES), F32),
                   jax.ShapeDtypeStruct((b, pairs, LANES), F32),
                   jax.ShapeDtypeStruct((b, ML_HEADS, LANES), F32),
                   jax.ShapeDtypeStruct((b, SUBLANES, w2), F32)],
        scratch_shapes=[pltpu.VMEM((bb, tl + 2 * SUBLANES, w2), F32)],
        compiler_params=_cparams("parallel", "arbitrary"),
        name="mlstm",
    )(u3, c0p, n0p, m0b, cv0, cw.astype(F32), cb.astype(F32).reshape(1, w2), gbp,
      jnp.tile(ng.astype(F32).reshape(1, ML_DH), (1, 2)), _group_mean_matrix(LANES))
    c6 = c_new.reshape(b, pairs, 2, ML_DH, 2, ML_DH)
    c_out = jnp.stack([c6[:, :, h, :, h, :] for h in range(2)], axis=2).reshape(b, ML_HEADS, ML_DH, ML_DH)
    return (o, c_out, n_new.reshape(b, ML_HEADS, ML_DH), m_new[:, :, 0], cv[:, SUBLANES - (CONV_W - 1):, :])


def _rope_lanes(x, cos_t, sin_t):
    w = x.shape[1]
    half = ROT_DIM // 2
    reps = w // cos_t.shape[1]
    if reps > 1:
        cos_t = jnp.concatenate([cos_t] * reps, axis=1)
        sin_t = jnp.concatenate([sin_t] * reps, axis=1)
    lane = lax.broadcasted_iota(jnp.int32, x.shape, 1) % HEAD_DIM
    partner = jnp.where(lane < half, pltpu.roll(x, w - half, 1), pltpu.roll(x, half, 1))
    return x * cos_t + partner * sin_t


def _nsa_prep_kernel(u_ref, cos_ref, sin_ref, qg_ref, kg_ref, g4_ref, g1_ref, qn_ref, qr_ref, rows_ref, win_ref):
    u = u_ref[...]
    cos_t = cos_ref[...]
    sin_t = sin_ref[...]
    q = _group_norm(u[:, 0:NSA_WIDTH], g4_ref[...], qg_ref[...])
    qn_ref[...] = (q * ATTN_SCALE).astype(BF16)
    qr = _rope_lanes(q, cos_t, sin_t) * (ATTN_SCALE * LOG2E)
    low = lax.broadcasted_iota(jnp.int32, (qr.shape[0], LANES), 1) < HEAD_DIM
    for j in range(NSA_HEADS // 2):
        pair = qr[:, j * LANES:(j + 1) * LANES]
        swapped = pltpu.roll(pair, HEAD_DIM, 1)
        qr_ref[2 * j] = jnp.where(low, pair, swapped).astype(BF16)
        qr_ref[2 * j + 1] = jnp.where(low, swapped, pair).astype(BF16)
    kv = NSA_WIDTH
    k_slc = _rope_lanes(_group_norm(u[:, kv + 2 * KV_W:kv + 3 * KV_W], g1_ref[...], kg_ref[1:2, :]), cos_t, sin_t)
    k_win = _rope_lanes(_group_norm(u[:, kv + 4 * KV_W:kv + 5 * KV_W], g1_ref[...], kg_ref[2:3, :]), cos_t, sin_t)
    rows_ref[:, 0:2 * KV_W] = u[:, kv:kv + 2 * KV_W]
    rows_ref[:, 2 * KV_W:3 * KV_W] = k_slc
    rows_ref[:, 3 * KV_W:4 * KV_W] = u[:, kv + 3 * KV_W:kv + 4 * KV_W]
    win_ref[:, 0:KV_W] = k_win
    win_ref[:, KV_W:2 * KV_W] = u[:, kv + 5 * KV_W:kv + 6 * KV_W]


def _rope_tables(pos):
    half = ROT_DIM // 2
    inv = jnp.exp(-math.log(ROPE_THETA) * jnp.arange(half, dtype=F32) * 2.0 / ROT_DIM)
    ang = pos.astype(F32)[:, None] * inv[None, :]
    cos, sin = jnp.cos(ang), jnp.sin(ang)
    n = pos.shape[0]
    ones = jnp.ones((n, HEAD_DIM - ROT_DIM), F32)
    cos_h = jnp.concatenate([cos, cos, ones], axis=1)
    sin_h = jnp.concatenate([-sin, sin, 0.0 * ones], axis=1)
    reps = LANES // HEAD_DIM
    return jnp.tile(cos_h, (1, reps)), jnp.tile(sin_h, (1, reps))


def _nsa_prep(u_nsa, pos, q_g, k_g, l):
    rows = u_nsa.shape[0]
    tl = min(l, PROJ_ROWS)
    nb = l // tl
    cos_t, sin_t = _rope_tables(pos)
    qg = jnp.tile(q_g.astype(F32).reshape(1, HEAD_DIM), (1, NSA_HEADS))
    kg = jnp.tile(k_g.astype(F32), (1, NSA_KV_HEADS))
    cst = lambda i: (0, 0)
    return pl.pallas_call(
        _nsa_prep_kernel,
        grid=(rows // tl,),
        in_specs=[pl.BlockSpec((tl, W_NSA), lambda i: (i, 0)),
                  pl.BlockSpec((tl, LANES), lambda i: (i % nb, 0)),
                  pl.BlockSpec((tl, LANES), lambda i: (i % nb, 0)),
                  pl.BlockSpec((1, NSA_WIDTH), cst),
                  pl.BlockSpec((3, KV_W), cst),
                  pl.BlockSpec((NSA_WIDTH, NSA_WIDTH), cst),
                  pl.BlockSpec((KV_W, KV_W), cst)],
        out_specs=[pl.BlockSpec((tl, NSA_WIDTH), lambda i: (i, 0)),
                   pl.BlockSpec((NSA_HEADS, tl, LANES), lambda i: (0, i, 0)),
                   pl.BlockSpec((tl, N_KV_SLOTS * KV_W), lambda i: (i, 0)),
                   pl.BlockSpec((tl, 2 * KV_W), lambda i: (i, 0))],
        out_shape=[jax.ShapeDtypeStruct((rows, NSA_WIDTH), BF16),
                   jax.ShapeDtypeStruct((NSA_HEADS, rows, LANES), BF16),
                   jax.ShapeDtypeStruct((rows, N_KV_SLOTS * KV_W), F32),
                   jax.ShapeDtypeStruct((rows, 2 * KV_W), F32)],
        compiler_params=_cparams("parallel"),
        name="nsa_prep",
    )(u_nsa, cos_t, sin_t, qg, kg, _group_mean_matrix(NSA_WIDTH), _group_mean_matrix(KV_W))


def _cmp_halves(xk_ref, xv_ref, w_ref, n_half):
    acc_k = jnp.zeros((n_half, 2 * KV_W), F32)
    acc_v = jnp.zeros((n_half, 2 * KV_W), F32)
    for s in range(CMP_STRIDE):
        acc_k = acc_k + _dot(xk_ref[pl.ds(s, n_half, stride=CMP_STRIDE), :], w_ref[0, s])
        acc_v = acc_v + _dot(xv_ref[pl.ds(s, n_half, stride=CMP_STRIDE), :], w_ref[1, s])
    return acc_k, acc_v


def _cmp_finish(acc_k, acc_v, bias_ref, kg_ref, g1_ref, kc_ref, vc_ref, n_half):
    valid = lax.broadcasted_iota(jnp.int32, (n_half, KV_W), 0) < n_half - 1

    def summary(acc, bias):
        return acc[:, 0:KV_W] + pltpu.roll(acc[:, KV_W:2 * KV_W], n_half - 1, 0) + bias

    kc = _group_norm(summary(acc_k, bias_ref[0:1, :]), g1_ref[...], kg_ref[...])
    kc_ref[0] = jnp.where(valid, kc, 0.0).astype(BF16)
    vc_ref[0] = jnp.where(valid, summary(acc_v, bias_ref[1:2, :]), 0.0).astype(BF16)


def _cmp_kv_kernel(xk_ref, xv_ref, w_ref, bias_ref, kg_ref, g1_ref, kc_ref, vc_ref, *, n_half):
    acc_k, acc_v = _cmp_halves(xk_ref.at[0], xv_ref.at[0], w_ref, n_half)
    _cmp_finish(acc_k, acc_v, bias_ref, kg_ref, g1_ref, kc_ref, vc_ref, n_half)


PAGES_PER_STEP = 64


def _cmp_paged_kernel(pt_ref, *refs, npg, page, whole):
    page_refs = refs[:npg]
    if whole:
        w_ref, bias_ref, kg_ref, g1_ref, kc_ref, vc_ref, xk_sc, xv_sc = refs[npg:]
    else:
        w_ref, acck_ref, accv_ref, xk_sc, xv_sc = refs[npg:]
    for i in range(npg):
        xk_sc[i * page:(i + 1) * page, :] = page_refs[i][0, 0:KV_W, :].T
        xv_sc[i * page:(i + 1) * page, :] = page_refs[i][0, KV_W:2 * KV_W, :].T
    n_half = npg * page // CMP_STRIDE
    acc_k, acc_v = _cmp_halves(xk_sc, xv_sc, w_ref, n_half)
    if whole:
        _cmp_finish(acc_k, acc_v, bias_ref, kg_ref, g1_ref, kc_ref, vc_ref, n_half)
    else:
        acck_ref[0] = acc_k
        accv_ref[0] = acc_v


def _cmp_fin_kernel(acck_ref, accv_ref, bias_ref, kg_ref, g1_ref, kc_ref, vc_ref, *, n_half):
    _cmp_finish(acck_ref[0], accv_ref[0], bias_ref, kg_ref, g1_ref, kc_ref, vc_ref, n_half)


def _page_specs(npg, page, row_blk, pool_off):
    def spec(i):
        return pl.BlockSpec((1, 2 * KV_W, page), lambda bi, ji, pt: (pt[bi, ji * npg + i] + pool_off, row_blk, 0))
    return [spec(i) for i in range(npg)]


def _cmp_kv_paged(cache_t, page_table, pool_off, wcat, bias, kg0):
    b, n_pages = page_table.shape
    page = cache_t.shape[2]
    npg = math.gcd(n_pages, PAGES_PER_STEP)
    nh_step = npg * page // CMP_STRIDE
    n_half = n_pages * page // CMP_STRIDE
    kg = jnp.tile(kg0.astype(F32).reshape(1, HEAD_DIM), (1, NSA_KV_HEADS))
    if npg == n_pages:
        cst = lambda bi, ji, pt: (0, 0)
        return pl.pallas_call(
            functools.partial(_cmp_paged_kernel, npg=npg, page=page, whole=True),
            grid_spec=pltpu.PrefetchScalarGridSpec(
                num_scalar_prefetch=1, grid=(b, 1),
                in_specs=_page_specs(npg, page, 0, pool_off)
                + [pl.BlockSpec((2, CMP_STRIDE, KV_W, 2 * KV_W), lambda bi, ji, pt: (0, 0, 0, 0)),
                   pl.BlockSpec((2, KV_W), cst), pl.BlockSpec((1, KV_W), cst), pl.BlockSpec((KV_W, KV_W), cst)],
                out_specs=[pl.BlockSpec((1, n_half, KV_W), lambda bi, ji, pt: (bi, 0, 0)),
                           pl.BlockSpec((1, n_half, KV_W), lambda bi, ji, pt: (bi, 0, 0))],
                scratch_shapes=[pltpu.VMEM((npg * page, KV_W), F32), pltpu.VMEM((npg * page, KV_W), F32)]),
            out_shape=[jax.ShapeDtypeStruct((b, n_half, KV_W), BF16),
                       jax.ShapeDtypeStruct((b, n_half, KV_W), BF16)],
            compiler_params=_cparams("parallel", "arbitrary"),
            name="cmp_paged",
        )(page_table, *([cache_t] * npg), wcat, bias, kg, _group_mean_matrix(KV_W))
    kern = functools.partial(_cmp_paged_kernel, npg=npg, page=page, whole=False)
    acc_k, acc_v = pl.pallas_call(
        kern,
        grid_spec=pltpu.PrefetchScalarGridSpec(
            num_scalar_prefetch=1, grid=(b, n_pages // npg),
            in_specs=_page_specs(npg, page, 0, pool_off)
            + [pl.BlockSpec((2, CMP_STRIDE, KV_W, 2 * KV_W), lambda bi, ji, pt: (0, 0, 0, 0))],
            out_specs=[pl.BlockSpec((1, nh_step, 2 * KV_W), lambda bi, ji, pt: (bi, ji, 0)),
                       pl.BlockSpec((1, nh_step, 2 * KV_W), lambda bi, ji, pt: (bi, ji, 0))],
            scratch_shapes=[pltpu.VMEM((npg * page, KV_W), F32), pltpu.VMEM((npg * page, KV_W), F32)]),
        out_shape=[jax.ShapeDtypeStruct((b, n_half, 2 * KV_W), F32),
                   jax.ShapeDtypeStruct((b, n_half, 2 * KV_W), F32)],
        compiler_params=_cparams("parallel", "arbitrary"),
        name="cmp_paged",
    )(page_table, *([cache_t] * npg), wcat)
    blk = lambda bi: (bi, 0, 0)
    return pl.pallas_call(
        functools.partial(_cmp_fin_kernel, n_half=n_half),
        grid=(b,),
        in_specs=[pl.BlockSpec((1, n_half, 2 * KV_W), blk),
                  pl.BlockSpec((1, n_half, 2 * KV_W), blk),
                  pl.BlockSpec((2, KV_W), lambda bi: (0, 0)),
                  pl.BlockSpec((1, KV_W), lambda bi: (0, 0)),
                  pl.BlockSpec((KV_W, KV_W), lambda bi: (0, 0))],
        out_specs=[pl.BlockSpec((1, n_half, KV_W), blk), pl.BlockSpec((1, n_half, KV_W), blk)],
        out_shape=[jax.ShapeDtypeStruct((b, n_half, KV_W), BF16),
                   jax.ShapeDtypeStruct((b, n_half, KV_W), BF16)],
        compiler_params=_cparams("parallel"),
        name="cmp_fin",
    )(acc_k, acc_v, bias, kg, _group_mean_matrix(KV_W))


def _cmp_weights(cmp_pos, cmp_w):
    wf = cmp_w.astype(F32)
    eye_g = jnp.eye(NSA_KV_HEADS, dtype=F32)

    def bd(w):
        return jnp.einsum('ksde,gh->ksgdhe', w, eye_g).reshape(2, CMP_STRIDE, KV_W, KV_W)

    wcat = jnp.concatenate([bd(wf[:, :CMP_STRIDE]), bd(wf[:, CMP_STRIDE:])], axis=3).astype(BF16)
    bias = jnp.einsum('ksd,ksde->ke', cmp_pos.astype(F32), wf)
    return wcat, jnp.tile(bias, (1, NSA_KV_HEADS))


def _cmp_kv(x3, k_blk, v_blk, t_use, wcat, bias, kg0):
    b = x3.shape[0]
    n_half = t_use // CMP_STRIDE
    kern = functools.partial(_cmp_kv_kernel, n_half=n_half)
    kg = jnp.tile(kg0.astype(F32).reshape(1, HEAD_DIM), (1, NSA_KV_HEADS))
    return pl.pallas_call(
        kern,
        grid=(b,),
        in_specs=[pl.BlockSpec((1, t_use, KV_W), lambda bi: (bi, 0, k_blk)),
                  pl.BlockSpec((1, t_use, KV_W), lambda bi: (bi, 0, v_blk)),
                  pl.BlockSpec((2, CMP_STRIDE, KV_W, 2 * KV_W), lambda bi: (0, 0, 0, 0)),
                  pl.BlockSpec((2, KV_W), lambda bi: (0, 0)),
                  pl.BlockSpec((1, KV_W), lambda bi: (0, 0)),
                  pl.BlockSpec((KV_W, KV_W), lambda bi: (0, 0))],
        out_specs=[pl.BlockSpec((1, n_half, KV_W), lambda bi: (bi, 0, 0)),
                   pl.BlockSpec((1, n_half, KV_W), lambda bi: (bi, 0, 0))],
        out_shape=[jax.ShapeDtypeStruct((b, n_half, KV_W), BF16),
                   jax.ShapeDtypeStruct((b, n_half, KV_W), BF16)],
        compiler_params=_cparams("parallel"),
        name="cmp_kv",
    )(x3, x3, wcat, bias, kg, _group_mean_matrix(KV_W))


def _cmp_attn_kernel(qn_ref, kc_ref, vc_ref, ov_ref, o_ref, sel_ref, *, bb, tq, n_half, n_cmp, n_sel, nbp, pos0,
                     bias_out):
    biases = []
    qi = pl.program_id(1)
    nq = bb * tq
    pos_c = pos0 + qi * tq + lax.broadcasted_iota(jnp.int32, (tq, 1), 0)
    ncol = lax.broadcasted_iota(jnp.int32, (1, n_half), 1)
    cmask = (ncol * CMP_STRIDE + (CMP_BLOCK - 1) <= pos_c) & (ncol < n_cmp)
    pos_r = pos0 + qi * tq + lax.broadcasted_iota(jnp.int32, (1, nq), 1) % tq
    blk = lax.broadcasted_iota(jnp.int32, (nbp, 1), 0)
    cur = pos_r // SEL_BLOCK
    forced = (blk == 0) | (blk == cur) | (blk == cur - 1)
    valid = blk * SEL_BLOCK <= pos_r
    real = blk < n_sel
    for g in range(NSA_KV_HEADS):
        gs = slice(g * HEAD_DIM, (g + 1) * HEAD_DIM)
        psums = []
        for i in range(bb):
            kc = kc_ref[i, :, gs]
            vc = vc_ref[i, :, gs]
            psum = jnp.zeros((tq, n_half), F32)
            for h in range(NSA_HPG):
                hs = slice((g * NSA_HPG + h) * HEAD_DIM, (g * NSA_HPG + h + 1) * HEAD_DIM)
                s = _dot_nt(qn_ref[i, :, hs], kc)
                m = jnp.max(jnp.where(cmask, s, NEG_BIG), axis=-1, keepdims=True)
                m = jnp.where(m > 0.5 * NEG_BIG, m, 0.0)
                e = jnp.where(cmask, jnp.exp(s - m), 0.0)
                p = e / jnp.maximum(jnp.sum(e, axis=-1, keepdims=True), 1e-30)
                o_ref[i, :, hs] = _dot(p, vc)
                psum = psum + p
            psums.append(psum)
        psum = psums[0] if bb == 1 else jnp.concatenate(psums, axis=0)
        p_hi = psum.astype(BF16)
        p_lo = (psum - p_hi.astype(F32)).astype(BF16)
        ov = ov_ref[...]
        imp = _dot_nt(ov, p_hi) + _dot_nt(ov, p_lo)
        score = jnp.where(forced, 3e38, jnp.where(valid, imp, -1e38))
        score = jnp.where(real, score, -3e38)
        cnt = jnp.zeros((nbp, nq), F32)
        for jp in range(n_sel):
            rowv = score[jp:jp + 1, :]
            beats = (rowv > score) | ((rowv == score) & (blk > jp))
            cnt = cnt + jnp.where(beats, 1.0, 0.0)
        sel = (cnt < float(min(N_SELECT, n_sel))) & real
        if bias_out:
            sbg = jnp.where(sel, 0.0, NEG_BIG)
            if nbp < HEAD_DIM:
                sbg = jnp.concatenate([sbg, jnp.full((HEAD_DIM - nbp, nq), NEG_BIG, F32)], axis=0)
            biases.append(sbg)
        else:
            sel_ref[0, g * nbp:(g + 1) * nbp, :] = jnp.where(sel, 1.0, 0.0).astype(BF16)
    if bias_out:
        sel_ref[0] = jnp.concatenate(biases[::-1], axis=0).T.astype(BF16)


def _overlap_t(n_cmp, n_sel, n_half, nbp):
    a = SEL_BLOCK // CMP_STRIDE
    bb = CMP_BLOCK // CMP_STRIDE
    i = np.arange(n_half)[None, :]
    j = np.arange(nbp)[:, None]
    s = i - a * j + (bb - 1)
    cnt = np.maximum(np.minimum(np.minimum(s + 1, a + bb - 1 - s), min(a, bb)), 0)
    cnt = np.where((i < n_cmp) & (j < n_sel), cnt, 0)
    return jnp.asarray(cnt, dtype=BF16)


CMP_TQ = 256


def _cmp_attn(qn3, kc, vc, pos0, t_len, bias_out=False):
    b, l, _ = qn3.shape
    n_half = kc.shape[1]
    n_cmp = n_half - 1
    n_sel = -(-t_len // SEL_BLOCK)
    nbp = _round_up(n_sel, 16)
    tq = min(l, CMP_TQ)
    bb = math.gcd(b, max(LANES // tq, 1))
    kern = functools.partial(_cmp_attn_kernel, bb=bb, tq=tq, n_half=n_half, n_cmp=n_cmp, n_sel=n_sel, nbp=nbp,
                             pos0=pos0, bias_out=bias_out)
    if bias_out:
        assert bb == 1 and NSA_KV_HEADS == 2 and nbp <= HEAD_DIM
        sel_spec = pl.BlockSpec((1, tq, LANES), lambda bi, qi: (bi, qi, 0))
        sel_shape = jax.ShapeDtypeStruct((b, l, LANES), BF16)
    else:
        sel_spec = pl.BlockSpec((1, NSA_KV_HEADS * nbp, bb * tq), lambda bi, qi: (bi, 0, qi))
        sel_shape = jax.ShapeDtypeStruct((b // bb, NSA_KV_HEADS * nbp, bb * l), BF16)
    o_cmp, sel_t = pl.pallas_call(
        kern,
        grid=(b // bb, l // tq),
        in_specs=[pl.BlockSpec((bb, tq, NSA_WIDTH), lambda bi, qi: (bi, qi, 0)),
                  pl.BlockSpec((bb, n_half, KV_W), lambda bi, qi: (bi, 0, 0)),
                  pl.BlockSpec((bb, n_half, KV_W), lambda bi, qi: (bi, 0, 0)),
                  pl.BlockSpec((nbp, n_half), lambda bi, qi: (0, 0))],
        out_specs=[pl.BlockSpec((bb, tq, NSA_WIDTH), lambda bi, qi: (bi, qi, 0)), sel_spec],
        out_shape=[jax.ShapeDtypeStruct((b, l, NSA_WIDTH), F32), sel_shape],
        compiler_params=_cparams("parallel", "parallel"),
        name="cmp_attn",
    )(qn3, kc, vc, _overlap_t(n_cmp, n_sel, n_half, nbp))
    if bb > 1:
        sel_t = sel_t.reshape(b // bb, NSA_KV_HEADS * nbp, bb, l)
        sel_t = jnp.swapaxes(sel_t, 1, 2).reshape(b, NSA_KV_HEADS * nbp, l)
    return o_cmp, sel_t


SEL_TQ = 128
SEL_TK = 2048
WIN_TQ = 256


def _group_lanes(shape, g):
    lane = lax.broadcasted_iota(jnp.int32, shape, len(shape) - 1)
    return (lane < HEAD_DIM) if g == 0 else (lane >= HEAD_DIM)


def _store_heads(o_ref, r, g, tq):
    lo = g * HEAD_DIM
    for h in range(NSA_HPG):
        hh = g * NSA_HPG + h
        o_ref[0, :, hh * HEAD_DIM:(hh + 1) * HEAD_DIM] = r[h * tq:(h + 1) * tq, lo:lo + HEAD_DIM]


def _flash_sel_kernel(qi_ref, ki_ref, q_ref, k_ref, v_ref, sb_ref, o_ref, m_sc, acc_sc, *, tq, tk):
    qi = qi_ref[pl.program_id(1)]
    ki = ki_ref[pl.program_id(1)]
    kmax = (qi * tq + tq - 1) // tk
    hq = NSA_HPG * tq

    @pl.when(ki == 0)
    def _init():
        m_sc[...] = jnp.full(m_sc.shape, NEG_BIG, F32)
        acc_sc[...] = jnp.zeros(acc_sc.shape, F32)

    def step(diagonal):
        key_blk = ki * (tk // SEL_BLOCK) + lax.broadcasted_iota(jnp.int32, (tk, LANES), 0) // SEL_BLOCK
        one_hot = jnp.where(lax.broadcasted_iota(jnp.int32, (tk, LANES), 1) % HEAD_DIM == key_blk, 1.0, 0.0).astype(BF16)
        kblk = k_ref[0].astype(BF16)
        vblk = v_ref[0].astype(BF16)
        sb4 = jnp.concatenate([sb_ref[0]] * NSA_HPG, axis=0)
        if diagonal:
            qpos = qi * tq + lax.broadcasted_iota(jnp.int32, (tq, tk), 0)
            kpos = ki * tk + lax.broadcasted_iota(jnp.int32, (tq, tk), 1)
            causal4 = jnp.concatenate([jnp.where(kpos <= qpos, 0.0, NEG_BIG)] * NSA_HPG, axis=0)
        m_prev = [m_sc[g] for g in range(NSA_KV_HEADS)]
        acc_prev = [acc_sc[g] for g in range(NSA_KV_HEADS)]
        m_out, acc_out = [], []
        for g in range(NSA_KV_HEADS):
            own_k = _group_lanes((tk, LANES), g)
            q4 = q_ref[g * NSA_HPG:(g + 1) * NSA_HPG].reshape(hq, LANES)
            qx = jnp.where(_group_lanes((hq, LANES), g), q4, sb4)
            kx = jnp.where(own_k, kblk, one_hot)
            s = lax.dot_general(qx, kx, (((1,), (1,)), ((), ())), preferred_element_type=F32)
            if diagonal:
                s = s + causal4
            m_new = jnp.maximum(m_prev[g], jnp.max(s, axis=-1, keepdims=True))
            p = jnp.exp2(s - m_new[:, :1]).astype(BF16)
            vx = jnp.where(own_k, vblk, 1.0)
            acc_out.append(jnp.exp2(m_prev[g] - m_new) * acc_prev[g] + jnp.dot(p, vx, preferred_element_type=F32))
            m_out.append(m_new)
        for g in range(NSA_KV_HEADS):
            m_sc[g] = m_out[g]
            acc_sc[g] = acc_out[g]

    @pl.when(ki < kmax)
    def _full():
        step(False)

    @pl.when(ki == kmax)
    def _last():
        step(True)
        for g in range(NSA_KV_HEADS):
            acc = acc_sc[g]
            _store_heads(o_ref, acc / pltpu.roll(acc, HEAD_DIM, 1), g, tq)


def _flash_sel(q8, rows3, selb):
    b, l, _ = rows3.shape
    tq, tk = min(SEL_TQ, l), min(SEL_TK, l)
    nq = l // tq
    pairs = [(qi, ki) for qi in range(nq) for ki in range((qi * tq + tq - 1) // tk + 1)]
    qi_tab = jnp.asarray([p[0] for p in pairs], jnp.int32)
    ki_tab = jnp.asarray([p[1] for p in pairs], jnp.int32)
    kv_idx = lambda blk: (lambda bi, si, qt, kt: (bi, kt[si], blk))
    return pl.pallas_call(
        functools.partial(_flash_sel_kernel, tq=tq, tk=tk),
        grid_spec=pltpu.PrefetchScalarGridSpec(
            num_scalar_prefetch=2, grid=(b, len(pairs)),
            in_specs=[pl.BlockSpec((NSA_HEADS, tq, LANES), lambda bi, si, qt, kt: (0, bi * nq + qt[si], 0)),
                      pl.BlockSpec((1, tk, KV_W), kv_idx(2)),
                      pl.BlockSpec((1, tk, KV_W), kv_idx(3)),
                      pl.BlockSpec((1, tq, LANES), lambda bi, si, qt, kt: (bi, qt[si], 0))],
            out_specs=pl.BlockSpec((1, tq, NSA_WIDTH), lambda bi, si, qt, kt: (bi, qt[si], 0)),
            scratch_shapes=[pltpu.VMEM((NSA_KV_HEADS, NSA_HPG * tq, LANES), F32),
                            pltpu.VMEM((NSA_KV_HEADS, NSA_HPG * tq, LANES), F32)]),
        out_shape=jax.ShapeDtypeStruct((b, l, NSA_WIDTH), F32),
        compiler_params=_cparams("parallel", "arbitrary"),
        name="flash_sel",
    )(qi_tab, ki_tab, q8, rows3, rows3, selb)


def _flash_win_kernel(q_ref, *refs, tq, back):
    nspan = back + 1
    k_refs, v_refs, o_ref = refs[:nspan], refs[nspan:2 * nspan], refs[2 * nspan]
    qi = pl.program_id(1)
    hq = NSA_HPG * tq
    span = nspan * tq
    qpos = qi * tq + lax.broadcasted_iota(jnp.int32, (tq, span), 0)
    kpos = (qi - back) * tq + lax.broadcasted_iota(jnp.int32, (tq, span), 1)
    ok = (kpos <= qpos) & (kpos > qpos - WINDOW) & (kpos >= 0)
    bias4 = jnp.concatenate([jnp.where(ok, 0.0, NEG_BIG)] * NSA_HPG, axis=0)
    kcat = jnp.concatenate([r[0] for r in k_refs], axis=0).astype(BF16)
    vcat = jnp.concatenate([r[0] for r in v_refs], axis=0).astype(BF16)
    for g in range(NSA_KV_HEADS):
        q4 = q_ref[g * NSA_HPG:(g + 1) * NSA_HPG].reshape(hq, LANES)
        qx = jnp.where(_group_lanes((hq, LANES), g), q4, 0.0)
        s = lax.dot_general(qx, kcat, (((1,), (1,)), ((), ())), preferred_element_type=F32) + bias4
        p = jnp.exp2(s - jnp.max(s, axis=-1, keepdims=True)).astype(BF16)
        vx = jnp.where(_group_lanes((span, LANES), g), vcat, 1.0)
        r = jnp.dot(p, vx, preferred_element_type=F32)
        _store_heads(o_ref, r / pltpu.roll(r, HEAD_DIM, 1), g, tq)


def _flash_win(q8, win3):
    b, l, _ = win3.shape
    tq = min(WIN_TQ, l)
    nq = l // tq
    back = -(-(WINDOW - 1) // tq)

    def kv_specs(blk):
        return [pl.BlockSpec((1, tq, KV_W), (lambda bi, qi, j=j: (bi, jnp.maximum(qi - back + j, 0), blk)))
                for j in range(back + 1)]

    return pl.pallas_call(
        functools.partial(_flash_win_kernel, tq=tq, back=back),
        grid=(b, nq),
        in_specs=[pl.BlockSpec((NSA_HEADS, tq, LANES), lambda bi, qi: (0, bi * nq + qi, 0))]
        + kv_specs(0) + kv_specs(1),
        out_specs=pl.BlockSpec((1, tq, NSA_WIDTH), lambda bi, qi: (bi, qi, 0)),
        out_shape=jax.ShapeDtypeStruct((b, l, NSA_WIDTH), F32),
        compiler_params=_cparams("parallel", "parallel"),
        name="flash_win",
    )(q8, *([win3] * (2 * (back + 1))))


def _softmax_update(s, m_prev, l_prev):
    m_new = jnp.maximum(m_prev, jnp.max(s, axis=-1, keepdims=True))
    alpha = jnp.exp2(m_prev - m_new)
    p = jnp.exp2(s - m_new[:, :1])
    return m_new, alpha, alpha * l_prev + jnp.sum(p, axis=-1, keepdims=True), p


def _paged_attn_kernel(pt_ref, *refs, npg, page, l_new, nbp):
    page_refs = refs[:npg]
    (qz_ref, sel_ref, kn_ref, vn_ref, win_ref, kwn_ref, vwn_ref,
     oslc_ref, owin_ref, wout_ref, m_sc, l_sc, acc_sc) = refs[npg:]
    ji = pl.program_id(1)
    nr = qz_ref.shape[1]
    span = npg * page

    @pl.when(ji == 0)
    def _init():
        m_sc[...] = jnp.full(m_sc.shape, NEG_BIG, F32)
        l_sc[...] = jnp.zeros(l_sc.shape, F32)
        acc_sc[...] = jnp.zeros(acc_sc.shape, F32)

    qz = qz_ref[0]
    kt = jnp.concatenate([page_refs[i][0, 0:KV_W, :] for i in range(npg)], axis=1).astype(BF16)
    vt = jnp.concatenate([page_refs[i][0, KV_W:2 * KV_W, :] for i in range(npg)], axis=1).astype(BF16)
    s = jnp.dot(qz, kt, preferred_element_type=F32)
    e_row = lax.broadcasted_iota(jnp.int32, (nbp, span), 0)
    e_col = lax.broadcasted_iota(jnp.int32, (nbp, span), 1)
    expand = jnp.where(e_row == ji * (span // SEL_BLOCK) + e_col // SEL_BLOCK, 1.0, 0.0).astype(BF16)
    picked = jnp.dot(sel_ref[0], expand, preferred_element_type=F32) > 0.5
    m_new, alpha, l_new_v, p = _softmax_update(jnp.where(picked, s, NEG_BIG), m_sc[...], l_sc[...])
    m_sc[...] = m_new
    l_sc[...] = l_new_v
    acc_sc[...] = alpha * acc_sc[...] + _dot_nt(p, vt)

    @pl.when(ji == pl.num_programs(1) - 1)
    def _fin():
        q_of_row = lax.broadcasted_iota(jnp.int32, (nr, l_new), 0) % l_new
        j_new = lax.broadcasted_iota(jnp.int32, (nr, l_new), 1)
        new_ok = j_new <= q_of_row
        sn = jnp.where(new_ok, _dot_nt(qz, kn_ref[0]), NEG_BIG)
        m2, a2, l2, p2 = _softmax_update(sn, m_sc[...], l_sc[...])
        oslc_ref[0] = (a2 * acc_sc[...] + _dot(p2, vn_ref[0])) / l2
        wlen = win_ref.shape[2]
        kw = win_ref[0, 0:KV_W, :]
        vw = win_ref[0, KV_W:2 * KV_W, :]
        i_old = lax.broadcasted_iota(jnp.int32, (nr, wlen), 1)
        q_old = lax.broadcasted_iota(jnp.int32, (nr, wlen), 0) % l_new
        sw = jnp.where(i_old + (WINDOW - wlen) > q_old, _dot(qz, kw), NEG_BIG)
        swn = jnp.where(new_ok, _dot_nt(qz, kwn_ref[0]), NEG_BIG)
        mw = jnp.maximum(jnp.max(sw, axis=-1, keepdims=True), jnp.max(swn, axis=-1, keepdims=True))
        pw = jnp.exp2(sw - mw)
        pwn = jnp.exp2(swn - mw)
        lw = jnp.sum(pw, axis=-1, keepdims=True) + jnp.sum(pwn, axis=-1, keepdims=True)
        owin_ref[0] = (_dot_nt(pw, vw) + _dot(pwn, vwn_ref[0])) / lw
        new_t = jnp.concatenate([kwn_ref[0], vwn_ref[0]], axis=1).T
        place = (lax.broadcasted_iota(jnp.int32, (l_new, wlen), 1)
                 == lax.broadcasted_iota(jnp.int32, (l_new, wlen), 0) + (wlen - l_new)).astype(F32)
        lane = lax.broadcasted_iota(jnp.int32, (2 * KV_W, wlen), 1)
        wout_ref[0] = jnp.where(lane < wlen - l_new, pltpu.roll(win_ref[0], wlen - l_new, 1), _dot_f32(new_t, place))


def _paged_attn(cache_t, page_table, pool_off, qz, sel_rows, rows3, win_t, win_off, win3):
    b, n_pages = page_table.shape
    page = cache_t.shape[2]
    npg = math.gcd(n_pages, PAGES_PER_STEP)
    nr = qz.shape[1]
    l_new = rows3.shape[1]
    nbp = sel_rows.shape[2]
    wlen = win_t.shape[2]
    kern = functools.partial(_paged_attn_kernel, npg=npg, page=page, l_new=l_new, nbp=nbp)
    per_b = lambda bi, ji, pt: (bi, 0, 0)
    return pl.pallas_call(
        kern,
        grid_spec=pltpu.PrefetchScalarGridSpec(
            num_scalar_prefetch=1, grid=(b, n_pages // npg),
            in_specs=_page_specs(npg, page, 1, pool_off)
            + [pl.BlockSpec((1, nr, KV_W), per_b),
               pl.BlockSpec((1, nr, nbp), per_b),
               pl.BlockSpec((1, l_new, KV_W), lambda bi, ji, pt: (bi, 0, 2)),
               pl.BlockSpec((1, l_new, KV_W), lambda bi, ji, pt: (bi, 0, 3)),
               pl.BlockSpec((1, 2 * KV_W, wlen), lambda bi, ji, pt: (bi + win_off, 0, 0)),
               pl.BlockSpec((1, l_new, KV_W), lambda bi, ji, pt: (bi, 0, 0)),
               pl.BlockSpec((1, l_new, KV_W), lambda bi, ji, pt: (bi, 0, 1))],
            out_specs=[pl.BlockSpec((1, nr, KV_W), per_b),
                       pl.BlockSpec((1, nr, KV_W), per_b),
                       pl.BlockSpec((1, 2 * KV_W, wlen), per_b)],
            scratch_shapes=[pltpu.VMEM((nr, LANES), F32), pltpu.VMEM((nr, LANES), F32),
                            pltpu.VMEM((nr, KV_W), F32)]),
        out_shape=[jax.ShapeDtypeStruct((b, nr, KV_W), F32),
                   jax.ShapeDtypeStruct((b, nr, KV_W), F32),
                   jax.ShapeDtypeStruct((b, 2 * KV_W, wlen), F32)],
        compiler_params=_cparams("parallel", "arbitrary"),
        name="paged_attn",
    )(page_table, *([cache_t] * npg), qz, sel_rows, rows3, rows3, win_t, win3, win3)


def _nsa_fresh(u_nsa, b, l, win_keep, q_g, k_g, cmp_pos, cmp_w):
    qn, qr, rows, win = _nsa_prep(u_nsa, jnp.arange(l), q_g, k_g, l)
    rows3 = rows.reshape(b, l, N_KV_SLOTS * KV_W)
    win3 = win.reshape(b, l, 2 * KV_W)
    wcat, bias = _cmp_weights(cmp_pos, cmp_w)
    t_use = (l // CMP_STRIDE) * CMP_STRIDE
    kc, vc = _cmp_kv(rows3, 0, 1, t_use, wcat, bias, k_g[0])
    o_cmp, selb = _cmp_attn(qn.reshape(b, l, NSA_WIDTH), kc, vc, 0, l, bias_out=True)
    o_slc = _flash_sel(qr, rows3, selb)
    o_win = _flash_win(qr, win3)
    branches = tuple(o.reshape(b * l, NSA_WIDTH) for o in (o_cmp, o_slc, o_win))
    new_rows = rows3.reshape(b, l, N_KV_SLOTS, NSA_KV_HEADS, HEAD_DIM)
    new_win = win3[:, l - win_keep:].reshape(b, win_keep, 2, NSA_KV_HEADS, HEAD_DIM)
    return branches, new_rows, new_win


def _nsa_paged(u_nsa, b, l, paged, q_g, k_g, cmp_pos, cmp_w):
    cache_t, page_table, pool_off, win_t, win_off = paged
    past_len = page_table.shape[1] * cache_t.shape[2]
    assert (past_len + l) // CMP_STRIDE == past_len // CMP_STRIDE and past_len % SEL_BLOCK == 0
    qn, qr, rows, win = _nsa_prep(u_nsa, past_len + jnp.arange(l), q_g, k_g, l)
    rows3 = rows.reshape(b, l, N_KV_SLOTS * KV_W)
    win3 = win.reshape(b, l, 2 * KV_W)
    wcat, bias = _cmp_weights(cmp_pos, cmp_w)
    kc, vc = _cmp_kv_paged(cache_t, page_table, pool_off, wcat, bias, k_g[0])
    o_cmp, sel_t = _cmp_attn(qn.reshape(b, l, NSA_WIDTH), kc, vc, past_len, past_len + l)
    nbp = sel_t.shape[1] // NSA_KV_HEADS
    sel_rows = jnp.swapaxes(sel_t.reshape(b, NSA_KV_HEADS, 1, nbp, l), 3, 4)
    sel_rows = jnp.broadcast_to(sel_rows, (b, NSA_KV_HEADS, NSA_HPG, l, nbp)).reshape(b, NSA_HEADS * l, nbp)
    q5 = jnp.transpose(qr[:, :, :HEAD_DIM].reshape(NSA_KV_HEADS, NSA_HPG, b, l, HEAD_DIM), (2, 0, 1, 3, 4))
    qz = jnp.einsum('bghqd,gk->bghqkd', q5, jnp.eye(NSA_KV_HEADS, dtype=q5.dtype)).reshape(b, NSA_HEADS * l, KV_W)
    o_slc_z, o_win_z, wout = _paged_attn(cache_t, page_table, pool_off, qz, sel_rows, rows3, win_t, win_off, win3)

    def own_group(o):
        o6 = o.reshape(b, NSA_KV_HEADS, NSA_HPG, l, NSA_KV_HEADS, HEAD_DIM)
        d = jnp.stack([o6[:, g, :, :, g, :] for g in range(NSA_KV_HEADS)], axis=1)
        return jnp.transpose(d, (0, 3, 1, 2, 4)).reshape(b * l, NSA_WIDTH)

    branches = (o_cmp.reshape(b * l, NSA_WIDTH), own_group(o_slc_z), own_group(o_win_z))
    new_rows = rows3.reshape(b, l, N_KV_SLOTS, NSA_KV_HEADS, HEAD_DIM)
    wlen = wout.shape[2]
    new_win = jnp.transpose(wout.reshape(b, 2, NSA_KV_HEADS, HEAD_DIM, wlen), (0, 4, 1, 2, 3))
    return branches, new_rows, new_win


PROJ_ROWS = 512


def _hybrid_layer(x, paged, win_keep, s_gla, c_ml, n_ml, m_ml, conv_ml,
                  norm_g, w_in_pad, w_out_bf, gla_w_gate, gla_b_gate, gla_norm_g,
                  nsa_q_norm_g, nsa_k_norm_g, nsa_cmp_pos, nsa_cmp_w,
                  ml_conv_w, ml_conv_b, ml_gate_b, ml_norm_g):
    b, l, _ = x.shape
    rows = b * l
    tm = math.gcd(rows, PROJ_ROWS)
    x2d = x.reshape(rows, D_MODEL)
    u_gla, u_nsa, u_ml = _proj_in(x2d, norm_g, w_in_pad, tm)

    nk = GLA_HEADS * GLA_DK
    wg = jnp.pad(gla_w_gate.astype(F32), ((0, LANES - GLA_RANK), (0, 0))).astype(BF16)
    o_a, st_new = _gla(u_gla, _gla_state_in(s_gla), wg, gla_b_gate.astype(F32).reshape(1, nk),
                       jnp.tile(gla_norm_g.astype(F32).reshape(1, GLA_DV), (1, GLA_HEADS)), b, l)
    s_new = _gla_state_out(st_new)

    if paged is None:
        o_nsa, new_rows, new_win = _nsa_fresh(u_nsa, b, l, win_keep, nsa_q_norm_g, nsa_k_norm_g,
                                              nsa_cmp_pos, nsa_cmp_w)
    else:
        o_nsa, new_rows, new_win = _nsa_paged(u_nsa, b, l, paged, nsa_q_norm_g, nsa_k_norm_g,
                                              nsa_cmp_pos, nsa_cmp_w)

    o_c, c_new, n_new, m_new, conv_new = _mlstm(u_ml, c_ml, n_ml, m_ml, conv_ml, ml_conv_w, ml_conv_b,
                                                ml_gate_b, ml_norm_g, b, l)

    y = _proj_out(o_a.reshape(rows, GLA_WIDTH), *o_nsa, u_nsa, o_c.reshape(rows, ML_WIDTH), x2d, w_out_bf, tm)
    return y.reshape(b, l, D_MODEL), new_rows, new_win, s_new, c_new, n_new, m_new, conv_new


def kernel(x_prompt, x_sample, cache_nsa_kv, state_nsa_win, state_gla, state_mlstm_C, state_mlstm_n,
           state_mlstm_m, state_mlstm_conv, page_table, norm_g, w_in, w_out, gla_w_gate, gla_b_gate,
           gla_norm_g, nsa_q_norm_g, nsa_k_norm_g, nsa_cmp_pos, nsa_cmp_w, ml_conv_w, ml_conv_b,
           ml_gate_b, ml_norm_g):
    bp, sp, _ = x_prompt.shape
    bs, _, _ = x_sample.shape
    depth = w_in.shape[0]
    dt = x_prompt.dtype
    zero_gla = jnp.zeros((bp, GLA_HEADS, GLA_DK, GLA_DV), F32)
    zero_c = jnp.zeros((bp, ML_HEADS, ML_DH, ML_DH), F32)
    zero_n = jnp.zeros((bp, ML_HEADS, ML_DH), F32)
    zero_m = jnp.zeros((bp, ML_HEADS), F32)
    zero_conv = jnp.zeros((bp, CONV_W - 1, 2 * ML_WIDTH), dt)
    keep_p = min(WINDOW, sp)
    keep_s = state_nsa_win.shape[2]
    n_pool = cache_nsa_kv.shape[1]
    cache_t = jnp.transpose(cache_nsa_kv, (0, 1, 3, 4, 5, 2)).reshape(
        depth * n_pool, N_KV_SLOTS * KV_W, cache_nsa_kv.shape[2]).astype(F32)
    win_t = jnp.transpose(state_nsa_win, (0, 1, 3, 4, 5, 2)).reshape(depth * bs, 2 * KV_W, keep_s).astype(F32)

    y_prompt, y_sample = x_prompt, x_sample
    p_layers, s_layers = [], []
    for layer in range(depth):
        w = (norm_g[layer], _pad_w_in(w_in[layer]), w_out[layer].astype(BF16), gla_w_gate[layer],
             gla_b_gate[layer], gla_norm_g[layer],
             nsa_q_norm_g[layer], nsa_k_norm_g[layer], nsa_cmp_pos[layer], nsa_cmp_w[layer],
             ml_conv_w[layer], ml_conv_b[layer], ml_gate_b[layer], ml_norm_g[layer])
        y_prompt, *p_new = _hybrid_layer(y_prompt, None, keep_p, zero_gla, zero_c, zero_n,
                                         zero_m, zero_conv, *w)
        paged = (cache_t, page_table, layer * n_pool, win_t, layer * bs)
        y_sample, *s_new = _hybrid_layer(y_sample, paged, keep_s,
                                         state_gla[layer], state_mlstm_C[layer], state_mlstm_n[layer],
                                         state_mlstm_m[layer], state_mlstm_conv[layer], *w)
        p_layers.append(p_new)
        s_layers.append(s_new)
    p_kv, p_win, p_gla, p_c, p_n, p_m, p_conv = [jnp.stack(z) for z in zip(*p_layers)]
    s_kv, s_win, s_gla, s_c, s_n, s_m, s_conv = [jnp.stack(z) for z in zip(*s_layers)]
    return (y_prompt, y_sample, p_kv, s_kv, p_win, s_win, p_gla, s_gla, p_c, s_c, p_n, s_n, p_m, s_m, p_conv, s_conv)
```

```python
import functools
import math

import jax
import jax.numpy as jnp
import numpy as np
from jax import lax
from jax.experimental import pallas as pl
from jax.experimental.pallas import tpu as pltpu

F32 = jnp.float32
BF16 = jnp.bfloat16
HIGHEST = lax.Precision.HIGHEST

D_MODEL = 1024
HEAD_DIM = 64
GLA_WIDTH = D_MODEL // 4
NSA_WIDTH = D_MODEL // 2
ML_WIDTH = D_MODEL - GLA_WIDTH - NSA_WIDTH
D_MIX = GLA_WIDTH + NSA_WIDTH + ML_WIDTH

GLA_HEADS = GLA_WIDTH // HEAD_DIM
GLA_DK = HEAD_DIM // 2
GLA_DV = HEAD_DIM
GLA_RANK = 16
GLA_TAU = 16.0
GLA_CHUNK = 64

NSA_HEADS = NSA_WIDTH // HEAD_DIM
NSA_KV_HEADS = 2
NSA_HPG = NSA_HEADS // NSA_KV_HEADS
CMP_BLOCK = 32
CMP_STRIDE = 16
SEL_BLOCK = 64
N_SELECT = 16
WINDOW = 512
Q_BLOCK = 128
N_KV_SLOTS = 4
ROT_DIM = HEAD_DIM // 4
ROPE_THETA = 500000.0
ATTN_SCALE = HEAD_DIM ** -0.5

ML_HEADS = ML_WIDTH // HEAD_DIM
ML_DH = HEAD_DIM
ML_CHUNK = 64
CONV_W = 4

SPLIT_SIZES = (GLA_HEADS * GLA_DK, GLA_HEADS * GLA_DK, GLA_WIDTH, GLA_RANK, GLA_WIDTH,
               NSA_WIDTH, 6 * NSA_KV_HEADS * HEAD_DIM, 3 * NSA_HEADS, NSA_WIDTH,
               2 * ML_WIDTH, ML_WIDTH, 2 * ML_HEADS, ML_WIDTH, ML_WIDTH)

LANES = 128
SUBLANES = 8
VMEM_LIMIT = 56 * 1024 * 1024
NEG_BIG = -1e30
EPS = 1e-6
LOG2E = math.log2(math.e)


def _round_up(n, m):
    return -(-n // m) * m


PAD_SIZES = tuple(_round_up(s, LANES) for s in SPLIT_SIZES)
D_IN_PAD = sum(PAD_SIZES)
W_GLA = sum(PAD_SIZES[0:5])
W_NSA = sum(PAD_SIZES[5:9])
W_ML = sum(PAD_SIZES[9:14])
KV_W = NSA_KV_HEADS * HEAD_DIM


def _dot(a, b):
    return jnp.dot(a.astype(BF16), b.astype(BF16), preferred_element_type=F32)


def _dot_nt(a, b):
    return lax.dot_general(a.astype(BF16), b.astype(BF16), (((1,), (1,)), ((), ())), preferred_element_type=F32)


def _dot_tn(a, b):
    return lax.dot_general(a.astype(BF16), b.astype(BF16), (((0,), (0,)), ((), ())), preferred_element_type=F32)


def _dot_f32(a, b):
    return jnp.dot(a, b, precision=HIGHEST, preferred_element_type=F32)


def _dot_split(a, b):
    a_hi = a.astype(BF16)
    a_lo = (a - a_hi.astype(F32)).astype(BF16)
    bb = b.astype(BF16)
    return jnp.dot(a_hi, bb, preferred_element_type=F32) + jnp.dot(a_lo, bb, preferred_element_type=F32)


def _log_sigmoid(x):
    return jnp.minimum(x, 0.0) - jnp.log1p(jnp.exp(-jnp.abs(x)))


def _sigmoid(x):
    return 1.0 / (1.0 + jnp.exp(-x))


def _silu(x):
    return x * _sigmoid(x)


def _group_mean_matrix(width):
    g = np.kron(np.eye(width // HEAD_DIM, dtype=np.float32), np.full((HEAD_DIM, HEAD_DIM), 1.0 / HEAD_DIM, np.float32))
    return jnp.asarray(g, dtype=BF16)


def _group_norm(x, gmat, gain):
    ms = _dot_split(x * x, gmat)
    return x * lax.rsqrt(ms + EPS) * gain


def _cparams(*sem):
    return pltpu.CompilerParams(dimension_semantics=sem, vmem_limit_bytes=VMEM_LIMIT)


def _proj_in_kernel(x_ref, g_ref, w_ref, ug_ref, un_ref, um_ref):
    x = x_ref[...]
    y = x * lax.rsqrt(jnp.mean(x * x, axis=-1, keepdims=True) + EPS) * g_ref[...]
    r = jnp.dot(y.astype(BF16), w_ref[...], preferred_element_type=F32)
    ug_ref[...] = r[:, 0:W_GLA]
    un_ref[...] = r[:, W_GLA:W_GLA + W_NSA]
    um_ref[...] = r[:, W_GLA + W_NSA:D_IN_PAD]


def _proj_in(x2d, g, w_pad, tm):
    rows = x2d.shape[0]
    return pl.pallas_call(
        _proj_in_kernel,
        grid=(rows // tm,),
        in_specs=[pl.BlockSpec((tm, D_MODEL), lambda i: (i, 0)),
                  pl.BlockSpec((1, D_MODEL), lambda i: (0, 0)),
                  pl.BlockSpec((D_MODEL, D_IN_PAD), lambda i: (0, 0))],
        out_specs=[pl.BlockSpec((tm, W_GLA), lambda i: (i, 0)),
                   pl.BlockSpec((tm, W_NSA), lambda i: (i, 0)),
                   pl.BlockSpec((tm, W_ML), lambda i: (i, 0))],
        out_shape=[jax.ShapeDtypeStruct((rows, W_GLA), F32),
                   jax.ShapeDtypeStruct((rows, W_NSA), F32),
                   jax.ShapeDtypeStruct((rows, W_ML), F32)],
        compiler_params=_cparams("parallel"),
        name="proj_in",
    )(x2d, g.reshape(1, D_MODEL), w_pad)


def _pad_w_in(w_in):
    parts = []
    off = 0
    for s, p in zip(SPLIT_SIZES, PAD_SIZES):
        seg = w_in[:, off:off + s]
        if p != s:
            seg = jnp.pad(seg, ((0, 0), (0, p - s)))
        parts.append(seg)
        off += s
    return jnp.concatenate(parts, axis=1).astype(BF16)


def _proj_out_kernel(oa_ref, ocmp_ref, oslc_ref, owin_ref, gz_ref, oc_ref, x_ref, w_ref, e_ref, y_ref):
    gate = _sigmoid(gz_ref[:, 0:LANES])
    ob = (_dot_split(gate, e_ref[0]) * ocmp_ref[...] + _dot_split(gate, e_ref[1]) * oslc_ref[...]
          + _dot_split(gate, e_ref[2]) * owin_ref[...]) * _silu(gz_ref[:, LANES:LANES + NSA_WIDTH])
    y = x_ref[...]
    y = y + _dot(oa_ref[...], w_ref[0:GLA_WIDTH, :])
    y = y + _dot(ob, w_ref[GLA_WIDTH:GLA_WIDTH + NSA_WIDTH, :])
    y = y + _dot(oc_ref[...], w_ref[GLA_WIDTH + NSA_WIDTH:D_MIX, :])
    y_ref[...] = y


def _gate_expand():
    e = np.zeros((3, LANES, NSA_WIDTH), np.float32)
    for j in range(3):
        for h in range(NSA_HEADS):
            e[j, j * NSA_HEADS + h, h * HEAD_DIM:(h + 1) * HEAD_DIM] = 1.0
    return jnp.asarray(e, dtype=BF16)


def _proj_out(oa, o_cmp, o_slc, o_win, u_nsa, oc, x2d, w_bf, tm):
    rows = x2d.shape[0]
    gz_w = LANES + NSA_WIDTH
    gz_blk = (NSA_WIDTH + 6 * KV_W) // gz_w
    assert gz_blk * gz_w == NSA_WIDTH + 6 * KV_W
    row = lambda i: (i, 0)
    return pl.pallas_call(
        _proj_out_kernel,
        grid=(rows // tm,),
        in_specs=[pl.BlockSpec((tm, GLA_WIDTH), row),
                  pl.BlockSpec((tm, NSA_WIDTH), row),
                  pl.BlockSpec((tm, NSA_WIDTH), row),
                  pl.BlockSpec((tm, NSA_WIDTH), row),
                  pl.BlockSpec((tm, gz_w), lambda i: (i, gz_blk)),
                  pl.BlockSpec((tm, ML_WIDTH), row),
                  pl.BlockSpec((tm, D_MODEL), row),
                  pl.BlockSpec((D_MIX, D_MODEL), lambda i: (0, 0)),
                  pl.BlockSpec((3, LANES, NSA_WIDTH), lambda i: (0, 0, 0))],
        out_specs=pl.BlockSpec((tm, D_MODEL), row),
        out_shape=jax.ShapeDtypeStruct((rows, D_MODEL), F32),
        compiler_params=_cparams("parallel"),
        name="proj_out",
    )(oa, o_cmp, o_slc, o_win, u_nsa, oc, x2d, w_bf, _gate_expand())


def _gla_kernel(u_ref, st0_ref, wg_ref, bg_ref, ng_ref, gm_ref, o_ref, st_ref, *, bb, tl, c):
    @pl.when(pl.program_id(1) == 0)
    def _init():
        st_ref[...] = st0_ref[...]

    nk = GLA_HEADS * GLA_DK
    a_off = 2 * nk + GLA_WIDTH
    q, k, v, log_a = [], [], [], []
    for i in range(bb):
        q.append(u_ref[i, :, 0:nk] * (GLA_DK ** -0.5))
        k.append(u_ref[i, :, nk:2 * nk])
        v.append(u_ref[i, :, 2 * nk:2 * nk + GLA_WIDTH])
        pre = _dot(u_ref[i, :, a_off:a_off + LANES], wg_ref[...]) + bg_ref[...]
        log_a.append(_log_sigmoid(pre) * (1.0 / GLA_TAU))

    row_c = lax.broadcasted_iota(jnp.int32, (c, nk), 0)
    shifts = [1 << t for t in range(c.bit_length()) if (1 << t) < c]
    hc = GLA_HEADS * c
    tri_h = (lax.broadcasted_iota(jnp.int32, (hc, c), 0) % c) >= lax.broadcasted_iota(jnp.int32, (hc, c), 1)
    k_head = lax.broadcasted_iota(jnp.int32, (1, nk), 1) // GLA_DK
    v_head = lax.broadcasted_iota(jnp.int32, (1, GLA_WIDTH), 1) // GLA_DV
    st_diag = (lax.broadcasted_iota(jnp.int32, (GLA_WIDTH, nk), 0) // GLA_DV
               == lax.broadcasted_iota(jnp.int32, (GLA_WIDTH, nk), 1) // GLA_DK)

    st = [st_ref[i] for i in range(bb)]
    outs = [[] for _ in range(bb)]
    for j in range(tl // c):
        sl = slice(j * c, (j + 1) * c)
        for i in range(bb):
            b = log_a[i][sl]
            for sh in shifts:
                b = b + jnp.where(row_c >= sh, pltpu.roll(b, sh, 0), 0.0)
            blast = b[c - 1:c]
            qe = q[i][sl] * jnp.exp(b)
            ke = k[i][sl] * jnp.exp(-b)
            kl = k[i][sl] * jnp.exp(blast - b)
            vc = v[i][sl]
            qx = jnp.concatenate([jnp.where(k_head == h, qe, 0.0) for h in range(GLA_HEADS)], axis=0)
            a = jnp.where(tri_h, _dot_nt(qx, ke), 0.0)
            r = _dot(a, vc)
            o = _dot_nt(qe, st[i])
            for h in range(GLA_HEADS):
                o = o + jnp.where(v_head == h, r[h * c:(h + 1) * c], 0.0)
            st[i] = st[i] * jnp.exp(blast) + jnp.where(st_diag, _dot_tn(vc, kl), 0.0)
            outs[i].append(o)
    for i in range(bb):
        st_ref[i] = st[i]
        o = outs[i][0] if len(outs[i]) == 1 else jnp.concatenate(outs[i], axis=0)
        z = u_ref[i, :, a_off + LANES:a_off + LANES + GLA_WIDTH]
        o_ref[i] = _group_norm(o, gm_ref[...], ng_ref[...]) * _silu(z)


GLA_SEQS_PER_STEP = 4


def _gla(u_gla, st0, wg, bg, ng, b, l):
    tl = min(l, 128)
    c = min(l, 16)
    bb = math.gcd(b, GLA_SEQS_PER_STEP)
    nk = GLA_HEADS * GLA_DK
    u3 = u_gla.reshape(b, l, W_GLA)
    kern = functools.partial(_gla_kernel, bb=bb, tl=tl, c=c)
    return pl.pallas_call(
        kern,
        grid=(b // bb, l // tl),
        in_specs=[pl.BlockSpec((bb, tl, W_GLA), lambda bi, li: (bi, li, 0)),
                  pl.BlockSpec((bb, GLA_WIDTH, nk), lambda bi, li: (bi, 0, 0)),
                  pl.BlockSpec((LANES, nk), lambda bi, li: (0, 0)),
                  pl.BlockSpec((1, nk), lambda bi, li: (0, 0)),
                  pl.BlockSpec((1, GLA_WIDTH), lambda bi, li: (0, 0)),
                  pl.BlockSpec((GLA_WIDTH, GLA_WIDTH), lambda bi, li: (0, 0))],
        out_specs=[pl.BlockSpec((bb, tl, GLA_WIDTH), lambda bi, li: (bi, li, 0)),
                   pl.BlockSpec((bb, GLA_WIDTH, nk), lambda bi, li: (bi, 0, 0))],
        out_shape=[jax.ShapeDtypeStruct((b, l, GLA_WIDTH), F32),
                   jax.ShapeDtypeStruct((b, GLA_WIDTH, nk), F32)],
        compiler_params=_cparams("parallel", "arbitrary"),
        name="gla",
    )(u3, st0, wg, bg, ng, _group_mean_matrix(GLA_WIDTH))


def _gla_state_in(s):
    b = s.shape[0]
    st = jnp.swapaxes(s.astype(F32), 2, 3)
    eye = jnp.eye(GLA_HEADS, dtype=F32)
    full = st[:, :, :, None, :] * eye[None, :, None, :, None]
    return full.reshape(b, GLA_WIDTH, GLA_HEADS * GLA_DK)


def _gla_state_out(st):
    b = st.shape[0]
    full = st.reshape(b, GLA_HEADS, GLA_DV, GLA_HEADS, GLA_DK)
    diag = jnp.stack([full[:, h, :, h, :] for h in range(GLA_HEADS)], axis=1)
    return jnp.swapaxes(diag, 2, 3)


def _mlstm_kernel(u_ref, c0_ref, n0_ref, m0_ref, cv0_ref, cw_ref, cb_ref, gb_ref, ng_ref, gm_ref,
                  o_ref, c_ref, n_ref, m_ref, cv_ref, xp_sc, *, bb, tl):
    @pl.when(pl.program_id(1) == 0)
    def _init():
        c_ref[...] = c0_ref[...]
        n_ref[...] = n0_ref[...]
        m_ref[...] = m0_ref[...]
        xp_sc[:, 0:SUBLANES, :] = cv0_ref[...]

    pairs = ML_HEADS // 2
    state = [([c_ref[i, j] for j in range(pairs)],
              [n_ref[i, j:j + 1, :] for j in range(pairs)],
              [m_ref[i, h:h + 1, :] for h in range(ML_HEADS)]) for i in range(bb)]
    new_state = [_mlstm_seq(i, state[i], u_ref, cw_ref, cb_ref, gb_ref, ng_ref, gm_ref, o_ref, cv_ref, xp_sc, tl)
                 for i in range(bb)]
    for i in range(bb):
        cps, nps, mbs = new_state[i]
        for j in range(pairs):
            c_ref[i, j] = cps[j]
            n_ref[i, j:j + 1, :] = nps[j]
        for h in range(ML_HEADS):
            m_ref[i, h:h + 1, :] = mbs[h]


def _mlstm_seq(i, state, u_ref, cw_ref, cb_ref, gb_ref, ng_ref, gm_ref, o_ref, cv_ref, xp_sc, tl):
    cps, nps, mbs = state
    c = tl
    w2 = 2 * ML_WIDTH
    u = u_ref[i]
    xp_sc[i, SUBLANES:SUBLANES + tl, :] = u[:, 0:w2]
    conv = cb_ref[...]
    for w in range(CONV_W):
        off = SUBLANES - (CONV_W - 1) + w
        conv = conv + xp_sc[i, off:off + tl, :] * cw_ref[w:w + 1, :]
    tail = xp_sc[i, tl:tl + SUBLANES, :]
    xp_sc[i, 0:SUBLANES, :] = tail
    cv_ref[i] = tail

    qk = _silu(conv)
    mq = qk[:, 0:ML_WIDTH]
    mk = qk[:, ML_WIDTH:w2] * (ML_DH ** -0.5)
    mv = u[:, w2:w2 + ML_WIDTH]
    ifg = u[:, w2 + ML_WIDTH:w2 + ML_WIDTH + LANES] + gb_ref[...]
    logf = _log_sigmoid(ifg)
    og_off = w2 + ML_WIDTH + LANES
    og = _sigmoid(u[:, og_off:og_off + ML_WIDTH])
    zz = _silu(u[:, og_off + ML_WIDTH:og_off + 2 * ML_WIDTH])

    tri = lax.broadcasted_iota(jnp.int32, (c, c), 0) >= lax.broadcasted_iota(jnp.int32, (c, c), 1)
    lane = lax.broadcasted_iota(jnp.int32, (c, LANES), 1)
    low = lane < ML_DH
    low_row = lax.broadcasted_iota(jnp.int32, (1, LANES), 1) < ML_DH
    sq_row = lax.broadcasted_iota(jnp.int32, (LANES, LANES), 0)
    sq_col = lax.broadcasted_iota(jnp.int32, (LANES, LANES), 1)
    same_head = (sq_row < ML_DH) == (sq_col < ML_DH)

    def wide(x):
        return x[:, :c] if c <= LANES else jnp.concatenate([x] * (c // LANES), axis=1)

    fcum_all = logf
    row_l = lax.broadcasted_iota(jnp.int32, (c, LANES), 0)
    for sh in [1 << t for t in range(c.bit_length()) if (1 << t) < c]:
        fcum_all = fcum_all + jnp.where(row_l >= sh, pltpu.roll(fcum_all, sh, 0), 0.0)
    gates_t = jnp.where(lane < ML_HEADS, ifg, fcum_all).T
    new_cps, new_nps, new_mbs = [], [], []
    for j in range(ML_HEADS // 2):
        ps = slice(j * LANES, (j + 1) * LANES)
        q_s, k_s, v_s = mq[:, ps], mk[:, ps], mv[:, ps]
        acc = jnp.zeros((c, 2 * LANES), F32)
        per_head = []
        for hl in range(2):
            h = 2 * j + hl
            own = low if hl == 0 else jnp.logical_not(low)
            fc = jnp.broadcast_to(fcum_all[:, ML_HEADS + h:ML_HEADS + h + 1], (c, LANES))
            ii = jnp.broadcast_to(ifg[:, h:h + 1], (c, LANES))
            dm = jnp.where(tri, wide(fc) - gates_t[ML_HEADS + h:ML_HEADS + h + 1, :] + gates_t[h:h + 1, :], NEG_BIG)
            inter = fc + mbs[h]
            m = jnp.maximum(inter, jnp.max(dm, axis=-1, keepdims=True))
            sij = _dot_nt(jnp.where(own, q_s, 0.0), k_s) * jnp.exp(dm - wide(m))
            acc = acc + _dot(sij, jnp.concatenate([jnp.where(own, v_s, 0.0), jnp.where(own, 1.0, 0.0)], axis=1))
            m_last = m[c - 1:c]
            f_last = fc[c - 1:c]
            per_head.append((m, jnp.exp(inter - m), jnp.exp(f_last - fc + ii - m_last),
                             jnp.exp(f_last + mbs[h] - m_last)))
            new_mbs.append(m_last)
        m_p, w_p, wj_p = (jnp.where(low, per_head[0][t], per_head[1][t]) for t in range(3))
        dec_row = jnp.where(low_row, per_head[0][3], per_head[1][3])
        n_mat = jnp.where(same_head, jnp.broadcast_to(nps[j], (LANES, LANES)), 0.0)
        num = w_p * _dot_nt(q_s, cps[j]) + acc[:, 0:LANES]
        den = w_p * _dot_nt(q_s, n_mat) + acc[:, LANES:2 * LANES]
        hh = num / jnp.maximum(jnp.abs(den), jnp.exp(-m_p))
        hn = hh * lax.rsqrt(_dot_split(hh * hh, gm_ref[...]) + EPS) * ng_ref[...]
        o_ref[i, :, ps] = hn * og[:, ps] * zz[:, ps]
        dec_mat = jnp.where(sq_row < ML_DH, jnp.broadcast_to(per_head[0][3], (LANES, LANES)),
                            jnp.broadcast_to(per_head[1][3], (LANES, LANES)))
        new_cps.append(dec_mat * cps[j] + jnp.where(same_head, _dot_tn(wj_p * v_s, k_s), 0.0))
        new_nps.append(dec_row * nps[j] + jnp.sum(wj_p * k_s, axis=0, keepdims=True))
    return new_cps, new_nps, new_mbs


ML_SEQS_PER_STEP = 2


def _mlstm(u_ml, c0, n0, m0, conv0, cw, cb, gb, ng, b, l):
    tl = min(l, 256)
    bb = math.gcd(b, ML_SEQS_PER_STEP)
    pairs = ML_HEADS // 2
    w2 = 2 * ML_WIDTH
    u3 = u_ml.reshape(b, l, W_ML)
    eye2 = jnp.eye(2, dtype=F32)
    c0p = jnp.einsum('bphed,hk->bphekd', c0.astype(F32).reshape(b, pairs, 2, ML_DH, ML_DH), eye2)
    c0p = c0p.reshape(b, pairs, LANES, LANES)
    n0p = n0.astype(F32).reshape(b, pairs, LANES)
    m0b = jnp.broadcast_to(m0.astype(F32)[:, :, None], (b, ML_HEADS, LANES))
    cv0 = jnp.pad(conv0.astype(F32), ((0, 0), (SUBLANES - (CONV_W - 1), 0), (0, 0)))
    gbp = jnp.pad(gb.astype(F32).reshape(1, 2 * ML_HEADS), ((0, 0), (0, LANES - 2 * ML_HEADS)))
    kern = functools.partial(_mlstm_kernel, bb=bb, tl=tl)
    st = lambda bi, li: (bi, 0, 0)
    st4 = lambda bi, li: (bi, 0, 0, 0)
    cst = lambda bi, li: (0, 0)
    o, c_new, n_new, m_new, cv = pl.pallas_call(
        kern,
        grid=(b // bb, l // tl),
        in_specs=[pl.BlockSpec((bb, tl, W_ML), lambda bi, li: (bi, li, 0)),
                  pl.BlockSpec((bb, pairs, LANES, LANES), st4),
                  pl.BlockSpec((bb, pairs, LANES), st),
                  pl.BlockSpec((bb, ML_HEADS, LANES), st),
                  pl.BlockSpec((bb, SUBLANES, w2), st),
                  pl.BlockSpec((CONV_W, w2), cst),
                  pl.BlockSpec((1, w2), cst),
                  pl.BlockSpec((1, LANES), cst),
                  pl.BlockSpec((1, LANES), cst),
                  pl.BlockSpec((LANES, LANES), cst)],
        out_specs=[pl.BlockSpec((bb, tl, ML_WIDTH), lambda bi, li: (bi, li, 0)),
                   pl.BlockSpec((bb, pairs, LANES, LANES), st4),
                   pl.BlockSpec((bb, pairs, LANES), st),
                   pl.BlockSpec((bb, ML_HEADS, LANES), st),
                   pl.BlockSpec((bb, SUBLANES, w2), st)],
        out_shape=[jax.ShapeDtypeStruct((b, l, ML_WIDTH), F32),
                   jax.ShapeDtypeStruct((b, pairs, LANES, LANES), F32),
                   jax.ShapeDtypeStruct((b, pairs, LANES), F32),
                   jax.ShapeDtypeStruct((b, ML_HEADS, LANES), F32),
                   jax.ShapeDtypeStruct((b, SUBLANES, w2), F32)],
        scratch_shapes=[pltpu.VMEM((bb, tl + 2 * SUBLANES, w2), F32)],
        compiler_params=_cparams("parallel", "arbitrary"),
        name="mlstm",
    )(u3, c0p, n0p, m0b, cv0, cw.astype(F32), cb.astype(F32).reshape(1, w2), gbp,
      jnp.tile(ng.astype(F32).reshape(1, ML_DH), (1, 2)), _group_mean_matrix(LANES))
    c6 = c_new.reshape(b, pairs, 2, ML_DH, 2, ML_DH)
    c_out = jnp.stack([c6[:, :, h, :, h, :] for h in range(2)], axis=2).reshape(b, ML_HEADS, ML_DH, ML_DH)
    return (o, c_out, n_new.reshape(b, ML_HEADS, ML_DH), m_new[:, :, 0], cv[:, SUBLANES - (CONV_W - 1):, :])


def _rope_lanes(x, cos_t, sin_t):
    w = x.shape[1]
    half = ROT_DIM // 2
    reps = w // cos_t.shape[1]
    if reps > 1:
        cos_t = jnp.concatenate([cos_t] * reps, axis=1)
        sin_t = jnp.concatenate([sin_t] * reps, axis=1)
    lane = lax.broadcasted_iota(jnp.int32, x.shape, 1) % HEAD_DIM
    partner = jnp.where(lane < half, pltpu.roll(x, w - half, 1), pltpu.roll(x, half, 1))
    return x * cos_t + partner * sin_t


def _nsa_prep_kernel(u_ref, cos_ref, sin_ref, qg_ref, kg_ref, g4_ref, g1_ref, qn_ref, qr_ref, rows_ref, win_ref):
    u = u_ref[...]
    cos_t = cos_ref[...]
    sin_t = sin_ref[...]
    q = _group_norm(u[:, 0:NSA_WIDTH], g4_ref[...], qg_ref[...])
    qn_ref[...] = (q * ATTN_SCALE).astype(BF16)
    qr = _rope_lanes(q, cos_t, sin_t) * (ATTN_SCALE * LOG2E)
    low = lax.broadcasted_iota(jnp.int32, (qr.shape[0], LANES), 1) < HEAD_DIM
    for j in range(NSA_HEADS // 2):
        pair = qr[:, j * LANES:(j + 1) * LANES]
        swapped = pltpu.roll(pair, HEAD_DIM, 1)
        qr_ref[2 * j] = jnp.where(low, pair, swapped).astype(BF16)
        qr_ref[2 * j + 1] = jnp.where(low, swapped, pair).astype(BF16)
    kv = NSA_WIDTH
    k_slc = _rope_lanes(_group_norm(u[:, kv + 2 * KV_W:kv + 3 * KV_W], g1_ref[...], kg_ref[1:2, :]), cos_t, sin_t)
    k_win = _rope_lanes(_group_norm(u[:, kv + 4 * KV_W:kv + 5 * KV_W], g1_ref[...], kg_ref[2:3, :]), cos_t, sin_t)
    rows_ref[:, 0:2 * KV_W] = u[:, kv:kv + 2 * KV_W]
    rows_ref[:, 2 * KV_W:3 * KV_W] = k_slc
    rows_ref[:, 3 * KV_W:4 * KV_W] = u[:, kv + 3 * KV_W:kv + 4 * KV_W]
    win_ref[:, 0:KV_W] = k_win
    win_ref[:, KV_W:2 * KV_W] = u[:, kv + 5 * KV_W:kv + 6 * KV_W]


def _rope_tables(pos):
    half = ROT_DIM // 2
    inv = jnp.exp(-math.log(ROPE_THETA) * jnp.arange(half, dtype=F32) * 2.0 / ROT_DIM)
    ang = pos.astype(F32)[:, None] * inv[None, :]
    cos, sin = jnp.cos(ang), jnp.sin(ang)
    n = pos.shape[0]
    ones = jnp.ones((n, HEAD_DIM - ROT_DIM), F32)
    cos_h = jnp.concatenate([cos, cos, ones], axis=1)
    sin_h = jnp.concatenate([-sin, sin, 0.0 * ones], axis=1)
    reps = LANES // HEAD_DIM
    return jnp.tile(cos_h, (1, reps)), jnp.tile(sin_h, (1, reps))


def _nsa_prep(u_nsa, pos, q_g, k_g, l):
    rows = u_nsa.shape[0]
    tl = min(l, PROJ_ROWS)
    nb = l // tl
    cos_t, sin_t = _rope_tables(pos)
    qg = jnp.tile(q_g.astype(F32).reshape(1, HEAD_DIM), (1, NSA_HEADS))
    kg = jnp.tile(k_g.astype(F32), (1, NSA_KV_HEADS))
    cst = lambda i: (0, 0)
    return pl.pallas_call(
        _nsa_prep_kernel,
        grid=(rows // tl,),
        in_specs=[pl.BlockSpec((tl, W_NSA), lambda i: (i, 0)),
                  pl.BlockSpec((tl, LANES), lambda i: (i % nb, 0)),
                  pl.BlockSpec((tl, LANES), lambda i: (i % nb, 0)),
                  pl.BlockSpec((1, NSA_WIDTH), cst),
                  pl.BlockSpec((3, KV_W), cst),
                  pl.BlockSpec((NSA_WIDTH, NSA_WIDTH), cst),
                  pl.BlockSpec((KV_W, KV_W), cst)],
        out_specs=[pl.BlockSpec((tl, NSA_WIDTH), lambda i: (i, 0)),
                   pl.BlockSpec((NSA_HEADS, tl, LANES), lambda i: (0, i, 0)),
                   pl.BlockSpec((tl, N_KV_SLOTS * KV_W), lambda i: (i, 0)),
                   pl.BlockSpec((tl, 2 * KV_W), lambda i: (i, 0))],
        out_shape=[jax.ShapeDtypeStruct((rows, NSA_WIDTH), BF16),
                   jax.ShapeDtypeStruct((NSA_HEADS, rows, LANES), BF16),
                   jax.ShapeDtypeStruct((rows, N_KV_SLOTS * KV_W), F32),
                   jax.ShapeDtypeStruct((rows, 2 * KV_W), F32)],
        compiler_params=_cparams("parallel"),
        name="nsa_prep",
    )(u_nsa, cos_t, sin_t, qg, kg, _group_mean_matrix(NSA_WIDTH), _group_mean_matrix(KV_W))


def _cmp_halves(xk_ref, xv_ref, w_ref, n_half):
    acc_k = jnp.zeros((n_half, 2 * KV_W), F32)
    acc_v = jnp.zeros((n_half, 2 * KV_W), F32)
    for s in range(CMP_STRIDE):
        acc_k = acc_k + _dot(xk_ref[pl.ds(s, n_half, stride=CMP_STRIDE), :], w_ref[0, s])
        acc_v = acc_v + _dot(xv_ref[pl.ds(s, n_half, stride=CMP_STRIDE), :], w_ref[1, s])
    return acc_k, acc_v


def _cmp_finish(acc_k, acc_v, bias_ref, kg_ref, g1_ref, kc_ref, vc_ref, n_half):
    valid = lax.broadcasted_iota(jnp.int32, (n_half, KV_W), 0) < n_half - 1

    def summary(acc, bias):
        return acc[:, 0:KV_W] + pltpu.roll(acc[:, KV_W:2 * KV_W], n_half - 1, 0) + bias

    kc = _group_norm(summary(acc_k, bias_ref[0:1, :]), g1_ref[...], kg_ref[...])
    kc_ref[0] = jnp.where(valid, kc, 0.0).astype(BF16)
    vc_ref[0] = jnp.where(valid, summary(acc_v, bias_ref[1:2, :]), 0.0).astype(BF16)


def _cmp_kv_kernel(xk_ref, xv_ref, w_ref, bias_ref, kg_ref, g1_ref, kc_ref, vc_ref, *, n_half):
    acc_k, acc_v = _cmp_halves(xk_ref.at[0], xv_ref.at[0], w_ref, n_half)
    _cmp_finish(acc_k, acc_v, bias_ref, kg_ref, g1_ref, kc_ref, vc_ref, n_half)


PAGES_PER_STEP = 64
CMP_PAGE_GROUP = 16


def _cmp_paged_kernel(pt_ref, *refs, npg, page, whole):
    page_refs = refs[:npg]
    ng = -(-npg // CMP_PAGE_GROUP)
    scratch = refs[len(refs) - 2 * ng:]
    if whole:
        w_ref, bias_ref, kg_ref, g1_ref, kc_ref, vc_ref = refs[npg:len(refs) - 2 * ng]
    else:
        w_ref, acck_ref, accv_ref = refs[npg:len(refs) - 2 * ng]
    accs = []
    for g in range(ng):
        xk_sc, xv_sc = scratch[2 * g], scratch[2 * g + 1]
        pages = range(g * CMP_PAGE_GROUP, min((g + 1) * CMP_PAGE_GROUP, npg))
        for j, i in enumerate(pages):
            xk_sc[j * page:(j + 1) * page, :] = page_refs[i][0, 0:KV_W, :].T
            xv_sc[j * page:(j + 1) * page, :] = page_refs[i][0, KV_W:2 * KV_W, :].T
        accs.append(_cmp_halves(xk_sc, xv_sc, w_ref, len(pages) * page // CMP_STRIDE))
    acc_k = accs[0][0] if ng == 1 else jnp.concatenate([a[0] for a in accs], axis=0)
    acc_v = accs[0][1] if ng == 1 else jnp.concatenate([a[1] for a in accs], axis=0)
    n_half = npg * page // CMP_STRIDE
    if whole:
        _cmp_finish(acc_k, acc_v, bias_ref, kg_ref, g1_ref, kc_ref, vc_ref, n_half)
    else:
        acck_ref[0] = acc_k
        accv_ref[0] = acc_v


def _cmp_page_scratch(npg, page):
    sizes = [min(CMP_PAGE_GROUP, npg - g) for g in range(0, npg, CMP_PAGE_GROUP)]
    return [pltpu.VMEM((n * page, KV_W), F32) for n in sizes for _ in range(2)]


def _cmp_fin_kernel(acck_ref, accv_ref, bias_ref, kg_ref, g1_ref, kc_ref, vc_ref, *, n_half):
    _cmp_finish(acck_ref[0], accv_ref[0], bias_ref, kg_ref, g1_ref, kc_ref, vc_ref, n_half)


def _page_specs(npg, page, row_blk, pool_off):
    def spec(i):
        return pl.BlockSpec((1, 2 * KV_W, page), lambda bi, ji, pt: (pt[bi, ji * npg + i] + pool_off, row_blk, 0))
    return [spec(i) for i in range(npg)]


def _cmp_kv_paged(cache_t, page_table, pool_off, wcat, bias, kg0):
    b, n_pages = page_table.shape
    page = cache_t.shape[2]
    npg = math.gcd(n_pages, PAGES_PER_STEP)
    nh_step = npg * page // CMP_STRIDE
    n_half = n_pages * page // CMP_STRIDE
    kg = jnp.tile(kg0.astype(F32).reshape(1, HEAD_DIM), (1, NSA_KV_HEADS))
    if npg == n_pages:
        cst = lambda bi, ji, pt: (0, 0)
        return pl.pallas_call(
            functools.partial(_cmp_paged_kernel, npg=npg, page=page, whole=True),
            grid_spec=pltpu.PrefetchScalarGridSpec(
                num_scalar_prefetch=1, grid=(b, 1),
                in_specs=_page_specs(npg, page, 0, pool_off)
                + [pl.BlockSpec((2, CMP_STRIDE, KV_W, 2 * KV_W), lambda bi, ji, pt: (0, 0, 0, 0)),
                   pl.BlockSpec((2, KV_W), cst), pl.BlockSpec((1, KV_W), cst), pl.BlockSpec((KV_W, KV_W), cst)],
                out_specs=[pl.BlockSpec((1, n_half, KV_W), lambda bi, ji, pt: (bi, 0, 0)),
                           pl.BlockSpec((1, n_half, KV_W), lambda bi, ji, pt: (bi, 0, 0))],
                scratch_shapes=_cmp_page_scratch(npg, page)),
            out_shape=[jax.ShapeDtypeStruct((b, n_half, KV_W), BF16),
                       jax.ShapeDtypeStruct((b, n_half, KV_W), BF16)],
            compiler_params=_cparams("parallel", "arbitrary"),
            name="cmp_paged",
        )(page_table, *([cache_t] * npg), wcat, bias, kg, _group_mean_matrix(KV_W))
    kern = functools.partial(_cmp_paged_kernel, npg=npg, page=page, whole=False)
    acc_k, acc_v = pl.pallas_call(
        kern,
        grid_spec=pltpu.PrefetchScalarGridSpec(
            num_scalar_prefetch=1, grid=(b, n_pages // npg),
            in_specs=_page_specs(npg, page, 0, pool_off)
            + [pl.BlockSpec((2, CMP_STRIDE, KV_W, 2 * KV_W), lambda bi, ji, pt: (0, 0, 0, 0))],
            out_specs=[pl.BlockSpec((1, nh_step, 2 * KV_W), lambda bi, ji, pt: (bi, ji, 0)),
                       pl.BlockSpec((1, nh_step, 2 * KV_W), lambda bi, ji, pt: (bi, ji, 0))],
            scratch_shapes=_cmp_page_scratch(npg, page)),
        out_shape=[jax.ShapeDtypeStruct((b, n_half, 2 * KV_W), F32),
                   jax.ShapeDtypeStruct((b, n_half, 2 * KV_W), F32)],
        compiler_params=_cparams("parallel", "arbitrary"),
        name="cmp_paged",
    )(page_table, *([cache_t] * npg), wcat)
    blk = lambda bi: (bi, 0, 0)
    return pl.pallas_call(
        functools.partial(_cmp_fin_kernel, n_half=n_half),
        grid=(b,),
        in_specs=[pl.BlockSpec((1, n_half, 2 * KV_W), blk),
                  pl.BlockSpec((1, n_half, 2 * KV_W), blk),
                  pl.BlockSpec((2, KV_W), lambda bi: (0, 0)),
                  pl.BlockSpec((1, KV_W), lambda bi: (0, 0)),
                  pl.BlockSpec((KV_W, KV_W), lambda bi: (0, 0))],
        out_specs=[pl.BlockSpec((1, n_half, KV_W), blk), pl.BlockSpec((1, n_half, KV_W), blk)],
        out_shape=[jax.ShapeDtypeStruct((b, n_half, KV_W), BF16),
                   jax.ShapeDtypeStruct((b, n_half, KV_W), BF16)],
        compiler_params=_cparams("parallel"),
        name="cmp_fin",
    )(acc_k, acc_v, bias, kg, _group_mean_matrix(KV_W))


def _cmp_weights(cmp_pos, cmp_w):
    wf = cmp_w.astype(F32)
    eye_g = jnp.eye(NSA_KV_HEADS, dtype=F32)

    def bd(w):
        return jnp.einsum('ksde,gh->ksgdhe', w, eye_g).reshape(2, CMP_STRIDE, KV_W, KV_W)

    wcat = jnp.concatenate([bd(wf[:, :CMP_STRIDE]), bd(wf[:, CMP_STRIDE:])], axis=3).astype(BF16)
    bias = jnp.einsum('ksd,ksde->ke', cmp_pos.astype(F32), wf)
    return wcat, jnp.tile(bias, (1, NSA_KV_HEADS))


def _cmp_kv(x3, k_blk, v_blk, t_use, wcat, bias, kg0):
    b = x3.shape[0]
    n_half = t_use // CMP_STRIDE
    kern = functools.partial(_cmp_kv_kernel, n_half=n_half)
    kg = jnp.tile(kg0.astype(F32).reshape(1, HEAD_DIM), (1, NSA_KV_HEADS))
    return pl.pallas_call(
        kern,
        grid=(b,),
        in_specs=[pl.BlockSpec((1, t_use, KV_W), lambda bi: (bi, 0, k_blk)),
                  pl.BlockSpec((1, t_use, KV_W), lambda bi: (bi, 0, v_blk)),
                  pl.BlockSpec((2, CMP_STRIDE, KV_W, 2 * KV_W), lambda bi: (0, 0, 0, 0)),
                  pl.BlockSpec((2, KV_W), lambda bi: (0, 0)),
                  pl.BlockSpec((1, KV_W), lambda bi: (0, 0)),
                  pl.BlockSpec((KV_W, KV_W), lambda bi: (0, 0))],
        out_specs=[pl.BlockSpec((1, n_half, KV_W), lambda bi: (bi, 0, 0)),
                   pl.BlockSpec((1, n_half, KV_W), lambda bi: (bi, 0, 0))],
        out_shape=[jax.ShapeDtypeStruct((b, n_half, KV_W), BF16),
                   jax.ShapeDtypeStruct((b, n_half, KV_W), BF16)],
        compiler_params=_cparams("parallel"),
        name="cmp_kv",
    )(x3, x3, wcat, bias, kg, _group_mean_matrix(KV_W))


def _cmp_attn_kernel(qn_ref, kc_ref, vc_ref, ov_ref, o_ref, sel_ref, *, bb, tq, n_half, n_cmp, n_sel, nbp, pos0,
                     bias_out):
    biases = []
    qi = pl.program_id(1)
    nq = bb * tq
    pos_c = pos0 + qi * tq + lax.broadcasted_iota(jnp.int32, (tq, 1), 0)
    ncol = lax.broadcasted_iota(jnp.int32, (1, n_half), 1)
    cmask = (ncol * CMP_STRIDE + (CMP_BLOCK - 1) <= pos_c) & (ncol < n_cmp)
    pos_r = pos0 + qi * tq + lax.broadcasted_iota(jnp.int32, (1, nq), 1) % tq
    blk = lax.broadcasted_iota(jnp.int32, (nbp, 1), 0)
    cur = pos_r // SEL_BLOCK
    forced = (blk == 0) | (blk == cur) | (blk == cur - 1)
    valid = blk * SEL_BLOCK <= pos_r
    real = blk < n_sel
    for g in range(NSA_KV_HEADS):
        gs = slice(g * HEAD_DIM, (g + 1) * HEAD_DIM)
        psums = []
        for i in range(bb):
            kc = kc_ref[i, :, gs]
            vc = vc_ref[i, :, gs]
            psum = jnp.zeros((tq, n_half), F32)
            for h in range(NSA_HPG):
                hs = slice((g * NSA_HPG + h) * HEAD_DIM, (g * NSA_HPG + h + 1) * HEAD_DIM)
                s = _dot_nt(qn_ref[i, :, hs], kc)
                m = jnp.max(jnp.where(cmask, s, NEG_BIG), axis=-1, keepdims=True)
                m = jnp.where(m > 0.5 * NEG_BIG, m, 0.0)
                e = jnp.where(cmask, jnp.exp(s - m), 0.0)
                p = e / jnp.maximum(jnp.sum(e, axis=-1, keepdims=True), 1e-30)
                o_ref[i, :, hs] = _dot(p, vc)
                psum = psum + p
            psums.append(psum)
        psum = psums[0] if bb == 1 else jnp.concatenate(psums, axis=0)
        p_hi = psum.astype(BF16)
        p_lo = (psum - p_hi.astype(F32)).astype(BF16)
        ov = ov_ref[...]
        imp = _dot_nt(ov, p_hi) + _dot_nt(ov, p_lo)
        score = jnp.where(forced, 3e38, jnp.where(valid, imp, -1e38))
        score = jnp.where(real, score, -3e38)
        cnt = jnp.zeros((nbp, nq), F32)
        for jp in range(n_sel):
            rowv = score[jp:jp + 1, :]
            beats = (rowv > score) | ((rowv == score) & (blk > jp))
            cnt = cnt + jnp.where(beats, 1.0, 0.0)
        sel = (cnt < float(min(N_SELECT, n_sel))) & real
        if bias_out:
            sbg = jnp.where(sel, 0.0, NEG_BIG)
            if nbp < HEAD_DIM:
                sbg = jnp.concatenate([sbg, jnp.full((HEAD_DIM - nbp, nq), NEG_BIG, F32)], axis=0)
            biases.append(sbg)
        else:
            sel_ref[0, g * nbp:(g + 1) * nbp, :] = jnp.where(sel, 1.0, 0.0).astype(BF16)
    if bias_out:
        sel_ref[0] = jnp.concatenate(biases[::-1], axis=0).T.astype(BF16)


def _overlap_t(n_cmp, n_sel, n_half, nbp):
    a = SEL_BLOCK // CMP_STRIDE
    bb = CMP_BLOCK // CMP_STRIDE
    i = np.arange(n_half)[None, :]
    j = np.arange(nbp)[:, None]
    s = i - a * j + (bb - 1)
    cnt = np.maximum(np.minimum(np.minimum(s + 1, a + bb - 1 - s), min(a, bb)), 0)
    cnt = np.where((i < n_cmp) & (j < n_sel), cnt, 0)
    return jnp.asarray(cnt, dtype=BF16)


CMP_TQ = 512


def _cmp_attn(qn3, kc, vc, pos0, t_len, bias_out=False):
    b, l, _ = qn3.shape
    n_half = kc.shape[1]
    n_cmp = n_half - 1
    n_sel = -(-t_len // SEL_BLOCK)
    nbp = _round_up(n_sel, 16)
    tq = min(l, CMP_TQ)
    bb = math.gcd(b, max(LANES // tq, 1))
    kern = functools.partial(_cmp_attn_kernel, bb=bb, tq=tq, n_half=n_half, n_cmp=n_cmp, n_sel=n_sel, nbp=nbp,
                             pos0=pos0, bias_out=bias_out)
    if bias_out:
        assert bb == 1 and NSA_KV_HEADS == 2 and nbp <= HEAD_DIM
        sel_spec = pl.BlockSpec((1, tq, LANES), lambda bi, qi: (bi, qi, 0))
        sel_shape = jax.ShapeDtypeStruct((b, l, LANES), BF16)
    else:
        sel_spec = pl.BlockSpec((1, NSA_KV_HEADS * nbp, bb * tq), lambda bi, qi: (bi, 0, qi))
        sel_shape = jax.ShapeDtypeStruct((b // bb, NSA_KV_HEADS * nbp, bb * l), BF16)
    o_cmp, sel_t = pl.pallas_call(
        kern,
        grid=(b // bb, l // tq),
        in_specs=[pl.BlockSpec((bb, tq, NSA_WIDTH), lambda bi, qi: (bi, qi, 0)),
                  pl.BlockSpec((bb, n_half, KV_W), lambda bi, qi: (bi, 0, 0)),
                  pl.BlockSpec((bb, n_half, KV_W), lambda bi, qi: (bi, 0, 0)),
                  pl.BlockSpec((nbp, n_half), lambda bi, qi: (0, 0))],
        out_specs=[pl.BlockSpec((bb, tq, NSA_WIDTH), lambda bi, qi: (bi, qi, 0)), sel_spec],
        out_shape=[jax.ShapeDtypeStruct((b, l, NSA_WIDTH), F32), sel_shape],
        compiler_params=_cparams("parallel", "parallel"),
        name="cmp_attn",
    )(qn3, kc, vc, _overlap_t(n_cmp, n_sel, n_half, nbp))
    if bb > 1:
        sel_t = sel_t.reshape(b // bb, NSA_KV_HEADS * nbp, bb, l)
        sel_t = jnp.swapaxes(sel_t, 1, 2).reshape(b, NSA_KV_HEADS * nbp, l)
    return o_cmp, sel_t


SEL_TQ = 128
SEL_TK = 2048
WIN_TQ = 256


def _group_lanes(shape, g):
    lane = lax.broadcasted_iota(jnp.int32, shape, len(shape) - 1)
    return (lane < HEAD_DIM) if g == 0 else (lane >= HEAD_DIM)


def _store_heads(o_ref, r, g, tq):
    lo = g * HEAD_DIM
    for h in range(NSA_HPG):
        hh = g * NSA_HPG + h
        o_ref[0, :, hh * HEAD_DIM:(hh + 1) * HEAD_DIM] = r[h * tq:(h + 1) * tq, lo:lo + HEAD_DIM]


def _flash_sel_kernel(qi_ref, ki_ref, q_ref, k_ref, v_ref, sb_ref, o_ref, m_sc, acc_sc, *, tq, tk):
    qi = qi_ref[pl.program_id(1)]
    ki = ki_ref[pl.program_id(1)]
    kmax = (qi * tq + tq - 1) // tk
    hq = NSA_HPG * tq

    @pl.when(ki == 0)
    def _init():
        m_sc[...] = jnp.full(m_sc.shape, NEG_BIG, F32)
        acc_sc[...] = jnp.zeros(acc_sc.shape, F32)

    def step(diagonal):
        key_blk = ki * (tk // SEL_BLOCK) + lax.broadcasted_iota(jnp.int32, (tk, LANES), 0) // SEL_BLOCK
        one_hot = jnp.where(lax.broadcasted_iota(jnp.int32, (tk, LANES), 1) % HEAD_DIM == key_blk, 1.0, 0.0).astype(BF16)
        kblk = k_ref[0].astype(BF16)
        vblk = v_ref[0].astype(BF16)
        sb4 = jnp.concatenate([sb_ref[0]] * NSA_HPG, axis=0)
        if diagonal:
            qpos = qi * tq + lax.broadcasted_iota(jnp.int32, (tq, tk), 0)
            kpos = ki * tk + lax.broadcasted_iota(jnp.int32, (tq, tk), 1)
            causal4 = jnp.concatenate([jnp.where(kpos <= qpos, 0.0, NEG_BIG)] * NSA_HPG, axis=0)
        m_prev = [m_sc[g] for g in range(NSA_KV_HEADS)]
        acc_prev = [acc_sc[g] for g in range(NSA_KV_HEADS)]
        m_out, acc_out = [], []
        for g in range(NSA_KV_HEADS):
            own_k = _group_lanes((tk, LANES), g)
            q4 = q_ref[g * NSA_HPG:(g + 1) * NSA_HPG].reshape(hq, LANES)
            qx = jnp.where(_group_lanes((hq, LANES), g), q4, sb4)
            kx = jnp.where(own_k, kblk, one_hot)
            s = lax.dot_general(qx, kx, (((1,), (1,)), ((), ())), preferred_element_type=F32)
            if diagonal:
                s = s + causal4
            m_new = jnp.maximum(m_prev[g], jnp.max(s, axis=-1, keepdims=True))
            p = jnp.exp2(s - m_new[:, :1]).astype(BF16)
            vx = jnp.where(own_k, vblk, 1.0)
            acc_out.append(jnp.exp2(m_prev[g] - m_new) * acc_prev[g] + jnp.dot(p, vx, preferred_element_type=F32))
            m_out.append(m_new)
        for g in range(NSA_KV_HEADS):
            m_sc[g] = m_out[g]
            acc_sc[g] = acc_out[g]

    @pl.when(ki < kmax)
    def _full():
        step(False)

    @pl.when(ki == kmax)
    def _last():
        step(True)
        for g in range(NSA_KV_HEADS):
            acc = acc_sc[g]
            _store_heads(o_ref, acc / pltpu.roll(acc, HEAD_DIM, 1), g, tq)


def _flash_sel(q8, rows3, selb):
    b, l, _ = rows3.shape
    tq, tk = min(SEL_TQ, l), min(SEL_TK, l)
    nq = l // tq
    pairs = [(qi, ki) for qi in range(nq) for ki in range((qi * tq + tq - 1) // tk + 1)]
    qi_tab = jnp.asarray([p[0] for p in pairs], jnp.int32)
    ki_tab = jnp.asarray([p[1] for p in pairs], jnp.int32)
    kv_idx = lambda blk: (lambda bi, si, qt, kt: (bi, kt[si], blk))
    return pl.pallas_call(
        functools.partial(_flash_sel_kernel, tq=tq, tk=tk),
        grid_spec=pltpu.PrefetchScalarGridSpec(
            num_scalar_prefetch=2, grid=(b, len(pairs)),
            in_specs=[pl.BlockSpec((NSA_HEADS, tq, LANES), lambda bi, si, qt, kt: (0, bi * nq + qt[si], 0)),
                      pl.BlockSpec((1, tk, KV_W), kv_idx(2)),
                      pl.BlockSpec((1, tk, KV_W), kv_idx(3)),
                      pl.BlockSpec((1, tq, LANES), lambda bi, si, qt, kt: (bi, qt[si], 0))],
            out_specs=pl.BlockSpec((1, tq, NSA_WIDTH), lambda bi, si, qt, kt: (bi, qt[si], 0)),
            scratch_shapes=[pltpu.VMEM((NSA_KV_HEADS, NSA_HPG * tq, LANES), F32),
                            pltpu.VMEM((NSA_KV_HEADS, NSA_HPG * tq, LANES), F32)]),
        out_shape=jax.ShapeDtypeStruct((b, l, NSA_WIDTH), F32),
        compiler_params=_cparams("parallel", "arbitrary"),
        name="flash_sel",
    )(qi_tab, ki_tab, q8, rows3, rows3, selb)


def _flash_win_kernel(q_ref, *refs, tq, back):
    nspan = back + 1
    k_refs, v_refs, o_ref = refs[:nspan], refs[nspan:2 * nspan], refs[2 * nspan]
    qi = pl.program_id(1)
    hq = NSA_HPG * tq
    span = nspan * tq
    qpos = qi * tq + lax.broadcasted_iota(jnp.int32, (tq, span), 0)
    kpos = (qi - back) * tq + lax.broadcasted_iota(jnp.int32, (tq, span), 1)
    ok = (kpos <= qpos) & (kpos > qpos - WINDOW) & (kpos >= 0)
    bias4 = jnp.concatenate([jnp.where(ok, 0.0, NEG_BIG)] * NSA_HPG, axis=0)
    kcat = jnp.concatenate([r[0] for r in k_refs], axis=0).astype(BF16)
    vcat = jnp.concatenate([r[0] for r in v_refs], axis=0).astype(BF16)
    for g in range(NSA_KV_HEADS):
        q4 = q_ref[g * NSA_HPG:(g + 1) * NSA_HPG].reshape(hq, LANES)
        qx = jnp.where(_group_lanes((hq, LANES), g), q4, 0.0)
        s = lax.dot_general(qx, kcat, (((1,), (1,)), ((), ())), preferred_element_type=F32) + bias4
        p = jnp.exp2(s - jnp.max(s, axis=-1, keepdims=True)).astype(BF16)
        vx = jnp.where(_group_lanes((span, LANES), g), vcat, 1.0)
        r = jnp.dot(p, vx, preferred_element_type=F32)
        _store_heads(o_ref, r / pltpu.roll(r, HEAD_DIM, 1), g, tq)


def _flash_win(q8, win3):
    b, l, _ = win3.shape
    tq = min(WIN_TQ, l)
    nq = l // tq
    back = -(-(WINDOW - 1) // tq)

    def kv_specs(blk):
        return [pl.BlockSpec((1, tq, KV_W), (lambda bi, qi, j=j: (bi, jnp.maximum(qi - back + j, 0), blk)))
                for j in range(back + 1)]

    return pl.pallas_call(
        functools.partial(_flash_win_kernel, tq=tq, back=back),
        grid=(b, nq),
        in_specs=[pl.BlockSpec((NSA_HEADS, tq, LANES), lambda bi, qi: (0, bi * nq + qi, 0))]
        + kv_specs(0) + kv_specs(1),
        out_specs=pl.BlockSpec((1, tq, NSA_WIDTH), lambda bi, qi: (bi, qi, 0)),
        out_shape=jax.ShapeDtypeStruct((b, l, NSA_WIDTH), F32),
        compiler_params=_cparams("parallel", "parallel"),
        name="flash_win",
    )(q8, *([win3] * (2 * (back + 1))))


def _softmax_update(s, m_prev, l_prev):
    m_new = jnp.maximum(m_prev, jnp.max(s, axis=-1, keepdims=True))
    alpha = jnp.exp2(m_prev - m_new)
    p = jnp.exp2(s - m_new[:, :1])
    return m_new, alpha, alpha * l_prev + jnp.sum(p, axis=-1, keepdims=True), p


def _paged_attn_kernel(pt_ref, *refs, npg, page, l_new, nbp):
    page_refs = refs[:npg]
    (qz_ref, sel_ref, kn_ref, vn_ref, win_ref, kwn_ref, vwn_ref,
     oslc_ref, owin_ref, wout_ref, m_sc, l_sc, acc_sc) = refs[npg:]
    ji = pl.program_id(1)
    nr = qz_ref.shape[1]
    span = npg * page

    @pl.when(ji == 0)
    def _init():
        m_sc[...] = jnp.full(m_sc.shape, NEG_BIG, F32)
        l_sc[...] = jnp.zeros(l_sc.shape, F32)
        acc_sc[...] = jnp.zeros(acc_sc.shape, F32)

    qz = qz_ref[0]
    kt = jnp.concatenate([page_refs[i][0, 0:KV_W, :] for i in range(npg)], axis=1).astype(BF16)
    vt = jnp.concatenate([page_refs[i][0, KV_W:2 * KV_W, :] for i in range(npg)], axis=1).astype(BF16)
    s = jnp.dot(qz, kt, preferred_element_type=F32)
    e_row = lax.broadcasted_iota(jnp.int32, (nbp, span), 0)
    e_col = lax.broadcasted_iota(jnp.int32, (nbp, span), 1)
    expand = jnp.where(e_row == ji * (span // SEL_BLOCK) + e_col // SEL_BLOCK, 1.0, 0.0).astype(BF16)
    picked = jnp.dot(sel_ref[0], expand, preferred_element_type=F32) > 0.5
    m_new, alpha, l_new_v, p = _softmax_update(jnp.where(picked, s, NEG_BIG), m_sc[...], l_sc[...])
    m_sc[...] = m_new
    l_sc[...] = l_new_v
    acc_sc[...] = alpha * acc_sc[...] + _dot_nt(p, vt)

    @pl.when(ji == pl.num_programs(1) - 1)
    def _fin():
        q_of_row = lax.broadcasted_iota(jnp.int32, (nr, l_new), 0) % l_new
        j_new = lax.broadcasted_iota(jnp.int32, (nr, l_new), 1)
        new_ok = j_new <= q_of_row
        sn = jnp.where(new_ok, _dot_nt(qz, kn_ref[0]), NEG_BIG)
        m2, a2, l2, p2 = _softmax_update(sn, m_sc[...], l_sc[...])
        oslc_ref[0] = (a2 * acc_sc[...] + _dot(p2, vn_ref[0])) / l2
        wlen = win_ref.shape[2]
        kw = win_ref[0, 0:KV_W, :]
        vw = win_ref[0, KV_W:2 * KV_W, :]
        i_old = lax.broadcasted_iota(jnp.int32, (nr, wlen), 1)
        q_old = lax.broadcasted_iota(jnp.int32, (nr, wlen), 0) % l_new
        sw = jnp.where(i_old + (WINDOW - wlen) > q_old, _dot(qz, kw), NEG_BIG)
        swn = jnp.where(new_ok, _dot_nt(qz, kwn_ref[0]), NEG_BIG)
        mw = jnp.maximum(jnp.max(sw, axis=-1, keepdims=True), jnp.max(swn, axis=-1, keepdims=True))
        pw = jnp.exp2(sw - mw)
        pwn = jnp.exp2(swn - mw)
        lw = jnp.sum(pw, axis=-1, keepdims=True) + jnp.sum(pwn, axis=-1, keepdims=True)
        owin_ref[0] = (_dot_nt(pw, vw) + _dot(pwn, vwn_ref[0])) / lw
        new_t = jnp.concatenate([kwn_ref[0], vwn_ref[0]], axis=1).T
        place = (lax.broadcasted_iota(jnp.int32, (l_new, wlen), 1)
                 == lax.broadcasted_iota(jnp.int32, (l_new, wlen), 0) + (wlen - l_new)).astype(F32)
        lane = lax.broadcasted_iota(jnp.int32, (2 * KV_W, wlen), 1)
        wout_ref[0] = jnp.where(lane < wlen - l_new, pltpu.roll(win_ref[0], wlen - l_new, 1), _dot_f32(new_t, place))


def _paged_attn(cache_t, page_table, pool_off, qz, sel_rows, rows3, win_t, win_off, win3):
    b, n_pages = page_table.shape
    page = cache_t.shape[2]
    npg = math.gcd(n_pages, PAGES_PER_STEP)
    nr = qz.shape[1]
    l_new = rows3.shape[1]
    nbp = sel_rows.shape[2]
    wlen = win_t.shape[2]
    kern = functools.partial(_paged_attn_kernel, npg=npg, page=page, l_new=l_new, nbp=nbp)
    per_b = lambda bi, ji, pt: (bi, 0, 0)
    return pl.pallas_call(
        kern,
        grid_spec=pltpu.PrefetchScalarGridSpec(
            num_scalar_prefetch=1, grid=(b, n_pages // npg),
            in_specs=_page_specs(npg, page, 1, pool_off)
            + [pl.BlockSpec((1, nr, KV_W), per_b),
               pl.BlockSpec((1, nr, nbp), per_b),
               pl.BlockSpec((1, l_new, KV_W), lambda bi, ji, pt: (bi, 0, 2)),
               pl.BlockSpec((1, l_new, KV_W), lambda bi, ji, pt: (bi, 0, 3)),
               pl.BlockSpec((1, 2 * KV_W, wlen), lambda bi, ji, pt: (bi + win_off, 0, 0)),
               pl.BlockSpec((1, l_new, KV_W), lambda bi, ji, pt: (bi, 0, 0)),
               pl.BlockSpec((1, l_new, KV_W), lambda bi, ji, pt: (bi, 0, 1))],
            out_specs=[pl.BlockSpec((1, nr, KV_W), per_b),
                       pl.BlockSpec((1, nr, KV_W), per_b),
                       pl.BlockSpec((1, 2 * KV_W, wlen), per_b)],
            scratch_shapes=[pltpu.VMEM((nr, LANES), F32), pltpu.VMEM((nr, LANES), F32),
                            pltpu.VMEM((nr, KV_W), F32)]),
        out_shape=[jax.ShapeDtypeStruct((b, nr, KV_W), F32),
                   jax.ShapeDtypeStruct((b, nr, KV_W), F32),
                   jax.ShapeDtypeStruct((b, 2 * KV_W, wlen), F32)],
        compiler_params=_cparams("parallel", "arbitrary"),
        name="paged_attn",
    )(page_table, *([cache_t] * npg), qz, sel_rows, rows3, rows3, win_t, win3, win3)


def _nsa_fresh(u_nsa, b, l, win_keep, q_g, k_g, cmp_pos, cmp_w):
    qn, qr, rows, win = _nsa_prep(u_nsa, jnp.arange(l), q_g, k_g, l)
    rows3 = rows.reshape(b, l, N_KV_SLOTS * KV_W)
    win3 = win.reshape(b, l, 2 * KV_W)
    wcat, bias = _cmp_weights(cmp_pos, cmp_w)
    t_use = (l // CMP_STRIDE) * CMP_STRIDE
    kc, vc = _cmp_kv(rows3, 0, 1, t_use, wcat, bias, k_g[0])
    o_cmp, selb = _cmp_attn(qn.reshape(b, l, NSA_WIDTH), kc, vc, 0, l, bias_out=True)
    o_slc = _flash_sel(qr, rows3, selb)
    o_win = _flash_win(qr, win3)
    branches = tuple(o.reshape(b * l, NSA_WIDTH) for o in (o_cmp, o_slc, o_win))
    new_rows = rows3.reshape(b, l, N_KV_SLOTS, NSA_KV_HEADS, HEAD_DIM)
    new_win = win3[:, l - win_keep:].reshape(b, win_keep, 2, NSA_KV_HEADS, HEAD_DIM)
    return branches, new_rows, new_win


def _nsa_paged(u_nsa, b, l, paged, q_g, k_g, cmp_pos, cmp_w):
    cache_t, page_table, pool_off, win_t, win_off = paged
    past_len = page_table.shape[1] * cache_t.shape[2]
    assert (past_len + l) // CMP_STRIDE == past_len // CMP_STRIDE and past_len % SEL_BLOCK == 0
    qn, qr, rows, win = _nsa_prep(u_nsa, past_len + jnp.arange(l), q_g, k_g, l)
    rows3 = rows.reshape(b, l, N_KV_SLOTS * KV_W)
    win3 = win.reshape(b, l, 2 * KV_W)
    wcat, bias = _cmp_weights(cmp_pos, cmp_w)
    kc, vc = _cmp_kv_paged(cache_t, page_table, pool_off, wcat, bias, k_g[0])
    o_cmp, sel_t = _cmp_attn(qn.reshape(b, l, NSA_WIDTH), kc, vc, past_len, past_len + l)
    nbp = sel_t.shape[1] // NSA_KV_HEADS
    sel_rows = jnp.swapaxes(sel_t.reshape(b, NSA_KV_HEADS, 1, nbp, l), 3, 4)
    sel_rows = jnp.broadcast_to(sel_rows, (b, NSA_KV_HEADS, NSA_HPG, l, nbp)).reshape(b, NSA_HEADS * l, nbp)
    q5 = jnp.transpose(qr[:, :, :HEAD_DIM].reshape(NSA_KV_HEADS, NSA_HPG, b, l, HEAD_DIM), (2, 0, 1, 3, 4))
    qz = jnp.einsum('bghqd,gk->bghqkd', q5, jnp.eye(NSA_KV_HEADS, dtype=q5.dtype)).reshape(b, NSA_HEADS * l, KV_W)
    o_slc_z, o_win_z, wout = _paged_attn(cache_t, page_table, pool_off, qz, sel_rows, rows3, win_t, win_off, win3)

    def own_group(o):
        o6 = o.reshape(b, NSA_KV_HEADS, NSA_HPG, l, NSA_KV_HEADS, HEAD_DIM)
        d = jnp.stack([o6[:, g, :, :, g, :] for g in range(NSA_KV_HEADS)], axis=1)
        return jnp.transpose(d, (0, 3, 1, 2, 4)).reshape(b * l, NSA_WIDTH)

    branches = (o_cmp.reshape(b * l, NSA_WIDTH), own_group(o_slc_z), own_group(o_win_z))
    new_rows = rows3.reshape(b, l, N_KV_SLOTS, NSA_KV_HEADS, HEAD_DIM)
    wlen = wout.shape[2]
    new_win = jnp.transpose(wout.reshape(b, 2, NSA_KV_HEADS, HEAD_DIM, wlen), (0, 4, 1, 2, 3))
    return branches, new_rows, new_win


PROJ_ROWS = 512


def _hybrid_layer(x, paged, win_keep, s_gla, c_ml, n_ml, m_ml, conv_ml,
                  norm_g, w_in_pad, w_out_bf, gla_w_gate, gla_b_gate, gla_norm_g,
                  nsa_q_norm_g, nsa_k_norm_g, nsa_cmp_pos, nsa_cmp_w,
                  ml_conv_w, ml_conv_b, ml_gate_b, ml_norm_g):
    b, l, _ = x.shape
    rows = b * l
    tm = math.gcd(rows, PROJ_ROWS)
    x2d = x.reshape(rows, D_MODEL)
    u_gla, u_nsa, u_ml = _proj_in(x2d, norm_g, w_in_pad, tm)

    nk = GLA_HEADS * GLA_DK
    wg = jnp.pad(gla_w_gate.astype(F32), ((0, LANES - GLA_RANK), (0, 0))).astype(BF16)
    o_a, st_new = _gla(u_gla, _gla_state_in(s_gla), wg, gla_b_gate.astype(F32).reshape(1, nk),
                       jnp.tile(gla_norm_g.astype(F32).reshape(1, GLA_DV), (1, GLA_HEADS)), b, l)
    s_new = _gla_state_out(st_new)

    if paged is None:
        o_nsa, new_rows, new_win = _nsa_fresh(u_nsa, b, l, win_keep, nsa_q_norm_g, nsa_k_norm_g,
                                              nsa_cmp_pos, nsa_cmp_w)
    else:
        o_nsa, new_rows, new_win = _nsa_paged(u_nsa, b, l, paged, nsa_q_norm_g, nsa_k_norm_g,
                                              nsa_cmp_pos, nsa_cmp_w)

    o_c, c_new, n_new, m_new, conv_new = _mlstm(u_ml, c_ml, n_ml, m_ml, conv_ml, ml_conv_w, ml_conv_b,
                                                ml_gate_b, ml_norm_g, b, l)

    y = _proj_out(o_a.reshape(rows, GLA_WIDTH), *o_nsa, u_nsa, o_c.reshape(rows, ML_WIDTH), x2d, w_out_bf, tm)
    return y.reshape(b, l, D_MODEL), new_rows, new_win, s_new, c_new, n_new, m_new, conv_new


def kernel(x_prompt, x_sample, cache_nsa_kv, state_nsa_win, state_gla, state_mlstm_C, state_mlstm_n,
           state_mlstm_m, state_mlstm_conv, page_table, norm_g, w_in, w_out, gla_w_gate, gla_b_gate,
           gla_norm_g, nsa_q_norm_g, nsa_k_norm_g, nsa_cmp_pos, nsa_cmp_w, ml_conv_w, ml_conv_b,
           ml_gate_b, ml_norm_g):
    bp, sp, _ = x_prompt.shape
    bs, _, _ = x_sample.shape
    depth = w_in.shape[0]
    dt = x_prompt.dtype
    zero_gla = jnp.zeros((bp, GLA_HEADS, GLA_DK, GLA_DV), F32)
    zero_c = jnp.zeros((bp, ML_HEADS, ML_DH, ML_DH), F32)
    zero_n = jnp.zeros((bp, ML_HEADS, ML_DH), F32)
    zero_m = jnp.zeros((bp, ML_HEADS), F32)
    zero_conv = jnp.zeros((bp, CONV_W - 1, 2 * ML_WIDTH), dt)
    keep_p = min(WINDOW, sp)
    keep_s = state_nsa_win.shape[2]
    n_pool = cache_nsa_kv.shape[1]
    cache_t = jnp.transpose(cache_nsa_kv, (0, 1, 3, 4, 5, 2)).reshape(
        depth * n_pool, N_KV_SLOTS * KV_W, cache_nsa_kv.shape[2]).astype(F32)
    win_t = jnp.transpose(state_nsa_win, (0, 1, 3, 4, 5, 2)).reshape(depth * bs, 2 * KV_W, keep_s).astype(F32)

    y_prompt, y_sample = x_prompt, x_sample
    p_layers, s_layers = [], []
    for layer in range(depth):
        w = (norm_g[layer], _pad_w_in(w_in[layer]), w_out[layer].astype(BF16), gla_w_gate[layer],
             gla_b_gate[layer], gla_norm_g[layer],
             nsa_q_norm_g[layer], nsa_k_norm_g[layer], nsa_cmp_pos[layer], nsa_cmp_w[layer],
             ml_conv_w[layer], ml_conv_b[layer], ml_gate_b[layer], ml_norm_g[layer])
        y_prompt, *p_new = _hybrid_layer(y_prompt, None, keep_p, zero_gla, zero_c, zero_n,
                                         zero_m, zero_conv, *w)
        paged = (cache_t, page_table, layer * n_pool, win_t, layer * bs)
        y_sample, *s_new = _hybrid_layer(y_sample, paged, keep_s,
                                         state_gla[layer], state_mlstm_C[layer], state_mlstm_n[layer],
                                         state_mlstm_m[layer], state_mlstm_conv[layer], *w)
        p_layers.append(p_new)
        s_layers.append(s_new)
    p_kv, p_win, p_gla, p_c, p_n, p_m, p_conv = [jnp.stack(z) for z in zip(*p_layers)]
    s_kv, s_win, s_gla, s_c, s_n, s_m, s_conv = [jnp.stack(z) for z in zip(*s_layers)]
    return (y_prompt, y_sample, p_kv, s_kv, p_win, s_win, p_gla, s_gla, p_c, s_c, p_n, s_n, p_m, s_m, p_conv, s_conv)
```

```python
import functools
import math

import jax
import jax.numpy as jnp
import numpy as np
from jax import lax
from jax.experimental import pallas as pl
from jax.experimental.pallas import tpu as pltpu

F32 = jnp.float32
BF16 = jnp.bfloat16
HIGHEST = lax.Precision.HIGHEST

D_MODEL = 1024
HEAD_DIM = 64
GLA_WIDTH = D_MODEL // 4
NSA_WIDTH = D_MODEL // 2
ML_WIDTH = D_MODEL - GLA_WIDTH - NSA_WIDTH
D_MIX = GLA_WIDTH + NSA_WIDTH + ML_WIDTH

GLA_HEADS = GLA_WIDTH // HEAD_DIM
GLA_DK = HEAD_DIM // 2
GLA_DV = HEAD_DIM
GLA_RANK = 16
GLA_TAU = 16.0
GLA_CHUNK = 64

NSA_HEADS = NSA_WIDTH // HEAD_DIM
NSA_KV_HEADS = 2
NSA_HPG = NSA_HEADS // NSA_KV_HEADS
CMP_BLOCK = 32
CMP_STRIDE = 16
SEL_BLOCK = 64
N_SELECT = 16
WINDOW = 512
Q_BLOCK = 128
N_KV_SLOTS = 4
ROT_DIM = HEAD_DIM // 4
ROPE_THETA = 500000.0
ATTN_SCALE = HEAD_DIM ** -0.5

ML_HEADS = ML_WIDTH // HEAD_DIM
ML_DH = HEAD_DIM
ML_CHUNK = 64
CONV_W = 4

SPLIT_SIZES = (GLA_HEADS * GLA_DK, GLA_HEADS * GLA_DK, GLA_WIDTH, GLA_RANK, GLA_WIDTH,
               NSA_WIDTH, 6 * NSA_KV_HEADS * HEAD_DIM, 3 * NSA_HEADS, NSA_WIDTH,
               2 * ML_WIDTH, ML_WIDTH, 2 * ML_HEADS, ML_WIDTH, ML_WIDTH)

LANES = 128
SUBLANES = 8
VMEM_LIMIT = 56 * 1024 * 1024
NEG_BIG = -1e30
EPS = 1e-6
LOG2E = math.log2(math.e)


def _round_up(n, m):
    return -(-n // m) * m


PAD_SIZES = tuple(_round_up(s, LANES) for s in SPLIT_SIZES)
D_IN_PAD = sum(PAD_SIZES)
W_GLA = sum(PAD_SIZES[0:5])
W_NSA = sum(PAD_SIZES[5:9])
W_ML = sum(PAD_SIZES[9:14])
KV_W = NSA_KV_HEADS * HEAD_DIM


def _dot(a, b):
    return jnp.dot(a.astype(BF16), b.astype(BF16), preferred_element_type=F32)


def _dot_nt(a, b):
    return lax.dot_general(a.astype(BF16), b.astype(BF16), (((1,), (1,)), ((), ())), preferred_element_type=F32)


def _dot_tn(a, b):
    return lax.dot_general(a.astype(BF16), b.astype(BF16), (((0,), (0,)), ((), ())), preferred_element_type=F32)


def _dot_f32(a, b):
    return jnp.dot(a, b, precision=HIGHEST, preferred_element_type=F32)


def _dot_split(a, b):
    a_hi = a.astype(BF16)
    a_lo = (a - a_hi.astype(F32)).astype(BF16)
    bb = b.astype(BF16)
    return jnp.dot(a_hi, bb, preferred_element_type=F32) + jnp.dot(a_lo, bb, preferred_element_type=F32)


def _log_sigmoid(x):
    return jnp.minimum(x, 0.0) - jnp.log1p(jnp.exp(-jnp.abs(x)))


def _sigmoid(x):
    return 1.0 / (1.0 + jnp.exp(-x))


def _silu(x):
    return x * _sigmoid(x)


def _group_mean_matrix(width):
    g = np.kron(np.eye(width // HEAD_DIM, dtype=np.float32), np.full((HEAD_DIM, HEAD_DIM), 1.0 / HEAD_DIM, np.float32))
    return jnp.asarray(g, dtype=BF16)


def _group_norm(x, gmat, gain):
    ms = _dot_split(x * x, gmat)
    return x * lax.rsqrt(ms + EPS) * gain


def _cparams(*sem):
    return pltpu.CompilerParams(dimension_semantics=sem, vmem_limit_bytes=VMEM_LIMIT)


def _proj_in_kernel(x_ref, g_ref, w_ref, ug_ref, un_ref, um_ref):
    x = x_ref[...]
    y = x * lax.rsqrt(jnp.mean(x * x, axis=-1, keepdims=True) + EPS) * g_ref[...]
    r = jnp.dot(y.astype(BF16), w_ref[...], preferred_element_type=F32)
    ug_ref[...] = r[:, 0:W_GLA]
    un_ref[...] = r[:, W_GLA:W_GLA + W_NSA]
    um_ref[...] = r[:, W_GLA + W_NSA:D_IN_PAD]


def _proj_in(x2d, g, w_pad, tm):
    rows = x2d.shape[0]
    return pl.pallas_call(
        _proj_in_kernel,
        grid=(rows // tm,),
        in_specs=[pl.BlockSpec((tm, D_MODEL), lambda i: (i, 0)),
                  pl.BlockSpec((1, D_MODEL), lambda i: (0, 0)),
                  pl.BlockSpec((D_MODEL, D_IN_PAD), lambda i: (0, 0), pipeline_mode=pl.Buffered(1))],
        out_specs=[pl.BlockSpec((tm, W_GLA), lambda i: (i, 0)),
                   pl.BlockSpec((tm, W_NSA), lambda i: (i, 0)),
                   pl.BlockSpec((tm, W_ML), lambda i: (i, 0))],
        out_shape=[jax.ShapeDtypeStruct((rows, W_GLA), F32),
                   jax.ShapeDtypeStruct((rows, W_NSA), F32),
                   jax.ShapeDtypeStruct((rows, W_ML), F32)],
        compiler_params=_cparams("parallel"),
        name="proj_in",
    )(x2d, g.reshape(1, D_MODEL), w_pad)


def _pad_w_in(w_in):
    parts = []
    off = 0
    for s, p in zip(SPLIT_SIZES, PAD_SIZES):
        seg = w_in[:, off:off + s]
        if p != s:
            seg = jnp.pad(seg, ((0, 0), (0, p - s)))
        parts.append(seg)
        off += s
    return jnp.concatenate(parts, axis=1).astype(BF16)


def _proj_out_kernel(oa_ref, ocmp_ref, oslc_ref, owin_ref, gz_ref, oc_ref, x_ref, w_ref, e_ref, y_ref):
    gate = _sigmoid(gz_ref[:, 0:LANES])
    ob = (_dot_split(gate, e_ref[0]) * ocmp_ref[...] + _dot_split(gate, e_ref[1]) * oslc_ref[...]
          + _dot_split(gate, e_ref[2]) * owin_ref[...]) * _silu(gz_ref[:, LANES:LANES + NSA_WIDTH])
    y = x_ref[...]
    y = y + _dot(oa_ref[...], w_ref[0:GLA_WIDTH, :])
    y = y + _dot(ob, w_ref[GLA_WIDTH:GLA_WIDTH + NSA_WIDTH, :])
    y = y + _dot(oc_ref[...], w_ref[GLA_WIDTH + NSA_WIDTH:D_MIX, :])
    y_ref[...] = y


def _gate_expand():
    e = np.zeros((3, LANES, NSA_WIDTH), np.float32)
    for j in range(3):
        for h in range(NSA_HEADS):
            e[j, j * NSA_HEADS + h, h * HEAD_DIM:(h + 1) * HEAD_DIM] = 1.0
    return jnp.asarray(e, dtype=BF16)


def _proj_out(oa, o_cmp, o_slc, o_win, u_nsa, oc, x2d, w_bf, tm):
    rows = x2d.shape[0]
    gz_w = LANES + NSA_WIDTH
    gz_blk = (NSA_WIDTH + 6 * KV_W) // gz_w
    assert gz_blk * gz_w == NSA_WIDTH + 6 * KV_W
    row = lambda i: (i, 0)
    return pl.pallas_call(
        _proj_out_kernel,
        grid=(rows // tm,),
        in_specs=[pl.BlockSpec((tm, GLA_WIDTH), row),
                  pl.BlockSpec((tm, NSA_WIDTH), row),
                  pl.BlockSpec((tm, NSA_WIDTH), row),
                  pl.BlockSpec((tm, NSA_WIDTH), row),
                  pl.BlockSpec((tm, gz_w), lambda i: (i, gz_blk)),
                  pl.BlockSpec((tm, ML_WIDTH), row),
                  pl.BlockSpec((tm, D_MODEL), row),
                  pl.BlockSpec((D_MIX, D_MODEL), lambda i: (0, 0)),
                  pl.BlockSpec((3, LANES, NSA_WIDTH), lambda i: (0, 0, 0))],
        out_specs=pl.BlockSpec((tm, D_MODEL), row),
        out_shape=jax.ShapeDtypeStruct((rows, D_MODEL), F32),
        compiler_params=_cparams("parallel"),
        name="proj_out",
    )(oa, o_cmp, o_slc, o_win, u_nsa, oc, x2d, w_bf, _gate_expand())


def _gla_kernel(u_ref, st0_ref, wg_ref, bg_ref, ng_ref, gm_ref, o_ref, st_ref, *, bb, tl, c):
    @pl.when(pl.program_id(1) == 0)
    def _init():
        st_ref[...] = st0_ref[...]

    nk = GLA_HEADS * GLA_DK
    a_off = 2 * nk + GLA_WIDTH
    q, k, v, log_a = [], [], [], []
    for i in range(bb):
        q.append(u_ref[i, :, 0:nk] * (GLA_DK ** -0.5))
        k.append(u_ref[i, :, nk:2 * nk])
        v.append(u_ref[i, :, 2 * nk:2 * nk + GLA_WIDTH])
        pre = _dot(u_ref[i, :, a_off:a_off + LANES], wg_ref[...]) + bg_ref[...]
        log_a.append(_log_sigmoid(pre) * (1.0 / GLA_TAU))

    row_c = lax.broadcasted_iota(jnp.int32, (c, nk), 0)
    shifts = [1 << t for t in range(c.bit_length()) if (1 << t) < c]
    hc = GLA_HEADS * c
    tri_h = (lax.broadcasted_iota(jnp.int32, (hc, c), 0) % c) >= lax.broadcasted_iota(jnp.int32, (hc, c), 1)
    k_head = lax.broadcasted_iota(jnp.int32, (1, nk), 1) // GLA_DK
    v_head = lax.broadcasted_iota(jnp.int32, (1, GLA_WIDTH), 1) // GLA_DV
    st_diag = (lax.broadcasted_iota(jnp.int32, (GLA_WIDTH, nk), 0) // GLA_DV
               == lax.broadcasted_iota(jnp.int32, (GLA_WIDTH, nk), 1) // GLA_DK)

    st = [st_ref[i] for i in range(bb)]
    outs = [[] for _ in range(bb)]
    for j in range(tl // c):
        sl = slice(j * c, (j + 1) * c)
        for i in range(bb):
            b = log_a[i][sl]
            for sh in shifts:
                b = b + jnp.where(row_c >= sh, pltpu.roll(b, sh, 0), 0.0)
            blast = b[c - 1:c]
            qe = q[i][sl] * jnp.exp(b)
            ke = k[i][sl] * jnp.exp(-b)
            kl = k[i][sl] * jnp.exp(blast - b)
            vc = v[i][sl]
            qx = jnp.concatenate([jnp.where(k_head == h, qe, 0.0) for h in range(GLA_HEADS)], axis=0)
            a = jnp.where(tri_h, _dot_nt(qx, ke), 0.0)
            r = _dot(a, vc)
            o = _dot_nt(qe, st[i])
            for h in range(GLA_HEADS):
                o = o + jnp.where(v_head == h, r[h * c:(h + 1) * c], 0.0)
            st[i] = st[i] * jnp.exp(blast) + jnp.where(st_diag, _dot_tn(vc, kl), 0.0)
            outs[i].append(o)
    for i in range(bb):
        st_ref[i] = st[i]
        o = outs[i][0] if len(outs[i]) == 1 else jnp.concatenate(outs[i], axis=0)
        z = u_ref[i, :, a_off + LANES:a_off + LANES + GLA_WIDTH]
        o_ref[i] = _group_norm(o, gm_ref[...], ng_ref[...]) * _silu(z)


GLA_SEQS_PER_STEP = 4


def _gla(u_gla, st0, wg, bg, ng, b, l):
    tl = min(l, 128)
    c = min(l, 16)
    bb = math.gcd(b, GLA_SEQS_PER_STEP)
    nk = GLA_HEADS * GLA_DK
    u3 = u_gla.reshape(b, l, W_GLA)
    kern = functools.partial(_gla_kernel, bb=bb, tl=tl, c=c)
    return pl.pallas_call(
        kern,
        grid=(b // bb, l // tl),
        in_specs=[pl.BlockSpec((bb, tl, W_GLA), lambda bi, li: (bi, li, 0)),
                  pl.BlockSpec((bb, GLA_WIDTH, nk), lambda bi, li: (bi, 0, 0)),
                  pl.BlockSpec((LANES, nk), lambda bi, li: (0, 0)),
                  pl.BlockSpec((1, nk), lambda bi, li: (0, 0)),
                  pl.BlockSpec((1, GLA_WIDTH), lambda bi, li: (0, 0)),
                  pl.BlockSpec((GLA_WIDTH, GLA_WIDTH), lambda bi, li: (0, 0))],
        out_specs=[pl.BlockSpec((bb, tl, GLA_WIDTH), lambda bi, li: (bi, li, 0)),
                   pl.BlockSpec((bb, GLA_WIDTH, nk), lambda bi, li: (bi, 0, 0))],
        out_shape=[jax.ShapeDtypeStruct((b, l, GLA_WIDTH), F32),
                   jax.ShapeDtypeStruct((b, GLA_WIDTH, nk), F32)],
        compiler_params=_cparams("parallel", "arbitrary"),
        name="gla",
    )(u3, st0, wg, bg, ng, _group_mean_matrix(GLA_WIDTH))


def _gla_state_in(s):
    b = s.shape[0]
    st = jnp.swapaxes(s.astype(F32), 2, 3)
    eye = jnp.eye(GLA_HEADS, dtype=F32)
    full = st[:, :, :, None, :] * eye[None, :, None, :, None]
    return full.reshape(b, GLA_WIDTH, GLA_HEADS * GLA_DK)


def _gla_state_out(st):
    b = st.shape[0]
    full = st.reshape(b, GLA_HEADS, GLA_DV, GLA_HEADS, GLA_DK)
    diag = jnp.stack([full[:, h, :, h, :] for h in range(GLA_HEADS)], axis=1)
    return jnp.swapaxes(diag, 2, 3)


def _mlstm_kernel(u_ref, c0_ref, n0_ref, m0_ref, cv0_ref, cw_ref, cb_ref, gb_ref, ng_ref, gm_ref,
                  o_ref, c_ref, n_ref, m_ref, cv_ref, xp_sc, *, bb, tl):
    @pl.when(pl.program_id(1) == 0)
    def _init():
        c_ref[...] = c0_ref[...]
        n_ref[...] = n0_ref[...]
        m_ref[...] = m0_ref[...]
        xp_sc[:, 0:SUBLANES, :] = cv0_ref[...]

    pairs = ML_HEADS // 2
    state = [([c_ref[i, j] for j in range(pairs)],
              [n_ref[i, j:j + 1, :] for j in range(pairs)],
              [m_ref[i, h:h + 1, :] for h in range(ML_HEADS)]) for i in range(bb)]
    new_state = [_mlstm_seq(i, state[i], u_ref, cw_ref, cb_ref, gb_ref, ng_ref, gm_ref, o_ref, cv_ref, xp_sc, tl)
                 for i in range(bb)]
    for i in range(bb):
        cps, nps, mbs = new_state[i]
        for j in range(pairs):
            c_ref[i, j] = cps[j]
            n_ref[i, j:j + 1, :] = nps[j]
        for h in range(ML_HEADS):
            m_ref[i, h:h + 1, :] = mbs[h]


def _mlstm_seq(i, state, u_ref, cw_ref, cb_ref, gb_ref, ng_ref, gm_ref, o_ref, cv_ref, xp_sc, tl):
    cps, nps, mbs = state
    c = tl
    w2 = 2 * ML_WIDTH
    u = u_ref[i]
    xp_sc[i, SUBLANES:SUBLANES + tl, :] = u[:, 0:w2]
    conv = cb_ref[...]
    for w in range(CONV_W):
        off = SUBLANES - (CONV_W - 1) + w
        conv = conv + xp_sc[i, off:off + tl, :] * cw_ref[w:w + 1, :]
    tail = xp_sc[i, tl:tl + SUBLANES, :]
    xp_sc[i, 0:SUBLANES, :] = tail
    cv_ref[i] = tail

    qk = _silu(conv)
    mq = qk[:, 0:ML_WIDTH]
    mk = qk[:, ML_WIDTH:w2] * (ML_DH ** -0.5)
    mv = u[:, w2:w2 + ML_WIDTH]
    ifg = u[:, w2 + ML_WIDTH:w2 + ML_WIDTH + LANES] + gb_ref[...]
    logf = _log_sigmoid(ifg)
    og_off = w2 + ML_WIDTH + LANES
    og = _sigmoid(u[:, og_off:og_off + ML_WIDTH])
    zz = _silu(u[:, og_off + ML_WIDTH:og_off + 2 * ML_WIDTH])

    tri = lax.broadcasted_iota(jnp.int32, (c, c), 0) >= lax.broadcasted_iota(jnp.int32, (c, c), 1)
    lane = lax.broadcasted_iota(jnp.int32, (c, LANES), 1)
    low = lane < ML_DH
    low_row = lax.broadcasted_iota(jnp.int32, (1, LANES), 1) < ML_DH
    sq_row = lax.broadcasted_iota(jnp.int32, (LANES, LANES), 0)
    sq_col = lax.broadcasted_iota(jnp.int32, (LANES, LANES), 1)
    same_head = (sq_row < ML_DH) == (sq_col < ML_DH)

    def wide(x):
        return x[:, :c] if c <= LANES else jnp.concatenate([x] * (c // LANES), axis=1)

    fcum_all = logf
    row_l = lax.broadcasted_iota(jnp.int32, (c, LANES), 0)
    for sh in [1 << t for t in range(c.bit_length()) if (1 << t) < c]:
        fcum_all = fcum_all + jnp.where(row_l >= sh, pltpu.roll(fcum_all, sh, 0), 0.0)
    gates_t = jnp.where(lane < ML_HEADS, ifg, fcum_all).T
    new_cps, new_nps, new_mbs = [], [], []
    for j in range(ML_HEADS // 2):
        ps = slice(j * LANES, (j + 1) * LANES)
        q_s, k_s, v_s = mq[:, ps], mk[:, ps], mv[:, ps]
        acc = jnp.zeros((c, 2 * LANES), F32)
        per_head = []
        for hl in range(2):
            h = 2 * j + hl
            own = low if hl == 0 else jnp.logical_not(low)
            fc = jnp.broadcast_to(fcum_all[:, ML_HEADS + h:ML_HEADS + h + 1], (c, LANES))
            ii = jnp.broadcast_to(ifg[:, h:h + 1], (c, LANES))
            dm = jnp.where(tri, wide(fc) - gates_t[ML_HEADS + h:ML_HEADS + h + 1, :] + gates_t[h:h + 1, :], NEG_BIG)
            inter = fc + mbs[h]
            m = jnp.maximum(inter, jnp.max(dm, axis=-1, keepdims=True))
            sij = _dot_nt(jnp.where(own, q_s, 0.0), k_s) * jnp.exp(dm - wide(m))
            acc = acc + _dot(sij, jnp.concatenate([jnp.where(own, v_s, 0.0), jnp.where(own, 1.0, 0.0)], axis=1))
            m_last = m[c - 1:c]
            f_last = fc[c - 1:c]
            per_head.append((m, jnp.exp(inter - m), jnp.exp(f_last - fc + ii - m_last),
                             jnp.exp(f_last + mbs[h] - m_last)))
            new_mbs.append(m_last)
        m_p, w_p, wj_p = (jnp.where(low, per_head[0][t], per_head[1][t]) for t in range(3))
        dec_row = jnp.where(low_row, per_head[0][3], per_head[1][3])
        n_mat = jnp.where(same_head, jnp.broadcast_to(nps[j], (LANES, LANES)), 0.0)
        num = w_p * _dot_nt(q_s, cps[j]) + acc[:, 0:LANES]
        den = w_p * _dot_nt(q_s, n_mat) + acc[:, LANES:2 * LANES]
        hh = num / jnp.maximum(jnp.abs(den), jnp.exp(-m_p))
        hn = hh * lax.rsqrt(_dot_split(hh * hh, gm_ref[...]) + EPS) * ng_ref[...]
        o_ref[i, :, ps] = hn * og[:, ps] * zz[:, ps]
        dec_mat = jnp.where(sq_row < ML_DH, jnp.broadcast_to(per_head[0][3], (LANES, LANES)),
                            jnp.broadcast_to(per_head[1][3], (LANES, LANES)))
        new_cps.append(dec_mat * cps[j] + jnp.where(same_head, _dot_tn(wj_p * v_s, k_s), 0.0))
        new_nps.append(dec_row * nps[j] + jnp.sum(wj_p * k_s, axis=0, keepdims=True))
    return new_cps, new_nps, new_mbs


ML_SEQS_PER_STEP = 2


def _mlstm(u_ml, c0, n0, m0, conv0, cw, cb, gb, ng, b, l):
    tl = min(l, 256)
    bb = math.gcd(b, ML_SEQS_PER_STEP)
    pairs = ML_HEADS // 2
    w2 = 2 * ML_WIDTH
    u3 = u_ml.reshape(b, l, W_ML)
    eye2 = jnp.eye(2, dtype=F32)
    c0p = jnp.einsum('bphed,hk->bphekd', c0.astype(F32).reshape(b, pairs, 2, ML_DH, ML_DH), eye2)
    c0p = c0p.reshape(b, pairs, LANES, LANES)
    n0p = n0.astype(F32).reshape(b, pairs, LANES)
    m0b = jnp.broadcast_to(m0.astype(F32)[:, :, None], (b, ML_HEADS, LANES))
    cv0 = jnp.pad(conv0.astype(F32), ((0, 0), (SUBLANES - (CONV_W - 1), 0), (0, 0)))
    gbp = jnp.pad(gb.astype(F32).reshape(1, 2 * ML_HEADS), ((0, 0), (0, LANES - 2 * ML_HEADS)))
    kern = functools.partial(_mlstm_kernel, bb=bb, tl=tl)
    st = lambda bi, li: (bi, 0, 0)
    st4 = lambda bi, li: (bi, 0, 0, 0)
    cst = lambda bi, li: (0, 0)
    o, c_new, n_new, m_new, cv = pl.pallas_call(
        kern,
        grid=(b // bb, l // tl),
        in_specs=[pl.BlockSpec((bb, tl, W_ML), lambda bi, li: (bi, li, 0)),
                  pl.BlockSpec((bb, pairs, LANES, LANES), st4),
                  pl.BlockSpec((bb, pairs, LANES), st),
                  pl.BlockSpec((bb, ML_HEADS, LANES), st),
                  pl.BlockSpec((bb, SUBLANES, w2), st),
                  pl.BlockSpec((CONV_W, w2), cst),
                  pl.BlockSpec((1, w2), cst),
                  pl.BlockSpec((1, LANES), cst),
                  pl.BlockSpec((1, LANES), cst),
                  pl.BlockSpec((LANES, LANES), cst)],
        out_specs=[pl.BlockSpec((bb, tl, ML_WIDTH), lambda bi, li: (bi, li, 0)),
                   pl.BlockSpec((bb, pairs, LANES, LANES), st4),
                   pl.BlockSpec((bb, pairs, LANES), st),
                   pl.BlockSpec((bb, ML_HEADS, LANES), st),
                   pl.BlockSpec((bb, SUBLANES, w2), st)],
        out_shape=[jax.ShapeDtypeStruct((b, l, ML_WIDTH), F32),
                   jax.ShapeDtypeStruct((b, pairs, LANES, LANES), F32),
                   jax.ShapeDtypeStruct((b, pairs, LANES), F32),
                   jax.ShapeDtypeStruct((b, ML_HEADS, LANES), F32),
                   jax.ShapeDtypeStruct((b, SUBLANES, w2), F32)],
        scratch_shapes=[pltpu.VMEM((bb, tl + 2 * SUBLANES, w2), F32)],
        compiler_params=_cparams("parallel", "arbitrary"),
        name="mlstm",
    )(u3, c0p, n0p, m0b, cv0, cw.astype(F32), cb.astype(F32).reshape(1, w2), gbp,
      jnp.tile(ng.astype(F32).reshape(1, ML_DH), (1, 2)), _group_mean_matrix(LANES))
    c6 = c_new.reshape(b, pairs, 2, ML_DH, 2, ML_DH)
    c_out = jnp.stack([c6[:, :, h, :, h, :] for h in range(2)], axis=2).reshape(b, ML_HEADS, ML_DH, ML_DH)
    return (o, c_out, n_new.reshape(b, ML_HEADS, ML_DH), m_new[:, :, 0], cv[:, SUBLANES - (CONV_W - 1):, :])


def _rope_lanes(x, cos_t, sin_t):
    w = x.shape[1]
    half = ROT_DIM // 2
    reps = w // cos_t.shape[1]
    if reps > 1:
        cos_t = jnp.concatenate([cos_t] * reps, axis=1)
        sin_t = jnp.concatenate([sin_t] * reps, axis=1)
    lane = lax.broadcasted_iota(jnp.int32, x.shape, 1) % HEAD_DIM
    partner = jnp.where(lane < half, pltpu.roll(x, w - half, 1), pltpu.roll(x, half, 1))
    return x * cos_t + partner * sin_t


def _nsa_prep_kernel(u_ref, cos_ref, sin_ref, qg_ref, kg_ref, g4_ref, g1_ref, qn_ref, qr_ref, rows_ref, win_ref):
    u = u_ref[...]
    cos_t = cos_ref[...]
    sin_t = sin_ref[...]
    q = _group_norm(u[:, 0:NSA_WIDTH], g4_ref[...], qg_ref[...])
    qn_ref[...] = (q * ATTN_SCALE).astype(BF16)
    qr = _rope_lanes(q, cos_t, sin_t) * (ATTN_SCALE * LOG2E)
    low = lax.broadcasted_iota(jnp.int32, (qr.shape[0], LANES), 1) < HEAD_DIM
    for j in range(NSA_HEADS // 2):
        pair = qr[:, j * LANES:(j + 1) * LANES]
        swapped = pltpu.roll(pair, HEAD_DIM, 1)
        qr_ref[2 * j] = jnp.where(low, pair, swapped).astype(BF16)
        qr_ref[2 * j + 1] = jnp.where(low, swapped, pair).astype(BF16)
    kv = NSA_WIDTH
    k_slc = _rope_lanes(_group_norm(u[:, kv + 2 * KV_W:kv + 3 * KV_W], g1_ref[...], kg_ref[1:2, :]), cos_t, sin_t)
    k_win = _rope_lanes(_group_norm(u[:, kv + 4 * KV_W:kv + 5 * KV_W], g1_ref[...], kg_ref[2:3, :]), cos_t, sin_t)
    rows_ref[:, 0:2 * KV_W] = u[:, kv:kv + 2 * KV_W]
    rows_ref[:, 2 * KV_W:3 * KV_W] = k_slc
    rows_ref[:, 3 * KV_W:4 * KV_W] = u[:, kv + 3 * KV_W:kv + 4 * KV_W]
    win_ref[:, 0:KV_W] = k_win
    win_ref[:, KV_W:2 * KV_W] = u[:, kv + 5 * KV_W:kv + 6 * KV_W]


def _rope_tables(pos):
    half = ROT_DIM // 2
    inv = jnp.exp(-math.log(ROPE_THETA) * jnp.arange(half, dtype=F32) * 2.0 / ROT_DIM)
    ang = pos.astype(F32)[:, None] * inv[None, :]
    cos, sin = jnp.cos(ang), jnp.sin(ang)
    n = pos.shape[0]
    ones = jnp.ones((n, HEAD_DIM - ROT_DIM), F32)
    cos_h = jnp.concatenate([cos, cos, ones], axis=1)
    sin_h = jnp.concatenate([-sin, sin, 0.0 * ones], axis=1)
    reps = LANES // HEAD_DIM
    return jnp.tile(cos_h, (1, reps)), jnp.tile(sin_h, (1, reps))


def _nsa_prep(u_nsa, pos, q_g, k_g, l):
    rows = u_nsa.shape[0]
    tl = min(l, PROJ_ROWS)
    nb = l // tl
    cos_t, sin_t = _rope_tables(pos)
    qg = jnp.tile(q_g.astype(F32).reshape(1, HEAD_DIM), (1, NSA_HEADS))
    kg = jnp.tile(k_g.astype(F32), (1, NSA_KV_HEADS))
    cst = lambda i: (0, 0)
    return pl.pallas_call(
        _nsa_prep_kernel,
        grid=(rows // tl,),
        in_specs=[pl.BlockSpec((tl, W_NSA), lambda i: (i, 0)),
                  pl.BlockSpec((tl, LANES), lambda i: (i % nb, 0)),
                  pl.BlockSpec((tl, LANES), lambda i: (i % nb, 0)),
                  pl.BlockSpec((1, NSA_WIDTH), cst),
                  pl.BlockSpec((3, KV_W), cst),
                  pl.BlockSpec((NSA_WIDTH, NSA_WIDTH), cst),
                  pl.BlockSpec((KV_W, KV_W), cst)],
        out_specs=[pl.BlockSpec((tl, NSA_WIDTH), lambda i: (i, 0)),
                   pl.BlockSpec((NSA_HEADS, tl, LANES), lambda i: (0, i, 0)),
                   pl.BlockSpec((tl, N_KV_SLOTS * KV_W), lambda i: (i, 0)),
                   pl.BlockSpec((tl, 2 * KV_W), lambda i: (i, 0))],
        out_shape=[jax.ShapeDtypeStruct((rows, NSA_WIDTH), BF16),
                   jax.ShapeDtypeStruct((NSA_HEADS, rows, LANES), BF16),
                   jax.ShapeDtypeStruct((rows, N_KV_SLOTS * KV_W), F32),
                   jax.ShapeDtypeStruct((rows, 2 * KV_W), F32)],
        compiler_params=_cparams("parallel"),
        name="nsa_prep",
    )(u_nsa, cos_t, sin_t, qg, kg, _group_mean_matrix(NSA_WIDTH), _group_mean_matrix(KV_W))


def _cmp_halves(xk_ref, xv_ref, w_ref, n_half):
    acc_k = jnp.zeros((n_half, 2 * KV_W), F32)
    acc_v = jnp.zeros((n_half, 2 * KV_W), F32)
    for s in range(CMP_STRIDE):
        acc_k = acc_k + _dot(xk_ref[pl.ds(s, n_half, stride=CMP_STRIDE), :], w_ref[0, s])
        acc_v = acc_v + _dot(xv_ref[pl.ds(s, n_half, stride=CMP_STRIDE), :], w_ref[1, s])
    return acc_k, acc_v


def _cmp_finish(acc_k, acc_v, bias_ref, kg_ref, g1_ref, kc_ref, vc_ref, n_half):
    valid = lax.broadcasted_iota(jnp.int32, (n_half, KV_W), 0) < n_half - 1

    def summary(acc, bias):
        return acc[:, 0:KV_W] + pltpu.roll(acc[:, KV_W:2 * KV_W], n_half - 1, 0) + bias

    kc = _group_norm(summary(acc_k, bias_ref[0:1, :]), g1_ref[...], kg_ref[...])
    kc_ref[0] = jnp.where(valid, kc, 0.0).astype(BF16)
    vc_ref[0] = jnp.where(valid, summary(acc_v, bias_ref[1:2, :]), 0.0).astype(BF16)


def _cmp_kv_kernel(xk_ref, xv_ref, w_ref, bias_ref, kg_ref, g1_ref, kc_ref, vc_ref, *, n_half):
    acc_k, acc_v = _cmp_halves(xk_ref.at[0], xv_ref.at[0], w_ref, n_half)
    _cmp_finish(acc_k, acc_v, bias_ref, kg_ref, g1_ref, kc_ref, vc_ref, n_half)


PAGES_PER_STEP = 64
CMP_PAGE_GROUP = 16


def _cmp_paged_kernel(pt_ref, *refs, npg, page, whole):
    page_refs = refs[:npg]
    ng = -(-npg // CMP_PAGE_GROUP)
    scratch = refs[len(refs) - 2 * ng:]
    if whole:
        w_ref, bias_ref, kg_ref, g1_ref, kc_ref, vc_ref = refs[npg:len(refs) - 2 * ng]
    else:
        w_ref, acck_ref, accv_ref = refs[npg:len(refs) - 2 * ng]
    accs = []
    for g in range(ng):
        xk_sc, xv_sc = scratch[2 * g], scratch[2 * g + 1]
        pages = range(g * CMP_PAGE_GROUP, min((g + 1) * CMP_PAGE_GROUP, npg))
        for j, i in enumerate(pages):
            xk_sc[j * page:(j + 1) * page, :] = page_refs[i][0, 0:KV_W, :].T
            xv_sc[j * page:(j + 1) * page, :] = page_refs[i][0, KV_W:2 * KV_W, :].T
        accs.append(_cmp_halves(xk_sc, xv_sc, w_ref, len(pages) * page // CMP_STRIDE))
    acc_k = accs[0][0] if ng == 1 else jnp.concatenate([a[0] for a in accs], axis=0)
    acc_v = accs[0][1] if ng == 1 else jnp.concatenate([a[1] for a in accs], axis=0)
    n_half = npg * page // CMP_STRIDE
    if whole:
        _cmp_finish(acc_k, acc_v, bias_ref, kg_ref, g1_ref, kc_ref, vc_ref, n_half)
    else:
        acck_ref[0] = acc_k
        accv_ref[0] = acc_v


def _cmp_page_scratch(npg, page):
    sizes = [min(CMP_PAGE_GROUP, npg - g) for g in range(0, npg, CMP_PAGE_GROUP)]
    return [pltpu.VMEM((n * page, KV_W), F32) for n in sizes for _ in range(2)]


def _cmp_fin_kernel(acck_ref, accv_ref, bias_ref, kg_ref, g1_ref, kc_ref, vc_ref, *, n_half):
    _cmp_finish(acck_ref[0], accv_ref[0], bias_ref, kg_ref, g1_ref, kc_ref, vc_ref, n_half)


def _page_specs(npg, page, row_blk, pool_off):
    def spec(i):
        return pl.BlockSpec((1, 2 * KV_W, page), lambda bi, ji, pt: (pt[bi, ji * npg + i] + pool_off, row_blk, 0))
    return [spec(i) for i in range(npg)]


def _cmp_kv_paged(cache_t, page_table, pool_off, wcat, bias, kg0):
    b, n_pages = page_table.shape
    page = cache_t.shape[2]
    npg = math.gcd(n_pages, PAGES_PER_STEP)
    nh_step = npg * page // CMP_STRIDE
    n_half = n_pages * page // CMP_STRIDE
    kg = jnp.tile(kg0.astype(F32).reshape(1, HEAD_DIM), (1, NSA_KV_HEADS))
    if npg == n_pages:
        cst = lambda bi, ji, pt: (0, 0)
        return pl.pallas_call(
            functools.partial(_cmp_paged_kernel, npg=npg, page=page, whole=True),
            grid_spec=pltpu.PrefetchScalarGridSpec(
                num_scalar_prefetch=1, grid=(b, 1),
                in_specs=_page_specs(npg, page, 0, pool_off)
                + [pl.BlockSpec((2, CMP_STRIDE, KV_W, 2 * KV_W), lambda bi, ji, pt: (0, 0, 0, 0)),
                   pl.BlockSpec((2, KV_W), cst), pl.BlockSpec((1, KV_W), cst), pl.BlockSpec((KV_W, KV_W), cst)],
                out_specs=[pl.BlockSpec((1, n_half, KV_W), lambda bi, ji, pt: (bi, 0, 0)),
                           pl.BlockSpec((1, n_half, KV_W), lambda bi, ji, pt: (bi, 0, 0))],
                scratch_shapes=_cmp_page_scratch(npg, page)),
            out_shape=[jax.ShapeDtypeStruct((b, n_half, KV_W), BF16),
                       jax.ShapeDtypeStruct((b, n_half, KV_W), BF16)],
            compiler_params=_cparams("parallel", "arbitrary"),
            name="cmp_paged",
        )(page_table, *([cache_t] * npg), wcat, bias, kg, _group_mean_matrix(KV_W))
    kern = functools.partial(_cmp_paged_kernel, npg=npg, page=page, whole=False)
    acc_k, acc_v = pl.pallas_call(
        kern,
        grid_spec=pltpu.PrefetchScalarGridSpec(
            num_scalar_prefetch=1, grid=(b, n_pages // npg),
            in_specs=_page_specs(npg, page, 0, pool_off)
            + [pl.BlockSpec((2, CMP_STRIDE, KV_W, 2 * KV_W), lambda bi, ji, pt: (0, 0, 0, 0))],
            out_specs=[pl.BlockSpec((1, nh_step, 2 * KV_W), lambda bi, ji, pt: (bi, ji, 0)),
                       pl.BlockSpec((1, nh_step, 2 * KV_W), lambda bi, ji, pt: (bi, ji, 0))],
            scratch_shapes=_cmp_page_scratch(npg, page)),
        out_shape=[jax.ShapeDtypeStruct((b, n_half, 2 * KV_W), F32),
                   jax.ShapeDtypeStruct((b, n_half, 2 * KV_W), F32)],
        compiler_params=_cparams("parallel", "arbitrary"),
        name="cmp_paged",
    )(page_table, *([cache_t] * npg), wcat)
    blk = lambda bi: (bi, 0, 0)
    return pl.pallas_call(
        functools.partial(_cmp_fin_kernel, n_half=n_half),
        grid=(b,),
        in_specs=[pl.BlockSpec((1, n_half, 2 * KV_W), blk),
                  pl.BlockSpec((1, n_half, 2 * KV_W), blk),
                  pl.BlockSpec((2, KV_W), lambda bi: (0, 0)),
                  pl.BlockSpec((1, KV_W), lambda bi: (0, 0)),
                  pl.BlockSpec((KV_W, KV_W), lambda bi: (0, 0))],
        out_specs=[pl.BlockSpec((1, n_half, KV_W), blk), pl.BlockSpec((1, n_half, KV_W), blk)],
        out_shape=[jax.ShapeDtypeStruct((b, n_half, KV_W), BF16),
                   jax.ShapeDtypeStruct((b, n_half, KV_W), BF16)],
        compiler_params=_cparams("parallel"),
        name="cmp_fin",
    )(acc_k, acc_v, bias, kg, _group_mean_matrix(KV_W))


def _cmp_weights(cmp_pos, cmp_w):
    wf = cmp_w.astype(F32)
    eye_g = jnp.eye(NSA_KV_HEADS, dtype=F32)

    def bd(w):
        return jnp.einsum('ksde,gh->ksgdhe', w, eye_g).reshape(2, CMP_STRIDE, KV_W, KV_W)

    wcat = jnp.concatenate([bd(wf[:, :CMP_STRIDE]), bd(wf[:, CMP_STRIDE:])], axis=3).astype(BF16)
    bias = jnp.einsum('ksd,ksde->ke', cmp_pos.astype(F32), wf)
    return wcat, jnp.tile(bias, (1, NSA_KV_HEADS))


def _cmp_kv(x3, k_blk, v_blk, t_use, wcat, bias, kg0):
    b = x3.shape[0]
    n_half = t_use // CMP_STRIDE
    kern = functools.partial(_cmp_kv_kernel, n_half=n_half)
    kg = jnp.tile(kg0.astype(F32).reshape(1, HEAD_DIM), (1, NSA_KV_HEADS))
    return pl.pallas_call(
        kern,
        grid=(b,),
        in_specs=[pl.BlockSpec((1, t_use, KV_W), lambda bi: (bi, 0, k_blk)),
                  pl.BlockSpec((1, t_use, KV_W), lambda bi: (bi, 0, v_blk)),
                  pl.BlockSpec((2, CMP_STRIDE, KV_W, 2 * KV_W), lambda bi: (0, 0, 0, 0)),
                  pl.BlockSpec((2, KV_W), lambda bi: (0, 0)),
                  pl.BlockSpec((1, KV_W), lambda bi: (0, 0)),
                  pl.BlockSpec((KV_W, KV_W), lambda bi: (0, 0))],
        out_specs=[pl.BlockSpec((1, n_half, KV_W), lambda bi: (bi, 0, 0)),
                   pl.BlockSpec((1, n_half, KV_W), lambda bi: (bi, 0, 0))],
        out_shape=[jax.ShapeDtypeStruct((b, n_half, KV_W), BF16),
                   jax.ShapeDtypeStruct((b, n_half, KV_W), BF16)],
        compiler_params=_cparams("parallel"),
        name="cmp_kv",
    )(x3, x3, wcat, bias, kg, _group_mean_matrix(KV_W))


def _cmp_attn_kernel(qn_ref, kc_ref, vc_ref, ov_ref, o_ref, sel_ref, *, bb, tq, n_half, n_cmp, n_sel, nbp, pos0,
                     bias_out):
    biases = []
    qi = pl.program_id(1)
    nq = bb * tq
    pos_c = pos0 + qi * tq + lax.broadcasted_iota(jnp.int32, (tq, 1), 0)
    ncol = lax.broadcasted_iota(jnp.int32, (1, n_half), 1)
    cmask = (ncol * CMP_STRIDE + (CMP_BLOCK - 1) <= pos_c) & (ncol < n_cmp)
    pos_r = pos0 + qi * tq + lax.broadcasted_iota(jnp.int32, (1, nq), 1) % tq
    blk = lax.broadcasted_iota(jnp.int32, (nbp, 1), 0)
    cur = pos_r // SEL_BLOCK
    forced = (blk == 0) | (blk == cur) | (blk == cur - 1)
    valid = blk * SEL_BLOCK <= pos_r
    real = blk < n_sel
    for g in range(NSA_KV_HEADS):
        gs = slice(g * HEAD_DIM, (g + 1) * HEAD_DIM)
        psums = []
        for i in range(bb):
            kc = kc_ref[i, :, gs]
            vc = vc_ref[i, :, gs]
            psum = jnp.zeros((tq, n_half), F32)
            for h in range(NSA_HPG):
                hs = slice((g * NSA_HPG + h) * HEAD_DIM, (g * NSA_HPG + h + 1) * HEAD_DIM)
                s = _dot_nt(qn_ref[i, :, hs], kc)
                m = jnp.max(jnp.where(cmask, s, NEG_BIG), axis=-1, keepdims=True)
                m = jnp.where(m > 0.5 * NEG_BIG, m, 0.0)
                e = jnp.where(cmask, jnp.exp(s - m), 0.0)
                p = e / jnp.maximum(jnp.sum(e, axis=-1, keepdims=True), 1e-30)
                o_ref[i, :, hs] = _dot(p, vc)
                psum = psum + p
            psums.append(psum)
        psum = psums[0] if bb == 1 else jnp.concatenate(psums, axis=0)
        p_hi = psum.astype(BF16)
        p_lo = (psum - p_hi.astype(F32)).astype(BF16)
        ov = ov_ref[...]
        imp = _dot_nt(ov, p_hi) + _dot_nt(ov, p_lo)
        score = jnp.where(forced, 3e38, jnp.where(valid, imp, -1e38))
        score = jnp.where(real, score, -3e38)
        cnt = jnp.zeros((nbp, nq), F32)
        for jp in range(n_sel):
            rowv = score[jp:jp + 1, :]
            beats = (rowv > score) | ((rowv == score) & (blk > jp))
            cnt = cnt + jnp.where(beats, 1.0, 0.0)
        sel = (cnt < float(min(N_SELECT, n_sel))) & real
        if bias_out:
            sbg = jnp.where(sel, 0.0, NEG_BIG)
            if nbp < HEAD_DIM:
                sbg = jnp.concatenate([sbg, jnp.full((HEAD_DIM - nbp, nq), NEG_BIG, F32)], axis=0)
            biases.append(sbg)
        else:
            sel_ref[0, g * nbp:(g + 1) * nbp, :] = jnp.where(sel, 1.0, 0.0).astype(BF16)
    if bias_out:
        sel_ref[0] = jnp.concatenate(biases[::-1], axis=0).T.astype(BF16)


def _overlap_t(n_cmp, n_sel, n_half, nbp):
    a = SEL_BLOCK // CMP_STRIDE
    bb = CMP_BLOCK // CMP_STRIDE
    i = np.arange(n_half)[None, :]
    j = np.arange(nbp)[:, None]
    s = i - a * j + (bb - 1)
    cnt = np.maximum(np.minimum(np.minimum(s + 1, a + bb - 1 - s), min(a, bb)), 0)
    cnt = np.where((i < n_cmp) & (j < n_sel), cnt, 0)
    return jnp.asarray(cnt, dtype=BF16)


CMP_TQ = 512


def _cmp_attn(qn3, kc, vc, pos0, t_len, bias_out=False):
    b, l, _ = qn3.shape
    n_half = kc.shape[1]
    n_cmp = n_half - 1
    n_sel = -(-t_len // SEL_BLOCK)
    nbp = _round_up(n_sel, 16)
    tq = min(l, CMP_TQ)
    bb = math.gcd(b, max(LANES // tq, 1))
    kern = functools.partial(_cmp_attn_kernel, bb=bb, tq=tq, n_half=n_half, n_cmp=n_cmp, n_sel=n_sel, nbp=nbp,
                             pos0=pos0, bias_out=bias_out)
    if bias_out:
        assert bb == 1 and NSA_KV_HEADS == 2 and nbp <= HEAD_DIM
        sel_spec = pl.BlockSpec((1, tq, LANES), lambda bi, qi: (bi, qi, 0))
        sel_shape = jax.ShapeDtypeStruct((b, l, LANES), BF16)
    else:
        sel_spec = pl.BlockSpec((1, NSA_KV_HEADS * nbp, bb * tq), lambda bi, qi: (bi, 0, qi))
        sel_shape = jax.ShapeDtypeStruct((b // bb, NSA_KV_HEADS * nbp, bb * l), BF16)
    o_cmp, sel_t = pl.pallas_call(
        kern,
        grid=(b // bb, l // tq),
        in_specs=[pl.BlockSpec((bb, tq, NSA_WIDTH), lambda bi, qi: (bi, qi, 0)),
                  pl.BlockSpec((bb, n_half, KV_W), lambda bi, qi: (bi, 0, 0)),
                  pl.BlockSpec((bb, n_half, KV_W), lambda bi, qi: (bi, 0, 0)),
                  pl.BlockSpec((nbp, n_half), lambda bi, qi: (0, 0))],
        out_specs=[pl.BlockSpec((bb, tq, NSA_WIDTH), lambda bi, qi: (bi, qi, 0)), sel_spec],
        out_shape=[jax.ShapeDtypeStruct((b, l, NSA_WIDTH), F32), sel_shape],
        compiler_params=_cparams("parallel", "parallel"),
        name="cmp_attn",
    )(qn3, kc, vc, _overlap_t(n_cmp, n_sel, n_half, nbp))
    if bb > 1:
        sel_t = sel_t.reshape(b // bb, NSA_KV_HEADS * nbp, bb, l)
        sel_t = jnp.swapaxes(sel_t, 1, 2).reshape(b, NSA_KV_HEADS * nbp, l)
    return o_cmp, sel_t


SEL_TQ = 128
SEL_TK = 2048
WIN_TQ = 256


def _group_lanes(shape, g):
    lane = lax.broadcasted_iota(jnp.int32, shape, len(shape) - 1)
    return (lane < HEAD_DIM) if g == 0 else (lane >= HEAD_DIM)


def _store_heads(o_ref, r, g, tq):
    lo = g * HEAD_DIM
    for h in range(NSA_HPG):
        hh = g * NSA_HPG + h
        o_ref[0, :, hh * HEAD_DIM:(hh + 1) * HEAD_DIM] = r[h * tq:(h + 1) * tq, lo:lo + HEAD_DIM]


def _flash_sel_kernel(qi_ref, ki_ref, q_ref, k_ref, v_ref, sb_ref, o_ref, m_sc, acc_sc, *, tq, tk):
    qi = qi_ref[pl.program_id(1)]
    ki = ki_ref[pl.program_id(1)]
    kmax = (qi * tq + tq - 1) // tk
    hq = NSA_HPG * tq

    @pl.when(ki == 0)
    def _init():
        m_sc[...] = jnp.full(m_sc.shape, NEG_BIG, F32)
        acc_sc[...] = jnp.zeros(acc_sc.shape, F32)

    def step(diagonal):
        key_blk = ki * (tk // SEL_BLOCK) + lax.broadcasted_iota(jnp.int32, (tk, LANES), 0) // SEL_BLOCK
        one_hot = jnp.where(lax.broadcasted_iota(jnp.int32, (tk, LANES), 1) % HEAD_DIM == key_blk, 1.0, 0.0).astype(BF16)
        kblk = k_ref[0].astype(BF16)
        vblk = v_ref[0].astype(BF16)
        sb4 = jnp.concatenate([sb_ref[0]] * NSA_HPG, axis=0)
        if diagonal:
            qpos = qi * tq + lax.broadcasted_iota(jnp.int32, (tq, tk), 0)
            kpos = ki * tk + lax.broadcasted_iota(jnp.int32, (tq, tk), 1)
            causal4 = jnp.concatenate([jnp.where(kpos <= qpos, 0.0, NEG_BIG)] * NSA_HPG, axis=0)
        m_prev = [m_sc[g] for g in range(NSA_KV_HEADS)]
        acc_prev = [acc_sc[g] for g in range(NSA_KV_HEADS)]
        m_out, acc_out = [], []
        for g in range(NSA_KV_HEADS):
            own_k = _group_lanes((tk, LANES), g)
            q4 = q_ref[g * NSA_HPG:(g + 1) * NSA_HPG].reshape(hq, LANES)
            qx = jnp.where(_group_lanes((hq, LANES), g), q4, sb4)
            kx = jnp.where(own_k, kblk, one_hot)
            s = lax.dot_general(qx, kx, (((1,), (1,)), ((), ())), preferred_element_type=F32)
            if diagonal:
                s = s + causal4
            m_new = jnp.maximum(m_prev[g], jnp.max(s, axis=-1, keepdims=True))
            p = jnp.exp2(s - m_new[:, :1]).astype(BF16)
            vx = jnp.where(own_k, vblk, 1.0)
            acc_out.append(jnp.exp2(m_prev[g] - m_new) * acc_prev[g] + jnp.dot(p, vx, preferred_element_type=F32))
            m_out.append(m_new)
        for g in range(NSA_KV_HEADS):
            m_sc[g] = m_out[g]
            acc_sc[g] = acc_out[g]

    @pl.when(ki < kmax)
    def _full():
        step(False)

    @pl.when(ki == kmax)
    def _last():
        step(True)
        for g in range(NSA_KV_HEADS):
            acc = acc_sc[g]
            _store_heads(o_ref, acc / pltpu.roll(acc, HEAD_DIM, 1), g, tq)


def _flash_sel(q8, rows3, selb):
    b, l, _ = rows3.shape
    tq, tk = min(SEL_TQ, l), min(SEL_TK, l)
    nq = l // tq
    pairs = [(qi, ki) for qi in range(nq) for ki in range((qi * tq + tq - 1) // tk + 1)]
    qi_tab = jnp.asarray([p[0] for p in pairs], jnp.int32)
    ki_tab = jnp.asarray([p[1] for p in pairs], jnp.int32)
    kv_idx = lambda blk: (lambda bi, si, qt, kt: (bi, kt[si], blk))
    return pl.pallas_call(
        functools.partial(_flash_sel_kernel, tq=tq, tk=tk),
        grid_spec=pltpu.PrefetchScalarGridSpec(
            num_scalar_prefetch=2, grid=(b, len(pairs)),
            in_specs=[pl.BlockSpec((NSA_HEADS, tq, LANES), lambda bi, si, qt, kt: (0, bi * nq + qt[si], 0)),
                      pl.BlockSpec((1, tk, KV_W), kv_idx(2)),
                      pl.BlockSpec((1, tk, KV_W), kv_idx(3)),
                      pl.BlockSpec((1, tq, LANES), lambda bi, si, qt, kt: (bi, qt[si], 0))],
            out_specs=pl.BlockSpec((1, tq, NSA_WIDTH), lambda bi, si, qt, kt: (bi, qt[si], 0)),
            scratch_shapes=[pltpu.VMEM((NSA_KV_HEADS, NSA_HPG * tq, LANES), F32),
                            pltpu.VMEM((NSA_KV_HEADS, NSA_HPG * tq, LANES), F32)]),
        out_shape=jax.ShapeDtypeStruct((b, l, NSA_WIDTH), F32),
        compiler_params=_cparams("parallel", "arbitrary"),
        name="flash_sel",
    )(qi_tab, ki_tab, q8, rows3, rows3, selb)


def _flash_win_kernel(q_ref, *refs, tq, back):
    nspan = back + 1
    k_refs, v_refs, o_ref = refs[:nspan], refs[nspan:2 * nspan], refs[2 * nspan]
    qi = pl.program_id(1)
    hq = NSA_HPG * tq
    span = nspan * tq
    qpos = qi * tq + lax.broadcasted_iota(jnp.int32, (tq, span), 0)
    kpos = (qi - back) * tq + lax.broadcasted_iota(jnp.int32, (tq, span), 1)
    ok = (kpos <= qpos) & (kpos > qpos - WINDOW) & (kpos >= 0)
    bias4 = jnp.concatenate([jnp.where(ok, 0.0, NEG_BIG)] * NSA_HPG, axis=0)
    kcat = jnp.concatenate([r[0] for r in k_refs], axis=0).astype(BF16)
    vcat = jnp.concatenate([r[0] for r in v_refs], axis=0).astype(BF16)
    for g in range(NSA_KV_HEADS):
        q4 = q_ref[g * NSA_HPG:(g + 1) * NSA_HPG].reshape(hq, LANES)
        qx = jnp.where(_group_lanes((hq, LANES), g), q4, 0.0)
        s = lax.dot_general(qx, kcat, (((1,), (1,)), ((), ())), preferred_element_type=F32) + bias4
        p = jnp.exp2(s - jnp.max(s, axis=-1, keepdims=True)).astype(BF16)
        vx = jnp.where(_group_lanes((span, LANES), g), vcat, 1.0)
        r = jnp.dot(p, vx, preferred_element_type=F32)
        _store_heads(o_ref, r / pltpu.roll(r, HEAD_DIM, 1), g, tq)


def _flash_win(q8, win3):
    b, l, _ = win3.shape
    tq = min(WIN_TQ, l)
    nq = l // tq
    back = -(-(WINDOW - 1) // tq)

    def kv_specs(blk):
        return [pl.BlockSpec((1, tq, KV_W), (lambda bi, qi, j=j: (bi, jnp.maximum(qi - back + j, 0), blk)))
                for j in range(back + 1)]

    return pl.pallas_call(
        functools.partial(_flash_win_kernel, tq=tq, back=back),
        grid=(b, nq),
        in_specs=[pl.BlockSpec((NSA_HEADS, tq, LANES), lambda bi, qi: (0, bi * nq + qi, 0))]
        + kv_specs(0) + kv_specs(1),
        out_specs=pl.BlockSpec((1, tq, NSA_WIDTH), lambda bi, qi: (bi, qi, 0)),
        out_shape=jax.ShapeDtypeStruct((b, l, NSA_WIDTH), F32),
        compiler_params=_cparams("parallel", "parallel"),
        name="flash_win",
    )(q8, *([win3] * (2 * (back + 1))))


def _softmax_update(s, m_prev, l_prev):
    m_new = jnp.maximum(m_prev, jnp.max(s, axis=-1, keepdims=True))
    alpha = jnp.exp2(m_prev - m_new)
    p = jnp.exp2(s - m_new[:, :1])
    return m_new, alpha, alpha * l_prev + jnp.sum(p, axis=-1, keepdims=True), p


def _paged_attn_kernel(pt_ref, *refs, npg, page, l_new, nbp):
    page_refs = refs[:npg]
    (qz_ref, sel_ref, kn_ref, vn_ref, win_ref, kwn_ref, vwn_ref,
     oslc_ref, owin_ref, wout_ref, m_sc, l_sc, acc_sc) = refs[npg:]
    ji = pl.program_id(1)
    nr = qz_ref.shape[1]
    span = npg * page

    @pl.when(ji == 0)
    def _init():
        m_sc[...] = jnp.full(m_sc.shape, NEG_BIG, F32)
        l_sc[...] = jnp.zeros(l_sc.shape, F32)
        acc_sc[...] = jnp.zeros(acc_sc.shape, F32)

    qz = qz_ref[0]
    kt = jnp.concatenate([page_refs[i][0, 0:KV_W, :] for i in range(npg)], axis=1).astype(BF16)
    vt = jnp.concatenate([page_refs[i][0, KV_W:2 * KV_W, :] for i in range(npg)], axis=1).astype(BF16)
    s = jnp.dot(qz, kt, preferred_element_type=F32)
    e_row = lax.broadcasted_iota(jnp.int32, (nbp, span), 0)
    e_col = lax.broadcasted_iota(jnp.int32, (nbp, span), 1)
    expand = jnp.where(e_row == ji * (span // SEL_BLOCK) + e_col // SEL_BLOCK, 1.0, 0.0).astype(BF16)
    picked = jnp.dot(sel_ref[0], expand, preferred_element_type=F32) > 0.5
    m_new, alpha, l_new_v, p = _softmax_update(jnp.where(picked, s, NEG_BIG), m_sc[...], l_sc[...])
    m_sc[...] = m_new
    l_sc[...] = l_new_v
    acc_sc[...] = alpha * acc_sc[...] + _dot_nt(p, vt)

    @pl.when(ji == pl.num_programs(1) - 1)
    def _fin():
        q_of_row = lax.broadcasted_iota(jnp.int32, (nr, l_new), 0) % l_new
        j_new = lax.broadcasted_iota(jnp.int32, (nr, l_new), 1)
        new_ok = j_new <= q_of_row
        sn = jnp.where(new_ok, _dot_nt(qz, kn_ref[0]), NEG_BIG)
        m2, a2, l2, p2 = _softmax_update(sn, m_sc[...], l_sc[...])
        oslc_ref[0] = (a2 * acc_sc[...] + _dot(p2, vn_ref[0])) / l2
        wlen = win_ref.shape[2]
        kw = win_ref[0, 0:KV_W, :]
        vw = win_ref[0, KV_W:2 * KV_W, :]
        i_old = lax.broadcasted_iota(jnp.int32, (nr, wlen), 1)
        q_old = lax.broadcasted_iota(jnp.int32, (nr, wlen), 0) % l_new
        sw = jnp.where(i_old + (WINDOW - wlen) > q_old, _dot(qz, kw), NEG_BIG)
        swn = jnp.where(new_ok, _dot_nt(qz, kwn_ref[0]), NEG_BIG)
        mw = jnp.maximum(jnp.max(sw, axis=-1, keepdims=True), jnp.max(swn, axis=-1, keepdims=True))
        pw = jnp.exp2(sw - mw)
        pwn = jnp.exp2(swn - mw)
        lw = jnp.sum(pw, axis=-1, keepdims=True) + jnp.sum(pwn, axis=-1, keepdims=True)
        owin_ref[0] = (_dot_nt(pw, vw) + _dot(pwn, vwn_ref[0])) / lw
        new_t = jnp.concatenate([kwn_ref[0], vwn_ref[0]], axis=1).T
        place = (lax.broadcasted_iota(jnp.int32, (l_new, wlen), 1)
                 == lax.broadcasted_iota(jnp.int32, (l_new, wlen), 0) + (wlen - l_new)).astype(F32)
        lane = lax.broadcasted_iota(jnp.int32, (2 * KV_W, wlen), 1)
        wout_ref[0] = jnp.where(lane < wlen - l_new, pltpu.roll(win_ref[0], wlen - l_new, 1), _dot_f32(new_t, place))


def _paged_attn(cache_t, page_table, pool_off, qz, sel_rows, rows3, win_t, win_off, win3):
    b, n_pages = page_table.shape
    page = cache_t.shape[2]
    npg = math.gcd(n_pages, PAGES_PER_STEP)
    nr = qz.shape[1]
    l_new = rows3.shape[1]
    nbp = sel_rows.shape[2]
    wlen = win_t.shape[2]
    kern = functools.partial(_paged_attn_kernel, npg=npg, page=page, l_new=l_new, nbp=nbp)
    per_b = lambda bi, ji, pt: (bi, 0, 0)
    return pl.pallas_call(
        kern,
        grid_spec=pltpu.PrefetchScalarGridSpec(
            num_scalar_prefetch=1, grid=(b, n_pages // npg),
            in_specs=_page_specs(npg, page, 1, pool_off)
            + [pl.BlockSpec((1, nr, KV_W), per_b),
               pl.BlockSpec((1, nr, nbp), per_b),
               pl.BlockSpec((1, l_new, KV_W), lambda bi, ji, pt: (bi, 0, 2)),
               pl.BlockSpec((1, l_new, KV_W), lambda bi, ji, pt: (bi, 0, 3)),
               pl.BlockSpec((1, 2 * KV_W, wlen), lambda bi, ji, pt: (bi + win_off, 0, 0)),
               pl.BlockSpec((1, l_new, KV_W), lambda bi, ji, pt: (bi, 0, 0)),
               pl.BlockSpec((1, l_new, KV_W), lambda bi, ji, pt: (bi, 0, 1))],
            out_specs=[pl.BlockSpec((1, nr, KV_W), per_b),
                       pl.BlockSpec((1, nr, KV_W), per_b),
                       pl.BlockSpec((1, 2 * KV_W, wlen), per_b)],
            scratch_shapes=[pltpu.VMEM((nr, LANES), F32), pltpu.VMEM((nr, LANES), F32),
                            pltpu.VMEM((nr, KV_W), F32)]),
        out_shape=[jax.ShapeDtypeStruct((b, nr, KV_W), F32),
                   jax.ShapeDtypeStruct((b, nr, KV_W), F32),
                   jax.ShapeDtypeStruct((b, 2 * KV_W, wlen), F32)],
        compiler_params=_cparams("parallel", "arbitrary"),
        name="paged_attn",
    )(page_table, *([cache_t] * npg), qz, sel_rows, rows3, rows3, win_t, win3, win3)


def _nsa_fresh(u_nsa, b, l, win_keep, q_g, k_g, cmp_pos, cmp_w):
    qn, qr, rows, win = _nsa_prep(u_nsa, jnp.arange(l), q_g, k_g, l)
    rows3 = rows.reshape(b, l, N_KV_SLOTS * KV_W)
    win3 = win.reshape(b, l, 2 * KV_W)
    wcat, bias = _cmp_weights(cmp_pos, cmp_w)
    t_use = (l // CMP_STRIDE) * CMP_STRIDE
    kc, vc = _cmp_kv(rows3, 0, 1, t_use, wcat, bias, k_g[0])
    o_cmp, selb = _cmp_attn(qn.reshape(b, l, NSA_WIDTH), kc, vc, 0, l, bias_out=True)
    o_slc = _flash_sel(qr, rows3, selb)
    o_win = _flash_win(qr, win3)
    branches = tuple(o.reshape(b * l, NSA_WIDTH) for o in (o_cmp, o_slc, o_win))
    new_rows = rows3.reshape(b, l, N_KV_SLOTS, NSA_KV_HEADS, HEAD_DIM)
    new_win = win3[:, l - win_keep:].reshape(b, win_keep, 2, NSA_KV_HEADS, HEAD_DIM)
    return branches, new_rows, new_win


def _nsa_paged(u_nsa, b, l, paged, q_g, k_g, cmp_pos, cmp_w):
    cache_t, page_table, pool_off, win_t, win_off = paged
    past_len = page_table.shape[1] * cache_t.shape[2]
    assert (past_len + l) // CMP_STRIDE == past_len // CMP_STRIDE and past_len % SEL_BLOCK == 0
    qn, qr, rows, win = _nsa_prep(u_nsa, past_len + jnp.arange(l), q_g, k_g, l)
    rows3 = rows.reshape(b, l, N_KV_SLOTS * KV_W)
    win3 = win.reshape(b, l, 2 * KV_W)
    wcat, bias = _cmp_weights(cmp_pos, cmp_w)
    kc, vc = _cmp_kv_paged(cache_t, page_table, pool_off, wcat, bias, k_g[0])
    o_cmp, sel_t = _cmp_attn(qn.reshape(b, l, NSA_WIDTH), kc, vc, past_len, past_len + l)
    nbp = sel_t.shape[1] // NSA_KV_HEADS
    sel_rows = jnp.swapaxes(sel_t.reshape(b, NSA_KV_HEADS, 1, nbp, l), 3, 4)
    sel_rows = jnp.broadcast_to(sel_rows, (b, NSA_KV_HEADS, NSA_HPG, l, nbp)).reshape(b, NSA_HEADS * l, nbp)
    q5 = jnp.transpose(qr[:, :, :HEAD_DIM].reshape(NSA_KV_HEADS, NSA_HPG, b, l, HEAD_DIM), (2, 0, 1, 3, 4))
    qz = jnp.einsum('bghqd,gk->bghqkd', q5, jnp.eye(NSA_KV_HEADS, dtype=q5.dtype)).reshape(b, NSA_HEADS * l, KV_W)
    o_slc_z, o_win_z, wout = _paged_attn(cache_t, page_table, pool_off, qz, sel_rows, rows3, win_t, win_off, win3)

    def own_group(o):
        o6 = o.reshape(b, NSA_KV_HEADS, NSA_HPG, l, NSA_KV_HEADS, HEAD_DIM)
        d = jnp.stack([o6[:, g, :, :, g, :] for g in range(NSA_KV_HEADS)], axis=1)
        return jnp.transpose(d, (0, 3, 1, 2, 4)).reshape(b * l, NSA_WIDTH)

    branches = (o_cmp.reshape(b * l, NSA_WIDTH), own_group(o_slc_z), own_group(o_win_z))
    new_rows = rows3.reshape(b, l, N_KV_SLOTS, NSA_KV_HEADS, HEAD_DIM)
    wlen = wout.shape[2]
    new_win = jnp.transpose(wout.reshape(b, 2, NSA_KV_HEADS, HEAD_DIM, wlen), (0, 4, 1, 2, 3))
    return branches, new_rows, new_win


PROJ_ROWS = 512


def _hybrid_layer(x, paged, win_keep, s_gla, c_ml, n_ml, m_ml, conv_ml,
                  norm_g, w_in_pad, w_out_bf, gla_w_gate, gla_b_gate, gla_norm_g,
                  nsa_q_norm_g, nsa_k_norm_g, nsa_cmp_pos, nsa_cmp_w,
                  ml_conv_w, ml_conv_b, ml_gate_b, ml_norm_g):
    b, l, _ = x.shape
    rows = b * l
    tm = math.gcd(rows, PROJ_ROWS)
    x2d = x.reshape(rows, D_MODEL)
    u_gla, u_nsa, u_ml = _proj_in(x2d, norm_g, w_in_pad, math.gcd(rows, 2 * PROJ_ROWS))

    nk = GLA_HEADS * GLA_DK
    wg = jnp.pad(gla_w_gate.astype(F32), ((0, LANES - GLA_RANK), (0, 0))).astype(BF16)
    o_a, st_new = _gla(u_gla, _gla_state_in(s_gla), wg, gla_b_gate.astype(F32).reshape(1, nk),
                       jnp.tile(gla_norm_g.astype(F32).reshape(1, GLA_DV), (1, GLA_HEADS)), b, l)
    s_new = _gla_state_out(st_new)

    if paged is None:
        o_nsa, new_rows, new_win = _nsa_fresh(u_nsa, b, l, win_keep, nsa_q_norm_g, nsa_k_norm_g,
                                              nsa_cmp_pos, nsa_cmp_w)
    else:
        o_nsa, new_rows, new_win = _nsa_paged(u_nsa, b, l, paged, nsa_q_norm_g, nsa_k_norm_g,
                                              nsa_cmp_pos, nsa_cmp_w)

    o_c, c_new, n_new, m_new, conv_new = _mlstm(u_ml, c_ml, n_ml, m_ml, conv_ml, ml_conv_w, ml_conv_b,
                                                ml_gate_b, ml_norm_g, b, l)

    y = _proj_out(o_a.reshape(rows, GLA_WIDTH), *o_nsa, u_nsa, o_c.reshape(rows, ML_WIDTH), x2d, w_out_bf, tm)
    return y.reshape(b, l, D_MODEL), new_rows, new_win, s_new, c_new, n_new, m_new, conv_new


def kernel(x_prompt, x_sample, cache_nsa_kv, state_nsa_win, state_gla, state_mlstm_C, state_mlstm_n,
           state_mlstm_m, state_mlstm_conv, page_table, norm_g, w_in, w_out, gla_w_gate, gla_b_gate,
           gla_norm_g, nsa_q_norm_g, nsa_k_norm_g, nsa_cmp_pos, nsa_cmp_w, ml_conv_w, ml_conv_b,
           ml_gate_b, ml_norm_g):
    bp, sp, _ = x_prompt.shape
    bs, _, _ = x_sample.shape
    depth = w_in.shape[0]
    dt = x_prompt.dtype
    zero_gla = jnp.zeros((bp, GLA_HEADS, GLA_DK, GLA_DV), F32)
    zero_c = jnp.zeros((bp, ML_HEADS, ML_DH, ML_DH), F32)
    zero_n = jnp.zeros((bp, ML_HEADS, ML_DH), F32)
    zero_m = jnp.zeros((bp, ML_HEADS), F32)
    zero_conv = jnp.zeros((bp, CONV_W - 1, 2 * ML_WIDTH), dt)
    keep_p = min(WINDOW, sp)
    keep_s = state_nsa_win.shape[2]
    n_pool = cache_nsa_kv.shape[1]
    cache_t = jnp.transpose(cache_nsa_kv, (0, 1, 3, 4, 5, 2)).reshape(
        depth * n_pool, N_KV_SLOTS * KV_W, cache_nsa_kv.shape[2]).astype(F32)
    win_t = jnp.transpose(state_nsa_win, (0, 1, 3, 4, 5, 2)).reshape(depth * bs, 2 * KV_W, keep_s).astype(F32)

    y_prompt, y_sample = x_prompt, x_sample
    p_layers, s_layers = [], []
    for layer in range(depth):
        w = (norm_g[layer], _pad_w_in(w_in[layer]), w_out[layer].astype(BF16), gla_w_gate[layer],
             gla_b_gate[layer], gla_norm_g[layer],
             nsa_q_norm_g[layer], nsa_k_norm_g[layer], nsa_cmp_pos[layer], nsa_cmp_w[layer],
             ml_conv_w[layer], ml_conv_b[layer], ml_gate_b[layer], ml_norm_g[layer])
        y_prompt, *p_new = _hybrid_layer(y_prompt, None, keep_p, zero_gla, zero_c, zero_n,
                                         zero_m, zero_conv, *w)
        paged = (cache_t, page_table, layer * n_pool, win_t, layer * bs)
        y_sample, *s_new = _hybrid_layer(y_sample, paged, keep_s,
                                         state_gla[layer], state_mlstm_C[layer], state_mlstm_n[layer],
                                         state_mlstm_m[layer], state_mlstm_conv[layer], *w)
        p_layers.append(p_new)
        s_layers.append(s_new)
    p_kv, p_win, p_gla, p_c, p_n, p_m, p_conv = [jnp.stack(z) for z in zip(*p_layers)]
    s_kv, s_win, s_gla, s_c, s_n, s_m, s_conv = [jnp.stack(z) for z in zip(*s_layers)]
    return (y_prompt, y_sample, p_kv, s_kv, p_win, s_win, p_gla, s_gla, p_c, s_c, p_n, s_n, p_m, s_m, p_conv, s_conv)
```
